```python
import jax
import jax.numpy as jnp
from jax import lax
import numpy as np

D_MODEL = 1024
BATCH = 8
SEQ = 2048
DEPTH = 2
DEC_BATCH = 128
DEC_SEQ = 8
PAST_LEN = 16384
PAGE_SIZE = 128

CONV_K = 3
CONV_CH = D_MODEL // 2
GLA_HEADS = 4
GLA_DK = D_MODEL // 8
GLA_DV = D_MODEL // 4
GLA_QK = GLA_HEADS * GLA_DK
GLA_V = GLA_HEADS * GLA_DV
GLA_LOWRANK = 16
GLA_TAU = 16.0
GLA_CHUNK = 64
CM_GROUPS = 4
CM_CHUNK = 128
CM_GCH = D_MODEL // 8
CM_CH = CM_GROUPS * CM_GCH
SPLIT_SIZES = (CONV_CH, CONV_CH, CONV_CH,
               GLA_QK, GLA_QK, GLA_V, GLA_V, GLA_LOWRANK,
               CM_CH, CM_CH,
               D_MODEL, D_MODEL, D_MODEL)
D_IN = sum(SPLIT_SIZES)
MOE_GROUPS = 8
EXPERTS_PER_GROUP = 8
N_EXPERTS = MOE_GROUPS * EXPERTS_PER_GROUP
TOP_K = 2
D_EXPERT = D_MODEL // 4
MOE_BLOCK = 128
EPS = 1e-6

kernel_name = 'hybrid_conv_gla_chunkmlp_hiermoe_step'


def rmsnorm(x, g):
    xf = x.astype(jnp.float32)
    y = xf * lax.rsqrt(jnp.mean(xf * xf, axis=-1, keepdims=True) + EPS)
    return (y * g.astype(jnp.float32)).astype(x.dtype)


def split_cols(z):
    outs = []
    off = 0
    for n in SPLIT_SIZES:
        outs.append(z[..., off:off + n])
        off += n
    return outs


def gla_chunked(q, k, v, log_a, s0):
    bn, t, h, _ = q.shape
    dv = v.shape[-1]
    c = min(GLA_CHUNK, t)
    n = -(-t // c)
    tp = n * c

    def blocks(a):
        a = jnp.pad(a.astype(jnp.float32), ((0, 0), (0, tp - t), (0, 0), (0, 0)))
        return a.reshape(bn, n, c, h, a.shape[-1]).transpose(1, 0, 3, 2, 4)

    qc, kc, vc, ac = blocks(q), blocks(k), blocks(v), blocks(log_a)
    b = jnp.cumsum(ac, axis=3)
    b_last = b[:, :, :, -1:, :]
    q_t = qc * jnp.exp(b)
    k_t = kc * jnp.exp(-b)
    k_end = kc * jnp.exp(b_last - b)
    mask = jnp.tril(jnp.ones((c, c), dtype=bool))
    att = jnp.where(mask, jnp.einsum('nbhcd,nbhsd->nbhcs', q_t, k_t), 0.0)
    o_intra = jnp.einsum('nbhcs,nbhse->nbhce', att, vc)

    def step(s, inp):
        q_i, k_i, v_i, d_i = inp
        o_i = jnp.einsum('bhcd,bhde->bhce', q_i, s)
        s = d_i[:, :, 0, :, None] * s + jnp.einsum('bhcd,bhce->bhde', k_i, v_i)
        return s, o_i

    s_fin, o_inter = lax.scan(step, s0.astype(jnp.float32), (q_t, k_end, vc, jnp.exp(b_last)))
    o = (o_intra + o_inter).transpose(1, 0, 3, 2, 4).reshape(bn, tp, h, dv)[:, :t]
    return o.astype(q.dtype), s_fin.astype(s0.dtype)


def chunk_spatial_mix(vg, ws, bias):
    bn, t, g, ch = vg.shape
    n = -(-t // CM_CHUNK)
    tp = n * CM_CHUNK
    vp = jnp.pad(vg, ((0, 0), (0, tp - t), (0, 0), (0, 0))).reshape(bn, n, CM_CHUNK, g, ch)
    s = jnp.einsum('gts,bnsgc->bntgc', jnp.tril(ws), vp) + jnp.swapaxes(bias, 0, 1)[None, None, :, :, None]
    return s.reshape(bn, tp, g, ch)[:, :t]


def hier_moe(x2d, rg_w, rg_b, re_w, re_b, w1, w3, w2):
    t = x2d.shape[0]
    lg = (x2d @ rg_w + rg_b).astype(jnp.float32)
    _, grp = lax.top_k(lg, 1)
    p_grp = jnp.take_along_axis(jax.nn.softmax(lg, axis=-1), grp, axis=-1)
    le = (x2d @ re_w + re_b).astype(jnp.float32).reshape(t, MOE_GROUPS, EXPERTS_PER_GROUP)
    le_g = jnp.take_along_axis(le, grp[:, :, None], axis=1)[:, 0]
    top_v, top_i = lax.top_k(le_g, TOP_K)
    gate = p_grp * jax.nn.softmax(top_v, axis=-1)
    eidx = grp * EXPERTS_PER_GROUP + top_i

    n_slots = t * TOP_K
    e_flat = eidx.reshape(n_slots).astype(jnp.int32)
    tok_flat = jnp.repeat(jnp.arange(t, dtype=jnp.int32), TOP_K)
    g_flat = gate.reshape(n_slots)
    order = jnp.argsort(e_flat)
    e_s, tok_s, g_s = e_flat[order], tok_flat[order], g_flat[order]
    counts = jnp.zeros((N_EXPERTS,), jnp.int32).at[e_flat].add(1)
    start = jnp.cumsum(counts) - counts
    blocks_per = (counts + MOE_BLOCK - 1) // MOE_BLOCK
    blk_end = jnp.cumsum(blocks_per)
    blk_start = blk_end - blocks_per
    dest = blk_start[e_s] * MOE_BLOCK + (jnp.arange(n_slots, dtype=jnp.int32) - start[e_s])
    nb = -(-n_slots // MOE_BLOCK) + N_EXPERTS
    tok_buf = jnp.full((nb * MOE_BLOCK,), t, jnp.int32).at[dest].set(tok_s)
    g_buf = jnp.zeros((nb * MOE_BLOCK,), jnp.float32).at[dest].set(g_s)
    block_expert = jnp.minimum(jnp.searchsorted(blk_end, jnp.arange(nb), side='right'), N_EXPERTS - 1)
    x_pad = jnp.concatenate([x2d, jnp.zeros((1, x2d.shape[1]), x2d.dtype)], axis=0)
    xb = x_pad[tok_buf].reshape(nb, MOE_BLOCK, x2d.shape[1])

    def expert_block(args):
        xblk, e = args
        hdn = jax.nn.silu(xblk @ w1[e]) * (xblk @ w3[e])
        return hdn @ w2[e]

    yb = lax.map(expert_block, (xb, block_expert)).reshape(nb * MOE_BLOCK, x2d.shape[1])
    y = jax.ops.segment_sum(yb * g_buf[:, None].astype(yb.dtype), tok_buf, num_segments=t + 1)
    return y[:t]


def trunk_layer(x, conv_buf, gla_state, p):
    (n1, w_in, conv_w, gla_a2, gla_a_b, gla_ng, cm_ng, cm_ws, cm_b, proj_a, proj_b, proj_c,
     w_out, n2, rg_w, rg_b, re_w, re_b, w1, w3, w2) = p
    bn, t, _ = x.shape
    xn = rmsnorm(x, n1)
    h, cg, bg, q, k, v, r, a_lr, u, vv, ga, gb, gc = split_cols(xn @ w_in)

    cin = cg * h
    cp = jnp.concatenate([conv_buf.astype(cin.dtype), cin], axis=1)
    conv = sum(cp[:, i:i + t] * conv_w[i] for i in range(CONV_K))
    y_a = (bg * conv) @ proj_a
    new_conv = cp[:, t:]

    qh = q.reshape(bn, t, GLA_HEADS, GLA_DK) * (GLA_DK ** -0.5)
    kh = k.reshape(bn, t, GLA_HEADS, GLA_DK)
    vh = v.reshape(bn, t, GLA_HEADS, GLA_DV)
    log_a = jax.nn.log_sigmoid((a_lr @ gla_a2 + gla_a_b).astype(jnp.float32)) / GLA_TAU
    o, new_gla = gla_chunked(qh, kh, vh, log_a.reshape(bn, t, GLA_HEADS, GLA_DK), gla_state)
    o = rmsnorm(o, gla_ng) * jax.nn.silu(r.reshape(bn, t, GLA_HEADS, GLA_DV))
    y_b = o.reshape(bn, t, GLA_V) @ proj_b

    ug = jax.nn.gelu(u)
    vg = rmsnorm(jax.nn.gelu(vv).reshape(bn, t, CM_GROUPS, CM_GCH), cm_ng)
    s = chunk_spatial_mix(vg, cm_ws, cm_b)
    y_c = (ug * s.reshape(bn, t, CM_CH)) @ proj_c

    mix = jax.nn.sigmoid(ga) * y_a + jax.nn.sigmoid(gb) * y_b + jax.nn.sigmoid(gc) * y_c
    x = x + mix @ w_out

    xn2 = rmsnorm(x, n2)
    x = x + hier_moe(xn2.reshape(bn * t, D_MODEL), rg_w, rg_b, re_w, re_b, w1, w3, w2).reshape(bn, t, D_MODEL)
    return x, new_conv, new_gla, vg.reshape(bn, t, CM_CH)


def setup_inputs(seed: int = 0) -> dict:
    key = jax.random.key(seed)
    ks = jax.random.split(key, 32)

    def nrm(k, shape, s):
        return jax.random.normal(k, shape, jnp.float32) * s

    return {
        'x_prompt': nrm(ks[0], (BATCH, SEQ, D_MODEL), 1.0),
        'x_sample': nrm(ks[1], (DEC_BATCH, DEC_SEQ, D_MODEL), 1.0),
        'state_conv': nrm(ks[2], (DEPTH, DEC_BATCH, CONV_K - 1, CONV_CH), 1.0),
        'state_gla': nrm(ks[3], (DEPTH, DEC_BATCH, GLA_HEADS, GLA_DK, GLA_DV), 1.0),
        'norm1_g': 1.0 + nrm(ks[4], (DEPTH, D_MODEL), 0.02),
        'w_in': nrm(ks[5], (DEPTH, D_MODEL, D_IN), D_MODEL ** -0.5),
        'conv_w': nrm(ks[6], (DEPTH, CONV_K, CONV_CH), CONV_K ** -0.5),
        'gla_a2': nrm(ks[7], (DEPTH, GLA_LOWRANK, GLA_QK), GLA_LOWRANK ** -0.5),
        'gla_a_b': nrm(ks[8], (DEPTH, GLA_QK), 0.1),
        'gla_norm_g': 1.0 + nrm(ks[9], (DEPTH, GLA_DV), 0.02),
        'cm_norm_g': 1.0 + nrm(ks[10], (DEPTH, CM_GROUPS, CM_GCH), 0.02),
        'cm_ws': nrm(ks[11], (DEPTH, CM_GROUPS, CM_CHUNK, CM_CHUNK), CM_CHUNK ** -0.5),
        'cm_b': nrm(ks[12], (DEPTH, CM_GROUPS, CM_CHUNK), 0.1),
        'proj_a': nrm(ks[13], (DEPTH, CONV_CH, D_MODEL), CONV_CH ** -0.5),
        'proj_b': nrm(ks[14], (DEPTH, GLA_V, D_MODEL), GLA_V ** -0.5),
        'proj_c': nrm(ks[15], (DEPTH, CM_CH, D_MODEL), CM_CH ** -0.5),
        'w_out': nrm(ks[16], (DEPTH, D_MODEL, D_MODEL), D_MODEL ** -0.5),
        'norm2_g': 1.0 + nrm(ks[17], (DEPTH, D_MODEL), 0.02),
        'router_group_w': nrm(ks[18], (DEPTH, D_MODEL, MOE_GROUPS), D_MODEL ** -0.5),
        'router_group_b': nrm(ks[19], (DEPTH, MOE_GROUPS), 0.01),
        'router_expert_w': nrm(ks[20], (DEPTH, D_MODEL, N_EXPERTS), D_MODEL ** -0.5),
        'router_expert_b': nrm(ks[21], (DEPTH, N_EXPERTS), 0.01),
        'exp_w1': nrm(ks[22], (DEPTH, N_EXPERTS, D_MODEL, D_EXPERT), D_MODEL ** -0.5),
        'exp_w3': nrm(ks[23], (DEPTH, N_EXPERTS, D_MODEL, D_EXPERT), D_MODEL ** -0.5),
        'exp_w2': nrm(ks[24], (DEPTH, N_EXPERTS, D_EXPERT, D_MODEL), D_EXPERT ** -0.5),
        'final_norm_g': 1.0 + nrm(ks[25], (D_MODEL,), 0.02),
    }


def reference(x_prompt, x_sample, state_conv, state_gla, norm1_g, w_in, conv_w, gla_a2, gla_a_b,
              gla_norm_g, cm_norm_g, cm_ws, cm_b, proj_a, proj_b, proj_c, w_out, norm2_g,
              router_group_w, router_group_b, router_expert_w, router_expert_b,
              exp_w1, exp_w3, exp_w2, final_norm_g):
    params = (norm1_g, w_in, conv_w, gla_a2, gla_a_b, gla_norm_g, cm_norm_g, cm_ws, cm_b,
              proj_a, proj_b, proj_c, w_out, norm2_g, router_group_w, router_group_b,
              router_expert_w, router_expert_b, exp_w1, exp_w3, exp_w2)
    xp = x_prompt
    xs = x_sample
    bp = x_prompt.shape[0]
    conv_p, gla_p, conv_s, gla_s, cmv_s = [], [], [], [], []
    for l in range(DEPTH):
        p = tuple(a[l] for a in params)
        xp, cb, gs, _ = trunk_layer(
            xp,
            jnp.zeros((bp, CONV_K - 1, CONV_CH), xp.dtype),
            jnp.zeros((bp, GLA_HEADS, GLA_DK, GLA_DV), xp.dtype),
            p)
        conv_p.append(cb)
        gla_p.append(gs)
        xs, cb, gs, vrows = trunk_layer(xs, state_conv[l], state_gla[l], p)
        conv_s.append(cb)
        gla_s.append(gs)
        cmv_s.append(vrows)
    y_prompt = rmsnorm(xp, final_norm_g)
    y_sample = rmsnorm(xs, final_norm_g)
    return (y_prompt, y_sample, jnp.stack(conv_p), jnp.stack(gla_p), jnp.stack(conv_s), jnp.stack(gla_s), jnp.stack(cmv_s))
```

```python
import functools

import jax
import jax.numpy as jnp
from jax import lax
from jax.experimental import pallas as pl
from jax.experimental.pallas import tpu as pltpu

F32 = jnp.float32
BF16 = jnp.bfloat16

D_MODEL = 1024
BATCH = 8
SEQ = 2048
DEPTH = 2
DEC_BATCH = 128
DEC_SEQ = 8
CONV_K = 3
CONV_CH = 512
GLA_HEADS = 4
GLA_DK = 128
GLA_DV = 256
GLA_QK = GLA_HEADS * GLA_DK
GLA_V = GLA_HEADS * GLA_DV
GLA_LOWRANK = 16
GLA_TAU = 16.0
GLA_CHUNK = 64
CM_GROUPS = 4
CM_CHUNK = 128
CM_GCH = 128
CM_CH = 512
MOE_GROUPS = 8
EXPERTS_PER_GROUP = 8
N_EXPERTS = 64
D_EXPERT = 256
MOE_BLOCK = 128
EPS = 1e-6

LANES = 128
T_PROMPT = BATCH * SEQ
T_SAMPLE = DEC_BATCH * DEC_SEQ
T_ALL = T_PROMPT + T_SAMPLE
N_SLOTS = 2 * T_ALL
N_BLOCKS = -(-N_SLOTS // MOE_BLOCK) + N_EXPERTS

COL_GA, COL_GB, COL_GC = 0, 1024, 2048
COL_V, COL_R, COL_Q, COL_K = 3072, 4096, 5120, 5632
COL_H, COL_CG, COL_BG, COL_U, COL_VV = 6144, 6656, 7168, 7680, 8192
Z_COLS = 8704
ACT_COLS = 2048

TM_IN = 1024
TN_IN = 512
TC_MIX = 256
SEQ_PER_BLK = 16
ROWS_S = SEQ_PER_BLK * DEC_SEQ
TM_OUT = 256
TM_RANK = 512
TM_COMB = 256
VMEM_LIMIT = 56 * 1024 * 1024


def _sigmoid(x):
    return 1.0 / (1.0 + jnp.exp(-x))


def _gelu_tanh(x):
    return 0.5 * x * (1.0 + jnp.tanh(0.7978845608028654 * (x + 0.044715 * (x * x * x))))


def _log_sigmoid(x):
    return jnp.minimum(x, 0.0) - jnp.log(1.0 + jnp.exp(-jnp.abs(x)))


def _rms(x, g):
    ms = jnp.mean(x * x, axis=-1, keepdims=True)
    return x * lax.rsqrt(ms + EPS) * g


def _split_bf16(x):
    hi = x.astype(BF16)
    lo = (x - hi.astype(F32)).astype(BF16)
    return hi, lo


def _dot(a, b):
    return jnp.dot(a, b, preferred_element_type=F32)


def _dot_nt(a, b):
    return lax.dot_general(a, b, (((1,), (1,)), ((), ())), preferred_element_type=F32)


def _inproj_kernel(x_ref, g_ref, w_ref, wa_ref, z_ref, a_ref, xn_ref):
    @pl.when(pl.program_id(1) == 0)
    def _():
        xn = _rms(x_ref[...], g_ref[...]).astype(BF16)
        xn_ref[...] = xn
        a_ref[...] = _dot(xn, wa_ref[...]).astype(BF16)

    z_ref[...] = _dot(xn_ref[...], w_ref[...]).astype(BF16)


def _inproj(x, g, w, wa):
    m = x.shape[0]
    return pl.pallas_call(
        _inproj_kernel,
        grid=(m // TM_IN, Z_COLS // TN_IN),
        in_specs=[
            pl.BlockSpec((TM_IN, D_MODEL), lambda i, j: (i, 0)),
            pl.BlockSpec((1, D_MODEL), lambda i, j: (0, 0)),
            pl.BlockSpec((D_MODEL, TN_IN), lambda i, j: (0, j)),
            pl.BlockSpec((D_MODEL, LANES), lambda i, j: (0, 0)),
        ],
        out_specs=[
            pl.BlockSpec((TM_IN, TN_IN), lambda i, j: (i, j)),
            pl.BlockSpec((TM_IN, LANES), lambda i, j: (i, 0)),
        ],
        out_shape=[
            jax.ShapeDtypeStruct((m, Z_COLS), BF16),
            jax.ShapeDtypeStruct((m, LANES), BF16),
        ],
        scratch_shapes=[pltpu.VMEM((TM_IN, D_MODEL), BF16)],
        compiler_params=pltpu.CompilerParams(
            dimension_semantics=("arbitrary", "arbitrary"), vmem_limit_bytes=VMEM_LIMIT),
        name="inproj",
    )(x, g, w, wa)


def _gla_decay_terms(alr_ref, a2_ref, ab_ref, tril_mask, same_mask):
    la = _log_sigmoid(_dot(alr_ref[...], a2_ref[...]) + ab_ref[...]) * (1.0 / GLA_TAU)
    la_hi, la_lo = _split_bf16(la)
    l_tri = jnp.where(tril_mask, 1.0, 0.0).astype(BF16)
    l_same = jnp.where(same_mask, 1.0, 0.0).astype(BF16)
    b = _dot(l_tri, la_hi) + _dot(l_tri, la_lo)
    bl = _dot(l_same, la_hi) + _dot(l_same, la_lo)
    return la_hi, la_lo, b, bl


def _gla_out_gate(o, g_ref, r):
    return _rms(o, g_ref[...]) * (r * _sigmoid(r))


def _chunk_mlp(u, vv, cmg_ref, ws_ref, cmb_ref, n_chunks):
    ug = _gelu_tanh(u)
    gv = _gelu_tanh(vv)
    outs, vgs = [], []
    for g in range(CM_GROUPS):
        sl = slice(g * CM_GCH, (g + 1) * CM_GCH)
        vg = _rms(gv[:, sl], cmg_ref[:, sl])
        vgs.append(vg)
        vgb = vg.astype(BF16)
        rows = []
        for j in range(n_chunks):
            rs = slice(j * CM_CHUNK, (j + 1) * CM_CHUNK)
            rows.append(_dot(ws_ref[g], vgb[rs]) + cmb_ref[:, sl])
        s = rows[0] if n_chunks == 1 else jnp.concatenate(rows, axis=0)
        outs.append(ug[:, sl] * s)
    return jnp.concatenate(outs, axis=1), jnp.concatenate(vgs, axis=1)


def _mix_prompt_kernel(v_ref, r_ref, qk_ref, hcg_ref, bgu_ref, vv_ref, alr_ref,
                       a2_ref, ab_ref, cw_ref, gng_ref, cmg_ref, ws_ref, cmb_ref,
                       acts_ref, nconv_ref, ngla_ref, s_ref, carry_ref):
    tc = TC_MIX

    @pl.when(pl.program_id(1) == 0)
    def _():
        s_ref[...] = jnp.zeros_like(s_ref)
        carry_ref[...] = jnp.zeros_like(carry_ref)

    row = lax.broadcasted_iota(jnp.int32, (tc, 1), 0)
    h = hcg_ref[:, :CONV_CH].astype(F32)
    cg = hcg_ref[:, CONV_CH:].astype(F32)
    bg = bgu_ref[:, :CONV_CH].astype(F32)
    cin = cg * h
    c0 = carry_ref[0:1, :]
    c1 = carry_ref[1:2, :]
    x1 = jnp.where(row >= 1, pltpu.roll(cin, 1, 0), c1)
    x2 = jnp.where(row >= 2, pltpu.roll(cin, 2, 0), jnp.where(row == 1, c1, c0))
    conv = x2 * cw_ref[0:1, :] + x1 * cw_ref[1:2, :] + cin * cw_ref[2:3, :]
    acts_ref[:, 0:CONV_CH] = (bg * conv).astype(BF16)
    carry_ref[0:2, :] = cin[tc - 2:tc, :]
    nconv_ref[0] = cin[tc - 2:tc, :]

    rr = lax.broadcasted_iota(jnp.int32, (tc, tc), 0)
    cc = lax.broadcasted_iota(jnp.int32, (tc, tc), 1)
    same = (rr >> 6) == (cc >> 6)
    tril = same & (cc <= rr)
    la_hi, la_lo, b, bl = _gla_decay_terms(alr_ref, a2_ref, ab_ref, tril, same)
    q = qk_ref[:, :GLA_QK].astype(F32) * (GLA_DK ** -0.5)
    k = qk_ref[:, GLA_QK:].astype(F32)
    q_t = (q * jnp.exp(b)).astype(BF16)
    k_t = (k * jnp.exp(-b)).astype(BF16)
    k_end = k * jnp.exp(bl - b)
    la_hi = la_hi.astype(F32)
    la_lo = la_lo.astype(F32)
    lane_chunk = lax.broadcasted_iota(jnp.int32, (GLA_DK, tc), 1) >> 6
    ones = jnp.ones((tc, GLA_DV), BF16)
    n_chunks = tc // GLA_CHUNK
    for hd in range(GLA_HEADS):
        ks = slice(hd * GLA_DK, (hd + 1) * GLA_DK)
        vs = slice(hd * GLA_DV, (hd + 1) * GLA_DV)
        qh = q_t[:, ks]
        vh = v_ref[:, vs]
        att = jnp.where(tril, _dot_nt(qh, k_t[:, ks]), 0.0).astype(BF16)
        o_intra = _dot(att, vh)
        k_tr = k_end[:, ks].T
        lh_tr = la_hi[:, ks].T
        ll_tr = la_lo[:, ks].T
        o_rows = []
        for c in range(n_chunks):
            rs = slice(c * GLA_CHUNK, (c + 1) * GLA_CHUNK)
            s_old = s_ref[hd]
            o_rows.append(o_intra[rs] + _dot(qh[rs], s_old.astype(BF16)))
            in_c = lane_chunk == c
            zero = jnp.zeros_like(k_tr)
            dlog = (_dot(jnp.where(in_c, lh_tr, zero).astype(BF16), ones)
                    + _dot(jnp.where(in_c, ll_tr, zero).astype(BF16), ones))
            upd = _dot(jnp.where(in_c, k_tr, zero).astype(BF16), vh)
            s_ref[hd] = jnp.exp(dlog) * s_old + upd
        o = jnp.concatenate(o_rows, axis=0)
        r = r_ref[:, vs].astype(F32)
        acts_ref[:, CONV_CH + hd * GLA_DV:CONV_CH + (hd + 1) * GLA_DV] = (
            _gla_out_gate(o, gng_ref, r).astype(BF16))
    ngla_ref[0] = s_ref[...]

    us, _ = _chunk_mlp(bgu_ref[:, CONV_CH:].astype(F32), vv_ref[...].astype(F32),
                       cmg_ref, ws_ref, cmb_ref, tc // CM_CHUNK)
    acts_ref[:, CONV_CH + GLA_V:] = us.astype(BF16)


def _z_specs(rows, row_map):
    def spec(width, col):
        blk = col // width
        return pl.BlockSpec((rows, width), lambda *g: (row_map(*g), blk))
    return [spec(1024, COL_V), spec(1024, COL_R), spec(1024, COL_Q), spec(1024, COL_H),
            spec(1024, COL_BG), spec(512, COL_VV)]


def _const_spec(shape):
    nd = len(shape)
    return pl.BlockSpec(shape, lambda *g: (0,) * nd)


def _mix_prompt(z, alr, a2, ab, cw, gng, cmg, ws, cmb):
    nt = SEQ // TC_MIX
    row_map = lambda b, c: b * nt + c
    in_specs = _z_specs(TC_MIX, row_map) + [
        pl.BlockSpec((TC_MIX, LANES), lambda b, c: (row_map(b, c), 0)),
        _const_spec(a2.shape), _const_spec(ab.shape), _const_spec(cw.shape),
        _const_spec(gng.shape), _const_spec(cmg.shape), _const_spec(ws.shape),
        _const_spec(cmb.shape),
    ]
    return pl.pallas_call(
        _mix_prompt_kernel,
        grid=(BATCH, nt),
        in_specs=in_specs,
        out_specs=[
            pl.BlockSpec((TC_MIX, ACT_COLS), lambda b, c: (row_map(b, c), 0)),
            pl.BlockSpec((1, CONV_K - 1, CONV_CH), lambda b, c: (b, 0, 0)),
            pl.BlockSpec((1, GLA_HEADS, GLA_DK, GLA_DV), lambda b, c: (b, 0, 0, 0)),
        ],
        out_shape=[
            jax.ShapeDtypeStruct((T_PROMPT, ACT_COLS), BF16),
            jax.ShapeDtypeStruct((BATCH, CONV_K - 1, CONV_CH), F32),
            jax.ShapeDtypeStruct((BATCH, GLA_HEADS, GLA_DK, GLA_DV), F32),
        ],
        scratch_shapes=[pltpu.VMEM((GLA_HEADS, GLA_DK, GLA_DV), F32),
                        pltpu.VMEM((8, CONV_CH), F32)],
        compiler_params=pltpu.CompilerParams(
            dimension_semantics=("arbitrary", "arbitrary"), vmem_limit_bytes=VMEM_LIMIT),
        name="mix_prompt",
    )(z, z, z, z, z, z, alr, a2, ab, cw, gng, cmg, ws, cmb)


def _mix_sample_kernel(v_ref, r_ref, qk_ref, hcg_ref, bgu_ref, vv_ref, alr_ref,
                       a2_ref, ab_ref, cw_ref, gng_ref, cmg_ref, ws_ref, cmb_ref,
                       p1_ref, p2_ref, s0_ref,
                       acts_ref, cin_ref, ns_ref, vrow_ref):
    n = ROWS_S

    pos = lax.broadcasted_iota(jnp.int32, (n, 1), 0) & (DEC_SEQ - 1)
    h = hcg_ref[:, :CONV_CH].astype(F32)
    cg = hcg_ref[:, CONV_CH:].astype(F32)
    bg = bgu_ref[:, :CONV_CH].astype(F32)
    cin = cg * h
    x1 = jnp.where(pos >= 1, pltpu.roll(cin, 1, 0), p1_ref[...])
    x2 = jnp.where(pos >= 2, pltpu.roll(cin, 2, 0), p2_ref[...])
    conv = x2 * cw_ref[0:1, :] + x1 * cw_ref[1:2, :] + cin * cw_ref[2:3, :]
    acts_ref[:, 0:CONV_CH] = (bg * conv).astype(BF16)
    cin_ref[...] = cin

    rr = lax.broadcasted_iota(jnp.int32, (n, n), 0)
    cc = lax.broadcasted_iota(jnp.int32, (n, n), 1)
    same = (rr >> 3) == (cc >> 3)
    tril = same & (cc <= rr)
    la_hi, la_lo, b, bl = _gla_decay_terms(alr_ref, a2_ref, ab_ref, tril, same)
    q = qk_ref[:, :GLA_QK].astype(F32) * (GLA_DK ** -0.5)
    k = qk_ref[:, GLA_QK:].astype(F32)
    q_t = (q * jnp.exp(b)).astype(BF16)
    k_t = (k * jnp.exp(-b)).astype(BF16)
    k_end = k * jnp.exp(bl - b)
    la_hi = la_hi.astype(F32)
    la_lo = la_lo.astype(F32)
    row_seq = lax.broadcasted_iota(jnp.int32, (n, GLA_DK), 0) >> 3
    seq3 = lax.broadcasted_iota(jnp.int32, (SEQ_PER_BLK, GLA_DK, n), 0)
    lane_seq3 = lax.broadcasted_iota(jnp.int32, (SEQ_PER_BLK, GLA_DK, n), 2) >> 3
    mask3 = seq3 == lane_seq3
    ones = jnp.ones((n, GLA_DV), BF16)
    big = SEQ_PER_BLK * GLA_DK

    def per_seq(x_tr):
        x3 = jnp.where(mask3, x_tr[None, :, :], 0.0)
        return x3.reshape(big, n).astype(BF16)

    for hd in range(GLA_HEADS):
        ks = slice(hd * GLA_DK, (hd + 1) * GLA_DK)
        vs = slice(hd * GLA_DV, (hd + 1) * GLA_DV)
        qh = q_t[:, ks]
        vh = v_ref[:, vs]
        att = jnp.where(tril, _dot_nt(qh, k_t[:, ks]), 0.0).astype(BF16)
        o_intra = _dot(att, vh)
        s_old = s0_ref[:, hd].reshape(big, GLA_DV)
        zero = jnp.zeros_like(qh)
        q_big = jnp.concatenate(
            [jnp.where(row_seq == j, qh, zero) for j in range(SEQ_PER_BLK)], axis=1)
        o = o_intra + _dot(q_big, s_old.astype(BF16))
        dlog = _dot(per_seq(la_hi[:, ks].T), ones) + _dot(per_seq(la_lo[:, ks].T), ones)
        upd = _dot(per_seq(k_end[:, ks].T), vh)
        s_new = jnp.exp(dlog) * s_old + upd
        ns_ref[:, hd] = s_new.reshape(SEQ_PER_BLK, GLA_DK, GLA_DV)
        r = r_ref[:, vs].astype(F32)
        acts_ref[:, CONV_CH + hd * GLA_DV:CONV_CH + (hd + 1) * GLA_DV] = (
            _gla_out_gate(o, gng_ref, r).astype(BF16))

    us, vg = _chunk_mlp(bgu_ref[:, CONV_CH:].astype(F32), vv_ref[...].astype(F32),
                        cmg_ref, ws_ref, cmb_ref, 1)
    acts_ref[:, CONV_CH + GLA_V:] = us.astype(BF16)
    vrow_ref[...] = vg


def _mix_sample(z, alr, a2, ab, cw, gng, cmg, ws, cmb, p1, p2, s0):
    row0 = T_PROMPT // ROWS_S
    row_map = lambda i: row0 + i
    in_specs = _z_specs(ROWS_S, row_map) + [
        pl.BlockSpec((ROWS_S, LANES), lambda i: (row_map(i), 0)),
        _const_spec(a2.shape), _const_spec(ab.shape), _const_spec(cw.shape),
        _const_spec(gng.shape), _const_spec(cmg.shape), _const_spec(ws.shape),
        _const_spec(cmb.shape),
        pl.BlockSpec((ROWS_S, CONV_CH), lambda i: (i, 0)),
        pl.BlockSpec((ROWS_S, CONV_CH), lambda i: (i, 0)),
        pl.BlockSpec((SEQ_PER_BLK, GLA_HEADS, GLA_DK, GLA_DV), lambda i: (i, 0, 0, 0)),
    ]
    return pl.pallas_call(
        _mix_sample_kernel,
        grid=(DEC_BATCH // SEQ_PER_BLK,),
        in_specs=in_specs,
        out_specs=[
            pl.BlockSpec((ROWS_S, ACT_COLS), lambda i: (i, 0)),
            pl.BlockSpec((ROWS_S, CONV_CH), lambda i: (i, 0)),
            pl.BlockSpec((SEQ_PER_BLK, GLA_HEADS, GLA_DK, GLA_DV), lambda i: (i, 0, 0, 0)),
            pl.BlockSpec((ROWS_S, CM_CH), lambda i: (i, 0)),
        ],
        out_shape=[
            jax.ShapeDtypeStruct((T_SAMPLE, ACT_COLS), BF16),
            jax.ShapeDtypeStruct((T_SAMPLE, CONV_CH), F32),
            jax.ShapeDtypeStruct((DEC_BATCH, GLA_HEADS, GLA_DK, GLA_DV), F32),
            jax.ShapeDtypeStruct((T_SAMPLE, CM_CH), F32),
        ],
        compiler_params=pltpu.CompilerParams(
            dimension_semantics=("arbitrary",), vmem_limit_bytes=VMEM_LIMIT),
        name="mix_sample",
    )(z, z, z, z, z, z, alr, a2, ab, cw, gng, cmg, ws, cmb, p1, p2, s0)


def _outproj_kernel(actp_ref, acts_ref, gates_ref, x_ref, pa_ref, pb_ref, pc_ref, wo_ref, n2_ref,
                    wrh_ref, wrl_ref, rb_ref, xo_ref, xn_ref, route_ref):
    acts = jnp.where(pl.program_id(0) < T_PROMPT // TM_OUT, actp_ref[...], acts_ref[...])
    ya = _dot(acts[:, :CONV_CH], pa_ref[...])
    yb = _dot(acts[:, CONV_CH:CONV_CH + GLA_V], pb_ref[...])
    yc = _dot(acts[:, CONV_CH + GLA_V:], pc_ref[...])
    ga = _sigmoid(gates_ref[:, COL_GA:COL_GA + D_MODEL].astype(F32))
    gb = _sigmoid(gates_ref[:, COL_GB:COL_GB + D_MODEL].astype(F32))
    gc = _sigmoid(gates_ref[:, COL_GC:COL_GC + D_MODEL].astype(F32))
    mix = ga * ya + gb * yb + gc * yc
    x = x_ref[...] + _dot(mix.astype(BF16), wo_ref[...])
    xo_ref[...] = x
    xn = _rms(x, n2_ref[...])
    xn_ref[...] = xn

    hi, lo = _split_bf16(xn)
    logits = (_dot(hi, wrh_ref[...]) + _dot(lo, wrh_ref[...]) + _dot(hi, wrl_ref[...])
              + rb_ref[...])
    lane_i = lax.broadcasted_iota(jnp.int32, logits.shape, 1)
    lane = lane_i.astype(F32)
    neg = jnp.float32(-jnp.inf)
    is_g = lane_i < MOE_GROUPS
    lg = jnp.where(is_g, logits, neg)
    gmax = jnp.max(lg, axis=-1, keepdims=True)
    grp = jnp.min(jnp.where(lg == gmax, lane, 1e9), axis=-1, keepdims=True)
    p_grp = 1.0 / jnp.sum(jnp.where(is_g, jnp.exp(lg - gmax), 0.0), axis=-1, keepdims=True)
    in_grp = ((lane_i >= MOE_GROUPS) & (lane_i < MOE_GROUPS + N_EXPERTS)
              & (((lane_i - MOE_GROUPS) >> 3).astype(F32) == grp))
    le = jnp.where(in_grp, logits, neg)
    v1 = jnp.max(le, axis=-1, keepdims=True)
    i1 = jnp.min(jnp.where(le == v1, lane, 1e9), axis=-1, keepdims=True)
    le2 = jnp.where(lane == i1, neg, le)
    v2 = jnp.max(le2, axis=-1, keepdims=True)
    i2 = jnp.min(jnp.where(le2 == v2, lane, 1e9), axis=-1, keepdims=True)
    t = jnp.exp(v2 - v1)
    g1 = p_grp / (1.0 + t)
    g2 = p_grp * t / (1.0 + t)
    route_ref[...] = jnp.where(
        lane_i == 0, i1 - MOE_GROUPS,
        jnp.where(lane_i == 1, i2 - MOE_GROUPS,
                  jnp.where(lane_i == 2, g1, jnp.where(lane_i == 3, g2, 0.0))))


def _outproj(acts_p, acts_s, z, x, pa, pb, pc, wo, n2, wrh, wrl, rb):
    m = x.shape[0]
    row = lambda i: (i, 0)
    n_p = T_PROMPT // TM_OUT
    return pl.pallas_call(
        _outproj_kernel,
        grid=(m // TM_OUT,),
        in_specs=[
            pl.BlockSpec((TM_OUT, ACT_COLS), lambda i: (jnp.minimum(i, n_p - 1), 0)),
            pl.BlockSpec((TM_OUT, ACT_COLS), lambda i: (jnp.maximum(i - n_p, 0), 0)),
            pl.BlockSpec((TM_OUT, 3 * D_MODEL), row),
            pl.BlockSpec((TM_OUT, D_MODEL), row),
            _const_spec(pa.shape), _const_spec(pb.shape), _const_spec(pc.shape),
            _const_spec(wo.shape), _const_spec(n2.shape), _const_spec(wrh.shape),
            _const_spec(wrl.shape), _const_spec(rb.shape),
        ],
        out_specs=[
            pl.BlockSpec((TM_OUT, D_MODEL), row),
            pl.BlockSpec((TM_OUT, D_MODEL), row),
            pl.BlockSpec((TM_OUT, LANES), row),
        ],
        out_shape=[
            jax.ShapeDtypeStruct((m, D_MODEL), F32),
            jax.ShapeDtypeStruct((m, D_MODEL), F32),
            jax.ShapeDtypeStruct((m, LANES), F32),
        ],
        compiler_params=pltpu.CompilerParams(
            dimension_semantics=("arbitrary",), vmem_limit_bytes=VMEM_LIMIT),
        name="outproj",
    )(acts_p, acts_s, z, x, pa, pb, pc, wo, n2, wrh, wrl, rb)


def _rank_kernel(route_ref, rank_ref, cnt_ref):
    @pl.when(pl.program_id(0) == 0)
    def _():
        cnt_ref[...] = jnp.zeros_like(cnt_ref)

    tm = TM_RANK
    lane = lax.broadcasted_iota(jnp.int32, (tm, LANES), 1).astype(F32)
    o1 = jnp.where(lane == route_ref[:, 0:1], 1.0, 0.0)
    o2 = jnp.where(lane == route_ref[:, 1:2], 1.0, 0.0)
    rr = lax.broadcasted_iota(jnp.int32, (tm, tm), 0)
    cc = lax.broadcasted_iota(jnp.int32, (tm, tm), 1)
    before = jnp.where(cc < rr, 1.0, 0.0).astype(BF16)
    p1 = _dot(before, o1.astype(BF16))
    p2 = _dot(before, o2.astype(BF16))
    c1 = jnp.sum(o1, axis=0, keepdims=True)
    c2 = jnp.sum(o2, axis=0, keepdims=True)
    carry = cnt_ref[...]
    r1 = jnp.sum(o1 * (p1 + carry), axis=-1, keepdims=True)
    r2 = jnp.sum(o2 * (p2 + carry + c1), axis=-1, keepdims=True)
    lane_i = lax.broadcasted_iota(jnp.int32, (tm, LANES), 1)
    rank_ref[...] = jnp.where(lane_i == 0, r1, jnp.where(lane_i == 1, r2, 0.0)).astype(jnp.int32)
    cnt_ref[...] = carry + c1 + c2


def _rank(route):
    m = route.shape[0]
    return pl.pallas_call(
        _rank_kernel,
        grid=(m // TM_RANK,),
        in_specs=[pl.BlockSpec((TM_RANK, LANES), lambda i: (i, 0))],
        out_specs=[pl.BlockSpec((TM_RANK, LANES), lambda i: (i, 0)),
                   pl.BlockSpec((1, LANES), lambda i: (0, 0))],
        out_shape=[jax.ShapeDtypeStruct((m, LANES), jnp.int32),
                   jax.ShapeDtypeStruct((1, LANES), F32)],
        compiler_params=pltpu.CompilerParams(dimension_semantics=("arbitrary",)),
        name="rank",
    )(route)


def _ffn_kernel(be_ref, nu_ref, idx_ref, idx_next_ref, x_hbm, w1_ref, w3_ref, w2_ref,
                y_ref, xbuf, sem):
    del be_ref
    b = pl.program_id(0)
    n_used = nu_ref[0]
    slot = b % 2

    def row_copy(ids, r, s):
        return pltpu.make_async_copy(
            x_hbm.at[pl.ds(ids[0, 0, r], 1)], xbuf.at[s, pl.ds(r, 1)], sem.at[s])

    def gather(ids, s):
        def body(r, carry):
            row_copy(ids, r, s).start()
            return carry
        lax.fori_loop(0, MOE_BLOCK, body, 0)

    @pl.when((b == 0) & (n_used > 0))
    def _():
        gather(idx_ref, 0)

    @pl.when(b + 1 < n_used)
    def _():
        gather(idx_next_ref, 1 - slot)

    @pl.when(b < n_used)
    def _():
        def body(r, carry):
            row_copy(idx_ref, r, slot).wait()
            return carry
        lax.fori_loop(0, MOE_BLOCK, body, 0)
        x = xbuf[slot].astype(BF16)
        h1 = _dot(x, w1_ref[0].astype(BF16))
        h3 = _dot(x, w3_ref[0].astype(BF16))
        hdn = (h1 * _sigmoid(h1) * h3).astype(BF16)
        y_ref[...] = _dot(hdn, w2_ref[0].astype(BF16))

    @pl.when(b >= n_used)
    def _():
        y_ref[...] = jnp.zeros_like(y_ref)


def _ffn(block_expert, n_used, tok_buf, xn, w1, w3, w2):
    ids = tok_buf.reshape(N_BLOCKS, 1, MOE_BLOCK)
    grid_spec = pltpu.PrefetchScalarGridSpec(
        num_scalar_prefetch=2,
        grid=(N_BLOCKS,),
        in_specs=[
            pl.BlockSpec((1, 1, MOE_BLOCK), lambda b, be, nu: (b, 0, 0),
                         memory_space=pltpu.SMEM),
            pl.BlockSpec((1, 1, MOE_BLOCK), lambda b, be, nu: (jnp.minimum(b + 1, N_BLOCKS - 1), 0, 0),
                         memory_space=pltpu.SMEM),
            pl.BlockSpec(memory_space=pl.ANY),
            pl.BlockSpec((1, D_MODEL, D_EXPERT), lambda b, be, nu: (be[b], 0, 0)),
            pl.BlockSpec((1, D_MODEL, D_EXPERT), lambda b, be, nu: (be[b], 0, 0)),
            pl.BlockSpec((1, D_EXPERT, D_MODEL), lambda b, be, nu: (be[b], 0, 0)),
        ],
        out_specs=pl.BlockSpec((MOE_BLOCK, D_MODEL), lambda b, be, nu: (b, 0)),
        scratch_shapes=[pltpu.VMEM((2, MOE_BLOCK, D_MODEL), F32),
                        pltpu.SemaphoreType.DMA((2,))],
    )
    return pl.pallas_call(
        _ffn_kernel,
        grid_spec=grid_spec,
        out_shape=jax.ShapeDtypeStruct((N_BLOCKS * MOE_BLOCK, D_MODEL), F32),
        compiler_params=pltpu.CompilerParams(
            dimension_semantics=("arbitrary",), vmem_limit_bytes=VMEM_LIMIT),
        name="ffn",
    )(block_expert, n_used, ids, ids, xn, w1, w3, w2)


def _combine_kernel(d1_ref, d2_ref, d1n_ref, d2n_ref, yb_hbm, x_ref, route_ref, g_ref,
                    o_ref, buf, sem, *, final):
    i = pl.program_id(0)
    n = pl.num_programs(0)
    slot = i % 2

    def row_copy(ids, k, r, s):
        return pltpu.make_async_copy(
            yb_hbm.at[pl.ds(ids[0, 0, r], 1)], buf.at[s, k, pl.ds(r, 1)], sem.at[s])

    def gather(ids1, ids2, s):
        def body(r, carry):
            row_copy(ids1, 0, r, s).start()
            row_copy(ids2, 1, r, s).start()
            return carry
        lax.fori_loop(0, TM_COMB, body, 0)

    @pl.when(i == 0)
    def _():
        gather(d1_ref, d2_ref, 0)

    @pl.when(i + 1 < n)
    def _():
        gather(d1n_ref, d2n_ref, 1 - slot)

    def body(r, carry):
        row_copy(d1_ref, 0, r, slot).wait()
        row_copy(d2_ref, 1, r, slot).wait()
        return carry
    lax.fori_loop(0, TM_COMB, body, 0)

    g1 = route_ref[:, 2:3]
    g2 = route_ref[:, 3:4]
    y = x_ref[...] + (g1 * buf[slot, 0] + g2 * buf[slot, 1])
    if final:
        y = _rms(y, g_ref[...])
    o_ref[...] = y


def _combine(dest1, dest2, yb, x, route, g, final):
    m = x.shape[0]
    nt = m // TM_COMB
    d1 = dest1.reshape(nt, 1, TM_COMB)
    d2 = dest2.reshape(nt, 1, TM_COMB)
    cur = lambda i: (i, 0, 0)
    nxt = lambda i: (jnp.minimum(i + 1, nt - 1), 0, 0)
    smem = functools.partial(pl.BlockSpec, (1, 1, TM_COMB), memory_space=pltpu.SMEM)
    return pl.pallas_call(
        functools.partial(_combine_kernel, final=final),
        grid=(nt,),
        in_specs=[
            smem(cur), smem(cur), smem(nxt), smem(nxt),
            pl.BlockSpec(memory_space=pl.ANY),
            pl.BlockSpec((TM_COMB, D_MODEL), lambda i: (i, 0)),
            pl.BlockSpec((TM_COMB, LANES), lambda i: (i, 0)),
            _const_spec(g.shape),
        ],
        out_specs=pl.BlockSpec((TM_COMB, D_MODEL), lambda i: (i, 0)),
        out_shape=jax.ShapeDtypeStruct((m, D_MODEL), F32),
        scratch_shapes=[pltpu.VMEM((2, 2, TM_COMB, D_MODEL), F32),
                        pltpu.SemaphoreType.DMA((2,))],
        compiler_params=pltpu.CompilerParams(
            dimension_semantics=("arbitrary",), vmem_limit_bytes=VMEM_LIMIT),
        name="combine",
    )(d1, d2, d1, d2, yb, x, route, g)


def _prep_weights(w_in, gla_a2, cm_ws, cm_b, router_group_w, router_group_b,
                  router_expert_w, router_expert_b):
    off = {}
    o = 0
    for name, n in (("h", 512), ("cg", 512), ("bg", 512), ("q", 512), ("k", 512), ("v", 1024),
                    ("r", 1024), ("alr", 16), ("u", 512), ("vv", 512), ("ga", 1024),
                    ("gb", 1024), ("gc", 1024)):
        off[name] = (o, n)
        o += n
    order = ("ga", "gb", "gc", "v", "r", "q", "k", "h", "cg", "bg", "u", "vv")
    w_z = jnp.concatenate([w_in[:, :, off[n][0]:off[n][0] + off[n][1]] for n in order],
                          axis=-1).astype(BF16)
    a0 = off["alr"][0]
    w_alr = jnp.pad(w_in[:, :, a0:a0 + GLA_LOWRANK],
                    ((0, 0), (0, 0), (0, LANES - GLA_LOWRANK))).astype(BF16)
    a2 = jnp.pad(gla_a2, ((0, 0), (0, LANES - GLA_LOWRANK), (0, 0))).astype(BF16)
    ws_p = jnp.tril(cm_ws).astype(BF16)
    small = jnp.tril(cm_ws[:, :, :DEC_SEQ, :DEC_SEQ])
    eye = jnp.eye(SEQ_PER_BLK, dtype=F32)
    ws_s = jnp.einsum("ij,lgab->lgiajb", eye, small).reshape(
        DEPTH, CM_GROUPS, ROWS_S, ROWS_S).astype(BF16)
    cmb_p = jnp.broadcast_to(jnp.transpose(cm_b, (0, 2, 1))[:, :, :, None],
                             (DEPTH, CM_CHUNK, CM_GROUPS, CM_GCH)).reshape(DEPTH, CM_CHUNK, CM_CH)
    cmb_s = jnp.tile(cmb_p[:, :DEC_SEQ], (1, SEQ_PER_BLK, 1))
    pad = LANES - MOE_GROUPS - N_EXPERTS
    w_r = jnp.pad(jnp.concatenate([router_group_w, router_expert_w], axis=-1),
                  ((0, 0), (0, 0), (0, pad)))
    w_r_hi = w_r.astype(BF16)
    w_r_lo = (w_r - w_r_hi.astype(F32)).astype(BF16)
    r_b = jnp.pad(jnp.concatenate([router_group_b, router_expert_b], axis=-1),
                  ((0, 0), (0, pad)))[:, None, :]
    return w_z, w_alr, a2, ws_p, ws_s, cmb_p, cmb_s, w_r_hi, w_r_lo, r_b


def _dispatch(route, rank, counts):
    e1 = route[:, 0].astype(jnp.int32)
    e2 = route[:, 1].astype(jnp.int32)
    cnt = counts[0, :N_EXPERTS].astype(jnp.int32)
    blocks_per = (cnt + MOE_BLOCK - 1) // MOE_BLOCK
    blk_end = jnp.cumsum(blocks_per)
    base = (blk_end - blocks_per) * MOE_BLOCK
    dest1 = base[e1] + rank[:, 0]
    dest2 = base[e2] + rank[:, 1]
    tok = jnp.arange(T_ALL, dtype=jnp.int32)
    tok_buf = jnp.zeros((N_BLOCKS * MOE_BLOCK,), jnp.int32).at[dest1].set(tok).at[dest2].set(tok)
    block_expert = jnp.minimum(
        jnp.searchsorted(blk_end, jnp.arange(N_BLOCKS, dtype=jnp.int32), side="right"),
        N_EXPERTS - 1).astype(jnp.int32)
    n_used = blk_end[-1:].astype(jnp.int32)
    return dest1, dest2, tok_buf, block_expert, n_used


def kernel(x_prompt, x_sample, state_conv, state_gla, norm1_g, w_in, conv_w, gla_a2, gla_a_b,
           gla_norm_g, cm_norm_g, cm_ws, cm_b, proj_a, proj_b, proj_c, w_out, norm2_g,
           router_group_w, router_group_b, router_expert_w, router_expert_b,
           exp_w1, exp_w3, exp_w2, final_norm_g):
    (w_z, w_alr, a2, ws_p, ws_s, cmb_p, cmb_s, w_r_hi, w_r_lo, r_b) = _prep_weights(
        w_in, gla_a2, cm_ws, cm_b, router_group_w, router_group_b, router_expert_w,
        router_expert_b)
    pa, pb, pc, wo = (w.astype(BF16) for w in (proj_a, proj_b, proj_c, w_out))
    x = jnp.concatenate([x_prompt.reshape(T_PROMPT, D_MODEL),
                         x_sample.reshape(T_SAMPLE, D_MODEL)], axis=0)
    conv_p, gla_p, conv_s, gla_s, cmv_s = [], [], [], [], []
    for l in range(DEPTH):
        z, alr = _inproj(x, norm1_g[l][None, :], w_z[l], w_alr[l])
        ab = gla_a_b[l][None, :]
        gng = gla_norm_g[l][None, :]
        cmg = cm_norm_g[l].reshape(1, CM_CH)
        acts_p, nconv, ngla = _mix_prompt(z, alr, a2[l], ab, conv_w[l], gng, cmg, ws_p[l], cmb_p[l])
        sc = state_conv[l]
        p2 = jnp.pad(sc, ((0, 0), (0, DEC_SEQ - 2), (0, 0))).reshape(T_SAMPLE, CONV_CH)
        p1 = jnp.pad(sc[:, 1:2], ((0, 0), (0, DEC_SEQ - 1), (0, 0))).reshape(T_SAMPLE, CONV_CH)
        acts_s, cin_s, ns, vrows = _mix_sample(z, alr, a2[l], ab, conv_w[l], gng, cmg, ws_s[l],
                                               cmb_s[l], p1, p2, state_gla[l])
        conv_p.append(nconv)
        gla_p.append(ngla)
        conv_s.append(cin_s.reshape(DEC_BATCH, DEC_SEQ, CONV_CH)[:, DEC_SEQ - (CONV_K - 1):])
        gla_s.append(ns)
        cmv_s.append(vrows.reshape(DEC_BATCH, DEC_SEQ, CM_CH))

        x, xn, route = _outproj(acts_p, acts_s, z, x, pa[l], pb[l], pc[l], wo[l], norm2_g[l][None, :],
                                w_r_hi[l], w_r_lo[l], r_b[l])
        rank, counts = _rank(route)
        dest1, dest2, tok_buf, block_expert, n_used = _dispatch(route, rank, counts)
        yb = _ffn(block_expert, n_used, tok_buf, xn, exp_w1[l], exp_w3[l], exp_w2[l])
        x = _combine(dest1, dest2, yb, x, route, final_norm_g[None, :], l == DEPTH - 1)

    y_prompt = x[:T_PROMPT].reshape(BATCH, SEQ, D_MODEL)
    y_sample = x[T_PROMPT:].reshape(DEC_BATCH, DEC_SEQ, D_MODEL)
    return (y_prompt, y_sample, jnp.stack(conv_p), jnp.stack(gla_p), jnp.stack(conv_s),
            jnp.stack(gla_s), jnp.stack(cmv_s))
```

```python
import functools

import jax
import jax.numpy as jnp
from jax import lax
from jax.experimental import pallas as pl
from jax.experimental.pallas import tpu as pltpu

F32 = jnp.float32
BF16 = jnp.bfloat16

D_MODEL = 1024
BATCH = 8
SEQ = 2048
DEPTH = 2
DEC_BATCH = 128
DEC_SEQ = 8
CONV_K = 3
CONV_CH = 512
GLA_HEADS = 4
GLA_DK = 128
GLA_DV = 256
GLA_QK = GLA_HEADS * GLA_DK
GLA_V = GLA_HEADS * GLA_DV
GLA_LOWRANK = 16
GLA_TAU = 16.0
GLA_CHUNK = 64
CM_GROUPS = 4
CM_CHUNK = 128
CM_GCH = 128
CM_CH = 512
MOE_GROUPS = 8
EXPERTS_PER_GROUP = 8
N_EXPERTS = 64
D_EXPERT = 256
EPS = 1e-6

LANES = 128
T_PROMPT = BATCH * SEQ
T_SAMPLE = DEC_BATCH * DEC_SEQ
T_ALL = T_PROMPT + T_SAMPLE

COL_GA, COL_GB, COL_GC = 0, 1024, 2048
COL_V, COL_R, COL_Q, COL_K = 3072, 4096, 5120, 5632
COL_H, COL_CG, COL_BG, COL_U, COL_VV = 6144, 6656, 7168, 7680, 8192
Z_COLS = 8704
ACT_COLS = 2048
XG_COLS = D_MODEL + LANES
GRP_LANE = EXPERTS_PER_GROUP

TM_IN = 1024
TN_IN = Z_COLS // 4
TC_MIX = 256
SEQ_PER_BLK = 16
ROWS_S = SEQ_PER_BLK * DEC_SEQ
TM_OUT = 256
TM_RANK = 512
TM_ROW = 256
MOE_BLK = 256
N_BLK = T_ALL // MOE_BLK + MOE_GROUPS
N_SORTED = N_BLK * MOE_BLK
VMEM_LIMIT = 56 * 1024 * 1024


def _sigmoid(x):
    return 1.0 / (1.0 + jnp.exp(-x))


def _gelu_tanh(x):
    return 0.5 * x * (1.0 + jnp.tanh(0.7978845608028654 * (x + 0.044715 * (x * x * x))))


def _log_sigmoid(x):
    return jnp.minimum(x, 0.0) - jnp.log(1.0 + jnp.exp(-jnp.abs(x)))


def _rms(x, g):
    ms = jnp.mean(x * x, axis=-1, keepdims=True)
    return x * lax.rsqrt(ms + EPS) * g


def _split_bf16(x):
    hi = x.astype(BF16)
    lo = (x - hi.astype(F32)).astype(BF16)
    return hi, lo


def _dot(a, b):
    return jnp.dot(a, b, preferred_element_type=F32)


def _dot_nt(a, b):
    return lax.dot_general(a, b, (((1,), (1,)), ((), ())), preferred_element_type=F32)


def _layer_spec(arr, layer):
    nd = arr.ndim - 1
    return pl.BlockSpec((None,) + arr.shape[1:], lambda *g: (layer,) + (0,) * nd)


def _const_spec(arr):
    nd = arr.ndim
    return pl.BlockSpec(arr.shape, lambda *g: (0,) * nd)


def _inproj_kernel(x_ref, g_ref, w_ref, wa_ref, z_ref, a_ref, xn_ref):
    @pl.when(pl.program_id(1) == 0)
    def _():
        xn = _rms(x_ref[...], g_ref[...]).astype(BF16)
        xn_ref[...] = xn
        a_ref[...] = _dot(xn, wa_ref[...]).astype(BF16)

    z_ref[...] = _dot(xn_ref[...], w_ref[...]).astype(BF16)


def _inproj(x, g, w, wa, layer):
    m = x.shape[0]
    return pl.pallas_call(
        _inproj_kernel,
        grid=(m // TM_IN, Z_COLS // TN_IN),
        in_specs=[
            pl.BlockSpec((TM_IN, D_MODEL), lambda i, j: (i, 0)),
            _layer_spec(g, layer),
            pl.BlockSpec((None, D_MODEL, TN_IN), lambda i, j: (layer, 0, j)),
            _layer_spec(wa, layer),
        ],
        out_specs=[
            pl.BlockSpec((TM_IN, TN_IN), lambda i, j: (i, j)),
            pl.BlockSpec((TM_IN, LANES), lambda i, j: (i, 0)),
        ],
        out_shape=[
            jax.ShapeDtypeStruct((m, Z_COLS), BF16),
            jax.ShapeDtypeStruct((m, LANES), BF16),
        ],
        scratch_shapes=[pltpu.VMEM((TM_IN, D_MODEL), BF16)],
        compiler_params=pltpu.CompilerParams(
            dimension_semantics=("arbitrary", "arbitrary"), vmem_limit_bytes=VMEM_LIMIT),
        name="inproj",
    )(x, g, w, wa)


def _gla_decay_terms(alr_ref, a2_ref, ab_ref, tril_mask, same_mask):
    la = _log_sigmoid(_dot(alr_ref[...], a2_ref[...]) + ab_ref[...]) * (1.0 / GLA_TAU)
    la_hi, la_lo = _split_bf16(la)
    l_tri = jnp.where(tril_mask, 1.0, 0.0).astype(BF16)
    l_same = jnp.where(same_mask, 1.0, 0.0).astype(BF16)
    b = _dot(l_tri, la_hi) + _dot(l_tri, la_lo)
    bl = _dot(l_same, la_hi) + _dot(l_same, la_lo)
    return la_hi, la_lo, b, bl


def _gla_out_gate(o, g_ref, r):
    return _rms(o, g_ref[...]) * (r * _sigmoid(r))


def _chunk_mlp(u, vv, cmg_ref, ws_ref, cmb_ref, n_chunks):
    ug = _gelu_tanh(u)
    gv = _gelu_tanh(vv)
    outs, vgs = [], []
    for g in range(CM_GROUPS):
        sl = slice(g * CM_GCH, (g + 1) * CM_GCH)
        vg = _rms(gv[:, sl], cmg_ref[:, sl])
        vgs.append(vg)
        vgb = vg.astype(BF16)
        rows = []
        for j in range(n_chunks):
            rs = slice(j * CM_CHUNK, (j + 1) * CM_CHUNK)
            rows.append(_dot(ws_ref[g], vgb[rs]) + cmb_ref[:, sl])
        s = rows[0] if n_chunks == 1 else jnp.concatenate(rows, axis=0)
        outs.append(ug[:, sl] * s)
    return jnp.concatenate(outs, axis=1), jnp.concatenate(vgs, axis=1)


def _mix_prompt_kernel(v_ref, r_ref, qk_ref, hcg_ref, bgu_ref, vv_ref, alr_ref,
                       a2_ref, ab_ref, cw_ref, gng_ref, cmg_ref, ws_ref, cmb_ref,
                       acts_ref, nconv_ref, ngla_ref, s_ref, carry_ref):
    tc = TC_MIX

    @pl.when(pl.program_id(1) == 0)
    def _():
        s_ref[...] = jnp.zeros_like(s_ref)
        carry_ref[...] = jnp.zeros_like(carry_ref)

    row = lax.broadcasted_iota(jnp.int32, (tc, 1), 0)
    h = hcg_ref[:, :CONV_CH].astype(F32)
    cg = hcg_ref[:, CONV_CH:].astype(F32)
    bg = bgu_ref[:, :CONV_CH].astype(F32)
    cin = cg * h
    c0 = carry_ref[0:1, :]
    c1 = carry_ref[1:2, :]
    x1 = jnp.where(row >= 1, pltpu.roll(cin, 1, 0), c1)
    x2 = jnp.where(row >= 2, pltpu.roll(cin, 2, 0), jnp.where(row == 1, c1, c0))
    conv = x2 * cw_ref[0:1, :] + x1 * cw_ref[1:2, :] + cin * cw_ref[2:3, :]
    acts_ref[:, 0:CONV_CH] = (bg * conv).astype(BF16)
    carry_ref[0:2, :] = cin[tc - 2:tc, :]
    nconv_ref[0] = cin[tc - 2:tc, :]

    rr = lax.broadcasted_iota(jnp.int32, (tc, tc), 0)
    cc = lax.broadcasted_iota(jnp.int32, (tc, tc), 1)
    same = (rr >> 6) == (cc >> 6)
    tril = same & (cc <= rr)
    la_hi, la_lo, b, bl = _gla_decay_terms(alr_ref, a2_ref, ab_ref, tril, same)
    q = qk_ref[:, :GLA_QK].astype(F32) * (GLA_DK ** -0.5)
    k = qk_ref[:, GLA_QK:].astype(F32)
    q_t = (q * jnp.exp(b)).astype(BF16)
    k_t = (k * jnp.exp(-b)).astype(BF16)
    k_end = k * jnp.exp(bl - b)
    la_hi = la_hi.astype(F32)
    la_lo = la_lo.astype(F32)
    lane_chunk = lax.broadcasted_iota(jnp.int32, (GLA_DK, tc), 1) >> 6
    ones = jnp.ones((tc, GLA_DV), BF16)
    n_chunks = tc // GLA_CHUNK
    for hd in range(GLA_HEADS):
        ks = slice(hd * GLA_DK, (hd + 1) * GLA_DK)
        vs = slice(hd * GLA_DV, (hd + 1) * GLA_DV)
        qh = q_t[:, ks]
        vh = v_ref[:, vs]
        att = jnp.where(tril, _dot_nt(qh, k_t[:, ks]), 0.0).astype(BF16)
        o_intra = _dot(att, vh)
        k_tr = k_end[:, ks].T
        lh_tr = la_hi[:, ks].T
        ll_tr = la_lo[:, ks].T
        o_rows = []
        for c in range(n_chunks):
            rs = slice(c * GLA_CHUNK, (c + 1) * GLA_CHUNK)
            s_old = s_ref[hd]
            o_rows.append(o_intra[rs] + _dot(qh[rs], s_old.astype(BF16)))
            in_c = lane_chunk == c
            zero = jnp.zeros_like(k_tr)
            dlog = (_dot(jnp.where(in_c, lh_tr, zero).astype(BF16), ones)
                    + _dot(jnp.where(in_c, ll_tr, zero).astype(BF16), ones))
            upd = _dot(jnp.where(in_c, k_tr, zero).astype(BF16), vh)
            s_ref[hd] = jnp.exp(dlog) * s_old + upd
        o = jnp.concatenate(o_rows, axis=0)
        r = r_ref[:, vs].astype(F32)
        acts_ref[:, CONV_CH + hd * GLA_DV:CONV_CH + (hd + 1) * GLA_DV] = (
            _gla_out_gate(o, gng_ref, r).astype(BF16))
    ngla_ref[0] = s_ref[...]

    us, _ = _chunk_mlp(bgu_ref[:, CONV_CH:].astype(F32), vv_ref[...].astype(F32),
                       cmg_ref, ws_ref, cmb_ref, tc // CM_CHUNK)
    acts_ref[:, CONV_CH + GLA_V:] = us.astype(BF16)


def _z_specs(rows, row_map):
    def spec(width, col):
        blk = col // width
        return pl.BlockSpec((rows, width), lambda *g: (row_map(*g), blk))
    return [spec(1024, COL_V), spec(1024, COL_R), spec(1024, COL_Q), spec(1024, COL_H),
            spec(1024, COL_BG), spec(512, COL_VV)]


def _mix_prompt(z, alr, a2, ab, cw, gng, cmg, ws, cmb, layer):
    nt = SEQ // TC_MIX
    row_map = lambda b, c: b * nt + c
    small = (a2, ab, cw, gng, cmg, ws, cmb)
    in_specs = _z_specs(TC_MIX, row_map) + [
        pl.BlockSpec((TC_MIX, LANES), lambda b, c: (row_map(b, c), 0)),
    ] + [_layer_spec(a, layer) for a in small]
    return pl.pallas_call(
        _mix_prompt_kernel,
        grid=(BATCH, nt),
        in_specs=in_specs,
        out_specs=[
            pl.BlockSpec((TC_MIX, ACT_COLS), lambda b, c: (row_map(b, c), 0)),
            pl.BlockSpec((1, CONV_K - 1, CONV_CH), lambda b, c: (b, 0, 0)),
            pl.BlockSpec((1, GLA_HEADS, GLA_DK, GLA_DV), lambda b, c: (b, 0, 0, 0)),
        ],
        out_shape=[
            jax.ShapeDtypeStruct((T_PROMPT, ACT_COLS), BF16),
            jax.ShapeDtypeStruct((BATCH, CONV_K - 1, CONV_CH), F32),
            jax.ShapeDtypeStruct((BATCH, GLA_HEADS, GLA_DK, GLA_DV), F32),
        ],
        scratch_shapes=[pltpu.VMEM((GLA_HEADS, GLA_DK, GLA_DV), F32),
                        pltpu.VMEM((8, CONV_CH), F32)],
        compiler_params=pltpu.CompilerParams(
            dimension_semantics=("arbitrary", "arbitrary"), vmem_limit_bytes=VMEM_LIMIT),
        name="mix_prompt",
    )(z, z, z, z, z, z, alr, *small)


def _mix_sample_kernel(v_ref, r_ref, qk_ref, hcg_ref, bgu_ref, vv_ref, alr_ref,
                       a2_ref, ab_ref, cw_ref, gng_ref, cmg_ref, ws_ref, cmb_ref,
                       p1_ref, p2_ref, s0_ref,
                       acts_ref, cin_ref, ns_ref, vrow_ref):
    n = ROWS_S

    pos = lax.broadcasted_iota(jnp.int32, (n, 1), 0) & (DEC_SEQ - 1)
    h = hcg_ref[:, :CONV_CH].astype(F32)
    cg = hcg_ref[:, CONV_CH:].astype(F32)
    bg = bgu_ref[:, :CONV_CH].astype(F32)
    cin = cg * h
    x1 = jnp.where(pos >= 1, pltpu.roll(cin, 1, 0), p1_ref[...])
    x2 = jnp.where(pos >= 2, pltpu.roll(cin, 2, 0), p2_ref[...])
    conv = x2 * cw_ref[0:1, :] + x1 * cw_ref[1:2, :] + cin * cw_ref[2:3, :]
    acts_ref[:, 0:CONV_CH] = (bg * conv).astype(BF16)
    cin_ref[...] = cin

    rr = lax.broadcasted_iota(jnp.int32, (n, n), 0)
    cc = lax.broadcasted_iota(jnp.int32, (n, n), 1)
    same = (rr >> 3) == (cc >> 3)
    tril = same & (cc <= rr)
    la_hi, la_lo, b, bl = _gla_decay_terms(alr_ref, a2_ref, ab_ref, tril, same)
    q = qk_ref[:, :GLA_QK].astype(F32) * (GLA_DK ** -0.5)
    k = qk_ref[:, GLA_QK:].astype(F32)
    q_t = (q * jnp.exp(b)).astype(BF16)
    k_t = (k * jnp.exp(-b)).astype(BF16)
    k_end = k * jnp.exp(bl - b)
    la_hi = la_hi.astype(F32)
    la_lo = la_lo.astype(F32)
    row_seq = lax.broadcasted_iota(jnp.int32, (n, GLA_DK), 0) >> 3
    seq3 = lax.broadcasted_iota(jnp.int32, (SEQ_PER_BLK, GLA_DK, n), 0)
    lane_seq3 = lax.broadcasted_iota(jnp.int32, (SEQ_PER_BLK, GLA_DK, n), 2) >> 3
    mask3 = seq3 == lane_seq3
    ones = jnp.ones((n, GLA_DV), BF16)
    big = SEQ_PER_BLK * GLA_DK

    def per_seq(x_tr):
        x3 = jnp.where(mask3, x_tr[None, :, :], 0.0)
        return x3.reshape(big, n).astype(BF16)

    for hd in range(GLA_HEADS):
        ks = slice(hd * GLA_DK, (hd + 1) * GLA_DK)
        vs = slice(hd * GLA_DV, (hd + 1) * GLA_DV)
        qh = q_t[:, ks]
        vh = v_ref[:, vs]
        att = jnp.where(tril, _dot_nt(qh, k_t[:, ks]), 0.0).astype(BF16)
        o_intra = _dot(att, vh)
        s_old = s0_ref[:, hd].reshape(big, GLA_DV)
        zero = jnp.zeros_like(qh)
        q_big = jnp.concatenate(
            [jnp.where(row_seq == j, qh, zero) for j in range(SEQ_PER_BLK)], axis=1)
        o = o_intra + _dot(q_big, s_old.astype(BF16))
        dlog = _dot(per_seq(la_hi[:, ks].T), ones) + _dot(per_seq(la_lo[:, ks].T), ones)
        upd = _dot(per_seq(k_end[:, ks].T), vh)
        s_new = jnp.exp(dlog) * s_old + upd
        ns_ref[:, hd] = s_new.reshape(SEQ_PER_BLK, GLA_DK, GLA_DV)
        r = r_ref[:, vs].astype(F32)
        acts_ref[:, CONV_CH + hd * GLA_DV:CONV_CH + (hd + 1) * GLA_DV] = (
            _gla_out_gate(o, gng_ref, r).astype(BF16))

    us, vg = _chunk_mlp(bgu_ref[:, CONV_CH:].astype(F32), vv_ref[...].astype(F32),
                        cmg_ref, ws_ref, cmb_ref, 1)
    acts_ref[:, CONV_CH + GLA_V:] = us.astype(BF16)
    vrow_ref[...] = vg


def _mix_sample(z, alr, a2, ab, cw, gng, cmg, ws, cmb, p1, p2, s0, layer):
    row0 = T_PROMPT // ROWS_S
    row_map = lambda i: row0 + i
    small = (a2, ab, cw, gng, cmg, ws, cmb)
    state_blk = (SEQ_PER_BLK, GLA_HEADS, GLA_DK, GLA_DV)
    in_specs = _z_specs(ROWS_S, row_map) + [
        pl.BlockSpec((ROWS_S, LANES), lambda i: (row_map(i), 0)),
    ] + [_layer_spec(a, layer) for a in small] + [
        pl.BlockSpec((ROWS_S, CONV_CH), lambda i: (i, 0)),
        pl.BlockSpec((ROWS_S, CONV_CH), lambda i: (i, 0)),
        pl.BlockSpec((None,) + state_blk, lambda i: (layer, i, 0, 0, 0)),
    ]
    return pl.pallas_call(
        _mix_sample_kernel,
        grid=(DEC_BATCH // SEQ_PER_BLK,),
        in_specs=in_specs,
        out_specs=[
            pl.BlockSpec((ROWS_S, ACT_COLS), lambda i: (i, 0)),
            pl.BlockSpec((ROWS_S, CONV_CH), lambda i: (i, 0)),
            pl.BlockSpec(state_blk, lambda i: (i, 0, 0, 0)),
            pl.BlockSpec((ROWS_S, CM_CH), lambda i: (i, 0)),
        ],
        out_shape=[
            jax.ShapeDtypeStruct((T_SAMPLE, ACT_COLS), BF16),
            jax.ShapeDtypeStruct((T_SAMPLE, CONV_CH), F32),
            jax.ShapeDtypeStruct((DEC_BATCH, GLA_HEADS, GLA_DK, GLA_DV), F32),
            jax.ShapeDtypeStruct((T_SAMPLE, CM_CH), F32),
        ],
        compiler_params=pltpu.CompilerParams(
            dimension_semantics=("arbitrary",), vmem_limit_bytes=VMEM_LIMIT),
        name="mix_sample",
    )(z, z, z, z, z, z, alr, *small, p1, p2, s0)


def _outproj_kernel(actp_ref, acts_ref, gates_ref, x_ref, pa_ref, pb_ref, pc_ref, wo_ref, n2_ref,
                    wrh_ref, wrl_ref, rb_ref, xo_ref, xg_ref):
    acts = jnp.where(pl.program_id(0) < T_PROMPT // TM_OUT, actp_ref[...], acts_ref[...])
    ya = _dot(acts[:, :CONV_CH], pa_ref[...])
    yb = _dot(acts[:, CONV_CH:CONV_CH + GLA_V], pb_ref[...])
    yc = _dot(acts[:, CONV_CH + GLA_V:], pc_ref[...])
    ga = _sigmoid(gates_ref[:, COL_GA:COL_GA + D_MODEL].astype(F32))
    gb = _sigmoid(gates_ref[:, COL_GB:COL_GB + D_MODEL].astype(F32))
    gc = _sigmoid(gates_ref[:, COL_GC:COL_GC + D_MODEL].astype(F32))
    mix = ga * ya + gb * yb + gc * yc
    x = x_ref[...] + _dot(mix.astype(BF16), wo_ref[...])
    xo_ref[...] = x
    xn = _rms(x, n2_ref[...])
    xg_ref[:, :D_MODEL] = xn

    hi, lo = _split_bf16(xn)
    logits = (_dot(hi, wrh_ref[...]) + _dot(lo, wrh_ref[...]) + _dot(hi, wrl_ref[...])
              + rb_ref[...])
    lane_i = lax.broadcasted_iota(jnp.int32, logits.shape, 1)
    lane = lane_i.astype(F32)
    neg = jnp.float32(-jnp.inf)
    is_g = lane_i < MOE_GROUPS
    lg = jnp.where(is_g, logits, neg)
    gmax = jnp.max(lg, axis=-1, keepdims=True)
    grp = jnp.min(jnp.where(lg == gmax, lane, 1e9), axis=-1, keepdims=True)
    p_grp = 1.0 / jnp.sum(jnp.where(is_g, jnp.exp(lg - gmax), 0.0), axis=-1, keepdims=True)
    first = MOE_GROUPS + EXPERTS_PER_GROUP * grp
    in_grp = (lane >= first) & (lane < first + EXPERTS_PER_GROUP)
    le = jnp.where(in_grp, logits, neg)
    v1 = jnp.max(le, axis=-1, keepdims=True)
    i1 = jnp.min(jnp.where(le == v1, lane, 1e9), axis=-1, keepdims=True)
    le2 = jnp.where(lane == i1, neg, le)
    v2 = jnp.max(le2, axis=-1, keepdims=True)
    i2 = jnp.min(jnp.where(le2 == v2, lane, 1e9), axis=-1, keepdims=True)
    t = jnp.exp(v2 - v1)
    g1 = p_grp / (1.0 + t)
    g2 = p_grp * t / (1.0 + t)
    xg_ref[:, D_MODEL:] = jnp.where(
        lane == i1 - first, g1,
        jnp.where(lane == i2 - first, g2, jnp.where(lane_i == GRP_LANE, grp, 0.0)))


def _outproj(acts_p, acts_s, z, x, pa, pb, pc, wo, n2, wrh, wrl, rb, layer):
    m = x.shape[0]
    row = lambda i: (i, 0)
    n_p = T_PROMPT // TM_OUT
    weights = (pa, pb, pc, wo, n2, wrh, wrl, rb)
    return pl.pallas_call(
        _outproj_kernel,
        grid=(m // TM_OUT,),
        in_specs=[
            pl.BlockSpec((TM_OUT, ACT_COLS), lambda i: (jnp.minimum(i, n_p - 1), 0)),
            pl.BlockSpec((TM_OUT, ACT_COLS), lambda i: (jnp.maximum(i - n_p, 0), 0)),
            pl.BlockSpec((TM_OUT, 3 * D_MODEL), row),
            pl.BlockSpec((TM_OUT, D_MODEL), row),
        ] + [_layer_spec(w, layer) for w in weights],
        out_specs=[
            pl.BlockSpec((TM_OUT, D_MODEL), row),
            pl.BlockSpec((TM_OUT, XG_COLS), row),
        ],
        out_shape=[
            jax.ShapeDtypeStruct((m, D_MODEL), F32),
            jax.ShapeDtypeStruct((m, XG_COLS), F32),
        ],
        compiler_params=pltpu.CompilerParams(
            dimension_semantics=("arbitrary",), vmem_limit_bytes=VMEM_LIMIT),
        name="outproj",
    )(acts_p, acts_s, z, x, *weights)


def _rank_kernel(r_ref, dest_ref, meta_ref, cnt_ref, base_ref):
    p = pl.program_id(0)
    i = pl.program_id(1)
    tm = TM_RANK
    lane_i = lax.broadcasted_iota(jnp.int32, (tm, LANES), 1)
    onehot = jnp.where(lane_i.astype(F32) == r_ref[:, GRP_LANE:GRP_LANE + 1], 1.0, 0.0)
    csum = jnp.sum(onehot, axis=0, keepdims=True)

    @pl.when((p == 0) & (i == 0))
    def _():
        cnt_ref[...] = jnp.zeros_like(cnt_ref)

    @pl.when(p == 0)
    def _():
        cnt_ref[...] += csum

    @pl.when((p == 1) & (i == 0))
    def _():
        blocks = jnp.floor((cnt_ref[...] + (MOE_BLK - 1)) * (1.0 / MOE_BLK))
        rr = lax.broadcasted_iota(jnp.int32, (LANES, LANES), 0)
        cc = lax.broadcasted_iota(jnp.int32, (LANES, LANES), 1)
        earlier = jnp.where(rr < cc, 1.0, 0.0).astype(BF16)
        start = _dot(jnp.broadcast_to(blocks, (8, LANES)).astype(BF16), earlier)[0:1]
        base_ref[...] = start * MOE_BLK
        end = start + blocks
        lane1_i = lax.broadcasted_iota(jnp.int32, (1, LANES), 1)
        lane1 = lane1_i.astype(F32)
        grp_of_blk = jnp.zeros((1, LANES), F32)
        for g in range(MOE_GROUPS):
            end_g = jnp.sum(jnp.where(lane1_i == g, end, 0.0), axis=-1, keepdims=True)
            grp_of_blk += jnp.where(lane1 >= end_g, 1.0, 0.0)
        grp_of_blk = jnp.minimum(grp_of_blk, MOE_GROUPS - 1)
        n_used = jnp.sum(blocks, axis=-1, keepdims=True)
        row = lax.broadcasted_iota(jnp.int32, (8, LANES), 0)
        meta_ref[...] = jnp.where(row == 0, grp_of_blk,
                                  jnp.where(row == 1, n_used, 0.0)).astype(jnp.int32)
        cnt_ref[...] = jnp.zeros_like(cnt_ref)

    @pl.when(p == 1)
    def _():
        rr = lax.broadcasted_iota(jnp.int32, (tm, tm), 0)
        cc = lax.broadcasted_iota(jnp.int32, (tm, tm), 1)
        before = jnp.where(cc < rr, 1.0, 0.0).astype(BF16)
        prefix = _dot(before, onehot.astype(BF16))
        carry = cnt_ref[...]
        d = jnp.sum(onehot * (prefix + carry + base_ref[...]), axis=-1, keepdims=True)
        dest_ref[...] = jnp.where(lane_i == 0, d, 0.0).astype(jnp.int32)
        cnt_ref[...] = carry + csum


def _rank(xg):
    m = xg.shape[0]
    gate_blk = D_MODEL // LANES
    return pl.pallas_call(
        _rank_kernel,
        grid=(2, m // TM_RANK),
        in_specs=[pl.BlockSpec((TM_RANK, LANES), lambda p, i: (i, gate_blk))],
        out_specs=[pl.BlockSpec((TM_RANK, LANES), lambda p, i: (i * p, 0)),
                   pl.BlockSpec((8, LANES), lambda p, i: (0, 0))],
        out_shape=[jax.ShapeDtypeStruct((m, LANES), jnp.int32),
                   jax.ShapeDtypeStruct((8, LANES), jnp.int32)],
        scratch_shapes=[pltpu.VMEM((1, LANES), F32), pltpu.VMEM((1, LANES), F32)],
        compiler_params=pltpu.CompilerParams(dimension_semantics=("arbitrary", "arbitrary")),
        name="rank",
    )(xg)


def _scatter_kernel(d_ref, xg_ref, xs_in_ref, xs_ref, sem):
    del xs_in_ref

    def row_copy(r):
        return pltpu.make_async_copy(
            xg_ref.at[pl.ds(r, 1)], xs_ref.at[pl.ds(d_ref[0, 0, r], 1)], sem)

    def start(r, carry):
        row_copy(r).start()
        return carry

    def wait(r, carry):
        row_copy(r).wait()
        return carry

    lax.fori_loop(0, TM_ROW, start, 0, unroll=8)
    lax.fori_loop(0, TM_ROW, wait, 0, unroll=8)


def _scatter(dest, xg, xs0):
    m = xg.shape[0]
    nt = m // TM_ROW
    return pl.pallas_call(
        _scatter_kernel,
        grid=(nt,),
        in_specs=[
            pl.BlockSpec((1, 1, TM_ROW), lambda i: (i, 0, 0), memory_space=pltpu.SMEM),
            pl.BlockSpec((TM_ROW, XG_COLS), lambda i: (i, 0)),
            pl.BlockSpec(memory_space=pl.ANY),
        ],
        out_specs=pl.BlockSpec(memory_space=pl.ANY),
        out_shape=jax.ShapeDtypeStruct((N_SORTED, XG_COLS), F32),
        scratch_shapes=[pltpu.SemaphoreType.DMA(())],
        input_output_aliases={2: 0},
        compiler_params=pltpu.CompilerParams(dimension_semantics=("arbitrary",)),
        name="scatter",
    )(dest.reshape(nt, 1, TM_ROW), xg, xs0)


def _ffn_kernel(bg_ref, nu_ref, xs_ref, w1_ref, w3_ref, w2_ref, y_ref):
    del bg_ref
    b = pl.program_id(0)

    @pl.when(b < nu_ref[0])
    def _():
        x = xs_ref[:, :D_MODEL].astype(BF16)
        gates = xs_ref[:, D_MODEL:]
        hs = []
        for e in range(EXPERTS_PER_GROUP):
            h1 = _dot(x, w1_ref[e])
            h3 = _dot(x, w3_ref[e])
            ge = gates[:, e:e + 1]
            hs.append(jnp.where(ge > 0.0, h1 * _sigmoid(h1) * h3 * ge, 0.0).astype(BF16))
        hcat = jnp.concatenate(hs, axis=1)
        y_ref[...] = _dot(hcat, w2_ref[...].reshape(EXPERTS_PER_GROUP * D_EXPERT, D_MODEL))

    @pl.when(b >= nu_ref[0])
    def _():
        y_ref[...] = jnp.zeros_like(y_ref)


def _ffn(blk_group, n_used, xs, w1, w3, w2, layer):
    wmap = lambda b, bg, nu: (layer * MOE_GROUPS + bg[b], 0, 0, 0)
    grid_spec = pltpu.PrefetchScalarGridSpec(
        num_scalar_prefetch=2,
        grid=(N_BLK,),
        in_specs=[
            pl.BlockSpec((MOE_BLK, XG_COLS), lambda b, bg, nu: (b, 0)),
            pl.BlockSpec((None, EXPERTS_PER_GROUP, D_MODEL, D_EXPERT), wmap),
            pl.BlockSpec((None, EXPERTS_PER_GROUP, D_MODEL, D_EXPERT), wmap),
            pl.BlockSpec((None, EXPERTS_PER_GROUP, D_EXPERT, D_MODEL), wmap),
        ],
        out_specs=pl.BlockSpec((MOE_BLK, D_MODEL), lambda b, bg, nu: (b, 0)),
    )
    return pl.pallas_call(
        _ffn_kernel,
        grid_spec=grid_spec,
        out_shape=jax.ShapeDtypeStruct((N_SORTED, D_MODEL), F32),
        compiler_params=pltpu.CompilerParams(
            dimension_semantics=("arbitrary",), vmem_limit_bytes=VMEM_LIMIT),
        name="ffn",
    )(blk_group, n_used, xs, w1, w3, w2)


def _combine_kernel(d_ref, dn_ref, ys_hbm, x_ref, g_ref, o_ref, buf, sem, *, final):
    i = pl.program_id(0)
    n = pl.num_programs(0)
    slot = i % 2

    def row_copy(ids, r, s):
        return pltpu.make_async_copy(
            ys_hbm.at[pl.ds(ids[0, 0, r], 1)], buf.at[s, pl.ds(r, 1)], sem.at[s])

    def gather(ids, s):
        def body(r, carry):
            row_copy(ids, r, s).start()
            return carry
        lax.fori_loop(0, TM_ROW, body, 0, unroll=8)

    @pl.when(i == 0)
    def _():
        gather(d_ref, 0)

    @pl.when(i + 1 < n)
    def _():
        gather(dn_ref, 1 - slot)

    def wait(r, carry):
        row_copy(d_ref, r, slot).wait()
        return carry
    lax.fori_loop(0, TM_ROW, wait, 0, unroll=8)

    y = x_ref[...] + buf[slot]
    if final:
        y = _rms(y, g_ref[...])
    o_ref[...] = y


def _combine(dest, ys, x, g, final):
    m = x.shape[0]
    nt = m // TM_ROW
    ids = dest.reshape(nt, 1, TM_ROW)
    smem = functools.partial(pl.BlockSpec, (1, 1, TM_ROW), memory_space=pltpu.SMEM)
    return pl.pallas_call(
        functools.partial(_combine_kernel, final=final),
        grid=(nt,),
        in_specs=[
            smem(lambda i: (i, 0, 0)),
            smem(lambda i: (jnp.minimum(i + 1, nt - 1), 0, 0)),
            pl.BlockSpec(memory_space=pl.ANY),
            pl.BlockSpec((TM_ROW, D_MODEL), lambda i: (i, 0)),
            _const_spec(g),
        ],
        out_specs=pl.BlockSpec((TM_ROW, D_MODEL), lambda i: (i, 0)),
        out_shape=jax.ShapeDtypeStruct((m, D_MODEL), F32),
        scratch_shapes=[pltpu.VMEM((2, TM_ROW, D_MODEL), F32),
                        pltpu.SemaphoreType.DMA((2,))],
        compiler_params=pltpu.CompilerParams(
            dimension_semantics=("arbitrary",), vmem_limit_bytes=VMEM_LIMIT),
        name="combine",
    )(ids, ids, ys, x, g)


def _prep_weights(w_in, gla_a2, cm_ws, cm_b, router_group_w, router_group_b,
                  router_expert_w, router_expert_b):
    off = {}
    o = 0
    for name, n in (("h", 512), ("cg", 512), ("bg", 512), ("q", 512), ("k", 512), ("v", 1024),
                    ("r", 1024), ("alr", 16), ("u", 512), ("vv", 512), ("ga", 1024),
                    ("gb", 1024), ("gc", 1024)):
        off[name] = (o, n)
        o += n
    order = ("ga", "gb", "gc", "v", "r", "q", "k", "h", "cg", "bg", "u", "vv")
    w_z = jnp.concatenate([w_in[:, :, off[n][0]:off[n][0] + off[n][1]] for n in order],
                          axis=-1).astype(BF16)
    a0 = off["alr"][0]
    w_alr = jnp.pad(w_in[:, :, a0:a0 + GLA_LOWRANK],
                    ((0, 0), (0, 0), (0, LANES - GLA_LOWRANK))).astype(BF16)
    a2 = jnp.pad(gla_a2, ((0, 0), (0, LANES - GLA_LOWRANK), (0, 0))).astype(BF16)
    ws_p = jnp.tril(cm_ws).astype(BF16)
    small = jnp.tril(cm_ws[:, :, :DEC_SEQ, :DEC_SEQ])
    eye = jnp.eye(SEQ_PER_BLK, dtype=F32)
    ws_s = jnp.einsum("ij,lgab->lgiajb", eye, small).reshape(
        DEPTH, CM_GROUPS, ROWS_S, ROWS_S).astype(BF16)
    cmb_p = jnp.broadcast_to(jnp.transpose(cm_b, (0, 2, 1))[:, :, :, None],
                             (DEPTH, CM_CHUNK, CM_GROUPS, CM_GCH)).reshape(DEPTH, CM_CHUNK, CM_CH)
    cmb_s = jnp.tile(cmb_p[:, :DEC_SEQ], (1, SEQ_PER_BLK, 1))
    pad = LANES - MOE_GROUPS - N_EXPERTS
    w_r = jnp.pad(jnp.concatenate([router_group_w, router_expert_w], axis=-1),
                  ((0, 0), (0, 0), (0, pad)))
    w_r_hi = w_r.astype(BF16)
    w_r_lo = (w_r - w_r_hi.astype(F32)).astype(BF16)
    r_b = jnp.pad(jnp.concatenate([router_group_b, router_expert_b], axis=-1),
                  ((0, 0), (0, pad)))[:, None, :]
    return w_z, w_alr, a2, ws_p, ws_s, cmb_p, cmb_s, w_r_hi, w_r_lo, r_b


def kernel(x_prompt, x_sample, state_conv, state_gla, norm1_g, w_in, conv_w, gla_a2, gla_a_b,
           gla_norm_g, cm_norm_g, cm_ws, cm_b, proj_a, proj_b, proj_c, w_out, norm2_g,
           router_group_w, router_group_b, router_expert_w, router_expert_b,
           exp_w1, exp_w3, exp_w2, final_norm_g):
    (w_z, w_alr, a2, ws_p, ws_s, cmb_p, cmb_s, w_r_hi, w_r_lo, r_b) = _prep_weights(
        w_in, gla_a2, cm_ws, cm_b, router_group_w, router_group_b, router_expert_w,
        router_expert_b)
    pa, pb, pc, wo = (w.astype(BF16) for w in (proj_a, proj_b, proj_c, w_out))
    grouped = (DEPTH * MOE_GROUPS, EXPERTS_PER_GROUP)
    w1 = exp_w1.astype(BF16).reshape(grouped + (D_MODEL, D_EXPERT))
    w3 = exp_w3.astype(BF16).reshape(grouped + (D_MODEL, D_EXPERT))
    w2 = exp_w2.astype(BF16).reshape(grouped + (D_EXPERT, D_MODEL))
    n1 = norm1_g[:, None, :]
    n2 = norm2_g[:, None, :]
    ab = gla_a_b[:, None, :]
    gng = gla_norm_g[:, None, :]
    cmg = cm_norm_g.reshape(DEPTH, 1, CM_CH)
    fg = final_norm_g[None, :]
    x = jnp.concatenate([x_prompt.reshape(T_PROMPT, D_MODEL),
                         x_sample.reshape(T_SAMPLE, D_MODEL)], axis=0)
    xs0 = jnp.zeros((N_SORTED, XG_COLS), F32)
    conv_p, gla_p, conv_s, gla_s, cmv_s = [], [], [], [], []
    for l in range(DEPTH):
        z, alr = _inproj(x, n1, w_z, w_alr, l)
        acts_p, nconv, ngla = _mix_prompt(z, alr, a2, ab, conv_w, gng, cmg, ws_p, cmb_p, l)
        sc = state_conv[l]
        p2 = jnp.pad(sc, ((0, 0), (0, DEC_SEQ - 2), (0, 0))).reshape(T_SAMPLE, CONV_CH)
        p1 = jnp.pad(sc[:, 1:2], ((0, 0), (0, DEC_SEQ - 1), (0, 0))).reshape(T_SAMPLE, CONV_CH)
        acts_s, cin_s, ns, vrows = _mix_sample(z, alr, a2, ab, conv_w, gng, cmg, ws_s, cmb_s,
                                               p1, p2, state_gla, l)
        conv_p.append(nconv)
        gla_p.append(ngla)
        conv_s.append(cin_s.reshape(DEC_BATCH, DEC_SEQ, CONV_CH)[:, DEC_SEQ - (CONV_K - 1):])
        gla_s.append(ns)
        cmv_s.append(vrows.reshape(DEC_BATCH, DEC_SEQ, CM_CH))

        x, xg = _outproj(acts_p, acts_s, z, x, pa, pb, pc, wo, n2, w_r_hi, w_r_lo, r_b, l)
        dest_rows, meta = _rank(xg)
        dest = dest_rows[:, 0]
        xs = _scatter(dest, xg, xs0)
        ys = _ffn(meta[0, :N_BLK], meta[1, :1], xs, w1, w3, w2, l)
        x = _combine(dest, ys, x, fg, l == DEPTH - 1)

    y_prompt = x[:T_PROMPT].reshape(BATCH, SEQ, D_MODEL)
    y_sample = x[T_PROMPT:].reshape(DEC_BATCH, DEC_SEQ, D_MODEL)
    return (y_prompt, y_sample, jnp.stack(conv_p), jnp.stack(gla_p), jnp.stack(conv_s),
            jnp.stack(gla_s), jnp.stack(cmv_s))
```

```python
import functools

import jax
import jax.numpy as jnp
from jax import lax
from jax.experimental import pallas as pl
from jax.experimental.pallas import tpu as pltpu

F32 = jnp.float32
BF16 = jnp.bfloat16

D_MODEL = 1024
BATCH = 8
SEQ = 2048
DEPTH = 2
DEC_BATCH = 128
DEC_SEQ = 8
CONV_K = 3
CONV_CH = 512
GLA_HEADS = 4
GLA_DK = 128
GLA_DV = 256
GLA_QK = GLA_HEADS * GLA_DK
GLA_V = GLA_HEADS * GLA_DV
GLA_LOWRANK = 16
GLA_TAU = 16.0
GLA_CHUNK = 64
CM_GROUPS = 4
CM_CHUNK = 128
CM_GCH = 128
CM_CH = 512
MOE_GROUPS = 8
EXPERTS_PER_GROUP = 8
N_EXPERTS = 64
D_EXPERT = 256
EPS = 1e-6

LANES = 128
T_PROMPT = BATCH * SEQ
T_SAMPLE = DEC_BATCH * DEC_SEQ
T_ALL = T_PROMPT + T_SAMPLE

COL_GA, COL_GB, COL_GC = 0, 1024, 2048
COL_V, COL_R, COL_Q, COL_K = 3072, 4096, 5120, 5632
COL_H, COL_CG, COL_BG, COL_U, COL_VV = 6144, 6656, 7168, 7680, 8192
Z_COLS = 8704
ACT_COLS = 2048
XG_COLS = D_MODEL + LANES
GRP_LANE = EXPERTS_PER_GROUP

TM_IN = 1024
TN_IN = Z_COLS // 4
TC_MIX = 256
SEQ_PER_BLK = 16
ROWS_S = SEQ_PER_BLK * DEC_SEQ
TM_OUT = 256
TM_RANK = 512
TM_ROW = 256
MOE_BLK = 256
N_BLK = T_ALL // MOE_BLK + MOE_GROUPS
N_SORTED = N_BLK * MOE_BLK
VMEM_LIMIT = 56 * 1024 * 1024


def _sigmoid(x):
    return 1.0 / (1.0 + jnp.exp(-x))


def _gelu_tanh(x):
    return 0.5 * x * (1.0 + jnp.tanh(0.7978845608028654 * (x + 0.044715 * (x * x * x))))


def _log_sigmoid(x):
    return jnp.minimum(x, 0.0) - jnp.log(1.0 + jnp.exp(-jnp.abs(x)))


def _rms(x, g):
    ms = jnp.mean(x * x, axis=-1, keepdims=True)
    return x * lax.rsqrt(ms + EPS) * g


def _split_bf16(x):
    hi = x.astype(BF16)
    lo = (x - hi.astype(F32)).astype(BF16)
    return hi, lo


def _dot(a, b):
    return jnp.dot(a, b, preferred_element_type=F32)


def _dot_nt(a, b):
    return lax.dot_general(a, b, (((1,), (1,)), ((), ())), preferred_element_type=F32)


def _layer_spec(arr, layer):
    nd = arr.ndim - 1
    return pl.BlockSpec((None,) + arr.shape[1:], lambda *g: (layer,) + (0,) * nd)


def _const_spec(arr):
    nd = arr.ndim
    return pl.BlockSpec(arr.shape, lambda *g: (0,) * nd)


def _two_part_specs(tile, xa, xb):
    n_a = xa.shape[0] // tile
    return n_a, [
        pl.BlockSpec((tile, xa.shape[1]), lambda i, *_: (jnp.minimum(i, n_a - 1), 0)),
        pl.BlockSpec((tile, xb.shape[1]), lambda i, *_: (jnp.maximum(i - n_a, 0), 0)),
    ]


def _inproj_kernel(xa_ref, xb_ref, g_ref, w_ref, wa_ref, z_ref, a_ref, xn_ref, *, n_a):
    @pl.when(pl.program_id(1) == 0)
    def _():
        x = jnp.where(pl.program_id(0) < n_a, xa_ref[...], xb_ref[...])
        xn = _rms(x, g_ref[...]).astype(BF16)
        xn_ref[...] = xn
        a_ref[...] = _dot_nt(xn, wa_ref[...]).astype(BF16)

    z_ref[...] = _dot_nt(xn_ref[...], w_ref[...]).astype(BF16)


def _inproj(xa, xb, g, w, wa, layer):
    m = T_ALL
    n_a, x_specs = _two_part_specs(TM_IN, xa, xb)
    return pl.pallas_call(
        functools.partial(_inproj_kernel, n_a=n_a),
        grid=(m // TM_IN, Z_COLS // TN_IN),
        in_specs=x_specs + [
            _layer_spec(g, layer),
            pl.BlockSpec((None, TN_IN, D_MODEL), lambda i, j: (layer, j, 0)),
            _layer_spec(wa, layer),
        ],
        out_specs=[
            pl.BlockSpec((TM_IN, TN_IN), lambda i, j: (i, j)),
            pl.BlockSpec((TM_IN, LANES), lambda i, j: (i, 0)),
        ],
        out_shape=[
            jax.ShapeDtypeStruct((m, Z_COLS), BF16),
            jax.ShapeDtypeStruct((m, LANES), BF16),
        ],
        scratch_shapes=[pltpu.VMEM((TM_IN, D_MODEL), BF16)],
        compiler_params=pltpu.CompilerParams(
            dimension_semantics=("arbitrary", "arbitrary"), vmem_limit_bytes=VMEM_LIMIT),
        name="inproj",
    )(xa, xb, g, w, wa)


def _gla_log_decay(alr_ref, a2_ref, ab_ref):
    la = _log_sigmoid(_dot(alr_ref[...], a2_ref[...]) + ab_ref[...]) * (1.0 / GLA_TAU)
    return _split_bf16(la)


def _masked_sum(mask, la_hi, la_lo):
    m = jnp.where(mask, 1.0, 0.0).astype(BF16)
    return _dot(m, la_hi) + _dot(m, la_lo)


def _gla_decay_prefix(alr_ref, a2_ref, ab_ref, tril_mask):
    la_hi, la_lo = _gla_log_decay(alr_ref, a2_ref, ab_ref)
    return _masked_sum(tril_mask, la_hi, la_lo)


def _gla_decay_terms(alr_ref, a2_ref, ab_ref, tril_mask, same_mask):
    la_hi, la_lo = _gla_log_decay(alr_ref, a2_ref, ab_ref)
    return (la_hi, la_lo, _masked_sum(tril_mask, la_hi, la_lo),
            _masked_sum(same_mask, la_hi, la_lo))


def _gla_out_gate(o, g_ref, r):
    return _rms(o, g_ref[...]) * (r * _sigmoid(r))


def _chunk_mlp(u, vv, cmg_ref, ws_ref, cmb_ref, n_chunks):
    ug = _gelu_tanh(u)
    gv = _gelu_tanh(vv)
    outs, vgs = [], []
    for g in range(CM_GROUPS):
        sl = slice(g * CM_GCH, (g + 1) * CM_GCH)
        vg = _rms(gv[:, sl], cmg_ref[:, sl])
        vgs.append(vg)
        vgb = vg.astype(BF16)
        rows = []
        for j in range(n_chunks):
            rs = slice(j * CM_CHUNK, (j + 1) * CM_CHUNK)
            rows.append(_dot(ws_ref[g], vgb[rs]) + cmb_ref[:, sl])
        s = rows[0] if n_chunks == 1 else jnp.concatenate(rows, axis=0)
        outs.append(ug[:, sl] * s)
    return jnp.concatenate(outs, axis=1), jnp.concatenate(vgs, axis=1)


def _mix_prompt_kernel(v_ref, r_ref, qk_ref, hcg_ref, bgu_ref, vv_ref, alr_ref,
                       a2_ref, ab_ref, cw_ref, gng_ref, cmg_ref, ws_ref, cmb_ref,
                       acts_ref, nconv_ref, ngla_ref, st_ref, carry_ref):
    tc = TC_MIX

    @pl.when(pl.program_id(1) == 0)
    def _():
        st_ref[...] = jnp.zeros_like(st_ref)
        carry_ref[...] = jnp.zeros_like(carry_ref)

    row = lax.broadcasted_iota(jnp.int32, (tc, 1), 0)
    h = hcg_ref[:, :CONV_CH].astype(F32)
    cg = hcg_ref[:, CONV_CH:].astype(F32)
    bg = bgu_ref[:, :CONV_CH].astype(F32)
    cin = cg * h
    c0 = carry_ref[0:1, :]
    c1 = carry_ref[1:2, :]
    x1 = jnp.where(row >= 1, pltpu.roll(cin, 1, 0), c1)
    x2 = jnp.where(row >= 2, pltpu.roll(cin, 2, 0), jnp.where(row == 1, c1, c0))
    conv = x2 * cw_ref[0:1, :] + x1 * cw_ref[1:2, :] + cin * cw_ref[2:3, :]
    acts_ref[:, 0:CONV_CH] = (bg * conv).astype(BF16)
    carry_ref[0:2, :] = cin[tc - 2:tc, :]
    nconv_ref[0] = cin[tc - 2:tc, :]

    rr = lax.broadcasted_iota(jnp.int32, (tc, tc), 0)
    cc = lax.broadcasted_iota(jnp.int32, (tc, tc), 1)
    same = (rr >> 6) == (cc >> 6)
    tril = same & (cc <= rr)
    b = _gla_decay_prefix(alr_ref, a2_ref, ab_ref, tril)
    n_chunks = tc // GLA_CHUNK
    b_last = [b[(c + 1) * GLA_CHUNK - 1:(c + 1) * GLA_CHUNK, :] for c in range(n_chunks)]
    bl = jnp.concatenate([jnp.broadcast_to(r_, (GLA_CHUNK, GLA_QK)) for r_ in b_last], axis=0)
    q = qk_ref[:, :GLA_QK].astype(F32) * (GLA_DK ** -0.5)
    k = qk_ref[:, GLA_QK:].astype(F32)
    q_t = (q * jnp.exp(b)).astype(BF16)
    k_t = (k * jnp.exp(-b)).astype(BF16)
    k_end = (k * jnp.exp(bl - b)).astype(BF16)
    lane_chunk = lax.broadcasted_iota(jnp.int32, (GLA_DV, tc), 1) >> 6
    for hd in range(GLA_HEADS):
        ks = slice(hd * GLA_DK, (hd + 1) * GLA_DK)
        vs = slice(hd * GLA_DV, (hd + 1) * GLA_DV)
        qh = q_t[:, ks]
        vh = v_ref[:, vs]
        att = jnp.where(tril, _dot_nt(qh, k_t[:, ks]), 0.0).astype(BF16)
        o_intra = _dot(att, vh)
        v_tr = vh.astype(F32).T
        o_rows = []
        for c in range(n_chunks):
            rs = slice(c * GLA_CHUNK, (c + 1) * GLA_CHUNK)
            st_old = st_ref[hd]
            o_rows.append(o_intra[rs] + _dot_nt(qh[rs], st_old.astype(BF16)))
            v_c = jnp.where(lane_chunk == c, v_tr, 0.0).astype(BF16)
            st_ref[hd] = jnp.exp(b_last[c][:, ks]) * st_old + _dot(v_c, k_end[:, ks])
        o = jnp.concatenate(o_rows, axis=0)
        r = r_ref[:, vs].astype(F32)
        acts_ref[:, CONV_CH + hd * GLA_DV:CONV_CH + (hd + 1) * GLA_DV] = (
            _gla_out_gate(o, gng_ref, r).astype(BF16))

    @pl.when(pl.program_id(1) == pl.num_programs(1) - 1)
    def _():
        for hd in range(GLA_HEADS):
            ngla_ref[0, hd] = st_ref[hd].T

    us, _ = _chunk_mlp(bgu_ref[:, CONV_CH:].astype(F32), vv_ref[...].astype(F32),
                       cmg_ref, ws_ref, cmb_ref, tc // CM_CHUNK)
    acts_ref[:, CONV_CH + GLA_V:] = us.astype(BF16)


def _z_specs(rows, row_map):
    def spec(width, col):
        blk = col // width
        return pl.BlockSpec((rows, width), lambda *g: (row_map(*g), blk))
    return [spec(1024, COL_V), spec(1024, COL_R), spec(1024, COL_Q), spec(1024, COL_H),
            spec(1024, COL_BG), spec(512, COL_VV)]


def _mix_prompt(z, alr, a2, ab, cw, gng, cmg, ws, cmb, layer):
    nt = SEQ // TC_MIX
    row_map = lambda b, c: b * nt + c
    small = (a2, ab, cw, gng, cmg, ws, cmb)
    in_specs = _z_specs(TC_MIX, row_map) + [
        pl.BlockSpec((TC_MIX, LANES), lambda b, c: (row_map(b, c), 0)),
    ] + [_layer_spec(a, layer) for a in small]
    return pl.pallas_call(
        _mix_prompt_kernel,
        grid=(BATCH, nt),
        in_specs=in_specs,
        out_specs=[
            pl.BlockSpec((TC_MIX, ACT_COLS), lambda b, c: (row_map(b, c), 0)),
            pl.BlockSpec((1, CONV_K - 1, CONV_CH), lambda b, c: (b, 0, 0)),
            pl.BlockSpec((1, GLA_HEADS, GLA_DK, GLA_DV), lambda b, c: (b, 0, 0, 0)),
        ],
        out_shape=[
            jax.ShapeDtypeStruct((T_PROMPT, ACT_COLS), BF16),
            jax.ShapeDtypeStruct((BATCH, CONV_K - 1, CONV_CH), F32),
            jax.ShapeDtypeStruct((BATCH, GLA_HEADS, GLA_DK, GLA_DV), F32),
        ],
        scratch_shapes=[pltpu.VMEM((GLA_HEADS, GLA_DV, GLA_DK), F32),
                        pltpu.VMEM((8, CONV_CH), F32)],
        compiler_params=pltpu.CompilerParams(
            dimension_semantics=("arbitrary", "arbitrary"), vmem_limit_bytes=VMEM_LIMIT),
        name="mix_prompt",
    )(z, z, z, z, z, z, alr, *small)


def _mix_sample_body(v_ref, r_ref, qk_ref, hcg_ref, bgu_ref, vv_ref, alr_ref,
                       a2_ref, ab_ref, cw_ref, gng_ref, cmg_ref, ws_ref, cmb_ref,
                       p1_ref, p2_ref, s0_ref,
                       acts_ref, cin_ref, ns_ref, vrow_ref):
    n = ROWS_S

    pos = lax.broadcasted_iota(jnp.int32, (n, 1), 0) & (DEC_SEQ - 1)
    h = hcg_ref[:, :CONV_CH].astype(F32)
    cg = hcg_ref[:, CONV_CH:].astype(F32)
    bg = bgu_ref[:, :CONV_CH].astype(F32)
    cin = cg * h
    x1 = jnp.where(pos >= 1, pltpu.roll(cin, 1, 0), p1_ref[...])
    x2 = jnp.where(pos >= 2, pltpu.roll(cin, 2, 0), p2_ref[...])
    conv = x2 * cw_ref[0:1, :] + x1 * cw_ref[1:2, :] + cin * cw_ref[2:3, :]
    acts_ref[:, 0:CONV_CH] = (bg * conv).astype(BF16)
    cin_ref[...] = cin

    rr = lax.broadcasted_iota(jnp.int32, (n, n), 0)
    cc = lax.broadcasted_iota(jnp.int32, (n, n), 1)
    same = (rr >> 3) == (cc >> 3)
    tril = same & (cc <= rr)
    la_hi, la_lo, b, bl = _gla_decay_terms(alr_ref, a2_ref, ab_ref, tril, same)
    q = qk_ref[:, :GLA_QK].astype(F32) * (GLA_DK ** -0.5)
    k = qk_ref[:, GLA_QK:].astype(F32)
    q_t = (q * jnp.exp(b)).astype(BF16)
    k_t = (k * jnp.exp(-b)).astype(BF16)
    k_end = k * jnp.exp(bl - b)
    la_hi = la_hi.astype(F32)
    la_lo = la_lo.astype(F32)
    row_seq = lax.broadcasted_iota(jnp.int32, (n, GLA_DK), 0) >> 3
    seq3 = lax.broadcasted_iota(jnp.int32, (SEQ_PER_BLK, GLA_DK, n), 0)
    lane_seq3 = lax.broadcasted_iota(jnp.int32, (SEQ_PER_BLK, GLA_DK, n), 2) >> 3
    mask3 = seq3 == lane_seq3
    ones = jnp.ones((n, GLA_DV), BF16)
    big = SEQ_PER_BLK * GLA_DK

    def per_seq(x_tr):
        x3 = jnp.where(mask3, x_tr[None, :, :], 0.0)
        return x3.reshape(big, n).astype(BF16)

    for hd in range(GLA_HEADS):
        ks = slice(hd * GLA_DK, (hd + 1) * GLA_DK)
        vs = slice(hd * GLA_DV, (hd + 1) * GLA_DV)
        qh = q_t[:, ks]
        vh = v_ref[:, vs]
        att = jnp.where(tril, _dot_nt(qh, k_t[:, ks]), 0.0).astype(BF16)
        o_intra = _dot(att, vh)
        s_old = s0_ref[:, hd].reshape(big, GLA_DV)
        zero = jnp.zeros_like(qh)
        q_big = jnp.concatenate(
            [jnp.where(row_seq == j, qh, zero) for j in range(SEQ_PER_BLK)], axis=1)
        o = o_intra + _dot(q_big, s_old.astype(BF16))
        dlog = _dot(per_seq(la_hi[:, ks].T), ones) + _dot(per_seq(la_lo[:, ks].T), ones)
        upd = _dot(per_seq(k_end[:, ks].T), vh)
        s_new = jnp.exp(dlog) * s_old + upd
        ns_ref[:, hd] = s_new.reshape(SEQ_PER_BLK, GLA_DK, GLA_DV)
        r = r_ref[:, vs].astype(F32)
        acts_ref[:, CONV_CH + hd * GLA_DV:CONV_CH + (hd + 1) * GLA_DV] = (
            _gla_out_gate(o, gng_ref, r).astype(BF16))

    us, vg = _chunk_mlp(bgu_ref[:, CONV_CH:].astype(F32), vv_ref[...].astype(F32),
                        cmg_ref, ws_ref, cmb_ref, 1)
    acts_ref[:, CONV_CH + GLA_V:] = us.astype(BF16)
    vrow_ref[...] = vg


N_MIX_S_IN = 17


def _mix_sample_kernel(*refs, layer):
    if layer == 0:
        @pl.when(pl.program_id(0) == 0)
        def _():
            _mix_sample_body(*refs)

        @pl.when(pl.program_id(0) > 0)
        def _():
            ns_ref = refs[N_MIX_S_IN + 2]
            ns_ref[...] = jnp.zeros_like(ns_ref)
    else:
        _mix_sample_body(*refs[:N_MIX_S_IN], *refs[N_MIX_S_IN + 1:])


def _mix_sample(z, alr, a2, ab, cw, gng, cmg, ws, cmb, p1, p2, s0, ns_all, layer):
    row0 = T_PROMPT // ROWS_S
    n_i = DEC_BATCH // SEQ_PER_BLK
    n_pass = DEPTH if layer == 0 else 1
    blk = lambda p, i: jnp.where(p == 0, i, n_i - 1)
    row_map = lambda p, i: row0 + blk(p, i)
    slot = lambda p, i: (layer + p, i, 0, 0, 0)
    small = (a2, ab, cw, gng, cmg, ws, cmb)
    state_blk = (None, SEQ_PER_BLK, GLA_HEADS, GLA_DK, GLA_DV)
    in_specs = _z_specs(ROWS_S, row_map) + [
        pl.BlockSpec((ROWS_S, LANES), lambda p, i: (row_map(p, i), 0)),
    ] + [_layer_spec(a, layer) for a in small] + [
        pl.BlockSpec((ROWS_S, CONV_CH), lambda p, i: (blk(p, i), 0)),
        pl.BlockSpec((ROWS_S, CONV_CH), lambda p, i: (blk(p, i), 0)),
        pl.BlockSpec(state_blk, lambda p, i: (layer, blk(p, i), 0, 0, 0)),
    ]
    args = (z, z, z, z, z, z, alr, *small, p1, p2, s0)
    assert len(args) == N_MIX_S_IN
    aliases = {}
    if layer > 0:
        in_specs.append(pl.BlockSpec(memory_space=pl.ANY))
        args += (ns_all,)
        aliases = {N_MIX_S_IN: 2}
    return pl.pallas_call(
        functools.partial(_mix_sample_kernel, layer=layer),
        grid=(n_pass, n_i),
        in_specs=in_specs,
        out_specs=[
            pl.BlockSpec((ROWS_S, ACT_COLS), lambda p, i: (blk(p, i), 0)),
            pl.BlockSpec((ROWS_S, CONV_CH), lambda p, i: (blk(p, i), 0)),
            pl.BlockSpec(state_blk, slot),
            pl.BlockSpec((ROWS_S, CM_CH), lambda p, i: (blk(p, i), 0)),
        ],
        out_shape=[
            jax.ShapeDtypeStruct((T_SAMPLE, ACT_COLS), BF16),
            jax.ShapeDtypeStruct((T_SAMPLE, CONV_CH), F32),
            jax.ShapeDtypeStruct((DEPTH, DEC_BATCH, GLA_HEADS, GLA_DK, GLA_DV), F32),
            jax.ShapeDtypeStruct((T_SAMPLE, CM_CH), F32),
        ],
        input_output_aliases=aliases,
        compiler_params=pltpu.CompilerParams(
            dimension_semantics=("arbitrary", "arbitrary"), vmem_limit_bytes=VMEM_LIMIT),
        name="mix_sample",
    )(*args)


def _outproj_kernel(actp_ref, acts_ref, gates_ref, xa_ref, xb_ref, pa_ref, pb_ref, pc_ref, wo_ref,
                    n2_ref, wrh_ref, wrl_ref, rb_ref, xo_ref, xg_ref, *, n_xa):
    i = pl.program_id(0)
    acts = jnp.where(i < T_PROMPT // TM_OUT, actp_ref[...], acts_ref[...])
    x_in = jnp.where(i < n_xa, xa_ref[...], xb_ref[...])
    ya = _dot(acts[:, :CONV_CH], pa_ref[...])
    yb = _dot(acts[:, CONV_CH:CONV_CH + GLA_V], pb_ref[...])
    yc = _dot(acts[:, CONV_CH + GLA_V:], pc_ref[...])
    ga = _sigmoid(gates_ref[:, COL_GA:COL_GA + D_MODEL].astype(F32))
    gb = _sigmoid(gates_ref[:, COL_GB:COL_GB + D_MODEL].astype(F32))
    gc = _sigmoid(gates_ref[:, COL_GC:COL_GC + D_MODEL].astype(F32))
    mix = ga * ya + gb * yb + gc * yc
    x = x_in + _dot(mix.astype(BF16), wo_ref[...])
    xo_ref[...] = x
    xn = _rms(x, n2_ref[...])
    xg_ref[:, :D_MODEL] = xn

    hi, lo = _split_bf16(xn)
    logits = (_dot(hi, wrh_ref[...]) + _dot(lo, wrh_ref[...]) + _dot(hi, wrl_ref[...])
              + rb_ref[...])
    lane_i = lax.broadcasted_iota(jnp.int32, logits.shape, 1)
    lane = lane_i.astype(F32)
    neg = jnp.float32(-jnp.inf)
    is_g = lane_i < MOE_GROUPS
    lg = jnp.where(is_g, logits, neg)
    gmax = jnp.max(lg, axis=-1, keepdims=True)
    grp = jnp.min(jnp.where(lg == gmax, lane, 1e9), axis=-1, keepdims=True)
    p_grp = 1.0 / jnp.sum(jnp.where(is_g, jnp.exp(lg - gmax), 0.0), axis=-1, keepdims=True)
    first = MOE_GROUPS + EXPERTS_PER_GROUP * grp
    in_grp = (lane >= first) & (lane < first + EXPERTS_PER_GROUP)
    le = jnp.where(in_grp, logits, neg)
    v1 = jnp.max(le, axis=-1, keepdims=True)
    i1 = jnp.min(jnp.where(le == v1, lane, 1e9), axis=-1, keepdims=True)
    le2 = jnp.where(lane == i1, neg, le)
    v2 = jnp.max(le2, axis=-1, keepdims=True)
    i2 = jnp.min(jnp.where(le2 == v2, lane, 1e9), axis=-1, keepdims=True)
    t = jnp.exp(v2 - v1)
    g1 = p_grp / (1.0 + t)
    g2 = p_grp * t / (1.0 + t)
    xg_ref[:, D_MODEL:] = jnp.where(
        lane == i1 - first, g1,
        jnp.where(lane == i2 - first, g2, jnp.where(lane_i == GRP_LANE, grp, 0.0)))


def _outproj(acts_p, acts_s, z, xa, xb, pa, pb, pc, wo, n2, wrh, wrl, rb, layer):
    m = T_ALL
    row = lambda i: (i, 0)
    _, act_specs = _two_part_specs(TM_OUT, acts_p, acts_s)
    n_xa, x_specs = _two_part_specs(TM_OUT, xa, xb)
    weights = (pa, pb, pc, wo, n2, wrh, wrl, rb)
    return pl.pallas_call(
        functools.partial(_outproj_kernel, n_xa=n_xa),
        grid=(m // TM_OUT,),
        in_specs=act_specs + [pl.BlockSpec((TM_OUT, 3 * D_MODEL), row)] + x_specs
        + [_layer_spec(w, layer) for w in weights],
        out_specs=[
            pl.BlockSpec((TM_OUT, D_MODEL), row),
            pl.BlockSpec((TM_OUT, XG_COLS), row),
        ],
        out_shape=[
            jax.ShapeDtypeStruct((m, D_MODEL), F32),
            jax.ShapeDtypeStruct((m, XG_COLS), F32),
        ],
        compiler_params=pltpu.CompilerParams(
            dimension_semantics=("arbitrary",), vmem_limit_bytes=VMEM_LIMIT),
        name="outproj",
    )(acts_p, acts_s, z, xa, xb, *weights)


def _rank_kernel(r_ref, dest_ref, meta_ref, cnt_ref, base_ref):
    p = pl.program_id(0)
    i = pl.program_id(1)
    tm = TM_RANK
    lane_i = lax.broadcasted_iota(jnp.int32, (tm, LANES), 1)
    onehot = jnp.where(lane_i.astype(F32) == r_ref[:, GRP_LANE:GRP_LANE + 1], 1.0, 0.0)
    csum = jnp.sum(onehot, axis=0, keepdims=True)

    @pl.when((p == 0) & (i == 0))
    def _():
        cnt_ref[...] = jnp.zeros_like(cnt_ref)

    @pl.when(p == 0)
    def _():
        cnt_ref[...] += csum

    @pl.when((p == 1) & (i == 0))
    def _():
        blocks = jnp.floor((cnt_ref[...] + (MOE_BLK - 1)) * (1.0 / MOE_BLK))
        rr = lax.broadcasted_iota(jnp.int32, (LANES, LANES), 0)
        cc = lax.broadcasted_iota(jnp.int32, (LANES, LANES), 1)
        earlier = jnp.where(rr < cc, 1.0, 0.0).astype(BF16)
        start = _dot(jnp.broadcast_to(blocks, (8, LANES)).astype(BF16), earlier)[0:1]
        base_ref[...] = start * MOE_BLK
        end = start + blocks
        lane1_i = lax.broadcasted_iota(jnp.int32, (1, LANES), 1)
        lane1 = lane1_i.astype(F32)
        grp_of_blk = jnp.zeros((1, LANES), F32)
        for g in range(MOE_GROUPS):
            end_g = jnp.sum(jnp.where(lane1_i == g, end, 0.0), axis=-1, keepdims=True)
            grp_of_blk += jnp.where(lane1 >= end_g, 1.0, 0.0)
        grp_of_blk = jnp.minimum(grp_of_blk, MOE_GROUPS - 1)
        n_used = jnp.sum(blocks, axis=-1, keepdims=True)
        row = lax.broadcasted_iota(jnp.int32, (8, LANES), 0)
        meta_ref[...] = jnp.where(
            row == 0, grp_of_blk,
            jnp.where(row == 1, n_used, jnp.where(row == 2, end, jnp.where(row == 3, blocks, 0.0)))
        ).astype(jnp.int32)
        cnt_ref[...] = jnp.zeros_like(cnt_ref)

    @pl.when(p == 1)
    def _():
        rr = lax.broadcasted_iota(jnp.int32, (tm, tm), 0)
        cc = lax.broadcasted_iota(jnp.int32, (tm, tm), 1)
        before = jnp.where(cc < rr, 1.0, 0.0).astype(BF16)
        prefix = _dot(before, onehot.astype(BF16))
        carry = cnt_ref[...]
        d = jnp.sum(onehot * (prefix + carry + base_ref[...]), axis=-1, keepdims=True)
        dest_ref[...] = jnp.where(lane_i == 0, d, 0.0).astype(jnp.int32)
        cnt_ref[...] = carry + csum


def _rank(xg):
    m = xg.shape[0]
    gate_blk = D_MODEL // LANES
    return pl.pallas_call(
        _rank_kernel,
        grid=(2, m // TM_RANK),
        in_specs=[pl.BlockSpec((TM_RANK, LANES), lambda p, i: (i, gate_blk))],
        out_specs=[pl.BlockSpec((TM_RANK, LANES), lambda p, i: (i * p, 0)),
                   pl.BlockSpec((8, LANES), lambda p, i: (0, 0))],
        out_shape=[jax.ShapeDtypeStruct((m, LANES), jnp.int32),
                   jax.ShapeDtypeStruct((8, LANES), jnp.int32)],
        scratch_shapes=[pltpu.VMEM((1, LANES), F32), pltpu.VMEM((1, LANES), F32)],
        compiler_params=pltpu.CompilerParams(dimension_semantics=("arbitrary", "arbitrary")),
        name="rank",
    )(xg)


def _scatter_kernel(nu_ref, end_ref, nb_ref, d_ref, xg_ref, xs_ref, zbuf, sem):
    def zero_block(b):
        return pltpu.make_async_copy(zbuf, xs_ref.at[pl.ds(b * MOE_BLK, MOE_BLK)], sem)

    def each_unfilled_block(fn):
        for g in range(MOE_GROUPS):
            @pl.when(nb_ref[g] > 0)
            def _():
                fn(zero_block(end_ref[g] - 1))
        for b in range(T_ALL // MOE_BLK, N_BLK):
            @pl.when(b >= nu_ref[0])
            def _():
                fn(zero_block(b))

    @pl.when(pl.program_id(0) == 0)
    def _():
        zbuf[...] = jnp.zeros_like(zbuf)
        each_unfilled_block(lambda c: c.start())
        each_unfilled_block(lambda c: c.wait())

    def start(r, carry):
        pltpu.make_async_copy(
            xg_ref.at[pl.ds(r, 1)], xs_ref.at[pl.ds(d_ref[0, 0, r], 1)], sem).start()
        return carry

    lax.fori_loop(0, TM_ROW, start, 0, unroll=8)
    pltpu.make_async_copy(xg_ref, xs_ref.at[pl.ds(0, TM_ROW)], sem).wait()


def _scatter(dest, xg, n_used, grp_end, grp_blocks):
    m = xg.shape[0]
    nt = m // TM_ROW
    grid_spec = pltpu.PrefetchScalarGridSpec(
        num_scalar_prefetch=3,
        grid=(nt,),
        in_specs=[
            pl.BlockSpec((1, 1, TM_ROW), lambda i, *_: (i, 0, 0), memory_space=pltpu.SMEM),
            pl.BlockSpec((TM_ROW, XG_COLS), lambda i, *_: (i, 0)),
        ],
        out_specs=pl.BlockSpec(memory_space=pl.ANY),
        scratch_shapes=[pltpu.VMEM((MOE_BLK, XG_COLS), F32), pltpu.SemaphoreType.DMA(())],
    )
    return pl.pallas_call(
        _scatter_kernel,
        grid_spec=grid_spec,
        out_shape=jax.ShapeDtypeStruct((N_SORTED, XG_COLS), F32),
        compiler_params=pltpu.CompilerParams(dimension_semantics=("arbitrary",)),
        name="scatter",
    )(n_used, grp_end, grp_blocks, dest.reshape(nt, 1, TM_ROW), xg)


def _ffn_kernel(bg_ref, nu_ref, xs_ref, w1_ref, w3_ref, w2_ref, y_ref):
    del bg_ref
    b = pl.program_id(0)

    @pl.when(b < nu_ref[0])
    def _():
        x = xs_ref[:, :D_MODEL].astype(BF16)
        gates = xs_ref[:, D_MODEL:]
        hs = []
        for e in range(EXPERTS_PER_GROUP):
            h1 = _dot(x, w1_ref[e])
            h3 = _dot(x, w3_ref[e])
            ge = gates[:, e:e + 1]
            hs.append(jnp.where(ge > 0.0, h1 * _sigmoid(h1) * h3 * ge, 0.0).astype(BF16))
        hcat = jnp.concatenate(hs, axis=1)
        y_ref[...] = _dot(hcat, w2_ref[...].reshape(EXPERTS_PER_GROUP * D_EXPERT, D_MODEL))

    @pl.when(b >= nu_ref[0])
    def _():
        y_ref[...] = jnp.zeros_like(y_ref)


def _ffn(blk_group, n_used, xs, w1, w3, w2, layer):
    wmap = lambda b, bg, nu: (layer * MOE_GROUPS + bg[b], 0, 0, 0)
    grid_spec = pltpu.PrefetchScalarGridSpec(
        num_scalar_prefetch=2,
        grid=(N_BLK,),
        in_specs=[
            pl.BlockSpec((MOE_BLK, XG_COLS), lambda b, bg, nu: (b, 0)),
            pl.BlockSpec((None, EXPERTS_PER_GROUP, D_MODEL, D_EXPERT), wmap),
            pl.BlockSpec((None, EXPERTS_PER_GROUP, D_MODEL, D_EXPERT), wmap),
            pl.BlockSpec((None, EXPERTS_PER_GROUP, D_EXPERT, D_MODEL), wmap),
        ],
        out_specs=pl.BlockSpec((MOE_BLK, D_MODEL), lambda b, bg, nu: (b, 0)),
    )
    return pl.pallas_call(
        _ffn_kernel,
        grid_spec=grid_spec,
        out_shape=jax.ShapeDtypeStruct((N_SORTED, D_MODEL), F32),
        compiler_params=pltpu.CompilerParams(
            dimension_semantics=("arbitrary",), vmem_limit_bytes=VMEM_LIMIT),
        name="ffn",
    )(blk_group, n_used, xs, w1, w3, w2)


def _combine_kernel(d_ref, dn_ref, ys_hbm, x_ref, g_ref, *refs, final):
    outs, (buf, sem) = refs[:-2], refs[-2:]
    i = pl.program_id(0)
    n = pl.num_programs(0)
    slot = i % 2

    def gather(ids, s):
        def body(r, carry):
            pltpu.make_async_copy(
                ys_hbm.at[pl.ds(ids[0, 0, r], 1)], buf.at[s, pl.ds(r, 1)], sem.at[s]).start()
            return carry
        lax.fori_loop(0, TM_ROW, body, 0, unroll=8)

    @pl.when(i == 0)
    def _():
        gather(d_ref, 0)

    @pl.when(i + 1 < n)
    def _():
        gather(dn_ref, 1 - slot)

    pltpu.make_async_copy(ys_hbm.at[pl.ds(0, TM_ROW)], buf.at[slot], sem.at[slot]).wait()

    y = x_ref[...] + buf[slot]
    if not final:
        outs[0][...] = y
    else:
        y = _rms(y, g_ref[...])

        @pl.when(i < T_PROMPT // TM_ROW)
        def _():
            outs[0][...] = y

        @pl.when(i >= T_PROMPT // TM_ROW)
        def _():
            outs[1][...] = y


def _combine(dest, ys, x, g, final):
    m = x.shape[0]
    nt = m // TM_ROW
    n_p = T_PROMPT // TM_ROW
    ids = dest.reshape(nt, 1, TM_ROW)
    smem = functools.partial(pl.BlockSpec, (1, 1, TM_ROW), memory_space=pltpu.SMEM)
    tile = (TM_ROW, D_MODEL)
    if final:
        out_specs = [pl.BlockSpec(tile, lambda i: (jnp.minimum(i, n_p - 1), 0)),
                     pl.BlockSpec(tile, lambda i: (jnp.maximum(i - n_p, 0), 0))]
        out_shape = [jax.ShapeDtypeStruct((T_PROMPT, D_MODEL), F32),
                     jax.ShapeDtypeStruct((T_SAMPLE, D_MODEL), F32)]
    else:
        out_specs = [pl.BlockSpec(tile, lambda i: (i, 0))]
        out_shape = [jax.ShapeDtypeStruct((m, D_MODEL), F32)]
    return pl.pallas_call(
        functools.partial(_combine_kernel, final=final),
        grid=(nt,),
        in_specs=[
            smem(lambda i: (i, 0, 0)),
            smem(lambda i: (jnp.minimum(i + 1, nt - 1), 0, 0)),
            pl.BlockSpec(memory_space=pl.ANY),
            pl.BlockSpec(tile, lambda i: (i, 0)),
            _const_spec(g),
        ],
        out_specs=out_specs,
        out_shape=out_shape,
        scratch_shapes=[pltpu.VMEM((2, TM_ROW, D_MODEL), F32),
                        pltpu.SemaphoreType.DMA((2,))],
        compiler_params=pltpu.CompilerParams(
            dimension_semantics=("arbitrary",), vmem_limit_bytes=VMEM_LIMIT),
        name="combine",
    )(ids, ids, ys, x, g)


def _prep_weights(w_in, gla_a2, cm_ws, cm_b, router_group_w, router_group_b,
                  router_expert_w, router_expert_b):
    off = {}
    o = 0
    for name, n in (("h", 512), ("cg", 512), ("bg", 512), ("q", 512), ("k", 512), ("v", 1024),
                    ("r", 1024), ("alr", 16), ("u", 512), ("vv", 512), ("ga", 1024),
                    ("gb", 1024), ("gc", 1024)):
        off[name] = (o, n)
        o += n
    order = ("ga", "gb", "gc", "v", "r", "q", "k", "h", "cg", "bg", "u", "vv")
    w_t = jnp.swapaxes(w_in, 1, 2)
    w_z = jnp.concatenate([w_t[:, off[n][0]:off[n][0] + off[n][1]] for n in order],
                          axis=1).astype(BF16)
    a0 = off["alr"][0]
    w_alr = jnp.pad(w_t[:, a0:a0 + GLA_LOWRANK],
                    ((0, 0), (0, LANES - GLA_LOWRANK), (0, 0))).astype(BF16)
    a2 = jnp.pad(gla_a2, ((0, 0), (0, LANES - GLA_LOWRANK), (0, 0))).astype(BF16)
    ws_p = jnp.tril(cm_ws).astype(BF16)
    small = jnp.tril(cm_ws[:, :, :DEC_SEQ, :DEC_SEQ])
    eye = jnp.eye(SEQ_PER_BLK, dtype=F32)
    ws_s = jnp.einsum("ij,lgab->lgiajb", eye, small).reshape(
        DEPTH, CM_GROUPS, ROWS_S, ROWS_S).astype(BF16)
    cmb_p = jnp.broadcast_to(jnp.transpose(cm_b, (0, 2, 1))[:, :, :, None],
                             (DEPTH, CM_CHUNK, CM_GROUPS, CM_GCH)).reshape(DEPTH, CM_CHUNK, CM_CH)
    cmb_s = jnp.tile(cmb_p[:, :DEC_SEQ], (1, SEQ_PER_BLK, 1))
    pad = LANES - MOE_GROUPS - N_EXPERTS
    w_r = jnp.pad(jnp.concatenate([router_group_w, router_expert_w], axis=-1),
                  ((0, 0), (0, 0), (0, pad)))
    w_r_hi = w_r.astype(BF16)
    w_r_lo = (w_r - w_r_hi.astype(F32)).astype(BF16)
    r_b = jnp.pad(jnp.concatenate([router_group_b, router_expert_b], axis=-1),
                  ((0, 0), (0, pad)))[:, None, :]
    return w_z, w_alr, a2, ws_p, ws_s, cmb_p, cmb_s, w_r_hi, w_r_lo, r_b


def kernel(x_prompt, x_sample, state_conv, state_gla, norm1_g, w_in, conv_w, gla_a2, gla_a_b,
           gla_norm_g, cm_norm_g, cm_ws, cm_b, proj_a, proj_b, proj_c, w_out, norm2_g,
           router_group_w, router_group_b, router_expert_w, router_expert_b,
           exp_w1, exp_w3, exp_w2, final_norm_g):
    (w_z, w_alr, a2, ws_p, ws_s, cmb_p, cmb_s, w_r_hi, w_r_lo, r_b) = _prep_weights(
        w_in, gla_a2, cm_ws, cm_b, router_group_w, router_group_b, router_expert_w,
        router_expert_b)
    pa, pb, pc, wo = (w.astype(BF16) for w in (proj_a, proj_b, proj_c, w_out))
    grouped = (DEPTH * MOE_GROUPS, EXPERTS_PER_GROUP)
    w1 = exp_w1.astype(BF16).reshape(grouped + (D_MODEL, D_EXPERT))
    w3 = exp_w3.astype(BF16).reshape(grouped + (D_MODEL, D_EXPERT))
    w2 = exp_w2.astype(BF16).reshape(grouped + (D_EXPERT, D_MODEL))
    n1 = norm1_g[:, None, :]
    n2 = norm2_g[:, None, :]
    ab = gla_a_b[:, None, :]
    gng = gla_norm_g[:, None, :]
    cmg = cm_norm_g.reshape(DEPTH, 1, CM_CH)
    fg = final_norm_g[None, :]
    xa = x_prompt.reshape(T_PROMPT, D_MODEL)
    xb = x_sample.reshape(T_SAMPLE, D_MODEL)
    gla_s = None
    conv_p, gla_p, conv_s, cmv_s = [], [], [], []
    for l in range(DEPTH):
        z, alr = _inproj(xa, xb, n1, w_z, w_alr, l)
        acts_p, nconv, ngla = _mix_prompt(z, alr, a2, ab, conv_w, gng, cmg, ws_p, cmb_p, l)
        sc = state_conv[l]
        p2 = jnp.pad(sc, ((0, 0), (0, DEC_SEQ - 2), (0, 0))).reshape(T_SAMPLE, CONV_CH)
        p1 = jnp.pad(sc[:, 1:2], ((0, 0), (0, DEC_SEQ - 1), (0, 0))).reshape(T_SAMPLE, CONV_CH)
        acts_s, cin_s, gla_s, vrows = _mix_sample(z, alr, a2, ab, conv_w, gng, cmg, ws_s, cmb_s,
                                                  p1, p2, state_gla, gla_s, l)
        conv_p.append(nconv)
        gla_p.append(ngla)
        conv_s.append(cin_s.reshape(DEC_BATCH, DEC_SEQ, CONV_CH)[:, DEC_SEQ - (CONV_K - 1):])
        cmv_s.append(vrows.reshape(DEC_BATCH, DEC_SEQ, CM_CH))

        x, xg = _outproj(acts_p, acts_s, z, xa, xb, pa, pb, pc, wo, n2, w_r_hi, w_r_lo, r_b, l)
        dest_rows, meta = _rank(xg)
        dest = dest_rows[:, 0]
        n_used = meta[1, :1]
        xs = _scatter(dest, xg, n_used, meta[2, :MOE_GROUPS], meta[3, :MOE_GROUPS])
        ys = _ffn(meta[0, :N_BLK], n_used, xs, w1, w3, w2, l)
        out = _combine(dest, ys, x, fg, l == DEPTH - 1)
        xa = xb = out[0]

    y_prompt = out[0].reshape(BATCH, SEQ, D_MODEL)
    y_sample = out[1].reshape(DEC_BATCH, DEC_SEQ, D_MODEL)
    return (y_prompt, y_sample, jnp.stack(conv_p), jnp.stack(gla_p), jnp.stack(conv_s),
            gla_s, jnp.stack(cmv_s))
```

```python
import functools

import jax
import jax.numpy as jnp
from jax import lax
from jax.experimental import pallas as pl
from jax.experimental.pallas import tpu as pltpu

F32 = jnp.float32
BF16 = jnp.bfloat16

D_MODEL = 1024
BATCH = 8
SEQ = 2048
DEPTH = 2
DEC_BATCH = 128
DEC_SEQ = 8
CONV_K = 3
CONV_CH = 512
GLA_HEADS = 4
GLA_DK = 128
GLA_DV = 256
GLA_QK = GLA_HEADS * GLA_DK
GLA_V = GLA_HEADS * GLA_DV
GLA_LOWRANK = 16
GLA_TAU = 16.0
GLA_CHUNK = 64
CM_GROUPS = 4
CM_CHUNK = 128
CM_GCH = 128
CM_CH = 512
MOE_GROUPS = 8
EXPERTS_PER_GROUP = 8
N_EXPERTS = 64
D_EXPERT = 256
EPS = 1e-6

LANES = 128
T_PROMPT = BATCH * SEQ
T_SAMPLE = DEC_BATCH * DEC_SEQ
T_ALL = T_PROMPT + T_SAMPLE

COL_GA, COL_GB, COL_GC = 0, 1024, 2048
COL_V, COL_R, COL_Q, COL_K = 3072, 4096, 5120, 5632
COL_H, COL_CG, COL_BG, COL_U, COL_VV = 6144, 6656, 7168, 7680, 8192
Z_COLS = 8704
ACT_COLS = 2048
XG_COLS = D_MODEL + LANES
GRP_LANE = EXPERTS_PER_GROUP

TM_IN = 1024
TN_IN = Z_COLS // 4
TC_MIX = 256
SEQ_PER_BLK = 16
ROWS_S = SEQ_PER_BLK * DEC_SEQ
TM_OUT = 256
TM_ROW = 256
TM_RANK = TM_ROW
MOE_BLK = 256
N_BLK = T_ALL // MOE_BLK + MOE_GROUPS
N_SORTED = N_BLK * MOE_BLK
VMEM_LIMIT = 56 * 1024 * 1024


def _sigmoid(x):
    return 1.0 / (1.0 + jnp.exp(-x))


def _gelu_tanh(x):
    return 0.5 * x * (1.0 + jnp.tanh(0.7978845608028654 * (x + 0.044715 * (x * x * x))))


def _log_sigmoid(x):
    return jnp.minimum(x, 0.0) - jnp.log(1.0 + jnp.exp(-jnp.abs(x)))


def _rms(x, g):
    ms = jnp.mean(x * x, axis=-1, keepdims=True)
    return x * lax.rsqrt(ms + EPS) * g


def _split_bf16(x):
    hi = x.astype(BF16)
    lo = (x - hi.astype(F32)).astype(BF16)
    return hi, lo


def _dot(a, b):
    return jnp.dot(a, b, preferred_element_type=F32)


def _dot_nt(a, b):
    return lax.dot_general(a, b, (((1,), (1,)), ((), ())), preferred_element_type=F32)


def _layer_spec(arr, layer):
    nd = arr.ndim - 1
    return pl.BlockSpec((None,) + arr.shape[1:], lambda *g: (layer,) + (0,) * nd)


def _const_spec(arr):
    nd = arr.ndim
    return pl.BlockSpec(arr.shape, lambda *g: (0,) * nd)


def _two_part_specs(tile, xa, xb):
    n_a = xa.shape[0] // tile
    return n_a, [
        pl.BlockSpec((tile, xa.shape[1]), lambda i, *_: (jnp.minimum(i, n_a - 1), 0)),
        pl.BlockSpec((tile, xb.shape[1]), lambda i, *_: (jnp.maximum(i - n_a, 0), 0)),
    ]


def _inproj_kernel(xa_ref, xb_ref, g_ref, w_ref, wa_ref, z_ref, a_ref, xn_ref, *, n_a):
    @pl.when(pl.program_id(1) == 0)
    def _():
        x = jnp.where(pl.program_id(0) < n_a, xa_ref[...], xb_ref[...])
        xn = _rms(x, g_ref[...]).astype(BF16)
        xn_ref[...] = xn
        a_ref[...] = _dot_nt(xn, wa_ref[...]).astype(BF16)

    z_ref[...] = _dot_nt(xn_ref[...], w_ref[...]).astype(BF16)


def _inproj(xa, xb, g, w, wa, layer):
    m = T_ALL
    n_a, x_specs = _two_part_specs(TM_IN, xa, xb)
    return pl.pallas_call(
        functools.partial(_inproj_kernel, n_a=n_a),
        grid=(m // TM_IN, Z_COLS // TN_IN),
        in_specs=x_specs + [
            _layer_spec(g, layer),
            pl.BlockSpec((None, TN_IN, D_MODEL), lambda i, j: (layer, j, 0)),
            _layer_spec(wa, layer),
        ],
        out_specs=[
            pl.BlockSpec((TM_IN, TN_IN), lambda i, j: (i, j)),
            pl.BlockSpec((TM_IN, LANES), lambda i, j: (i, 0)),
        ],
        out_shape=[
            jax.ShapeDtypeStruct((m, Z_COLS), BF16),
            jax.ShapeDtypeStruct((m, LANES), BF16),
        ],
        scratch_shapes=[pltpu.VMEM((TM_IN, D_MODEL), BF16)],
        compiler_params=pltpu.CompilerParams(
            dimension_semantics=("arbitrary", "arbitrary"), vmem_limit_bytes=VMEM_LIMIT),
        name="inproj",
    )(xa, xb, g, w, wa)


def _gla_log_decay(alr_ref, a2_ref, ab_ref):
    la = _log_sigmoid(_dot(alr_ref[...], a2_ref[...]) + ab_ref[...]) * (1.0 / GLA_TAU)
    return _split_bf16(la)


def _masked_sum(mask, la_hi, la_lo):
    m = jnp.where(mask, 1.0, 0.0).astype(BF16)
    return _dot(m, la_hi) + _dot(m, la_lo)


def _gla_decay_prefix(alr_ref, a2_ref, ab_ref, tril_mask):
    la_hi, la_lo = _gla_log_decay(alr_ref, a2_ref, ab_ref)
    return _masked_sum(tril_mask, la_hi, la_lo)


def _gla_decay_terms(alr_ref, a2_ref, ab_ref, tril_mask, same_mask):
    la_hi, la_lo = _gla_log_decay(alr_ref, a2_ref, ab_ref)
    return (la_hi, la_lo, _masked_sum(tril_mask, la_hi, la_lo),
            _masked_sum(same_mask, la_hi, la_lo))


def _gla_out_gate(o, g_ref, r):
    return _rms(o, g_ref[...]) * (r * _sigmoid(r))


def _chunk_mlp(u, vv, cmg_ref, ws_ref, cmb_ref, n_chunks):
    ug = _gelu_tanh(u)
    gv = _gelu_tanh(vv)
    outs, vgs = [], []
    for g in range(CM_GROUPS):
        sl = slice(g * CM_GCH, (g + 1) * CM_GCH)
        vg = _rms(gv[:, sl], cmg_ref[:, sl])
        vgs.append(vg)
        vgb = vg.astype(BF16)
        rows = []
        for j in range(n_chunks):
            rs = slice(j * CM_CHUNK, (j + 1) * CM_CHUNK)
            rows.append(_dot(ws_ref[g], vgb[rs]) + cmb_ref[:, sl])
        s = rows[0] if n_chunks == 1 else jnp.concatenate(rows, axis=0)
        outs.append(ug[:, sl] * s)
    return jnp.concatenate(outs, axis=1), jnp.concatenate(vgs, axis=1)


def _mix_prompt_kernel(v_ref, r_ref, qk_ref, hcg_ref, bgu_ref, vv_ref, alr_ref,
                       a2_ref, ab_ref, cw_ref, gng_ref, cmg_ref, ws_ref, cmb_ref,
                       acts_ref, nconv_ref, ngla_ref, st_ref, carry_ref):
    tc = TC_MIX

    @pl.when(pl.program_id(1) == 0)
    def _():
        st_ref[...] = jnp.zeros_like(st_ref)
        carry_ref[...] = jnp.zeros_like(carry_ref)

    row = lax.broadcasted_iota(jnp.int32, (tc, 1), 0)
    h = hcg_ref[:, :CONV_CH].astype(F32)
    cg = hcg_ref[:, CONV_CH:].astype(F32)
    bg = bgu_ref[:, :CONV_CH].astype(F32)
    cin = cg * h
    c0 = carry_ref[0:1, :]
    c1 = carry_ref[1:2, :]
    x1 = jnp.where(row >= 1, pltpu.roll(cin, 1, 0), c1)
    x2 = jnp.where(row >= 2, pltpu.roll(cin, 2, 0), jnp.where(row == 1, c1, c0))
    conv = x2 * cw_ref[0:1, :] + x1 * cw_ref[1:2, :] + cin * cw_ref[2:3, :]
    acts_ref[:, 0:CONV_CH] = (bg * conv).astype(BF16)
    carry_ref[0:2, :] = cin[tc - 2:tc, :]
    nconv_ref[0] = cin[tc - 2:tc, :]

    rr = lax.broadcasted_iota(jnp.int32, (tc, tc), 0)
    cc = lax.broadcasted_iota(jnp.int32, (tc, tc), 1)
    same = (rr >> 6) == (cc >> 6)
    tril = same & (cc <= rr)
    b = _gla_decay_prefix(alr_ref, a2_ref, ab_ref, tril)
    n_chunks = tc // GLA_CHUNK
    b_last = [b[(c + 1) * GLA_CHUNK - 1:(c + 1) * GLA_CHUNK, :] for c in range(n_chunks)]
    bl = jnp.concatenate([jnp.broadcast_to(r_, (GLA_CHUNK, GLA_QK)) for r_ in b_last], axis=0)
    q = qk_ref[:, :GLA_QK].astype(F32) * (GLA_DK ** -0.5)
    k = qk_ref[:, GLA_QK:].astype(F32)
    q_t = (q * jnp.exp(b)).astype(BF16)
    k_t = (k * jnp.exp(-b)).astype(BF16)
    k_end = (k * jnp.exp(bl - b)).astype(BF16)
    lane_chunk = lax.broadcasted_iota(jnp.int32, (GLA_DV, tc), 1) >> 6
    for hd in range(GLA_HEADS):
        ks = slice(hd * GLA_DK, (hd + 1) * GLA_DK)
        vs = slice(hd * GLA_DV, (hd + 1) * GLA_DV)
        qh = q_t[:, ks]
        vh = v_ref[:, vs]
        att = jnp.where(tril, _dot_nt(qh, k_t[:, ks]), 0.0).astype(BF16)
        o_intra = _dot(att, vh)
        v_tr = vh.astype(F32).T
        o_rows = []
        for c in range(n_chunks):
            rs = slice(c * GLA_CHUNK, (c + 1) * GLA_CHUNK)
            st_old = st_ref[hd]
            o_rows.append(o_intra[rs] + _dot_nt(qh[rs], st_old.astype(BF16)))
            v_c = jnp.where(lane_chunk == c, v_tr, 0.0).astype(BF16)
            st_ref[hd] = jnp.exp(b_last[c][:, ks]) * st_old + _dot(v_c, k_end[:, ks])
        o = jnp.concatenate(o_rows, axis=0)
        r = r_ref[:, vs].astype(F32)
        acts_ref[:, CONV_CH + hd * GLA_DV:CONV_CH + (hd + 1) * GLA_DV] = (
            _gla_out_gate(o, gng_ref, r).astype(BF16))

    @pl.when(pl.program_id(1) == pl.num_programs(1) - 1)
    def _():
        for hd in range(GLA_HEADS):
            ngla_ref[0, hd] = st_ref[hd].T

    us, _ = _chunk_mlp(bgu_ref[:, CONV_CH:].astype(F32), vv_ref[...].astype(F32),
                       cmg_ref, ws_ref, cmb_ref, tc // CM_CHUNK)
    acts_ref[:, CONV_CH + GLA_V:] = us.astype(BF16)


def _z_specs(rows, row_map):
    def spec(width, col):
        blk = col // width
        return pl.BlockSpec((rows, width), lambda *g: (row_map(*g), blk))
    return [spec(1024, COL_V), spec(1024, COL_R), spec(1024, COL_Q), spec(1024, COL_H),
            spec(1024, COL_BG), spec(512, COL_VV)]


def _mix_prompt(z, alr, a2, ab, cw, gng, cmg, ws, cmb, layer):
    nt = SEQ // TC_MIX
    row_map = lambda b, c: b * nt + c
    small = (a2, ab, cw, gng, cmg, ws, cmb)
    in_specs = _z_specs(TC_MIX, row_map) + [
        pl.BlockSpec((TC_MIX, LANES), lambda b, c: (row_map(b, c), 0)),
    ] + [_layer_spec(a, layer) for a in small]
    return pl.pallas_call(
        _mix_prompt_kernel,
        grid=(BATCH, nt),
        in_specs=in_specs,
        out_specs=[
            pl.BlockSpec((TC_MIX, ACT_COLS), lambda b, c: (row_map(b, c), 0)),
            pl.BlockSpec((1, CONV_K - 1, CONV_CH), lambda b, c: (b, 0, 0)),
            pl.BlockSpec((1, GLA_HEADS, GLA_DK, GLA_DV), lambda b, c: (b, 0, 0, 0)),
        ],
        out_shape=[
            jax.ShapeDtypeStruct((T_PROMPT, ACT_COLS), BF16),
            jax.ShapeDtypeStruct((BATCH, CONV_K - 1, CONV_CH), F32),
            jax.ShapeDtypeStruct((BATCH, GLA_HEADS, GLA_DK, GLA_DV), F32),
        ],
        scratch_shapes=[pltpu.VMEM((GLA_HEADS, GLA_DV, GLA_DK), F32),
                        pltpu.VMEM((8, CONV_CH), F32)],
        compiler_params=pltpu.CompilerParams(
            dimension_semantics=("arbitrary", "arbitrary"), vmem_limit_bytes=VMEM_LIMIT),
        name="mix_prompt",
    )(z, z, z, z, z, z, alr, *small)


def _mix_sample_body(v_ref, r_ref, qk_ref, hcg_ref, bgu_ref, vv_ref, alr_ref,
                       a2_ref, ab_ref, cw_ref, gng_ref, cmg_ref, ws_ref, cmb_ref,
                       p1_ref, p2_ref, s0_ref,
                       acts_ref, cin_ref, ns_ref, vrow_ref):
    n = ROWS_S

    pos = lax.broadcasted_iota(jnp.int32, (n, 1), 0) & (DEC_SEQ - 1)
    h = hcg_ref[:, :CONV_CH].astype(F32)
    cg = hcg_ref[:, CONV_CH:].astype(F32)
    bg = bgu_ref[:, :CONV_CH].astype(F32)
    cin = cg * h
    x1 = jnp.where(pos >= 1, pltpu.roll(cin, 1, 0), p1_ref[...])
    x2 = jnp.where(pos >= 2, pltpu.roll(cin, 2, 0), p2_ref[...])
    conv = x2 * cw_ref[0:1, :] + x1 * cw_ref[1:2, :] + cin * cw_ref[2:3, :]
    acts_ref[:, 0:CONV_CH] = (bg * conv).astype(BF16)
    cin_ref[...] = cin

    rr = lax.broadcasted_iota(jnp.int32, (n, n), 0)
    cc = lax.broadcasted_iota(jnp.int32, (n, n), 1)
    same = (rr >> 3) == (cc >> 3)
    tril = same & (cc <= rr)
    la_hi, la_lo, b, bl = _gla_decay_terms(alr_ref, a2_ref, ab_ref, tril, same)
    q = qk_ref[:, :GLA_QK].astype(F32) * (GLA_DK ** -0.5)
    k = qk_ref[:, GLA_QK:].astype(F32)
    q_t = (q * jnp.exp(b)).astype(BF16)
    k_t = (k * jnp.exp(-b)).astype(BF16)
    k_end = k * jnp.exp(bl - b)
    la_hi = la_hi.astype(F32)
    la_lo = la_lo.astype(F32)
    row_seq = lax.broadcasted_iota(jnp.int32, (n, GLA_DK), 0) >> 3
    seq3 = lax.broadcasted_iota(jnp.int32, (SEQ_PER_BLK, GLA_DK, n), 0)
    lane_seq3 = lax.broadcasted_iota(jnp.int32, (SEQ_PER_BLK, GLA_DK, n), 2) >> 3
    mask3 = seq3 == lane_seq3
    ones = jnp.ones((n, GLA_DV), BF16)
    big = SEQ_PER_BLK * GLA_DK

    def per_seq(x_tr):
        x3 = jnp.where(mask3, x_tr[None, :, :], 0.0)
        return x3.reshape(big, n).astype(BF16)

    for hd in range(GLA_HEADS):
        ks = slice(hd * GLA_DK, (hd + 1) * GLA_DK)
        vs = slice(hd * GLA_DV, (hd + 1) * GLA_DV)
        qh = q_t[:, ks]
        vh = v_ref[:, vs]
        att = jnp.where(tril, _dot_nt(qh, k_t[:, ks]), 0.0).astype(BF16)
        o_intra = _dot(att, vh)
        s_old = s0_ref[:, hd].reshape(big, GLA_DV)
        zero = jnp.zeros_like(qh)
        q_big = jnp.concatenate(
            [jnp.where(row_seq == j, qh, zero) for j in range(SEQ_PER_BLK)], axis=1)
        o = o_intra + _dot(q_big, s_old.astype(BF16))
        dlog = _dot(per_seq(la_hi[:, ks].T), ones) + _dot(per_seq(la_lo[:, ks].T), ones)
        upd = _dot(per_seq(k_end[:, ks].T), vh)
        s_new = jnp.exp(dlog) * s_old + upd
        ns_ref[:, hd] = s_new.reshape(SEQ_PER_BLK, GLA_DK, GLA_DV)
        r = r_ref[:, vs].astype(F32)
        acts_ref[:, CONV_CH + hd * GLA_DV:CONV_CH + (hd + 1) * GLA_DV] = (
            _gla_out_gate(o, gng_ref, r).astype(BF16))

    us, vg = _chunk_mlp(bgu_ref[:, CONV_CH:].astype(F32), vv_ref[...].astype(F32),
                        cmg_ref, ws_ref, cmb_ref, 1)
    acts_ref[:, CONV_CH + GLA_V:] = us.astype(BF16)
    vrow_ref[...] = vg


N_MIX_S_IN = 17


def _mix_sample_kernel(*refs, layer):
    if layer == 0:
        @pl.when(pl.program_id(0) == 0)
        def _():
            _mix_sample_body(*refs)

        @pl.when(pl.program_id(0) > 0)
        def _():
            ns_ref = refs[N_MIX_S_IN + 2]
            ns_ref[...] = jnp.zeros_like(ns_ref)
    else:
        _mix_sample_body(*refs[:N_MIX_S_IN], *refs[N_MIX_S_IN + 1:])


def _mix_sample(z, alr, a2, ab, cw, gng, cmg, ws, cmb, p1, p2, s0, ns_all, layer):
    row0 = T_PROMPT // ROWS_S
    n_i = DEC_BATCH // SEQ_PER_BLK
    n_pass = DEPTH if layer == 0 else 1
    blk = lambda p, i: jnp.where(p == 0, i, n_i - 1)
    row_map = lambda p, i: row0 + blk(p, i)
    slot = lambda p, i: (layer + p, i, 0, 0, 0)
    small = (a2, ab, cw, gng, cmg, ws, cmb)
    state_blk = (None, SEQ_PER_BLK, GLA_HEADS, GLA_DK, GLA_DV)
    in_specs = _z_specs(ROWS_S, row_map) + [
        pl.BlockSpec((ROWS_S, LANES), lambda p, i: (row_map(p, i), 0)),
    ] + [_layer_spec(a, layer) for a in small] + [
        pl.BlockSpec((ROWS_S, CONV_CH), lambda p, i: (blk(p, i), 0)),
        pl.BlockSpec((ROWS_S, CONV_CH), lambda p, i: (blk(p, i), 0)),
        pl.BlockSpec(state_blk, lambda p, i: (layer, blk(p, i), 0, 0, 0)),
    ]
    args = (z, z, z, z, z, z, alr, *small, p1, p2, s0)
    assert len(args) == N_MIX_S_IN
    aliases = {}
    if layer > 0:
        in_specs.append(pl.BlockSpec(memory_space=pl.ANY))
        args += (ns_all,)
        aliases = {N_MIX_S_IN: 2}
    return pl.pallas_call(
        functools.partial(_mix_sample_kernel, layer=layer),
        grid=(n_pass, n_i),
        in_specs=in_specs,
        out_specs=[
            pl.BlockSpec((ROWS_S, ACT_COLS), lambda p, i: (blk(p, i), 0)),
            pl.BlockSpec((ROWS_S, CONV_CH), lambda p, i: (blk(p, i), 0)),
            pl.BlockSpec(state_blk, slot),
            pl.BlockSpec((ROWS_S, CM_CH), lambda p, i: (blk(p, i), 0)),
        ],
        out_shape=[
            jax.ShapeDtypeStruct((T_SAMPLE, ACT_COLS), BF16),
            jax.ShapeDtypeStruct((T_SAMPLE, CONV_CH), F32),
            jax.ShapeDtypeStruct((DEPTH, DEC_BATCH, GLA_HEADS, GLA_DK, GLA_DV), F32),
            jax.ShapeDtypeStruct((T_SAMPLE, CM_CH), F32),
        ],
        input_output_aliases=aliases,
        compiler_params=pltpu.CompilerParams(
            dimension_semantics=("arbitrary", "arbitrary"), vmem_limit_bytes=VMEM_LIMIT),
        name="mix_sample",
    )(*args)


def _outproj_kernel(actp_ref, acts_ref, gates_ref, xa_ref, xb_ref, pa_ref, pb_ref, pc_ref, wo_ref,
                    n2_ref, wrh_ref, wrl_ref, rb_ref, xo_ref, xg_ref, *, n_xa):
    i = pl.program_id(0)
    acts = jnp.where(i < T_PROMPT // TM_OUT, actp_ref[...], acts_ref[...])
    x_in = jnp.where(i < n_xa, xa_ref[...], xb_ref[...])
    ya = _dot(acts[:, :CONV_CH], pa_ref[...])
    yb = _dot(acts[:, CONV_CH:CONV_CH + GLA_V], pb_ref[...])
    yc = _dot(acts[:, CONV_CH + GLA_V:], pc_ref[...])
    ga = _sigmoid(gates_ref[:, COL_GA:COL_GA + D_MODEL].astype(F32))
    gb = _sigmoid(gates_ref[:, COL_GB:COL_GB + D_MODEL].astype(F32))
    gc = _sigmoid(gates_ref[:, COL_GC:COL_GC + D_MODEL].astype(F32))
    mix = ga * ya + gb * yb + gc * yc
    x = x_in + _dot(mix.astype(BF16), wo_ref[...])
    xo_ref[...] = x
    xn = _rms(x, n2_ref[...])
    xg_ref[:, :D_MODEL] = xn

    hi, lo = _split_bf16(xn)
    logits = (_dot(hi, wrh_ref[...]) + _dot(lo, wrh_ref[...]) + _dot(hi, wrl_ref[...])
              + rb_ref[...])
    lane_i = lax.broadcasted_iota(jnp.int32, logits.shape, 1)
    lane = lane_i.astype(F32)
    neg = jnp.float32(-jnp.inf)
    is_g = lane_i < MOE_GROUPS
    lg = jnp.where(is_g, logits, neg)
    gmax = jnp.max(lg, axis=-1, keepdims=True)
    grp = jnp.min(jnp.where(lg == gmax, lane, 1e9), axis=-1, keepdims=True)
    p_grp = 1.0 / jnp.sum(jnp.where(is_g, jnp.exp(lg - gmax), 0.0), axis=-1, keepdims=True)
    first = MOE_GROUPS + EXPERTS_PER_GROUP * grp
    in_grp = (lane >= first) & (lane < first + EXPERTS_PER_GROUP)
    le = jnp.where(in_grp, logits, neg)
    v1 = jnp.max(le, axis=-1, keepdims=True)
    i1 = jnp.min(jnp.where(le == v1, lane, 1e9), axis=-1, keepdims=True)
    le2 = jnp.where(lane == i1, neg, le)
    v2 = jnp.max(le2, axis=-1, keepdims=True)
    i2 = jnp.min(jnp.where(le2 == v2, lane, 1e9), axis=-1, keepdims=True)
    t = jnp.exp(v2 - v1)
    g1 = p_grp / (1.0 + t)
    g2 = p_grp * t / (1.0 + t)
    xg_ref[:, D_MODEL:] = jnp.where(
        lane == i1 - first, g1,
        jnp.where(lane == i2 - first, g2, jnp.where(lane_i == GRP_LANE, grp, 0.0)))


def _outproj(acts_p, acts_s, z, xa, xb, pa, pb, pc, wo, n2, wrh, wrl, rb, layer):
    m = T_ALL
    row = lambda i: (i, 0)
    _, act_specs = _two_part_specs(TM_OUT, acts_p, acts_s)
    n_xa, x_specs = _two_part_specs(TM_OUT, xa, xb)
    weights = (pa, pb, pc, wo, n2, wrh, wrl, rb)
    return pl.pallas_call(
        functools.partial(_outproj_kernel, n_xa=n_xa),
        grid=(m // TM_OUT,),
        in_specs=act_specs + [pl.BlockSpec((TM_OUT, 3 * D_MODEL), row)] + x_specs
        + [_layer_spec(w, layer) for w in weights],
        out_specs=[
            pl.BlockSpec((TM_OUT, D_MODEL), row),
            pl.BlockSpec((TM_OUT, XG_COLS), row),
        ],
        out_shape=[
            jax.ShapeDtypeStruct((m, D_MODEL), F32),
            jax.ShapeDtypeStruct((m, XG_COLS), F32),
        ],
        compiler_params=pltpu.CompilerParams(
            dimension_semantics=("arbitrary",), vmem_limit_bytes=VMEM_LIMIT),
        name="outproj",
    )(acts_p, acts_s, z, xa, xb, *weights)


def _rank_kernel(r_ref, lrank_ref, seg_ref, meta_ref, cnt_ref, base_ref):
    p = pl.program_id(0)
    i = pl.program_id(1)
    tm = TM_RANK
    lane_i = lax.broadcasted_iota(jnp.int32, (tm, LANES), 1)
    onehot = jnp.where(lane_i.astype(F32) == r_ref[:, GRP_LANE:GRP_LANE + 1], 1.0, 0.0)
    csum = jnp.sum(onehot, axis=0, keepdims=True)

    @pl.when((p == 0) & (i == 0))
    def _():
        cnt_ref[...] = jnp.zeros_like(cnt_ref)

    @pl.when(p == 0)
    def _():
        cnt_ref[...] += csum

    @pl.when((p == 1) & (i == 0))
    def _():
        blocks = jnp.floor((cnt_ref[...] + (MOE_BLK - 1)) * (1.0 / MOE_BLK))
        rr = lax.broadcasted_iota(jnp.int32, (LANES, LANES), 0)
        cc = lax.broadcasted_iota(jnp.int32, (LANES, LANES), 1)
        earlier = jnp.where(rr < cc, 1.0, 0.0).astype(BF16)
        start = _dot(jnp.broadcast_to(blocks, (8, LANES)).astype(BF16), earlier)[0:1]
        base_ref[...] = start * MOE_BLK
        end = start + blocks
        lane1_i = lax.broadcasted_iota(jnp.int32, (1, LANES), 1)
        lane1 = lane1_i.astype(F32)
        grp_of_blk = jnp.zeros((1, LANES), F32)
        for g in range(MOE_GROUPS):
            end_g = jnp.sum(jnp.where(lane1_i == g, end, 0.0), axis=-1, keepdims=True)
            grp_of_blk += jnp.where(lane1 >= end_g, 1.0, 0.0)
        grp_of_blk = jnp.minimum(grp_of_blk, MOE_GROUPS - 1)
        n_used = jnp.sum(blocks, axis=-1, keepdims=True)
        row = lax.broadcasted_iota(jnp.int32, (8, LANES), 0)
        meta_ref[...] = jnp.where(
            row == 0, grp_of_blk,
            jnp.where(row == 1, n_used, jnp.where(row == 2, end, jnp.where(row == 3, blocks, 0.0)))
        ).astype(jnp.int32)
        cnt_ref[...] = jnp.zeros_like(cnt_ref)

    @pl.when(p == 1)
    def _():
        rr = lax.broadcasted_iota(jnp.int32, (tm, tm), 0)
        cc = lax.broadcasted_iota(jnp.int32, (tm, tm), 1)
        before = jnp.where(cc < rr, 1.0, 0.0).astype(BF16)
        prefix = _dot(before, onehot.astype(BF16))
        r2 = lax.broadcasted_iota(jnp.int32, (LANES, LANES), 0)
        c2 = lax.broadcasted_iota(jnp.int32, (LANES, LANES), 1)
        earlier = jnp.where(r2 < c2, 1.0, 0.0).astype(BF16)
        local = _dot(jnp.broadcast_to(csum, (8, LANES)).astype(BF16), earlier)[0:1]
        carry = cnt_ref[...]
        lrank = jnp.sum(onehot * (prefix + local), axis=-1, keepdims=True)
        lrank_ref[...] = jnp.where(lane_i == 0, lrank, 0.0).astype(jnp.int32)
        row = lax.broadcasted_iota(jnp.int32, (8, LANES), 0)
        seg_ref[...] = jnp.where(
            row == 0, carry + base_ref[...],
            jnp.where(row == 1, csum, jnp.where(row == 2, local, 0.0))).astype(jnp.int32)
        cnt_ref[...] = carry + csum


def _rank(xg):
    m = xg.shape[0]
    nt = m // TM_RANK
    gate_blk = D_MODEL // LANES
    return pl.pallas_call(
        _rank_kernel,
        grid=(2, nt),
        in_specs=[pl.BlockSpec((TM_RANK, LANES), lambda p, i: (i, gate_blk))],
        out_specs=[pl.BlockSpec((TM_RANK, LANES), lambda p, i: (i * p, 0)),
                   pl.BlockSpec((None, 8, LANES), lambda p, i: (i * p, 0, 0)),
                   pl.BlockSpec((8, LANES), lambda p, i: (0, 0))],
        out_shape=[jax.ShapeDtypeStruct((m, LANES), jnp.int32),
                   jax.ShapeDtypeStruct((nt, 8, LANES), jnp.int32),
                   jax.ShapeDtypeStruct((8, LANES), jnp.int32)],
        scratch_shapes=[pltpu.VMEM((1, LANES), F32), pltpu.VMEM((1, LANES), F32)],
        compiler_params=pltpu.CompilerParams(dimension_semantics=("arbitrary", "arbitrary")),
        name="rank",
    )(xg)


SEG_START, SEG_LEN, SEG_LOCAL = 0, 1, 2


def _segment_copies(seg_ref, make_copy):
    for g in range(MOE_GROUPS):
        n = seg_ref[SEG_LEN, g]
        k = TM_ROW
        while k >= 1:
            done = n & ~(2 * k - 1)
            @pl.when((n & k) != 0)
            def _():
                make_copy(seg_ref[SEG_LOCAL, g] + done, seg_ref[SEG_START, g] + done, k).start()
            k //= 2


def _perm_matrix(lrank_ref):
    col = lax.broadcasted_iota(jnp.int32, (TM_ROW, TM_ROW), 1)
    return jnp.where(col == lrank_ref[:, 0:1], 1.0, 0.0)


SUB = 8
SUB_X = D_MODEL // 2 // LANES
SUB_GATE = SUB_X
HI_MASK = -65536


def _rows(first_row, n_rows):
    return pl.ds(pl.multiple_of(first_row * SUB, SUB), n_rows * SUB)


def _sublane(s, n_rows):
    return pl.ds(s, n_rows, stride=SUB)


def _scatter_kernel(nu_ref, end_ref, nb_ref, seg_ref, lrank_ref, xg_ref, xs_ref, zbuf, sbuf, sems):
    i = pl.program_id(0)
    n = pl.num_programs(0)
    slot = i % 2
    sem = sems.at[0]

    def zero_block(b):
        return pltpu.make_async_copy(zbuf, xs_ref.at[_rows(b * MOE_BLK, MOE_BLK)], sem)

    def each_unfilled_block(fn):
        for g in range(MOE_GROUPS):
            @pl.when(nb_ref[g] > 0)
            def _():
                fn(zero_block(end_ref[g] - 1))
        for b in range(T_ALL // MOE_BLK, N_BLK):
            @pl.when(b >= nu_ref[0])
            def _():
                fn(zero_block(b))

    @pl.when(i == 0)
    def _():
        zbuf[...] = jnp.zeros_like(zbuf)
        sbuf[...] = jnp.zeros_like(sbuf)
        each_unfilled_block(lambda c: c.start())
        each_unfilled_block(lambda c: c.wait())

    def tile_done(s):
        pltpu.make_async_copy(sbuf.at[s], xs_ref.at[_rows(0, TM_ROW)], sems.at[s]).wait()

    @pl.when(i >= 2)
    def _():
        tile_done(slot)

    perm = _perm_matrix(lrank_ref).T.astype(BF16)
    xs = lax.bitcast_convert_type(_dot(perm, xg_ref[:, :D_MODEL].astype(BF16)), jnp.int32)
    half = D_MODEL // 2
    for s in range(SUB_X):
        hi = xs[:, s * LANES:(s + 1) * LANES] & HI_MASK
        lo = lax.shift_right_logical(xs[:, half + s * LANES:half + (s + 1) * LANES], 16)
        sbuf[slot, _sublane(s, TM_ROW), :] = hi | lo
    gate = xg_ref[:, D_MODEL:]
    g1 = gate.astype(BF16)
    r1 = gate - g1.astype(F32)
    g2 = r1.astype(BF16)
    g3 = (r1 - g2.astype(F32)).astype(BF16)
    sbuf[slot, _sublane(SUB_GATE, TM_ROW), :] = lax.bitcast_convert_type(
        _dot(perm, g1) + _dot(perm, g2) + _dot(perm, g3), jnp.int32)

    def make_copy(src_row, dst_row, k):
        return pltpu.make_async_copy(sbuf.at[slot, _rows(src_row, k)],
                                     xs_ref.at[_rows(dst_row, k)], sems.at[slot])
    _segment_copies(seg_ref, make_copy)

    @pl.when(i == n - 1)
    def _():
        tile_done(slot)

        @pl.when(n > 1)
        def _():
            tile_done(1 - slot)


def _scatter(seg, lrank, xg, n_used, grp_end, grp_blocks):
    m = xg.shape[0]
    nt = m // TM_ROW
    grid_spec = pltpu.PrefetchScalarGridSpec(
        num_scalar_prefetch=3,
        grid=(nt,),
        in_specs=[
            pl.BlockSpec((None, 8, LANES), lambda i, *_: (i, 0, 0), memory_space=pltpu.SMEM),
            pl.BlockSpec((TM_ROW, LANES), lambda i, *_: (i, 0)),
            pl.BlockSpec((TM_ROW, XG_COLS), lambda i, *_: (i, 0)),
        ],
        out_specs=pl.BlockSpec(memory_space=pl.ANY),
        scratch_shapes=[pltpu.VMEM((MOE_BLK * SUB, LANES), jnp.int32),
                        pltpu.VMEM((2, TM_ROW * SUB, LANES), jnp.int32),
                        pltpu.SemaphoreType.DMA((2,))],
    )
    return pl.pallas_call(
        _scatter_kernel,
        grid_spec=grid_spec,
        out_shape=jax.ShapeDtypeStruct((N_SORTED * SUB, LANES), jnp.int32),
        compiler_params=pltpu.CompilerParams(
            dimension_semantics=("arbitrary",), vmem_limit_bytes=VMEM_LIMIT),
        name="scatter",
    )(n_used, grp_end, grp_blocks, seg, lrank, xg)


def _ffn_kernel(bg_ref, nu_ref, xs_ref, w1_ref, w3_ref, w2_ref, y_ref):
    del bg_ref
    b = pl.program_id(0)

    @pl.when(b < nu_ref[0])
    def _():
        packed = [xs_ref[_sublane(s, MOE_BLK), :] for s in range(SUB_X)]
        x = jnp.concatenate(
            [lax.bitcast_convert_type(u & HI_MASK, F32).astype(BF16) for u in packed]
            + [lax.bitcast_convert_type(lax.shift_left(u, 16), F32).astype(BF16) for u in packed],
            axis=1)
        gates = lax.bitcast_convert_type(xs_ref[_sublane(SUB_GATE, MOE_BLK), :], F32)
        hs = []
        for e in range(EXPERTS_PER_GROUP):
            h1 = _dot(x, w1_ref[e])
            h3 = _dot(x, w3_ref[e])
            ge = gates[:, e:e + 1]
            hs.append(jnp.where(ge > 0.0, h1 * _sigmoid(h1) * h3 * ge, 0.0).astype(BF16))
        hcat = jnp.concatenate(hs, axis=1)
        y = _dot(hcat, w2_ref[...].reshape(EXPERTS_PER_GROUP * D_EXPERT, D_MODEL))
        for s in range(SUB):
            y_ref[_sublane(s, MOE_BLK), :] = y[:, s * LANES:(s + 1) * LANES]

    @pl.when(b >= nu_ref[0])
    def _():
        y_ref[...] = jnp.zeros_like(y_ref)


def _ffn(blk_group, n_used, xs, w1, w3, w2, layer):
    wmap = lambda b, bg, nu: (layer * MOE_GROUPS + bg[b], 0, 0, 0)
    grid_spec = pltpu.PrefetchScalarGridSpec(
        num_scalar_prefetch=2,
        grid=(N_BLK,),
        in_specs=[
            pl.BlockSpec((MOE_BLK * SUB, LANES), lambda b, bg, nu: (b, 0)),
            pl.BlockSpec((None, EXPERTS_PER_GROUP, D_MODEL, D_EXPERT), wmap),
            pl.BlockSpec((None, EXPERTS_PER_GROUP, D_MODEL, D_EXPERT), wmap),
            pl.BlockSpec((None, EXPERTS_PER_GROUP, D_EXPERT, D_MODEL), wmap),
        ],
        out_specs=pl.BlockSpec((MOE_BLK * SUB, LANES), lambda b, bg, nu: (b, 0)),
    )
    return pl.pallas_call(
        _ffn_kernel,
        grid_spec=grid_spec,
        out_shape=jax.ShapeDtypeStruct((N_SORTED * SUB, LANES), F32),
        compiler_params=pltpu.CompilerParams(
            dimension_semantics=("arbitrary",), vmem_limit_bytes=VMEM_LIMIT),
        name="ffn",
    )(blk_group, n_used, xs, w1, w3, w2)


def _combine_kernel(seg_ref, segn_ref, lrank_ref, ys_hbm, x_ref, g_ref, *refs, final):
    outs, (buf, sem) = refs[:-2], refs[-2:]
    i = pl.program_id(0)
    n = pl.num_programs(0)
    slot = i % 2

    def gather(seg, s):
        def make_copy(buf_row, ys_row, k):
            return pltpu.make_async_copy(ys_hbm.at[_rows(ys_row, k)],
                                         buf.at[s, _rows(buf_row, k)], sem.at[s])
        _segment_copies(seg, make_copy)

    @pl.when(i == 0)
    def _():
        gather(seg_ref, 0)

    @pl.when(i + 1 < n)
    def _():
        gather(segn_ref, 1 - slot)

    pltpu.make_async_copy(ys_hbm.at[_rows(0, TM_ROW)], buf.at[slot], sem.at[slot]).wait()

    pt = _perm_matrix(lrank_ref).astype(BF16)
    cols = []
    for s in range(SUB):
        hi, lo = _split_bf16(buf[slot, _sublane(s, TM_ROW), :])
        cols.append(_dot(pt, hi) + _dot(pt, lo))
    y = x_ref[...] + jnp.concatenate(cols, axis=1)
    if not final:
        outs[0][...] = y
    else:
        y = _rms(y, g_ref[...])

        @pl.when(i < T_PROMPT // TM_ROW)
        def _():
            outs[0][...] = y

        @pl.when(i >= T_PROMPT // TM_ROW)
        def _():
            outs[1][...] = y


def _combine(seg, lrank, ys, x, g, final):
    m = x.shape[0]
    nt = m // TM_ROW
    n_p = T_PROMPT // TM_ROW
    smem = functools.partial(pl.BlockSpec, (None, 8, LANES), memory_space=pltpu.SMEM)
    tile = (TM_ROW, D_MODEL)
    if final:
        out_specs = [pl.BlockSpec(tile, lambda i: (jnp.minimum(i, n_p - 1), 0)),
                     pl.BlockSpec(tile, lambda i: (jnp.maximum(i - n_p, 0), 0))]
        out_shape = [jax.ShapeDtypeStruct((T_PROMPT, D_MODEL), F32),
                     jax.ShapeDtypeStruct((T_SAMPLE, D_MODEL), F32)]
    else:
        out_specs = [pl.BlockSpec(tile, lambda i: (i, 0))]
        out_shape = [jax.ShapeDtypeStruct((m, D_MODEL), F32)]
    return pl.pallas_call(
        functools.partial(_combine_kernel, final=final),
        grid=(nt,),
        in_specs=[
            smem(lambda i: (i, 0, 0)),
            smem(lambda i: (jnp.minimum(i + 1, nt - 1), 0, 0)),
            pl.BlockSpec((TM_ROW, LANES), lambda i: (i, 0)),
            pl.BlockSpec(memory_space=pl.ANY),
            pl.BlockSpec(tile, lambda i: (i, 0)),
            _const_spec(g),
        ],
        out_specs=out_specs,
        out_shape=out_shape,
        scratch_shapes=[pltpu.VMEM((2, TM_ROW * SUB, LANES), F32),
                        pltpu.SemaphoreType.DMA((2,))],
        compiler_params=pltpu.CompilerParams(
            dimension_semantics=("arbitrary",), vmem_limit_bytes=VMEM_LIMIT),
        name="combine",
    )(seg, seg, lrank, ys, x, g)


def _prep_weights(w_in, gla_a2, cm_ws, cm_b, router_group_w, router_group_b,
                  router_expert_w, router_expert_b):
    off = {}
    o = 0
    for name, n in (("h", 512), ("cg", 512), ("bg", 512), ("q", 512), ("k", 512), ("v", 1024),
                    ("r", 1024), ("alr", 16), ("u", 512), ("vv", 512), ("ga", 1024),
                    ("gb", 1024), ("gc", 1024)):
        off[name] = (o, n)
        o += n
    order = ("ga", "gb", "gc", "v", "r", "q", "k", "h", "cg", "bg", "u", "vv")
    w_t = jnp.swapaxes(w_in, 1, 2)
    w_z = jnp.concatenate([w_t[:, off[n][0]:off[n][0] + off[n][1]] for n in order],
                          axis=1).astype(BF16)
    a0 = off["alr"][0]
    w_alr = jnp.pad(w_t[:, a0:a0 + GLA_LOWRANK],
                    ((0, 0), (0, LANES - GLA_LOWRANK), (0, 0))).astype(BF16)
    a2 = jnp.pad(gla_a2, ((0, 0), (0, LANES - GLA_LOWRANK), (0, 0))).astype(BF16)
    ws_p = jnp.tril(cm_ws).astype(BF16)
    small = jnp.tril(cm_ws[:, :, :DEC_SEQ, :DEC_SEQ])
    eye = jnp.eye(SEQ_PER_BLK, dtype=F32)
    ws_s = jnp.einsum("ij,lgab->lgiajb", eye, small).reshape(
        DEPTH, CM_GROUPS, ROWS_S, ROWS_S).astype(BF16)
    cmb_p = jnp.broadcast_to(jnp.transpose(cm_b, (0, 2, 1))[:, :, :, None],
                             (DEPTH, CM_CHUNK, CM_GROUPS, CM_GCH)).reshape(DEPTH, CM_CHUNK, CM_CH)
    cmb_s = jnp.tile(cmb_p[:, :DEC_SEQ], (1, SEQ_PER_BLK, 1))
    pad = LANES - MOE_GROUPS - N_EXPERTS
    w_r = jnp.pad(jnp.concatenate([router_group_w, router_expert_w], axis=-1),
                  ((0, 0), (0, 0), (0, pad)))
    w_r_hi = w_r.astype(BF16)
    w_r_lo = (w_r - w_r_hi.astype(F32)).astype(BF16)
    r_b = jnp.pad(jnp.concatenate([router_group_b, router_expert_b], axis=-1),
                  ((0, 0), (0, pad)))[:, None, :]
    return w_z, w_alr, a2, ws_p, ws_s, cmb_p, cmb_s, w_r_hi, w_r_lo, r_b


def kernel(x_prompt, x_sample, state_conv, state_gla, norm1_g, w_in, conv_w, gla_a2, gla_a_b,
           gla_norm_g, cm_norm_g, cm_ws, cm_b, proj_a, proj_b, proj_c, w_out, norm2_g,
           router_group_w, router_group_b, router_expert_w, router_expert_b,
           exp_w1, exp_w3, exp_w2, final_norm_g):
    (w_z, w_alr, a2, ws_p, ws_s, cmb_p, cmb_s, w_r_hi, w_r_lo, r_b) = _prep_weights(
        w_in, gla_a2, cm_ws, cm_b, router_group_w, router_group_b, router_expert_w,
        router_expert_b)
    pa, pb, pc, wo = (w.astype(BF16) for w in (proj_a, proj_b, proj_c, w_out))
    grouped = (DEPTH * MOE_GROUPS, EXPERTS_PER_GROUP)
    w1 = exp_w1.astype(BF16).reshape(grouped + (D_MODEL, D_EXPERT))
    w3 = exp_w3.astype(BF16).reshape(grouped + (D_MODEL, D_EXPERT))
    w2 = exp_w2.astype(BF16).reshape(grouped + (D_EXPERT, D_MODEL))
    n1 = norm1_g[:, None, :]
    n2 = norm2_g[:, None, :]
    ab = gla_a_b[:, None, :]
    gng = gla_norm_g[:, None, :]
    cmg = cm_norm_g.reshape(DEPTH, 1, CM_CH)
    fg = final_norm_g[None, :]
    xa = x_prompt.reshape(T_PROMPT, D_MODEL)
    xb = x_sample.reshape(T_SAMPLE, D_MODEL)
    gla_s = None
    conv_p, gla_p, conv_s, cmv_s = [], [], [], []
    for l in range(DEPTH):
        z, alr = _inproj(xa, xb, n1, w_z, w_alr, l)
        acts_p, nconv, ngla = _mix_prompt(z, alr, a2, ab, conv_w, gng, cmg, ws_p, cmb_p, l)
        sc = state_conv[l]
        p2 = jnp.pad(sc, ((0, 0), (0, DEC_SEQ - 2), (0, 0))).reshape(T_SAMPLE, CONV_CH)
        p1 = jnp.pad(sc[:, 1:2], ((0, 0), (0, DEC_SEQ - 1), (0, 0))).reshape(T_SAMPLE, CONV_CH)
        acts_s, cin_s, gla_s, vrows = _mix_sample(z, alr, a2, ab, conv_w, gng, cmg, ws_s, cmb_s,
                                                  p1, p2, state_gla, gla_s, l)
        conv_p.append(nconv)
        gla_p.append(ngla)
        conv_s.append(cin_s.reshape(DEC_BATCH, DEC_SEQ, CONV_CH)[:, DEC_SEQ - (CONV_K - 1):])
        cmv_s.append(vrows.reshape(DEC_BATCH, DEC_SEQ, CM_CH))

        x, xg = _outproj(acts_p, acts_s, z, xa, xb, pa, pb, pc, wo, n2, w_r_hi, w_r_lo, r_b, l)
        lrank, seg, meta = _rank(xg)
        n_used = meta[1, :1]
        xs = _scatter(seg, lrank, xg, n_used, meta[2, :MOE_GROUPS], meta[3, :MOE_GROUPS])
        ys = _ffn(meta[0, :N_BLK], n_used, xs, w1, w3, w2, l)
        out = _combine(seg, lrank, ys, x, fg, l == DEPTH - 1)
        xa = xb = out[0]

    y_prompt = out[0].reshape(BATCH, SEQ, D_MODEL)
    y_sample = out[1].reshape(DEC_BATCH, DEC_SEQ, D_MODEL)
    return (y_prompt, y_sample, jnp.stack(conv_p), jnp.stack(gla_p), jnp.stack(conv_s),
            gla_s, jnp.stack(cmv_s))
```

```python
import functools

import jax
import jax.numpy as jnp
from jax import lax
from jax.experimental import pallas as pl
from jax.experimental.pallas import tpu as pltpu

F32 = jnp.float32
BF16 = jnp.bfloat16

D_MODEL = 1024
BATCH = 8
SEQ = 2048
DEPTH = 2
DEC_BATCH = 128
DEC_SEQ = 8
CONV_K = 3
CONV_CH = 512
GLA_HEADS = 4
GLA_DK = 128
GLA_DV = 256
GLA_QK = GLA_HEADS * GLA_DK
GLA_V = GLA_HEADS * GLA_DV
GLA_LOWRANK = 16
GLA_TAU = 16.0
GLA_CHUNK = 64
CM_GROUPS = 4
CM_CHUNK = 128
CM_GCH = 128
CM_CH = 512
MOE_GROUPS = 8
EXPERTS_PER_GROUP = 8
N_EXPERTS = 64
D_EXPERT = 256
EPS = 1e-6

LANES = 128
T_PROMPT = BATCH * SEQ
T_SAMPLE = DEC_BATCH * DEC_SEQ
T_ALL = T_PROMPT + T_SAMPLE

COL_GA, COL_GB, COL_GC = 0, 1024, 2048
COL_V, COL_R, COL_Q, COL_K = 3072, 4096, 5120, 5632
COL_H, COL_CG, COL_BG, COL_U, COL_VV = 6144, 6656, 7168, 7680, 8192
Z_COLS = 8704
ACT_COLS = 2048
XG_COLS = D_MODEL + LANES

TM_IN = 1024
TN_IN = Z_COLS // 4
TC_MIX = 256
SEQ_PER_BLK = 16
ROWS_S = SEQ_PER_BLK * DEC_SEQ
TM_OUT = 256
TM_ROW = TM_OUT
MOE_BLK = 256
N_BLK = T_ALL // MOE_BLK + MOE_GROUPS
N_SORTED = N_BLK * MOE_BLK
VMEM_LIMIT = 56 * 1024 * 1024


def _sigmoid(x):
    return 1.0 / (1.0 + jnp.exp(-x))


def _gelu_tanh(x):
    return 0.5 * x * (1.0 + jnp.tanh(0.7978845608028654 * (x + 0.044715 * (x * x * x))))


def _log_sigmoid(x):
    return jnp.minimum(x, 0.0) - jnp.log(1.0 + jnp.exp(-jnp.abs(x)))


def _rms(x, g):
    ms = jnp.mean(x * x, axis=-1, keepdims=True)
    return x * lax.rsqrt(ms + EPS) * g


def _split_bf16(x):
    hi = x.astype(BF16)
    lo = (x - hi.astype(F32)).astype(BF16)
    return hi, lo


def _dot(a, b):
    return jnp.dot(a, b, preferred_element_type=F32)


def _dot_nt(a, b):
    return lax.dot_general(a, b, (((1,), (1,)), ((), ())), preferred_element_type=F32)


def _layer_spec(arr, layer):
    nd = arr.ndim - 1
    return pl.BlockSpec((None,) + arr.shape[1:], lambda *g: (layer,) + (0,) * nd)


def _const_spec(arr):
    nd = arr.ndim
    return pl.BlockSpec(arr.shape, lambda *g: (0,) * nd)


def _two_part_specs(tile, xa, xb):
    n_a = xa.shape[0] // tile
    return n_a, [
        pl.BlockSpec((tile, xa.shape[1]), lambda i, *_: (jnp.minimum(i, n_a - 1), 0)),
        pl.BlockSpec((tile, xb.shape[1]), lambda i, *_: (jnp.maximum(i - n_a, 0), 0)),
    ]


def _inproj_kernel(xa_ref, xb_ref, g_ref, w_ref, wa_ref, z_ref, a_ref, xn_ref, *, n_a):
    @pl.when(pl.program_id(1) == 0)
    def _():
        x = jnp.where(pl.program_id(0) < n_a, xa_ref[...], xb_ref[...])
        xn = _rms(x, g_ref[...]).astype(BF16)
        xn_ref[...] = xn
        a_ref[...] = _dot_nt(xn, wa_ref[...]).astype(BF16)

    z_ref[...] = _dot_nt(xn_ref[...], w_ref[...]).astype(BF16)


def _inproj(xa, xb, g, w, wa, layer):
    m = T_ALL
    n_a, x_specs = _two_part_specs(TM_IN, xa, xb)
    return pl.pallas_call(
        functools.partial(_inproj_kernel, n_a=n_a),
        grid=(m // TM_IN, Z_COLS // TN_IN),
        in_specs=x_specs + [
            _layer_spec(g, layer),
            pl.BlockSpec((None, TN_IN, D_MODEL), lambda i, j: (layer, j, 0)),
            _layer_spec(wa, layer),
        ],
        out_specs=[
            pl.BlockSpec((TM_IN, TN_IN), lambda i, j: (i, j)),
            pl.BlockSpec((TM_IN, LANES), lambda i, j: (i, 0)),
        ],
        out_shape=[
            jax.ShapeDtypeStruct((m, Z_COLS), BF16),
            jax.ShapeDtypeStruct((m, LANES), BF16),
        ],
        scratch_shapes=[pltpu.VMEM((TM_IN, D_MODEL), BF16)],
        compiler_params=pltpu.CompilerParams(
            dimension_semantics=("arbitrary", "arbitrary"), vmem_limit_bytes=VMEM_LIMIT),
        name="inproj",
    )(xa, xb, g, w, wa)


def _gla_log_decay(alr_ref, a2_ref, ab_ref):
    la = _log_sigmoid(_dot(alr_ref[...], a2_ref[...]) + ab_ref[...]) * (1.0 / GLA_TAU)
    return _split_bf16(la)


def _masked_sum(mask, la_hi, la_lo):
    m = jnp.where(mask, 1.0, 0.0).astype(BF16)
    return _dot(m, la_hi) + _dot(m, la_lo)


def _gla_decay_prefix(alr_ref, a2_ref, ab_ref, tril_mask):
    la_hi, la_lo = _gla_log_decay(alr_ref, a2_ref, ab_ref)
    return _masked_sum(tril_mask, la_hi, la_lo)


def _gla_decay_terms(alr_ref, a2_ref, ab_ref, tril_mask, same_mask):
    la_hi, la_lo = _gla_log_decay(alr_ref, a2_ref, ab_ref)
    return (la_hi, la_lo, _masked_sum(tril_mask, la_hi, la_lo),
            _masked_sum(same_mask, la_hi, la_lo))


def _gla_out_gate(o, g_ref, r):
    return _rms(o, g_ref[...]) * (r * _sigmoid(r))


def _chunk_mlp(u, vv, cmg_ref, ws_ref, cmb_ref, n_chunks):
    ug = _gelu_tanh(u)
    gv = _gelu_tanh(vv)
    outs, vgs = [], []
    for g in range(CM_GROUPS):
        sl = slice(g * CM_GCH, (g + 1) * CM_GCH)
        vg = _rms(gv[:, sl], cmg_ref[:, sl])
        vgs.append(vg)
        vgb = vg.astype(BF16)
        rows = []
        for j in range(n_chunks):
            rs = slice(j * CM_CHUNK, (j + 1) * CM_CHUNK)
            rows.append(_dot(ws_ref[g], vgb[rs]) + cmb_ref[:, sl])
        s = rows[0] if n_chunks == 1 else jnp.concatenate(rows, axis=0)
        outs.append(ug[:, sl] * s)
    return jnp.concatenate(outs, axis=1), jnp.concatenate(vgs, axis=1)


def _mix_prompt_kernel(v_ref, r_ref, qk_ref, hcg_ref, bgu_ref, vv_ref, alr_ref,
                       a2_ref, ab_ref, cw_ref, gng_ref, cmg_ref, ws_ref, cmb_ref,
                       acts_ref, nconv_ref, ngla_ref, st_ref, carry_ref):
    tc = TC_MIX

    @pl.when(pl.program_id(1) == 0)
    def _():
        st_ref[...] = jnp.zeros_like(st_ref)
        carry_ref[...] = jnp.zeros_like(carry_ref)

    row = lax.broadcasted_iota(jnp.int32, (tc, 1), 0)
    h = hcg_ref[:, :CONV_CH].astype(F32)
    cg = hcg_ref[:, CONV_CH:].astype(F32)
    bg = bgu_ref[:, :CONV_CH].astype(F32)
    cin = cg * h
    c0 = carry_ref[0:1, :]
    c1 = carry_ref[1:2, :]
    x1 = jnp.where(row >= 1, pltpu.roll(cin, 1, 0), c1)
    x2 = jnp.where(row >= 2, pltpu.roll(cin, 2, 0), jnp.where(row == 1, c1, c0))
    conv = x2 * cw_ref[0:1, :] + x1 * cw_ref[1:2, :] + cin * cw_ref[2:3, :]
    acts_ref[:, 0:CONV_CH] = (bg * conv).astype(BF16)
    carry_ref[0:2, :] = cin[tc - 2:tc, :]
    nconv_ref[0] = cin[tc - 2:tc, :]

    rr = lax.broadcasted_iota(jnp.int32, (tc, tc), 0)
    cc = lax.broadcasted_iota(jnp.int32, (tc, tc), 1)
    same = (rr >> 6) == (cc >> 6)
    tril = same & (cc <= rr)
    b = _gla_decay_prefix(alr_ref, a2_ref, ab_ref, tril)
    n_chunks = tc // GLA_CHUNK
    b_last = [b[(c + 1) * GLA_CHUNK - 1:(c + 1) * GLA_CHUNK, :] for c in range(n_chunks)]
    bl = jnp.concatenate([jnp.broadcast_to(r_, (GLA_CHUNK, GLA_QK)) for r_ in b_last], axis=0)
    q = qk_ref[:, :GLA_QK].astype(F32) * (GLA_DK ** -0.5)
    k = qk_ref[:, GLA_QK:].astype(F32)
    q_t = (q * jnp.exp(b)).astype(BF16)
    k_t = (k * jnp.exp(-b)).astype(BF16)
    k_end = (k * jnp.exp(bl - b)).astype(BF16)
    lane_chunk = lax.broadcasted_iota(jnp.int32, (GLA_DV, tc), 1) >> 6
    for hd in range(GLA_HEADS):
        ks = slice(hd * GLA_DK, (hd + 1) * GLA_DK)
        vs = slice(hd * GLA_DV, (hd + 1) * GLA_DV)
        qh = q_t[:, ks]
        vh = v_ref[:, vs]
        att = jnp.where(tril, _dot_nt(qh, k_t[:, ks]), 0.0).astype(BF16)
        o_intra = _dot(att, vh)
        v_tr = vh.astype(F32).T
        o_rows = []
        for c in range(n_chunks):
            rs = slice(c * GLA_CHUNK, (c + 1) * GLA_CHUNK)
            st_old = st_ref[hd]
            o_rows.append(o_intra[rs] + _dot_nt(qh[rs], st_old.astype(BF16)))
            v_c = jnp.where(lane_chunk == c, v_tr, 0.0).astype(BF16)
            st_ref[hd] = jnp.exp(b_last[c][:, ks]) * st_old + _dot(v_c, k_end[:, ks])
        o = jnp.concatenate(o_rows, axis=0)
        r = r_ref[:, vs].astype(F32)
        acts_ref[:, CONV_CH + hd * GLA_DV:CONV_CH + (hd + 1) * GLA_DV] = (
            _gla_out_gate(o, gng_ref, r).astype(BF16))

    @pl.when(pl.program_id(1) == pl.num_programs(1) - 1)
    def _():
        for hd in range(GLA_HEADS):
            ngla_ref[0, hd] = st_ref[hd].T

    us, _ = _chunk_mlp(bgu_ref[:, CONV_CH:].astype(F32), vv_ref[...].astype(F32),
                       cmg_ref, ws_ref, cmb_ref, tc // CM_CHUNK)
    acts_ref[:, CONV_CH + GLA_V:] = us.astype(BF16)


def _z_specs(rows, row_map):
    def spec(width, col):
        blk = col // width
        return pl.BlockSpec((rows, width), lambda *g: (row_map(*g), blk))
    return [spec(1024, COL_V), spec(1024, COL_R), spec(1024, COL_Q), spec(1024, COL_H),
            spec(1024, COL_BG), spec(512, COL_VV)]


def _mix_prompt(z, alr, a2, ab, cw, gng, cmg, ws, cmb, layer):
    nt = SEQ // TC_MIX
    row_map = lambda b, c: b * nt + c
    small = (a2, ab, cw, gng, cmg, ws, cmb)
    in_specs = _z_specs(TC_MIX, row_map) + [
        pl.BlockSpec((TC_MIX, LANES), lambda b, c: (row_map(b, c), 0)),
    ] + [_layer_spec(a, layer) for a in small]
    return pl.pallas_call(
        _mix_prompt_kernel,
        grid=(BATCH, nt),
        in_specs=in_specs,
        out_specs=[
            pl.BlockSpec((TC_MIX, ACT_COLS), lambda b, c: (row_map(b, c), 0)),
            pl.BlockSpec((1, CONV_K - 1, CONV_CH), lambda b, c: (b, 0, 0)),
            pl.BlockSpec((1, GLA_HEADS, GLA_DK, GLA_DV), lambda b, c: (b, 0, 0, 0)),
        ],
        out_shape=[
            jax.ShapeDtypeStruct((T_PROMPT, ACT_COLS), BF16),
            jax.ShapeDtypeStruct((BATCH, CONV_K - 1, CONV_CH), F32),
            jax.ShapeDtypeStruct((BATCH, GLA_HEADS, GLA_DK, GLA_DV), F32),
        ],
        scratch_shapes=[pltpu.VMEM((GLA_HEADS, GLA_DV, GLA_DK), F32),
                        pltpu.VMEM((8, CONV_CH), F32)],
        compiler_params=pltpu.CompilerParams(
            dimension_semantics=("arbitrary", "arbitrary"), vmem_limit_bytes=VMEM_LIMIT),
        name="mix_prompt",
    )(z, z, z, z, z, z, alr, *small)


def _mix_sample_body(v_ref, r_ref, qk_ref, hcg_ref, bgu_ref, vv_ref, alr_ref,
                       a2_ref, ab_ref, cw_ref, gng_ref, cmg_ref, ws_ref, cmb_ref,
                       p1_ref, p2_ref, s0_ref,
                       acts_ref, cin_ref, ns_ref, vrow_ref):
    n = ROWS_S

    pos = lax.broadcasted_iota(jnp.int32, (n, 1), 0) & (DEC_SEQ - 1)
    h = hcg_ref[:, :CONV_CH].astype(F32)
    cg = hcg_ref[:, CONV_CH:].astype(F32)
    bg = bgu_ref[:, :CONV_CH].astype(F32)
    cin = cg * h
    x1 = jnp.where(pos >= 1, pltpu.roll(cin, 1, 0), p1_ref[...])
    x2 = jnp.where(pos >= 2, pltpu.roll(cin, 2, 0), p2_ref[...])
    conv = x2 * cw_ref[0:1, :] + x1 * cw_ref[1:2, :] + cin * cw_ref[2:3, :]
    acts_ref[:, 0:CONV_CH] = (bg * conv).astype(BF16)
    cin_ref[...] = cin

    rr = lax.broadcasted_iota(jnp.int32, (n, n), 0)
    cc = lax.broadcasted_iota(jnp.int32, (n, n), 1)
    same = (rr >> 3) == (cc >> 3)
    tril = same & (cc <= rr)
    la_hi, la_lo, b, bl = _gla_decay_terms(alr_ref, a2_ref, ab_ref, tril, same)
    q = qk_ref[:, :GLA_QK].astype(F32) * (GLA_DK ** -0.5)
    k = qk_ref[:, GLA_QK:].astype(F32)
    q_t = (q * jnp.exp(b)).astype(BF16)
    k_t = (k * jnp.exp(-b)).astype(BF16)
    k_end = k * jnp.exp(bl - b)
    la_hi = la_hi.astype(F32)
    la_lo = la_lo.astype(F32)
    row_seq = lax.broadcasted_iota(jnp.int32, (n, GLA_DK), 0) >> 3
    seq3 = lax.broadcasted_iota(jnp.int32, (SEQ_PER_BLK, GLA_DK, n), 0)
    lane_seq3 = lax.broadcasted_iota(jnp.int32, (SEQ_PER_BLK, GLA_DK, n), 2) >> 3
    mask3 = seq3 == lane_seq3
    ones = jnp.ones((n, GLA_DV), BF16)
    big = SEQ_PER_BLK * GLA_DK

    def per_seq(x_tr):
        x3 = jnp.where(mask3, x_tr[None, :, :], 0.0)
        return x3.reshape(big, n).astype(BF16)

    for hd in range(GLA_HEADS):
        ks = slice(hd * GLA_DK, (hd + 1) * GLA_DK)
        vs = slice(hd * GLA_DV, (hd + 1) * GLA_DV)
        qh = q_t[:, ks]
        vh = v_ref[:, vs]
        att = jnp.where(tril, _dot_nt(qh, k_t[:, ks]), 0.0).astype(BF16)
        o_intra = _dot(att, vh)
        s_old = s0_ref[:, hd].reshape(big, GLA_DV)
        zero = jnp.zeros_like(qh)
        q_big = jnp.concatenate(
            [jnp.where(row_seq == j, qh, zero) for j in range(SEQ_PER_BLK)], axis=1)
        o = o_intra + _dot(q_big, s_old.astype(BF16))
        dlog = _dot(per_seq(la_hi[:, ks].T), ones) + _dot(per_seq(la_lo[:, ks].T), ones)
        upd = _dot(per_seq(k_end[:, ks].T), vh)
        s_new = jnp.exp(dlog) * s_old + upd
        ns_ref[:, hd] = s_new.reshape(SEQ_PER_BLK, GLA_DK, GLA_DV)
        r = r_ref[:, vs].astype(F32)
        acts_ref[:, CONV_CH + hd * GLA_DV:CONV_CH + (hd + 1) * GLA_DV] = (
            _gla_out_gate(o, gng_ref, r).astype(BF16))

    us, vg = _chunk_mlp(bgu_ref[:, CONV_CH:].astype(F32), vv_ref[...].astype(F32),
                        cmg_ref, ws_ref, cmb_ref, 1)
    acts_ref[:, CONV_CH + GLA_V:] = us.astype(BF16)
    vrow_ref[...] = vg


N_MIX_S_IN = 17


def _mix_sample_kernel(*refs, layer):
    if layer == 0:
        @pl.when(pl.program_id(0) == 0)
        def _():
            _mix_sample_body(*refs)

        @pl.when(pl.program_id(0) > 0)
        def _():
            ns_ref = refs[N_MIX_S_IN + 2]
            ns_ref[...] = jnp.zeros_like(ns_ref)
    else:
        _mix_sample_body(*refs[:N_MIX_S_IN], *refs[N_MIX_S_IN + 1:])


def _mix_sample(z, alr, a2, ab, cw, gng, cmg, ws, cmb, p1, p2, s0, ns_all, layer):
    row0 = T_PROMPT // ROWS_S
    n_i = DEC_BATCH // SEQ_PER_BLK
    n_pass = DEPTH if layer == 0 else 1
    blk = lambda p, i: jnp.where(p == 0, i, n_i - 1)
    row_map = lambda p, i: row0 + blk(p, i)
    slot = lambda p, i: (layer + p, i, 0, 0, 0)
    small = (a2, ab, cw, gng, cmg, ws, cmb)
    state_blk = (None, SEQ_PER_BLK, GLA_HEADS, GLA_DK, GLA_DV)
    in_specs = _z_specs(ROWS_S, row_map) + [
        pl.BlockSpec((ROWS_S, LANES), lambda p, i: (row_map(p, i), 0)),
    ] + [_layer_spec(a, layer) for a in small] + [
        pl.BlockSpec((ROWS_S, CONV_CH), lambda p, i: (blk(p, i), 0)),
        pl.BlockSpec((ROWS_S, CONV_CH), lambda p, i: (blk(p, i), 0)),
        pl.BlockSpec(state_blk, lambda p, i: (layer, blk(p, i), 0, 0, 0)),
    ]
    args = (z, z, z, z, z, z, alr, *small, p1, p2, s0)
    assert len(args) == N_MIX_S_IN
    aliases = {}
    if layer > 0:
        in_specs.append(pl.BlockSpec(memory_space=pl.ANY))
        args += (ns_all,)
        aliases = {N_MIX_S_IN: 2}
    return pl.pallas_call(
        functools.partial(_mix_sample_kernel, layer=layer),
        grid=(n_pass, n_i),
        in_specs=in_specs,
        out_specs=[
            pl.BlockSpec((ROWS_S, ACT_COLS), lambda p, i: (blk(p, i), 0)),
            pl.BlockSpec((ROWS_S, CONV_CH), lambda p, i: (blk(p, i), 0)),
            pl.BlockSpec(state_blk, slot),
            pl.BlockSpec((ROWS_S, CM_CH), lambda p, i: (blk(p, i), 0)),
        ],
        out_shape=[
            jax.ShapeDtypeStruct((T_SAMPLE, ACT_COLS), BF16),
            jax.ShapeDtypeStruct((T_SAMPLE, CONV_CH), F32),
            jax.ShapeDtypeStruct((DEPTH, DEC_BATCH, GLA_HEADS, GLA_DK, GLA_DV), F32),
            jax.ShapeDtypeStruct((T_SAMPLE, CM_CH), F32),
        ],
        input_output_aliases=aliases,
        compiler_params=pltpu.CompilerParams(
            dimension_semantics=("arbitrary", "arbitrary"), vmem_limit_bytes=VMEM_LIMIT),
        name="mix_sample",
    )(*args)


def _outproj_kernel(actp_ref, acts_ref, gates_ref, xa_ref, xb_ref, pa_ref, pb_ref, pc_ref, wo_ref,
                    n2_ref, wrh_ref, wrl_ref, rb_ref, xo_ref, xg_ref, lrank_ref, seg_ref, cnt_ref,
                    *, n_xa):
    i = pl.program_id(0)
    acts = jnp.where(i < T_PROMPT // TM_OUT, actp_ref[...], acts_ref[...])
    x_in = jnp.where(i < n_xa, xa_ref[...], xb_ref[...])
    ya = _dot(acts[:, :CONV_CH], pa_ref[...])
    yb = _dot(acts[:, CONV_CH:CONV_CH + GLA_V], pb_ref[...])
    yc = _dot(acts[:, CONV_CH + GLA_V:], pc_ref[...])
    ga = _sigmoid(gates_ref[:, COL_GA:COL_GA + D_MODEL].astype(F32))
    gb = _sigmoid(gates_ref[:, COL_GB:COL_GB + D_MODEL].astype(F32))
    gc = _sigmoid(gates_ref[:, COL_GC:COL_GC + D_MODEL].astype(F32))
    mix = ga * ya + gb * yb + gc * yc
    x = x_in + _dot(mix.astype(BF16), wo_ref[...])
    xo_ref[...] = x
    xn = _rms(x, n2_ref[...])
    xg_ref[:, :D_MODEL] = xn

    hi, lo = _split_bf16(xn)
    logits = (_dot(hi, wrh_ref[...]) + _dot(lo, wrh_ref[...]) + _dot(hi, wrl_ref[...])
              + rb_ref[...])
    lane_i = lax.broadcasted_iota(jnp.int32, logits.shape, 1)
    lane = lane_i.astype(F32)
    neg = jnp.float32(-jnp.inf)
    is_g = lane_i < MOE_GROUPS
    lg = jnp.where(is_g, logits, neg)
    gmax = jnp.max(lg, axis=-1, keepdims=True)
    grp = jnp.min(jnp.where(lg == gmax, lane, 1e9), axis=-1, keepdims=True)
    p_grp = 1.0 / jnp.sum(jnp.where(is_g, jnp.exp(lg - gmax), 0.0), axis=-1, keepdims=True)
    first = MOE_GROUPS + EXPERTS_PER_GROUP * grp
    in_grp = (lane >= first) & (lane < first + EXPERTS_PER_GROUP)
    le = jnp.where(in_grp, logits, neg)
    v1 = jnp.max(le, axis=-1, keepdims=True)
    i1 = jnp.min(jnp.where(le == v1, lane, 1e9), axis=-1, keepdims=True)
    le2 = jnp.where(lane == i1, neg, le)
    v2 = jnp.max(le2, axis=-1, keepdims=True)
    i2 = jnp.min(jnp.where(le2 == v2, lane, 1e9), axis=-1, keepdims=True)
    t = jnp.exp(v2 - v1)
    g1 = p_grp / (1.0 + t)
    g2 = p_grp * t / (1.0 + t)
    xg_ref[:, D_MODEL:] = jnp.where(lane == i1 - first, g1, jnp.where(lane == i2 - first, g2, 0.0))

    @pl.when(i == 0)
    def _():
        cnt_ref[...] = jnp.zeros_like(cnt_ref)

    tm = TM_OUT
    onehot = jnp.where(lane == grp, 1.0, 0.0)
    csum = jnp.sum(onehot, axis=0, keepdims=True)
    rr = lax.broadcasted_iota(jnp.int32, (tm, tm), 0)
    cc = lax.broadcasted_iota(jnp.int32, (tm, tm), 1)
    before = jnp.where(cc < rr, 1.0, 0.0).astype(BF16)
    prefix = _dot(before, onehot.astype(BF16))
    local = _lane_prefix(csum)
    lrank = jnp.sum(onehot * (prefix + local), axis=-1, keepdims=True)
    lrank_ref[...] = jnp.where(lane_i == 0, lrank, 0.0).astype(jnp.int32)
    carry = cnt_ref[0:1, :]
    row = lax.broadcasted_iota(jnp.int32, (8, LANES), 0)
    seg_ref[...] = jnp.where(
        row == SEG_START, carry,
        jnp.where(row == SEG_LEN, csum, jnp.where(row == SEG_LOCAL, local, 0.0))
    ).astype(jnp.int32)
    cnt_ref[0:1, :] = carry + csum


def _lane_prefix(v):
    rr = lax.broadcasted_iota(jnp.int32, (LANES, LANES), 0)
    cc = lax.broadcasted_iota(jnp.int32, (LANES, LANES), 1)
    earlier = jnp.where(rr < cc, 1.0, 0.0).astype(BF16)
    return _dot(jnp.broadcast_to(v, (8, LANES)).astype(BF16), earlier)[0:1]


def _outproj(acts_p, acts_s, z, xa, xb, pa, pb, pc, wo, n2, wrh, wrl, rb, layer):
    m = T_ALL
    row = lambda i: (i, 0)
    _, act_specs = _two_part_specs(TM_OUT, acts_p, acts_s)
    n_xa, x_specs = _two_part_specs(TM_OUT, xa, xb)
    weights = (pa, pb, pc, wo, n2, wrh, wrl, rb)
    return pl.pallas_call(
        functools.partial(_outproj_kernel, n_xa=n_xa),
        grid=(m // TM_OUT,),
        in_specs=act_specs + [pl.BlockSpec((TM_OUT, 3 * D_MODEL), row)] + x_specs
        + [_layer_spec(w, layer) for w in weights],
        out_specs=[
            pl.BlockSpec((TM_OUT, D_MODEL), row),
            pl.BlockSpec((TM_OUT, XG_COLS), row),
            pl.BlockSpec((TM_OUT, LANES), row),
            pl.BlockSpec((None, 8, LANES), lambda i: (i, 0, 0)),
            pl.BlockSpec((8, LANES), lambda i: (0, 0)),
        ],
        out_shape=[
            jax.ShapeDtypeStruct((m, D_MODEL), F32),
            jax.ShapeDtypeStruct((m, XG_COLS), F32),
            jax.ShapeDtypeStruct((m, LANES), jnp.int32),
            jax.ShapeDtypeStruct((m // TM_OUT, 8, LANES), jnp.int32),
            jax.ShapeDtypeStruct((8, LANES), F32),
        ],
        compiler_params=pltpu.CompilerParams(
            dimension_semantics=("arbitrary",), vmem_limit_bytes=VMEM_LIMIT),
        name="outproj",
    )(acts_p, acts_s, z, xa, xb, *weights)


PLAN_GROUP, PLAN_USED, PLAN_END, PLAN_BLOCKS, PLAN_BASE = 0, 1, 2, 3, 4


def _plan_kernel(cnt_ref, plan_ref):
    blocks = jnp.floor((cnt_ref[0:1, :] + (MOE_BLK - 1)) * (1.0 / MOE_BLK))
    start = _lane_prefix(blocks)
    end = start + blocks
    lane_i = lax.broadcasted_iota(jnp.int32, (1, LANES), 1)
    lane = lane_i.astype(F32)
    grp_of_blk = jnp.zeros((1, LANES), F32)
    for g in range(MOE_GROUPS):
        end_g = jnp.sum(jnp.where(lane_i == g, end, 0.0), axis=-1, keepdims=True)
        grp_of_blk += jnp.where(lane >= end_g, 1.0, 0.0)
    grp_of_blk = jnp.minimum(grp_of_blk, MOE_GROUPS - 1)
    n_used = jnp.sum(blocks, axis=-1, keepdims=True)
    row = lax.broadcasted_iota(jnp.int32, (8, LANES), 0)
    plan_ref[...] = jnp.where(
        row == PLAN_GROUP, grp_of_blk,
        jnp.where(row == PLAN_USED, n_used,
                  jnp.where(row == PLAN_END, end,
                            jnp.where(row == PLAN_BLOCKS, blocks,
                                      jnp.where(row == PLAN_BASE, start * MOE_BLK, 0.0))))
    ).astype(jnp.int32)


def _plan(cnt):
    return pl.pallas_call(
        _plan_kernel,
        out_shape=jax.ShapeDtypeStruct((8, LANES), jnp.int32),
        name="plan",
    )(cnt)


SEG_START, SEG_LEN, SEG_LOCAL = 0, 1, 2


def _segment_copies(seg_ref, base_ref, make_copy):
    for g in range(MOE_GROUPS):
        n = seg_ref[SEG_LEN, g]
        k = TM_ROW
        while k >= 1:
            done = n & ~(2 * k - 1)
            @pl.when((n & k) != 0)
            def _():
                make_copy(seg_ref[SEG_LOCAL, g] + done,
                          base_ref[g] + seg_ref[SEG_START, g] + done, k).start()
            k //= 2


def _perm_matrix(lrank_ref):
    col = lax.broadcasted_iota(jnp.int32, (TM_ROW, TM_ROW), 1)
    return jnp.where(col == lrank_ref[:, 0:1], 1.0, 0.0)


SUB = 8
SUB_X = D_MODEL // 2 // LANES
SUB_GATE = SUB_X
HI_MASK = -65536


def _rows(first_row, n_rows):
    return pl.ds(pl.multiple_of(first_row * SUB, SUB), n_rows * SUB)


def _sublane(s, n_rows):
    return pl.ds(s, n_rows, stride=SUB)


def _scatter_kernel(nu_ref, end_ref, nb_ref, base_ref, seg_ref, lrank_ref, xg_ref, xs_ref,
                    zbuf, sbuf, sems):
    i = pl.program_id(0)
    n = pl.num_programs(0)
    slot = i % 2
    sem = sems.at[0]

    def zero_block(b):
        return pltpu.make_async_copy(zbuf, xs_ref.at[_rows(b * MOE_BLK, MOE_BLK)], sem)

    def each_unfilled_block(fn):
        for g in range(MOE_GROUPS):
            @pl.when(nb_ref[g] > 0)
            def _():
                fn(zero_block(end_ref[g] - 1))
        for b in range(T_ALL // MOE_BLK, N_BLK):
            @pl.when(b >= nu_ref[0])
            def _():
                fn(zero_block(b))

    @pl.when(i == 0)
    def _():
        zbuf[...] = jnp.zeros_like(zbuf)
        sbuf[...] = jnp.zeros_like(sbuf)
        each_unfilled_block(lambda c: c.start())
        each_unfilled_block(lambda c: c.wait())

    def tile_done(s):
        pltpu.make_async_copy(sbuf.at[s], xs_ref.at[_rows(0, TM_ROW)], sems.at[s]).wait()

    @pl.when(i >= 2)
    def _():
        tile_done(slot)

    perm = _perm_matrix(lrank_ref).T.astype(BF16)
    xs = lax.bitcast_convert_type(_dot(perm, xg_ref[:, :D_MODEL].astype(BF16)), jnp.int32)
    half = D_MODEL // 2
    for s in range(SUB_X):
        hi = xs[:, s * LANES:(s + 1) * LANES] & HI_MASK
        lo = lax.shift_right_logical(xs[:, half + s * LANES:half + (s + 1) * LANES], 16)
        sbuf[slot, _sublane(s, TM_ROW), :] = hi | lo
    gate = xg_ref[:, D_MODEL:]
    g1 = gate.astype(BF16)
    r1 = gate - g1.astype(F32)
    g2 = r1.astype(BF16)
    g3 = (r1 - g2.astype(F32)).astype(BF16)
    sbuf[slot, _sublane(SUB_GATE, TM_ROW), :] = lax.bitcast_convert_type(
        _dot(perm, g1) + _dot(perm, g2) + _dot(perm, g3), jnp.int32)

    def make_copy(src_row, dst_row, k):
        return pltpu.make_async_copy(sbuf.at[slot, _rows(src_row, k)],
                                     xs_ref.at[_rows(dst_row, k)], sems.at[slot])
    _segment_copies(seg_ref, base_ref, make_copy)

    @pl.when(i == n - 1)
    def _():
        tile_done(slot)

        @pl.when(n > 1)
        def _():
            tile_done(1 - slot)


def _scatter(seg, lrank, xg, n_used, grp_end, grp_blocks, grp_base):
    m = xg.shape[0]
    nt = m // TM_ROW
    grid_spec = pltpu.PrefetchScalarGridSpec(
        num_scalar_prefetch=4,
        grid=(nt,),
        in_specs=[
            pl.BlockSpec((None, 8, LANES), lambda i, *_: (i, 0, 0), memory_space=pltpu.SMEM),
            pl.BlockSpec((TM_ROW, LANES), lambda i, *_: (i, 0)),
            pl.BlockSpec((TM_ROW, XG_COLS), lambda i, *_: (i, 0)),
        ],
        out_specs=pl.BlockSpec(memory_space=pl.ANY),
        scratch_shapes=[pltpu.VMEM((MOE_BLK * SUB, LANES), jnp.int32),
                        pltpu.VMEM((2, TM_ROW * SUB, LANES), jnp.int32),
                        pltpu.SemaphoreType.DMA((2,))],
    )
    return pl.pallas_call(
        _scatter_kernel,
        grid_spec=grid_spec,
        out_shape=jax.ShapeDtypeStruct((N_SORTED * SUB, LANES), jnp.int32),
        compiler_params=pltpu.CompilerParams(
            dimension_semantics=("arbitrary",), vmem_limit_bytes=VMEM_LIMIT),
        name="scatter",
    )(n_used, grp_end, grp_blocks, grp_base, seg, lrank, xg)


def _ffn_kernel(bg_ref, nu_ref, xs_ref, w1_ref, w3_ref, w2_ref, y_ref):
    del bg_ref
    b = pl.program_id(0)

    @pl.when(b < nu_ref[0])
    def _():
        packed = [xs_ref[_sublane(s, MOE_BLK), :] for s in range(SUB_X)]
        x = jnp.concatenate(
            [lax.bitcast_convert_type(u & HI_MASK, F32).astype(BF16) for u in packed]
            + [lax.bitcast_convert_type(lax.shift_left(u, 16), F32).astype(BF16) for u in packed],
            axis=1)
        gates = lax.bitcast_convert_type(xs_ref[_sublane(SUB_GATE, MOE_BLK), :], F32)
        hs = []
        for e in range(EXPERTS_PER_GROUP):
            h1 = _dot(x, w1_ref[e])
            h3 = _dot(x, w3_ref[e])
            ge = gates[:, e:e + 1]
            hs.append(jnp.where(ge > 0.0, h1 * _sigmoid(h1) * h3 * ge, 0.0).astype(BF16))
        hcat = jnp.concatenate(hs, axis=1)
        y = _dot(hcat, w2_ref[...].reshape(EXPERTS_PER_GROUP * D_EXPERT, D_MODEL))
        for s in range(SUB):
            y_ref[_sublane(s, MOE_BLK), :] = y[:, s * LANES:(s + 1) * LANES]

    @pl.when(b >= nu_ref[0])
    def _():
        y_ref[...] = jnp.zeros_like(y_ref)


def _ffn(blk_group, n_used, xs, w1, w3, w2, layer):
    wmap = lambda b, bg, nu: (layer * MOE_GROUPS + bg[b], 0, 0, 0)
    grid_spec = pltpu.PrefetchScalarGridSpec(
        num_scalar_prefetch=2,
        grid=(N_BLK,),
        in_specs=[
            pl.BlockSpec((MOE_BLK * SUB, LANES), lambda b, bg, nu: (b, 0)),
            pl.BlockSpec((None, EXPERTS_PER_GROUP, D_MODEL, D_EXPERT), wmap),
            pl.BlockSpec((None, EXPERTS_PER_GROUP, D_MODEL, D_EXPERT), wmap),
            pl.BlockSpec((None, EXPERTS_PER_GROUP, D_EXPERT, D_MODEL), wmap),
        ],
        out_specs=pl.BlockSpec((MOE_BLK * SUB, LANES), lambda b, bg, nu: (b, 0)),
    )
    return pl.pallas_call(
        _ffn_kernel,
        grid_spec=grid_spec,
        out_shape=jax.ShapeDtypeStruct((N_SORTED * SUB, LANES), F32),
        compiler_params=pltpu.CompilerParams(
            dimension_semantics=("arbitrary",), vmem_limit_bytes=VMEM_LIMIT),
        name="ffn",
    )(blk_group, n_used, xs, w1, w3, w2)


def _combine_kernel(base_ref, seg_ref, segn_ref, lrank_ref, ys_hbm, x_ref, g_ref, *refs, final):
    outs, (buf, sem) = refs[:-2], refs[-2:]
    i = pl.program_id(0)
    n = pl.num_programs(0)
    slot = i % 2

    def gather(seg, s):
        def make_copy(buf_row, ys_row, k):
            return pltpu.make_async_copy(ys_hbm.at[_rows(ys_row, k)],
                                         buf.at[s, _rows(buf_row, k)], sem.at[s])
        _segment_copies(seg, base_ref, make_copy)

    @pl.when(i == 0)
    def _():
        gather(seg_ref, 0)

    @pl.when(i + 1 < n)
    def _():
        gather(segn_ref, 1 - slot)

    pltpu.make_async_copy(ys_hbm.at[_rows(0, TM_ROW)], buf.at[slot], sem.at[slot]).wait()

    pt = _perm_matrix(lrank_ref).astype(BF16)
    cols = []
    for s in range(SUB):
        hi, lo = _split_bf16(buf[slot, _sublane(s, TM_ROW), :])
        cols.append(_dot(pt, hi) + _dot(pt, lo))
    y = x_ref[...] + jnp.concatenate(cols, axis=1)
    if not final:
        outs[0][...] = y
    else:
        y = _rms(y, g_ref[...])

        @pl.when(i < T_PROMPT // TM_ROW)
        def _():
            outs[0][...] = y

        @pl.when(i >= T_PROMPT // TM_ROW)
        def _():
            outs[1][...] = y


def _combine(seg, lrank, grp_base, ys, x, g, final):
    m = x.shape[0]
    nt = m // TM_ROW
    n_p = T_PROMPT // TM_ROW
    smem = functools.partial(pl.BlockSpec, (None, 8, LANES), memory_space=pltpu.SMEM)
    tile = (TM_ROW, D_MODEL)
    if final:
        out_specs = [pl.BlockSpec(tile, lambda i, *_: (jnp.minimum(i, n_p - 1), 0)),
                     pl.BlockSpec(tile, lambda i, *_: (jnp.maximum(i - n_p, 0), 0))]
        out_shape = [jax.ShapeDtypeStruct((T_PROMPT, D_MODEL), F32),
                     jax.ShapeDtypeStruct((T_SAMPLE, D_MODEL), F32)]
    else:
        out_specs = [pl.BlockSpec(tile, lambda i, *_: (i, 0))]
        out_shape = [jax.ShapeDtypeStruct((m, D_MODEL), F32)]
    grid_spec = pltpu.PrefetchScalarGridSpec(
        num_scalar_prefetch=1,
        grid=(nt,),
        in_specs=[
            smem(lambda i, *_: (i, 0, 0)),
            smem(lambda i, *_: (jnp.minimum(i + 1, nt - 1), 0, 0)),
            pl.BlockSpec((TM_ROW, LANES), lambda i, *_: (i, 0)),
            pl.BlockSpec(memory_space=pl.ANY),
            pl.BlockSpec(tile, lambda i, *_: (i, 0)),
            pl.BlockSpec(g.shape, lambda i, *_: (0, 0)),
        ],
        out_specs=out_specs,
        scratch_shapes=[pltpu.VMEM((2, TM_ROW * SUB, LANES), F32),
                        pltpu.SemaphoreType.DMA((2,))],
    )
    return pl.pallas_call(
        functools.partial(_combine_kernel, final=final),
        grid_spec=grid_spec,
        out_shape=out_shape,
        compiler_params=pltpu.CompilerParams(
            dimension_semantics=("arbitrary",), vmem_limit_bytes=VMEM_LIMIT),
        name="combine",
    )(grp_base, seg, seg, lrank, ys, x, g)


def _prep_weights(w_in, gla_a2, cm_ws, cm_b, router_group_w, router_group_b,
                  router_expert_w, router_expert_b):
    off = {}
    o = 0
    for name, n in (("h", 512), ("cg", 512), ("bg", 512), ("q", 512), ("k", 512), ("v", 1024),
                    ("r", 1024), ("alr", 16), ("u", 512), ("vv", 512), ("ga", 1024),
                    ("gb", 1024), ("gc", 1024)):
        off[name] = (o, n)
        o += n
    order = ("ga", "gb", "gc", "v", "r", "q", "k", "h", "cg", "bg", "u", "vv")
    w_t = jnp.swapaxes(w_in, 1, 2)
    w_z = jnp.concatenate([w_t[:, off[n][0]:off[n][0] + off[n][1]] for n in order],
                          axis=1).astype(BF16)
    a0 = off["alr"][0]
    w_alr = jnp.pad(w_t[:, a0:a0 + GLA_LOWRANK],
                    ((0, 0), (0, LANES - GLA_LOWRANK), (0, 0))).astype(BF16)
    a2 = jnp.pad(gla_a2, ((0, 0), (0, LANES - GLA_LOWRANK), (0, 0))).astype(BF16)
    ws_p = jnp.tril(cm_ws).astype(BF16)
    small = jnp.tril(cm_ws[:, :, :DEC_SEQ, :DEC_SEQ])
    eye = jnp.eye(SEQ_PER_BLK, dtype=F32)
    ws_s = jnp.einsum("ij,lgab->lgiajb", eye, small).reshape(
        DEPTH, CM_GROUPS, ROWS_S, ROWS_S).astype(BF16)
    cmb_p = jnp.broadcast_to(jnp.transpose(cm_b, (0, 2, 1))[:, :, :, None],
                             (DEPTH, CM_CHUNK, CM_GROUPS, CM_GCH)).reshape(DEPTH, CM_CHUNK, CM_CH)
    cmb_s = jnp.tile(cmb_p[:, :DEC_SEQ], (1, SEQ_PER_BLK, 1))
    pad = LANES - MOE_GROUPS - N_EXPERTS
    w_r = jnp.pad(jnp.concatenate([router_group_w, router_expert_w], axis=-1),
                  ((0, 0), (0, 0), (0, pad)))
    w_r_hi = w_r.astype(BF16)
    w_r_lo = (w_r - w_r_hi.astype(F32)).astype(BF16)
    r_b = jnp.pad(jnp.concatenate([router_group_b, router_expert_b], axis=-1),
                  ((0, 0), (0, pad)))[:, None, :]
    return w_z, w_alr, a2, ws_p, ws_s, cmb_p, cmb_s, w_r_hi, w_r_lo, r_b


def kernel(x_prompt, x_sample, state_conv, state_gla, norm1_g, w_in, conv_w, gla_a2, gla_a_b,
           gla_norm_g, cm_norm_g, cm_ws, cm_b, proj_a, proj_b, proj_c, w_out, norm2_g,
           router_group_w, router_group_b, router_expert_w, router_expert_b,
           exp_w1, exp_w3, exp_w2, final_norm_g):
    (w_z, w_alr, a2, ws_p, ws_s, cmb_p, cmb_s, w_r_hi, w_r_lo, r_b) = _prep_weights(
        w_in, gla_a2, cm_ws, cm_b, router_group_w, router_group_b, router_expert_w,
        router_expert_b)
    pa, pb, pc, wo = (w.astype(BF16) for w in (proj_a, proj_b, proj_c, w_out))
    grouped = (DEPTH * MOE_GROUPS, EXPERTS_PER_GROUP)
    w1 = exp_w1.astype(BF16).reshape(grouped + (D_MODEL, D_EXPERT))
    w3 = exp_w3.astype(BF16).reshape(grouped + (D_MODEL, D_EXPERT))
    w2 = exp_w2.astype(BF16).reshape(grouped + (D_EXPERT, D_MODEL))
    n1 = norm1_g[:, None, :]
    n2 = norm2_g[:, None, :]
    ab = gla_a_b[:, None, :]
    gng = gla_norm_g[:, None, :]
    cmg = cm_norm_g.reshape(DEPTH, 1, CM_CH)
    fg = final_norm_g[None, :]
    xa = x_prompt.reshape(T_PROMPT, D_MODEL)
    xb = x_sample.reshape(T_SAMPLE, D_MODEL)
    gla_s = None
    conv_p, gla_p, conv_s, cmv_s = [], [], [], []
    for l in range(DEPTH):
        z, alr = _inproj(xa, xb, n1, w_z, w_alr, l)
        acts_p, nconv, ngla = _mix_prompt(z, alr, a2, ab, conv_w, gng, cmg, ws_p, cmb_p, l)
        sc = state_conv[l]
        p2 = jnp.pad(sc, ((0, 0), (0, DEC_SEQ - 2), (0, 0))).reshape(T_SAMPLE, CONV_CH)
        p1 = jnp.pad(sc[:, 1:2], ((0, 0), (0, DEC_SEQ - 1), (0, 0))).reshape(T_SAMPLE, CONV_CH)
        acts_s, cin_s, gla_s, vrows = _mix_sample(z, alr, a2, ab, conv_w, gng, cmg, ws_s, cmb_s,
                                                  p1, p2, state_gla, gla_s, l)
        conv_p.append(nconv)
        gla_p.append(ngla)
        conv_s.append(cin_s.reshape(DEC_BATCH, DEC_SEQ, CONV_CH)[:, DEC_SEQ - (CONV_K - 1):])
        cmv_s.append(vrows.reshape(DEC_BATCH, DEC_SEQ, CM_CH))

        x, xg, lrank, seg, cnt = _outproj(acts_p, acts_s, z, xa, xb, pa, pb, pc, wo, n2,
                                          w_r_hi, w_r_lo, r_b, l)
        plan = _plan(cnt)
        n_used = plan[PLAN_USED, :1]
        grp_base = plan[PLAN_BASE, :MOE_GROUPS]
        xs = _scatter(seg, lrank, xg, n_used, plan[PLAN_END, :MOE_GROUPS],
                      plan[PLAN_BLOCKS, :MOE_GROUPS], grp_base)
        ys = _ffn(plan[PLAN_GROUP, :N_BLK], n_used, xs, w1, w3, w2, l)
        out = _combine(seg, lrank, grp_base, ys, x, fg, l == DEPTH - 1)
        xa = xb = out[0]

    y_prompt = out[0].reshape(BATCH, SEQ, D_MODEL)
    y_sample = out[1].reshape(DEC_BATCH, DEC_SEQ, D_MODEL)
    return (y_prompt, y_sample, jnp.stack(conv_p), jnp.stack(gla_p), jnp.stack(conv_s),
            gla_s, jnp.stack(cmv_s))
```

```python
import functools

import jax
import jax.numpy as jnp
from jax import lax
from jax.experimental import pallas as pl
from jax.experimental.pallas import tpu as pltpu

F32 = jnp.float32
BF16 = jnp.bfloat16

D_MODEL = 1024
BATCH = 8
SEQ = 2048
DEPTH = 2
DEC_BATCH = 128
DEC_SEQ = 8
CONV_K = 3
CONV_CH = 512
GLA_HEADS = 4
GLA_DK = 128
GLA_DV = 256
GLA_QK = GLA_HEADS * GLA_DK
GLA_V = GLA_HEADS * GLA_DV
GLA_LOWRANK = 16
GLA_TAU = 16.0
GLA_CHUNK = 64
CM_GROUPS = 4
CM_CHUNK = 128
CM_GCH = 128
CM_CH = 512
MOE_GROUPS = 8
EXPERTS_PER_GROUP = 8
N_EXPERTS = 64
D_EXPERT = 256
EPS = 1e-6

LANES = 128
T_PROMPT = BATCH * SEQ
T_SAMPLE = DEC_BATCH * DEC_SEQ
T_ALL = T_PROMPT + T_SAMPLE

COL_GA, COL_GB, COL_GC = 0, 1024, 2048
COL_V, COL_R, COL_Q, COL_K = 3072, 4096, 5120, 5632
COL_H, COL_CG, COL_BG, COL_U, COL_VV = 6144, 6656, 7168, 7680, 8192
Z_COLS = 8704
ACT_COLS = 2048
XG_COLS = D_MODEL + LANES

TM_IN = 1024
TN_IN = Z_COLS // 4
TC_MIX = 256
SEQ_PER_BLK = 16
ROWS_S = SEQ_PER_BLK * DEC_SEQ
TM_OUT = 256
OUT_PARTS = 1
TM_ROW = TM_OUT
MOE_BLK = 256
N_BLK = T_ALL // MOE_BLK + MOE_GROUPS
N_SORTED = N_BLK * MOE_BLK
VMEM_LIMIT = 56 * 1024 * 1024


def _sigmoid(x):
    return 0.5 * jnp.tanh(0.5 * x) + 0.5


def _gelu_tanh(x):
    c = 0.7978845608028654
    half = 0.5 * x
    return half + half * jnp.tanh(x * (c + (c * 0.044715) * (x * x)))


def _log_sigmoid(x):
    log2_e = 1.4426950408889634
    ln_2 = 0.6931471805599453
    return jnp.minimum(x, 0.0) - ln_2 * jnp.log2(1.0 + jnp.exp2(-log2_e * jnp.abs(x)))


def _rms(x, g):
    ms = jnp.mean(x * x, axis=-1, keepdims=True)
    return x * lax.rsqrt(ms + EPS) * g


def _split_bf16(x):
    hi = x.astype(BF16)
    lo = (x - hi.astype(F32)).astype(BF16)
    return hi, lo


def _dot(a, b):
    return jnp.dot(a, b, preferred_element_type=F32)


def _dot_nt(a, b):
    return lax.dot_general(a, b, (((1,), (1,)), ((), ())), preferred_element_type=F32)


def _layer_spec(arr, layer):
    nd = arr.ndim - 1
    return pl.BlockSpec((None,) + arr.shape[1:], lambda *g: (layer,) + (0,) * nd)


def _const_spec(arr):
    nd = arr.ndim
    return pl.BlockSpec(arr.shape, lambda *g: (0,) * nd)


def _two_part_specs(tile, xa, xb):
    n_a = xa.shape[0] // tile
    return n_a, [
        pl.BlockSpec((tile, xa.shape[1]), lambda i, *_: (jnp.minimum(i, n_a - 1), 0)),
        pl.BlockSpec((tile, xb.shape[1]), lambda i, *_: (jnp.maximum(i - n_a, 0), 0)),
    ]


def _inproj_kernel(xa_ref, xb_ref, g_ref, w_ref, wa_ref, z_ref, a_ref, xn_ref, *, n_a):
    @pl.when(pl.program_id(1) == 0)
    def _():
        x = jnp.where(pl.program_id(0) < n_a, xa_ref[...], xb_ref[...])
        xn = _rms(x, g_ref[...]).astype(BF16)
        xn_ref[...] = xn
        a_ref[...] = _dot_nt(xn, wa_ref[...]).astype(BF16)

    z_ref[...] = _dot_nt(xn_ref[...], w_ref[...]).astype(BF16)


def _inproj(xa, xb, g, w, wa, layer):
    m = T_ALL
    n_a, x_specs = _two_part_specs(TM_IN, xa, xb)
    return pl.pallas_call(
        functools.partial(_inproj_kernel, n_a=n_a),
        grid=(m // TM_IN, Z_COLS // TN_IN),
        in_specs=x_specs + [
            _layer_spec(g, layer),
            pl.BlockSpec((None, TN_IN, D_MODEL), lambda i, j: (layer, j, 0)),
            _layer_spec(wa, layer),
        ],
        out_specs=[
            pl.BlockSpec((TM_IN, TN_IN), lambda i, j: (i, j)),
            pl.BlockSpec((TM_IN, LANES), lambda i, j: (i, 0)),
        ],
        out_shape=[
            jax.ShapeDtypeStruct((m, Z_COLS), BF16),
            jax.ShapeDtypeStruct((m, LANES), BF16),
        ],
        scratch_shapes=[pltpu.VMEM((TM_IN, D_MODEL), BF16)],
        compiler_params=pltpu.CompilerParams(
            dimension_semantics=("arbitrary", "arbitrary"), vmem_limit_bytes=VMEM_LIMIT),
        name="inproj",
    )(xa, xb, g, w, wa)


def _gla_log_decay(alr_ref, a2_ref, ab_ref):
    la = _log_sigmoid(_dot(alr_ref[...], a2_ref[...]) + ab_ref[...]) * (1.0 / GLA_TAU)
    return _split_bf16(la)


def _masked_sum(mask, la_hi, la_lo):
    m = jnp.where(mask, 1.0, 0.0).astype(BF16)
    return _dot(m, la_hi) + _dot(m, la_lo)


def _gla_decay_prefix(alr_ref, a2_ref, ab_ref, tril_mask):
    la_hi, la_lo = _gla_log_decay(alr_ref, a2_ref, ab_ref)
    return _masked_sum(tril_mask, la_hi, la_lo)


def _gla_decay_terms(alr_ref, a2_ref, ab_ref, tril_mask, same_mask):
    la_hi, la_lo = _gla_log_decay(alr_ref, a2_ref, ab_ref)
    return (la_hi, la_lo, _masked_sum(tril_mask, la_hi, la_lo),
            _masked_sum(same_mask, la_hi, la_lo))


def _gla_out_gate(o, g_ref, r):
    return _rms(o, g_ref[...]) * (r * _sigmoid(r))


def _chunk_mlp_group(g, bgu_ref, vv_ref, cmg_ref, ws_ref, cmb_ref, n_chunks):
    sl = slice(g * CM_GCH, (g + 1) * CM_GCH)
    ug = _gelu_tanh(bgu_ref[:, CONV_CH + g * CM_GCH:CONV_CH + (g + 1) * CM_GCH])
    vg = _rms(_gelu_tanh(vv_ref[:, sl]).astype(F32), cmg_ref[:, sl])
    vgb = vg.astype(BF16)
    rows = []
    for j in range(n_chunks):
        rs = slice(j * CM_CHUNK, (j + 1) * CM_CHUNK)
        rows.append(_dot(ws_ref[g], vgb[rs]) + cmb_ref[:, sl])
    s = rows[0] if n_chunks == 1 else jnp.concatenate(rows, axis=0)
    return ug * s, vg


def _chunk_mlp(bgu_ref, vv_ref, cmg_ref, ws_ref, cmb_ref, n_chunks):
    parts = [_chunk_mlp_group(g, bgu_ref, vv_ref, cmg_ref, ws_ref, cmb_ref, n_chunks)
             for g in range(CM_GROUPS)]
    return (jnp.concatenate([p[0] for p in parts], axis=1),
            jnp.concatenate([p[1] for p in parts], axis=1))


N_Z_VIEWS = 7
N_MIX_W = 7
SEQ_PER_STEP = 1


def _mix_prompt_kernel(*refs):
    n_z = N_Z_VIEWS * SEQ_PER_STEP
    weights = refs[n_z:n_z + N_MIX_W]
    acts_ref, nconv_ref, ngla_ref, st_ref, carry_ref = refs[n_z + N_MIX_W:]
    for s in range(SEQ_PER_STEP):
        _mix_prompt_seq(*refs[N_Z_VIEWS * s:N_Z_VIEWS * (s + 1)], *weights,
                        acts_ref.at[s], nconv_ref.at[s], ngla_ref.at[s], st_ref.at[s],
                        carry_ref.at[s])


def _mix_prompt_seq(v_ref, r_ref, qk_ref, hcg_ref, bgu_ref, vv_ref, alr_ref,
                    a2_ref, ab_ref, cw_ref, gng_ref, cmg_ref, ws_ref, cmb_ref,
                    acts_ref, nconv_ref, ngla_ref, st_ref, carry_ref):
    tc = TC_MIX

    @pl.when(pl.program_id(1) == 0)
    def _():
        st_ref[...] = jnp.zeros_like(st_ref)
        carry_ref[...] = jnp.zeros_like(carry_ref)

    row = lax.broadcasted_iota(jnp.int32, (tc, 1), 0)
    h = hcg_ref[:, :CONV_CH].astype(F32)
    cg = hcg_ref[:, CONV_CH:].astype(F32)
    bg = bgu_ref[:, :CONV_CH].astype(F32)
    cin = cg * h
    c0 = carry_ref[0:1, :]
    c1 = carry_ref[1:2, :]
    x1 = jnp.where(row >= 1, pltpu.roll(cin, 1, 0), c1)
    x2 = jnp.where(row >= 2, pltpu.roll(cin, 2, 0), jnp.where(row == 1, c1, c0))
    conv = x2 * cw_ref[0:1, :] + x1 * cw_ref[1:2, :] + cin * cw_ref[2:3, :]
    acts_ref[:, 0:CONV_CH] = (bg * conv).astype(BF16)
    carry_ref[0:2, :] = cin[tc - 2:tc, :]
    nconv_ref[0] = cin[tc - 2:tc, :]

    rr = lax.broadcasted_iota(jnp.int32, (tc, tc), 0)
    cc = lax.broadcasted_iota(jnp.int32, (tc, tc), 1)
    same = (rr >> 6) == (cc >> 6)
    tril = same & (cc <= rr)
    b = _gla_decay_prefix(alr_ref, a2_ref, ab_ref, tril)
    n_chunks = tc // GLA_CHUNK
    b_last = [b[(c + 1) * GLA_CHUNK - 1:(c + 1) * GLA_CHUNK, :] for c in range(n_chunks)]
    bl = jnp.concatenate([jnp.broadcast_to(r_, (GLA_CHUNK, GLA_QK)) for r_ in b_last], axis=0)
    q = qk_ref[:, :GLA_QK].astype(F32) * (GLA_DK ** -0.5)
    k = qk_ref[:, GLA_QK:].astype(F32)
    q_t = (q * jnp.exp(b)).astype(BF16)
    k_t = (k * jnp.exp(-b)).astype(BF16)
    k_end = (k * jnp.exp(bl - b)).astype(BF16)
    states = [st_ref[hd] for hd in range(GLA_HEADS)]
    k_cols = [slice(hd * GLA_DK, (hd + 1) * GLA_DK) for hd in range(GLA_HEADS)]
    v_cols = [slice(hd * GLA_DV, (hd + 1) * GLA_DV) for hd in range(GLA_HEADS)]
    o_intra = []
    for hd in range(GLA_HEADS):
        att = jnp.where(tril, _dot_nt(q_t[:, k_cols[hd]], k_t[:, k_cols[hd]]), 0.0).astype(BF16)
        o_intra.append(_dot(att, v_ref[:, v_cols[hd]]))

    o_rows = [[] for _ in range(GLA_HEADS)]
    assert n_chunks == CM_GROUPS
    for c in range(n_chunks):
        rs = slice(c * GLA_CHUNK, (c + 1) * GLA_CHUNK)
        for hd in range(GLA_HEADS):
            ks = k_cols[hd]
            st = states[hd]
            o_rows[hd].append(o_intra[hd][rs] + _dot_nt(q_t[rs, ks], st.astype(BF16)))
            upd = lax.dot_general(v_ref[rs, v_cols[hd]], k_end[rs, ks], (((0,), (0,)), ((), ())),
                                  preferred_element_type=F32)
            states[hd] = jnp.exp(b_last[c][:, ks]) * st + upd
        us, _ = _chunk_mlp_group(c, bgu_ref, vv_ref, cmg_ref, ws_ref, cmb_ref, tc // CM_CHUNK)
        acts_ref[:, CONV_CH + GLA_V + c * CM_GCH:CONV_CH + GLA_V + (c + 1) * CM_GCH] = (
            us.astype(BF16))

    for hd in range(GLA_HEADS):
        o = jnp.concatenate(o_rows[hd], axis=0)
        r = r_ref[:, v_cols[hd]]
        acts_ref[:, CONV_CH + hd * GLA_DV:CONV_CH + (hd + 1) * GLA_DV] = (
            _gla_out_gate(o, gng_ref, r).astype(BF16))
        st_ref[hd] = states[hd]

    @pl.when(pl.program_id(1) == pl.num_programs(1) - 1)
    def _():
        for hd in range(GLA_HEADS):
            ngla_ref[0, hd] = states[hd].T


def _z_specs(rows, row_map):
    def spec(width, col):
        blk = col // width
        return pl.BlockSpec((rows, width), lambda *g: (row_map(*g), blk))
    return [spec(1024, COL_V), spec(1024, COL_R), spec(1024, COL_Q), spec(1024, COL_H),
            spec(1024, COL_BG), spec(512, COL_VV)]


def _mix_prompt(z, alr, a2, ab, cw, gng, cmg, ws, cmb, layer):
    nt = SEQ // TC_MIX
    nb = BATCH // SEQ_PER_STEP
    small = (a2, ab, cw, gng, cmg, ws, cmb)
    assert len(small) == N_MIX_W
    in_specs, args = [], []
    for s in range(SEQ_PER_STEP):
        row_map = lambda b, c, s=s: (b + s * nb) * nt + c
        in_specs += _z_specs(TC_MIX, row_map) + [
            pl.BlockSpec((TC_MIX, LANES), lambda b, c, row_map=row_map: (row_map(b, c), 0))]
        args += [z] * (N_Z_VIEWS - 1) + [alr]
    in_specs += [_layer_spec(a, layer) for a in small]
    acts, nconv, ngla = pl.pallas_call(
        _mix_prompt_kernel,
        grid=(nb, nt),
        in_specs=in_specs,
        out_specs=[
            pl.BlockSpec((SEQ_PER_STEP, TC_MIX, ACT_COLS), lambda b, c: (0, b * nt + c, 0)),
            pl.BlockSpec((SEQ_PER_STEP, 1, CONV_K - 1, CONV_CH), lambda b, c: (0, b, 0, 0)),
            pl.BlockSpec((SEQ_PER_STEP, 1, GLA_HEADS, GLA_DK, GLA_DV),
                         lambda b, c: (0, b, 0, 0, 0)),
        ],
        out_shape=[
            jax.ShapeDtypeStruct((SEQ_PER_STEP, T_PROMPT // SEQ_PER_STEP, ACT_COLS), BF16),
            jax.ShapeDtypeStruct((SEQ_PER_STEP, nb, CONV_K - 1, CONV_CH), F32),
            jax.ShapeDtypeStruct((SEQ_PER_STEP, nb, GLA_HEADS, GLA_DK, GLA_DV), F32),
        ],
        scratch_shapes=[pltpu.VMEM((SEQ_PER_STEP, GLA_HEADS, GLA_DV, GLA_DK), F32),
                        pltpu.VMEM((SEQ_PER_STEP, 8, CONV_CH), F32)],
        compiler_params=pltpu.CompilerParams(
            dimension_semantics=("arbitrary", "arbitrary"), vmem_limit_bytes=VMEM_LIMIT),
        name="mix_prompt",
    )(*args, *small)
    return (acts.reshape(T_PROMPT, ACT_COLS), nconv.reshape(BATCH, CONV_K - 1, CONV_CH),
            ngla.reshape(BATCH, GLA_HEADS, GLA_DK, GLA_DV))


def _mix_sample_body(v_ref, r_ref, qk_ref, hcg_ref, bgu_ref, vv_ref, alr_ref,
                       a2_ref, ab_ref, cw_ref, gng_ref, cmg_ref, ws_ref, cmb_ref,
                       p1_ref, p2_ref, s0_ref,
                       acts_ref, cin_ref, ns_ref, vrow_ref):
    n = ROWS_S

    pos = lax.broadcasted_iota(jnp.int32, (n, 1), 0) & (DEC_SEQ - 1)
    h = hcg_ref[:, :CONV_CH].astype(F32)
    cg = hcg_ref[:, CONV_CH:].astype(F32)
    bg = bgu_ref[:, :CONV_CH].astype(F32)
    cin = cg * h
    x1 = jnp.where(pos >= 1, pltpu.roll(cin, 1, 0), p1_ref[...])
    x2 = jnp.where(pos >= 2, pltpu.roll(cin, 2, 0), p2_ref[...])
    conv = x2 * cw_ref[0:1, :] + x1 * cw_ref[1:2, :] + cin * cw_ref[2:3, :]
    acts_ref[:, 0:CONV_CH] = (bg * conv).astype(BF16)
    cin_ref[...] = cin

    rr = lax.broadcasted_iota(jnp.int32, (n, n), 0)
    cc = lax.broadcasted_iota(jnp.int32, (n, n), 1)
    same = (rr >> 3) == (cc >> 3)
    tril = same & (cc <= rr)
    la_hi, la_lo, b, bl = _gla_decay_terms(alr_ref, a2_ref, ab_ref, tril, same)
    q = qk_ref[:, :GLA_QK].astype(F32) * (GLA_DK ** -0.5)
    k = qk_ref[:, GLA_QK:].astype(F32)
    q_t = (q * jnp.exp(b)).astype(BF16)
    k_t = (k * jnp.exp(-b)).astype(BF16)
    k_end = k * jnp.exp(bl - b)
    la_hi = la_hi.astype(F32)
    la_lo = la_lo.astype(F32)
    row_seq = lax.broadcasted_iota(jnp.int32, (n, GLA_DK), 0) >> 3
    seq3 = lax.broadcasted_iota(jnp.int32, (SEQ_PER_BLK, GLA_DK, n), 0)
    lane_seq3 = lax.broadcasted_iota(jnp.int32, (SEQ_PER_BLK, GLA_DK, n), 2) >> 3
    mask3 = seq3 == lane_seq3
    ones = jnp.ones((n, GLA_DV), BF16)
    big = SEQ_PER_BLK * GLA_DK

    def per_seq(x_tr):
        x3 = jnp.where(mask3, x_tr[None, :, :], 0.0)
        return x3.reshape(big, n).astype(BF16)

    for hd in range(GLA_HEADS):
        ks = slice(hd * GLA_DK, (hd + 1) * GLA_DK)
        vs = slice(hd * GLA_DV, (hd + 1) * GLA_DV)
        qh = q_t[:, ks]
        vh = v_ref[:, vs]
        att = jnp.where(tril, _dot_nt(qh, k_t[:, ks]), 0.0).astype(BF16)
        o_intra = _dot(att, vh)
        s_old = s0_ref[:, hd].reshape(big, GLA_DV)
        zero = jnp.zeros_like(qh)
        q_big = jnp.concatenate(
            [jnp.where(row_seq == j, qh, zero) for j in range(SEQ_PER_BLK)], axis=1)
        o = o_intra + _dot(q_big, s_old.astype(BF16))
        dlog = _dot(per_seq(la_hi[:, ks].T), ones) + _dot(per_seq(la_lo[:, ks].T), ones)
        upd = _dot(per_seq(k_end[:, ks].T), vh)
        s_new = jnp.exp(dlog) * s_old + upd
        ns_ref[:, hd] = s_new.reshape(SEQ_PER_BLK, GLA_DK, GLA_DV)
        r = r_ref[:, vs]
        acts_ref[:, CONV_CH + hd * GLA_DV:CONV_CH + (hd + 1) * GLA_DV] = (
            _gla_out_gate(o, gng_ref, r).astype(BF16))

    us, vg = _chunk_mlp(bgu_ref, vv_ref, cmg_ref, ws_ref, cmb_ref, 1)
    acts_ref[:, CONV_CH + GLA_V:] = us.astype(BF16)
    vrow_ref[...] = vg


N_MIX_S_IN = 17


def _mix_sample_kernel(*refs, layer):
    if layer == 0:
        @pl.when(pl.program_id(0) == 0)
        def _():
            _mix_sample_body(*refs)

        @pl.when(pl.program_id(0) > 0)
        def _():
            ns_ref = refs[N_MIX_S_IN + 2]
            ns_ref[...] = jnp.zeros_like(ns_ref)
    else:
        _mix_sample_body(*refs[:N_MIX_S_IN], *refs[N_MIX_S_IN + 1:])


def _mix_sample(z, alr, a2, ab, cw, gng, cmg, ws, cmb, p1, p2, s0, ns_all, layer):
    row0 = T_PROMPT // ROWS_S
    n_i = DEC_BATCH // SEQ_PER_BLK
    n_pass = DEPTH if layer == 0 else 1
    blk = lambda p, i: jnp.where(p == 0, i, n_i - 1)
    row_map = lambda p, i: row0 + blk(p, i)
    slot = lambda p, i: (layer + p, i, 0, 0, 0)
    small = (a2, ab, cw, gng, cmg, ws, cmb)
    state_blk = (None, SEQ_PER_BLK, GLA_HEADS, GLA_DK, GLA_DV)
    in_specs = _z_specs(ROWS_S, row_map) + [
        pl.BlockSpec((ROWS_S, LANES), lambda p, i: (row_map(p, i), 0)),
    ] + [_layer_spec(a, layer) for a in small] + [
        pl.BlockSpec((ROWS_S, CONV_CH), lambda p, i: (blk(p, i), 0)),
        pl.BlockSpec((ROWS_S, CONV_CH), lambda p, i: (blk(p, i), 0)),
        pl.BlockSpec(state_blk, lambda p, i: (layer, blk(p, i), 0, 0, 0)),
    ]
    args = (z, z, z, z, z, z, alr, *small, p1, p2, s0)
    assert len(args) == N_MIX_S_IN
    aliases = {}
    if layer > 0:
        in_specs.append(pl.BlockSpec(memory_space=pl.ANY))
        args += (ns_all,)
        aliases = {N_MIX_S_IN: 2}
    return pl.pallas_call(
        functools.partial(_mix_sample_kernel, layer=layer),
        grid=(n_pass, n_i),
        in_specs=in_specs,
        out_specs=[
            pl.BlockSpec((ROWS_S, ACT_COLS), lambda p, i: (blk(p, i), 0)),
            pl.BlockSpec((ROWS_S, CONV_CH), lambda p, i: (blk(p, i), 0)),
            pl.BlockSpec(state_blk, slot),
            pl.BlockSpec((ROWS_S, CM_CH), lambda p, i: (blk(p, i), 0)),
        ],
        out_shape=[
            jax.ShapeDtypeStruct((T_SAMPLE, ACT_COLS), BF16),
            jax.ShapeDtypeStruct((T_SAMPLE, CONV_CH), F32),
            jax.ShapeDtypeStruct((DEPTH, DEC_BATCH, GLA_HEADS, GLA_DK, GLA_DV), F32),
            jax.ShapeDtypeStruct((T_SAMPLE, CM_CH), F32),
        ],
        input_output_aliases=aliases,
        compiler_params=pltpu.CompilerParams(
            dimension_semantics=("arbitrary", "arbitrary"), vmem_limit_bytes=VMEM_LIMIT),
        name="mix_sample",
    )(*args)


def _outproj_kernel(actp_ref, acts_ref, gates_ref, xa_ref, xb_ref, pa_ref, pb_ref, pc_ref, wo_ref,
                    n2_ref, wrh_ref, wrl_ref, rb_ref, xo_ref, xg_ref, lrank_ref, seg_ref, cnt_ref,
                    *, n_xa):
    i = pl.program_id(0)
    tm = TM_OUT
    part = tm // OUT_PARTS
    grps = []
    for rows in (pl.ds(p * part, part) for p in range(OUT_PARTS)):
        acts = jnp.where(i < T_PROMPT // TM_OUT, actp_ref[rows, :], acts_ref[rows, :])
        x_in = jnp.where(i < n_xa, xa_ref[rows, :], xb_ref[rows, :])
        ya = _dot(acts[:, :CONV_CH], pa_ref[...])
        yb = _dot(acts[:, CONV_CH:CONV_CH + GLA_V], pb_ref[...])
        yc = _dot(acts[:, CONV_CH + GLA_V:], pc_ref[...])
        ga = _sigmoid(gates_ref[rows, COL_GA:COL_GA + D_MODEL])
        gb = _sigmoid(gates_ref[rows, COL_GB:COL_GB + D_MODEL])
        gc = _sigmoid(gates_ref[rows, COL_GC:COL_GC + D_MODEL])
        mix = ga * ya + gb * yb + gc * yc
        x = x_in + _dot(mix.astype(BF16), wo_ref[...])
        xo_ref[rows, :] = x
        xn = _rms(x, n2_ref[...])
        xg_ref[rows, :D_MODEL] = xn

        hi, lo = _split_bf16(xn)
        logits = (_dot(hi, wrh_ref[...]) + _dot(lo, wrh_ref[...]) + _dot(hi, wrl_ref[...])
                  + rb_ref[...])
        lane_i = lax.broadcasted_iota(jnp.int32, logits.shape, 1)
        lane = lane_i.astype(F32)
        neg = jnp.float32(-jnp.inf)
        is_g = lane_i < MOE_GROUPS
        lg = jnp.where(is_g, logits, neg)
        gmax = jnp.max(lg, axis=-1, keepdims=True)
        grp = jnp.min(jnp.where(lg == gmax, lane, 1e9), axis=-1, keepdims=True)
        p_grp = 1.0 / jnp.sum(jnp.where(is_g, jnp.exp(lg - gmax), 0.0), axis=-1, keepdims=True)
        first = MOE_GROUPS + EXPERTS_PER_GROUP * grp
        in_grp = (lane >= first) & (lane < first + EXPERTS_PER_GROUP)
        le = jnp.where(in_grp, logits, neg)
        v1 = jnp.max(le, axis=-1, keepdims=True)
        i1 = jnp.min(jnp.where(le == v1, lane, 1e9), axis=-1, keepdims=True)
        le2 = jnp.where(lane == i1, neg, le)
        v2 = jnp.max(le2, axis=-1, keepdims=True)
        i2 = jnp.min(jnp.where(le2 == v2, lane, 1e9), axis=-1, keepdims=True)
        t = jnp.exp(v2 - v1)
        g1 = p_grp / (1.0 + t)
        g2 = p_grp * t / (1.0 + t)
        xg_ref[rows, D_MODEL:] = jnp.where(lane == i1 - first, g1,
                                           jnp.where(lane == i2 - first, g2, 0.0))
        grps.append(grp)

    @pl.when(i == 0)
    def _():
        cnt_ref[...] = jnp.zeros_like(cnt_ref)

    grp = jnp.concatenate(grps, axis=0)
    lane_i = lax.broadcasted_iota(jnp.int32, (tm, LANES), 1)
    lane = lane_i.astype(F32)
    onehot = jnp.where(lane == grp, 1.0, 0.0)
    csum = jnp.sum(onehot, axis=0, keepdims=True)
    rr = lax.broadcasted_iota(jnp.int32, (tm, tm), 0)
    cc = lax.broadcasted_iota(jnp.int32, (tm, tm), 1)
    before = jnp.where(cc < rr, 1.0, 0.0).astype(BF16)
    prefix = _dot(before, onehot.astype(BF16))
    local = _lane_prefix(csum)
    lrank = jnp.sum(onehot * (prefix + local), axis=-1, keepdims=True)
    lrank_ref[...] = jnp.where(lane_i == 0, lrank, 0.0).astype(jnp.int32)
    carry = cnt_ref[0:1, :]
    row = lax.broadcasted_iota(jnp.int32, (8, LANES), 0)
    seg_ref[...] = jnp.where(
        row == SEG_START, carry,
        jnp.where(row == SEG_LEN, csum, jnp.where(row == SEG_LOCAL, local, 0.0))
    ).astype(jnp.int32)
    cnt_ref[0:1, :] = carry + csum


def _lane_prefix(v):
    rr = lax.broadcasted_iota(jnp.int32, (LANES, LANES), 0)
    cc = lax.broadcasted_iota(jnp.int32, (LANES, LANES), 1)
    earlier = jnp.where(rr < cc, 1.0, 0.0).astype(BF16)
    return _dot(jnp.broadcast_to(v, (8, LANES)).astype(BF16), earlier)[0:1]


def _outproj(acts_p, acts_s, z, xa, xb, pa, pb, pc, wo, n2, wrh, wrl, rb, layer):
    m = T_ALL
    row = lambda i: (i, 0)
    _, act_specs = _two_part_specs(TM_OUT, acts_p, acts_s)
    n_xa, x_specs = _two_part_specs(TM_OUT, xa, xb)
    weights = (pa, pb, pc, wo, n2, wrh, wrl, rb)
    return pl.pallas_call(
        functools.partial(_outproj_kernel, n_xa=n_xa),
        grid=(m // TM_OUT,),
        in_specs=act_specs + [pl.BlockSpec((TM_OUT, 3 * D_MODEL), row)] + x_specs
        + [_layer_spec(w, layer) for w in weights],
        out_specs=[
            pl.BlockSpec((TM_OUT, D_MODEL), row),
            pl.BlockSpec((TM_OUT, XG_COLS), row),
            pl.BlockSpec((TM_OUT, LANES), row),
            pl.BlockSpec((None, 8, LANES), lambda i: (i, 0, 0)),
            pl.BlockSpec((8, LANES), lambda i: (0, 0)),
        ],
        out_shape=[
            jax.ShapeDtypeStruct((m, D_MODEL), F32),
            jax.ShapeDtypeStruct((m, XG_COLS), F32),
            jax.ShapeDtypeStruct((m, LANES), jnp.int32),
            jax.ShapeDtypeStruct((m // TM_OUT, 8, LANES), jnp.int32),
            jax.ShapeDtypeStruct((8, LANES), F32),
        ],
        compiler_params=pltpu.CompilerParams(
            dimension_semantics=("arbitrary",), vmem_limit_bytes=VMEM_LIMIT),
        name="outproj",
    )(acts_p, acts_s, z, xa, xb, *weights)


PLAN_GROUP, PLAN_USED, PLAN_END, PLAN_BLOCKS, PLAN_BASE = 0, 1, 2, 3, 4


def _plan_kernel(cnt_ref, plan_ref):
    blocks = jnp.floor((cnt_ref[0:1, :] + (MOE_BLK - 1)) * (1.0 / MOE_BLK))
    start = _lane_prefix(blocks)
    end = start + blocks
    lane_i = lax.broadcasted_iota(jnp.int32, (1, LANES), 1)
    lane = lane_i.astype(F32)
    grp_of_blk = jnp.zeros((1, LANES), F32)
    for g in range(MOE_GROUPS):
        end_g = jnp.sum(jnp.where(lane_i == g, end, 0.0), axis=-1, keepdims=True)
        grp_of_blk += jnp.where(lane >= end_g, 1.0, 0.0)
    grp_of_blk = jnp.minimum(grp_of_blk, MOE_GROUPS - 1)
    n_used = jnp.sum(blocks, axis=-1, keepdims=True)
    row = lax.broadcasted_iota(jnp.int32, (8, LANES), 0)
    plan_ref[...] = jnp.where(
        row == PLAN_GROUP, grp_of_blk,
        jnp.where(row == PLAN_USED, n_used,
                  jnp.where(row == PLAN_END, end,
                            jnp.where(row == PLAN_BLOCKS, blocks,
                                      jnp.where(row == PLAN_BASE, start * MOE_BLK, 0.0))))
    ).astype(jnp.int32)


def _plan(cnt):
    return pl.pallas_call(
        _plan_kernel,
        out_shape=jax.ShapeDtypeStruct((8, LANES), jnp.int32),
        name="plan",
    )(cnt)


SEG_START, SEG_LEN, SEG_LOCAL = 0, 1, 2


def _segment_copies(seg_ref, base_ref, make_copy):
    for g in range(MOE_GROUPS):
        n = seg_ref[SEG_LEN, g]
        k = TM_ROW
        while k >= 1:
            done = n & ~(2 * k - 1)
            @pl.when((n & k) != 0)
            def _():
                make_copy(seg_ref[SEG_LOCAL, g] + done,
                          base_ref[g] + seg_ref[SEG_START, g] + done, k).start()
            k //= 2


def _perm_matrix(lrank_ref):
    col = lax.broadcasted_iota(jnp.int32, (TM_ROW, TM_ROW), 1)
    return jnp.where(col == lrank_ref[:, 0:1], 1.0, 0.0)


SUB = 8
SUB_X = D_MODEL // 2 // LANES
SUB_GATE = SUB_X
HI_MASK = -65536


def _rows(first_row, n_rows):
    return pl.ds(pl.multiple_of(first_row * SUB, SUB), n_rows * SUB)


def _sublane(s, n_rows):
    return pl.ds(s, n_rows, stride=SUB)


def _scatter_kernel(nu_ref, end_ref, nb_ref, base_ref, seg_ref, lrank_ref, xg_ref, xs_ref,
                    zbuf, sbuf, sems):
    i = pl.program_id(0)
    n = pl.num_programs(0)
    slot = i % 2
    sem = sems.at[0]

    def zero_block(b):
        return pltpu.make_async_copy(zbuf, xs_ref.at[_rows(b * MOE_BLK, MOE_BLK)], sem)

    def each_unfilled_block(fn):
        for g in range(MOE_GROUPS):
            @pl.when(nb_ref[g] > 0)
            def _():
                fn(zero_block(end_ref[g] - 1))
        for b in range(T_ALL // MOE_BLK, N_BLK):
            @pl.when(b >= nu_ref[0])
            def _():
                fn(zero_block(b))

    @pl.when(i == 0)
    def _():
        zbuf[...] = jnp.zeros_like(zbuf)
        sbuf[...] = jnp.zeros_like(sbuf)
        each_unfilled_block(lambda c: c.start())
        each_unfilled_block(lambda c: c.wait())

    def tile_done(s):
        pltpu.make_async_copy(sbuf.at[s], xs_ref.at[_rows(0, TM_ROW)], sems.at[s]).wait()

    @pl.when(i >= 2)
    def _():
        tile_done(slot)

    perm = _perm_matrix(lrank_ref).T.astype(BF16)
    xs = lax.bitcast_convert_type(_dot(perm, xg_ref[:, :D_MODEL].astype(BF16)), jnp.int32)
    half = D_MODEL // 2
    for s in range(SUB_X):
        hi = xs[:, s * LANES:(s + 1) * LANES] & HI_MASK
        lo = lax.shift_right_logical(xs[:, half + s * LANES:half + (s + 1) * LANES], 16)
        sbuf[slot, _sublane(s, TM_ROW), :] = hi | lo
    gate = xg_ref[:, D_MODEL:]
    g1 = gate.astype(BF16)
    r1 = gate - g1.astype(F32)
    g2 = r1.astype(BF16)
    g3 = (r1 - g2.astype(F32)).astype(BF16)
    sbuf[slot, _sublane(SUB_GATE, TM_ROW), :] = lax.bitcast_convert_type(
        _dot(perm, g1) + _dot(perm, g2) + _dot(perm, g3), jnp.int32)

    def make_copy(src_row, dst_row, k):
        return pltpu.make_async_copy(sbuf.at[slot, _rows(src_row, k)],
                                     xs_ref.at[_rows(dst_row, k)], sems.at[slot])
    _segment_copies(seg_ref, base_ref, make_copy)

    @pl.when(i == n - 1)
    def _():
        tile_done(slot)

        @pl.when(n > 1)
        def _():
            tile_done(1 - slot)


def _scatter(seg, lrank, xg, n_used, grp_end, grp_blocks, grp_base):
    m = xg.shape[0]
    nt = m // TM_ROW
    grid_spec = pltpu.PrefetchScalarGridSpec(
        num_scalar_prefetch=4,
        grid=(nt,),
        in_specs=[
            pl.BlockSpec((None, 8, LANES), lambda i, *_: (i, 0, 0), memory_space=pltpu.SMEM),
            pl.BlockSpec((TM_ROW, LANES), lambda i, *_: (i, 0)),
            pl.BlockSpec((TM_ROW, XG_COLS), lambda i, *_: (i, 0)),
        ],
        out_specs=pl.BlockSpec(memory_space=pl.ANY),
        scratch_shapes=[pltpu.VMEM((MOE_BLK * SUB, LANES), jnp.int32),
                        pltpu.VMEM((2, TM_ROW * SUB, LANES), jnp.int32),
                        pltpu.SemaphoreType.DMA((2,))],
    )
    return pl.pallas_call(
        _scatter_kernel,
        grid_spec=grid_spec,
        out_shape=jax.ShapeDtypeStruct((N_SORTED * SUB, LANES), jnp.int32),
        compiler_params=pltpu.CompilerParams(
            dimension_semantics=("arbitrary",), vmem_limit_bytes=VMEM_LIMIT),
        name="scatter",
    )(n_used, grp_end, grp_blocks, grp_base, seg, lrank, xg)


def _ffn_kernel(bg_ref, nu_ref, xs_ref, w1_ref, w3_ref, w2_ref, y_ref):
    del bg_ref
    b = pl.program_id(0)

    @pl.when(b < nu_ref[0])
    def _():
        packed = [xs_ref[_sublane(s, MOE_BLK), :] for s in range(SUB_X)]
        x = jnp.concatenate(
            [lax.bitcast_convert_type(u & HI_MASK, F32).astype(BF16) for u in packed]
            + [lax.bitcast_convert_type(lax.shift_left(u, 16), F32).astype(BF16) for u in packed],
            axis=1)
        gates = lax.bitcast_convert_type(xs_ref[_sublane(SUB_GATE, MOE_BLK), :], F32)
        hs = []
        for e in range(EXPERTS_PER_GROUP):
            h1 = _dot(x, w1_ref[e])
            h3 = _dot(x, w3_ref[e])
            ge = gates[:, e:e + 1]
            hs.append(jnp.where(ge > 0.0, h1 * _sigmoid(h1) * h3 * ge, 0.0).astype(BF16))
        hcat = jnp.concatenate(hs, axis=1)
        y = _dot(hcat, w2_ref[...].reshape(EXPERTS_PER_GROUP * D_EXPERT, D_MODEL))
        for s in range(SUB):
            y_ref[_sublane(s, MOE_BLK), :] = y[:, s * LANES:(s + 1) * LANES]

    @pl.when(b >= nu_ref[0])
    def _():
        y_ref[...] = jnp.zeros_like(y_ref)


def _ffn(blk_group, n_used, xs, w1, w3, w2, layer):
    wmap = lambda b, bg, nu: (layer * MOE_GROUPS + bg[b], 0, 0, 0)
    grid_spec = pltpu.PrefetchScalarGridSpec(
        num_scalar_prefetch=2,
        grid=(N_BLK,),
        in_specs=[
            pl.BlockSpec((MOE_BLK * SUB, LANES), lambda b, bg, nu: (b, 0)),
            pl.BlockSpec((None, EXPERTS_PER_GROUP, D_MODEL, D_EXPERT), wmap),
            pl.BlockSpec((None, EXPERTS_PER_GROUP, D_MODEL, D_EXPERT), wmap),
            pl.BlockSpec((None, EXPERTS_PER_GROUP, D_EXPERT, D_MODEL), wmap),
        ],
        out_specs=pl.BlockSpec((MOE_BLK * SUB, LANES), lambda b, bg, nu: (b, 0)),
    )
    return pl.pallas_call(
        _ffn_kernel,
        grid_spec=grid_spec,
        out_shape=jax.ShapeDtypeStruct((N_SORTED * SUB, LANES), F32),
        compiler_params=pltpu.CompilerParams(
            dimension_semantics=("arbitrary",), vmem_limit_bytes=VMEM_LIMIT),
        name="ffn",
    )(blk_group, n_used, xs, w1, w3, w2)


def _combine_kernel(base_ref, seg_ref, segn_ref, lrank_ref, ys_hbm, x_ref, g_ref, *refs, final):
    outs, (buf, sem) = refs[:-2], refs[-2:]
    i = pl.program_id(0)
    n = pl.num_programs(0)
    slot = i % 2

    def gather(seg, s):
        def make_copy(buf_row, ys_row, k):
            return pltpu.make_async_copy(ys_hbm.at[_rows(ys_row, k)],
                                         buf.at[s, _rows(buf_row, k)], sem.at[s])
        _segment_copies(seg, base_ref, make_copy)

    @pl.when(i == 0)
    def _():
        gather(seg_ref, 0)

    @pl.when(i + 1 < n)
    def _():
        gather(segn_ref, 1 - slot)

    pltpu.make_async_copy(ys_hbm.at[_rows(0, TM_ROW)], buf.at[slot], sem.at[slot]).wait()

    pt = _perm_matrix(lrank_ref).astype(BF16)
    cols = []
    for s in range(SUB):
        hi, lo = _split_bf16(buf[slot, _sublane(s, TM_ROW), :])
        cols.append(_dot(pt, hi) + _dot(pt, lo))
    y = x_ref[...] + jnp.concatenate(cols, axis=1)
    if not final:
        outs[0][...] = y
    else:
        y = _rms(y, g_ref[...])

        @pl.when(i < T_PROMPT // TM_ROW)
        def _():
            outs[0][...] = y

        @pl.when(i >= T_PROMPT // TM_ROW)
        def _():
            outs[1][...] = y


def _combine(seg, lrank, grp_base, ys, x, g, final):
    m = x.shape[0]
    nt = m // TM_ROW
    n_p = T_PROMPT // TM_ROW
    smem = functools.partial(pl.BlockSpec, (None, 8, LANES), memory_space=pltpu.SMEM)
    tile = (TM_ROW, D_MODEL)
    if final:
        out_specs = [pl.BlockSpec(tile, lambda i, *_: (jnp.minimum(i, n_p - 1), 0)),
                     pl.BlockSpec(tile, lambda i, *_: (jnp.maximum(i - n_p, 0), 0))]
        out_shape = [jax.ShapeDtypeStruct((T_PROMPT, D_MODEL), F32),
                     jax.ShapeDtypeStruct((T_SAMPLE, D_MODEL), F32)]
    else:
        out_specs = [pl.BlockSpec(tile, lambda i, *_: (i, 0))]
        out_shape = [jax.ShapeDtypeStruct((m, D_MODEL), F32)]
    grid_spec = pltpu.PrefetchScalarGridSpec(
        num_scalar_prefetch=1,
        grid=(nt,),
        in_specs=[
            smem(lambda i, *_: (i, 0, 0)),
            smem(lambda i, *_: (jnp.minimum(i + 1, nt - 1), 0, 0)),
            pl.BlockSpec((TM_ROW, LANES), lambda i, *_: (i, 0)),
            pl.BlockSpec(memory_space=pl.ANY),
            pl.BlockSpec(tile, lambda i, *_: (i, 0)),
            pl.BlockSpec(g.shape, lambda i, *_: (0, 0)),
        ],
        out_specs=out_specs,
        scratch_shapes=[pltpu.VMEM((2, TM_ROW * SUB, LANES), F32),
                        pltpu.SemaphoreType.DMA((2,))],
    )
    return pl.pallas_call(
        functools.partial(_combine_kernel, final=final),
        grid_spec=grid_spec,
        out_shape=out_shape,
        compiler_params=pltpu.CompilerParams(
            dimension_semantics=("arbitrary",), vmem_limit_bytes=VMEM_LIMIT),
        name="combine",
    )(grp_base, seg, seg, lrank, ys, x, g)


def _prep_weights(w_in, gla_a2, cm_ws, cm_b, router_group_w, router_group_b,
                  router_expert_w, router_expert_b):
    off = {}
    o = 0
    for name, n in (("h", 512), ("cg", 512), ("bg", 512), ("q", 512), ("k", 512), ("v", 1024),
                    ("r", 1024), ("alr", 16), ("u", 512), ("vv", 512), ("ga", 1024),
                    ("gb", 1024), ("gc", 1024)):
        off[name] = (o, n)
        o += n
    order = ("ga", "gb", "gc", "v", "r", "q", "k", "h", "cg", "bg", "u", "vv")
    w_t = jnp.swapaxes(w_in, 1, 2)
    w_z = jnp.concatenate([w_t[:, off[n][0]:off[n][0] + off[n][1]] for n in order],
                          axis=1).astype(BF16)
    a0 = off["alr"][0]
    w_alr = jnp.pad(w_t[:, a0:a0 + GLA_LOWRANK],
                    ((0, 0), (0, LANES - GLA_LOWRANK), (0, 0))).astype(BF16)
    a2 = jnp.pad(gla_a2, ((0, 0), (0, LANES - GLA_LOWRANK), (0, 0))).astype(BF16)
    ws_p = jnp.tril(cm_ws).astype(BF16)
    small = jnp.tril(cm_ws[:, :, :DEC_SEQ, :DEC_SEQ])
    eye = jnp.eye(SEQ_PER_BLK, dtype=F32)
    ws_s = jnp.einsum("ij,lgab->lgiajb", eye, small).reshape(
        DEPTH, CM_GROUPS, ROWS_S, ROWS_S).astype(BF16)
    cmb_p = jnp.broadcast_to(jnp.transpose(cm_b, (0, 2, 1))[:, :, :, None],
                             (DEPTH, CM_CHUNK, CM_GROUPS, CM_GCH)).reshape(DEPTH, CM_CHUNK, CM_CH)
    cmb_s = jnp.tile(cmb_p[:, :DEC_SEQ], (1, SEQ_PER_BLK, 1))
    pad = LANES - MOE_GROUPS - N_EXPERTS
    w_r = jnp.pad(jnp.concatenate([router_group_w, router_expert_w], axis=-1),
                  ((0, 0), (0, 0), (0, pad)))
    w_r_hi = w_r.astype(BF16)
    w_r_lo = (w_r - w_r_hi.astype(F32)).astype(BF16)
    r_b = jnp.pad(jnp.concatenate([router_group_b, router_expert_b], axis=-1),
                  ((0, 0), (0, pad)))[:, None, :]
    return w_z, w_alr, a2, ws_p, ws_s, cmb_p, cmb_s, w_r_hi, w_r_lo, r_b


def kernel(x_prompt, x_sample, state_conv, state_gla, norm1_g, w_in, conv_w, gla_a2, gla_a_b,
           gla_norm_g, cm_norm_g, cm_ws, cm_b, proj_a, proj_b, proj_c, w_out, norm2_g,
           router_group_w, router_group_b, router_expert_w, router_expert_b,
           exp_w1, exp_w3, exp_w2, final_norm_g):
    (w_z, w_alr, a2, ws_p, ws_s, cmb_p, cmb_s, w_r_hi, w_r_lo, r_b) = _prep_weights(
        w_in, gla_a2, cm_ws, cm_b, router_group_w, router_group_b, router_expert_w,
        router_expert_b)
    pa, pb, pc, wo = (w.astype(BF16) for w in (proj_a, proj_b, proj_c, w_out))
    grouped = (DEPTH * MOE_GROUPS, EXPERTS_PER_GROUP)
    w1 = exp_w1.astype(BF16).reshape(grouped + (D_MODEL, D_EXPERT))
    w3 = exp_w3.astype(BF16).reshape(grouped + (D_MODEL, D_EXPERT))
    w2 = exp_w2.astype(BF16).reshape(grouped + (D_EXPERT, D_MODEL))
    n1 = norm1_g[:, None, :]
    n2 = norm2_g[:, None, :]
    ab = gla_a_b[:, None, :]
    gng = gla_norm_g[:, None, :]
    cmg = cm_norm_g.reshape(DEPTH, 1, CM_CH)
    fg = final_norm_g[None, :]
    xa = x_prompt.reshape(T_PROMPT, D_MODEL)
    xb = x_sample.reshape(T_SAMPLE, D_MODEL)
    gla_s = None
    conv_p, gla_p, conv_s, cmv_s = [], [], [], []
    for l in range(DEPTH):
        z, alr = _inproj(xa, xb, n1, w_z, w_alr, l)
        acts_p, nconv, ngla = _mix_prompt(z, alr, a2, ab, conv_w, gng, cmg, ws_p, cmb_p, l)
        sc = state_conv[l]
        p2 = jnp.pad(sc, ((0, 0), (0, DEC_SEQ - 2), (0, 0))).reshape(T_SAMPLE, CONV_CH)
        p1 = jnp.pad(sc[:, 1:2], ((0, 0), (0, DEC_SEQ - 1), (0, 0))).reshape(T_SAMPLE, CONV_CH)
        acts_s, cin_s, gla_s, vrows = _mix_sample(z, alr, a2, ab, conv_w, gng, cmg, ws_s, cmb_s,
                                                  p1, p2, state_gla, gla_s, l)
        conv_p.append(nconv)
        gla_p.append(ngla)
        conv_s.append(cin_s.reshape(DEC_BATCH, DEC_SEQ, CONV_CH)[:, DEC_SEQ - (CONV_K - 1):])
        cmv_s.append(vrows.reshape(DEC_BATCH, DEC_SEQ, CM_CH))

        x, xg, lrank, seg, cnt = _outproj(acts_p, acts_s, z, xa, xb, pa, pb, pc, wo, n2,
                                          w_r_hi, w_r_lo, r_b, l)
        plan = _plan(cnt)
        n_used = plan[PLAN_USED, :1]
        grp_base = plan[PLAN_BASE, :MOE_GROUPS]
        xs = _scatter(seg, lrank, xg, n_used, plan[PLAN_END, :MOE_GROUPS],
                      plan[PLAN_BLOCKS, :MOE_GROUPS], grp_base)
        ys = _ffn(plan[PLAN_GROUP, :N_BLK], n_used, xs, w1, w3, w2, l)
        out = _combine(seg, lrank, grp_base, ys, x, fg, l == DEPTH - 1)
        xa = xb = out[0]

    y_prompt = out[0].reshape(BATCH, SEQ, D_MODEL)
    y_sample = out[1].reshape(DEC_BATCH, DEC_SEQ, D_MODEL)
    return (y_prompt, y_sample, jnp.stack(conv_p), jnp.stack(gla_p), jnp.stack(conv_s),
            gla_s, jnp.stack(cmv_s))
```

```python
import functools

import jax
import jax.numpy as jnp
from jax import lax
from jax.experimental import pallas as pl
from jax.experimental.pallas import tpu as pltpu

F32 = jnp.float32
BF16 = jnp.bfloat16

D_MODEL = 1024
BATCH = 8
SEQ = 2048
DEPTH = 2
DEC_BATCH = 128
DEC_SEQ = 8
CONV_K = 3
CONV_CH = 512
GLA_HEADS = 4
GLA_DK = 128
GLA_DV = 256
GLA_QK = GLA_HEADS * GLA_DK
GLA_V = GLA_HEADS * GLA_DV
GLA_LOWRANK = 16
GLA_TAU = 16.0
GLA_CHUNK = 64
CM_GROUPS = 4
CM_CHUNK = 128
CM_GCH = 128
CM_CH = 512
MOE_GROUPS = 8
EXPERTS_PER_GROUP = 8
N_EXPERTS = 64
D_EXPERT = 256
EPS = 1e-6

LANES = 128
T_PROMPT = BATCH * SEQ
T_SAMPLE = DEC_BATCH * DEC_SEQ
T_ALL = T_PROMPT + T_SAMPLE

COL_GA, COL_GB, COL_GC = 0, 1024, 2048
COL_V, COL_R, COL_Q, COL_K = 3072, 4096, 5120, 5632
COL_H, COL_CG, COL_BG, COL_U, COL_VV = 6144, 6656, 7168, 7680, 8192
Z_COLS = 8704
ACT_COLS = 2048
ROUTE_ROWS = 2 * EXPERTS_PER_GROUP
ROUTE_LRANK = EXPERTS_PER_GROUP

TM_IN = 1024
TN_IN = Z_COLS // 4
TC_MIX = 256
SEQ_PER_BLK = 16
ROWS_S = SEQ_PER_BLK * DEC_SEQ
TM_OUT = 256
TM_ROW = TM_OUT
MOE_BLK = 256
N_BLK = T_ALL // MOE_BLK + MOE_GROUPS
N_SORTED = N_BLK * MOE_BLK
VMEM_LIMIT = 56 * 1024 * 1024


def _sigmoid(x):
    return 0.5 * jnp.tanh(0.5 * x) + 0.5


def _gelu_tanh(x):
    c = 0.7978845608028654
    half = 0.5 * x
    return half + half * jnp.tanh(x * (c + (c * 0.044715) * (x * x)))


def _log_sigmoid(x):
    log2_e = 1.4426950408889634
    ln_2 = 0.6931471805599453
    return jnp.minimum(x, 0.0) - ln_2 * jnp.log2(1.0 + jnp.exp2(-log2_e * jnp.abs(x)))


def _rms(x, g):
    ms = jnp.mean(x * x, axis=-1, keepdims=True)
    return x * lax.rsqrt(ms + EPS) * g


def _split_bf16(x):
    hi = x.astype(BF16)
    lo = (x - hi.astype(F32)).astype(BF16)
    return hi, lo


def _dot(a, b):
    return jnp.dot(a, b, preferred_element_type=F32)


def _dot_nt(a, b):
    return lax.dot_general(a, b, (((1,), (1,)), ((), ())), preferred_element_type=F32)


def _layer_spec(arr, layer):
    nd = arr.ndim - 1
    return pl.BlockSpec((None,) + arr.shape[1:], lambda *g: (layer,) + (0,) * nd)


def _const_spec(arr):
    nd = arr.ndim
    return pl.BlockSpec(arr.shape, lambda *g: (0,) * nd)


def _two_part_specs(tile, xa, xb):
    n_a = xa.shape[0] // tile
    n_b = xb.shape[0] // tile
    return n_a, [
        pl.BlockSpec((tile, xa.shape[1]), lambda i, *_: (jnp.minimum(i, n_a - 1), 0)),
        pl.BlockSpec((tile, xb.shape[1]), lambda i, *_: (jnp.clip(i - n_a, 0, n_b - 1), 0)),
    ]


def _inproj_kernel(xa_ref, xb_ref, g_ref, w_ref, wa_ref, z_ref, a_ref, xn_ref, *, n_a):
    @pl.when(pl.program_id(1) == 0)
    def _():
        x = jnp.where(pl.program_id(0) < n_a, xa_ref[...], xb_ref[...])
        xn = _rms(x, g_ref[...]).astype(BF16)
        xn_ref[...] = xn
        a_ref[...] = _dot_nt(xn, wa_ref[...]).astype(BF16)

    z_ref[...] = _dot_nt(xn_ref[...], w_ref[...]).astype(BF16)


def _inproj(xa, xb, g, w, wa, layer):
    m = T_ALL
    n_a, x_specs = _two_part_specs(TM_IN, xa, xb)
    return pl.pallas_call(
        functools.partial(_inproj_kernel, n_a=n_a),
        grid=(m // TM_IN, Z_COLS // TN_IN),
        in_specs=x_specs + [
            _layer_spec(g, layer),
            pl.BlockSpec((None, TN_IN, D_MODEL), lambda i, j: (layer, j, 0)),
            _layer_spec(wa, layer),
        ],
        out_specs=[
            pl.BlockSpec((TM_IN, TN_IN), lambda i, j: (i, j)),
            pl.BlockSpec((TM_IN, LANES), lambda i, j: (i, 0)),
        ],
        out_shape=[
            jax.ShapeDtypeStruct((m, Z_COLS), BF16),
            jax.ShapeDtypeStruct((m, LANES), BF16),
        ],
        scratch_shapes=[pltpu.VMEM((TM_IN, D_MODEL), BF16)],
        compiler_params=pltpu.CompilerParams(
            dimension_semantics=("arbitrary", "arbitrary"), vmem_limit_bytes=VMEM_LIMIT),
        name="inproj",
    )(xa, xb, g, w, wa)


def _gla_log_decay(alr_ref, a2_ref, ab_ref):
    la = _log_sigmoid(_dot(alr_ref[...], a2_ref[...]) + ab_ref[...]) * (1.0 / GLA_TAU)
    return _split_bf16(la)


def _masked_sum(mask, la_hi, la_lo):
    m = jnp.where(mask, 1.0, 0.0).astype(BF16)
    return _dot(m, la_hi) + _dot(m, la_lo)


def _gla_decay_prefix(alr_ref, a2_ref, ab_ref, tril_mask):
    la_hi, la_lo = _gla_log_decay(alr_ref, a2_ref, ab_ref)
    return _masked_sum(tril_mask, la_hi, la_lo)


def _gla_decay_terms(alr_ref, a2_ref, ab_ref, tril_mask, same_mask):
    la_hi, la_lo = _gla_log_decay(alr_ref, a2_ref, ab_ref)
    return (la_hi, la_lo, _masked_sum(tril_mask, la_hi, la_lo),
            _masked_sum(same_mask, la_hi, la_lo))


def _gla_out_gate(o, g_ref, r):
    return _rms(o, g_ref[...]) * (r * _sigmoid(r))


def _chunk_mlp_group(g, bgu_ref, vv_ref, cmg_ref, ws_ref, cmb_ref, n_chunks):
    sl = slice(g * CM_GCH, (g + 1) * CM_GCH)
    ug = _gelu_tanh(bgu_ref[:, CONV_CH + g * CM_GCH:CONV_CH + (g + 1) * CM_GCH])
    vg = _rms(_gelu_tanh(vv_ref[:, sl]).astype(F32), cmg_ref[:, sl])
    vgb = vg.astype(BF16)
    rows = []
    for j in range(n_chunks):
        rs = slice(j * CM_CHUNK, (j + 1) * CM_CHUNK)
        rows.append(_dot(ws_ref[g], vgb[rs]) + cmb_ref[:, sl])
    s = rows[0] if n_chunks == 1 else jnp.concatenate(rows, axis=0)
    return ug * s, vg


def _chunk_mlp(bgu_ref, vv_ref, cmg_ref, ws_ref, cmb_ref, n_chunks):
    parts = [_chunk_mlp_group(g, bgu_ref, vv_ref, cmg_ref, ws_ref, cmb_ref, n_chunks)
             for g in range(CM_GROUPS)]
    return (jnp.concatenate([p[0] for p in parts], axis=1),
            jnp.concatenate([p[1] for p in parts], axis=1))


N_Z_VIEWS = 7
N_MIX_W = 7
SEQ_PER_STEP = 1


def _mix_prompt_kernel(*refs):
    n_z = N_Z_VIEWS * SEQ_PER_STEP
    weights = refs[n_z:n_z + N_MIX_W]
    acts_ref, nconv_ref, ngla_ref, st_ref, carry_ref = refs[n_z + N_MIX_W:]
    for s in range(SEQ_PER_STEP):
        _mix_prompt_seq(*refs[N_Z_VIEWS * s:N_Z_VIEWS * (s + 1)], *weights,
                        acts_ref.at[s], nconv_ref.at[s], ngla_ref.at[s], st_ref.at[s],
                        carry_ref.at[s])


def _mix_prompt_seq(v_ref, r_ref, qk_ref, hcg_ref, bgu_ref, vv_ref, alr_ref,
                    a2_ref, ab_ref, cw_ref, gng_ref, cmg_ref, ws_ref, cmb_ref,
                    acts_ref, nconv_ref, ngla_ref, st_ref, carry_ref):
    tc = TC_MIX

    @pl.when(pl.program_id(1) == 0)
    def _():
        st_ref[...] = jnp.zeros_like(st_ref)
        carry_ref[...] = jnp.zeros_like(carry_ref)

    row = lax.broadcasted_iota(jnp.int32, (tc, 1), 0)
    h = hcg_ref[:, :CONV_CH].astype(F32)
    cg = hcg_ref[:, CONV_CH:].astype(F32)
    bg = bgu_ref[:, :CONV_CH].astype(F32)
    cin = cg * h
    c0 = carry_ref[0:1, :]
    c1 = carry_ref[1:2, :]
    x1 = jnp.where(row >= 1, pltpu.roll(cin, 1, 0), c1)
    x2 = jnp.where(row >= 2, pltpu.roll(cin, 2, 0), jnp.where(row == 1, c1, c0))
    conv = x2 * cw_ref[0:1, :] + x1 * cw_ref[1:2, :] + cin * cw_ref[2:3, :]
    acts_ref[:, 0:CONV_CH] = (bg * conv).astype(BF16)
    carry_ref[0:2, :] = cin[tc - 2:tc, :]
    nconv_ref[0] = cin[tc - 2:tc, :]

    rr = lax.broadcasted_iota(jnp.int32, (tc, tc), 0)
    cc = lax.broadcasted_iota(jnp.int32, (tc, tc), 1)
    same = (rr >> 6) == (cc >> 6)
    tril = same & (cc <= rr)
    b = _gla_decay_prefix(alr_ref, a2_ref, ab_ref, tril)
    n_chunks = tc // GLA_CHUNK
    b_last = [b[(c + 1) * GLA_CHUNK - 1:(c + 1) * GLA_CHUNK, :] for c in range(n_chunks)]
    bl = jnp.concatenate([jnp.broadcast_to(r_, (GLA_CHUNK, GLA_QK)) for r_ in b_last], axis=0)
    q = qk_ref[:, :GLA_QK].astype(F32) * (GLA_DK ** -0.5)
    k = qk_ref[:, GLA_QK:].astype(F32)
    q_t = (q * jnp.exp(b)).astype(BF16)
    k_t = (k * jnp.exp(-b)).astype(BF16)
    k_end = (k * jnp.exp(bl - b)).astype(BF16)
    states = [st_ref[hd] for hd in range(GLA_HEADS)]
    k_cols = [slice(hd * GLA_DK, (hd + 1) * GLA_DK) for hd in range(GLA_HEADS)]
    v_cols = [slice(hd * GLA_DV, (hd + 1) * GLA_DV) for hd in range(GLA_HEADS)]
    o_intra = []
    for hd in range(GLA_HEADS):
        att = jnp.where(tril, _dot_nt(q_t[:, k_cols[hd]], k_t[:, k_cols[hd]]), 0.0).astype(BF16)
        o_intra.append(_dot(att, v_ref[:, v_cols[hd]]))

    o_rows = [[] for _ in range(GLA_HEADS)]
    assert n_chunks == CM_GROUPS
    for c in range(n_chunks):
        rs = slice(c * GLA_CHUNK, (c + 1) * GLA_CHUNK)
        for hd in range(GLA_HEADS):
            ks = k_cols[hd]
            st = states[hd]
            o_rows[hd].append(o_intra[hd][rs] + _dot_nt(q_t[rs, ks], st.astype(BF16)))
            upd = lax.dot_general(v_ref[rs, v_cols[hd]], k_end[rs, ks], (((0,), (0,)), ((), ())),
                                  preferred_element_type=F32)
            states[hd] = jnp.exp(b_last[c][:, ks]) * st + upd
        us, _ = _chunk_mlp_group(c, bgu_ref, vv_ref, cmg_ref, ws_ref, cmb_ref, tc // CM_CHUNK)
        acts_ref[:, CONV_CH + GLA_V + c * CM_GCH:CONV_CH + GLA_V + (c + 1) * CM_GCH] = (
            us.astype(BF16))

    for hd in range(GLA_HEADS):
        o = jnp.concatenate(o_rows[hd], axis=0)
        r = r_ref[:, v_cols[hd]]
        acts_ref[:, CONV_CH + hd * GLA_DV:CONV_CH + (hd + 1) * GLA_DV] = (
            _gla_out_gate(o, gng_ref, r).astype(BF16))
        st_ref[hd] = states[hd]

    @pl.when(pl.program_id(1) == pl.num_programs(1) - 1)
    def _():
        for hd in range(GLA_HEADS):
            ngla_ref[0, hd] = states[hd].T


def _z_specs(rows, row_map):
    def spec(width, col):
        blk = col // width
        return pl.BlockSpec((rows, width), lambda *g: (row_map(*g), blk))
    return [spec(1024, COL_V), spec(1024, COL_R), spec(1024, COL_Q), spec(1024, COL_H),
            spec(1024, COL_BG), spec(512, COL_VV)]


def _mix_prompt(z, alr, a2, ab, cw, gng, cmg, ws, cmb, layer):
    nt = SEQ // TC_MIX
    nb = BATCH // SEQ_PER_STEP
    small = (a2, ab, cw, gng, cmg, ws, cmb)
    assert len(small) == N_MIX_W
    in_specs, args = [], []
    for s in range(SEQ_PER_STEP):
        row_map = lambda b, c, s=s: (b + s * nb) * nt + c
        in_specs += _z_specs(TC_MIX, row_map) + [
            pl.BlockSpec((TC_MIX, LANES), lambda b, c, row_map=row_map: (row_map(b, c), 0))]
        args += [z] * (N_Z_VIEWS - 1) + [alr]
    in_specs += [_layer_spec(a, layer) for a in small]
    acts, nconv, ngla = pl.pallas_call(
        _mix_prompt_kernel,
        grid=(nb, nt),
        in_specs=in_specs,
        out_specs=[
            pl.BlockSpec((SEQ_PER_STEP, TC_MIX, ACT_COLS), lambda b, c: (0, b * nt + c, 0)),
            pl.BlockSpec((SEQ_PER_STEP, 1, CONV_K - 1, CONV_CH), lambda b, c: (0, b, 0, 0)),
            pl.BlockSpec((SEQ_PER_STEP, 1, GLA_HEADS, GLA_DK, GLA_DV),
                         lambda b, c: (0, b, 0, 0, 0)),
        ],
        out_shape=[
            jax.ShapeDtypeStruct((SEQ_PER_STEP, T_PROMPT // SEQ_PER_STEP, ACT_COLS), BF16),
            jax.ShapeDtypeStruct((SEQ_PER_STEP, nb, CONV_K - 1, CONV_CH), F32),
            jax.ShapeDtypeStruct((SEQ_PER_STEP, nb, GLA_HEADS, GLA_DK, GLA_DV), F32),
        ],
        scratch_shapes=[pltpu.VMEM((SEQ_PER_STEP, GLA_HEADS, GLA_DV, GLA_DK), F32),
                        pltpu.VMEM((SEQ_PER_STEP, 8, CONV_CH), F32)],
        compiler_params=pltpu.CompilerParams(
            dimension_semantics=("arbitrary", "arbitrary"), vmem_limit_bytes=VMEM_LIMIT),
        name="mix_prompt",
    )(*args, *small)
    return (acts.reshape(T_PROMPT, ACT_COLS), nconv.reshape(BATCH, CONV_K - 1, CONV_CH),
            ngla.reshape(BATCH, GLA_HEADS, GLA_DK, GLA_DV))


def _mix_sample_body(v_ref, r_ref, qk_ref, hcg_ref, bgu_ref, vv_ref, alr_ref,
                       a2_ref, ab_ref, cw_ref, gng_ref, cmg_ref, ws_ref, cmb_ref,
                       p1_ref, p2_ref, s0_ref,
                       acts_ref, cin_ref, ns_ref, vrow_ref):
    n = ROWS_S

    pos = lax.broadcasted_iota(jnp.int32, (n, 1), 0) & (DEC_SEQ - 1)
    h = hcg_ref[:, :CONV_CH].astype(F32)
    cg = hcg_ref[:, CONV_CH:].astype(F32)
    bg = bgu_ref[:, :CONV_CH].astype(F32)
    cin = cg * h
    x1 = jnp.where(pos >= 1, pltpu.roll(cin, 1, 0), p1_ref[...])
    x2 = jnp.where(pos >= 2, pltpu.roll(cin, 2, 0), p2_ref[...])
    conv = x2 * cw_ref[0:1, :] + x1 * cw_ref[1:2, :] + cin * cw_ref[2:3, :]
    acts_ref[:, 0:CONV_CH] = (bg * conv).astype(BF16)
    cin_ref[...] = cin

    rr = lax.broadcasted_iota(jnp.int32, (n, n), 0)
    cc = lax.broadcasted_iota(jnp.int32, (n, n), 1)
    same = (rr >> 3) == (cc >> 3)
    tril = same & (cc <= rr)
    la_hi, la_lo, b, bl = _gla_decay_terms(alr_ref, a2_ref, ab_ref, tril, same)
    q = qk_ref[:, :GLA_QK].astype(F32) * (GLA_DK ** -0.5)
    k = qk_ref[:, GLA_QK:].astype(F32)
    q_t = (q * jnp.exp(b)).astype(BF16)
    k_t = (k * jnp.exp(-b)).astype(BF16)
    k_end = k * jnp.exp(bl - b)
    la_hi = la_hi.astype(F32)
    la_lo = la_lo.astype(F32)
    row_seq = lax.broadcasted_iota(jnp.int32, (n, GLA_DK), 0) >> 3
    seq3 = lax.broadcasted_iota(jnp.int32, (SEQ_PER_BLK, GLA_DK, n), 0)
    lane_seq3 = lax.broadcasted_iota(jnp.int32, (SEQ_PER_BLK, GLA_DK, n), 2) >> 3
    mask3 = seq3 == lane_seq3
    ones = jnp.ones((n, GLA_DV), BF16)
    big = SEQ_PER_BLK * GLA_DK

    def per_seq(x_tr):
        x3 = jnp.where(mask3, x_tr[None, :, :], 0.0)
        return x3.reshape(big, n).astype(BF16)

    for hd in range(GLA_HEADS):
        ks = slice(hd * GLA_DK, (hd + 1) * GLA_DK)
        vs = slice(hd * GLA_DV, (hd + 1) * GLA_DV)
        qh = q_t[:, ks]
        vh = v_ref[:, vs]
        att = jnp.where(tril, _dot_nt(qh, k_t[:, ks]), 0.0).astype(BF16)
        o_intra = _dot(att, vh)
        s_old = s0_ref[:, hd].reshape(big, GLA_DV)
        zero = jnp.zeros_like(qh)
        q_big = jnp.concatenate(
            [jnp.where(row_seq == j, qh, zero) for j in range(SEQ_PER_BLK)], axis=1)
        o = o_intra + _dot(q_big, s_old.astype(BF16))
        dlog = _dot(per_seq(la_hi[:, ks].T), ones) + _dot(per_seq(la_lo[:, ks].T), ones)
        upd = _dot(per_seq(k_end[:, ks].T), vh)
        s_new = jnp.exp(dlog) * s_old + upd
        ns_ref[:, hd] = s_new.reshape(SEQ_PER_BLK, GLA_DK, GLA_DV)
        r = r_ref[:, vs]
        acts_ref[:, CONV_CH + hd * GLA_DV:CONV_CH + (hd + 1) * GLA_DV] = (
            _gla_out_gate(o, gng_ref, r).astype(BF16))

    us, vg = _chunk_mlp(bgu_ref, vv_ref, cmg_ref, ws_ref, cmb_ref, 1)
    acts_ref[:, CONV_CH + GLA_V:] = us.astype(BF16)
    vrow_ref[...] = vg


N_MIX_S_IN = 17


def _mix_sample_kernel(*refs, layer):
    if layer == 0:
        @pl.when(pl.program_id(0) == 0)
        def _():
            _mix_sample_body(*refs)

        @pl.when(pl.program_id(0) > 0)
        def _():
            ns_ref = refs[N_MIX_S_IN + 2]
            ns_ref[...] = jnp.zeros_like(ns_ref)
    else:
        _mix_sample_body(*refs[:N_MIX_S_IN], *refs[N_MIX_S_IN + 1:])


def _mix_sample(z, alr, a2, ab, cw, gng, cmg, ws, cmb, p1, p2, s0, ns_all, layer):
    row0 = T_PROMPT // ROWS_S
    n_i = DEC_BATCH // SEQ_PER_BLK
    n_pass = DEPTH if layer == 0 else 1
    blk = lambda p, i: jnp.where(p == 0, i, n_i - 1)
    row_map = lambda p, i: row0 + blk(p, i)
    slot = lambda p, i: (layer + p, i, 0, 0, 0)
    small = (a2, ab, cw, gng, cmg, ws, cmb)
    state_blk = (None, SEQ_PER_BLK, GLA_HEADS, GLA_DK, GLA_DV)
    in_specs = _z_specs(ROWS_S, row_map) + [
        pl.BlockSpec((ROWS_S, LANES), lambda p, i: (row_map(p, i), 0)),
    ] + [_layer_spec(a, layer) for a in small] + [
        pl.BlockSpec((ROWS_S, CONV_CH), lambda p, i: (blk(p, i), 0)),
        pl.BlockSpec((ROWS_S, CONV_CH), lambda p, i: (blk(p, i), 0)),
        pl.BlockSpec(state_blk, lambda p, i: (layer, blk(p, i), 0, 0, 0)),
    ]
    args = (z, z, z, z, z, z, alr, *small, p1, p2, s0)
    assert len(args) == N_MIX_S_IN
    aliases = {}
    if layer > 0:
        in_specs.append(pl.BlockSpec(memory_space=pl.ANY))
        args += (ns_all,)
        aliases = {N_MIX_S_IN: 2}
    return pl.pallas_call(
        functools.partial(_mix_sample_kernel, layer=layer),
        grid=(n_pass, n_i),
        in_specs=in_specs,
        out_specs=[
            pl.BlockSpec((ROWS_S, ACT_COLS), lambda p, i: (blk(p, i), 0)),
            pl.BlockSpec((ROWS_S, CONV_CH), lambda p, i: (blk(p, i), 0)),
            pl.BlockSpec(state_blk, slot),
            pl.BlockSpec((ROWS_S, CM_CH), lambda p, i: (blk(p, i), 0)),
        ],
        out_shape=[
            jax.ShapeDtypeStruct((T_SAMPLE, ACT_COLS), BF16),
            jax.ShapeDtypeStruct((T_SAMPLE, CONV_CH), F32),
            jax.ShapeDtypeStruct((DEPTH, DEC_BATCH, GLA_HEADS, GLA_DK, GLA_DV), F32),
            jax.ShapeDtypeStruct((T_SAMPLE, CM_CH), F32),
        ],
        input_output_aliases=aliases,
        compiler_params=pltpu.CompilerParams(
            dimension_semantics=("arbitrary", "arbitrary"), vmem_limit_bytes=VMEM_LIMIT),
        name="mix_sample",
    )(*args)


def _interleave(*stage_lists):
    pending = list(stage_lists)
    while pending:
        for gen in list(pending):
            if next(gen, StopIteration) is StopIteration:
                pending.remove(gen)


def _outproj_main(i, n_xa, actp_ref, acts_ref, gates_ref, xa_ref, xb_ref, pa_ref, pb_ref, pc_ref,
                  wo_ref, n2_ref, xo_ref, xn_ref, split_ref):
    acts = jnp.where(i < T_PROMPT // TM_OUT, actp_ref[...], acts_ref[...])
    x_in = jnp.where(i < n_xa, xa_ref[...], xb_ref[...])
    ya = _dot(acts[:, :CONV_CH], pa_ref[...])
    yb = _dot(acts[:, CONV_CH:CONV_CH + GLA_V], pb_ref[...])
    yc = _dot(acts[:, CONV_CH + GLA_V:], pc_ref[...])
    yield
    ga = _sigmoid(gates_ref[:, COL_GA:COL_GA + D_MODEL])
    gb = _sigmoid(gates_ref[:, COL_GB:COL_GB + D_MODEL])
    gc = _sigmoid(gates_ref[:, COL_GC:COL_GC + D_MODEL])
    mix = ga * ya + gb * yb + gc * yc
    yield
    x = x_in + _dot(mix.astype(BF16), wo_ref[...])
    xo_ref[...] = x
    yield
    xn = _rms(x, n2_ref[...])
    xn_ref[...] = xn
    hi, lo = _split_bf16(xn)
    yield
    split_ref[0] = hi
    split_ref[1] = lo


def _outproj_route(live, split_ref, wrh_ref, wrl_ref, rb_ref, route_ref, seg_ref, cnt_ref):
    tm = TM_OUT
    hi = split_ref[0]
    lo = split_ref[1]
    logits = (_dot_nt(wrh_ref[...], hi) + _dot_nt(wrh_ref[...], lo) + _dot_nt(wrl_ref[...], hi)
              + rb_ref[...])
    yield
    n_sub = EXPERTS_PER_GROUP
    sub = lax.broadcasted_iota(jnp.int32, (n_sub, tm), 0).astype(F32)
    neg = jnp.float32(-jnp.inf)
    lg = logits[0:MOE_GROUPS]
    gmax = jnp.max(lg, axis=0, keepdims=True)
    grp = jnp.min(jnp.where(lg == gmax, sub, 1e9), axis=0, keepdims=True)
    p_grp = 1.0 / jnp.sum(jnp.exp(lg - gmax), axis=0, keepdims=True)
    yield
    le = logits[MOE_GROUPS:MOE_GROUPS + n_sub]
    for g in range(1, MOE_GROUPS):
        le = jnp.where(grp == g, logits[MOE_GROUPS + g * n_sub:MOE_GROUPS + (g + 1) * n_sub], le)
    v1 = jnp.max(le, axis=0, keepdims=True)
    i1 = jnp.min(jnp.where(le == v1, sub, 1e9), axis=0, keepdims=True)
    le2 = jnp.where(sub == i1, neg, le)
    v2 = jnp.max(le2, axis=0, keepdims=True)
    i2 = jnp.min(jnp.where(le2 == v2, sub, 1e9), axis=0, keepdims=True)
    yield
    t = jnp.exp(v2 - v1)
    g1 = p_grp / (1.0 + t)
    g2 = p_grp * t / (1.0 + t)
    gate_t = jnp.where(sub == i1, g1, jnp.where(sub == i2, g2, 0.0))

    onehot_t = jnp.where(sub == grp, 1.0, 0.0)
    csum = jnp.where(live, jnp.sum(onehot_t, axis=1, keepdims=True), 0.0)
    rr = lax.broadcasted_iota(jnp.int32, (tm, tm), 0)
    cc = lax.broadcasted_iota(jnp.int32, (tm, tm), 1)
    earlier = jnp.where(rr < cc, 1.0, 0.0).astype(BF16)
    padded = jnp.concatenate([onehot_t, jnp.zeros_like(onehot_t)], axis=0).astype(BF16)
    same_before = _dot(padded, earlier)[0:n_sub]
    yield
    lower = jnp.sum(jnp.where(sub < grp, csum, 0.0), axis=0, keepdims=True)
    lrank = lower + jnp.sum(onehot_t * same_before, axis=0, keepdims=True)
    route_ref[0:n_sub, :] = gate_t
    route_ref[n_sub:, :] = jnp.broadcast_to(lrank, (n_sub, tm))
    carry = cnt_ref[...]
    lane = lax.broadcasted_iota(jnp.int32, (n_sub, LANES), 1)
    seg_ref[...] = jnp.where(lane == SEG_START, carry,
                             jnp.where(lane == SEG_LEN, csum, 0.0)).astype(jnp.int32)
    cnt_ref[...] = carry + csum


def _outproj_kernel(actp_ref, acts_ref, gates_ref, xa_ref, xb_ref, pa_ref, pb_ref, pc_ref, wo_ref,
                    n2_ref, wrh_ref, wrl_ref, rb_ref, xo_ref, xn_ref, route_ref, seg_ref, cnt_ref,
                    split_ref, *, n_xa):
    i = pl.program_id(0)

    @pl.when(i == 0)
    def _():
        cnt_ref[...] = jnp.zeros_like(cnt_ref)
        split_ref[...] = jnp.zeros_like(split_ref)

    tile = jnp.minimum(i, pl.num_programs(0) - 2)
    _interleave(
        _outproj_route(i >= 1, split_ref, wrh_ref, wrl_ref, rb_ref, route_ref, seg_ref, cnt_ref),
        _outproj_main(tile, n_xa, actp_ref, acts_ref, gates_ref, xa_ref, xb_ref, pa_ref, pb_ref,
                      pc_ref, wo_ref, n2_ref, xo_ref, xn_ref, split_ref))


def _lane_prefix(v):
    rr = lax.broadcasted_iota(jnp.int32, (LANES, LANES), 0)
    cc = lax.broadcasted_iota(jnp.int32, (LANES, LANES), 1)
    earlier = jnp.where(rr < cc, 1.0, 0.0).astype(BF16)
    return _dot(jnp.broadcast_to(v, (8, LANES)).astype(BF16), earlier)[0:1]


def _outproj(acts_p, acts_s, z, xa, xb, pa, pb, pc, wo, n2, wrh, wrl, rb, layer):
    m = T_ALL
    nt = m // TM_OUT
    row = lambda i: (jnp.minimum(i, nt - 1), 0)
    routed = lambda i: (jnp.maximum(i - 1, 0), 0, 0)
    _, act_specs = _two_part_specs(TM_OUT, acts_p, acts_s)
    n_xa, x_specs = _two_part_specs(TM_OUT, xa, xb)
    weights = (pa, pb, pc, wo, n2, wrh, wrl, rb)
    return pl.pallas_call(
        functools.partial(_outproj_kernel, n_xa=n_xa),
        grid=(nt + 1,),
        in_specs=act_specs + [pl.BlockSpec((TM_OUT, 3 * D_MODEL), row)] + x_specs
        + [_layer_spec(w, layer) for w in weights],
        out_specs=[
            pl.BlockSpec((TM_OUT, D_MODEL), row),
            pl.BlockSpec((TM_OUT, D_MODEL), row),
            pl.BlockSpec((None, ROUTE_ROWS, TM_OUT), routed),
            pl.BlockSpec((None, 8, LANES), routed),
            pl.BlockSpec((8, LANES), lambda i: (0, 0)),
        ],
        scratch_shapes=[pltpu.VMEM((2, TM_OUT, D_MODEL), BF16)],
        out_shape=[
            jax.ShapeDtypeStruct((m, D_MODEL), F32),
            jax.ShapeDtypeStruct((m, D_MODEL), F32),
            jax.ShapeDtypeStruct((m // TM_OUT, ROUTE_ROWS, TM_OUT), F32),
            jax.ShapeDtypeStruct((m // TM_OUT, 8, LANES), jnp.int32),
            jax.ShapeDtypeStruct((8, LANES), F32),
        ],
        compiler_params=pltpu.CompilerParams(
            dimension_semantics=("arbitrary",), vmem_limit_bytes=VMEM_LIMIT),
        name="outproj",
    )(acts_p, acts_s, z, xa, xb, *weights)


PLAN_GROUP, PLAN_USED, PLAN_END, PLAN_BLOCKS, PLAN_BASE = 0, 1, 2, 3, 4


def _plan_kernel(cnt_ref, plan_ref):
    sub_i = lax.broadcasted_iota(jnp.int32, (8, LANES), 0)
    lane8_i = lax.broadcasted_iota(jnp.int32, (8, LANES), 1)
    cnt = jnp.sum(jnp.where(sub_i == lane8_i, cnt_ref[...], 0.0), axis=0, keepdims=True)
    blocks = jnp.floor((cnt + (MOE_BLK - 1)) * (1.0 / MOE_BLK))
    start = _lane_prefix(blocks)
    end = start + blocks
    lane_i = lax.broadcasted_iota(jnp.int32, (1, LANES), 1)
    lane = lane_i.astype(F32)
    grp_of_blk = jnp.zeros((1, LANES), F32)
    for g in range(MOE_GROUPS):
        end_g = jnp.sum(jnp.where(lane_i == g, end, 0.0), axis=-1, keepdims=True)
        grp_of_blk += jnp.where(lane >= end_g, 1.0, 0.0)
    grp_of_blk = jnp.minimum(grp_of_blk, MOE_GROUPS - 1)
    n_used = jnp.sum(blocks, axis=-1, keepdims=True)
    row = lax.broadcasted_iota(jnp.int32, (8, LANES), 0)
    plan_ref[...] = jnp.where(
        row == PLAN_GROUP, grp_of_blk,
        jnp.where(row == PLAN_USED, n_used,
                  jnp.where(row == PLAN_END, end,
                            jnp.where(row == PLAN_BLOCKS, blocks,
                                      jnp.where(row == PLAN_BASE, start * MOE_BLK, 0.0))))
    ).astype(jnp.int32)


def _plan(cnt):
    return pl.pallas_call(
        _plan_kernel,
        out_shape=jax.ShapeDtypeStruct((8, LANES), jnp.int32),
        name="plan",
    )(cnt)


SEG_START, SEG_LEN = 0, 1


def _segment_copies(seg_ref, base_ref, make_copy):
    local = 0
    for g in range(MOE_GROUPS):
        n = seg_ref[g, SEG_LEN]
        first = base_ref[g] + seg_ref[g, SEG_START]
        k = TM_ROW
        while k >= 1:
            done = n & ~(2 * k - 1)
            @pl.when((n & k) != 0)
            def _():
                make_copy(local + done, first + done, k).start()
            k //= 2
        local = local + n


def _perm_matrix(route_ref):
    row = lax.broadcasted_iota(jnp.int32, (TM_ROW, TM_ROW), 0).astype(F32)
    return jnp.where(row == route_ref[ROUTE_LRANK:ROUTE_LRANK + 1, :], 1.0, 0.0).astype(BF16)


SUB = 8
SUB_X = D_MODEL // 2 // LANES
SUB_GATE = SUB_X
HI_MASK = -65536


def _rows(first_row, n_rows):
    return pl.ds(pl.multiple_of(first_row * SUB, SUB), n_rows * SUB)


def _sublane(s, n_rows):
    return pl.ds(s, n_rows, stride=SUB)


def _scatter_kernel(nu_ref, end_ref, nb_ref, base_ref, seg_ref, route_ref, xn_ref, xs_ref,
                    zbuf, sbuf, sems):
    i = pl.program_id(0)
    n = pl.num_programs(0)
    slot = i % 2
    sem = sems.at[0]

    def zero_block(b):
        return pltpu.make_async_copy(zbuf, xs_ref.at[_rows(b * MOE_BLK, MOE_BLK)], sem)

    def each_unfilled_block(fn):
        for g in range(MOE_GROUPS):
            @pl.when(nb_ref[g] > 0)
            def _():
                fn(zero_block(end_ref[g] - 1))
        for b in range(T_ALL // MOE_BLK, N_BLK):
            @pl.when(b >= nu_ref[0])
            def _():
                fn(zero_block(b))

    @pl.when(i == 0)
    def _():
        zbuf[...] = jnp.zeros_like(zbuf)
        sbuf[...] = jnp.zeros_like(sbuf)
        each_unfilled_block(lambda c: c.start())
        each_unfilled_block(lambda c: c.wait())

    def tile_done(s):
        pltpu.make_async_copy(sbuf.at[s], xs_ref.at[_rows(0, TM_ROW)], sems.at[s]).wait()

    @pl.when(i >= 2)
    def _():
        tile_done(slot)

    perm = _perm_matrix(route_ref)
    xs = lax.bitcast_convert_type(_dot(perm, xn_ref[...].astype(BF16)), jnp.int32)
    half = D_MODEL // 2
    for s in range(SUB_X):
        hi = xs[:, s * LANES:(s + 1) * LANES] & HI_MASK
        lo = lax.shift_right_logical(xs[:, half + s * LANES:half + (s + 1) * LANES], 16)
        sbuf[slot, _sublane(s, TM_ROW), :] = hi | lo
    gate_t = route_ref[0:EXPERTS_PER_GROUP, :]
    gate_t = jnp.concatenate(
        [gate_t, jnp.zeros((LANES - EXPERTS_PER_GROUP, TM_ROW), F32)], axis=0)
    g1 = gate_t.astype(BF16)
    r1 = gate_t - g1.astype(F32)
    g2 = r1.astype(BF16)
    g3 = (r1 - g2.astype(F32)).astype(BF16)
    sbuf[slot, _sublane(SUB_GATE, TM_ROW), :] = lax.bitcast_convert_type(
        _dot_nt(perm, g1) + _dot_nt(perm, g2) + _dot_nt(perm, g3), jnp.int32)

    def make_copy(src_row, dst_row, k):
        return pltpu.make_async_copy(sbuf.at[slot, _rows(src_row, k)],
                                     xs_ref.at[_rows(dst_row, k)], sems.at[slot])
    _segment_copies(seg_ref, base_ref, make_copy)

    @pl.when(i == n - 1)
    def _():
        tile_done(slot)

        @pl.when(n > 1)
        def _():
            tile_done(1 - slot)


def _scatter(seg, route, xn, n_used, grp_end, grp_blocks, grp_base):
    m = xn.shape[0]
    nt = m // TM_ROW
    grid_spec = pltpu.PrefetchScalarGridSpec(
        num_scalar_prefetch=4,
        grid=(nt,),
        in_specs=[
            pl.BlockSpec((None, 8, LANES), lambda i, *_: (i, 0, 0), memory_space=pltpu.SMEM),
            pl.BlockSpec((None, ROUTE_ROWS, TM_ROW), lambda i, *_: (i, 0, 0)),
            pl.BlockSpec((TM_ROW, D_MODEL), lambda i, *_: (i, 0)),
        ],
        out_specs=pl.BlockSpec(memory_space=pl.ANY),
        scratch_shapes=[pltpu.VMEM((MOE_BLK * SUB, LANES), jnp.int32),
                        pltpu.VMEM((2, TM_ROW * SUB, LANES), jnp.int32),
                        pltpu.SemaphoreType.DMA((2,))],
    )
    return pl.pallas_call(
        _scatter_kernel,
        grid_spec=grid_spec,
        out_shape=jax.ShapeDtypeStruct((N_SORTED * SUB, LANES), jnp.int32),
        compiler_params=pltpu.CompilerParams(
            dimension_semantics=("arbitrary",), vmem_limit_bytes=VMEM_LIMIT),
        name="scatter",
    )(n_used, grp_end, grp_blocks, grp_base, seg, route, xn)


def _ffn_kernel(bg_ref, nu_ref, xs_ref, w1_ref, w3_ref, w2_ref, y_ref):
    del bg_ref
    b = pl.program_id(0)

    @pl.when(b < nu_ref[0])
    def _():
        packed = [xs_ref[_sublane(s, MOE_BLK), :] for s in range(SUB_X)]
        x = jnp.concatenate(
            [lax.bitcast_convert_type(u & HI_MASK, F32).astype(BF16) for u in packed]
            + [lax.bitcast_convert_type(lax.shift_left(u, 16), F32).astype(BF16) for u in packed],
            axis=1)
        gates = lax.bitcast_convert_type(xs_ref[_sublane(SUB_GATE, MOE_BLK), :], F32)
        hs = []
        for e in range(EXPERTS_PER_GROUP):
            h1 = _dot(x, w1_ref[e])
            h3 = _dot(x, w3_ref[e])
            ge = gates[:, e:e + 1]
            hs.append(jnp.where(ge > 0.0, h1 * _sigmoid(h1) * h3 * ge, 0.0).astype(BF16))
        hcat = jnp.concatenate(hs, axis=1)
        y = _dot(hcat, w2_ref[...].reshape(EXPERTS_PER_GROUP * D_EXPERT, D_MODEL))
        for s in range(SUB):
            y_ref[_sublane(s, MOE_BLK), :] = y[:, s * LANES:(s + 1) * LANES]

    @pl.when(b >= nu_ref[0])
    def _():
        y_ref[...] = jnp.zeros_like(y_ref)


def _ffn(blk_group, n_used, xs, w1, w3, w2, layer):
    wmap = lambda b, bg, nu: (layer * MOE_GROUPS + bg[b], 0, 0, 0)
    grid_spec = pltpu.PrefetchScalarGridSpec(
        num_scalar_prefetch=2,
        grid=(N_BLK,),
        in_specs=[
            pl.BlockSpec((MOE_BLK * SUB, LANES), lambda b, bg, nu: (b, 0)),
            pl.BlockSpec((None, EXPERTS_PER_GROUP, D_MODEL, D_EXPERT), wmap),
            pl.BlockSpec((None, EXPERTS_PER_GROUP, D_MODEL, D_EXPERT), wmap),
            pl.BlockSpec((None, EXPERTS_PER_GROUP, D_EXPERT, D_MODEL), wmap),
        ],
        out_specs=pl.BlockSpec((MOE_BLK * SUB, LANES), lambda b, bg, nu: (b, 0)),
    )
    return pl.pallas_call(
        _ffn_kernel,
        grid_spec=grid_spec,
        out_shape=jax.ShapeDtypeStruct((N_SORTED * SUB, LANES), F32),
        compiler_params=pltpu.CompilerParams(
            dimension_semantics=("arbitrary",), vmem_limit_bytes=VMEM_LIMIT),
        name="ffn",
    )(blk_group, n_used, xs, w1, w3, w2)


def _combine_kernel(base_ref, seg_ref, segn_ref, route_ref, ys_hbm, x_ref, g_ref, *refs, final):
    outs, (buf, sem) = refs[:-2], refs[-2:]
    i = pl.program_id(0)
    n = pl.num_programs(0)
    slot = i % 2

    def gather(seg, s):
        def make_copy(buf_row, ys_row, k):
            return pltpu.make_async_copy(ys_hbm.at[_rows(ys_row, k)],
                                         buf.at[s, _rows(buf_row, k)], sem.at[s])
        _segment_copies(seg, base_ref, make_copy)

    @pl.when(i == 0)
    def _():
        gather(seg_ref, 0)

    @pl.when(i + 1 < n)
    def _():
        gather(segn_ref, 1 - slot)

    pltpu.make_async_copy(ys_hbm.at[_rows(0, TM_ROW)], buf.at[slot], sem.at[slot]).wait()

    perm = _perm_matrix(route_ref)
    tn = (((0,), (0,)), ((), ()))
    cols = []
    for s in range(SUB):
        hi, lo = _split_bf16(buf[slot, _sublane(s, TM_ROW), :])
        cols.append(lax.dot_general(perm, hi, tn, preferred_element_type=F32)
                    + lax.dot_general(perm, lo, tn, preferred_element_type=F32))
    y = x_ref[...] + jnp.concatenate(cols, axis=1)
    if not final:
        outs[0][...] = y
    else:
        y = _rms(y, g_ref[...])

        @pl.when(i < T_PROMPT // TM_ROW)
        def _():
            outs[0][...] = y

        @pl.when(i >= T_PROMPT // TM_ROW)
        def _():
            outs[1][...] = y


def _combine(seg, route, grp_base, ys, x, g, final):
    m = x.shape[0]
    nt = m // TM_ROW
    n_p = T_PROMPT // TM_ROW
    smem = functools.partial(pl.BlockSpec, (None, 8, LANES), memory_space=pltpu.SMEM)
    tile = (TM_ROW, D_MODEL)
    if final:
        out_specs = [pl.BlockSpec(tile, lambda i, *_: (jnp.minimum(i, n_p - 1), 0)),
                     pl.BlockSpec(tile, lambda i, *_: (jnp.maximum(i - n_p, 0), 0))]
        out_shape = [jax.ShapeDtypeStruct((T_PROMPT, D_MODEL), F32),
                     jax.ShapeDtypeStruct((T_SAMPLE, D_MODEL), F32)]
    else:
        out_specs = [pl.BlockSpec(tile, lambda i, *_: (i, 0))]
        out_shape = [jax.ShapeDtypeStruct((m, D_MODEL), F32)]
    grid_spec = pltpu.PrefetchScalarGridSpec(
        num_scalar_prefetch=1,
        grid=(nt,),
        in_specs=[
            smem(lambda i, *_: (i, 0, 0)),
            smem(lambda i, *_: (jnp.minimum(i + 1, nt - 1), 0, 0)),
            pl.BlockSpec((None, ROUTE_ROWS, TM_ROW), lambda i, *_: (i, 0, 0)),
            pl.BlockSpec(memory_space=pl.ANY),
            pl.BlockSpec(tile, lambda i, *_: (i, 0)),
            pl.BlockSpec(g.shape, lambda i, *_: (0, 0)),
        ],
        out_specs=out_specs,
        scratch_shapes=[pltpu.VMEM((2, TM_ROW * SUB, LANES), F32),
                        pltpu.SemaphoreType.DMA((2,))],
    )
    return pl.pallas_call(
        functools.partial(_combine_kernel, final=final),
        grid_spec=grid_spec,
        out_shape=out_shape,
        compiler_params=pltpu.CompilerParams(
            dimension_semantics=("arbitrary",), vmem_limit_bytes=VMEM_LIMIT),
        name="combine",
    )(grp_base, seg, seg, route, ys, x, g)


def _prep_weights(w_in, gla_a2, cm_ws, cm_b, router_group_w, router_group_b,
                  router_expert_w, router_expert_b):
    off = {}
    o = 0
    for name, n in (("h", 512), ("cg", 512), ("bg", 512), ("q", 512), ("k", 512), ("v", 1024),
                    ("r", 1024), ("alr", 16), ("u", 512), ("vv", 512), ("ga", 1024),
                    ("gb", 1024), ("gc", 1024)):
        off[name] = (o, n)
        o += n
    order = ("ga", "gb", "gc", "v", "r", "q", "k", "h", "cg", "bg", "u", "vv")
    w_t = jnp.swapaxes(w_in, 1, 2)
    w_z = jnp.concatenate([w_t[:, off[n][0]:off[n][0] + off[n][1]] for n in order],
                          axis=1).astype(BF16)
    a0 = off["alr"][0]
    w_alr = jnp.pad(w_t[:, a0:a0 + GLA_LOWRANK],
                    ((0, 0), (0, LANES - GLA_LOWRANK), (0, 0))).astype(BF16)
    a2 = jnp.pad(gla_a2, ((0, 0), (0, LANES - GLA_LOWRANK), (0, 0))).astype(BF16)
    ws_p = jnp.tril(cm_ws).astype(BF16)
    small = jnp.tril(cm_ws[:, :, :DEC_SEQ, :DEC_SEQ])
    eye = jnp.eye(SEQ_PER_BLK, dtype=F32)
    ws_s = jnp.einsum("ij,lgab->lgiajb", eye, small).reshape(
        DEPTH, CM_GROUPS, ROWS_S, ROWS_S).astype(BF16)
    cmb_p = jnp.broadcast_to(jnp.transpose(cm_b, (0, 2, 1))[:, :, :, None],
                             (DEPTH, CM_CHUNK, CM_GROUPS, CM_GCH)).reshape(DEPTH, CM_CHUNK, CM_CH)
    cmb_s = jnp.tile(cmb_p[:, :DEC_SEQ], (1, SEQ_PER_BLK, 1))
    pad = LANES - MOE_GROUPS - N_EXPERTS
    w_r = jnp.pad(jnp.swapaxes(jnp.concatenate([router_group_w, router_expert_w], axis=-1), 1, 2),
                  ((0, 0), (0, pad), (0, 0)))
    w_r_hi = w_r.astype(BF16)
    w_r_lo = (w_r - w_r_hi.astype(F32)).astype(BF16)
    r_b = jnp.pad(jnp.concatenate([router_group_b, router_expert_b], axis=-1),
                  ((0, 0), (0, pad)))[:, :, None]
    return w_z, w_alr, a2, ws_p, ws_s, cmb_p, cmb_s, w_r_hi, w_r_lo, r_b


def kernel(x_prompt, x_sample, state_conv, state_gla, norm1_g, w_in, conv_w, gla_a2, gla_a_b,
           gla_norm_g, cm_norm_g, cm_ws, cm_b, proj_a, proj_b, proj_c, w_out, norm2_g,
           router_group_w, router_group_b, router_expert_w, router_expert_b,
           exp_w1, exp_w3, exp_w2, final_norm_g):
    (w_z, w_alr, a2, ws_p, ws_s, cmb_p, cmb_s, w_r_hi, w_r_lo, r_b) = _prep_weights(
        w_in, gla_a2, cm_ws, cm_b, router_group_w, router_group_b, router_expert_w,
        router_expert_b)
    pa, pb, pc, wo = (w.astype(BF16) for w in (proj_a, proj_b, proj_c, w_out))
    grouped = (DEPTH * MOE_GROUPS, EXPERTS_PER_GROUP)
    w1 = exp_w1.astype(BF16).reshape(grouped + (D_MODEL, D_EXPERT))
    w3 = exp_w3.astype(BF16).reshape(grouped + (D_MODEL, D_EXPERT))
    w2 = exp_w2.astype(BF16).reshape(grouped + (D_EXPERT, D_MODEL))
    n1 = norm1_g[:, None, :]
    n2 = norm2_g[:, None, :]
    ab = gla_a_b[:, None, :]
    gng = gla_norm_g[:, None, :]
    cmg = cm_norm_g.reshape(DEPTH, 1, CM_CH)
    fg = final_norm_g[None, :]
    xa = x_prompt.reshape(T_PROMPT, D_MODEL)
    xb = x_sample.reshape(T_SAMPLE, D_MODEL)
    gla_s = None
    conv_p, gla_p, conv_s, cmv_s = [], [], [], []
    for l in range(DEPTH):
        z, alr = _inproj(xa, xb, n1, w_z, w_alr, l)
        acts_p, nconv, ngla = _mix_prompt(z, alr, a2, ab, conv_w, gng, cmg, ws_p, cmb_p, l)
        sc = state_conv[l]
        p2 = jnp.pad(sc, ((0, 0), (0, DEC_SEQ - 2), (0, 0))).reshape(T_SAMPLE, CONV_CH)
        p1 = jnp.pad(sc[:, 1:2], ((0, 0), (0, DEC_SEQ - 1), (0, 0))).reshape(T_SAMPLE, CONV_CH)
        acts_s, cin_s, gla_s, vrows = _mix_sample(z, alr, a2, ab, conv_w, gng, cmg, ws_s, cmb_s,
                                                  p1, p2, state_gla, gla_s, l)
        conv_p.append(nconv)
        gla_p.append(ngla)
        conv_s.append(cin_s.reshape(DEC_BATCH, DEC_SEQ, CONV_CH)[:, DEC_SEQ - (CONV_K - 1):])
        cmv_s.append(vrows.reshape(DEC_BATCH, DEC_SEQ, CM_CH))

        x, xn, route, seg, cnt = _outproj(acts_p, acts_s, z, xa, xb, pa, pb, pc, wo, n2,
                                          w_r_hi, w_r_lo, r_b, l)
        plan = _plan(cnt)
        n_used = plan[PLAN_USED, :1]
        grp_base = plan[PLAN_BASE, :MOE_GROUPS]
        xs = _scatter(seg, route, xn, n_used, plan[PLAN_END, :MOE_GROUPS],
                      plan[PLAN_BLOCKS, :MOE_GROUPS], grp_base)
        ys = _ffn(plan[PLAN_GROUP, :N_BLK], n_used, xs, w1, w3, w2, l)
        out = _combine(seg, route, grp_base, ys, x, fg, l == DEPTH - 1)
        xa = xb = out[0]

    y_prompt = out[0].reshape(BATCH, SEQ, D_MODEL)
    y_sample = out[1].reshape(DEC_BATCH, DEC_SEQ, D_MODEL)
    return (y_prompt, y_sample, jnp.stack(conv_p), jnp.stack(gla_p), jnp.stack(conv_s),
            gla_s, jnp.stack(cmv_s))
```

```python
import functools

import jax
import jax.numpy as jnp
from jax import lax
from jax.experimental import pallas as pl
from jax.experimental.pallas import tpu as pltpu

F32 = jnp.float32
BF16 = jnp.bfloat16

D_MODEL = 1024
BATCH = 8
SEQ = 2048
DEPTH = 2
DEC_BATCH = 128
DEC_SEQ = 8
CONV_K = 3
CONV_CH = 512
GLA_HEADS = 4
GLA_DK = 128
GLA_DV = 256
GLA_QK = GLA_HEADS * GLA_DK
GLA_V = GLA_HEADS * GLA_DV
GLA_LOWRANK = 16
GLA_TAU = 16.0
GLA_CHUNK = 64
CM_GROUPS = 4
CM_CHUNK = 128
CM_GCH = 128
CM_CH = 512
MOE_GROUPS = 8
EXPERTS_PER_GROUP = 8
N_EXPERTS = 64
D_EXPERT = 256
EPS = 1e-6

LANES = 128
T_PROMPT = BATCH * SEQ
T_SAMPLE = DEC_BATCH * DEC_SEQ
T_ALL = T_PROMPT + T_SAMPLE

COL_GA, COL_GB, COL_GC = 0, 1024, 2048
COL_V, COL_R, COL_Q, COL_K = 3072, 4096, 5120, 5632
COL_H, COL_CG, COL_BG, COL_U, COL_VV = 6144, 6656, 7168, 7680, 8192
Z_COLS = 8704
ACT_COLS = 2048
ROUTE_ROWS = 2 * EXPERTS_PER_GROUP
ROUTE_LRANK = EXPERTS_PER_GROUP

TM_IN = 1024
TN_IN = Z_COLS // 4
TC_MIX = 256
SEQ_PER_BLK = 16
ROWS_S = SEQ_PER_BLK * DEC_SEQ
TM_OUT = 256
TM_ROW = TM_OUT
MOE_BLK = 256
N_BLK = T_ALL // MOE_BLK + MOE_GROUPS
N_SORTED = N_BLK * MOE_BLK
VMEM_LIMIT = 56 * 1024 * 1024


def _sigmoid(x):
    return 0.5 * jnp.tanh(0.5 * x) + 0.5


def _gelu_tanh(x):
    c = 0.7978845608028654
    half = 0.5 * x
    return half + half * jnp.tanh(x * (c + (c * 0.044715) * (x * x)))


def _log_sigmoid(x):
    log2_e = 1.4426950408889634
    ln_2 = 0.6931471805599453
    return jnp.minimum(x, 0.0) - ln_2 * jnp.log2(1.0 + jnp.exp2(-log2_e * jnp.abs(x)))


def _rms(x, g):
    ms = jnp.mean(x * x, axis=-1, keepdims=True)
    return x * lax.rsqrt(ms + EPS) * g


def _split_bf16(x):
    hi = x.astype(BF16)
    lo = (x - hi.astype(F32)).astype(BF16)
    return hi, lo


def _dot(a, b):
    return jnp.dot(a, b, preferred_element_type=F32)


def _dot_nt(a, b):
    return lax.dot_general(a, b, (((1,), (1,)), ((), ())), preferred_element_type=F32)


def _layer_spec(arr, layer):
    nd = arr.ndim - 1
    return pl.BlockSpec((None,) + arr.shape[1:], lambda *g: (layer,) + (0,) * nd)


def _const_spec(arr):
    nd = arr.ndim
    return pl.BlockSpec(arr.shape, lambda *g: (0,) * nd)


def _two_part_specs(tile, xa, xb):
    n_a = xa.shape[0] // tile
    n_b = xb.shape[0] // tile
    return n_a, [
        pl.BlockSpec((tile, xa.shape[1]), lambda i, *_: (jnp.minimum(i, n_a - 1), 0)),
        pl.BlockSpec((tile, xb.shape[1]), lambda i, *_: (jnp.clip(i - n_a, 0, n_b - 1), 0)),
    ]


def _inproj_kernel(xa_ref, xb_ref, g_ref, w_ref, wa_ref, e1_ref, e3_ref, e2_ref,
                   z_ref, a_ref, o1_ref, o3_ref, o2_ref, xn_ref, *, n_a):
    o1_ref[...] = e1_ref[...].astype(BF16)
    o3_ref[...] = e3_ref[...].astype(BF16)
    o2_ref[...] = e2_ref[...].astype(BF16)

    @pl.when(pl.program_id(1) == 0)
    def _():
        x = jnp.where(pl.program_id(0) < n_a, xa_ref[...], xb_ref[...])
        xn = _rms(x, g_ref[...]).astype(BF16)
        xn_ref[...] = xn
        a_ref[...] = _dot_nt(xn, wa_ref[...]).astype(BF16)

    z_ref[...] = _dot_nt(xn_ref[...], w_ref[...]).astype(BF16)


def _inproj(xa, xb, g, w, wa, e1, e3, e2, layer):
    m = T_ALL
    n_i, n_j = m // TM_IN, Z_COLS // TN_IN
    assert n_i * n_j >= N_EXPERTS
    n_a, x_specs = _two_part_specs(TM_IN, xa, xb)
    expert = lambda i, j: jnp.minimum(i * n_j + j, N_EXPERTS - 1)
    up, down = (D_MODEL, D_EXPERT), (D_EXPERT, D_MODEL)
    return pl.pallas_call(
        functools.partial(_inproj_kernel, n_a=n_a),
        grid=(n_i, n_j),
        in_specs=x_specs + [
            _layer_spec(g, layer),
            pl.BlockSpec((None, TN_IN, D_MODEL), lambda i, j: (layer, j, 0)),
            _layer_spec(wa, layer),
            pl.BlockSpec((None, None) + up, lambda i, j: (layer, expert(i, j), 0, 0)),
            pl.BlockSpec((None, None) + up, lambda i, j: (layer, expert(i, j), 0, 0)),
            pl.BlockSpec((None, None) + down, lambda i, j: (layer, expert(i, j), 0, 0)),
        ],
        out_specs=[
            pl.BlockSpec((TM_IN, TN_IN), lambda i, j: (i, j)),
            pl.BlockSpec((TM_IN, LANES), lambda i, j: (i, 0)),
            pl.BlockSpec((None,) + up, lambda i, j: (expert(i, j), 0, 0)),
            pl.BlockSpec((None,) + up, lambda i, j: (expert(i, j), 0, 0)),
            pl.BlockSpec((None,) + down, lambda i, j: (expert(i, j), 0, 0)),
        ],
        out_shape=[
            jax.ShapeDtypeStruct((m, Z_COLS), BF16),
            jax.ShapeDtypeStruct((m, LANES), BF16),
            jax.ShapeDtypeStruct((N_EXPERTS,) + up, BF16),
            jax.ShapeDtypeStruct((N_EXPERTS,) + up, BF16),
            jax.ShapeDtypeStruct((N_EXPERTS,) + down, BF16),
        ],
        scratch_shapes=[pltpu.VMEM((TM_IN, D_MODEL), BF16)],
        compiler_params=pltpu.CompilerParams(
            dimension_semantics=("arbitrary", "arbitrary"), vmem_limit_bytes=VMEM_LIMIT),
        name="inproj",
    )(xa, xb, g, w, wa, e1, e3, e2)


def _gla_log_decay(alr_ref, a2_ref, ab_ref):
    la = _log_sigmoid(_dot(alr_ref[...], a2_ref[...]) + ab_ref[...]) * (1.0 / GLA_TAU)
    return _split_bf16(la)


def _masked_sum(mask, la_hi, la_lo):
    m = jnp.where(mask, 1.0, 0.0).astype(BF16)
    return _dot(m, la_hi) + _dot(m, la_lo)


def _gla_decay_prefix(alr_ref, a2_ref, ab_ref, tril_mask):
    la_hi, la_lo = _gla_log_decay(alr_ref, a2_ref, ab_ref)
    return _masked_sum(tril_mask, la_hi, la_lo)


def _gla_decay_terms(alr_ref, a2_ref, ab_ref, tril_mask, same_mask):
    la_hi, la_lo = _gla_log_decay(alr_ref, a2_ref, ab_ref)
    return (la_hi, la_lo, _masked_sum(tril_mask, la_hi, la_lo),
            _masked_sum(same_mask, la_hi, la_lo))


def _gla_out_gate(o, g_ref, r):
    return _rms(o, g_ref[...]) * (r * _sigmoid(r))


def _chunk_mlp_group(g, bgu_ref, vv_ref, cmg_ref, ws_ref, cmb_ref, n_chunks):
    sl = slice(g * CM_GCH, (g + 1) * CM_GCH)
    ug = _gelu_tanh(bgu_ref[:, CONV_CH + g * CM_GCH:CONV_CH + (g + 1) * CM_GCH])
    vg = _rms(_gelu_tanh(vv_ref[:, sl]).astype(F32), cmg_ref[:, sl])
    vgb = vg.astype(BF16)
    rows = []
    for j in range(n_chunks):
        rs = slice(j * CM_CHUNK, (j + 1) * CM_CHUNK)
        rows.append(_dot(ws_ref[g], vgb[rs]) + cmb_ref[:, sl])
    s = rows[0] if n_chunks == 1 else jnp.concatenate(rows, axis=0)
    return ug * s, vg


def _chunk_mlp(bgu_ref, vv_ref, cmg_ref, ws_ref, cmb_ref, n_chunks):
    parts = [_chunk_mlp_group(g, bgu_ref, vv_ref, cmg_ref, ws_ref, cmb_ref, n_chunks)
             for g in range(CM_GROUPS)]
    return (jnp.concatenate([p[0] for p in parts], axis=1),
            jnp.concatenate([p[1] for p in parts], axis=1))


N_Z_VIEWS = 7
N_MIX_W = 7
SEQ_PER_STEP = 2
SEQ_STAGGER = 4


def _mix_prompt_kernel(*refs):
    n_z = N_Z_VIEWS * SEQ_PER_STEP
    weights = refs[n_z:n_z + N_MIX_W]
    acts_ref, nconv_ref, ngla_ref, st_ref, carry_ref = refs[n_z + N_MIX_W:]

    @pl.when(pl.program_id(1) == 0)
    def _():
        st_ref[...] = jnp.zeros_like(st_ref)
        carry_ref[...] = jnp.zeros_like(carry_ref)

    def delayed(gen, n_stages):
        for _ in range(n_stages):
            yield
        yield from gen

    _interleave(*[
        delayed(_mix_prompt_seq(*refs[N_Z_VIEWS * s:N_Z_VIEWS * (s + 1)], *weights,
                                acts_ref.at[s], nconv_ref.at[s], st_ref.at[s], carry_ref.at[s]),
                s * SEQ_STAGGER)
        for s in range(SEQ_PER_STEP)])

    @pl.when(pl.program_id(1) == pl.num_programs(1) - 1)
    def _():
        for s in range(SEQ_PER_STEP):
            for hd in range(GLA_HEADS):
                ngla_ref[s, 0, hd] = st_ref[s, hd].T


def _mix_prompt_seq(v_ref, r_ref, qk_ref, hcg_ref, bgu_ref, vv_ref, alr_ref,
                    a2_ref, ab_ref, cw_ref, gng_ref, cmg_ref, ws_ref, cmb_ref,
                    acts_ref, nconv_ref, st_ref, carry_ref):
    tc = TC_MIX

    h = hcg_ref[:, :CONV_CH].astype(F32)
    cg = hcg_ref[:, CONV_CH:].astype(F32)
    bg = bgu_ref[:, :CONV_CH].astype(F32)
    cin = cg * h
    rr = lax.broadcasted_iota(jnp.int32, (tc, tc), 0)
    cc = lax.broadcasted_iota(jnp.int32, (tc, tc), 1)
    cin_b = cin.astype(BF16)
    x1 = _dot(jnp.where(rr - cc == 1, 1.0, 0.0).astype(BF16), cin_b)
    x2 = _dot(jnp.where(rr - cc == 2, 1.0, 0.0).astype(BF16), cin_b)
    conv = x2 * cw_ref[0:1, :] + x1 * cw_ref[1:2, :] + cin * cw_ref[2:3, :]
    c0 = carry_ref[0:1, :]
    c1 = carry_ref[1:2, :]
    row8 = lax.broadcasted_iota(jnp.int32, (8, 1), 0)
    head = jnp.where(row8 == 0, c0 * cw_ref[0:1, :] + c1 * cw_ref[1:2, :],
                     jnp.where(row8 == 1, c1 * cw_ref[0:1, :], 0.0))
    conv = jnp.concatenate([conv[0:8] + head, conv[8:]], axis=0)
    acts_ref[:, 0:CONV_CH] = (bg * conv).astype(BF16)
    carry_ref[0:2, :] = cin[tc - 2:tc, :]
    nconv_ref[0] = cin[tc - 2:tc, :]
    yield

    rr = lax.broadcasted_iota(jnp.int32, (tc, tc), 0)
    cc = lax.broadcasted_iota(jnp.int32, (tc, tc), 1)
    same = (rr >> 6) == (cc >> 6)
    tril = same & (cc <= rr)
    b = _gla_decay_prefix(alr_ref, a2_ref, ab_ref, tril)
    yield
    n_chunks = tc // GLA_CHUNK
    b_last = [b[(c + 1) * GLA_CHUNK - 1:(c + 1) * GLA_CHUNK, :] for c in range(n_chunks)]
    bl = jnp.concatenate([jnp.broadcast_to(r_, (GLA_CHUNK, GLA_QK)) for r_ in b_last], axis=0)
    q = qk_ref[:, :GLA_QK].astype(F32) * (GLA_DK ** -0.5)
    k = qk_ref[:, GLA_QK:].astype(F32)
    q_t = (q * jnp.exp(b)).astype(BF16)
    k_t = (k * jnp.exp(-b)).astype(BF16)
    k_end = (k * jnp.exp(bl - b)).astype(BF16)
    yield
    states = [st_ref[hd] for hd in range(GLA_HEADS)]
    k_cols = [slice(hd * GLA_DK, (hd + 1) * GLA_DK) for hd in range(GLA_HEADS)]
    v_cols = [slice(hd * GLA_DV, (hd + 1) * GLA_DV) for hd in range(GLA_HEADS)]
    o_intra = []
    for hd in range(GLA_HEADS):
        att = jnp.where(tril, _dot_nt(q_t[:, k_cols[hd]], k_t[:, k_cols[hd]]), 0.0).astype(BF16)
        o_intra.append(_dot(att, v_ref[:, v_cols[hd]]))
    yield

    o_rows = [[] for _ in range(GLA_HEADS)]
    assert n_chunks == CM_GROUPS
    for c in range(n_chunks):
        rs = slice(c * GLA_CHUNK, (c + 1) * GLA_CHUNK)
        for hd in range(GLA_HEADS):
            ks = k_cols[hd]
            st = states[hd]
            o_rows[hd].append(o_intra[hd][rs] + _dot_nt(q_t[rs, ks], st.astype(BF16)))
            upd = lax.dot_general(v_ref[rs, v_cols[hd]], k_end[rs, ks], (((0,), (0,)), ((), ())),
                                  preferred_element_type=F32)
            states[hd] = jnp.exp(b_last[c][:, ks]) * st + upd
        us, _ = _chunk_mlp_group(c, bgu_ref, vv_ref, cmg_ref, ws_ref, cmb_ref, tc // CM_CHUNK)
        acts_ref[:, CONV_CH + GLA_V + c * CM_GCH:CONV_CH + GLA_V + (c + 1) * CM_GCH] = (
            us.astype(BF16))
        yield

    for hd in range(GLA_HEADS):
        o = jnp.concatenate(o_rows[hd], axis=0)
        r = r_ref[:, v_cols[hd]]
        acts_ref[:, CONV_CH + hd * GLA_DV:CONV_CH + (hd + 1) * GLA_DV] = (
            _gla_out_gate(o, gng_ref, r).astype(BF16))
        st_ref[hd] = states[hd]
        if hd % 2 == 1:
            yield


def _z_specs(rows, row_map):
    def spec(width, col):
        blk = col // width
        return pl.BlockSpec((rows, width), lambda *g: (row_map(*g), blk))
    return [spec(1024, COL_V), spec(1024, COL_R), spec(1024, COL_Q), spec(1024, COL_H),
            spec(1024, COL_BG), spec(512, COL_VV)]


def _mix_prompt(z, alr, a2, ab, cw, gng, cmg, ws, cmb, layer):
    nt = SEQ // TC_MIX
    nb = BATCH // SEQ_PER_STEP
    small = (a2, ab, cw, gng, cmg, ws, cmb)
    assert len(small) == N_MIX_W
    in_specs, args = [], []
    for s in range(SEQ_PER_STEP):
        row_map = lambda b, c, s=s: (b + s * nb) * nt + c
        in_specs += _z_specs(TC_MIX, row_map) + [
            pl.BlockSpec((TC_MIX, LANES), lambda b, c, row_map=row_map: (row_map(b, c), 0))]
        args += [z] * (N_Z_VIEWS - 1) + [alr]
    in_specs += [_layer_spec(a, layer) for a in small]
    acts, nconv, ngla = pl.pallas_call(
        _mix_prompt_kernel,
        grid=(nb, nt),
        in_specs=in_specs,
        out_specs=[
            pl.BlockSpec((SEQ_PER_STEP, TC_MIX, ACT_COLS), lambda b, c: (0, b * nt + c, 0)),
            pl.BlockSpec((SEQ_PER_STEP, 1, CONV_K - 1, CONV_CH), lambda b, c: (0, b, 0, 0)),
            pl.BlockSpec((SEQ_PER_STEP, 1, GLA_HEADS, GLA_DK, GLA_DV),
                         lambda b, c: (0, b, 0, 0, 0)),
        ],
        out_shape=[
            jax.ShapeDtypeStruct((SEQ_PER_STEP, T_PROMPT // SEQ_PER_STEP, ACT_COLS), BF16),
            jax.ShapeDtypeStruct((SEQ_PER_STEP, nb, CONV_K - 1, CONV_CH), F32),
            jax.ShapeDtypeStruct((SEQ_PER_STEP, nb, GLA_HEADS, GLA_DK, GLA_DV), F32),
        ],
        scratch_shapes=[pltpu.VMEM((SEQ_PER_STEP, GLA_HEADS, GLA_DV, GLA_DK), F32),
                        pltpu.VMEM((SEQ_PER_STEP, 8, CONV_CH), F32)],
        compiler_params=pltpu.CompilerParams(
            dimension_semantics=("arbitrary", "arbitrary"), vmem_limit_bytes=VMEM_LIMIT),
        name="mix_prompt",
    )(*args, *small)
    return (acts.reshape(T_PROMPT, ACT_COLS), nconv.reshape(BATCH, CONV_K - 1, CONV_CH),
            ngla.reshape(BATCH, GLA_HEADS, GLA_DK, GLA_DV))


def _mix_sample_body(v_ref, r_ref, qk_ref, hcg_ref, bgu_ref, vv_ref, alr_ref,
                       a2_ref, ab_ref, cw_ref, gng_ref, cmg_ref, ws_ref, cmb_ref,
                       p1_ref, p2_ref, s0_ref,
                       acts_ref, cin_ref, ns_ref, vrow_ref):
    n = ROWS_S

    pos = lax.broadcasted_iota(jnp.int32, (n, 1), 0) & (DEC_SEQ - 1)
    h = hcg_ref[:, :CONV_CH].astype(F32)
    cg = hcg_ref[:, CONV_CH:].astype(F32)
    bg = bgu_ref[:, :CONV_CH].astype(F32)
    cin = cg * h
    x1 = jnp.where(pos >= 1, pltpu.roll(cin, 1, 0), p1_ref[...])
    x2 = jnp.where(pos >= 2, pltpu.roll(cin, 2, 0), p2_ref[...])
    conv = x2 * cw_ref[0:1, :] + x1 * cw_ref[1:2, :] + cin * cw_ref[2:3, :]
    acts_ref[:, 0:CONV_CH] = (bg * conv).astype(BF16)
    cin_ref[...] = cin

    rr = lax.broadcasted_iota(jnp.int32, (n, n), 0)
    cc = lax.broadcasted_iota(jnp.int32, (n, n), 1)
    same = (rr >> 3) == (cc >> 3)
    tril = same & (cc <= rr)
    la_hi, la_lo, b, bl = _gla_decay_terms(alr_ref, a2_ref, ab_ref, tril, same)
    q = qk_ref[:, :GLA_QK].astype(F32) * (GLA_DK ** -0.5)
    k = qk_ref[:, GLA_QK:].astype(F32)
    q_t = (q * jnp.exp(b)).astype(BF16)
    k_t = (k * jnp.exp(-b)).astype(BF16)
    k_end = k * jnp.exp(bl - b)
    la_hi = la_hi.astype(F32)
    la_lo = la_lo.astype(F32)
    row_seq = lax.broadcasted_iota(jnp.int32, (n, GLA_DK), 0) >> 3
    seq3 = lax.broadcasted_iota(jnp.int32, (SEQ_PER_BLK, GLA_DK, n), 0)
    lane_seq3 = lax.broadcasted_iota(jnp.int32, (SEQ_PER_BLK, GLA_DK, n), 2) >> 3
    mask3 = seq3 == lane_seq3
    ones = jnp.ones((n, GLA_DV), BF16)
    big = SEQ_PER_BLK * GLA_DK

    def per_seq(x_tr):
        x3 = jnp.where(mask3, x_tr[None, :, :], 0.0)
        return x3.reshape(big, n).astype(BF16)

    for hd in range(GLA_HEADS):
        ks = slice(hd * GLA_DK, (hd + 1) * GLA_DK)
        vs = slice(hd * GLA_DV, (hd + 1) * GLA_DV)
        qh = q_t[:, ks]
        vh = v_ref[:, vs]
        att = jnp.where(tril, _dot_nt(qh, k_t[:, ks]), 0.0).astype(BF16)
        o_intra = _dot(att, vh)
        s_old = s0_ref[:, hd].reshape(big, GLA_DV)
        zero = jnp.zeros_like(qh)
        q_big = jnp.concatenate(
            [jnp.where(row_seq == j, qh, zero) for j in range(SEQ_PER_BLK)], axis=1)
        o = o_intra + _dot(q_big, s_old.astype(BF16))
        dlog = _dot(per_seq(la_hi[:, ks].T), ones) + _dot(per_seq(la_lo[:, ks].T), ones)
        upd = _dot(per_seq(k_end[:, ks].T), vh)
        s_new = jnp.exp(dlog) * s_old + upd
        ns_ref[:, hd] = s_new.reshape(SEQ_PER_BLK, GLA_DK, GLA_DV)
        r = r_ref[:, vs]
        acts_ref[:, CONV_CH + hd * GLA_DV:CONV_CH + (hd + 1) * GLA_DV] = (
            _gla_out_gate(o, gng_ref, r).astype(BF16))

    us, vg = _chunk_mlp(bgu_ref, vv_ref, cmg_ref, ws_ref, cmb_ref, 1)
    acts_ref[:, CONV_CH + GLA_V:] = us.astype(BF16)
    vrow_ref[...] = vg


N_MIX_S_IN = 17


def _mix_sample_kernel(*refs, layer):
    if layer == 0:
        @pl.when(pl.program_id(0) == 0)
        def _():
            _mix_sample_body(*refs)

        @pl.when(pl.program_id(0) > 0)
        def _():
            ns_ref = refs[N_MIX_S_IN + 2]
            ns_ref[...] = jnp.zeros_like(ns_ref)
    else:
        _mix_sample_body(*refs[:N_MIX_S_IN], *refs[N_MIX_S_IN + 1:])


def _mix_sample(z, alr, a2, ab, cw, gng, cmg, ws, cmb, p1, p2, s0, ns_all, layer):
    row0 = T_PROMPT // ROWS_S
    n_i = DEC_BATCH // SEQ_PER_BLK
    n_pass = DEPTH if layer == 0 else 1
    blk = lambda p, i: jnp.where(p == 0, i, n_i - 1)
    row_map = lambda p, i: row0 + blk(p, i)
    slot = lambda p, i: (layer + p, i, 0, 0, 0)
    small = (a2, ab, cw, gng, cmg, ws, cmb)
    state_blk = (None, SEQ_PER_BLK, GLA_HEADS, GLA_DK, GLA_DV)
    in_specs = _z_specs(ROWS_S, row_map) + [
        pl.BlockSpec((ROWS_S, LANES), lambda p, i: (row_map(p, i), 0)),
    ] + [_layer_spec(a, layer) for a in small] + [
        pl.BlockSpec((ROWS_S, CONV_CH), lambda p, i: (blk(p, i), 0)),
        pl.BlockSpec((ROWS_S, CONV_CH), lambda p, i: (blk(p, i), 0)),
        pl.BlockSpec(state_blk, lambda p, i: (layer, blk(p, i), 0, 0, 0)),
    ]
    args = (z, z, z, z, z, z, alr, *small, p1, p2, s0)
    assert len(args) == N_MIX_S_IN
    aliases = {}
    if layer > 0:
        in_specs.append(pl.BlockSpec(memory_space=pl.ANY))
        args += (ns_all,)
        aliases = {N_MIX_S_IN: 2}
    return pl.pallas_call(
        functools.partial(_mix_sample_kernel, layer=layer),
        grid=(n_pass, n_i),
        in_specs=in_specs,
        out_specs=[
            pl.BlockSpec((ROWS_S, ACT_COLS), lambda p, i: (blk(p, i), 0)),
            pl.BlockSpec((ROWS_S, CONV_CH), lambda p, i: (blk(p, i), 0)),
            pl.BlockSpec(state_blk, slot),
            pl.BlockSpec((ROWS_S, CM_CH), lambda p, i: (blk(p, i), 0)),
        ],
        out_shape=[
            jax.ShapeDtypeStruct((T_SAMPLE, ACT_COLS), BF16),
            jax.ShapeDtypeStruct((T_SAMPLE, CONV_CH), F32),
            jax.ShapeDtypeStruct((DEPTH, DEC_BATCH, GLA_HEADS, GLA_DK, GLA_DV), F32),
            jax.ShapeDtypeStruct((T_SAMPLE, CM_CH), F32),
        ],
        input_output_aliases=aliases,
        compiler_params=pltpu.CompilerParams(
            dimension_semantics=("arbitrary", "arbitrary"), vmem_limit_bytes=VMEM_LIMIT),
        name="mix_sample",
    )(*args)


def _interleave(*stage_lists):
    pending = list(stage_lists)
    while pending:
        for gen in list(pending):
            if next(gen, StopIteration) is StopIteration:
                pending.remove(gen)


def _outproj_main(i, n_xa, actp_ref, acts_ref, gates_ref, xa_ref, xb_ref, pa_ref, pb_ref, pc_ref,
                  wo_ref, n2_ref, xo_ref, xn_ref, split_ref):
    acts = jnp.where(i < T_PROMPT // TM_OUT, actp_ref[...], acts_ref[...])
    x_in = jnp.where(i < n_xa, xa_ref[...], xb_ref[...])
    ya = _dot(acts[:, :CONV_CH], pa_ref[...])
    yb = _dot(acts[:, CONV_CH:CONV_CH + GLA_V], pb_ref[...])
    yc = _dot(acts[:, CONV_CH + GLA_V:], pc_ref[...])
    yield
    ga = _sigmoid(gates_ref[:, COL_GA:COL_GA + D_MODEL])
    gb = _sigmoid(gates_ref[:, COL_GB:COL_GB + D_MODEL])
    gc = _sigmoid(gates_ref[:, COL_GC:COL_GC + D_MODEL])
    mix = ga * ya + gb * yb + gc * yc
    yield
    x = x_in + _dot(mix.astype(BF16), wo_ref[...])
    xo_ref[...] = x
    yield
    xn = _rms(x, n2_ref[...])
    xn_ref[...] = xn
    hi, lo = _split_bf16(xn)
    yield
    split_ref[0] = hi
    split_ref[1] = lo


def _outproj_route(live, split_ref, wrh_ref, wrl_ref, rb_ref, route_ref, seg_ref, cnt_ref):
    tm = TM_OUT
    hi = split_ref[0]
    lo = split_ref[1]
    logits = (_dot_nt(wrh_ref[...], hi) + _dot_nt(wrh_ref[...], lo) + _dot_nt(wrl_ref[...], hi)
              + rb_ref[...])
    yield
    n_sub = EXPERTS_PER_GROUP
    sub = lax.broadcasted_iota(jnp.int32, (n_sub, tm), 0).astype(F32)
    neg = jnp.float32(-jnp.inf)
    lg = logits[0:MOE_GROUPS]
    gmax = jnp.max(lg, axis=0, keepdims=True)
    grp = jnp.min(jnp.where(lg == gmax, sub, 1e9), axis=0, keepdims=True)
    p_grp = 1.0 / jnp.sum(jnp.exp(lg - gmax), axis=0, keepdims=True)
    yield
    le = logits[MOE_GROUPS:MOE_GROUPS + n_sub]
    for g in range(1, MOE_GROUPS):
        le = jnp.where(grp == g, logits[MOE_GROUPS + g * n_sub:MOE_GROUPS + (g + 1) * n_sub], le)
    v1 = jnp.max(le, axis=0, keepdims=True)
    i1 = jnp.min(jnp.where(le == v1, sub, 1e9), axis=0, keepdims=True)
    le2 = jnp.where(sub == i1, neg, le)
    v2 = jnp.max(le2, axis=0, keepdims=True)
    i2 = jnp.min(jnp.where(le2 == v2, sub, 1e9), axis=0, keepdims=True)
    yield
    t = jnp.exp(v2 - v1)
    g1 = p_grp / (1.0 + t)
    g2 = p_grp * t / (1.0 + t)
    gate_t = jnp.where(sub == i1, g1, jnp.where(sub == i2, g2, 0.0))

    onehot_t = jnp.where(sub == grp, 1.0, 0.0)
    csum = jnp.where(live, jnp.sum(onehot_t, axis=1, keepdims=True), 0.0)
    rr = lax.broadcasted_iota(jnp.int32, (tm, tm), 0)
    cc = lax.broadcasted_iota(jnp.int32, (tm, tm), 1)
    earlier = jnp.where(rr < cc, 1.0, 0.0).astype(BF16)
    padded = jnp.concatenate([onehot_t, jnp.zeros_like(onehot_t)], axis=0).astype(BF16)
    same_before = _dot(padded, earlier)[0:n_sub]
    yield
    lower = jnp.sum(jnp.where(sub < grp, csum, 0.0), axis=0, keepdims=True)
    lrank = lower + jnp.sum(onehot_t * same_before, axis=0, keepdims=True)
    route_ref[0:n_sub, :] = gate_t
    route_ref[n_sub:, :] = jnp.broadcast_to(lrank, (n_sub, tm))
    carry = cnt_ref[...]
    lane = lax.broadcasted_iota(jnp.int32, (n_sub, LANES), 1)
    seg_ref[...] = jnp.where(lane == SEG_START, carry,
                             jnp.where(lane == SEG_LEN, csum, 0.0)).astype(jnp.int32)
    cnt_ref[...] = carry + csum


def _outproj_kernel(actp_ref, acts_ref, gates_ref, xa_ref, xb_ref, pa_ref, pb_ref, pc_ref, wo_ref,
                    n2_ref, wrh_ref, wrl_ref, rb_ref, xo_ref, xn_ref, route_ref, seg_ref, cnt_ref,
                    split_ref, *, n_xa):
    i = pl.program_id(0)

    @pl.when(i == 0)
    def _():
        cnt_ref[...] = jnp.zeros_like(cnt_ref)
        split_ref[...] = jnp.zeros_like(split_ref)

    tile = jnp.minimum(i, pl.num_programs(0) - 2)
    _interleave(
        _outproj_route(i >= 1, split_ref, wrh_ref, wrl_ref, rb_ref, route_ref, seg_ref, cnt_ref),
        _outproj_main(tile, n_xa, actp_ref, acts_ref, gates_ref, xa_ref, xb_ref, pa_ref, pb_ref,
                      pc_ref, wo_ref, n2_ref, xo_ref, xn_ref, split_ref))


def _lane_prefix(v):
    rr = lax.broadcasted_iota(jnp.int32, (LANES, LANES), 0)
    cc = lax.broadcasted_iota(jnp.int32, (LANES, LANES), 1)
    earlier = jnp.where(rr < cc, 1.0, 0.0).astype(BF16)
    return _dot(jnp.broadcast_to(v, (8, LANES)).astype(BF16), earlier)[0:1]


def _outproj(acts_p, acts_s, z, xa, xb, pa, pb, pc, wo, n2, wrh, wrl, rb, layer):
    m = T_ALL
    nt = m // TM_OUT
    row = lambda i: (jnp.minimum(i, nt - 1), 0)
    routed = lambda i: (jnp.maximum(i - 1, 0), 0, 0)
    _, act_specs = _two_part_specs(TM_OUT, acts_p, acts_s)
    n_xa, x_specs = _two_part_specs(TM_OUT, xa, xb)
    weights = (pa, pb, pc, wo, n2, wrh, wrl, rb)
    return pl.pallas_call(
        functools.partial(_outproj_kernel, n_xa=n_xa),
        grid=(nt + 1,),
        in_specs=act_specs + [pl.BlockSpec((TM_OUT, 3 * D_MODEL), row)] + x_specs
        + [_layer_spec(w, layer) for w in weights],
        out_specs=[
            pl.BlockSpec((TM_OUT, D_MODEL), row),
            pl.BlockSpec((TM_OUT, D_MODEL), row),
            pl.BlockSpec((None, ROUTE_ROWS, TM_OUT), routed),
            pl.BlockSpec((None, 8, LANES), routed),
            pl.BlockSpec((8, LANES), lambda i: (0, 0)),
        ],
        scratch_shapes=[pltpu.VMEM((2, TM_OUT, D_MODEL), BF16)],
        out_shape=[
            jax.ShapeDtypeStruct((m, D_MODEL), F32),
            jax.ShapeDtypeStruct((m, D_MODEL), F32),
            jax.ShapeDtypeStruct((m // TM_OUT, ROUTE_ROWS, TM_OUT), F32),
            jax.ShapeDtypeStruct((m // TM_OUT, 8, LANES), jnp.int32),
            jax.ShapeDtypeStruct((8, LANES), F32),
        ],
        compiler_params=pltpu.CompilerParams(
            dimension_semantics=("arbitrary",), vmem_limit_bytes=VMEM_LIMIT),
        name="outproj",
    )(acts_p, acts_s, z, xa, xb, *weights)


PLAN_GROUP, PLAN_USED, PLAN_END, PLAN_BLOCKS, PLAN_BASE = 0, 1, 2, 3, 4


def _plan_kernel(cnt_ref, plan_ref):
    sub_i = lax.broadcasted_iota(jnp.int32, (8, LANES), 0)
    lane8_i = lax.broadcasted_iota(jnp.int32, (8, LANES), 1)
    cnt = jnp.sum(jnp.where(sub_i == lane8_i, cnt_ref[...], 0.0), axis=0, keepdims=True)
    blocks = jnp.floor((cnt + (MOE_BLK - 1)) * (1.0 / MOE_BLK))
    start = _lane_prefix(blocks)
    end = start + blocks
    lane_i = lax.broadcasted_iota(jnp.int32, (1, LANES), 1)
    lane = lane_i.astype(F32)
    grp_of_blk = jnp.zeros((1, LANES), F32)
    for g in range(MOE_GROUPS):
        end_g = jnp.sum(jnp.where(lane_i == g, end, 0.0), axis=-1, keepdims=True)
        grp_of_blk += jnp.where(lane >= end_g, 1.0, 0.0)
    grp_of_blk = jnp.minimum(grp_of_blk, MOE_GROUPS - 1)
    n_used = jnp.sum(blocks, axis=-1, keepdims=True)
    row = lax.broadcasted_iota(jnp.int32, (8, LANES), 0)
    plan_ref[...] = jnp.where(
        row == PLAN_GROUP, grp_of_blk,
        jnp.where(row == PLAN_USED, n_used,
                  jnp.where(row == PLAN_END, end,
                            jnp.where(row == PLAN_BLOCKS, blocks,
                                      jnp.where(row == PLAN_BASE, start * MOE_BLK, 0.0))))
    ).astype(jnp.int32)


def _plan(cnt):
    return pl.pallas_call(
        _plan_kernel,
        out_shape=jax.ShapeDtypeStruct((8, LANES), jnp.int32),
        name="plan",
    )(cnt)


SEG_START, SEG_LEN = 0, 1


def _segment_copies(seg_ref, base_ref, make_copy):
    local = 0
    for g in range(MOE_GROUPS):
        n = seg_ref[g, SEG_LEN]
        first = base_ref[g] + seg_ref[g, SEG_START]
        k = TM_ROW
        while k >= 1:
            done = n & ~(2 * k - 1)
            @pl.when((n & k) != 0)
            def _():
                make_copy(local + done, first + done, k).start()
            k //= 2
        local = local + n


def _perm_matrix(route_ref):
    row = lax.broadcasted_iota(jnp.int32, (TM_ROW, TM_ROW), 0).astype(F32)
    return jnp.where(row == route_ref[ROUTE_LRANK:ROUTE_LRANK + 1, :], 1.0, 0.0).astype(BF16)


SUB = 8
SUB_X = D_MODEL // 2 // LANES
SUB_GATE = SUB_X
HI_MASK = -65536


def _rows(first_row, n_rows):
    return pl.ds(pl.multiple_of(first_row * SUB, SUB), n_rows * SUB)


def _sublane(s, n_rows):
    return pl.ds(s, n_rows, stride=SUB)


def _scatter_kernel(nu_ref, end_ref, nb_ref, base_ref, seg_ref, route_ref, xn_ref, xs_ref,
                    zbuf, sbuf, sems):
    i = pl.program_id(0)
    n = pl.num_programs(0)
    slot = i % 2
    sem = sems.at[0]

    def zero_block(b):
        return pltpu.make_async_copy(zbuf, xs_ref.at[_rows(b * MOE_BLK, MOE_BLK)], sem)

    def each_unfilled_block(fn):
        for g in range(MOE_GROUPS):
            @pl.when(nb_ref[g] > 0)
            def _():
                fn(zero_block(end_ref[g] - 1))
        for b in range(T_ALL // MOE_BLK, N_BLK):
            @pl.when(b >= nu_ref[0])
            def _():
                fn(zero_block(b))

    @pl.when(i == 0)
    def _():
        zbuf[...] = jnp.zeros_like(zbuf)
        sbuf[...] = jnp.zeros_like(sbuf)
        each_unfilled_block(lambda c: c.start())
        each_unfilled_block(lambda c: c.wait())

    def tile_done(s):
        pltpu.make_async_copy(sbuf.at[s], xs_ref.at[_rows(0, TM_ROW)], sems.at[s]).wait()

    @pl.when(i >= 2)
    def _():
        tile_done(slot)

    perm = _perm_matrix(route_ref)
    xs = lax.bitcast_convert_type(_dot(perm, xn_ref[...].astype(BF16)), jnp.int32)
    half = D_MODEL // 2
    for s in range(SUB_X):
        hi = xs[:, s * LANES:(s + 1) * LANES] & HI_MASK
        lo = lax.shift_right_logical(xs[:, half + s * LANES:half + (s + 1) * LANES], 16)
        sbuf[slot, _sublane(s, TM_ROW), :] = hi | lo
    gate_t = route_ref[0:EXPERTS_PER_GROUP, :]
    gate_t = jnp.concatenate(
        [gate_t, jnp.zeros((LANES - EXPERTS_PER_GROUP, TM_ROW), F32)], axis=0)
    g1 = gate_t.astype(BF16)
    r1 = gate_t - g1.astype(F32)
    g2 = r1.astype(BF16)
    g3 = (r1 - g2.astype(F32)).astype(BF16)
    sbuf[slot, _sublane(SUB_GATE, TM_ROW), :] = lax.bitcast_convert_type(
        _dot_nt(perm, g1) + _dot_nt(perm, g2) + _dot_nt(perm, g3), jnp.int32)

    def make_copy(src_row, dst_row, k):
        return pltpu.make_async_copy(sbuf.at[slot, _rows(src_row, k)],
                                     xs_ref.at[_rows(dst_row, k)], sems.at[slot])
    _segment_copies(seg_ref, base_ref, make_copy)

    @pl.when(i == n - 1)
    def _():
        tile_done(slot)

        @pl.when(n > 1)
        def _():
            tile_done(1 - slot)


def _scatter(seg, route, xn, n_used, grp_end, grp_blocks, grp_base):
    m = xn.shape[0]
    nt = m // TM_ROW
    grid_spec = pltpu.PrefetchScalarGridSpec(
        num_scalar_prefetch=4,
        grid=(nt,),
        in_specs=[
            pl.BlockSpec((None, 8, LANES), lambda i, *_: (i, 0, 0), memory_space=pltpu.SMEM),
            pl.BlockSpec((None, ROUTE_ROWS, TM_ROW), lambda i, *_: (i, 0, 0)),
            pl.BlockSpec((TM_ROW, D_MODEL), lambda i, *_: (i, 0)),
        ],
        out_specs=pl.BlockSpec(memory_space=pl.ANY),
        scratch_shapes=[pltpu.VMEM((MOE_BLK * SUB, LANES), jnp.int32),
                        pltpu.VMEM((2, TM_ROW * SUB, LANES), jnp.int32),
                        pltpu.SemaphoreType.DMA((2,))],
    )
    return pl.pallas_call(
        _scatter_kernel,
        grid_spec=grid_spec,
        out_shape=jax.ShapeDtypeStruct((N_SORTED * SUB, LANES), jnp.int32),
        compiler_params=pltpu.CompilerParams(
            dimension_semantics=("arbitrary",), vmem_limit_bytes=VMEM_LIMIT),
        name="scatter",
    )(n_used, grp_end, grp_blocks, grp_base, seg, route, xn)


def _ffn_kernel(bg_ref, nu_ref, xs_ref, w1_ref, w3_ref, w2_ref, y_ref):
    del bg_ref
    b = pl.program_id(0)

    @pl.when(b < nu_ref[0])
    def _():
        packed = [xs_ref[_sublane(s, MOE_BLK), :] for s in range(SUB_X)]
        x = jnp.concatenate(
            [lax.bitcast_convert_type(u & HI_MASK, F32).astype(BF16) for u in packed]
            + [lax.bitcast_convert_type(lax.shift_left(u, 16), F32).astype(BF16) for u in packed],
            axis=1)
        gates = lax.bitcast_convert_type(xs_ref[_sublane(SUB_GATE, MOE_BLK), :], F32)
        hs = []
        for e in range(EXPERTS_PER_GROUP):
            h1 = _dot(x, w1_ref[e])
            h3 = _dot(x, w3_ref[e])
            ge = gates[:, e:e + 1]
            hs.append(jnp.where(ge > 0.0, h1 * _sigmoid(h1) * h3 * ge, 0.0).astype(BF16))
        hcat = jnp.concatenate(hs, axis=1)
        y = _dot(hcat, w2_ref[...].reshape(EXPERTS_PER_GROUP * D_EXPERT, D_MODEL))
        for s in range(SUB):
            y_ref[_sublane(s, MOE_BLK), :] = y[:, s * LANES:(s + 1) * LANES]

    @pl.when(b >= nu_ref[0])
    def _():
        y_ref[...] = jnp.zeros_like(y_ref)


def _ffn(blk_group, n_used, xs, w1, w3, w2):
    grouped = (MOE_GROUPS, EXPERTS_PER_GROUP)
    w1, w3, w2 = (w.reshape(grouped + w.shape[1:]) for w in (w1, w3, w2))
    wmap = lambda b, bg, nu: (bg[b], 0, 0, 0)
    grid_spec = pltpu.PrefetchScalarGridSpec(
        num_scalar_prefetch=2,
        grid=(N_BLK,),
        in_specs=[
            pl.BlockSpec((MOE_BLK * SUB, LANES), lambda b, bg, nu: (b, 0)),
            pl.BlockSpec((None, EXPERTS_PER_GROUP, D_MODEL, D_EXPERT), wmap),
            pl.BlockSpec((None, EXPERTS_PER_GROUP, D_MODEL, D_EXPERT), wmap),
            pl.BlockSpec((None, EXPERTS_PER_GROUP, D_EXPERT, D_MODEL), wmap),
        ],
        out_specs=pl.BlockSpec((MOE_BLK * SUB, LANES), lambda b, bg, nu: (b, 0)),
    )
    return pl.pallas_call(
        _ffn_kernel,
        grid_spec=grid_spec,
        out_shape=jax.ShapeDtypeStruct((N_SORTED * SUB, LANES), F32),
        compiler_params=pltpu.CompilerParams(
            dimension_semantics=("arbitrary",), vmem_limit_bytes=VMEM_LIMIT),
        name="ffn",
    )(blk_group, n_used, xs, w1, w3, w2)


def _combine_kernel(base_ref, seg_ref, segn_ref, route_ref, ys_hbm, x_ref, g_ref, *refs, final):
    outs, (buf, sem) = refs[:-2], refs[-2:]
    i = pl.program_id(0)
    n = pl.num_programs(0)
    slot = i % 2

    def gather(seg, s):
        def make_copy(buf_row, ys_row, k):
            return pltpu.make_async_copy(ys_hbm.at[_rows(ys_row, k)],
                                         buf.at[s, _rows(buf_row, k)], sem.at[s])
        _segment_copies(seg, base_ref, make_copy)

    @pl.when(i == 0)
    def _():
        gather(seg_ref, 0)

    @pl.when(i + 1 < n)
    def _():
        gather(segn_ref, 1 - slot)

    pltpu.make_async_copy(ys_hbm.at[_rows(0, TM_ROW)], buf.at[slot], sem.at[slot]).wait()

    perm = _perm_matrix(route_ref)
    tn = (((0,), (0,)), ((), ()))
    cols = []
    for s in range(SUB):
        hi, lo = _split_bf16(buf[slot, _sublane(s, TM_ROW), :])
        cols.append(lax.dot_general(perm, hi, tn, preferred_element_type=F32)
                    + lax.dot_general(perm, lo, tn, preferred_element_type=F32))
    y = x_ref[...] + jnp.concatenate(cols, axis=1)
    if not final:
        outs[0][...] = y
    else:
        y = _rms(y, g_ref[...])

        @pl.when(i < T_PROMPT // TM_ROW)
        def _():
            outs[0][...] = y

        @pl.when(i >= T_PROMPT // TM_ROW)
        def _():
            outs[1][...] = y


def _combine(seg, route, grp_base, ys, x, g, final):
    m = x.shape[0]
    nt = m // TM_ROW
    n_p = T_PROMPT // TM_ROW
    smem = functools.partial(pl.BlockSpec, (None, 8, LANES), memory_space=pltpu.SMEM)
    tile = (TM_ROW, D_MODEL)
    if final:
        out_specs = [pl.BlockSpec(tile, lambda i, *_: (jnp.minimum(i, n_p - 1), 0)),
                     pl.BlockSpec(tile, lambda i, *_: (jnp.maximum(i - n_p, 0), 0))]
        out_shape = [jax.ShapeDtypeStruct((T_PROMPT, D_MODEL), F32),
                     jax.ShapeDtypeStruct((T_SAMPLE, D_MODEL), F32)]
    else:
        out_specs = [pl.BlockSpec(tile, lambda i, *_: (i, 0))]
        out_shape = [jax.ShapeDtypeStruct((m, D_MODEL), F32)]
    grid_spec = pltpu.PrefetchScalarGridSpec(
        num_scalar_prefetch=1,
        grid=(nt,),
        in_specs=[
            smem(lambda i, *_: (i, 0, 0)),
            smem(lambda i, *_: (jnp.minimum(i + 1, nt - 1), 0, 0)),
            pl.BlockSpec((None, ROUTE_ROWS, TM_ROW), lambda i, *_: (i, 0, 0)),
            pl.BlockSpec(memory_space=pl.ANY),
            pl.BlockSpec(tile, lambda i, *_: (i, 0)),
            pl.BlockSpec(g.shape, lambda i, *_: (0, 0)),
        ],
        out_specs=out_specs,
        scratch_shapes=[pltpu.VMEM((2, TM_ROW * SUB, LANES), F32),
                        pltpu.SemaphoreType.DMA((2,))],
    )
    return pl.pallas_call(
        functools.partial(_combine_kernel, final=final),
        grid_spec=grid_spec,
        out_shape=out_shape,
        compiler_params=pltpu.CompilerParams(
            dimension_semantics=("arbitrary",), vmem_limit_bytes=VMEM_LIMIT),
        name="combine",
    )(grp_base, seg, seg, route, ys, x, g)


def _prep_weights(w_in, gla_a2, cm_ws, cm_b, router_group_w, router_group_b,
                  router_expert_w, router_expert_b):
    off = {}
    o = 0
    for name, n in (("h", 512), ("cg", 512), ("bg", 512), ("q", 512), ("k", 512), ("v", 1024),
                    ("r", 1024), ("alr", 16), ("u", 512), ("vv", 512), ("ga", 1024),
                    ("gb", 1024), ("gc", 1024)):
        off[name] = (o, n)
        o += n
    order = ("ga", "gb", "gc", "v", "r", "q", "k", "h", "cg", "bg", "u", "vv")
    w_t = jnp.swapaxes(w_in, 1, 2)
    w_z = jnp.concatenate([w_t[:, off[n][0]:off[n][0] + off[n][1]] for n in order],
                          axis=1).astype(BF16)
    a0 = off["alr"][0]
    w_alr = jnp.pad(w_t[:, a0:a0 + GLA_LOWRANK],
                    ((0, 0), (0, LANES - GLA_LOWRANK), (0, 0))).astype(BF16)
    a2 = jnp.pad(gla_a2, ((0, 0), (0, LANES - GLA_LOWRANK), (0, 0))).astype(BF16)
    ws_p = jnp.tril(cm_ws).astype(BF16)
    small = jnp.tril(cm_ws[:, :, :DEC_SEQ, :DEC_SEQ])
    eye = jnp.eye(SEQ_PER_BLK, dtype=F32)
    ws_s = jnp.einsum("ij,lgab->lgiajb", eye, small).reshape(
        DEPTH, CM_GROUPS, ROWS_S, ROWS_S).astype(BF16)
    cmb_p = jnp.broadcast_to(jnp.transpose(cm_b, (0, 2, 1))[:, :, :, None],
                             (DEPTH, CM_CHUNK, CM_GROUPS, CM_GCH)).reshape(DEPTH, CM_CHUNK, CM_CH)
    cmb_s = jnp.tile(cmb_p[:, :DEC_SEQ], (1, SEQ_PER_BLK, 1))
    pad = LANES - MOE_GROUPS - N_EXPERTS
    w_r = jnp.pad(jnp.swapaxes(jnp.concatenate([router_group_w, router_expert_w], axis=-1), 1, 2),
                  ((0, 0), (0, pad), (0, 0)))
    w_r_hi = w_r.astype(BF16)
    w_r_lo = (w_r - w_r_hi.astype(F32)).astype(BF16)
    r_b = jnp.pad(jnp.concatenate([router_group_b, router_expert_b], axis=-1),
                  ((0, 0), (0, pad)))[:, :, None]
    return w_z, w_alr, a2, ws_p, ws_s, cmb_p, cmb_s, w_r_hi, w_r_lo, r_b


def kernel(x_prompt, x_sample, state_conv, state_gla, norm1_g, w_in, conv_w, gla_a2, gla_a_b,
           gla_norm_g, cm_norm_g, cm_ws, cm_b, proj_a, proj_b, proj_c, w_out, norm2_g,
           router_group_w, router_group_b, router_expert_w, router_expert_b,
           exp_w1, exp_w3, exp_w2, final_norm_g):
    (w_z, w_alr, a2, ws_p, ws_s, cmb_p, cmb_s, w_r_hi, w_r_lo, r_b) = _prep_weights(
        w_in, gla_a2, cm_ws, cm_b, router_group_w, router_group_b, router_expert_w,
        router_expert_b)
    pa, pb, pc, wo = (w.astype(BF16) for w in (proj_a, proj_b, proj_c, w_out))
    n1 = norm1_g[:, None, :]
    n2 = norm2_g[:, None, :]
    ab = gla_a_b[:, None, :]
    gng = gla_norm_g[:, None, :]
    cmg = cm_norm_g.reshape(DEPTH, 1, CM_CH)
    fg = final_norm_g[None, :]
    xa = x_prompt.reshape(T_PROMPT, D_MODEL)
    xb = x_sample.reshape(T_SAMPLE, D_MODEL)
    gla_s = None
    conv_p, gla_p, conv_s, cmv_s = [], [], [], []
    for l in range(DEPTH):
        z, alr, w1, w3, w2 = _inproj(xa, xb, n1, w_z, w_alr, exp_w1, exp_w3, exp_w2, l)
        acts_p, nconv, ngla = _mix_prompt(z, alr, a2, ab, conv_w, gng, cmg, ws_p, cmb_p, l)
        sc = state_conv[l]
        p2 = jnp.pad(sc, ((0, 0), (0, DEC_SEQ - 2), (0, 0))).reshape(T_SAMPLE, CONV_CH)
        p1 = jnp.pad(sc[:, 1:2], ((0, 0), (0, DEC_SEQ - 1), (0, 0))).reshape(T_SAMPLE, CONV_CH)
        acts_s, cin_s, gla_s, vrows = _mix_sample(z, alr, a2, ab, conv_w, gng, cmg, ws_s, cmb_s,
                                                  p1, p2, state_gla, gla_s, l)
        conv_p.append(nconv)
        gla_p.append(ngla)
        conv_s.append(cin_s.reshape(DEC_BATCH, DEC_SEQ, CONV_CH)[:, DEC_SEQ - (CONV_K - 1):])
        cmv_s.append(vrows.reshape(DEC_BATCH, DEC_SEQ, CM_CH))

        x, xn, route, seg, cnt = _outproj(acts_p, acts_s, z, xa, xb, pa, pb, pc, wo, n2,
                                          w_r_hi, w_r_lo, r_b, l)
        plan = _plan(cnt)
        n_used = plan[PLAN_USED, :1]
        grp_base = plan[PLAN_BASE, :MOE_GROUPS]
        xs = _scatter(seg, route, xn, n_used, plan[PLAN_END, :MOE_GROUPS],
                      plan[PLAN_BLOCKS, :MOE_GROUPS], grp_base)
        ys = _ffn(plan[PLAN_GROUP, :N_BLK], n_used, xs, w1, w3, w2)
        out = _combine(seg, route, grp_base, ys, x, fg, l == DEPTH - 1)
        xa = xb = out[0]

    y_prompt = out[0].reshape(BATCH, SEQ, D_MODEL)
    y_sample = out[1].reshape(DEC_BATCH, DEC_SEQ, D_MODEL)
    return (y_prompt, y_sample, jnp.stack(conv_p), jnp.stack(gla_p), jnp.stack(conv_s),
            gla_s, jnp.stack(cmv_s))
```

```python
import functools

import jax
import jax.numpy as jnp
from jax import lax
from jax.experimental import pallas as pl
from jax.experimental.pallas import tpu as pltpu

F32 = jnp.float32
BF16 = jnp.bfloat16

D_MODEL = 1024
BATCH = 8
SEQ = 2048
DEPTH = 2
DEC_BATCH = 128
DEC_SEQ = 8
CONV_K = 3
CONV_CH = 512
GLA_HEADS = 4
GLA_DK = 128
GLA_DV = 256
GLA_QK = GLA_HEADS * GLA_DK
GLA_V = GLA_HEADS * GLA_DV
GLA_LOWRANK = 16
GLA_TAU = 16.0
GLA_CHUNK = 64
CM_GROUPS = 4
CM_CHUNK = 128
CM_GCH = 128
CM_CH = 512
MOE_GROUPS = 8
EXPERTS_PER_GROUP = 8
N_EXPERTS = 64
D_EXPERT = 256
EPS = 1e-6

LANES = 128
T_PROMPT = BATCH * SEQ
T_SAMPLE = DEC_BATCH * DEC_SEQ
T_ALL = T_PROMPT + T_SAMPLE

COL_GA, COL_GB, COL_GC = 0, 1024, 2048
COL_V, COL_R, COL_Q, COL_K = 3072, 4096, 5120, 5632
COL_H, COL_CG, COL_BG, COL_U, COL_VV = 6144, 6656, 7168, 7680, 8192
Z_COLS = 8704
ACT_COLS = 2048
ROUTE_ROWS = 2 * EXPERTS_PER_GROUP
ROUTE_LRANK = EXPERTS_PER_GROUP

TM_IN = 1024
TN_IN = Z_COLS // 4
TC_MIX = 256
SEQ_PER_BLK = 16
ROWS_S = SEQ_PER_BLK * DEC_SEQ
TM_OUT = 256
OUT_PARTS = 1
OUT_STAGGER = 1
TM_ROW = TM_OUT
ROW_TILES = 4
MOE_BLK = 256
N_BLK = T_ALL // MOE_BLK + MOE_GROUPS
N_SORTED = N_BLK * MOE_BLK
VMEM_LIMIT = 56 * 1024 * 1024


def _sigmoid(x):
    return 0.5 * jnp.tanh(0.5 * x) + 0.5


def _gelu_tanh(x):
    c = 0.7978845608028654
    half = 0.5 * x
    return half + half * jnp.tanh(x * (c + (c * 0.044715) * (x * x)))


def _log_sigmoid(x):
    log2_e = 1.4426950408889634
    ln_2 = 0.6931471805599453
    return jnp.minimum(x, 0.0) - ln_2 * jnp.log2(1.0 + jnp.exp2(-log2_e * jnp.abs(x)))


def _rms(x, g):
    ms = jnp.mean(x * x, axis=-1, keepdims=True)
    return x * lax.rsqrt(ms + EPS) * g


def _split_bf16(x):
    hi = x.astype(BF16)
    lo = (x - hi.astype(F32)).astype(BF16)
    return hi, lo


def _dot(a, b):
    return jnp.dot(a, b, preferred_element_type=F32)


def _dot_nt(a, b):
    return lax.dot_general(a, b, (((1,), (1,)), ((), ())), preferred_element_type=F32)


def _layer_spec(arr, layer):
    nd = arr.ndim - 1
    return pl.BlockSpec((None,) + arr.shape[1:], lambda *g: (layer,) + (0,) * nd)


def _const_spec(arr):
    nd = arr.ndim
    return pl.BlockSpec(arr.shape, lambda *g: (0,) * nd)


def _two_part_specs(tile, xa, xb):
    n_a = xa.shape[0] // tile
    n_b = xb.shape[0] // tile
    return n_a, [
        pl.BlockSpec((tile, xa.shape[1]), lambda i, *_: (jnp.minimum(i, n_a - 1), 0)),
        pl.BlockSpec((tile, xb.shape[1]), lambda i, *_: (jnp.clip(i - n_a, 0, n_b - 1), 0)),
    ]


def _inproj_kernel(xa_ref, xb_ref, g_ref, w_ref, wa_ref, e1_ref, e3_ref, e2_ref,
                   z_ref, a_ref, o1_ref, o3_ref, o2_ref, xn_ref, *, n_a):
    o1_ref[...] = e1_ref[...].astype(BF16)
    o3_ref[...] = e3_ref[...].astype(BF16)
    o2_ref[...] = e2_ref[...].astype(BF16)

    @pl.when(pl.program_id(1) == 0)
    def _():
        x = jnp.where(pl.program_id(0) < n_a, xa_ref[...], xb_ref[...])
        xn = _rms(x, g_ref[...]).astype(BF16)
        xn_ref[...] = xn
        a_ref[...] = _dot_nt(xn, wa_ref[...]).astype(BF16)

    z_ref[...] = _dot_nt(xn_ref[...], w_ref[...]).astype(BF16)


def _inproj(xa, xb, g, w, wa, e1, e3, e2, layer):
    m = T_ALL
    n_i, n_j = m // TM_IN, Z_COLS // TN_IN
    assert n_i * n_j >= N_EXPERTS
    n_a, x_specs = _two_part_specs(TM_IN, xa, xb)
    expert = lambda i, j: jnp.minimum(i * n_j + j, N_EXPERTS - 1)
    up, down = (D_MODEL, D_EXPERT), (D_EXPERT, D_MODEL)
    return pl.pallas_call(
        functools.partial(_inproj_kernel, n_a=n_a),
        grid=(n_i, n_j),
        in_specs=x_specs + [
            _layer_spec(g, layer),
            pl.BlockSpec((None, TN_IN, D_MODEL), lambda i, j: (layer, j, 0)),
            _layer_spec(wa, layer),
            pl.BlockSpec((None, None) + up, lambda i, j: (layer, expert(i, j), 0, 0)),
            pl.BlockSpec((None, None) + up, lambda i, j: (layer, expert(i, j), 0, 0)),
            pl.BlockSpec((None, None) + down, lambda i, j: (layer, expert(i, j), 0, 0)),
        ],
        out_specs=[
            pl.BlockSpec((TM_IN, TN_IN), lambda i, j: (i, j)),
            pl.BlockSpec((TM_IN, LANES), lambda i, j: (i, 0)),
            pl.BlockSpec((None,) + up, lambda i, j: (expert(i, j), 0, 0)),
            pl.BlockSpec((None,) + up, lambda i, j: (expert(i, j), 0, 0)),
            pl.BlockSpec((None,) + down, lambda i, j: (expert(i, j), 0, 0)),
        ],
        out_shape=[
            jax.ShapeDtypeStruct((m, Z_COLS), BF16),
            jax.ShapeDtypeStruct((m, LANES), BF16),
            jax.ShapeDtypeStruct((N_EXPERTS,) + up, BF16),
            jax.ShapeDtypeStruct((N_EXPERTS,) + up, BF16),
            jax.ShapeDtypeStruct((N_EXPERTS,) + down, BF16),
        ],
        scratch_shapes=[pltpu.VMEM((TM_IN, D_MODEL), BF16)],
        compiler_params=pltpu.CompilerParams(
            dimension_semantics=("arbitrary", "arbitrary"), vmem_limit_bytes=VMEM_LIMIT),
        name="inproj",
    )(xa, xb, g, w, wa, e1, e3, e2)


def _gla_log_decay(alr_ref, a2_ref, ab_ref):
    la = _log_sigmoid(_dot(alr_ref[...], a2_ref[...]) + ab_ref[...]) * (1.0 / GLA_TAU)
    return _split_bf16(la)


def _masked_sum(mask, la_hi, la_lo):
    m = jnp.where(mask, 1.0, 0.0).astype(BF16)
    return _dot(m, la_hi) + _dot(m, la_lo)


def _gla_decay_prefix(alr_ref, a2_ref, ab_ref, tril_mask):
    la_hi, la_lo = _gla_log_decay(alr_ref, a2_ref, ab_ref)
    return _masked_sum(tril_mask, la_hi, la_lo)


def _gla_decay_terms(alr_ref, a2_ref, ab_ref, tril_mask, same_mask):
    la_hi, la_lo = _gla_log_decay(alr_ref, a2_ref, ab_ref)
    return (la_hi, la_lo, _masked_sum(tril_mask, la_hi, la_lo),
            _masked_sum(same_mask, la_hi, la_lo))


def _gla_out_gate(o, g_ref, r):
    return _rms(o, g_ref[...]) * (r * _sigmoid(r))


def _chunk_mlp_group(g, bgu_ref, vv_ref, cmg_ref, ws_ref, cmb_ref, n_chunks):
    sl = slice(g * CM_GCH, (g + 1) * CM_GCH)
    ug = _gelu_tanh(bgu_ref[:, CONV_CH + g * CM_GCH:CONV_CH + (g + 1) * CM_GCH])
    vg = _rms(_gelu_tanh(vv_ref[:, sl]).astype(F32), cmg_ref[:, sl])
    vgb = vg.astype(BF16)
    rows = []
    for j in range(n_chunks):
        rs = slice(j * CM_CHUNK, (j + 1) * CM_CHUNK)
        rows.append(_dot(ws_ref[g], vgb[rs]) + cmb_ref[:, sl])
    s = rows[0] if n_chunks == 1 else jnp.concatenate(rows, axis=0)
    return ug * s, vg


def _chunk_mlp(bgu_ref, vv_ref, cmg_ref, ws_ref, cmb_ref, n_chunks):
    parts = [_chunk_mlp_group(g, bgu_ref, vv_ref, cmg_ref, ws_ref, cmb_ref, n_chunks)
             for g in range(CM_GROUPS)]
    return (jnp.concatenate([p[0] for p in parts], axis=1),
            jnp.concatenate([p[1] for p in parts], axis=1))


N_Z_VIEWS = 7
N_MIX_W = 7
SEQ_PER_STEP = 2
SEQ_STAGGER = 4


def _mix_prompt_kernel(*refs):
    n_z = N_Z_VIEWS * SEQ_PER_STEP
    weights = refs[n_z:n_z + N_MIX_W]
    acts_ref, nconv_ref, ngla_ref, st_ref, carry_ref = refs[n_z + N_MIX_W:]

    @pl.when(pl.program_id(1) == 0)
    def _():
        st_ref[...] = jnp.zeros_like(st_ref)
        carry_ref[...] = jnp.zeros_like(carry_ref)

    _interleave(*[
        _delayed(_mix_prompt_seq(*refs[N_Z_VIEWS * s:N_Z_VIEWS * (s + 1)], *weights,
                                acts_ref.at[s], nconv_ref.at[s], st_ref.at[s], carry_ref.at[s]),
                s * SEQ_STAGGER)
        for s in range(SEQ_PER_STEP)])

    @pl.when(pl.program_id(1) == pl.num_programs(1) - 1)
    def _():
        for s in range(SEQ_PER_STEP):
            for hd in range(GLA_HEADS):
                ngla_ref[s, 0, hd] = st_ref[s, hd].T


def _mix_prompt_seq(v_ref, r_ref, qk_ref, hcg_ref, bgu_ref, vv_ref, alr_ref,
                    a2_ref, ab_ref, cw_ref, gng_ref, cmg_ref, ws_ref, cmb_ref,
                    acts_ref, nconv_ref, st_ref, carry_ref):
    tc = TC_MIX

    h = hcg_ref[:, :CONV_CH].astype(F32)
    cg = hcg_ref[:, CONV_CH:].astype(F32)
    bg = bgu_ref[:, :CONV_CH].astype(F32)
    cin = cg * h
    rr = lax.broadcasted_iota(jnp.int32, (tc, tc), 0)
    cc = lax.broadcasted_iota(jnp.int32, (tc, tc), 1)
    cin_b = cin.astype(BF16)
    x1 = _dot(jnp.where(rr - cc == 1, 1.0, 0.0).astype(BF16), cin_b)
    x2 = _dot(jnp.where(rr - cc == 2, 1.0, 0.0).astype(BF16), cin_b)
    conv = x2 * cw_ref[0:1, :] + x1 * cw_ref[1:2, :] + cin * cw_ref[2:3, :]
    c0 = carry_ref[0:1, :]
    c1 = carry_ref[1:2, :]
    row8 = lax.broadcasted_iota(jnp.int32, (8, 1), 0)
    head = jnp.where(row8 == 0, c0 * cw_ref[0:1, :] + c1 * cw_ref[1:2, :],
                     jnp.where(row8 == 1, c1 * cw_ref[0:1, :], 0.0))
    conv = jnp.concatenate([conv[0:8] + head, conv[8:]], axis=0)
    acts_ref[:, 0:CONV_CH] = (bg * conv).astype(BF16)
    carry_ref[0:2, :] = cin[tc - 2:tc, :]
    nconv_ref[0] = cin[tc - 2:tc, :]
    yield

    rr = lax.broadcasted_iota(jnp.int32, (tc, tc), 0)
    cc = lax.broadcasted_iota(jnp.int32, (tc, tc), 1)
    same = (rr >> 6) == (cc >> 6)
    tril = same & (cc <= rr)
    b = _gla_decay_prefix(alr_ref, a2_ref, ab_ref, tril)
    yield
    n_chunks = tc // GLA_CHUNK
    b_last = [b[(c + 1) * GLA_CHUNK - 1:(c + 1) * GLA_CHUNK, :] for c in range(n_chunks)]
    bl = jnp.concatenate([jnp.broadcast_to(r_, (GLA_CHUNK, GLA_QK)) for r_ in b_last], axis=0)
    q = qk_ref[:, :GLA_QK].astype(F32) * (GLA_DK ** -0.5)
    k = qk_ref[:, GLA_QK:].astype(F32)
    q_t = (q * jnp.exp(b)).astype(BF16)
    k_t = (k * jnp.exp(-b)).astype(BF16)
    k_end = (k * jnp.exp(bl - b)).astype(BF16)
    yield
    states = [st_ref[hd] for hd in range(GLA_HEADS)]
    k_cols = [slice(hd * GLA_DK, (hd + 1) * GLA_DK) for hd in range(GLA_HEADS)]
    v_cols = [slice(hd * GLA_DV, (hd + 1) * GLA_DV) for hd in range(GLA_HEADS)]
    o_intra = []
    for hd in range(GLA_HEADS):
        att = jnp.where(tril, _dot_nt(q_t[:, k_cols[hd]], k_t[:, k_cols[hd]]), 0.0).astype(BF16)
        o_intra.append(_dot(att, v_ref[:, v_cols[hd]]))
    yield

    o_rows = [[] for _ in range(GLA_HEADS)]
    assert n_chunks == CM_GROUPS
    for c in range(n_chunks):
        rs = slice(c * GLA_CHUNK, (c + 1) * GLA_CHUNK)
        for hd in range(GLA_HEADS):
            ks = k_cols[hd]
            st = states[hd]
            o_rows[hd].append(o_intra[hd][rs] + _dot_nt(q_t[rs, ks], st.astype(BF16)))
            upd = lax.dot_general(v_ref[rs, v_cols[hd]], k_end[rs, ks], (((0,), (0,)), ((), ())),
                                  preferred_element_type=F32)
            states[hd] = jnp.exp(b_last[c][:, ks]) * st + upd
        us, _ = _chunk_mlp_group(c, bgu_ref, vv_ref, cmg_ref, ws_ref, cmb_ref, tc // CM_CHUNK)
        acts_ref[:, CONV_CH + GLA_V + c * CM_GCH:CONV_CH + GLA_V + (c + 1) * CM_GCH] = (
            us.astype(BF16))
        yield

    for hd in range(GLA_HEADS):
        o = jnp.concatenate(o_rows[hd], axis=0)
        r = r_ref[:, v_cols[hd]]
        acts_ref[:, CONV_CH + hd * GLA_DV:CONV_CH + (hd + 1) * GLA_DV] = (
            _gla_out_gate(o, gng_ref, r).astype(BF16))
        st_ref[hd] = states[hd]
        if hd % 2 == 1:
            yield


def _z_specs(rows, row_map):
    def spec(width, col):
        blk = col // width
        return pl.BlockSpec((rows, width), lambda *g: (row_map(*g), blk))
    return [spec(1024, COL_V), spec(1024, COL_R), spec(1024, COL_Q), spec(1024, COL_H),
            spec(1024, COL_BG), spec(512, COL_VV)]


def _mix_prompt(z, alr, a2, ab, cw, gng, cmg, ws, cmb, layer):
    nt = SEQ // TC_MIX
    nb = BATCH // SEQ_PER_STEP
    small = (a2, ab, cw, gng, cmg, ws, cmb)
    assert len(small) == N_MIX_W
    in_specs, args = [], []
    for s in range(SEQ_PER_STEP):
        row_map = lambda b, c, s=s: (b + s * nb) * nt + c
        in_specs += _z_specs(TC_MIX, row_map) + [
            pl.BlockSpec((TC_MIX, LANES), lambda b, c, row_map=row_map: (row_map(b, c), 0))]
        args += [z] * (N_Z_VIEWS - 1) + [alr]
    in_specs += [_layer_spec(a, layer) for a in small]
    acts, nconv, ngla = pl.pallas_call(
        _mix_prompt_kernel,
        grid=(nb, nt),
        in_specs=in_specs,
        out_specs=[
            pl.BlockSpec((SEQ_PER_STEP, TC_MIX, ACT_COLS), lambda b, c: (0, b * nt + c, 0)),
            pl.BlockSpec((SEQ_PER_STEP, 1, CONV_K - 1, CONV_CH), lambda b, c: (0, b, 0, 0)),
            pl.BlockSpec((SEQ_PER_STEP, 1, GLA_HEADS, GLA_DK, GLA_DV),
                         lambda b, c: (0, b, 0, 0, 0)),
        ],
        out_shape=[
            jax.ShapeDtypeStruct((SEQ_PER_STEP, T_PROMPT // SEQ_PER_STEP, ACT_COLS), BF16),
            jax.ShapeDtypeStruct((SEQ_PER_STEP, nb, CONV_K - 1, CONV_CH), F32),
            jax.ShapeDtypeStruct((SEQ_PER_STEP, nb, GLA_HEADS, GLA_DK, GLA_DV), F32),
        ],
        scratch_shapes=[pltpu.VMEM((SEQ_PER_STEP, GLA_HEADS, GLA_DV, GLA_DK), F32),
                        pltpu.VMEM((SEQ_PER_STEP, 8, CONV_CH), F32)],
        compiler_params=pltpu.CompilerParams(
            dimension_semantics=("arbitrary", "arbitrary"), vmem_limit_bytes=VMEM_LIMIT),
        name="mix_prompt",
    )(*args, *small)
    return (acts.reshape(T_PROMPT, ACT_COLS), nconv.reshape(BATCH, CONV_K - 1, CONV_CH),
            ngla.reshape(BATCH, GLA_HEADS, GLA_DK, GLA_DV))


def _mix_sample_body(v_ref, r_ref, qk_ref, hcg_ref, bgu_ref, vv_ref, alr_ref,
                       a2_ref, ab_ref, cw_ref, gng_ref, cmg_ref, ws_ref, cmb_ref,
                       p1_ref, p2_ref, s0_ref,
                       acts_ref, cin_ref, ns_ref, vrow_ref):
    n = ROWS_S

    pos = lax.broadcasted_iota(jnp.int32, (n, 1), 0) & (DEC_SEQ - 1)
    h = hcg_ref[:, :CONV_CH].astype(F32)
    cg = hcg_ref[:, CONV_CH:].astype(F32)
    bg = bgu_ref[:, :CONV_CH].astype(F32)
    cin = cg * h
    x1 = jnp.where(pos >= 1, pltpu.roll(cin, 1, 0), p1_ref[...])
    x2 = jnp.where(pos >= 2, pltpu.roll(cin, 2, 0), p2_ref[...])
    conv = x2 * cw_ref[0:1, :] + x1 * cw_ref[1:2, :] + cin * cw_ref[2:3, :]
    acts_ref[:, 0:CONV_CH] = (bg * conv).astype(BF16)
    cin_ref[...] = cin

    rr = lax.broadcasted_iota(jnp.int32, (n, n), 0)
    cc = lax.broadcasted_iota(jnp.int32, (n, n), 1)
    same = (rr >> 3) == (cc >> 3)
    tril = same & (cc <= rr)
    la_hi, la_lo, b, bl = _gla_decay_terms(alr_ref, a2_ref, ab_ref, tril, same)
    q = qk_ref[:, :GLA_QK].astype(F32) * (GLA_DK ** -0.5)
    k = qk_ref[:, GLA_QK:].astype(F32)
    q_t = (q * jnp.exp(b)).astype(BF16)
    k_t = (k * jnp.exp(-b)).astype(BF16)
    k_end = k * jnp.exp(bl - b)
    la_hi = la_hi.astype(F32)
    la_lo = la_lo.astype(F32)
    row_seq = lax.broadcasted_iota(jnp.int32, (n, GLA_DK), 0) >> 3
    seq3 = lax.broadcasted_iota(jnp.int32, (SEQ_PER_BLK, GLA_DK, n), 0)
    lane_seq3 = lax.broadcasted_iota(jnp.int32, (SEQ_PER_BLK, GLA_DK, n), 2) >> 3
    mask3 = seq3 == lane_seq3
    ones = jnp.ones((n, GLA_DV), BF16)
    big = SEQ_PER_BLK * GLA_DK

    def per_seq(x_tr):
        x3 = jnp.where(mask3, x_tr[None, :, :], 0.0)
        return x3.reshape(big, n).astype(BF16)

    for hd in range(GLA_HEADS):
        ks = slice(hd * GLA_DK, (hd + 1) * GLA_DK)
        vs = slice(hd * GLA_DV, (hd + 1) * GLA_DV)
        qh = q_t[:, ks]
        vh = v_ref[:, vs]
        att = jnp.where(tril, _dot_nt(qh, k_t[:, ks]), 0.0).astype(BF16)
        o_intra = _dot(att, vh)
        s_old = s0_ref[:, hd].reshape(big, GLA_DV)
        zero = jnp.zeros_like(qh)
        q_big = jnp.concatenate(
            [jnp.where(row_seq == j, qh, zero) for j in range(SEQ_PER_BLK)], axis=1)
        o = o_intra + _dot(q_big, s_old.astype(BF16))
        dlog = _dot(per_seq(la_hi[:, ks].T), ones) + _dot(per_seq(la_lo[:, ks].T), ones)
        upd = _dot(per_seq(k_end[:, ks].T), vh)
        s_new = jnp.exp(dlog) * s_old + upd
        ns_ref[:, hd] = s_new.reshape(SEQ_PER_BLK, GLA_DK, GLA_DV)
        r = r_ref[:, vs]
        acts_ref[:, CONV_CH + hd * GLA_DV:CONV_CH + (hd + 1) * GLA_DV] = (
            _gla_out_gate(o, gng_ref, r).astype(BF16))

    us, vg = _chunk_mlp(bgu_ref, vv_ref, cmg_ref, ws_ref, cmb_ref, 1)
    acts_ref[:, CONV_CH + GLA_V:] = us.astype(BF16)
    vrow_ref[...] = vg


N_MIX_S_IN = 17


def _mix_sample_kernel(*refs, layer):
    if layer == 0:
        @pl.when(pl.program_id(0) == 0)
        def _():
            _mix_sample_body(*refs)

        @pl.when(pl.program_id(0) > 0)
        def _():
            ns_ref = refs[N_MIX_S_IN + 2]
            ns_ref[...] = jnp.zeros_like(ns_ref)
    else:
        _mix_sample_body(*refs[:N_MIX_S_IN], *refs[N_MIX_S_IN + 1:])


def _mix_sample(z, alr, a2, ab, cw, gng, cmg, ws, cmb, p1, p2, s0, ns_all, layer):
    row0 = T_PROMPT // ROWS_S
    n_i = DEC_BATCH // SEQ_PER_BLK
    n_pass = DEPTH if layer == 0 else 1
    blk = lambda p, i: jnp.where(p == 0, i, n_i - 1)
    row_map = lambda p, i: row0 + blk(p, i)
    slot = lambda p, i: (layer + p, i, 0, 0, 0)
    small = (a2, ab, cw, gng, cmg, ws, cmb)
    state_blk = (None, SEQ_PER_BLK, GLA_HEADS, GLA_DK, GLA_DV)
    in_specs = _z_specs(ROWS_S, row_map) + [
        pl.BlockSpec((ROWS_S, LANES), lambda p, i: (row_map(p, i), 0)),
    ] + [_layer_spec(a, layer) for a in small] + [
        pl.BlockSpec((ROWS_S, CONV_CH), lambda p, i: (blk(p, i), 0)),
        pl.BlockSpec((ROWS_S, CONV_CH), lambda p, i: (blk(p, i), 0)),
        pl.BlockSpec(state_blk, lambda p, i: (layer, blk(p, i), 0, 0, 0)),
    ]
    args = (z, z, z, z, z, z, alr, *small, p1, p2, s0)
    assert len(args) == N_MIX_S_IN
    aliases = {}
    if layer > 0:
        in_specs.append(pl.BlockSpec(memory_space=pl.ANY))
        args += (ns_all,)
        aliases = {N_MIX_S_IN: 2}
    return pl.pallas_call(
        functools.partial(_mix_sample_kernel, layer=layer),
        grid=(n_pass, n_i),
        in_specs=in_specs,
        out_specs=[
            pl.BlockSpec((ROWS_S, ACT_COLS), lambda p, i: (blk(p, i), 0)),
            pl.BlockSpec((ROWS_S, CONV_CH), lambda p, i: (blk(p, i), 0)),
            pl.BlockSpec(state_blk, slot),
            pl.BlockSpec((ROWS_S, CM_CH), lambda p, i: (blk(p, i), 0)),
        ],
        out_shape=[
            jax.ShapeDtypeStruct((T_SAMPLE, ACT_COLS), BF16),
            jax.ShapeDtypeStruct((T_SAMPLE, CONV_CH), F32),
            jax.ShapeDtypeStruct((DEPTH, DEC_BATCH, GLA_HEADS, GLA_DK, GLA_DV), F32),
            jax.ShapeDtypeStruct((T_SAMPLE, CM_CH), F32),
        ],
        input_output_aliases=aliases,
        compiler_params=pltpu.CompilerParams(
            dimension_semantics=("arbitrary", "arbitrary"), vmem_limit_bytes=VMEM_LIMIT),
        name="mix_sample",
    )(*args)


def _delayed(gen, n_stages):
    for _ in range(n_stages):
        yield
    yield from gen


def _interleave(*stage_lists):
    pending = list(stage_lists)
    while pending:
        for gen in list(pending):
            if next(gen, StopIteration) is StopIteration:
                pending.remove(gen)


def _outproj_main(i, rows, n_xa, actp_ref, acts_ref, gates_ref, xa_ref, xb_ref, pa_ref, pb_ref,
                  pc_ref, wo_ref, n2_ref, xo_ref, xn_ref, split_ref):
    acts = jnp.where(i < T_PROMPT // TM_OUT, actp_ref[rows, :], acts_ref[rows, :])
    x_in = jnp.where(i < n_xa, xa_ref[rows, :], xb_ref[rows, :])
    ya = _dot(acts[:, :CONV_CH], pa_ref[...])
    yb = _dot(acts[:, CONV_CH:CONV_CH + GLA_V], pb_ref[...])
    yc = _dot(acts[:, CONV_CH + GLA_V:], pc_ref[...])
    yield
    ga = _sigmoid(gates_ref[rows, COL_GA:COL_GA + D_MODEL])
    gb = _sigmoid(gates_ref[rows, COL_GB:COL_GB + D_MODEL])
    gc = _sigmoid(gates_ref[rows, COL_GC:COL_GC + D_MODEL])
    mix = ga * ya + gb * yb + gc * yc
    yield
    x = x_in + _dot(mix.astype(BF16), wo_ref[...])
    xo_ref[rows, :] = x
    yield
    xn = _rms(x, n2_ref[...])
    xn_ref[rows, :] = xn
    hi, lo = _split_bf16(xn)
    yield
    split_ref[0, rows, :] = hi
    split_ref[1, rows, :] = lo


def _outproj_route(live, split_ref, wrh_ref, wrl_ref, rb_ref, route_ref, seg_ref, cnt_ref):
    tm = TM_OUT
    hi = split_ref[0]
    lo = split_ref[1]
    logits = (_dot_nt(wrh_ref[...], hi) + _dot_nt(wrh_ref[...], lo) + _dot_nt(wrl_ref[...], hi)
              + rb_ref[...])
    yield
    n_sub = EXPERTS_PER_GROUP
    sub = lax.broadcasted_iota(jnp.int32, (n_sub, tm), 0).astype(F32)
    neg = jnp.float32(-jnp.inf)
    lg = logits[0:MOE_GROUPS]
    gmax = jnp.max(lg, axis=0, keepdims=True)
    grp = jnp.min(jnp.where(lg == gmax, sub, 1e9), axis=0, keepdims=True)
    p_grp = 1.0 / jnp.sum(jnp.exp(lg - gmax), axis=0, keepdims=True)
    yield
    le = logits[MOE_GROUPS:MOE_GROUPS + n_sub]
    for g in range(1, MOE_GROUPS):
        le = jnp.where(grp == g, logits[MOE_GROUPS + g * n_sub:MOE_GROUPS + (g + 1) * n_sub], le)
    v1 = jnp.max(le, axis=0, keepdims=True)
    i1 = jnp.min(jnp.where(le == v1, sub, 1e9), axis=0, keepdims=True)
    le2 = jnp.where(sub == i1, neg, le)
    v2 = jnp.max(le2, axis=0, keepdims=True)
    i2 = jnp.min(jnp.where(le2 == v2, sub, 1e9), axis=0, keepdims=True)
    yield
    t = jnp.exp(v2 - v1)
    g1 = p_grp / (1.0 + t)
    g2 = p_grp * t / (1.0 + t)
    gate_t = jnp.where(sub == i1, g1, jnp.where(sub == i2, g2, 0.0))

    onehot_t = jnp.where(sub == grp, 1.0, 0.0)
    csum = jnp.where(live, jnp.sum(onehot_t, axis=1, keepdims=True), 0.0)
    rr = lax.broadcasted_iota(jnp.int32, (tm, tm), 0)
    cc = lax.broadcasted_iota(jnp.int32, (tm, tm), 1)
    earlier = jnp.where(rr < cc, 1.0, 0.0).astype(BF16)
    padded = jnp.concatenate([onehot_t, jnp.zeros_like(onehot_t)], axis=0).astype(BF16)
    same_before = _dot(padded, earlier)[0:n_sub]
    yield
    lower = jnp.sum(jnp.where(sub < grp, csum, 0.0), axis=0, keepdims=True)
    lrank = lower + jnp.sum(onehot_t * same_before, axis=0, keepdims=True)
    route_ref[0:n_sub, :] = gate_t
    route_ref[n_sub:, :] = jnp.broadcast_to(lrank, (n_sub, tm))
    carry = cnt_ref[...]
    lane = lax.broadcasted_iota(jnp.int32, (n_sub, LANES), 1)
    seg_ref[...] = jnp.where(lane == SEG_START, carry,
                             jnp.where(lane == SEG_LEN, csum, 0.0)).astype(jnp.int32)
    cnt_ref[...] = carry + csum


def _outproj_kernel(actp_ref, acts_ref, gates_ref, xa_ref, xb_ref, pa_ref, pb_ref, pc_ref, wo_ref,
                    n2_ref, wrh_ref, wrl_ref, rb_ref, xo_ref, xn_ref, route_ref, seg_ref, cnt_ref,
                    split_ref, *, n_xa):
    i = pl.program_id(0)

    @pl.when(i == 0)
    def _():
        cnt_ref[...] = jnp.zeros_like(cnt_ref)
        split_ref[...] = jnp.zeros_like(split_ref)

    tile = jnp.minimum(i, pl.num_programs(0) - 2)
    part = TM_OUT // OUT_PARTS
    _interleave(
        _outproj_route(i >= 1, split_ref, wrh_ref, wrl_ref, rb_ref, route_ref, seg_ref, cnt_ref),
        *[_delayed(_outproj_main(tile, pl.ds(p * part, part), n_xa, actp_ref, acts_ref, gates_ref,
                                 xa_ref, xb_ref, pa_ref, pb_ref, pc_ref, wo_ref, n2_ref, xo_ref,
                                 xn_ref, split_ref), p * OUT_STAGGER)
          for p in range(OUT_PARTS)])


def _lane_prefix(v):
    rr = lax.broadcasted_iota(jnp.int32, (LANES, LANES), 0)
    cc = lax.broadcasted_iota(jnp.int32, (LANES, LANES), 1)
    earlier = jnp.where(rr < cc, 1.0, 0.0).astype(BF16)
    return _dot(jnp.broadcast_to(v, (8, LANES)).astype(BF16), earlier)[0:1]


def _outproj(acts_p, acts_s, z, xa, xb, pa, pb, pc, wo, n2, wrh, wrl, rb, layer):
    m = T_ALL
    nt = m // TM_OUT
    row = lambda i: (jnp.minimum(i, nt - 1), 0)
    routed = lambda i: (jnp.maximum(i - 1, 0), 0, 0)
    _, act_specs = _two_part_specs(TM_OUT, acts_p, acts_s)
    n_xa, x_specs = _two_part_specs(TM_OUT, xa, xb)
    weights = (pa, pb, pc, wo, n2, wrh, wrl, rb)
    return pl.pallas_call(
        functools.partial(_outproj_kernel, n_xa=n_xa),
        grid=(nt + 1,),
        in_specs=act_specs + [pl.BlockSpec((TM_OUT, 3 * D_MODEL), row)] + x_specs
        + [_layer_spec(w, layer) for w in weights],
        out_specs=[
            pl.BlockSpec((TM_OUT, D_MODEL), row),
            pl.BlockSpec((TM_OUT, D_MODEL), row),
            pl.BlockSpec((None, ROUTE_ROWS, TM_OUT), routed),
            pl.BlockSpec((None, 8, LANES), routed),
            pl.BlockSpec((8, LANES), lambda i: (0, 0)),
        ],
        scratch_shapes=[pltpu.VMEM((2, TM_OUT, D_MODEL), BF16)],
        out_shape=[
            jax.ShapeDtypeStruct((m, D_MODEL), F32),
            jax.ShapeDtypeStruct((m, D_MODEL), F32),
            jax.ShapeDtypeStruct((m // TM_OUT, ROUTE_ROWS, TM_OUT), F32),
            jax.ShapeDtypeStruct((m // TM_OUT, 8, LANES), jnp.int32),
            jax.ShapeDtypeStruct((8, LANES), F32),
        ],
        compiler_params=pltpu.CompilerParams(
            dimension_semantics=("arbitrary",), vmem_limit_bytes=VMEM_LIMIT),
        name="outproj",
    )(acts_p, acts_s, z, xa, xb, *weights)


PLAN_GROUP, PLAN_USED, PLAN_END, PLAN_BLOCKS, PLAN_BASE = 0, 1, 2, 3, 4


def _plan_kernel(cnt_ref, plan_ref):
    sub_i = lax.broadcasted_iota(jnp.int32, (8, LANES), 0)
    lane8_i = lax.broadcasted_iota(jnp.int32, (8, LANES), 1)
    cnt = jnp.sum(jnp.where(sub_i == lane8_i, cnt_ref[...], 0.0), axis=0, keepdims=True)
    blocks = jnp.floor((cnt + (MOE_BLK - 1)) * (1.0 / MOE_BLK))
    start = _lane_prefix(blocks)
    end = start + blocks
    lane_i = lax.broadcasted_iota(jnp.int32, (1, LANES), 1)
    lane = lane_i.astype(F32)
    grp_of_blk = jnp.zeros((1, LANES), F32)
    for g in range(MOE_GROUPS):
        end_g = jnp.sum(jnp.where(lane_i == g, end, 0.0), axis=-1, keepdims=True)
        grp_of_blk += jnp.where(lane >= end_g, 1.0, 0.0)
    grp_of_blk = jnp.minimum(grp_of_blk, MOE_GROUPS - 1)
    n_used = jnp.sum(blocks, axis=-1, keepdims=True)
    row = lax.broadcasted_iota(jnp.int32, (8, LANES), 0)
    plan_ref[...] = jnp.where(
        row == PLAN_GROUP, grp_of_blk,
        jnp.where(row == PLAN_USED, n_used,
                  jnp.where(row == PLAN_END, end,
                            jnp.where(row == PLAN_BLOCKS, blocks,
                                      jnp.where(row == PLAN_BASE, start * MOE_BLK, 0.0))))
    ).astype(jnp.int32)


def _plan(cnt):
    return pl.pallas_call(
        _plan_kernel,
        out_shape=jax.ShapeDtypeStruct((8, LANES), jnp.int32),
        name="plan",
    )(cnt)


SEG_START, SEG_LEN = 0, 1


def _segment_copies(seg_ref, base_ref, make_copy):
    local = 0
    for g in range(MOE_GROUPS):
        n = seg_ref[g, SEG_LEN]
        first = base_ref[g] + seg_ref[g, SEG_START]
        k = TM_ROW
        while k >= 1:
            done = n & ~(2 * k - 1)
            @pl.when((n & k) != 0)
            def _():
                make_copy(local + done, first + done, k).start()
            k //= 2
        local = local + n


def _perm_matrix(route_ref):
    row = lax.broadcasted_iota(jnp.int32, (TM_ROW, TM_ROW), 0).astype(F32)
    return jnp.where(row == route_ref[ROUTE_LRANK:ROUTE_LRANK + 1, :], 1.0, 0.0).astype(BF16)


SUB = 8
SUB_X = D_MODEL // 2 // LANES
SUB_GATE = SUB_X
HI_MASK = -65536


def _rows(first_row, n_rows):
    return pl.ds(pl.multiple_of(first_row * SUB, SUB), n_rows * SUB)


def _sublane(s, n_rows):
    return pl.ds(s, n_rows, stride=SUB)


def _scatter_kernel(nu_ref, end_ref, nb_ref, base_ref, seg_ref, route_ref, xn_ref, xs_ref,
                    zbuf, sbuf, sems):
    i = pl.program_id(0)
    n = pl.num_programs(0)
    sem = sems.at[0]

    def zero_block(b):
        return pltpu.make_async_copy(zbuf, xs_ref.at[_rows(b * MOE_BLK, MOE_BLK)], sem)

    def each_unfilled_block(fn):
        for g in range(MOE_GROUPS):
            @pl.when(nb_ref[g] > 0)
            def _():
                fn(zero_block(end_ref[g] - 1))
        for b in range(T_ALL // MOE_BLK, N_BLK):
            @pl.when(b >= nu_ref[0])
            def _():
                fn(zero_block(b))

    @pl.when(i == 0)
    def _():
        zbuf[...] = jnp.zeros_like(zbuf)
        sbuf[...] = jnp.zeros_like(sbuf)
        each_unfilled_block(lambda c: c.start())
        each_unfilled_block(lambda c: c.wait())

    def tile_done(s):
        pltpu.make_async_copy(sbuf.at[s], xs_ref.at[_rows(0, TM_ROW)], sems.at[s]).wait()

    for t in range(ROW_TILES):
        slot = (i % 2) * ROW_TILES + t

        @pl.when(i >= 2)
        def _():
            tile_done(slot)

        route = route_ref.at[t]
        perm = _perm_matrix(route)
        xn = xn_ref[t * TM_ROW:(t + 1) * TM_ROW, :]
        xs = lax.bitcast_convert_type(_dot(perm, xn.astype(BF16)), jnp.int32)
        half = D_MODEL // 2
        for s in range(SUB_X):
            hi = xs[:, s * LANES:(s + 1) * LANES] & HI_MASK
            lo = lax.shift_right_logical(xs[:, half + s * LANES:half + (s + 1) * LANES], 16)
            sbuf[slot, _sublane(s, TM_ROW), :] = hi | lo
        gate_t = route[0:EXPERTS_PER_GROUP, :]
        gate_t = jnp.concatenate(
            [gate_t, jnp.zeros((LANES - EXPERTS_PER_GROUP, TM_ROW), F32)], axis=0)
        g1 = gate_t.astype(BF16)
        r1 = gate_t - g1.astype(F32)
        g2 = r1.astype(BF16)
        g3 = (r1 - g2.astype(F32)).astype(BF16)
        sbuf[slot, _sublane(SUB_GATE, TM_ROW), :] = lax.bitcast_convert_type(
            _dot_nt(perm, g1) + _dot_nt(perm, g2) + _dot_nt(perm, g3), jnp.int32)

        def make_copy(src_row, dst_row, k, slot=slot):
            return pltpu.make_async_copy(sbuf.at[slot, _rows(src_row, k)],
                                         xs_ref.at[_rows(dst_row, k)], sems.at[slot])
        _segment_copies(seg_ref.at[t], base_ref, make_copy)

    @pl.when(i == n - 1)
    def _():
        for t in range(ROW_TILES):
            tile_done((i % 2) * ROW_TILES + t)

            @pl.when(n > 1)
            def _():
                tile_done((1 - i % 2) * ROW_TILES + t)


def _scatter(seg, route, xn, n_used, grp_end, grp_blocks, grp_base):
    m = xn.shape[0]
    step_rows = ROW_TILES * TM_ROW
    grid_spec = pltpu.PrefetchScalarGridSpec(
        num_scalar_prefetch=4,
        grid=(m // step_rows,),
        in_specs=[
            pl.BlockSpec((ROW_TILES, 8, LANES), lambda i, *_: (i, 0, 0), memory_space=pltpu.SMEM),
            pl.BlockSpec((ROW_TILES, ROUTE_ROWS, TM_ROW), lambda i, *_: (i, 0, 0)),
            pl.BlockSpec((step_rows, D_MODEL), lambda i, *_: (i, 0)),
        ],
        out_specs=pl.BlockSpec(memory_space=pl.ANY),
        scratch_shapes=[pltpu.VMEM((MOE_BLK * SUB, LANES), jnp.int32),
                        pltpu.VMEM((2 * ROW_TILES, TM_ROW * SUB, LANES), jnp.int32),
                        pltpu.SemaphoreType.DMA((2 * ROW_TILES,))],
    )
    return pl.pallas_call(
        _scatter_kernel,
        grid_spec=grid_spec,
        out_shape=jax.ShapeDtypeStruct((N_SORTED * SUB, LANES), jnp.int32),
        compiler_params=pltpu.CompilerParams(
            dimension_semantics=("arbitrary",), vmem_limit_bytes=VMEM_LIMIT),
        name="scatter",
    )(n_used, grp_end, grp_blocks, grp_base, seg, route, xn)


def _ffn_kernel(bg_ref, nu_ref, xs_ref, w1_ref, w3_ref, w2_ref, y_ref):
    del bg_ref
    b = pl.program_id(0)

    @pl.when(b < nu_ref[0])
    def _():
        packed = [xs_ref[_sublane(s, MOE_BLK), :] for s in range(SUB_X)]
        x = jnp.concatenate(
            [lax.bitcast_convert_type(u & HI_MASK, F32).astype(BF16) for u in packed]
            + [lax.bitcast_convert_type(lax.shift_left(u, 16), F32).astype(BF16) for u in packed],
            axis=1)
        gates = lax.bitcast_convert_type(xs_ref[_sublane(SUB_GATE, MOE_BLK), :], F32)
        hs = []
        for e in range(EXPERTS_PER_GROUP):
            h1 = _dot(x, w1_ref[e])
            h3 = _dot(x, w3_ref[e])
            ge = gates[:, e:e + 1]
            hs.append(jnp.where(ge > 0.0, h1 * _sigmoid(h1) * h3 * ge, 0.0).astype(BF16))
        hcat = jnp.concatenate(hs, axis=1)
        y = _dot(hcat, w2_ref[...].reshape(EXPERTS_PER_GROUP * D_EXPERT, D_MODEL))
        for s in range(SUB):
            y_ref[_sublane(s, MOE_BLK), :] = y[:, s * LANES:(s + 1) * LANES]

    @pl.when(b >= nu_ref[0])
    def _():
        y_ref[...] = jnp.zeros_like(y_ref)


def _ffn(blk_group, n_used, xs, w1, w3, w2):
    grouped = (MOE_GROUPS, EXPERTS_PER_GROUP)
    w1, w3, w2 = (w.reshape(grouped + w.shape[1:]) for w in (w1, w3, w2))
    wmap = lambda b, bg, nu: (bg[b], 0, 0, 0)
    grid_spec = pltpu.PrefetchScalarGridSpec(
        num_scalar_prefetch=2,
        grid=(N_BLK,),
        in_specs=[
            pl.BlockSpec((MOE_BLK * SUB, LANES), lambda b, bg, nu: (b, 0)),
            pl.BlockSpec((None, EXPERTS_PER_GROUP, D_MODEL, D_EXPERT), wmap),
            pl.BlockSpec((None, EXPERTS_PER_GROUP, D_MODEL, D_EXPERT), wmap),
            pl.BlockSpec((None, EXPERTS_PER_GROUP, D_EXPERT, D_MODEL), wmap),
        ],
        out_specs=pl.BlockSpec((MOE_BLK * SUB, LANES), lambda b, bg, nu: (b, 0)),
    )
    return pl.pallas_call(
        _ffn_kernel,
        grid_spec=grid_spec,
        out_shape=jax.ShapeDtypeStruct((N_SORTED * SUB, LANES), F32),
        compiler_params=pltpu.CompilerParams(
            dimension_semantics=("arbitrary",), vmem_limit_bytes=VMEM_LIMIT),
        name="ffn",
    )(blk_group, n_used, xs, w1, w3, w2)


def _combine_kernel(base_ref, seg_ref, segn_ref, route_ref, ys_hbm, x_ref, g_ref, *refs, final):
    outs, (buf, sem) = refs[:-2], refs[-2:]
    i = pl.program_id(0)
    n = pl.num_programs(0)
    step_rows = ROW_TILES * TM_ROW

    def gather(seg, parity):
        for t in range(ROW_TILES):
            s = parity * ROW_TILES + t

            def make_copy(buf_row, ys_row, k, s=s):
                return pltpu.make_async_copy(ys_hbm.at[_rows(ys_row, k)],
                                             buf.at[s, _rows(buf_row, k)], sem.at[s])
            _segment_copies(seg.at[t], base_ref, make_copy)

    @pl.when(i == 0)
    def _():
        gather(seg_ref, 0)

    @pl.when(i + 1 < n)
    def _():
        gather(segn_ref, 1 - i % 2)

    tn = (((0,), (0,)), ((), ()))
    for t in range(ROW_TILES):
        slot = (i % 2) * ROW_TILES + t
        rows = slice(t * TM_ROW, (t + 1) * TM_ROW)
        pltpu.make_async_copy(ys_hbm.at[_rows(0, TM_ROW)], buf.at[slot], sem.at[slot]).wait()

        perm = _perm_matrix(route_ref.at[t])
        cols = []
        for s in range(SUB):
            hi, lo = _split_bf16(buf[slot, _sublane(s, TM_ROW), :])
            cols.append(lax.dot_general(perm, hi, tn, preferred_element_type=F32)
                        + lax.dot_general(perm, lo, tn, preferred_element_type=F32))
        y = x_ref[rows, :] + jnp.concatenate(cols, axis=1)
        if not final:
            outs[0][rows, :] = y
        else:
            y = _rms(y, g_ref[...])

            @pl.when(i < T_PROMPT // step_rows)
            def _():
                outs[0][rows, :] = y

            @pl.when(i >= T_PROMPT // step_rows)
            def _():
                outs[1][rows, :] = y


def _combine(seg, route, grp_base, ys, x, g, final):
    m = x.shape[0]
    step_rows = ROW_TILES * TM_ROW
    nt = m // step_rows
    n_p = T_PROMPT // step_rows
    smem = functools.partial(pl.BlockSpec, (ROW_TILES, 8, LANES), memory_space=pltpu.SMEM)
    tile = (step_rows, D_MODEL)
    if final:
        out_specs = [pl.BlockSpec(tile, lambda i, *_: (jnp.minimum(i, n_p - 1), 0)),
                     pl.BlockSpec(tile, lambda i, *_: (jnp.maximum(i - n_p, 0), 0))]
        out_shape = [jax.ShapeDtypeStruct((T_PROMPT, D_MODEL), F32),
                     jax.ShapeDtypeStruct((T_SAMPLE, D_MODEL), F32)]
    else:
        out_specs = [pl.BlockSpec(tile, lambda i, *_: (i, 0))]
        out_shape = [jax.ShapeDtypeStruct((m, D_MODEL), F32)]
    grid_spec = pltpu.PrefetchScalarGridSpec(
        num_scalar_prefetch=1,
        grid=(nt,),
        in_specs=[
            smem(lambda i, *_: (i, 0, 0)),
            smem(lambda i, *_: (jnp.minimum(i + 1, nt - 1), 0, 0)),
            pl.BlockSpec((ROW_TILES, ROUTE_ROWS, TM_ROW), lambda i, *_: (i, 0, 0)),
            pl.BlockSpec(memory_space=pl.ANY),
            pl.BlockSpec(tile, lambda i, *_: (i, 0)),
            pl.BlockSpec(g.shape, lambda i, *_: (0, 0)),
        ],
        out_specs=out_specs,
        scratch_shapes=[pltpu.VMEM((2 * ROW_TILES, TM_ROW * SUB, LANES), F32),
                        pltpu.SemaphoreType.DMA((2 * ROW_TILES,))],
    )
    return pl.pallas_call(
        functools.partial(_combine_kernel, final=final),
        grid_spec=grid_spec,
        out_shape=out_shape,
        compiler_params=pltpu.CompilerParams(
            dimension_semantics=("arbitrary",), vmem_limit_bytes=VMEM_LIMIT),
        name="combine",
    )(grp_base, seg, seg, route, ys, x, g)


def _prep_weights(w_in, gla_a2, cm_ws, cm_b, router_group_w, router_group_b,
                  router_expert_w, router_expert_b):
    off = {}
    o = 0
    for name, n in (("h", 512), ("cg", 512), ("bg", 512), ("q", 512), ("k", 512), ("v", 1024),
                    ("r", 1024), ("alr", 16), ("u", 512), ("vv", 512), ("ga", 1024),
                    ("gb", 1024), ("gc", 1024)):
        off[name] = (o, n)
        o += n
    runs = (("ga", "gc"), ("v", "r"), ("q", "k"), ("h", "cg"), ("bg", "bg"), ("u", "vv"))
    w_t = jnp.swapaxes(w_in, 1, 2)
    w_z = jnp.concatenate([w_t[:, off[a][0]:off[b][0] + off[b][1]] for a, b in runs],
                          axis=1).astype(BF16)
    assert w_z.shape[1] == Z_COLS
    a0 = off["alr"][0]
    w_alr = jnp.pad(w_t[:, a0:a0 + GLA_LOWRANK],
                    ((0, 0), (0, LANES - GLA_LOWRANK), (0, 0))).astype(BF16)
    a2 = jnp.pad(gla_a2, ((0, 0), (0, LANES - GLA_LOWRANK), (0, 0))).astype(BF16)
    ws_p = jnp.tril(cm_ws).astype(BF16)
    small = jnp.tril(cm_ws[:, :, :DEC_SEQ, :DEC_SEQ])
    eye = jnp.eye(SEQ_PER_BLK, dtype=F32)
    ws_s = jnp.einsum("ij,lgab->lgiajb", eye, small).reshape(
        DEPTH, CM_GROUPS, ROWS_S, ROWS_S).astype(BF16)
    cmb_p = jnp.broadcast_to(jnp.transpose(cm_b, (0, 2, 1))[:, :, :, None],
                             (DEPTH, CM_CHUNK, CM_GROUPS, CM_GCH)).reshape(DEPTH, CM_CHUNK, CM_CH)
    cmb_s = jnp.tile(cmb_p[:, :DEC_SEQ], (1, SEQ_PER_BLK, 1))
    pad = LANES - MOE_GROUPS - N_EXPERTS
    w_r = jnp.pad(jnp.swapaxes(jnp.concatenate([router_group_w, router_expert_w], axis=-1), 1, 2),
                  ((0, 0), (0, pad), (0, 0)))
    w_r_hi = w_r.astype(BF16)
    w_r_lo = (w_r - w_r_hi.astype(F32)).astype(BF16)
    r_b = jnp.pad(jnp.concatenate([router_group_b, router_expert_b], axis=-1),
                  ((0, 0), (0, pad)))[:, :, None]
    return w_z, w_alr, a2, ws_p, ws_s, cmb_p, cmb_s, w_r_hi, w_r_lo, r_b


def kernel(x_prompt, x_sample, state_conv, state_gla, norm1_g, w_in, conv_w, gla_a2, gla_a_b,
           gla_norm_g, cm_norm_g, cm_ws, cm_b, proj_a, proj_b, proj_c, w_out, norm2_g,
           router_group_w, router_group_b, router_expert_w, router_expert_b,
           exp_w1, exp_w3, exp_w2, final_norm_g):
    (w_z, w_alr, a2, ws_p, ws_s, cmb_p, cmb_s, w_r_hi, w_r_lo, r_b) = _prep_weights(
        w_in, gla_a2, cm_ws, cm_b, router_group_w, router_group_b, router_expert_w,
        router_expert_b)
    pa, pb, pc, wo = (w.astype(BF16) for w in (proj_a, proj_b, proj_c, w_out))
    n1 = norm1_g[:, None, :]
    n2 = norm2_g[:, None, :]
    ab = gla_a_b[:, None, :]
    gng = gla_norm_g[:, None, :]
    cmg = cm_norm_g.reshape(DEPTH, 1, CM_CH)
    fg = final_norm_g[None, :]
    xa = x_prompt.reshape(T_PROMPT, D_MODEL)
    xb = x_sample.reshape(T_SAMPLE, D_MODEL)
    gla_s = None
    conv_p, gla_p, conv_s, cmv_s = [], [], [], []
    for l in range(DEPTH):
        z, alr, w1, w3, w2 = _inproj(xa, xb, n1, w_z, w_alr, exp_w1, exp_w3, exp_w2, l)
        acts_p, nconv, ngla = _mix_prompt(z, alr, a2, ab, conv_w, gng, cmg, ws_p, cmb_p, l)
        sc = state_conv[l]
        p2 = jnp.pad(sc, ((0, 0), (0, DEC_SEQ - 2), (0, 0))).reshape(T_SAMPLE, CONV_CH)
        p1 = jnp.pad(sc[:, 1:2], ((0, 0), (0, DEC_SEQ - 1), (0, 0))).reshape(T_SAMPLE, CONV_CH)
        acts_s, cin_s, gla_s, vrows = _mix_sample(z, alr, a2, ab, conv_w, gng, cmg, ws_s, cmb_s,
                                                  p1, p2, state_gla, gla_s, l)
        conv_p.append(nconv)
        gla_p.append(ngla)
        conv_s.append(cin_s.reshape(DEC_BATCH, DEC_SEQ, CONV_CH)[:, DEC_SEQ - (CONV_K - 1):])
        cmv_s.append(vrows.reshape(DEC_BATCH, DEC_SEQ, CM_CH))

        x, xn, route, seg, cnt = _outproj(acts_p, acts_s, z, xa, xb, pa, pb, pc, wo, n2,
                                          w_r_hi, w_r_lo, r_b, l)
        plan = _plan(cnt)
        n_used = plan[PLAN_USED, :1]
        grp_base = plan[PLAN_BASE, :MOE_GROUPS]
        xs = _scatter(seg, route, xn, n_used, plan[PLAN_END, :MOE_GROUPS],
                      plan[PLAN_BLOCKS, :MOE_GROUPS], grp_base)
        ys = _ffn(plan[PLAN_GROUP, :N_BLK], n_used, xs, w1, w3, w2)
        out = _combine(seg, route, grp_base, ys, x, fg, l == DEPTH - 1)
        xa = xb = out[0]

    y_prompt = out[0].reshape(BATCH, SEQ, D_MODEL)
    y_sample = out[1].reshape(DEC_BATCH, DEC_SEQ, D_MODEL)
    return (y_prompt, y_sample, jnp.stack(conv_p), jnp.stack(gla_p), jnp.stack(conv_s),
            gla_s, jnp.stack(cmv_s))
```

```python
import functools

import jax
import jax.numpy as jnp
from jax import lax
from jax.experimental import pallas as pl
from jax.experimental.pallas import tpu as pltpu

F32 = jnp.float32
BF16 = jnp.bfloat16

D_MODEL = 1024
BATCH = 8
SEQ = 2048
DEPTH = 2
DEC_BATCH = 128
DEC_SEQ = 8
CONV_K = 3
CONV_CH = 512
GLA_HEADS = 4
GLA_DK = 128
GLA_DV = 256
GLA_QK = GLA_HEADS * GLA_DK
GLA_V = GLA_HEADS * GLA_DV
GLA_LOWRANK = 16
GLA_TAU = 16.0
GLA_CHUNK = 64
CM_GROUPS = 4
CM_CHUNK = 128
CM_GCH = 128
CM_CH = 512
MOE_GROUPS = 8
EXPERTS_PER_GROUP = 8
N_EXPERTS = 64
D_EXPERT = 256
EPS = 1e-6

LANES = 128
T_PROMPT = BATCH * SEQ
T_SAMPLE = DEC_BATCH * DEC_SEQ
T_ALL = T_PROMPT + T_SAMPLE

COL_GA, COL_GB, COL_GC = 0, 1024, 2048
COL_V, COL_R, COL_Q, COL_K = 3072, 4096, 5120, 5632
COL_H, COL_CG, COL_BG, COL_U, COL_VV = 6144, 6656, 7168, 7680, 8192
Z_COLS = 8704
ACT_COLS = 2048
ROUTE_ROWS = 2 * EXPERTS_PER_GROUP
ROUTE_LRANK = EXPERTS_PER_GROUP

TM_IN = 1024
TN_IN = Z_COLS // 4
TC_MIX = 256
SEQ_PER_BLK = 16
ROWS_S = SEQ_PER_BLK * DEC_SEQ
TM_OUT = 256
OUT_PARTS = 1
OUT_STAGGER = 1
TM_ROW = TM_OUT
ROW_TILES = 4
MOE_BLK = 256
FFN_CAP = 96
N_BLK = T_ALL // MOE_BLK + MOE_GROUPS
N_SORTED = N_BLK * MOE_BLK
VMEM_LIMIT = 56 * 1024 * 1024


def _sigmoid(x):
    return 0.5 * jnp.tanh(0.5 * x) + 0.5


def _gelu_tanh(x):
    c = 0.7978845608028654
    half = 0.5 * x
    return half + half * jnp.tanh(x * (c + (c * 0.044715) * (x * x)))


def _log_sigmoid(x):
    log2_e = 1.4426950408889634
    ln_2 = 0.6931471805599453
    return jnp.minimum(x, 0.0) - ln_2 * jnp.log2(1.0 + jnp.exp2(-log2_e * jnp.abs(x)))


def _rms(x, g):
    ms = jnp.mean(x * x, axis=-1, keepdims=True)
    return x * lax.rsqrt(ms + EPS) * g


def _split_bf16(x):
    hi = x.astype(BF16)
    lo = (x - hi.astype(F32)).astype(BF16)
    return hi, lo


def _dot(a, b):
    return jnp.dot(a, b, preferred_element_type=F32)


def _dot_nt(a, b):
    return lax.dot_general(a, b, (((1,), (1,)), ((), ())), preferred_element_type=F32)


def _layer_spec(arr, layer):
    nd = arr.ndim - 1
    return pl.BlockSpec((None,) + arr.shape[1:], lambda *g: (layer,) + (0,) * nd)


def _const_spec(arr):
    nd = arr.ndim
    return pl.BlockSpec(arr.shape, lambda *g: (0,) * nd)


def _two_part_specs(tile, xa, xb):
    n_a = xa.shape[0] // tile
    n_b = xb.shape[0] // tile
    return n_a, [
        pl.BlockSpec((tile, xa.shape[1]), lambda i, *_: (jnp.minimum(i, n_a - 1), 0)),
        pl.BlockSpec((tile, xb.shape[1]), lambda i, *_: (jnp.clip(i - n_a, 0, n_b - 1), 0)),
    ]


def _inproj_kernel(xa_ref, xb_ref, g_ref, w_ref, wa_ref, e1_ref, e3_ref, e2_ref,
                   z_ref, a_ref, o1_ref, o3_ref, o2_ref, xn_ref, *, n_a):
    o1_ref[...] = e1_ref[...].astype(BF16)
    o3_ref[...] = e3_ref[...].astype(BF16)
    o2_ref[...] = e2_ref[...].astype(BF16)

    @pl.when(pl.program_id(1) == 0)
    def _():
        x = jnp.where(pl.program_id(0) < n_a, xa_ref[...], xb_ref[...])
        xn = _rms(x, g_ref[...]).astype(BF16)
        xn_ref[...] = xn
        a_ref[...] = _dot_nt(xn, wa_ref[...]).astype(BF16)

    z_ref[...] = _dot_nt(xn_ref[...], w_ref[...]).astype(BF16)


def _inproj(xa, xb, g, w, wa, e1, e3, e2, layer):
    m = T_ALL
    n_i, n_j = m // TM_IN, Z_COLS // TN_IN
    assert n_i * n_j >= N_EXPERTS
    n_a, x_specs = _two_part_specs(TM_IN, xa, xb)
    expert = lambda i, j: jnp.minimum(i * n_j + j, N_EXPERTS - 1)
    up, down = (D_MODEL, D_EXPERT), (D_EXPERT, D_MODEL)
    return pl.pallas_call(
        functools.partial(_inproj_kernel, n_a=n_a),
        grid=(n_i, n_j),
        in_specs=x_specs + [
            _layer_spec(g, layer),
            pl.BlockSpec((None, TN_IN, D_MODEL), lambda i, j: (layer, j, 0)),
            _layer_spec(wa, layer),
            pl.BlockSpec((None, None) + up, lambda i, j: (layer, expert(i, j), 0, 0)),
            pl.BlockSpec((None, None) + up, lambda i, j: (layer, expert(i, j), 0, 0)),
            pl.BlockSpec((None, None) + down, lambda i, j: (layer, expert(i, j), 0, 0)),
        ],
        out_specs=[
            pl.BlockSpec((TM_IN, TN_IN), lambda i, j: (i, j)),
            pl.BlockSpec((TM_IN, LANES), lambda i, j: (i, 0)),
            pl.BlockSpec((None,) + up, lambda i, j: (expert(i, j), 0, 0)),
            pl.BlockSpec((None,) + up, lambda i, j: (expert(i, j), 0, 0)),
            pl.BlockSpec((None,) + down, lambda i, j: (expert(i, j), 0, 0)),
        ],
        out_shape=[
            jax.ShapeDtypeStruct((m, Z_COLS), BF16),
            jax.ShapeDtypeStruct((m, LANES), BF16),
            jax.ShapeDtypeStruct((N_EXPERTS,) + up, BF16),
            jax.ShapeDtypeStruct((N_EXPERTS,) + up, BF16),
            jax.ShapeDtypeStruct((N_EXPERTS,) + down, BF16),
        ],
        scratch_shapes=[pltpu.VMEM((TM_IN, D_MODEL), BF16)],
        compiler_params=pltpu.CompilerParams(
            dimension_semantics=("arbitrary", "arbitrary"), vmem_limit_bytes=VMEM_LIMIT),
        name="inproj",
    )(xa, xb, g, w, wa, e1, e3, e2)


def _gla_log_decay(alr_ref, a2_ref, ab_ref):
    la = _log_sigmoid(_dot(alr_ref[...], a2_ref[...]) + ab_ref[...]) * (1.0 / GLA_TAU)
    return _split_bf16(la)


def _masked_sum(mask, la_hi, la_lo):
    m = jnp.where(mask, 1.0, 0.0).astype(BF16)
    return _dot(m, la_hi) + _dot(m, la_lo)


def _gla_decay_prefix(alr_ref, a2_ref, ab_ref, tril_mask):
    la_hi, la_lo = _gla_log_decay(alr_ref, a2_ref, ab_ref)
    return _masked_sum(tril_mask, la_hi, la_lo)


def _gla_decay_terms(alr_ref, a2_ref, ab_ref, tril_mask, same_mask):
    la_hi, la_lo = _gla_log_decay(alr_ref, a2_ref, ab_ref)
    return (la_hi, la_lo, _masked_sum(tril_mask, la_hi, la_lo),
            _masked_sum(same_mask, la_hi, la_lo))


def _gla_out_gate(o, g_ref, r):
    return _rms(o, g_ref[...]) * (r * _sigmoid(r))


def _chunk_mlp_group(g, bgu_ref, vv_ref, cmg_ref, ws_ref, cmb_ref, n_chunks):
    sl = slice(g * CM_GCH, (g + 1) * CM_GCH)
    ug = _gelu_tanh(bgu_ref[:, CONV_CH + g * CM_GCH:CONV_CH + (g + 1) * CM_GCH])
    vg = _rms(_gelu_tanh(vv_ref[:, sl]).astype(F32), cmg_ref[:, sl])
    vgb = vg.astype(BF16)
    rows = []
    for j in range(n_chunks):
        rs = slice(j * CM_CHUNK, (j + 1) * CM_CHUNK)
        rows.append(_dot(ws_ref[g], vgb[rs]) + cmb_ref[:, sl])
    s = rows[0] if n_chunks == 1 else jnp.concatenate(rows, axis=0)
    return ug * s, vg


def _chunk_mlp(bgu_ref, vv_ref, cmg_ref, ws_ref, cmb_ref, n_chunks):
    parts = [_chunk_mlp_group(g, bgu_ref, vv_ref, cmg_ref, ws_ref, cmb_ref, n_chunks)
             for g in range(CM_GROUPS)]
    return (jnp.concatenate([p[0] for p in parts], axis=1),
            jnp.concatenate([p[1] for p in parts], axis=1))


N_Z_VIEWS = 7
N_MIX_W = 7
SEQ_PER_STEP = 2
SEQ_STAGGER = 4


def _mix_prompt_kernel(*refs):
    n_z = N_Z_VIEWS * SEQ_PER_STEP
    weights = refs[n_z:n_z + N_MIX_W]
    acts_ref, nconv_ref, ngla_ref, st_ref, carry_ref = refs[n_z + N_MIX_W:]

    @pl.when(pl.program_id(1) == 0)
    def _():
        st_ref[...] = jnp.zeros_like(st_ref)
        carry_ref[...] = jnp.zeros_like(carry_ref)

    _interleave(*[
        _delayed(_mix_prompt_seq(*refs[N_Z_VIEWS * s:N_Z_VIEWS * (s + 1)], *weights,
                                acts_ref.at[s], nconv_ref.at[s], st_ref.at[s], carry_ref.at[s]),
                s * SEQ_STAGGER)
        for s in range(SEQ_PER_STEP)])

    @pl.when(pl.program_id(1) == pl.num_programs(1) - 1)
    def _():
        for s in range(SEQ_PER_STEP):
            for hd in range(GLA_HEADS):
                ngla_ref[s, 0, hd] = st_ref[s, hd].T


def _mix_prompt_seq(v_ref, r_ref, qk_ref, hcg_ref, bgu_ref, vv_ref, alr_ref,
                    a2_ref, ab_ref, cw_ref, gng_ref, cmg_ref, ws_ref, cmb_ref,
                    acts_ref, nconv_ref, st_ref, carry_ref):
    tc = TC_MIX

    h = hcg_ref[:, :CONV_CH].astype(F32)
    cg = hcg_ref[:, CONV_CH:].astype(F32)
    bg = bgu_ref[:, :CONV_CH].astype(F32)
    cin = cg * h
    rr = lax.broadcasted_iota(jnp.int32, (tc, tc), 0)
    cc = lax.broadcasted_iota(jnp.int32, (tc, tc), 1)
    cin_b = cin.astype(BF16)
    x1 = _dot(jnp.where(rr - cc == 1, 1.0, 0.0).astype(BF16), cin_b)
    x2 = _dot(jnp.where(rr - cc == 2, 1.0, 0.0).astype(BF16), cin_b)
    conv = x2 * cw_ref[0:1, :] + x1 * cw_ref[1:2, :] + cin * cw_ref[2:3, :]
    c0 = carry_ref[0:1, :]
    c1 = carry_ref[1:2, :]
    row8 = lax.broadcasted_iota(jnp.int32, (8, 1), 0)
    head = jnp.where(row8 == 0, c0 * cw_ref[0:1, :] + c1 * cw_ref[1:2, :],
                     jnp.where(row8 == 1, c1 * cw_ref[0:1, :], 0.0))
    conv = jnp.concatenate([conv[0:8] + head, conv[8:]], axis=0)
    acts_ref[:, 0:CONV_CH] = (bg * conv).astype(BF16)
    carry_ref[0:2, :] = cin[tc - 2:tc, :]
    nconv_ref[0] = cin[tc - 2:tc, :]
    yield

    rr = lax.broadcasted_iota(jnp.int32, (tc, tc), 0)
    cc = lax.broadcasted_iota(jnp.int32, (tc, tc), 1)
    same = (rr >> 6) == (cc >> 6)
    tril = same & (cc <= rr)
    b = _gla_decay_prefix(alr_ref, a2_ref, ab_ref, tril)
    yield
    n_chunks = tc // GLA_CHUNK
    b_last = [b[(c + 1) * GLA_CHUNK - 1:(c + 1) * GLA_CHUNK, :] for c in range(n_chunks)]
    bl = jnp.concatenate([jnp.broadcast_to(r_, (GLA_CHUNK, GLA_QK)) for r_ in b_last], axis=0)
    q = qk_ref[:, :GLA_QK].astype(F32) * (GLA_DK ** -0.5)
    k = qk_ref[:, GLA_QK:].astype(F32)
    q_t = (q * jnp.exp(b)).astype(BF16)
    k_t = (k * jnp.exp(-b)).astype(BF16)
    k_end = (k * jnp.exp(bl - b)).astype(BF16)
    yield
    states = [st_ref[hd] for hd in range(GLA_HEADS)]
    k_cols = [slice(hd * GLA_DK, (hd + 1) * GLA_DK) for hd in range(GLA_HEADS)]
    v_cols = [slice(hd * GLA_DV, (hd + 1) * GLA_DV) for hd in range(GLA_HEADS)]
    o_intra = []
    for hd in range(GLA_HEADS):
        att = jnp.where(tril, _dot_nt(q_t[:, k_cols[hd]], k_t[:, k_cols[hd]]), 0.0).astype(BF16)
        o_intra.append(_dot(att, v_ref[:, v_cols[hd]]))
    yield

    o_rows = [[] for _ in range(GLA_HEADS)]
    assert n_chunks == CM_GROUPS
    for c in range(n_chunks):
        rs = slice(c * GLA_CHUNK, (c + 1) * GLA_CHUNK)
        for hd in range(GLA_HEADS):
            ks = k_cols[hd]
            st = states[hd]
            o_rows[hd].append(o_intra[hd][rs] + _dot_nt(q_t[rs, ks], st.astype(BF16)))
            upd = lax.dot_general(v_ref[rs, v_cols[hd]], k_end[rs, ks], (((0,), (0,)), ((), ())),
                                  preferred_element_type=F32)
            states[hd] = jnp.exp(b_last[c][:, ks]) * st + upd
        us, _ = _chunk_mlp_group(c, bgu_ref, vv_ref, cmg_ref, ws_ref, cmb_ref, tc // CM_CHUNK)
        acts_ref[:, CONV_CH + GLA_V + c * CM_GCH:CONV_CH + GLA_V + (c + 1) * CM_GCH] = (
            us.astype(BF16))
        yield

    for hd in range(GLA_HEADS):
        o = jnp.concatenate(o_rows[hd], axis=0)
        r = r_ref[:, v_cols[hd]]
        acts_ref[:, CONV_CH + hd * GLA_DV:CONV_CH + (hd + 1) * GLA_DV] = (
            _gla_out_gate(o, gng_ref, r).astype(BF16))
        st_ref[hd] = states[hd]
        if hd % 2 == 1:
            yield


def _z_specs(rows, row_map):
    def spec(width, col):
        blk = col // width
        return pl.BlockSpec((rows, width), lambda *g: (row_map(*g), blk))
    return [spec(1024, COL_V), spec(1024, COL_R), spec(1024, COL_Q), spec(1024, COL_H),
            spec(1024, COL_BG), spec(512, COL_VV)]


def _mix_prompt(z, alr, a2, ab, cw, gng, cmg, ws, cmb, layer):
    nt = SEQ // TC_MIX
    nb = BATCH // SEQ_PER_STEP
    small = (a2, ab, cw, gng, cmg, ws, cmb)
    assert len(small) == N_MIX_W
    in_specs, args = [], []
    for s in range(SEQ_PER_STEP):
        row_map = lambda b, c, s=s: (b + s * nb) * nt + c
        in_specs += _z_specs(TC_MIX, row_map) + [
            pl.BlockSpec((TC_MIX, LANES), lambda b, c, row_map=row_map: (row_map(b, c), 0))]
        args += [z] * (N_Z_VIEWS - 1) + [alr]
    in_specs += [_layer_spec(a, layer) for a in small]
    acts, nconv, ngla = pl.pallas_call(
        _mix_prompt_kernel,
        grid=(nb, nt),
        in_specs=in_specs,
        out_specs=[
            pl.BlockSpec((SEQ_PER_STEP, TC_MIX, ACT_COLS), lambda b, c: (0, b * nt + c, 0)),
            pl.BlockSpec((SEQ_PER_STEP, 1, CONV_K - 1, CONV_CH), lambda b, c: (0, b, 0, 0)),
            pl.BlockSpec((SEQ_PER_STEP, 1, GLA_HEADS, GLA_DK, GLA_DV),
                         lambda b, c: (0, b, 0, 0, 0)),
        ],
        out_shape=[
            jax.ShapeDtypeStruct((SEQ_PER_STEP, T_PROMPT // SEQ_PER_STEP, ACT_COLS), BF16),
            jax.ShapeDtypeStruct((SEQ_PER_STEP, nb, CONV_K - 1, CONV_CH), F32),
            jax.ShapeDtypeStruct((SEQ_PER_STEP, nb, GLA_HEADS, GLA_DK, GLA_DV), F32),
        ],
        scratch_shapes=[pltpu.VMEM((SEQ_PER_STEP, GLA_HEADS, GLA_DV, GLA_DK), F32),
                        pltpu.VMEM((SEQ_PER_STEP, 8, CONV_CH), F32)],
        compiler_params=pltpu.CompilerParams(
            dimension_semantics=("arbitrary", "arbitrary"), vmem_limit_bytes=VMEM_LIMIT),
        name="mix_prompt",
    )(*args, *small)
    return (acts.reshape(T_PROMPT, ACT_COLS), nconv.reshape(BATCH, CONV_K - 1, CONV_CH),
            ngla.reshape(BATCH, GLA_HEADS, GLA_DK, GLA_DV))


def _mix_sample_body(v_ref, r_ref, qk_ref, hcg_ref, bgu_ref, vv_ref, alr_ref,
                       a2_ref, ab_ref, cw_ref, gng_ref, cmg_ref, ws_ref, cmb_ref,
                       p1_ref, p2_ref, s0_ref,
                       acts_ref, cin_ref, ns_ref, vrow_ref):
    n = ROWS_S

    pos = lax.broadcasted_iota(jnp.int32, (n, 1), 0) & (DEC_SEQ - 1)
    h = hcg_ref[:, :CONV_CH].astype(F32)
    cg = hcg_ref[:, CONV_CH:].astype(F32)
    bg = bgu_ref[:, :CONV_CH].astype(F32)
    cin = cg * h
    x1 = jnp.where(pos >= 1, pltpu.roll(cin, 1, 0), p1_ref[...])
    x2 = jnp.where(pos >= 2, pltpu.roll(cin, 2, 0), p2_ref[...])
    conv = x2 * cw_ref[0:1, :] + x1 * cw_ref[1:2, :] + cin * cw_ref[2:3, :]
    acts_ref[:, 0:CONV_CH] = (bg * conv).astype(BF16)
    cin_ref[...] = cin

    rr = lax.broadcasted_iota(jnp.int32, (n, n), 0)
    cc = lax.broadcasted_iota(jnp.int32, (n, n), 1)
    same = (rr >> 3) == (cc >> 3)
    tril = same & (cc <= rr)
    la_hi, la_lo, b, bl = _gla_decay_terms(alr_ref, a2_ref, ab_ref, tril, same)
    q = qk_ref[:, :GLA_QK].astype(F32) * (GLA_DK ** -0.5)
    k = qk_ref[:, GLA_QK:].astype(F32)
    q_t = (q * jnp.exp(b)).astype(BF16)
    k_t = (k * jnp.exp(-b)).astype(BF16)
    k_end = k * jnp.exp(bl - b)
    la_hi = la_hi.astype(F32)
    la_lo = la_lo.astype(F32)
    row_seq = lax.broadcasted_iota(jnp.int32, (n, GLA_DK), 0) >> 3
    seq3 = lax.broadcasted_iota(jnp.int32, (SEQ_PER_BLK, GLA_DK, n), 0)
    lane_seq3 = lax.broadcasted_iota(jnp.int32, (SEQ_PER_BLK, GLA_DK, n), 2) >> 3
    mask3 = seq3 == lane_seq3
    ones = jnp.ones((n, GLA_DV), BF16)
    big = SEQ_PER_BLK * GLA_DK

    def per_seq(x_tr):
        x3 = jnp.where(mask3, x_tr[None, :, :], 0.0)
        return x3.reshape(big, n).astype(BF16)

    for hd in range(GLA_HEADS):
        ks = slice(hd * GLA_DK, (hd + 1) * GLA_DK)
        vs = slice(hd * GLA_DV, (hd + 1) * GLA_DV)
        qh = q_t[:, ks]
        vh = v_ref[:, vs]
        att = jnp.where(tril, _dot_nt(qh, k_t[:, ks]), 0.0).astype(BF16)
        o_intra = _dot(att, vh)
        s_old = s0_ref[:, hd].reshape(big, GLA_DV)
        zero = jnp.zeros_like(qh)
        q_big = jnp.concatenate(
            [jnp.where(row_seq == j, qh, zero) for j in range(SEQ_PER_BLK)], axis=1)
        o = o_intra + _dot(q_big, s_old.astype(BF16))
        dlog = _dot(per_seq(la_hi[:, ks].T), ones) + _dot(per_seq(la_lo[:, ks].T), ones)
        upd = _dot(per_seq(k_end[:, ks].T), vh)
        s_new = jnp.exp(dlog) * s_old + upd
        ns_ref[:, hd] = s_new.reshape(SEQ_PER_BLK, GLA_DK, GLA_DV)
        r = r_ref[:, vs]
        acts_ref[:, CONV_CH + hd * GLA_DV:CONV_CH + (hd + 1) * GLA_DV] = (
            _gla_out_gate(o, gng_ref, r).astype(BF16))

    us, vg = _chunk_mlp(bgu_ref, vv_ref, cmg_ref, ws_ref, cmb_ref, 1)
    acts_ref[:, CONV_CH + GLA_V:] = us.astype(BF16)
    vrow_ref[...] = vg


N_MIX_S_IN = 17


def _mix_sample_kernel(*refs, layer):
    if layer == 0:
        @pl.when(pl.program_id(0) == 0)
        def _():
            _mix_sample_body(*refs)

        @pl.when(pl.program_id(0) > 0)
        def _():
            ns_ref = refs[N_MIX_S_IN + 2]
            ns_ref[...] = jnp.zeros_like(ns_ref)
    else:
        _mix_sample_body(*refs[:N_MIX_S_IN], *refs[N_MIX_S_IN + 1:])


def _mix_sample(z, alr, a2, ab, cw, gng, cmg, ws, cmb, p1, p2, s0, ns_all, layer):
    row0 = T_PROMPT // ROWS_S
    n_i = DEC_BATCH // SEQ_PER_BLK
    n_pass = DEPTH if layer == 0 else 1
    blk = lambda p, i: jnp.where(p == 0, i, n_i - 1)
    row_map = lambda p, i: row0 + blk(p, i)
    slot = lambda p, i: (layer + p, i, 0, 0, 0)
    small = (a2, ab, cw, gng, cmg, ws, cmb)
    state_blk = (None, SEQ_PER_BLK, GLA_HEADS, GLA_DK, GLA_DV)
    in_specs = _z_specs(ROWS_S, row_map) + [
        pl.BlockSpec((ROWS_S, LANES), lambda p, i: (row_map(p, i), 0)),
    ] + [_layer_spec(a, layer) for a in small] + [
        pl.BlockSpec((ROWS_S, CONV_CH), lambda p, i: (blk(p, i), 0)),
        pl.BlockSpec((ROWS_S, CONV_CH), lambda p, i: (blk(p, i), 0)),
        pl.BlockSpec(state_blk, lambda p, i: (layer, blk(p, i), 0, 0, 0)),
    ]
    args = (z, z, z, z, z, z, alr, *small, p1, p2, s0)
    assert len(args) == N_MIX_S_IN
    aliases = {}
    if layer > 0:
        in_specs.append(pl.BlockSpec(memory_space=pl.ANY))
        args += (ns_all,)
        aliases = {N_MIX_S_IN: 2}
    return pl.pallas_call(
        functools.partial(_mix_sample_kernel, layer=layer),
        grid=(n_pass, n_i),
        in_specs=in_specs,
        out_specs=[
            pl.BlockSpec((ROWS_S, ACT_COLS), lambda p, i: (blk(p, i), 0)),
            pl.BlockSpec((ROWS_S, CONV_CH), lambda p, i: (blk(p, i), 0)),
            pl.BlockSpec(state_blk, slot),
            pl.BlockSpec((ROWS_S, CM_CH), lambda p, i: (blk(p, i), 0)),
        ],
        out_shape=[
            jax.ShapeDtypeStruct((T_SAMPLE, ACT_COLS), BF16),
            jax.ShapeDtypeStruct((T_SAMPLE, CONV_CH), F32),
            jax.ShapeDtypeStruct((DEPTH, DEC_BATCH, GLA_HEADS, GLA_DK, GLA_DV), F32),
            jax.ShapeDtypeStruct((T_SAMPLE, CM_CH), F32),
        ],
        input_output_aliases=aliases,
        compiler_params=pltpu.CompilerParams(
            dimension_semantics=("arbitrary", "arbitrary"), vmem_limit_bytes=VMEM_LIMIT),
        name="mix_sample",
    )(*args)


def _delayed(gen, n_stages):
    for _ in range(n_stages):
        yield
    yield from gen


def _interleave(*stage_lists):
    pending = list(stage_lists)
    while pending:
        for gen in list(pending):
            if next(gen, StopIteration) is StopIteration:
                pending.remove(gen)


def _outproj_main(i, rows, n_xa, actp_ref, acts_ref, gates_ref, xa_ref, xb_ref, pa_ref, pb_ref,
                  pc_ref, wo_ref, n2_ref, xo_ref, xn_ref, split_ref):
    acts = jnp.where(i < T_PROMPT // TM_OUT, actp_ref[rows, :], acts_ref[rows, :])
    x_in = jnp.where(i < n_xa, xa_ref[rows, :], xb_ref[rows, :])
    ya = _dot(acts[:, :CONV_CH], pa_ref[...])
    yb = _dot(acts[:, CONV_CH:CONV_CH + GLA_V], pb_ref[...])
    yc = _dot(acts[:, CONV_CH + GLA_V:], pc_ref[...])
    yield
    ga = _sigmoid(gates_ref[rows, COL_GA:COL_GA + D_MODEL])
    gb = _sigmoid(gates_ref[rows, COL_GB:COL_GB + D_MODEL])
    gc = _sigmoid(gates_ref[rows, COL_GC:COL_GC + D_MODEL])
    mix = ga * ya + gb * yb + gc * yc
    yield
    x = x_in + _dot(mix.astype(BF16), wo_ref[...])
    xo_ref[rows, :] = x
    yield
    xn = _rms(x, n2_ref[...])
    xn_ref[rows, :] = xn
    hi, lo = _split_bf16(xn)
    yield
    split_ref[0, rows, :] = hi
    split_ref[1, rows, :] = lo


def _outproj_route(live, split_ref, wrh_ref, wrl_ref, rb_ref, route_ref, seg_ref, cnt_ref):
    tm = TM_OUT
    hi = split_ref[0]
    lo = split_ref[1]
    logits = (_dot_nt(wrh_ref[...], hi) + _dot_nt(wrh_ref[...], lo) + _dot_nt(wrl_ref[...], hi)
              + rb_ref[...])
    yield
    n_sub = EXPERTS_PER_GROUP
    sub = lax.broadcasted_iota(jnp.int32, (n_sub, tm), 0).astype(F32)
    neg = jnp.float32(-jnp.inf)
    lg = logits[0:MOE_GROUPS]
    gmax = jnp.max(lg, axis=0, keepdims=True)
    grp = jnp.min(jnp.where(lg == gmax, sub, 1e9), axis=0, keepdims=True)
    p_grp = 1.0 / jnp.sum(jnp.exp(lg - gmax), axis=0, keepdims=True)
    yield
    le = logits[MOE_GROUPS:MOE_GROUPS + n_sub]
    for g in range(1, MOE_GROUPS):
        le = jnp.where(grp == g, logits[MOE_GROUPS + g * n_sub:MOE_GROUPS + (g + 1) * n_sub], le)
    v1 = jnp.max(le, axis=0, keepdims=True)
    i1 = jnp.min(jnp.where(le == v1, sub, 1e9), axis=0, keepdims=True)
    le2 = jnp.where(sub == i1, neg, le)
    v2 = jnp.max(le2, axis=0, keepdims=True)
    i2 = jnp.min(jnp.where(le2 == v2, sub, 1e9), axis=0, keepdims=True)
    yield
    t = jnp.exp(v2 - v1)
    g1 = p_grp / (1.0 + t)
    g2 = p_grp * t / (1.0 + t)
    gate_t = jnp.where(sub == i1, g1, jnp.where(sub == i2, g2, 0.0))

    onehot_t = jnp.where(sub == grp, 1.0, 0.0)
    csum = jnp.where(live, jnp.sum(onehot_t, axis=1, keepdims=True), 0.0)
    rr = lax.broadcasted_iota(jnp.int32, (tm, tm), 0)
    cc = lax.broadcasted_iota(jnp.int32, (tm, tm), 1)
    earlier = jnp.where(rr < cc, 1.0, 0.0).astype(BF16)
    padded = jnp.concatenate([onehot_t, jnp.zeros_like(onehot_t)], axis=0).astype(BF16)
    same_before = _dot(padded, earlier)[0:n_sub]
    yield
    lower = jnp.sum(jnp.where(sub < grp, csum, 0.0), axis=0, keepdims=True)
    lrank = lower + jnp.sum(onehot_t * same_before, axis=0, keepdims=True)
    route_ref[0:n_sub, :] = gate_t
    route_ref[n_sub:, :] = jnp.broadcast_to(lrank, (n_sub, tm))
    carry = cnt_ref[...]
    lane = lax.broadcasted_iota(jnp.int32, (n_sub, LANES), 1)
    seg_ref[...] = jnp.where(lane == SEG_START, carry,
                             jnp.where(lane == SEG_LEN, csum, 0.0)).astype(jnp.int32)
    cnt_ref[...] = carry + csum


def _outproj_kernel(actp_ref, acts_ref, gates_ref, xa_ref, xb_ref, pa_ref, pb_ref, pc_ref, wo_ref,
                    n2_ref, wrh_ref, wrl_ref, rb_ref, xo_ref, xn_ref, route_ref, seg_ref, cnt_ref,
                    split_ref, *, n_xa):
    i = pl.program_id(0)

    @pl.when(i == 0)
    def _():
        cnt_ref[...] = jnp.zeros_like(cnt_ref)
        split_ref[...] = jnp.zeros_like(split_ref)

    tile = jnp.minimum(i, pl.num_programs(0) - 2)
    part = TM_OUT // OUT_PARTS
    _interleave(
        _outproj_route(i >= 1, split_ref, wrh_ref, wrl_ref, rb_ref, route_ref, seg_ref, cnt_ref),
        *[_delayed(_outproj_main(tile, pl.ds(p * part, part), n_xa, actp_ref, acts_ref, gates_ref,
                                 xa_ref, xb_ref, pa_ref, pb_ref, pc_ref, wo_ref, n2_ref, xo_ref,
                                 xn_ref, split_ref), p * OUT_STAGGER)
          for p in range(OUT_PARTS)])


def _lane_prefix(v):
    rr = lax.broadcasted_iota(jnp.int32, (LANES, LANES), 0)
    cc = lax.broadcasted_iota(jnp.int32, (LANES, LANES), 1)
    earlier = jnp.where(rr < cc, 1.0, 0.0).astype(BF16)
    return _dot(jnp.broadcast_to(v, (8, LANES)).astype(BF16), earlier)[0:1]


def _outproj(acts_p, acts_s, z, xa, xb, pa, pb, pc, wo, n2, wrh, wrl, rb, layer):
    m = T_ALL
    nt = m // TM_OUT
    row = lambda i: (jnp.minimum(i, nt - 1), 0)
    routed = lambda i: (jnp.maximum(i - 1, 0), 0, 0)
    _, act_specs = _two_part_specs(TM_OUT, acts_p, acts_s)
    n_xa, x_specs = _two_part_specs(TM_OUT, xa, xb)
    weights = (pa, pb, pc, wo, n2, wrh, wrl, rb)
    return pl.pallas_call(
        functools.partial(_outproj_kernel, n_xa=n_xa),
        grid=(nt + 1,),
        in_specs=act_specs + [pl.BlockSpec((TM_OUT, 3 * D_MODEL), row)] + x_specs
        + [_layer_spec(w, layer) for w in weights],
        out_specs=[
            pl.BlockSpec((TM_OUT, D_MODEL), row),
            pl.BlockSpec((TM_OUT, D_MODEL), row),
            pl.BlockSpec((None, ROUTE_ROWS, TM_OUT), routed),
            pl.BlockSpec((None, 8, LANES), routed),
            pl.BlockSpec((8, LANES), lambda i: (0, 0)),
        ],
        scratch_shapes=[pltpu.VMEM((2, TM_OUT, D_MODEL), BF16)],
        out_shape=[
            jax.ShapeDtypeStruct((m, D_MODEL), F32),
            jax.ShapeDtypeStruct((m, D_MODEL), F32),
            jax.ShapeDtypeStruct((m // TM_OUT, ROUTE_ROWS, TM_OUT), F32),
            jax.ShapeDtypeStruct((m // TM_OUT, 8, LANES), jnp.int32),
            jax.ShapeDtypeStruct((8, LANES), F32),
        ],
        compiler_params=pltpu.CompilerParams(
            dimension_semantics=("arbitrary",), vmem_limit_bytes=VMEM_LIMIT),
        name="outproj",
    )(acts_p, acts_s, z, xa, xb, *weights)


PLAN_GROUP, PLAN_USED, PLAN_END, PLAN_BLOCKS, PLAN_BASE = 0, 1, 2, 3, 4


def _plan_kernel(cnt_ref, plan_ref):
    sub_i = lax.broadcasted_iota(jnp.int32, (8, LANES), 0)
    lane8_i = lax.broadcasted_iota(jnp.int32, (8, LANES), 1)
    cnt = jnp.sum(jnp.where(sub_i == lane8_i, cnt_ref[...], 0.0), axis=0, keepdims=True)
    blocks = jnp.floor((cnt + (MOE_BLK - 1)) * (1.0 / MOE_BLK))
    start = _lane_prefix(blocks)
    end = start + blocks
    lane_i = lax.broadcasted_iota(jnp.int32, (1, LANES), 1)
    lane = lane_i.astype(F32)
    grp_of_blk = jnp.zeros((1, LANES), F32)
    for g in range(MOE_GROUPS):
        end_g = jnp.sum(jnp.where(lane_i == g, end, 0.0), axis=-1, keepdims=True)
        grp_of_blk += jnp.where(lane >= end_g, 1.0, 0.0)
    grp_of_blk = jnp.minimum(grp_of_blk, MOE_GROUPS - 1)
    n_used = jnp.sum(blocks, axis=-1, keepdims=True)
    row = lax.broadcasted_iota(jnp.int32, (8, LANES), 0)
    plan_ref[...] = jnp.where(
        row == PLAN_GROUP, grp_of_blk,
        jnp.where(row == PLAN_USED, n_used,
                  jnp.where(row == PLAN_END, end,
                            jnp.where(row == PLAN_BLOCKS, blocks,
                                      jnp.where(row == PLAN_BASE, start * MOE_BLK, 0.0))))
    ).astype(jnp.int32)


def _plan(cnt):
    return pl.pallas_call(
        _plan_kernel,
        out_shape=jax.ShapeDtypeStruct((8, LANES), jnp.int32),
        name="plan",
    )(cnt)


SEG_START, SEG_LEN = 0, 1


def _segment_copies(seg_ref, base_ref, make_copy):
    local = 0
    for g in range(MOE_GROUPS):
        n = seg_ref[g, SEG_LEN]
        first = base_ref[g] + seg_ref[g, SEG_START]
        k = TM_ROW
        while k >= 1:
            done = n & ~(2 * k - 1)
            @pl.when((n & k) != 0)
            def _():
                make_copy(local + done, first + done, k).start()
            k //= 2
        local = local + n


def _perm_matrix(route_ref):
    row = lax.broadcasted_iota(jnp.int32, (TM_ROW, TM_ROW), 0).astype(F32)
    return jnp.where(row == route_ref[ROUTE_LRANK:ROUTE_LRANK + 1, :], 1.0, 0.0).astype(BF16)


SUB = 8
SUB_X = D_MODEL // 2 // LANES
SUB_GATE = SUB_X
HI_MASK = -65536


def _rows(first_row, n_rows):
    return pl.ds(pl.multiple_of(first_row * SUB, SUB), n_rows * SUB)


def _sublane(s, n_rows):
    return pl.ds(s, n_rows, stride=SUB)


def _scatter_kernel(nu_ref, end_ref, nb_ref, base_ref, seg_ref, route_ref, xn_ref, xs_ref,
                    zbuf, sbuf, sems):
    i = pl.program_id(0)
    n = pl.num_programs(0)
    sem = sems.at[0]

    def zero_block(b):
        return pltpu.make_async_copy(zbuf, xs_ref.at[_rows(b * MOE_BLK, MOE_BLK)], sem)

    def each_unfilled_block(fn):
        for g in range(MOE_GROUPS):
            @pl.when(nb_ref[g] > 0)
            def _():
                fn(zero_block(end_ref[g] - 1))
        for b in range(T_ALL // MOE_BLK, N_BLK):
            @pl.when(b >= nu_ref[0])
            def _():
                fn(zero_block(b))

    @pl.when(i == 0)
    def _():
        zbuf[...] = jnp.zeros_like(zbuf)
        sbuf[...] = jnp.zeros_like(sbuf)
        each_unfilled_block(lambda c: c.start())
        each_unfilled_block(lambda c: c.wait())

    def tile_done(s):
        pltpu.make_async_copy(sbuf.at[s], xs_ref.at[_rows(0, TM_ROW)], sems.at[s]).wait()

    for t in range(ROW_TILES):
        slot = (i % 2) * ROW_TILES + t

        @pl.when(i >= 2)
        def _():
            tile_done(slot)

        route = route_ref.at[t]
        perm = _perm_matrix(route)
        xn = xn_ref[t * TM_ROW:(t + 1) * TM_ROW, :]
        xs = lax.bitcast_convert_type(_dot(perm, xn.astype(BF16)), jnp.int32)
        half = D_MODEL // 2
        for s in range(SUB_X):
            hi = xs[:, s * LANES:(s + 1) * LANES] & HI_MASK
            lo = lax.shift_right_logical(xs[:, half + s * LANES:half + (s + 1) * LANES], 16)
            sbuf[slot, _sublane(s, TM_ROW), :] = hi | lo
        gate_t = route[0:EXPERTS_PER_GROUP, :]
        gate_t = jnp.concatenate(
            [gate_t, jnp.zeros((LANES - EXPERTS_PER_GROUP, TM_ROW), F32)], axis=0)
        g1 = gate_t.astype(BF16)
        r1 = gate_t - g1.astype(F32)
        g2 = r1.astype(BF16)
        g3 = (r1 - g2.astype(F32)).astype(BF16)
        sbuf[slot, _sublane(SUB_GATE, TM_ROW), :] = lax.bitcast_convert_type(
            _dot_nt(perm, g1) + _dot_nt(perm, g2) + _dot_nt(perm, g3), jnp.int32)

        def make_copy(src_row, dst_row, k, slot=slot):
            return pltpu.make_async_copy(sbuf.at[slot, _rows(src_row, k)],
                                         xs_ref.at[_rows(dst_row, k)], sems.at[slot])
        _segment_copies(seg_ref.at[t], base_ref, make_copy)

    @pl.when(i == n - 1)
    def _():
        for t in range(ROW_TILES):
            tile_done((i % 2) * ROW_TILES + t)

            @pl.when(n > 1)
            def _():
                tile_done((1 - i % 2) * ROW_TILES + t)


def _scatter(seg, route, xn, n_used, grp_end, grp_blocks, grp_base):
    m = xn.shape[0]
    step_rows = ROW_TILES * TM_ROW
    grid_spec = pltpu.PrefetchScalarGridSpec(
        num_scalar_prefetch=4,
        grid=(m // step_rows,),
        in_specs=[
            pl.BlockSpec((ROW_TILES, 8, LANES), lambda i, *_: (i, 0, 0), memory_space=pltpu.SMEM),
            pl.BlockSpec((ROW_TILES, ROUTE_ROWS, TM_ROW), lambda i, *_: (i, 0, 0)),
            pl.BlockSpec((step_rows, D_MODEL), lambda i, *_: (i, 0)),
        ],
        out_specs=pl.BlockSpec(memory_space=pl.ANY),
        scratch_shapes=[pltpu.VMEM((MOE_BLK * SUB, LANES), jnp.int32),
                        pltpu.VMEM((2 * ROW_TILES, TM_ROW * SUB, LANES), jnp.int32),
                        pltpu.SemaphoreType.DMA((2 * ROW_TILES,))],
    )
    return pl.pallas_call(
        _scatter_kernel,
        grid_spec=grid_spec,
        out_shape=jax.ShapeDtypeStruct((N_SORTED * SUB, LANES), jnp.int32),
        compiler_params=pltpu.CompilerParams(
            dimension_semantics=("arbitrary",), vmem_limit_bytes=VMEM_LIMIT),
        name="scatter",
    )(n_used, grp_end, grp_blocks, grp_base, seg, route, xn)


def _ffn_kernel(bg_ref, nu_ref, xs_ref, w1_ref, w3_ref, w2_ref, y_ref):
    del bg_ref
    b = pl.program_id(0)

    def block_inputs():
        packed = [xs_ref[_sublane(s, MOE_BLK), :] for s in range(SUB_X)]
        x = jnp.concatenate(
            [lax.bitcast_convert_type(u & HI_MASK, F32).astype(BF16) for u in packed]
            + [lax.bitcast_convert_type(lax.shift_left(u, 16), F32).astype(BF16) for u in packed],
            axis=1)
        return x, lax.bitcast_convert_type(xs_ref[_sublane(SUB_GATE, MOE_BLK), :], F32)

    def store(y):
        for s in range(SUB):
            y_ref[_sublane(s, MOE_BLK), :] = y[:, s * LANES:(s + 1) * LANES]

    live = b < nu_ref[0]
    gates_t = lax.bitcast_convert_type(xs_ref[_sublane(SUB_GATE, MOE_BLK), :], F32).T
    chosen_t = jnp.where(gates_t[0:EXPERTS_PER_GROUP] > 0.0, 1.0, 0.0)
    most = jnp.max(jnp.sum(chosen_t, axis=1, keepdims=True))
    fits = most <= float(FFN_CAP)

    @pl.when(live & fits)
    def _():
        x, _ = block_inputs()
        n_e = EXPERTS_PER_GROUP
        rr = lax.broadcasted_iota(jnp.int32, (MOE_BLK, MOE_BLK), 0)
        cc = lax.broadcasted_iota(jnp.int32, (MOE_BLK, MOE_BLK), 1)
        earlier = jnp.where(rr < cc, 1.0, 0.0).astype(BF16)
        padded = jnp.concatenate([chosen_t, jnp.zeros_like(chosen_t)], axis=0).astype(BF16)
        rank_t = _dot(padded, earlier)[0:n_e]
        slot = lax.broadcasted_iota(jnp.int32, (FFN_CAP, MOE_BLK), 0).astype(F32)
        hit = [(slot == rank_t[e:e + 1]) & (chosen_t[e:e + 1] > 0.0) for e in range(n_e)]
        take = jnp.concatenate([jnp.where(h, 1.0, 0.0).astype(BF16) for h in hit], axis=0)
        xe = _dot(take, x).astype(BF16)
        rows = [xe[e * FFN_CAP:(e + 1) * FFN_CAP] for e in range(n_e)]
        h1 = [_dot(rows[e], w1_ref[e]) for e in range(n_e)]
        h3 = [_dot(rows[e], w3_ref[e]) for e in range(n_e)]
        give = [jnp.where(hit[e], gates_t[e:e + 1], 0.0).astype(BF16) for e in range(n_e)]
        hid = [(h1[e] * _sigmoid(h1[e]) * h3[e]).astype(BF16) for e in range(n_e)]
        ys = [_dot(hid[e], w2_ref[e]).astype(BF16) for e in range(n_e)]
        store(lax.dot_general(jnp.concatenate(give, axis=0), jnp.concatenate(ys, axis=0),
                              (((0,), (0,)), ((), ())), preferred_element_type=F32))

    @pl.when(live & jnp.logical_not(fits))
    def _():
        x, gates = block_inputs()
        hs = []
        for e in range(EXPERTS_PER_GROUP):
            h1 = _dot(x, w1_ref[e])
            h3 = _dot(x, w3_ref[e])
            ge = gates[:, e:e + 1]
            hs.append(jnp.where(ge > 0.0, h1 * _sigmoid(h1) * h3 * ge, 0.0).astype(BF16))
        hcat = jnp.concatenate(hs, axis=1)
        store(_dot(hcat, w2_ref[...].reshape(EXPERTS_PER_GROUP * D_EXPERT, D_MODEL)))

    @pl.when(b >= nu_ref[0])
    def _():
        y_ref[...] = jnp.zeros_like(y_ref)


def _ffn(blk_group, n_used, xs, w1, w3, w2):
    grouped = (MOE_GROUPS, EXPERTS_PER_GROUP)
    w1, w3, w2 = (w.reshape(grouped + w.shape[1:]) for w in (w1, w3, w2))
    wmap = lambda b, bg, nu: (bg[b], 0, 0, 0)
    grid_spec = pltpu.PrefetchScalarGridSpec(
        num_scalar_prefetch=2,
        grid=(N_BLK,),
        in_specs=[
            pl.BlockSpec((MOE_BLK * SUB, LANES), lambda b, bg, nu: (b, 0)),
            pl.BlockSpec((None, EXPERTS_PER_GROUP, D_MODEL, D_EXPERT), wmap),
            pl.BlockSpec((None, EXPERTS_PER_GROUP, D_MODEL, D_EXPERT), wmap),
            pl.BlockSpec((None, EXPERTS_PER_GROUP, D_EXPERT, D_MODEL), wmap),
        ],
        out_specs=pl.BlockSpec((MOE_BLK * SUB, LANES), lambda b, bg, nu: (b, 0)),
    )
    return pl.pallas_call(
        _ffn_kernel,
        grid_spec=grid_spec,
        out_shape=jax.ShapeDtypeStruct((N_SORTED * SUB, LANES), F32),
        compiler_params=pltpu.CompilerParams(
            dimension_semantics=("arbitrary",), vmem_limit_bytes=VMEM_LIMIT),
        name="ffn",
    )(blk_group, n_used, xs, w1, w3, w2)


def _combine_kernel(base_ref, seg_ref, segn_ref, route_ref, ys_hbm, x_ref, g_ref, *refs, final):
    outs, (buf, sem) = refs[:-2], refs[-2:]
    i = pl.program_id(0)
    n = pl.num_programs(0)
    step_rows = ROW_TILES * TM_ROW

    def gather(seg, parity):
        for t in range(ROW_TILES):
            s = parity * ROW_TILES + t

            def make_copy(buf_row, ys_row, k, s=s):
                return pltpu.make_async_copy(ys_hbm.at[_rows(ys_row, k)],
                                             buf.at[s, _rows(buf_row, k)], sem.at[s])
            _segment_copies(seg.at[t], base_ref, make_copy)

    @pl.when(i == 0)
    def _():
        gather(seg_ref, 0)

    @pl.when(i + 1 < n)
    def _():
        gather(segn_ref, 1 - i % 2)

    tn = (((0,), (0,)), ((), ()))
    for t in range(ROW_TILES):
        slot = (i % 2) * ROW_TILES + t
        rows = slice(t * TM_ROW, (t + 1) * TM_ROW)
        pltpu.make_async_copy(ys_hbm.at[_rows(0, TM_ROW)], buf.at[slot], sem.at[slot]).wait()

        perm = _perm_matrix(route_ref.at[t])
        cols = []
        for s in range(SUB):
            hi, lo = _split_bf16(buf[slot, _sublane(s, TM_ROW), :])
            cols.append(lax.dot_general(perm, hi, tn, preferred_element_type=F32)
                        + lax.dot_general(perm, lo, tn, preferred_element_type=F32))
        y = x_ref[rows, :] + jnp.concatenate(cols, axis=1)
        if not final:
            outs[0][rows, :] = y
        else:
            y = _rms(y, g_ref[...])

            @pl.when(i < T_PROMPT // step_rows)
            def _():
                outs[0][rows, :] = y

            @pl.when(i >= T_PROMPT // step_rows)
            def _():
                outs[1][rows, :] = y


def _combine(seg, route, grp_base, ys, x, g, final):
    m = x.shape[0]
    step_rows = ROW_TILES * TM_ROW
    nt = m // step_rows
    n_p = T_PROMPT // step_rows
    smem = functools.partial(pl.BlockSpec, (ROW_TILES, 8, LANES), memory_space=pltpu.SMEM)
    tile = (step_rows, D_MODEL)
    if final:
        out_specs = [pl.BlockSpec(tile, lambda i, *_: (jnp.minimum(i, n_p - 1), 0)),
                     pl.BlockSpec(tile, lambda i, *_: (jnp.maximum(i - n_p, 0), 0))]
        out_shape = [jax.ShapeDtypeStruct((T_PROMPT, D_MODEL), F32),
                     jax.ShapeDtypeStruct((T_SAMPLE, D_MODEL), F32)]
    else:
        out_specs = [pl.BlockSpec(tile, lambda i, *_: (i, 0))]
        out_shape = [jax.ShapeDtypeStruct((m, D_MODEL), F32)]
    grid_spec = pltpu.PrefetchScalarGridSpec(
        num_scalar_prefetch=1,
        grid=(nt,),
        in_specs=[
            smem(lambda i, *_: (i, 0, 0)),
            smem(lambda i, *_: (jnp.minimum(i + 1, nt - 1), 0, 0)),
            pl.BlockSpec((ROW_TILES, ROUTE_ROWS, TM_ROW), lambda i, *_: (i, 0, 0)),
            pl.BlockSpec(memory_space=pl.ANY),
            pl.BlockSpec(tile, lambda i, *_: (i, 0)),
            pl.BlockSpec(g.shape, lambda i, *_: (0, 0)),
        ],
        out_specs=out_specs,
        scratch_shapes=[pltpu.VMEM((2 * ROW_TILES, TM_ROW * SUB, LANES), F32),
                        pltpu.SemaphoreType.DMA((2 * ROW_TILES,))],
    )
    return pl.pallas_call(
        functools.partial(_combine_kernel, final=final),
        grid_spec=grid_spec,
        out_shape=out_shape,
        compiler_params=pltpu.CompilerParams(
            dimension_semantics=("arbitrary",), vmem_limit_bytes=VMEM_LIMIT),
        name="combine",
    )(grp_base, seg, seg, route, ys, x, g)


def _prep_weights(w_in, gla_a2, cm_ws, cm_b, router_group_w, router_group_b,
                  router_expert_w, router_expert_b):
    off = {}
    o = 0
    for name, n in (("h", 512), ("cg", 512), ("bg", 512), ("q", 512), ("k", 512), ("v", 1024),
                    ("r", 1024), ("alr", 16), ("u", 512), ("vv", 512), ("ga", 1024),
                    ("gb", 1024), ("gc", 1024)):
        off[name] = (o, n)
        o += n
    runs = (("ga", "gc"), ("v", "r"), ("q", "k"), ("h", "cg"), ("bg", "bg"), ("u", "vv"))
    w_t = jnp.swapaxes(w_in, 1, 2)
    w_z = jnp.concatenate([w_t[:, off[a][0]:off[b][0] + off[b][1]] for a, b in runs],
                          axis=1).astype(BF16)
    assert w_z.shape[1] == Z_COLS
    a0 = off["alr"][0]
    w_alr = jnp.pad(w_t[:, a0:a0 + GLA_LOWRANK],
                    ((0, 0), (0, LANES - GLA_LOWRANK), (0, 0))).astype(BF16)
    a2 = jnp.pad(gla_a2, ((0, 0), (0, LANES - GLA_LOWRANK), (0, 0))).astype(BF16)
    ws_p = jnp.tril(cm_ws).astype(BF16)
    small = jnp.tril(cm_ws[:, :, :DEC_SEQ, :DEC_SEQ])
    eye = jnp.eye(SEQ_PER_BLK, dtype=F32)
    ws_s = jnp.einsum("ij,lgab->lgiajb", eye, small).reshape(
        DEPTH, CM_GROUPS, ROWS_S, ROWS_S).astype(BF16)
    cmb_p = jnp.broadcast_to(jnp.transpose(cm_b, (0, 2, 1))[:, :, :, None],
                             (DEPTH, CM_CHUNK, CM_GROUPS, CM_GCH)).reshape(DEPTH, CM_CHUNK, CM_CH)
    cmb_s = jnp.tile(cmb_p[:, :DEC_SEQ], (1, SEQ_PER_BLK, 1))
    pad = LANES - MOE_GROUPS - N_EXPERTS
    w_r = jnp.pad(jnp.swapaxes(jnp.concatenate([router_group_w, router_expert_w], axis=-1), 1, 2),
                  ((0, 0), (0, pad), (0, 0)))
    w_r_hi = w_r.astype(BF16)
    w_r_lo = (w_r - w_r_hi.astype(F32)).astype(BF16)
    r_b = jnp.pad(jnp.concatenate([router_group_b, router_expert_b], axis=-1),
                  ((0, 0), (0, pad)))[:, :, None]
    return w_z, w_alr, a2, ws_p, ws_s, cmb_p, cmb_s, w_r_hi, w_r_lo, r_b


def kernel(x_prompt, x_sample, state_conv, state_gla, norm1_g, w_in, conv_w, gla_a2, gla_a_b,
           gla_norm_g, cm_norm_g, cm_ws, cm_b, proj_a, proj_b, proj_c, w_out, norm2_g,
           router_group_w, router_group_b, router_expert_w, router_expert_b,
           exp_w1, exp_w3, exp_w2, final_norm_g):
    (w_z, w_alr, a2, ws_p, ws_s, cmb_p, cmb_s, w_r_hi, w_r_lo, r_b) = _prep_weights(
        w_in, gla_a2, cm_ws, cm_b, router_group_w, router_group_b, router_expert_w,
        router_expert_b)
    pa, pb, pc, wo = (w.astype(BF16) for w in (proj_a, proj_b, proj_c, w_out))
    n1 = norm1_g[:, None, :]
    n2 = norm2_g[:, None, :]
    ab = gla_a_b[:, None, :]
    gng = gla_norm_g[:, None, :]
    cmg = cm_norm_g.reshape(DEPTH, 1, CM_CH)
    fg = final_norm_g[None, :]
    xa = x_prompt.reshape(T_PROMPT, D_MODEL)
    xb = x_sample.reshape(T_SAMPLE, D_MODEL)
    gla_s = None
    conv_p, gla_p, conv_s, cmv_s = [], [], [], []
    for l in range(DEPTH):
        z, alr, w1, w3, w2 = _inproj(xa, xb, n1, w_z, w_alr, exp_w1, exp_w3, exp_w2, l)
        acts_p, nconv, ngla = _mix_prompt(z, alr, a2, ab, conv_w, gng, cmg, ws_p, cmb_p, l)
        sc = state_conv[l]
        p2 = jnp.pad(sc, ((0, 0), (0, DEC_SEQ - 2), (0, 0))).reshape(T_SAMPLE, CONV_CH)
        p1 = jnp.pad(sc[:, 1:2], ((0, 0), (0, DEC_SEQ - 1), (0, 0))).reshape(T_SAMPLE, CONV_CH)
        acts_s, cin_s, gla_s, vrows = _mix_sample(z, alr, a2, ab, conv_w, gng, cmg, ws_s, cmb_s,
                                                  p1, p2, state_gla, gla_s, l)
        conv_p.append(nconv)
        gla_p.append(ngla)
        conv_s.append(cin_s.reshape(DEC_BATCH, DEC_SEQ, CONV_CH)[:, DEC_SEQ - (CONV_K - 1):])
        cmv_s.append(vrows.reshape(DEC_BATCH, DEC_SEQ, CM_CH))

        x, xn, route, seg, cnt = _outproj(acts_p, acts_s, z, xa, xb, pa, pb, pc, wo, n2,
                                          w_r_hi, w_r_lo, r_b, l)
        plan = _plan(cnt)
        n_used = plan[PLAN_USED, :1]
        grp_base = plan[PLAN_BASE, :MOE_GROUPS]
        xs = _scatter(seg, route, xn, n_used, plan[PLAN_END, :MOE_GROUPS],
                      plan[PLAN_BLOCKS, :MOE_GROUPS], grp_base)
        ys = _ffn(plan[PLAN_GROUP, :N_BLK], n_used, xs, w1, w3, w2)
        out = _combine(seg, route, grp_base, ys, x, fg, l == DEPTH - 1)
        xa = xb = out[0]

    y_prompt = out[0].reshape(BATCH, SEQ, D_MODEL)
    y_sample = out[1].reshape(DEC_BATCH, DEC_SEQ, D_MODEL)
    return (y_prompt, y_sample, jnp.stack(conv_p), jnp.stack(gla_p), jnp.stack(conv_s),
            gla_s, jnp.stack(cmv_s))
```

```python
import functools

import jax
import jax.numpy as jnp
from jax import lax
from jax.experimental import pallas as pl
from jax.experimental.pallas import tpu as pltpu

F32 = jnp.float32
BF16 = jnp.bfloat16

D_MODEL = 1024
BATCH = 8
SEQ = 2048
DEPTH = 2
DEC_BATCH = 128
DEC_SEQ = 8
CONV_K = 3
CONV_CH = 512
GLA_HEADS = 4
GLA_DK = 128
GLA_DV = 256
GLA_QK = GLA_HEADS * GLA_DK
GLA_V = GLA_HEADS * GLA_DV
GLA_LOWRANK = 16
GLA_TAU = 16.0
GLA_CHUNK = 64
CM_GROUPS = 4
CM_CHUNK = 128
CM_GCH = 128
CM_CH = 512
MOE_GROUPS = 8
EXPERTS_PER_GROUP = 8
N_EXPERTS = 64
D_EXPERT = 256
EPS = 1e-6

LANES = 128
T_PROMPT = BATCH * SEQ
T_SAMPLE = DEC_BATCH * DEC_SEQ
T_ALL = T_PROMPT + T_SAMPLE

COL_GA, COL_GB, COL_GC = 0, 1024, 2048
COL_V, COL_R, COL_Q, COL_K = 3072, 4096, 5120, 5632
COL_H, COL_CG, COL_BG, COL_U, COL_VV = 6144, 6656, 7168, 7680, 8192
Z_COLS = 8704
ACT_COLS = 2048
ROUTE_ROWS = 2 * EXPERTS_PER_GROUP
ROUTE_LRANK = EXPERTS_PER_GROUP

TM_IN = 1024
TN_IN = Z_COLS // 4
TC_MIX = 256
SEQ_PER_BLK = 16
ROWS_S = SEQ_PER_BLK * DEC_SEQ
TM_OUT = 256
ROUTE_DELAY = 1
OUT_PARTS = 1
OUT_STAGGER = 1
TM_ROW = TM_OUT
ROW_TILES = 4
MOE_BLK = 256
N_BLK = T_ALL // MOE_BLK + MOE_GROUPS
N_SORTED = N_BLK * MOE_BLK
VMEM_LIMIT = 56 * 1024 * 1024


def _sigmoid(x):
    return 0.5 * jnp.tanh(0.5 * x) + 0.5


def _gelu_tanh(x):
    c = 0.7978845608028654
    half = 0.5 * x
    return half + half * jnp.tanh(x * (c + (c * 0.044715) * (x * x)))


def _log_sigmoid(x):
    log2_e = 1.4426950408889634
    ln_2 = 0.6931471805599453
    return jnp.minimum(x, 0.0) - ln_2 * jnp.log2(1.0 + jnp.exp2(-log2_e * jnp.abs(x)))


def _rms(x, g):
    ms = jnp.mean(x * x, axis=-1, keepdims=True)
    return x * lax.rsqrt(ms + EPS) * g


def _split_bf16(x):
    hi = x.astype(BF16)
    lo = (x - hi.astype(F32)).astype(BF16)
    return hi, lo


def _dot(a, b):
    return jnp.dot(a, b, preferred_element_type=F32)


def _dot_nt(a, b):
    return lax.dot_general(a, b, (((1,), (1,)), ((), ())), preferred_element_type=F32)


def _layer_spec(arr, layer):
    nd = arr.ndim - 1
    return pl.BlockSpec((None,) + arr.shape[1:], lambda *g: (layer,) + (0,) * nd)


def _const_spec(arr):
    nd = arr.ndim
    return pl.BlockSpec(arr.shape, lambda *g: (0,) * nd)


def _two_part_specs(tile, xa, xb):
    n_a = xa.shape[0] // tile
    n_b = xb.shape[0] // tile
    return n_a, [
        pl.BlockSpec((tile, xa.shape[1]), lambda i, *_: (jnp.minimum(i, n_a - 1), 0)),
        pl.BlockSpec((tile, xb.shape[1]), lambda i, *_: (jnp.clip(i - n_a, 0, n_b - 1), 0)),
    ]


def _inproj_kernel(xa_ref, xb_ref, g_ref, w_ref, wa_ref, e1_ref, e3_ref, e2_ref,
                   z_ref, a_ref, o1_ref, o3_ref, o2_ref, xn_ref, *, n_a):
    o1_ref[...] = e1_ref[...].astype(BF16)
    o3_ref[...] = e3_ref[...].astype(BF16)
    o2_ref[...] = e2_ref[...].astype(BF16)

    @pl.when(pl.program_id(1) == 0)
    def _():
        x = jnp.where(pl.program_id(0) < n_a, xa_ref[...], xb_ref[...])
        xn = _rms(x, g_ref[...]).astype(BF16)
        xn_ref[...] = xn
        a_ref[...] = _dot_nt(xn, wa_ref[...]).astype(BF16)

    z_ref[...] = _dot_nt(xn_ref[...], w_ref[...]).astype(BF16)


def _inproj(xa, xb, g, w, wa, e1, e3, e2, layer):
    m = T_ALL
    n_i, n_j = m // TM_IN, Z_COLS // TN_IN
    assert n_i * n_j >= N_EXPERTS
    n_a, x_specs = _two_part_specs(TM_IN, xa, xb)
    expert = lambda i, j: jnp.minimum(i * n_j + j, N_EXPERTS - 1)
    up, down = (D_MODEL, D_EXPERT), (D_EXPERT, D_MODEL)
    return pl.pallas_call(
        functools.partial(_inproj_kernel, n_a=n_a),
        grid=(n_i, n_j),
        in_specs=x_specs + [
            _layer_spec(g, layer),
            pl.BlockSpec((None, TN_IN, D_MODEL), lambda i, j: (layer, j, 0)),
            _layer_spec(wa, layer),
            pl.BlockSpec((None, None) + up, lambda i, j: (layer, expert(i, j), 0, 0)),
            pl.BlockSpec((None, None) + up, lambda i, j: (layer, expert(i, j), 0, 0)),
            pl.BlockSpec((None, None) + down, lambda i, j: (layer, expert(i, j), 0, 0)),
        ],
        out_specs=[
            pl.BlockSpec((TM_IN, TN_IN), lambda i, j: (i, j)),
            pl.BlockSpec((TM_IN, LANES), lambda i, j: (i, 0)),
            pl.BlockSpec((None,) + up, lambda i, j: (expert(i, j), 0, 0)),
            pl.BlockSpec((None,) + up, lambda i, j: (expert(i, j), 0, 0)),
            pl.BlockSpec((None,) + down, lambda i, j: (expert(i, j), 0, 0)),
        ],
        out_shape=[
            jax.ShapeDtypeStruct((m, Z_COLS), BF16),
            jax.ShapeDtypeStruct((m, LANES), BF16),
            jax.ShapeDtypeStruct((N_EXPERTS,) + up, BF16),
            jax.ShapeDtypeStruct((N_EXPERTS,) + up, BF16),
            jax.ShapeDtypeStruct((N_EXPERTS,) + down, BF16),
        ],
        scratch_shapes=[pltpu.VMEM((TM_IN, D_MODEL), BF16)],
        compiler_params=pltpu.CompilerParams(
            dimension_semantics=("arbitrary", "arbitrary"), vmem_limit_bytes=VMEM_LIMIT),
        name="inproj",
    )(xa, xb, g, w, wa, e1, e3, e2)


def _gla_log_decay(alr_ref, a2_ref, ab_ref):
    la = _log_sigmoid(_dot(alr_ref[...], a2_ref[...]) + ab_ref[...]) * (1.0 / GLA_TAU)
    return _split_bf16(la)


def _masked_sum(mask, la_hi, la_lo):
    m = jnp.where(mask, 1.0, 0.0).astype(BF16)
    return _dot(m, la_hi) + _dot(m, la_lo)


def _gla_decay_prefix(alr_ref, a2_ref, ab_ref, tril_mask):
    la_hi, la_lo = _gla_log_decay(alr_ref, a2_ref, ab_ref)
    return _masked_sum(tril_mask, la_hi, la_lo)


def _gla_decay_terms(alr_ref, a2_ref, ab_ref, tril_mask, same_mask):
    la_hi, la_lo = _gla_log_decay(alr_ref, a2_ref, ab_ref)
    return (la_hi, la_lo, _masked_sum(tril_mask, la_hi, la_lo),
            _masked_sum(same_mask, la_hi, la_lo))


def _gla_out_gate(o, g_ref, r):
    return _rms(o, g_ref[...]) * (r * _sigmoid(r))


def _chunk_mlp_group(g, bgu_ref, vv_ref, cmg_ref, ws_ref, cmb_ref, n_chunks):
    sl = slice(g * CM_GCH, (g + 1) * CM_GCH)
    ug = _gelu_tanh(bgu_ref[:, CONV_CH + g * CM_GCH:CONV_CH + (g + 1) * CM_GCH])
    vg = _rms(_gelu_tanh(vv_ref[:, sl]).astype(F32), cmg_ref[:, sl])
    vgb = vg.astype(BF16)
    rows = []
    for j in range(n_chunks):
        rs = slice(j * CM_CHUNK, (j + 1) * CM_CHUNK)
        rows.append(_dot(ws_ref[g], vgb[rs]) + cmb_ref[:, sl])
    s = rows[0] if n_chunks == 1 else jnp.concatenate(rows, axis=0)
    return ug * s, vg


def _chunk_mlp(bgu_ref, vv_ref, cmg_ref, ws_ref, cmb_ref, n_chunks):
    parts = [_chunk_mlp_group(g, bgu_ref, vv_ref, cmg_ref, ws_ref, cmb_ref, n_chunks)
             for g in range(CM_GROUPS)]
    return (jnp.concatenate([p[0] for p in parts], axis=1),
            jnp.concatenate([p[1] for p in parts], axis=1))


N_Z_VIEWS = 7
N_MIX_W = 7
SEQ_PER_STEP = 4
SEQ_STAGGER = 3


def _mix_prompt_kernel(*refs):
    n_z = N_Z_VIEWS * SEQ_PER_STEP
    weights = refs[n_z:n_z + N_MIX_W]
    acts_ref, nconv_ref, ngla_ref, st_ref, carry_ref = refs[n_z + N_MIX_W:]

    @pl.when(pl.program_id(1) == 0)
    def _():
        st_ref[...] = jnp.zeros_like(st_ref)
        carry_ref[...] = jnp.zeros_like(carry_ref)

    _interleave(*[
        _delayed(_mix_prompt_seq(*refs[N_Z_VIEWS * s:N_Z_VIEWS * (s + 1)], *weights,
                                acts_ref.at[s], nconv_ref.at[s], st_ref.at[s], carry_ref.at[s]),
                s * SEQ_STAGGER)
        for s in range(SEQ_PER_STEP)])

    @pl.when(pl.program_id(1) == pl.num_programs(1) - 1)
    def _():
        for s in range(SEQ_PER_STEP):
            for hd in range(GLA_HEADS):
                ngla_ref[s, 0, hd] = st_ref[s, hd].T


def _mix_prompt_seq(v_ref, r_ref, qk_ref, hcg_ref, bgu_ref, vv_ref, alr_ref,
                    a2_ref, ab_ref, cw_ref, gng_ref, cmg_ref, ws_ref, cmb_ref,
                    acts_ref, nconv_ref, st_ref, carry_ref):
    tc = TC_MIX

    h = hcg_ref[:, :CONV_CH].astype(F32)
    cg = hcg_ref[:, CONV_CH:].astype(F32)
    bg = bgu_ref[:, :CONV_CH].astype(F32)
    cin = cg * h
    rr = lax.broadcasted_iota(jnp.int32, (tc, tc), 0)
    cc = lax.broadcasted_iota(jnp.int32, (tc, tc), 1)
    cin_b = cin.astype(BF16)
    x1 = _dot(jnp.where(rr - cc == 1, 1.0, 0.0).astype(BF16), cin_b)
    x2 = _dot(jnp.where(rr - cc == 2, 1.0, 0.0).astype(BF16), cin_b)
    conv = x2 * cw_ref[0:1, :] + x1 * cw_ref[1:2, :] + cin * cw_ref[2:3, :]
    c0 = carry_ref[0:1, :]
    c1 = carry_ref[1:2, :]
    row8 = lax.broadcasted_iota(jnp.int32, (8, 1), 0)
    head = jnp.where(row8 == 0, c0 * cw_ref[0:1, :] + c1 * cw_ref[1:2, :],
                     jnp.where(row8 == 1, c1 * cw_ref[0:1, :], 0.0))
    conv = jnp.concatenate([conv[0:8] + head, conv[8:]], axis=0)
    acts_ref[:, 0:CONV_CH] = (bg * conv).astype(BF16)
    carry_ref[0:2, :] = cin[tc - 2:tc, :]
    nconv_ref[0] = cin[tc - 2:tc, :]
    yield

    rr = lax.broadcasted_iota(jnp.int32, (tc, tc), 0)
    cc = lax.broadcasted_iota(jnp.int32, (tc, tc), 1)
    same = (rr >> 6) == (cc >> 6)
    tril = same & (cc <= rr)
    b = _gla_decay_prefix(alr_ref, a2_ref, ab_ref, tril)
    yield
    n_chunks = tc // GLA_CHUNK
    b_last = [b[(c + 1) * GLA_CHUNK - 1:(c + 1) * GLA_CHUNK, :] for c in range(n_chunks)]
    bl = jnp.concatenate([jnp.broadcast_to(r_, (GLA_CHUNK, GLA_QK)) for r_ in b_last], axis=0)
    q = qk_ref[:, :GLA_QK].astype(F32) * (GLA_DK ** -0.5)
    k = qk_ref[:, GLA_QK:].astype(F32)
    q_t = (q * jnp.exp(b)).astype(BF16)
    k_t = (k * jnp.exp(-b)).astype(BF16)
    k_end = (k * jnp.exp(bl - b)).astype(BF16)
    yield
    states = [st_ref[hd] for hd in range(GLA_HEADS)]
    k_cols = [slice(hd * GLA_DK, (hd + 1) * GLA_DK) for hd in range(GLA_HEADS)]
    v_cols = [slice(hd * GLA_DV, (hd + 1) * GLA_DV) for hd in range(GLA_HEADS)]
    o_intra = []
    for hd in range(GLA_HEADS):
        att = jnp.where(tril, _dot_nt(q_t[:, k_cols[hd]], k_t[:, k_cols[hd]]), 0.0).astype(BF16)
        o_intra.append(_dot(att, v_ref[:, v_cols[hd]]))
    yield

    o_rows = [[] for _ in range(GLA_HEADS)]
    assert n_chunks == CM_GROUPS
    for c in range(n_chunks):
        rs = slice(c * GLA_CHUNK, (c + 1) * GLA_CHUNK)
        for hd in range(GLA_HEADS):
            ks = k_cols[hd]
            st = states[hd]
            o_rows[hd].append(o_intra[hd][rs] + _dot_nt(q_t[rs, ks], st.astype(BF16)))
            upd = lax.dot_general(v_ref[rs, v_cols[hd]], k_end[rs, ks], (((0,), (0,)), ((), ())),
                                  preferred_element_type=F32)
            states[hd] = jnp.exp(b_last[c][:, ks]) * st + upd
        us, _ = _chunk_mlp_group(c, bgu_ref, vv_ref, cmg_ref, ws_ref, cmb_ref, tc // CM_CHUNK)
        acts_ref[:, CONV_CH + GLA_V + c * CM_GCH:CONV_CH + GLA_V + (c + 1) * CM_GCH] = (
            us.astype(BF16))
        yield

    for hd in range(GLA_HEADS):
        o = jnp.concatenate(o_rows[hd], axis=0)
        r = r_ref[:, v_cols[hd]]
        acts_ref[:, CONV_CH + hd * GLA_DV:CONV_CH + (hd + 1) * GLA_DV] = (
            _gla_out_gate(o, gng_ref, r).astype(BF16))
        st_ref[hd] = states[hd]
        if hd % 2 == 1:
            yield


def _z_specs(rows, row_map):
    def spec(width, col):
        blk = col // width
        return pl.BlockSpec((rows, width), lambda *g: (row_map(*g), blk))
    return [spec(1024, COL_V), spec(1024, COL_R), spec(1024, COL_Q), spec(1024, COL_H),
            spec(1024, COL_BG), spec(512, COL_VV)]


def _mix_prompt(z, alr, a2, ab, cw, gng, cmg, ws, cmb, layer):
    nt = SEQ // TC_MIX
    nb = BATCH // SEQ_PER_STEP
    small = (a2, ab, cw, gng, cmg, ws, cmb)
    assert len(small) == N_MIX_W
    in_specs, args = [], []
    for s in range(SEQ_PER_STEP):
        row_map = lambda b, c, s=s: (b + s * nb) * nt + c
        in_specs += _z_specs(TC_MIX, row_map) + [
            pl.BlockSpec((TC_MIX, LANES), lambda b, c, row_map=row_map: (row_map(b, c), 0))]
        args += [z] * (N_Z_VIEWS - 1) + [alr]
    in_specs += [_layer_spec(a, layer) for a in small]
    acts, nconv, ngla = pl.pallas_call(
        _mix_prompt_kernel,
        grid=(nb, nt),
        in_specs=in_specs,
        out_specs=[
            pl.BlockSpec((SEQ_PER_STEP, TC_MIX, ACT_COLS), lambda b, c: (0, b * nt + c, 0)),
            pl.BlockSpec((SEQ_PER_STEP, 1, CONV_K - 1, CONV_CH), lambda b, c: (0, b, 0, 0)),
            pl.BlockSpec((SEQ_PER_STEP, 1, GLA_HEADS, GLA_DK, GLA_DV),
                         lambda b, c: (0, b, 0, 0, 0)),
        ],
        out_shape=[
            jax.ShapeDtypeStruct((SEQ_PER_STEP, T_PROMPT // SEQ_PER_STEP, ACT_COLS), BF16),
            jax.ShapeDtypeStruct((SEQ_PER_STEP, nb, CONV_K - 1, CONV_CH), F32),
            jax.ShapeDtypeStruct((SEQ_PER_STEP, nb, GLA_HEADS, GLA_DK, GLA_DV), F32),
        ],
        scratch_shapes=[pltpu.VMEM((SEQ_PER_STEP, GLA_HEADS, GLA_DV, GLA_DK), F32),
                        pltpu.VMEM((SEQ_PER_STEP, 8, CONV_CH), F32)],
        compiler_params=pltpu.CompilerParams(
            dimension_semantics=("arbitrary", "arbitrary"), vmem_limit_bytes=VMEM_LIMIT),
        name="mix_prompt",
    )(*args, *small)
    return (acts.reshape(T_PROMPT, ACT_COLS), nconv.reshape(BATCH, CONV_K - 1, CONV_CH),
            ngla.reshape(BATCH, GLA_HEADS, GLA_DK, GLA_DV))


def _mix_sample_body(v_ref, r_ref, qk_ref, hcg_ref, bgu_ref, vv_ref, alr_ref,
                       a2_ref, ab_ref, cw_ref, gng_ref, cmg_ref, ws_ref, cmb_ref,
                       p1_ref, p2_ref, s0_ref,
                       acts_ref, cin_ref, ns_ref, vrow_ref):
    n = ROWS_S

    pos = lax.broadcasted_iota(jnp.int32, (n, 1), 0) & (DEC_SEQ - 1)
    h = hcg_ref[:, :CONV_CH].astype(F32)
    cg = hcg_ref[:, CONV_CH:].astype(F32)
    bg = bgu_ref[:, :CONV_CH].astype(F32)
    cin = cg * h
    x1 = jnp.where(pos >= 1, pltpu.roll(cin, 1, 0), p1_ref[...])
    x2 = jnp.where(pos >= 2, pltpu.roll(cin, 2, 0), p2_ref[...])
    conv = x2 * cw_ref[0:1, :] + x1 * cw_ref[1:2, :] + cin * cw_ref[2:3, :]
    acts_ref[:, 0:CONV_CH] = (bg * conv).astype(BF16)
    cin_ref[...] = cin

    rr = lax.broadcasted_iota(jnp.int32, (n, n), 0)
    cc = lax.broadcasted_iota(jnp.int32, (n, n), 1)
    same = (rr >> 3) == (cc >> 3)
    tril = same & (cc <= rr)
    la_hi, la_lo, b, bl = _gla_decay_terms(alr_ref, a2_ref, ab_ref, tril, same)
    q = qk_ref[:, :GLA_QK].astype(F32) * (GLA_DK ** -0.5)
    k = qk_ref[:, GLA_QK:].astype(F32)
    q_t = (q * jnp.exp(b)).astype(BF16)
    k_t = (k * jnp.exp(-b)).astype(BF16)
    k_end = k * jnp.exp(bl - b)
    la_hi = la_hi.astype(F32)
    la_lo = la_lo.astype(F32)
    row_seq = lax.broadcasted_iota(jnp.int32, (n, GLA_DK), 0) >> 3
    seq3 = lax.broadcasted_iota(jnp.int32, (SEQ_PER_BLK, GLA_DK, n), 0)
    lane_seq3 = lax.broadcasted_iota(jnp.int32, (SEQ_PER_BLK, GLA_DK, n), 2) >> 3
    mask3 = seq3 == lane_seq3
    ones = jnp.ones((n, GLA_DV), BF16)
    big = SEQ_PER_BLK * GLA_DK

    def per_seq(x_tr):
        x3 = jnp.where(mask3, x_tr[None, :, :], 0.0)
        return x3.reshape(big, n).astype(BF16)

    for hd in range(GLA_HEADS):
        ks = slice(hd * GLA_DK, (hd + 1) * GLA_DK)
        vs = slice(hd * GLA_DV, (hd + 1) * GLA_DV)
        qh = q_t[:, ks]
        vh = v_ref[:, vs]
        att = jnp.where(tril, _dot_nt(qh, k_t[:, ks]), 0.0).astype(BF16)
        o_intra = _dot(att, vh)
        s_old = s0_ref[:, hd].reshape(big, GLA_DV)
        zero = jnp.zeros_like(qh)
        q_big = jnp.concatenate(
            [jnp.where(row_seq == j, qh, zero) for j in range(SEQ_PER_BLK)], axis=1)
        o = o_intra + _dot(q_big, s_old.astype(BF16))
        dlog = _dot(per_seq(la_hi[:, ks].T), ones) + _dot(per_seq(la_lo[:, ks].T), ones)
        upd = _dot(per_seq(k_end[:, ks].T), vh)
        s_new = jnp.exp(dlog) * s_old + upd
        ns_ref[:, hd] = s_new.reshape(SEQ_PER_BLK, GLA_DK, GLA_DV)
        r = r_ref[:, vs]
        acts_ref[:, CONV_CH + hd * GLA_DV:CONV_CH + (hd + 1) * GLA_DV] = (
            _gla_out_gate(o, gng_ref, r).astype(BF16))

    us, vg = _chunk_mlp(bgu_ref, vv_ref, cmg_ref, ws_ref, cmb_ref, 1)
    acts_ref[:, CONV_CH + GLA_V:] = us.astype(BF16)
    vrow_ref[...] = vg


N_MIX_S_IN = 17


def _mix_sample_kernel(*refs, layer):
    if layer == 0:
        @pl.when(pl.program_id(0) == 0)
        def _():
            _mix_sample_body(*refs)

        @pl.when(pl.program_id(0) > 0)
        def _():
            ns_ref = refs[N_MIX_S_IN + 2]
            ns_ref[...] = jnp.zeros_like(ns_ref)
    else:
        _mix_sample_body(*refs[:N_MIX_S_IN], *refs[N_MIX_S_IN + 1:])


def _mix_sample(z, alr, a2, ab, cw, gng, cmg, ws, cmb, p1, p2, s0, ns_all, layer):
    row0 = T_PROMPT // ROWS_S
    n_i = DEC_BATCH // SEQ_PER_BLK
    n_pass = DEPTH if layer == 0 else 1
    blk = lambda p, i: jnp.where(p == 0, i, n_i - 1)
    row_map = lambda p, i: row0 + blk(p, i)
    slot = lambda p, i: (layer + p, i, 0, 0, 0)
    small = (a2, ab, cw, gng, cmg, ws, cmb)
    state_blk = (None, SEQ_PER_BLK, GLA_HEADS, GLA_DK, GLA_DV)
    in_specs = _z_specs(ROWS_S, row_map) + [
        pl.BlockSpec((ROWS_S, LANES), lambda p, i: (row_map(p, i), 0)),
    ] + [_layer_spec(a, layer) for a in small] + [
        pl.BlockSpec((ROWS_S, CONV_CH), lambda p, i: (blk(p, i), 0)),
        pl.BlockSpec((ROWS_S, CONV_CH), lambda p, i: (blk(p, i), 0)),
        pl.BlockSpec(state_blk, lambda p, i: (layer, blk(p, i), 0, 0, 0)),
    ]
    args = (z, z, z, z, z, z, alr, *small, p1, p2, s0)
    assert len(args) == N_MIX_S_IN
    aliases = {}
    if layer > 0:
        in_specs.append(pl.BlockSpec(memory_space=pl.ANY))
        args += (ns_all,)
        aliases = {N_MIX_S_IN: 2}
    return pl.pallas_call(
        functools.partial(_mix_sample_kernel, layer=layer),
        grid=(n_pass, n_i),
        in_specs=in_specs,
        out_specs=[
            pl.BlockSpec((ROWS_S, ACT_COLS), lambda p, i: (blk(p, i), 0)),
            pl.BlockSpec((ROWS_S, CONV_CH), lambda p, i: (blk(p, i), 0)),
            pl.BlockSpec(state_blk, slot),
            pl.BlockSpec((ROWS_S, CM_CH), lambda p, i: (blk(p, i), 0)),
        ],
        out_shape=[
            jax.ShapeDtypeStruct((T_SAMPLE, ACT_COLS), BF16),
            jax.ShapeDtypeStruct((T_SAMPLE, CONV_CH), F32),
            jax.ShapeDtypeStruct((DEPTH, DEC_BATCH, GLA_HEADS, GLA_DK, GLA_DV), F32),
            jax.ShapeDtypeStruct((T_SAMPLE, CM_CH), F32),
        ],
        input_output_aliases=aliases,
        compiler_params=pltpu.CompilerParams(
            dimension_semantics=("arbitrary", "arbitrary"), vmem_limit_bytes=VMEM_LIMIT),
        name="mix_sample",
    )(*args)


def _delayed(gen, n_stages):
    for _ in range(n_stages):
        yield
    yield from gen


def _interleave(*stage_lists):
    pending = list(stage_lists)
    while pending:
        for gen in list(pending):
            if next(gen, StopIteration) is StopIteration:
                pending.remove(gen)


def _outproj_main(i, rows, n_xa, actp_ref, acts_ref, gates_ref, xa_ref, xb_ref, pa_ref, pb_ref,
                  pc_ref, wo_ref, n2_ref, xo_ref, xn_ref, split_ref):
    acts = jnp.where(i < T_PROMPT // TM_OUT, actp_ref[rows, :], acts_ref[rows, :])
    x_in = jnp.where(i < n_xa, xa_ref[rows, :], xb_ref[rows, :])
    yb = _dot(acts[:, CONV_CH:CONV_CH + GLA_V], pb_ref[...])
    ga = _sigmoid(gates_ref[rows, COL_GA:COL_GA + D_MODEL])
    ya = _dot(acts[:, :CONV_CH], pa_ref[...])
    gb = _sigmoid(gates_ref[rows, COL_GB:COL_GB + D_MODEL])
    yield
    yc = _dot(acts[:, CONV_CH + GLA_V:], pc_ref[...])
    gc = _sigmoid(gates_ref[rows, COL_GC:COL_GC + D_MODEL])
    mix = ga * ya + gb * yb
    yield
    mix = mix + gc * yc
    x = x_in + _dot(mix.astype(BF16), wo_ref[...])
    xo_ref[rows, :] = x
    yield
    xn = _rms(x, n2_ref[...])
    xn_ref[rows, :] = xn
    hi, lo = _split_bf16(xn)
    yield
    split_ref[0, rows, :] = hi
    split_ref[1, rows, :] = lo


def _outproj_route(live, split_ref, wrh_ref, wrl_ref, rb_ref, route_ref, seg_ref, cnt_ref):
    tm = TM_OUT
    hi = split_ref[0]
    lo = split_ref[1]
    logits = (_dot_nt(wrh_ref[...], hi) + _dot_nt(wrh_ref[...], lo) + _dot_nt(wrl_ref[...], hi)
              + rb_ref[...])
    yield
    n_sub = EXPERTS_PER_GROUP
    sub = lax.broadcasted_iota(jnp.int32, (n_sub, tm), 0).astype(F32)
    neg = jnp.float32(-jnp.inf)
    lg = logits[0:MOE_GROUPS]
    gmax = jnp.max(lg, axis=0, keepdims=True)
    grp = jnp.min(jnp.where(lg == gmax, sub, 1e9), axis=0, keepdims=True)
    p_grp = 1.0 / jnp.sum(jnp.exp(lg - gmax), axis=0, keepdims=True)
    yield
    le = logits[MOE_GROUPS:MOE_GROUPS + n_sub]
    for g in range(1, MOE_GROUPS):
        le = jnp.where(grp == g, logits[MOE_GROUPS + g * n_sub:MOE_GROUPS + (g + 1) * n_sub], le)
    v1 = jnp.max(le, axis=0, keepdims=True)
    i1 = jnp.min(jnp.where(le == v1, sub, 1e9), axis=0, keepdims=True)
    le2 = jnp.where(sub == i1, neg, le)
    v2 = jnp.max(le2, axis=0, keepdims=True)
    i2 = jnp.min(jnp.where(le2 == v2, sub, 1e9), axis=0, keepdims=True)
    yield
    t = jnp.exp(v2 - v1)
    g1 = p_grp / (1.0 + t)
    g2 = p_grp * t / (1.0 + t)
    gate_t = jnp.where(sub == i1, g1, jnp.where(sub == i2, g2, 0.0))

    onehot_t = jnp.where(sub == grp, 1.0, 0.0)
    csum = jnp.where(live, jnp.sum(onehot_t, axis=1, keepdims=True), 0.0)
    rr = lax.broadcasted_iota(jnp.int32, (tm, tm), 0)
    cc = lax.broadcasted_iota(jnp.int32, (tm, tm), 1)
    earlier = jnp.where(rr < cc, 1.0, 0.0).astype(BF16)
    padded = jnp.concatenate([onehot_t, jnp.zeros_like(onehot_t)], axis=0).astype(BF16)
    same_before = _dot(padded, earlier)[0:n_sub]
    yield
    lower = jnp.sum(jnp.where(sub < grp, csum, 0.0), axis=0, keepdims=True)
    lrank = lower + jnp.sum(onehot_t * same_before, axis=0, keepdims=True)
    route_ref[0:n_sub, :] = gate_t
    route_ref[n_sub:, :] = jnp.broadcast_to(lrank, (n_sub, tm))
    carry = cnt_ref[...]
    lane = lax.broadcasted_iota(jnp.int32, (n_sub, LANES), 1)
    seg_ref[...] = jnp.where(lane == SEG_START, carry,
                             jnp.where(lane == SEG_LEN, csum, 0.0)).astype(jnp.int32)
    cnt_ref[...] = carry + csum


def _outproj_kernel(actp_ref, acts_ref, gates_ref, xa_ref, xb_ref, pa_ref, pb_ref, pc_ref, wo_ref,
                    n2_ref, wrh_ref, wrl_ref, rb_ref, xo_ref, xn_ref, route_ref, seg_ref, cnt_ref,
                    split_ref, *, n_xa):
    i = pl.program_id(0)

    @pl.when(i == 0)
    def _():
        cnt_ref[...] = jnp.zeros_like(cnt_ref)
        split_ref[...] = jnp.zeros_like(split_ref)

    tile = jnp.minimum(i, pl.num_programs(0) - 2)
    part = TM_OUT // OUT_PARTS
    _interleave(
        _delayed(_outproj_route(i >= 1, split_ref, wrh_ref, wrl_ref, rb_ref, route_ref, seg_ref,
                                cnt_ref), ROUTE_DELAY),
        *[_delayed(_outproj_main(tile, pl.ds(p * part, part), n_xa, actp_ref, acts_ref, gates_ref,
                                 xa_ref, xb_ref, pa_ref, pb_ref, pc_ref, wo_ref, n2_ref, xo_ref,
                                 xn_ref, split_ref), p * OUT_STAGGER)
          for p in range(OUT_PARTS)])


def _lane_prefix(v):
    rr = lax.broadcasted_iota(jnp.int32, (LANES, LANES), 0)
    cc = lax.broadcasted_iota(jnp.int32, (LANES, LANES), 1)
    earlier = jnp.where(rr < cc, 1.0, 0.0).astype(BF16)
    return _dot(jnp.broadcast_to(v, (8, LANES)).astype(BF16), earlier)[0:1]


def _outproj(acts_p, acts_s, z, xa, xb, pa, pb, pc, wo, n2, wrh, wrl, rb, layer):
    m = T_ALL
    nt = m // TM_OUT
    row = lambda i: (jnp.minimum(i, nt - 1), 0)
    routed = lambda i: (jnp.maximum(i - 1, 0), 0, 0)
    _, act_specs = _two_part_specs(TM_OUT, acts_p, acts_s)
    n_xa, x_specs = _two_part_specs(TM_OUT, xa, xb)
    weights = (pa, pb, pc, wo, n2, wrh, wrl, rb)
    return pl.pallas_call(
        functools.partial(_outproj_kernel, n_xa=n_xa),
        grid=(nt + 1,),
        in_specs=act_specs + [pl.BlockSpec((TM_OUT, 3 * D_MODEL), row)] + x_specs
        + [_layer_spec(w, layer) for w in weights],
        out_specs=[
            pl.BlockSpec((TM_OUT, D_MODEL), row),
            pl.BlockSpec((TM_OUT, D_MODEL), row),
            pl.BlockSpec((None, ROUTE_ROWS, TM_OUT), routed),
            pl.BlockSpec((None, 8, LANES), routed),
            pl.BlockSpec((8, LANES), lambda i: (0, 0)),
        ],
        scratch_shapes=[pltpu.VMEM((2, TM_OUT, D_MODEL), BF16)],
        out_shape=[
            jax.ShapeDtypeStruct((m, D_MODEL), F32),
            jax.ShapeDtypeStruct((m, D_MODEL), F32),
            jax.ShapeDtypeStruct((m // TM_OUT, ROUTE_ROWS, TM_OUT), F32),
            jax.ShapeDtypeStruct((m // TM_OUT, 8, LANES), jnp.int32),
            jax.ShapeDtypeStruct((8, LANES), F32),
        ],
        compiler_params=pltpu.CompilerParams(
            dimension_semantics=("arbitrary",), vmem_limit_bytes=VMEM_LIMIT),
        name="outproj",
    )(acts_p, acts_s, z, xa, xb, *weights)


PLAN_GROUP, PLAN_USED, PLAN_END, PLAN_BLOCKS, PLAN_BASE = 0, 1, 2, 3, 4


def _plan_kernel(cnt_ref, plan_ref):
    sub_i = lax.broadcasted_iota(jnp.int32, (8, LANES), 0)
    lane8_i = lax.broadcasted_iota(jnp.int32, (8, LANES), 1)
    cnt = jnp.sum(jnp.where(sub_i == lane8_i, cnt_ref[...], 0.0), axis=0, keepdims=True)
    blocks = jnp.floor((cnt + (MOE_BLK - 1)) * (1.0 / MOE_BLK))
    start = _lane_prefix(blocks)
    end = start + blocks
    lane_i = lax.broadcasted_iota(jnp.int32, (1, LANES), 1)
    lane = lane_i.astype(F32)
    grp_of_blk = jnp.zeros((1, LANES), F32)
    for g in range(MOE_GROUPS):
        end_g = jnp.sum(jnp.where(lane_i == g, end, 0.0), axis=-1, keepdims=True)
        grp_of_blk += jnp.where(lane >= end_g, 1.0, 0.0)
    grp_of_blk = jnp.minimum(grp_of_blk, MOE_GROUPS - 1)
    n_used = jnp.sum(blocks, axis=-1, keepdims=True)
    row = lax.broadcasted_iota(jnp.int32, (8, LANES), 0)
    plan_ref[...] = jnp.where(
        row == PLAN_GROUP, grp_of_blk,
        jnp.where(row == PLAN_USED, n_used,
                  jnp.where(row == PLAN_END, end,
                            jnp.where(row == PLAN_BLOCKS, blocks,
                                      jnp.where(row == PLAN_BASE, start * MOE_BLK, 0.0))))
    ).astype(jnp.int32)


def _plan(cnt):
    return pl.pallas_call(
        _plan_kernel,
        out_shape=jax.ShapeDtypeStruct((8, LANES), jnp.int32),
        name="plan",
    )(cnt)


SEG_START, SEG_LEN = 0, 1


def _segment_copies(seg_ref, base_ref, make_copy):
    local = 0
    for g in range(MOE_GROUPS):
        n = seg_ref[g, SEG_LEN]
        first = base_ref[g] + seg_ref[g, SEG_START]
        k = TM_ROW
        while k >= 1:
            done = n & ~(2 * k - 1)
            @pl.when((n & k) != 0)
            def _():
                make_copy(local + done, first + done, k).start()
            k //= 2
        local = local + n


def _perm_matrix(route_ref):
    row = lax.broadcasted_iota(jnp.int32, (TM_ROW, TM_ROW), 0).astype(F32)
    return jnp.where(row == route_ref[ROUTE_LRANK:ROUTE_LRANK + 1, :], 1.0, 0.0).astype(BF16)


SUB = 8
SUB_X = D_MODEL // 2 // LANES
SUB_GATE = SUB_X
HI_MASK = -65536


def _rows(first_row, n_rows):
    return pl.ds(pl.multiple_of(first_row * SUB, SUB), n_rows * SUB)


def _sublane(s, n_rows):
    return pl.ds(s, n_rows, stride=SUB)


def _scatter_kernel(nu_ref, end_ref, nb_ref, base_ref, seg_ref, route_ref, xn_ref, xs_ref,
                    zbuf, sbuf, sems):
    i = pl.program_id(0)
    n = pl.num_programs(0)
    sem = sems.at[0]

    def zero_block(b):
        return pltpu.make_async_copy(zbuf, xs_ref.at[_rows(b * MOE_BLK, MOE_BLK)], sem)

    def each_unfilled_block(fn):
        for g in range(MOE_GROUPS):
            @pl.when(nb_ref[g] > 0)
            def _():
                fn(zero_block(end_ref[g] - 1))
        for b in range(T_ALL // MOE_BLK, N_BLK):
            @pl.when(b >= nu_ref[0])
            def _():
                fn(zero_block(b))

    @pl.when(i == 0)
    def _():
        zbuf[...] = jnp.zeros_like(zbuf)
        sbuf[...] = jnp.zeros_like(sbuf)
        each_unfilled_block(lambda c: c.start())
        each_unfilled_block(lambda c: c.wait())

    def tile_done(s):
        pltpu.make_async_copy(sbuf.at[s], xs_ref.at[_rows(0, TM_ROW)], sems.at[s]).wait()

    for t in range(ROW_TILES):
        slot = (i % 2) * ROW_TILES + t

        @pl.when(i >= 2)
        def _():
            tile_done(slot)

        route = route_ref.at[t]
        perm = _perm_matrix(route)
        xn = xn_ref[t * TM_ROW:(t + 1) * TM_ROW, :]
        xs = lax.bitcast_convert_type(_dot(perm, xn.astype(BF16)), jnp.int32)
        half = D_MODEL // 2
        for s in range(SUB_X):
            hi = xs[:, s * LANES:(s + 1) * LANES] & HI_MASK
            lo = lax.shift_right_logical(xs[:, half + s * LANES:half + (s + 1) * LANES], 16)
            sbuf[slot, _sublane(s, TM_ROW), :] = hi | lo
        gate_t = route[0:EXPERTS_PER_GROUP, :]
        gate_t = jnp.concatenate(
            [gate_t, jnp.zeros((LANES - EXPERTS_PER_GROUP, TM_ROW), F32)], axis=0)
        g1 = gate_t.astype(BF16)
        r1 = gate_t - g1.astype(F32)
        g2 = r1.astype(BF16)
        g3 = (r1 - g2.astype(F32)).astype(BF16)
        sbuf[slot, _sublane(SUB_GATE, TM_ROW), :] = lax.bitcast_convert_type(
            _dot_nt(perm, g1) + _dot_nt(perm, g2) + _dot_nt(perm, g3), jnp.int32)

        def make_copy(src_row, dst_row, k, slot=slot):
            return pltpu.make_async_copy(sbuf.at[slot, _rows(src_row, k)],
                                         xs_ref.at[_rows(dst_row, k)], sems.at[slot])
        _segment_copies(seg_ref.at[t], base_ref, make_copy)

    @pl.when(i == n - 1)
    def _():
        for t in range(ROW_TILES):
            tile_done((i % 2) * ROW_TILES + t)

            @pl.when(n > 1)
            def _():
                tile_done((1 - i % 2) * ROW_TILES + t)


def _scatter(seg, route, xn, n_used, grp_end, grp_blocks, grp_base):
    m = xn.shape[0]
    step_rows = ROW_TILES * TM_ROW
    grid_spec = pltpu.PrefetchScalarGridSpec(
        num_scalar_prefetch=4,
        grid=(m // step_rows,),
        in_specs=[
            pl.BlockSpec((ROW_TILES, 8, LANES), lambda i, *_: (i, 0, 0), memory_space=pltpu.SMEM),
            pl.BlockSpec((ROW_TILES, ROUTE_ROWS, TM_ROW), lambda i, *_: (i, 0, 0)),
            pl.BlockSpec((step_rows, D_MODEL), lambda i, *_: (i, 0)),
        ],
        out_specs=pl.BlockSpec(memory_space=pl.ANY),
        scratch_shapes=[pltpu.VMEM((MOE_BLK * SUB, LANES), jnp.int32),
                        pltpu.VMEM((2 * ROW_TILES, TM_ROW * SUB, LANES), jnp.int32),
                        pltpu.SemaphoreType.DMA((2 * ROW_TILES,))],
    )
    return pl.pallas_call(
        _scatter_kernel,
        grid_spec=grid_spec,
        out_shape=jax.ShapeDtypeStruct((N_SORTED * SUB, LANES), jnp.int32),
        compiler_params=pltpu.CompilerParams(
            dimension_semantics=("arbitrary",), vmem_limit_bytes=VMEM_LIMIT),
        name="scatter",
    )(n_used, grp_end, grp_blocks, grp_base, seg, route, xn)


def _ffn_kernel(bg_ref, nu_ref, xs_ref, w1_ref, w3_ref, w2_ref, y_ref):
    del bg_ref
    b = pl.program_id(0)

    def block_inputs():
        packed = [xs_ref[_sublane(s, MOE_BLK), :] for s in range(SUB_X)]
        x = jnp.concatenate(
            [lax.bitcast_convert_type(u & HI_MASK, F32).astype(BF16) for u in packed]
            + [lax.bitcast_convert_type(lax.shift_left(u, 16), F32).astype(BF16) for u in packed],
            axis=1)
        return x, lax.bitcast_convert_type(xs_ref[_sublane(SUB_GATE, MOE_BLK), :], F32)

    def store(y):
        for s in range(SUB):
            y_ref[_sublane(s, MOE_BLK), :] = y[:, s * LANES:(s + 1) * LANES]

    @pl.when(b < nu_ref[0])
    def _():
        x, gates = block_inputs()
        n_e = EXPERTS_PER_GROUP
        h1 = [_dot(x, w1_ref[e]) for e in range(n_e)]
        h3 = [_dot(x, w3_ref[e]) for e in range(n_e)]
        hs = []
        for e in range(n_e):
            ge = gates[:, e:e + 1]
            hs.append(jnp.where(ge > 0.0, h1[e] * _sigmoid(h1[e]) * h3[e] * ge, 0.0).astype(BF16))
        hcat = jnp.concatenate(hs, axis=1)
        store(_dot(hcat, w2_ref[...].reshape(n_e * D_EXPERT, D_MODEL)))

    @pl.when(b >= nu_ref[0])
    def _():
        y_ref[...] = jnp.zeros_like(y_ref)


def _ffn(blk_group, n_used, xs, w1, w3, w2):
    grouped = (MOE_GROUPS, EXPERTS_PER_GROUP)
    w1, w3, w2 = (w.reshape(grouped + w.shape[1:]) for w in (w1, w3, w2))
    wmap = lambda b, bg, nu: (bg[b], 0, 0, 0)
    grid_spec = pltpu.PrefetchScalarGridSpec(
        num_scalar_prefetch=2,
        grid=(N_BLK,),
        in_specs=[
            pl.BlockSpec((MOE_BLK * SUB, LANES), lambda b, bg, nu: (b, 0)),
            pl.BlockSpec((None, EXPERTS_PER_GROUP, D_MODEL, D_EXPERT), wmap),
            pl.BlockSpec((None, EXPERTS_PER_GROUP, D_MODEL, D_EXPERT), wmap),
            pl.BlockSpec((None, EXPERTS_PER_GROUP, D_EXPERT, D_MODEL), wmap),
        ],
        out_specs=pl.BlockSpec((MOE_BLK * SUB, LANES), lambda b, bg, nu: (b, 0)),
    )
    return pl.pallas_call(
        _ffn_kernel,
        grid_spec=grid_spec,
        out_shape=jax.ShapeDtypeStruct((N_SORTED * SUB, LANES), F32),
        compiler_params=pltpu.CompilerParams(
            dimension_semantics=("arbitrary",), vmem_limit_bytes=VMEM_LIMIT),
        name="ffn",
    )(blk_group, n_used, xs, w1, w3, w2)


def _combine_kernel(base_ref, seg_ref, segn_ref, route_ref, ys_hbm, x_ref, g_ref, *refs, final):
    outs, (buf, sem) = refs[:-2], refs[-2:]
    i = pl.program_id(0)
    n = pl.num_programs(0)
    step_rows = ROW_TILES * TM_ROW

    def gather(seg, parity):
        for t in range(ROW_TILES):
            s = parity * ROW_TILES + t

            def make_copy(buf_row, ys_row, k, s=s):
                return pltpu.make_async_copy(ys_hbm.at[_rows(ys_row, k)],
                                             buf.at[s, _rows(buf_row, k)], sem.at[s])
            _segment_copies(seg.at[t], base_ref, make_copy)

    @pl.when(i == 0)
    def _():
        gather(seg_ref, 0)

    @pl.when(i + 1 < n)
    def _():
        gather(segn_ref, 1 - i % 2)

    tn = (((0,), (0,)), ((), ()))
    for t in range(ROW_TILES):
        slot = (i % 2) * ROW_TILES + t
        rows = slice(t * TM_ROW, (t + 1) * TM_ROW)
        pltpu.make_async_copy(ys_hbm.at[_rows(0, TM_ROW)], buf.at[slot], sem.at[slot]).wait()

        perm = _perm_matrix(route_ref.at[t])
        cols = []
        for s in range(SUB):
            hi, lo = _split_bf16(buf[slot, _sublane(s, TM_ROW), :])
            cols.append(lax.dot_general(perm, hi, tn, preferred_element_type=F32)
                        + lax.dot_general(perm, lo, tn, preferred_element_type=F32))
        y = x_ref[rows, :] + jnp.concatenate(cols, axis=1)
        if not final:
            outs[0][rows, :] = y
        else:
            y = _rms(y, g_ref[...])

            @pl.when(i < T_PROMPT // step_rows)
            def _():
                outs[0][rows, :] = y

            @pl.when(i >= T_PROMPT // step_rows)
            def _():
                outs[1][rows, :] = y


def _combine(seg, route, grp_base, ys, x, g, final):
    m = x.shape[0]
    step_rows = ROW_TILES * TM_ROW
    nt = m // step_rows
    n_p = T_PROMPT // step_rows
    smem = functools.partial(pl.BlockSpec, (ROW_TILES, 8, LANES), memory_space=pltpu.SMEM)
    tile = (step_rows, D_MODEL)
    if final:
        out_specs = [pl.BlockSpec(tile, lambda i, *_: (jnp.minimum(i, n_p - 1), 0)),
                     pl.BlockSpec(tile, lambda i, *_: (jnp.maximum(i - n_p, 0), 0))]
        out_shape = [jax.ShapeDtypeStruct((T_PROMPT, D_MODEL), F32),
                     jax.ShapeDtypeStruct((T_SAMPLE, D_MODEL), F32)]
    else:
        out_specs = [pl.BlockSpec(tile, lambda i, *_: (i, 0))]
        out_shape = [jax.ShapeDtypeStruct((m, D_MODEL), F32)]
    grid_spec = pltpu.PrefetchScalarGridSpec(
        num_scalar_prefetch=1,
        grid=(nt,),
        in_specs=[
            smem(lambda i, *_: (i, 0, 0)),
            smem(lambda i, *_: (jnp.minimum(i + 1, nt - 1), 0, 0)),
            pl.BlockSpec((ROW_TILES, ROUTE_ROWS, TM_ROW), lambda i, *_: (i, 0, 0)),
            pl.BlockSpec(memory_space=pl.ANY),
            pl.BlockSpec(tile, lambda i, *_: (i, 0)),
            pl.BlockSpec(g.shape, lambda i, *_: (0, 0)),
        ],
        out_specs=out_specs,
        scratch_shapes=[pltpu.VMEM((2 * ROW_TILES, TM_ROW * SUB, LANES), F32),
                        pltpu.SemaphoreType.DMA((2 * ROW_TILES,))],
    )
    return pl.pallas_call(
        functools.partial(_combine_kernel, final=final),
        grid_spec=grid_spec,
        out_shape=out_shape,
        compiler_params=pltpu.CompilerParams(
            dimension_semantics=("arbitrary",), vmem_limit_bytes=VMEM_LIMIT),
        name="combine",
    )(grp_base, seg, seg, route, ys, x, g)


def _prep_weights(w_in, gla_a2, cm_ws, cm_b, router_group_w, router_group_b,
                  router_expert_w, router_expert_b):
    off = {}
    o = 0
    for name, n in (("h", 512), ("cg", 512), ("bg", 512), ("q", 512), ("k", 512), ("v", 1024),
                    ("r", 1024), ("alr", 16), ("u", 512), ("vv", 512), ("ga", 1024),
                    ("gb", 1024), ("gc", 1024)):
        off[name] = (o, n)
        o += n
    runs = (("ga", "gc"), ("v", "r"), ("q", "k"), ("h", "cg"), ("bg", "bg"), ("u", "vv"))
    w_t = jnp.swapaxes(w_in, 1, 2)
    w_z = jnp.concatenate([w_t[:, off[a][0]:off[b][0] + off[b][1]] for a, b in runs],
                          axis=1).astype(BF16)
    assert w_z.shape[1] == Z_COLS
    a0 = off["alr"][0]
    w_alr = jnp.pad(w_t[:, a0:a0 + GLA_LOWRANK],
                    ((0, 0), (0, LANES - GLA_LOWRANK), (0, 0))).astype(BF16)
    a2 = jnp.pad(gla_a2, ((0, 0), (0, LANES - GLA_LOWRANK), (0, 0))).astype(BF16)
    ws_p = jnp.tril(cm_ws).astype(BF16)
    small = jnp.tril(cm_ws[:, :, :DEC_SEQ, :DEC_SEQ])
    eye = jnp.eye(SEQ_PER_BLK, dtype=F32)
    ws_s = jnp.einsum("ij,lgab->lgiajb", eye, small).reshape(
        DEPTH, CM_GROUPS, ROWS_S, ROWS_S).astype(BF16)
    cmb_p = jnp.broadcast_to(jnp.transpose(cm_b, (0, 2, 1))[:, :, :, None],
                             (DEPTH, CM_CHUNK, CM_GROUPS, CM_GCH)).reshape(DEPTH, CM_CHUNK, CM_CH)
    cmb_s = jnp.tile(cmb_p[:, :DEC_SEQ], (1, SEQ_PER_BLK, 1))
    pad = LANES - MOE_GROUPS - N_EXPERTS
    w_r = jnp.pad(jnp.swapaxes(jnp.concatenate([router_group_w, router_expert_w], axis=-1), 1, 2),
                  ((0, 0), (0, pad), (0, 0)))
    w_r_hi = w_r.astype(BF16)
    w_r_lo = (w_r - w_r_hi.astype(F32)).astype(BF16)
    r_b = jnp.pad(jnp.concatenate([router_group_b, router_expert_b], axis=-1),
                  ((0, 0), (0, pad)))[:, :, None]
    return w_z, w_alr, a2, ws_p, ws_s, cmb_p, cmb_s, w_r_hi, w_r_lo, r_b


def kernel(x_prompt, x_sample, state_conv, state_gla, norm1_g, w_in, conv_w, gla_a2, gla_a_b,
           gla_norm_g, cm_norm_g, cm_ws, cm_b, proj_a, proj_b, proj_c, w_out, norm2_g,
           router_group_w, router_group_b, router_expert_w, router_expert_b,
           exp_w1, exp_w3, exp_w2, final_norm_g):
    (w_z, w_alr, a2, ws_p, ws_s, cmb_p, cmb_s, w_r_hi, w_r_lo, r_b) = _prep_weights(
        w_in, gla_a2, cm_ws, cm_b, router_group_w, router_group_b, router_expert_w,
        router_expert_b)
    pa, pb, pc, wo = (w.astype(BF16) for w in (proj_a, proj_b, proj_c, w_out))
    n1 = norm1_g[:, None, :]
    n2 = norm2_g[:, None, :]
    ab = gla_a_b[:, None, :]
    gng = gla_norm_g[:, None, :]
    cmg = cm_norm_g.reshape(DEPTH, 1, CM_CH)
    fg = final_norm_g[None, :]
    xa = x_prompt.reshape(T_PROMPT, D_MODEL)
    xb = x_sample.reshape(T_SAMPLE, D_MODEL)
    gla_s = None
    conv_p, gla_p, conv_s, cmv_s = [], [], [], []
    for l in range(DEPTH):
        z, alr, w1, w3, w2 = _inproj(xa, xb, n1, w_z, w_alr, exp_w1, exp_w3, exp_w2, l)
        acts_p, nconv, ngla = _mix_prompt(z, alr, a2, ab, conv_w, gng, cmg, ws_p, cmb_p, l)
        sc = state_conv[l]
        p2 = jnp.pad(sc, ((0, 0), (0, DEC_SEQ - 2), (0, 0))).reshape(T_SAMPLE, CONV_CH)
        p1 = jnp.pad(sc[:, 1:2], ((0, 0), (0, DEC_SEQ - 1), (0, 0))).reshape(T_SAMPLE, CONV_CH)
        acts_s, cin_s, gla_s, vrows = _mix_sample(z, alr, a2, ab, conv_w, gng, cmg, ws_s, cmb_s,
                                                  p1, p2, state_gla, gla_s, l)
        conv_p.append(nconv)
        gla_p.append(ngla)
        conv_s.append(cin_s.reshape(DEC_BATCH, DEC_SEQ, CONV_CH)[:, DEC_SEQ - (CONV_K - 1):])
        cmv_s.append(vrows.reshape(DEC_BATCH, DEC_SEQ, CM_CH))

        x, xn, route, seg, cnt = _outproj(acts_p, acts_s, z, xa, xb, pa, pb, pc, wo, n2,
                                          w_r_hi, w_r_lo, r_b, l)
        plan = _plan(cnt)
        n_used = plan[PLAN_USED, :1]
        grp_base = plan[PLAN_BASE, :MOE_GROUPS]
        xs = _scatter(seg, route, xn, n_used, plan[PLAN_END, :MOE_GROUPS],
                      plan[PLAN_BLOCKS, :MOE_GROUPS], grp_base)
        ys = _ffn(plan[PLAN_GROUP, :N_BLK], n_used, xs, w1, w3, w2)
        out = _combine(seg, route, grp_base, ys, x, fg, l == DEPTH - 1)
        xa = xb = out[0]

    y_prompt = out[0].reshape(BATCH, SEQ, D_MODEL)
    y_sample = out[1].reshape(DEC_BATCH, DEC_SEQ, D_MODEL)
    return (y_prompt, y_sample, jnp.stack(conv_p), jnp.stack(gla_p), jnp.stack(conv_s),
            gla_s, jnp.stack(cmv_s))
```

```python
import functools

import jax
import jax.numpy as jnp
from jax import lax
from jax.experimental import pallas as pl
from jax.experimental.pallas import tpu as pltpu

F32 = jnp.float32
BF16 = jnp.bfloat16

D_MODEL = 1024
BATCH = 8
SEQ = 2048
DEPTH = 2
DEC_BATCH = 128
DEC_SEQ = 8
CONV_K = 3
CONV_CH = 512
GLA_HEADS = 4
GLA_DK = 128
GLA_DV = 256
GLA_QK = GLA_HEADS * GLA_DK
GLA_V = GLA_HEADS * GLA_DV
GLA_LOWRANK = 16
GLA_TAU = 16.0
GLA_CHUNK = 64
CM_GROUPS = 4
CM_CHUNK = 128
CM_GCH = 128
CM_CH = 512
MOE_GROUPS = 8
EXPERTS_PER_GROUP = 8
N_EXPERTS = 64
D_EXPERT = 256
EPS = 1e-6

LANES = 128
T_PROMPT = BATCH * SEQ
T_SAMPLE = DEC_BATCH * DEC_SEQ
T_ALL = T_PROMPT + T_SAMPLE

COL_GA, COL_GB, COL_GC = 0, 1024, 2048
COL_V, COL_R, COL_Q, COL_K = 3072, 4096, 5120, 5632
COL_H, COL_CG, COL_BG, COL_U, COL_VV = 6144, 6656, 7168, 7680, 8192
Z_COLS = 8704
ACT_COLS = 2048
ROUTE_ROWS = 2 * EXPERTS_PER_GROUP
ROUTE_LRANK = EXPERTS_PER_GROUP

TM_PRE = 1024
TM_IN = 512
TN_IN = Z_COLS // 2
TC_MIX = 256
SEQ_PER_BLK = 16
ROWS_S = SEQ_PER_BLK * DEC_SEQ
TM_OUT = 256
ROUTE_DELAY = 1
OUT_PARTS = 1
OUT_STAGGER = 1
TM_ROW = TM_OUT
ROW_TILES = 4
MOE_BLK = 256
N_BLK = T_ALL // MOE_BLK + MOE_GROUPS
N_SORTED = N_BLK * MOE_BLK
VMEM_LIMIT = 56 * 1024 * 1024


def _sigmoid(x):
    return 0.5 * jnp.tanh(0.5 * x) + 0.5


def _gelu_tanh(x):
    c = 0.7978845608028654
    half = 0.5 * x
    return half + half * jnp.tanh(x * (c + (c * 0.044715) * (x * x)))


def _log_sigmoid(x):
    log2_e = 1.4426950408889634
    ln_2 = 0.6931471805599453
    return jnp.minimum(x, 0.0) - ln_2 * jnp.log2(1.0 + jnp.exp2(-log2_e * jnp.abs(x)))


def _rms(x, g):
    ms = jnp.mean(x * x, axis=-1, keepdims=True)
    return x * lax.rsqrt(ms + EPS) * g


def _split_bf16(x):
    hi = x.astype(BF16)
    lo = (x - hi.astype(F32)).astype(BF16)
    return hi, lo


def _dot(a, b):
    return jnp.dot(a, b, preferred_element_type=F32)


def _dot_nt(a, b):
    return lax.dot_general(a, b, (((1,), (1,)), ((), ())), preferred_element_type=F32)


def _layer_spec(arr, layer):
    nd = arr.ndim - 1
    return pl.BlockSpec((None,) + arr.shape[1:], lambda *g: (layer,) + (0,) * nd)


def _const_spec(arr):
    nd = arr.ndim
    return pl.BlockSpec(arr.shape, lambda *g: (0,) * nd)


def _two_part_specs(tile, xa, xb):
    n_a = xa.shape[0] // tile
    n_b = xb.shape[0] // tile
    return n_a, [
        pl.BlockSpec((tile, xa.shape[1]), lambda i, *_: (jnp.minimum(i, n_a - 1), 0)),
        pl.BlockSpec((tile, xb.shape[1]), lambda i, *_: (jnp.clip(i - n_a, 0, n_b - 1), 0)),
    ]


def _norm_and_lowrank(x, g_ref, wa_ref, xn_ref, a_ref):
    xn = _rms(x, g_ref[...]).astype(BF16)
    xn_ref[...] = xn
    a_ref[...] = _dot_nt(xn, wa_ref[...]).astype(BF16)


def _prenorm_kernel(xa_ref, xb_ref, g_ref, wa_ref, xn_ref, a_ref, *, n_a):
    x = jnp.where(pl.program_id(0) < n_a, xa_ref[...], xb_ref[...])
    _norm_and_lowrank(x, g_ref, wa_ref, xn_ref, a_ref)


def _prenorm(xa, xb, g, wa, layer):
    m = T_ALL
    n_a, x_specs = _two_part_specs(TM_PRE, xa, xb)
    return pl.pallas_call(
        functools.partial(_prenorm_kernel, n_a=n_a),
        grid=(m // TM_PRE,),
        in_specs=x_specs + [_layer_spec(g, layer), _layer_spec(wa, layer)],
        out_specs=[pl.BlockSpec((TM_PRE, D_MODEL), lambda i: (i, 0)),
                   pl.BlockSpec((TM_PRE, LANES), lambda i: (i, 0))],
        out_shape=[jax.ShapeDtypeStruct((m, D_MODEL), BF16),
                   jax.ShapeDtypeStruct((m, LANES), BF16)],
        compiler_params=pltpu.CompilerParams(
            dimension_semantics=("arbitrary",), vmem_limit_bytes=VMEM_LIMIT),
        name="prenorm",
    )(xa, xb, g, wa)


def _inproj_kernel(xn_ref, w_ref, e1_ref, e3_ref, e2_ref, z_ref, o1_ref, o3_ref, o2_ref):
    o1_ref[...] = e1_ref[...].astype(BF16)
    o3_ref[...] = e3_ref[...].astype(BF16)
    o2_ref[...] = e2_ref[...].astype(BF16)
    z_ref[...] = _dot_nt(xn_ref[...], w_ref[...]).astype(BF16)


def _inproj(xn, w, e1, e3, e2, layer):
    m = xn.shape[0]
    n_i, n_j = m // TM_IN, Z_COLS // TN_IN
    assert n_i * n_j >= N_EXPERTS
    expert = lambda j, i: jnp.minimum(j * n_i + i, N_EXPERTS - 1)
    up, down = (D_MODEL, D_EXPERT), (D_EXPERT, D_MODEL)
    return pl.pallas_call(
        _inproj_kernel,
        grid=(n_j, n_i),
        in_specs=[
            pl.BlockSpec((TM_IN, D_MODEL), lambda j, i: (i, 0)),
            pl.BlockSpec((None, TN_IN, D_MODEL), lambda j, i: (layer, j, 0)),
            pl.BlockSpec((None, None) + up, lambda j, i: (layer, expert(j, i), 0, 0)),
            pl.BlockSpec((None, None) + up, lambda j, i: (layer, expert(j, i), 0, 0)),
            pl.BlockSpec((None, None) + down, lambda j, i: (layer, expert(j, i), 0, 0)),
        ],
        out_specs=[
            pl.BlockSpec((TM_IN, TN_IN), lambda j, i: (i, j)),
            pl.BlockSpec((None,) + up, lambda j, i: (expert(j, i), 0, 0)),
            pl.BlockSpec((None,) + up, lambda j, i: (expert(j, i), 0, 0)),
            pl.BlockSpec((None,) + down, lambda j, i: (expert(j, i), 0, 0)),
        ],
        out_shape=[
            jax.ShapeDtypeStruct((m, Z_COLS), BF16),
            jax.ShapeDtypeStruct((N_EXPERTS,) + up, BF16),
            jax.ShapeDtypeStruct((N_EXPERTS,) + up, BF16),
            jax.ShapeDtypeStruct((N_EXPERTS,) + down, BF16),
        ],
        compiler_params=pltpu.CompilerParams(
            dimension_semantics=("arbitrary", "arbitrary"), vmem_limit_bytes=VMEM_LIMIT),
        name="inproj",
    )(xn, w, e1, e3, e2)


def _gla_log_decay(alr_ref, a2_ref, ab_ref):
    la = _log_sigmoid(_dot(alr_ref[...], a2_ref[...]) + ab_ref[...]) * (1.0 / GLA_TAU)
    return _split_bf16(la)


def _masked_sum(mask, la_hi, la_lo):
    m = jnp.where(mask, 1.0, 0.0).astype(BF16)
    return _dot(m, la_hi) + _dot(m, la_lo)


def _gla_decay_prefix(alr_ref, a2_ref, ab_ref, tril_mask):
    la_hi, la_lo = _gla_log_decay(alr_ref, a2_ref, ab_ref)
    return _masked_sum(tril_mask, la_hi, la_lo)


def _gla_decay_terms(alr_ref, a2_ref, ab_ref, tril_mask, same_mask):
    la_hi, la_lo = _gla_log_decay(alr_ref, a2_ref, ab_ref)
    return (la_hi, la_lo, _masked_sum(tril_mask, la_hi, la_lo),
            _masked_sum(same_mask, la_hi, la_lo))


def _gla_out_gate(o, g_ref, r):
    return _rms(o, g_ref[...]) * (r * _sigmoid(r))


def _chunk_mlp_group(g, bgu_ref, vv_ref, cmg_ref, ws_ref, cmb_ref, n_chunks):
    sl = slice(g * CM_GCH, (g + 1) * CM_GCH)
    ug = _gelu_tanh(bgu_ref[:, CONV_CH + g * CM_GCH:CONV_CH + (g + 1) * CM_GCH])
    vg = _rms(_gelu_tanh(vv_ref[:, sl]).astype(F32), cmg_ref[:, sl])
    vgb = vg.astype(BF16)
    rows = []
    for j in range(n_chunks):
        rs = slice(j * CM_CHUNK, (j + 1) * CM_CHUNK)
        rows.append(_dot(ws_ref[g], vgb[rs]) + cmb_ref[:, sl])
    s = rows[0] if n_chunks == 1 else jnp.concatenate(rows, axis=0)
    return ug * s, vg


def _chunk_mlp(bgu_ref, vv_ref, cmg_ref, ws_ref, cmb_ref, n_chunks):
    parts = [_chunk_mlp_group(g, bgu_ref, vv_ref, cmg_ref, ws_ref, cmb_ref, n_chunks)
             for g in range(CM_GROUPS)]
    return (jnp.concatenate([p[0] for p in parts], axis=1),
            jnp.concatenate([p[1] for p in parts], axis=1))


N_Z_VIEWS = 7
N_MIX_W = 7
SEQ_PER_STEP = 4
SEQ_STAGGER = 3


def _mix_prompt_kernel(*refs):
    n_z = N_Z_VIEWS * SEQ_PER_STEP
    weights = refs[n_z:n_z + N_MIX_W]
    acts_ref, nconv_ref, ngla_ref, st_ref, carry_ref = refs[n_z + N_MIX_W:]

    @pl.when(pl.program_id(1) == 0)
    def _():
        st_ref[...] = jnp.zeros_like(st_ref)
        carry_ref[...] = jnp.zeros_like(carry_ref)

    _interleave(*[
        _delayed(_mix_prompt_seq(*refs[N_Z_VIEWS * s:N_Z_VIEWS * (s + 1)], *weights,
                                acts_ref.at[s], nconv_ref.at[s], st_ref.at[s], carry_ref.at[s]),
                s * SEQ_STAGGER)
        for s in range(SEQ_PER_STEP)])

    @pl.when(pl.program_id(1) == pl.num_programs(1) - 1)
    def _():
        for s in range(SEQ_PER_STEP):
            for hd in range(GLA_HEADS):
                ngla_ref[s, 0, hd] = st_ref[s, hd].T


def _mix_prompt_seq(v_ref, r_ref, qk_ref, hcg_ref, bgu_ref, vv_ref, alr_ref,
                    a2_ref, ab_ref, cw_ref, gng_ref, cmg_ref, ws_ref, cmb_ref,
                    acts_ref, nconv_ref, st_ref, carry_ref):
    tc = TC_MIX

    h = hcg_ref[:, :CONV_CH].astype(F32)
    cg = hcg_ref[:, CONV_CH:].astype(F32)
    bg = bgu_ref[:, :CONV_CH].astype(F32)
    cin = cg * h
    rr = lax.broadcasted_iota(jnp.int32, (tc, tc), 0)
    cc = lax.broadcasted_iota(jnp.int32, (tc, tc), 1)
    cin_b = cin.astype(BF16)
    x1 = _dot(jnp.where(rr - cc == 1, 1.0, 0.0).astype(BF16), cin_b)
    x2 = _dot(jnp.where(rr - cc == 2, 1.0, 0.0).astype(BF16), cin_b)
    conv = x2 * cw_ref[0:1, :] + x1 * cw_ref[1:2, :] + cin * cw_ref[2:3, :]
    c0 = carry_ref[0:1, :]
    c1 = carry_ref[1:2, :]
    row8 = lax.broadcasted_iota(jnp.int32, (8, 1), 0)
    head = jnp.where(row8 == 0, c0 * cw_ref[0:1, :] + c1 * cw_ref[1:2, :],
                     jnp.where(row8 == 1, c1 * cw_ref[0:1, :], 0.0))
    conv = jnp.concatenate([conv[0:8] + head, conv[8:]], axis=0)
    acts_ref[:, 0:CONV_CH] = (bg * conv).astype(BF16)
    carry_ref[0:2, :] = cin[tc - 2:tc, :]
    nconv_ref[0] = cin[tc - 2:tc, :]
    yield

    rr = lax.broadcasted_iota(jnp.int32, (tc, tc), 0)
    cc = lax.broadcasted_iota(jnp.int32, (tc, tc), 1)
    same = (rr >> 6) == (cc >> 6)
    tril = same & (cc <= rr)
    b = _gla_decay_prefix(alr_ref, a2_ref, ab_ref, tril)
    yield
    n_chunks = tc // GLA_CHUNK
    b_last = [b[(c + 1) * GLA_CHUNK - 1:(c + 1) * GLA_CHUNK, :] for c in range(n_chunks)]
    bl = jnp.concatenate([jnp.broadcast_to(r_, (GLA_CHUNK, GLA_QK)) for r_ in b_last], axis=0)
    q = qk_ref[:, :GLA_QK].astype(F32) * (GLA_DK ** -0.5)
    k = qk_ref[:, GLA_QK:].astype(F32)
    q_t = (q * jnp.exp(b)).astype(BF16)
    k_t = (k * jnp.exp(-b)).astype(BF16)
    k_end = (k * jnp.exp(bl - b)).astype(BF16)
    yield
    states = [st_ref[hd] for hd in range(GLA_HEADS)]
    k_cols = [slice(hd * GLA_DK, (hd + 1) * GLA_DK) for hd in range(GLA_HEADS)]
    v_cols = [slice(hd * GLA_DV, (hd + 1) * GLA_DV) for hd in range(GLA_HEADS)]
    o_intra = []
    for hd in range(GLA_HEADS):
        att = jnp.where(tril, _dot_nt(q_t[:, k_cols[hd]], k_t[:, k_cols[hd]]), 0.0).astype(BF16)
        o_intra.append(_dot(att, v_ref[:, v_cols[hd]]))
    yield

    o_rows = [[] for _ in range(GLA_HEADS)]
    assert n_chunks == CM_GROUPS
    for c in range(n_chunks):
        rs = slice(c * GLA_CHUNK, (c + 1) * GLA_CHUNK)
        for hd in range(GLA_HEADS):
            ks = k_cols[hd]
            st = states[hd]
            o_rows[hd].append(o_intra[hd][rs] + _dot_nt(q_t[rs, ks], st.astype(BF16)))
            upd = lax.dot_general(v_ref[rs, v_cols[hd]], k_end[rs, ks], (((0,), (0,)), ((), ())),
                                  preferred_element_type=F32)
            states[hd] = jnp.exp(b_last[c][:, ks]) * st + upd
        us, _ = _chunk_mlp_group(c, bgu_ref, vv_ref, cmg_ref, ws_ref, cmb_ref, tc // CM_CHUNK)
        acts_ref[:, CONV_CH + GLA_V + c * CM_GCH:CONV_CH + GLA_V + (c + 1) * CM_GCH] = (
            us.astype(BF16))
        yield

    for hd in range(GLA_HEADS):
        o = jnp.concatenate(o_rows[hd], axis=0)
        r = r_ref[:, v_cols[hd]]
        acts_ref[:, CONV_CH + hd * GLA_DV:CONV_CH + (hd + 1) * GLA_DV] = (
            _gla_out_gate(o, gng_ref, r).astype(BF16))
        st_ref[hd] = states[hd]
        if hd % 2 == 1:
            yield


def _z_specs(rows, row_map):
    def spec(width, col):
        blk = col // width
        return pl.BlockSpec((rows, width), lambda *g: (row_map(*g), blk))
    return [spec(1024, COL_V), spec(1024, COL_R), spec(1024, COL_Q), spec(1024, COL_H),
            spec(1024, COL_BG), spec(512, COL_VV)]


def _mix_prompt(z, alr, a2, ab, cw, gng, cmg, ws, cmb, layer):
    nt = SEQ // TC_MIX
    nb = BATCH // SEQ_PER_STEP
    small = (a2, ab, cw, gng, cmg, ws, cmb)
    assert len(small) == N_MIX_W
    in_specs, args = [], []
    for s in range(SEQ_PER_STEP):
        row_map = lambda b, c, s=s: (b + s * nb) * nt + c
        in_specs += _z_specs(TC_MIX, row_map) + [
            pl.BlockSpec((TC_MIX, LANES), lambda b, c, row_map=row_map: (row_map(b, c), 0))]
        args += [z] * (N_Z_VIEWS - 1) + [alr]
    in_specs += [_layer_spec(a, layer) for a in small]
    acts, nconv, ngla = pl.pallas_call(
        _mix_prompt_kernel,
        grid=(nb, nt),
        in_specs=in_specs,
        out_specs=[
            pl.BlockSpec((SEQ_PER_STEP, TC_MIX, ACT_COLS), lambda b, c: (0, b * nt + c, 0)),
            pl.BlockSpec((SEQ_PER_STEP, 1, CONV_K - 1, CONV_CH), lambda b, c: (0, b, 0, 0)),
            pl.BlockSpec((SEQ_PER_STEP, 1, GLA_HEADS, GLA_DK, GLA_DV),
                         lambda b, c: (0, b, 0, 0, 0)),
        ],
        out_shape=[
            jax.ShapeDtypeStruct((SEQ_PER_STEP, T_PROMPT // SEQ_PER_STEP, ACT_COLS), BF16),
            jax.ShapeDtypeStruct((SEQ_PER_STEP, nb, CONV_K - 1, CONV_CH), F32),
            jax.ShapeDtypeStruct((SEQ_PER_STEP, nb, GLA_HEADS, GLA_DK, GLA_DV), F32),
        ],
        scratch_shapes=[pltpu.VMEM((SEQ_PER_STEP, GLA_HEADS, GLA_DV, GLA_DK), F32),
                        pltpu.VMEM((SEQ_PER_STEP, 8, CONV_CH), F32)],
        compiler_params=pltpu.CompilerParams(
            dimension_semantics=("arbitrary", "arbitrary"), vmem_limit_bytes=VMEM_LIMIT),
        name="mix_prompt",
    )(*args, *small)
    return (acts.reshape(T_PROMPT, ACT_COLS), nconv.reshape(BATCH, CONV_K - 1, CONV_CH),
            ngla.reshape(BATCH, GLA_HEADS, GLA_DK, GLA_DV))


def _mix_sample_body(v_ref, r_ref, qk_ref, hcg_ref, bgu_ref, vv_ref, alr_ref,
                       a2_ref, ab_ref, cw_ref, gng_ref, cmg_ref, ws_ref, cmb_ref,
                       p1_ref, p2_ref, s0_ref,
                       acts_ref, cin_ref, ns_ref, vrow_ref):
    n = ROWS_S

    pos = lax.broadcasted_iota(jnp.int32, (n, 1), 0) & (DEC_SEQ - 1)
    h = hcg_ref[:, :CONV_CH].astype(F32)
    cg = hcg_ref[:, CONV_CH:].astype(F32)
    bg = bgu_ref[:, :CONV_CH].astype(F32)
    cin = cg * h
    x1 = jnp.where(pos >= 1, pltpu.roll(cin, 1, 0), p1_ref[...])
    x2 = jnp.where(pos >= 2, pltpu.roll(cin, 2, 0), p2_ref[...])
    conv = x2 * cw_ref[0:1, :] + x1 * cw_ref[1:2, :] + cin * cw_ref[2:3, :]
    acts_ref[:, 0:CONV_CH] = (bg * conv).astype(BF16)
    cin_ref[...] = cin

    rr = lax.broadcasted_iota(jnp.int32, (n, n), 0)
    cc = lax.broadcasted_iota(jnp.int32, (n, n), 1)
    same = (rr >> 3) == (cc >> 3)
    tril = same & (cc <= rr)
    la_hi, la_lo, b, bl = _gla_decay_terms(alr_ref, a2_ref, ab_ref, tril, same)
    q = qk_ref[:, :GLA_QK].astype(F32) * (GLA_DK ** -0.5)
    k = qk_ref[:, GLA_QK:].astype(F32)
    q_t = (q * jnp.exp(b)).astype(BF16)
    k_t = (k * jnp.exp(-b)).astype(BF16)
    k_end = k * jnp.exp(bl - b)
    la_hi = la_hi.astype(F32)
    la_lo = la_lo.astype(F32)
    row_seq = lax.broadcasted_iota(jnp.int32, (n, GLA_DK), 0) >> 3
    seq3 = lax.broadcasted_iota(jnp.int32, (SEQ_PER_BLK, GLA_DK, n), 0)
    lane_seq3 = lax.broadcasted_iota(jnp.int32, (SEQ_PER_BLK, GLA_DK, n), 2) >> 3
    mask3 = seq3 == lane_seq3
    ones = jnp.ones((n, GLA_DV), BF16)
    big = SEQ_PER_BLK * GLA_DK

    def per_seq(x_tr):
        x3 = jnp.where(mask3, x_tr[None, :, :], 0.0)
        return x3.reshape(big, n).astype(BF16)

    for hd in range(GLA_HEADS):
        ks = slice(hd * GLA_DK, (hd + 1) * GLA_DK)
        vs = slice(hd * GLA_DV, (hd + 1) * GLA_DV)
        qh = q_t[:, ks]
        vh = v_ref[:, vs]
        att = jnp.where(tril, _dot_nt(qh, k_t[:, ks]), 0.0).astype(BF16)
        o_intra = _dot(att, vh)
        s_old = s0_ref[:, hd].reshape(big, GLA_DV)
        zero = jnp.zeros_like(qh)
        q_big = jnp.concatenate(
            [jnp.where(row_seq == j, qh, zero) for j in range(SEQ_PER_BLK)], axis=1)
        o = o_intra + _dot(q_big, s_old.astype(BF16))
        dlog = _dot(per_seq(la_hi[:, ks].T), ones) + _dot(per_seq(la_lo[:, ks].T), ones)
        upd = _dot(per_seq(k_end[:, ks].T), vh)
        s_new = jnp.exp(dlog) * s_old + upd
        ns_ref[:, hd] = s_new.reshape(SEQ_PER_BLK, GLA_DK, GLA_DV)
        r = r_ref[:, vs]
        acts_ref[:, CONV_CH + hd * GLA_DV:CONV_CH + (hd + 1) * GLA_DV] = (
            _gla_out_gate(o, gng_ref, r).astype(BF16))

    us, vg = _chunk_mlp(bgu_ref, vv_ref, cmg_ref, ws_ref, cmb_ref, 1)
    acts_ref[:, CONV_CH + GLA_V:] = us.astype(BF16)
    vrow_ref[...] = vg


N_MIX_S_IN = 17


def _mix_sample_kernel(*refs, layer):
    if layer == 0:
        @pl.when(pl.program_id(0) == 0)
        def _():
            _mix_sample_body(*refs)

        @pl.when(pl.program_id(0) > 0)
        def _():
            ns_ref = refs[N_MIX_S_IN + 2]
            ns_ref[...] = jnp.zeros_like(ns_ref)
    else:
        _mix_sample_body(*refs[:N_MIX_S_IN], *refs[N_MIX_S_IN + 1:])


def _mix_sample(z, alr, a2, ab, cw, gng, cmg, ws, cmb, p1, p2, s0, ns_all, layer):
    row0 = T_PROMPT // ROWS_S
    n_i = DEC_BATCH // SEQ_PER_BLK
    n_pass = DEPTH if layer == 0 else 1
    blk = lambda p, i: jnp.where(p == 0, i, n_i - 1)
    row_map = lambda p, i: row0 + blk(p, i)
    slot = lambda p, i: (layer + p, i, 0, 0, 0)
    small = (a2, ab, cw, gng, cmg, ws, cmb)
    state_blk = (None, SEQ_PER_BLK, GLA_HEADS, GLA_DK, GLA_DV)
    in_specs = _z_specs(ROWS_S, row_map) + [
        pl.BlockSpec((ROWS_S, LANES), lambda p, i: (row_map(p, i), 0)),
    ] + [_layer_spec(a, layer) for a in small] + [
        pl.BlockSpec((ROWS_S, CONV_CH), lambda p, i: (blk(p, i), 0)),
        pl.BlockSpec((ROWS_S, CONV_CH), lambda p, i: (blk(p, i), 0)),
        pl.BlockSpec(state_blk, lambda p, i: (layer, blk(p, i), 0, 0, 0)),
    ]
    args = (z, z, z, z, z, z, alr, *small, p1, p2, s0)
    assert len(args) == N_MIX_S_IN
    aliases = {}
    if layer > 0:
        in_specs.append(pl.BlockSpec(memory_space=pl.ANY))
        args += (ns_all,)
        aliases = {N_MIX_S_IN: 2}
    return pl.pallas_call(
        functools.partial(_mix_sample_kernel, layer=layer),
        grid=(n_pass, n_i),
        in_specs=in_specs,
        out_specs=[
            pl.BlockSpec((ROWS_S, ACT_COLS), lambda p, i: (blk(p, i), 0)),
            pl.BlockSpec((ROWS_S, CONV_CH), lambda p, i: (blk(p, i), 0)),
            pl.BlockSpec(state_blk, slot),
            pl.BlockSpec((ROWS_S, CM_CH), lambda p, i: (blk(p, i), 0)),
        ],
        out_shape=[
            jax.ShapeDtypeStruct((T_SAMPLE, ACT_COLS), BF16),
            jax.ShapeDtypeStruct((T_SAMPLE, CONV_CH), F32),
            jax.ShapeDtypeStruct((DEPTH, DEC_BATCH, GLA_HEADS, GLA_DK, GLA_DV), F32),
            jax.ShapeDtypeStruct((T_SAMPLE, CM_CH), F32),
        ],
        input_output_aliases=aliases,
        compiler_params=pltpu.CompilerParams(
            dimension_semantics=("arbitrary", "arbitrary"), vmem_limit_bytes=VMEM_LIMIT),
        name="mix_sample",
    )(*args)


def _delayed(gen, n_stages):
    for _ in range(n_stages):
        yield
    yield from gen


def _interleave(*stage_lists):
    pending = list(stage_lists)
    while pending:
        for gen in list(pending):
            if next(gen, StopIteration) is StopIteration:
                pending.remove(gen)


def _outproj_main(i, rows, n_xa, actp_ref, acts_ref, gates_ref, xa_ref, xb_ref, pa_ref, pb_ref,
                  pc_ref, wo_ref, n2_ref, xo_ref, xn_ref, split_ref):
    acts = jnp.where(i < T_PROMPT // TM_OUT, actp_ref[rows, :], acts_ref[rows, :])
    x_in = jnp.where(i < n_xa, xa_ref[rows, :], xb_ref[rows, :])
    yb = _dot(acts[:, CONV_CH:CONV_CH + GLA_V], pb_ref[...])
    ga = _sigmoid(gates_ref[rows, COL_GA:COL_GA + D_MODEL])
    ya = _dot(acts[:, :CONV_CH], pa_ref[...])
    gb = _sigmoid(gates_ref[rows, COL_GB:COL_GB + D_MODEL])
    yield
    yc = _dot(acts[:, CONV_CH + GLA_V:], pc_ref[...])
    gc = _sigmoid(gates_ref[rows, COL_GC:COL_GC + D_MODEL])
    mix = ga * ya + gb * yb
    yield
    mix = mix + gc * yc
    x = x_in + _dot(mix.astype(BF16), wo_ref[...])
    xo_ref[rows, :] = x
    yield
    xn = _rms(x, n2_ref[...])
    xn_ref[rows, :] = xn
    hi, lo = _split_bf16(xn)
    yield
    split_ref[0, rows, :] = hi
    split_ref[1, rows, :] = lo


def _outproj_route(live, split_ref, wrh_ref, wrl_ref, rb_ref, route_ref, seg_ref, cnt_ref):
    tm = TM_OUT
    hi = split_ref[0]
    lo = split_ref[1]
    logits = (_dot_nt(wrh_ref[...], hi) + _dot_nt(wrh_ref[...], lo) + _dot_nt(wrl_ref[...], hi)
              + rb_ref[...])
    yield
    n_sub = EXPERTS_PER_GROUP
    sub = lax.broadcasted_iota(jnp.int32, (n_sub, tm), 0).astype(F32)
    neg = jnp.float32(-jnp.inf)
    lg = logits[0:MOE_GROUPS]
    gmax = jnp.max(lg, axis=0, keepdims=True)
    grp = jnp.min(jnp.where(lg == gmax, sub, 1e9), axis=0, keepdims=True)
    p_grp = 1.0 / jnp.sum(jnp.exp(lg - gmax), axis=0, keepdims=True)
    yield
    le = logits[MOE_GROUPS:MOE_GROUPS + n_sub]
    for g in range(1, MOE_GROUPS):
        le = jnp.where(grp == g, logits[MOE_GROUPS + g * n_sub:MOE_GROUPS + (g + 1) * n_sub], le)
    v1 = jnp.max(le, axis=0, keepdims=True)
    i1 = jnp.min(jnp.where(le == v1, sub, 1e9), axis=0, keepdims=True)
    le2 = jnp.where(sub == i1, neg, le)
    v2 = jnp.max(le2, axis=0, keepdims=True)
    i2 = jnp.min(jnp.where(le2 == v2, sub, 1e9), axis=0, keepdims=True)
    yield
    t = jnp.exp(v2 - v1)
    g1 = p_grp / (1.0 + t)
    g2 = p_grp * t / (1.0 + t)
    gate_t = jnp.where(sub == i1, g1, jnp.where(sub == i2, g2, 0.0))

    onehot_t = jnp.where(sub == grp, 1.0, 0.0)
    csum = jnp.where(live, jnp.sum(onehot_t, axis=1, keepdims=True), 0.0)
    rr = lax.broadcasted_iota(jnp.int32, (tm, tm), 0)
    cc = lax.broadcasted_iota(jnp.int32, (tm, tm), 1)
    earlier = jnp.where(rr < cc, 1.0, 0.0).astype(BF16)
    padded = jnp.concatenate([onehot_t, jnp.zeros_like(onehot_t)], axis=0).astype(BF16)
    same_before = _dot(padded, earlier)[0:n_sub]
    yield
    lower = jnp.sum(jnp.where(sub < grp, csum, 0.0), axis=0, keepdims=True)
    lrank = lower + jnp.sum(onehot_t * same_before, axis=0, keepdims=True)
    route_ref[0:n_sub, :] = gate_t
    route_ref[n_sub:, :] = jnp.broadcast_to(lrank, (n_sub, tm))
    carry = cnt_ref[...]
    lane = lax.broadcasted_iota(jnp.int32, (n_sub, LANES), 1)
    seg_ref[...] = jnp.where(lane == SEG_START, carry,
                             jnp.where(lane == SEG_LEN, csum, 0.0)).astype(jnp.int32)
    cnt_ref[...] = carry + csum


def _outproj_kernel(actp_ref, acts_ref, gates_ref, xa_ref, xb_ref, pa_ref, pb_ref, pc_ref, wo_ref,
                    n2_ref, wrh_ref, wrl_ref, rb_ref, xo_ref, xn_ref, route_ref, seg_ref, cnt_ref,
                    split_ref, *, n_xa):
    i = pl.program_id(0)

    @pl.when(i == 0)
    def _():
        cnt_ref[...] = jnp.zeros_like(cnt_ref)
        split_ref[...] = jnp.zeros_like(split_ref)

    tile = jnp.minimum(i, pl.num_programs(0) - 2)
    part = TM_OUT // OUT_PARTS
    _interleave(
        _delayed(_outproj_route(i >= 1, split_ref, wrh_ref, wrl_ref, rb_ref, route_ref, seg_ref,
                                cnt_ref), ROUTE_DELAY),
        *[_delayed(_outproj_main(tile, pl.ds(p * part, part), n_xa, actp_ref, acts_ref, gates_ref,
                                 xa_ref, xb_ref, pa_ref, pb_ref, pc_ref, wo_ref, n2_ref, xo_ref,
                                 xn_ref, split_ref), p * OUT_STAGGER)
          for p in range(OUT_PARTS)])


def _lane_prefix(v):
    rr = lax.broadcasted_iota(jnp.int32, (LANES, LANES), 0)
    cc = lax.broadcasted_iota(jnp.int32, (LANES, LANES), 1)
    earlier = jnp.where(rr < cc, 1.0, 0.0).astype(BF16)
    return _dot(jnp.broadcast_to(v, (8, LANES)).astype(BF16), earlier)[0:1]


def _outproj(acts_p, acts_s, z, xa, xb, pa, pb, pc, wo, n2, wrh, wrl, rb, layer):
    m = T_ALL
    nt = m // TM_OUT
    row = lambda i: (jnp.minimum(i, nt - 1), 0)
    routed = lambda i: (jnp.maximum(i - 1, 0), 0, 0)
    _, act_specs = _two_part_specs(TM_OUT, acts_p, acts_s)
    n_xa, x_specs = _two_part_specs(TM_OUT, xa, xb)
    weights = (pa, pb, pc, wo, n2, wrh, wrl, rb)
    return pl.pallas_call(
        functools.partial(_outproj_kernel, n_xa=n_xa),
        grid=(nt + 1,),
        in_specs=act_specs + [pl.BlockSpec((TM_OUT, 3 * D_MODEL), row)] + x_specs
        + [_layer_spec(w, layer) for w in weights],
        out_specs=[
            pl.BlockSpec((TM_OUT, D_MODEL), row),
            pl.BlockSpec((TM_OUT, D_MODEL), row),
            pl.BlockSpec((None, ROUTE_ROWS, TM_OUT), routed),
            pl.BlockSpec((None, 8, LANES), routed),
            pl.BlockSpec((8, LANES), lambda i: (0, 0)),
        ],
        scratch_shapes=[pltpu.VMEM((2, TM_OUT, D_MODEL), BF16)],
        out_shape=[
            jax.ShapeDtypeStruct((m, D_MODEL), F32),
            jax.ShapeDtypeStruct((m, D_MODEL), F32),
            jax.ShapeDtypeStruct((m // TM_OUT, ROUTE_ROWS, TM_OUT), F32),
            jax.ShapeDtypeStruct((m // TM_OUT, 8, LANES), jnp.int32),
            jax.ShapeDtypeStruct((8, LANES), F32),
        ],
        compiler_params=pltpu.CompilerParams(
            dimension_semantics=("arbitrary",), vmem_limit_bytes=VMEM_LIMIT),
        name="outproj",
    )(acts_p, acts_s, z, xa, xb, *weights)


PLAN_GROUP, PLAN_USED, PLAN_END, PLAN_BLOCKS, PLAN_BASE = 0, 1, 2, 3, 4


def _plan_kernel(cnt_ref, plan_ref):
    sub_i = lax.broadcasted_iota(jnp.int32, (8, LANES), 0)
    lane8_i = lax.broadcasted_iota(jnp.int32, (8, LANES), 1)
    cnt = jnp.sum(jnp.where(sub_i == lane8_i, cnt_ref[...], 0.0), axis=0, keepdims=True)
    blocks = jnp.floor((cnt + (MOE_BLK - 1)) * (1.0 / MOE_BLK))
    start = _lane_prefix(blocks)
    end = start + blocks
    lane_i = lax.broadcasted_iota(jnp.int32, (1, LANES), 1)
    lane = lane_i.astype(F32)
    grp_of_blk = jnp.zeros((1, LANES), F32)
    for g in range(MOE_GROUPS):
        end_g = jnp.sum(jnp.where(lane_i == g, end, 0.0), axis=-1, keepdims=True)
        grp_of_blk += jnp.where(lane >= end_g, 1.0, 0.0)
    grp_of_blk = jnp.minimum(grp_of_blk, MOE_GROUPS - 1)
    n_used = jnp.sum(blocks, axis=-1, keepdims=True)
    row = lax.broadcasted_iota(jnp.int32, (8, LANES), 0)
    plan_ref[...] = jnp.where(
        row == PLAN_GROUP, grp_of_blk,
        jnp.where(row == PLAN_USED, n_used,
                  jnp.where(row == PLAN_END, end,
                            jnp.where(row == PLAN_BLOCKS, blocks,
                                      jnp.where(row == PLAN_BASE, start * MOE_BLK, 0.0))))
    ).astype(jnp.int32)


def _plan(cnt):
    return pl.pallas_call(
        _plan_kernel,
        out_shape=jax.ShapeDtypeStruct((8, LANES), jnp.int32),
        name="plan",
    )(cnt)


SEG_START, SEG_LEN = 0, 1


def _segment_copies(seg_ref, base_ref, make_copy):
    local = 0
    for g in range(MOE_GROUPS):
        n = seg_ref[g, SEG_LEN]
        first = base_ref[g] + seg_ref[g, SEG_START]
        k = TM_ROW
        while k >= 1:
            done = n & ~(2 * k - 1)
            @pl.when((n & k) != 0)
            def _():
                make_copy(local + done, first + done, k).start()
            k //= 2
        local = local + n


def _perm_matrix(route_ref):
    row = lax.broadcasted_iota(jnp.int32, (TM_ROW, TM_ROW), 0).astype(F32)
    return jnp.where(row == route_ref[ROUTE_LRANK:ROUTE_LRANK + 1, :], 1.0, 0.0).astype(BF16)


SUB = 8
SUB_X = D_MODEL // 2 // LANES
SUB_GATE = SUB_X
HI_MASK = -65536


def _rows(first_row, n_rows):
    return pl.ds(pl.multiple_of(first_row * SUB, SUB), n_rows * SUB)


def _sublane(s, n_rows):
    return pl.ds(s, n_rows, stride=SUB)


def _scatter_kernel(nu_ref, end_ref, nb_ref, base_ref, seg_ref, route_ref, xn_ref, xs_ref,
                    zbuf, sbuf, sems):
    i = pl.program_id(0)
    n = pl.num_programs(0)
    sem = sems.at[0]

    def zero_block(b):
        return pltpu.make_async_copy(zbuf, xs_ref.at[_rows(b * MOE_BLK, MOE_BLK)], sem)

    def each_unfilled_block(fn):
        for g in range(MOE_GROUPS):
            @pl.when(nb_ref[g] > 0)
            def _():
                fn(zero_block(end_ref[g] - 1))
        for b in range(T_ALL // MOE_BLK, N_BLK):
            @pl.when(b >= nu_ref[0])
            def _():
                fn(zero_block(b))

    @pl.when(i == 0)
    def _():
        zbuf[...] = jnp.zeros_like(zbuf)
        sbuf[...] = jnp.zeros_like(sbuf)
        each_unfilled_block(lambda c: c.start())
        each_unfilled_block(lambda c: c.wait())

    def tile_done(s):
        pltpu.make_async_copy(sbuf.at[s], xs_ref.at[_rows(0, TM_ROW)], sems.at[s]).wait()

    for t in range(ROW_TILES):
        slot = (i % 2) * ROW_TILES + t

        @pl.when(i >= 2)
        def _():
            tile_done(slot)

        route = route_ref.at[t]
        perm = _perm_matrix(route)
        xn = xn_ref[t * TM_ROW:(t + 1) * TM_ROW, :]
        xs = lax.bitcast_convert_type(_dot(perm, xn.astype(BF16)), jnp.int32)
        half = D_MODEL // 2
        for s in range(SUB_X):
            hi = xs[:, s * LANES:(s + 1) * LANES] & HI_MASK
            lo = lax.shift_right_logical(xs[:, half + s * LANES:half + (s + 1) * LANES], 16)
            sbuf[slot, _sublane(s, TM_ROW), :] = hi | lo
        gate_t = route[0:EXPERTS_PER_GROUP, :]
        gate_t = jnp.concatenate(
            [gate_t, jnp.zeros((LANES - EXPERTS_PER_GROUP, TM_ROW), F32)], axis=0)
        g1 = gate_t.astype(BF16)
        r1 = gate_t - g1.astype(F32)
        g2 = r1.astype(BF16)
        g3 = (r1 - g2.astype(F32)).astype(BF16)
        sbuf[slot, _sublane(SUB_GATE, TM_ROW), :] = lax.bitcast_convert_type(
            _dot_nt(perm, g1) + _dot_nt(perm, g2) + _dot_nt(perm, g3), jnp.int32)

        def make_copy(src_row, dst_row, k, slot=slot):
            return pltpu.make_async_copy(sbuf.at[slot, _rows(src_row, k)],
                                         xs_ref.at[_rows(dst_row, k)], sems.at[slot])
        _segment_copies(seg_ref.at[t], base_ref, make_copy)

    @pl.when(i == n - 1)
    def _():
        for t in range(ROW_TILES):
            tile_done((i % 2) * ROW_TILES + t)

            @pl.when(n > 1)
            def _():
                tile_done((1 - i % 2) * ROW_TILES + t)


def _scatter(seg, route, xn, n_used, grp_end, grp_blocks, grp_base):
    m = xn.shape[0]
    step_rows = ROW_TILES * TM_ROW
    grid_spec = pltpu.PrefetchScalarGridSpec(
        num_scalar_prefetch=4,
        grid=(m // step_rows,),
        in_specs=[
            pl.BlockSpec((ROW_TILES, 8, LANES), lambda i, *_: (i, 0, 0), memory_space=pltpu.SMEM),
            pl.BlockSpec((ROW_TILES, ROUTE_ROWS, TM_ROW), lambda i, *_: (i, 0, 0)),
            pl.BlockSpec((step_rows, D_MODEL), lambda i, *_: (i, 0)),
        ],
        out_specs=pl.BlockSpec(memory_space=pl.ANY),
        scratch_shapes=[pltpu.VMEM((MOE_BLK * SUB, LANES), jnp.int32),
                        pltpu.VMEM((2 * ROW_TILES, TM_ROW * SUB, LANES), jnp.int32),
                        pltpu.SemaphoreType.DMA((2 * ROW_TILES,))],
    )
    return pl.pallas_call(
        _scatter_kernel,
        grid_spec=grid_spec,
        out_shape=jax.ShapeDtypeStruct((N_SORTED * SUB, LANES), jnp.int32),
        compiler_params=pltpu.CompilerParams(
            dimension_semantics=("arbitrary",), vmem_limit_bytes=VMEM_LIMIT),
        name="scatter",
    )(n_used, grp_end, grp_blocks, grp_base, seg, route, xn)


def _ffn_kernel(bg_ref, nu_ref, xs_ref, w1_ref, w3_ref, w2_ref, y_ref):
    del bg_ref
    b = pl.program_id(0)

    def block_inputs():
        packed = [xs_ref[_sublane(s, MOE_BLK), :] for s in range(SUB_X)]
        x = jnp.concatenate(
            [lax.bitcast_convert_type(u & HI_MASK, F32).astype(BF16) for u in packed]
            + [lax.bitcast_convert_type(lax.shift_left(u, 16), F32).astype(BF16) for u in packed],
            axis=1)
        return x, lax.bitcast_convert_type(xs_ref[_sublane(SUB_GATE, MOE_BLK), :], F32)

    def store(y):
        for s in range(SUB):
            y_ref[_sublane(s, MOE_BLK), :] = y[:, s * LANES:(s + 1) * LANES]

    @pl.when(b < nu_ref[0])
    def _():
        x, gates = block_inputs()
        n_e = EXPERTS_PER_GROUP
        h1 = [_dot(x, w1_ref[e]) for e in range(n_e)]
        h3 = [_dot(x, w3_ref[e]) for e in range(n_e)]
        hs = []
        for e in range(n_e):
            ge = gates[:, e:e + 1]
            hs.append(jnp.where(ge > 0.0, h1[e] * _sigmoid(h1[e]) * h3[e] * ge, 0.0).astype(BF16))
        hcat = jnp.concatenate(hs, axis=1)
        store(_dot(hcat, w2_ref[...].reshape(n_e * D_EXPERT, D_MODEL)))

    @pl.when(b >= nu_ref[0])
    def _():
        y_ref[...] = jnp.zeros_like(y_ref)


def _ffn(blk_group, n_used, xs, w1, w3, w2):
    grouped = (MOE_GROUPS, EXPERTS_PER_GROUP)
    w1, w3, w2 = (w.reshape(grouped + w.shape[1:]) for w in (w1, w3, w2))
    wmap = lambda b, bg, nu: (bg[b], 0, 0, 0)
    grid_spec = pltpu.PrefetchScalarGridSpec(
        num_scalar_prefetch=2,
        grid=(N_BLK,),
        in_specs=[
            pl.BlockSpec((MOE_BLK * SUB, LANES), lambda b, bg, nu: (b, 0)),
            pl.BlockSpec((None, EXPERTS_PER_GROUP, D_MODEL, D_EXPERT), wmap),
            pl.BlockSpec((None, EXPERTS_PER_GROUP, D_MODEL, D_EXPERT), wmap),
            pl.BlockSpec((None, EXPERTS_PER_GROUP, D_EXPERT, D_MODEL), wmap),
        ],
        out_specs=pl.BlockSpec((MOE_BLK * SUB, LANES), lambda b, bg, nu: (b, 0)),
    )
    return pl.pallas_call(
        _ffn_kernel,
        grid_spec=grid_spec,
        out_shape=jax.ShapeDtypeStruct((N_SORTED * SUB, LANES), F32),
        compiler_params=pltpu.CompilerParams(
            dimension_semantics=("arbitrary",), vmem_limit_bytes=VMEM_LIMIT),
        name="ffn",
    )(blk_group, n_used, xs, w1, w3, w2)


def _combine_kernel(base_ref, seg_ref, segn_ref, route_ref, ys_hbm, x_ref, g_ref, wa_ref, *refs,
                    final):
    outs, (buf, sem) = refs[:-2], refs[-2:]
    i = pl.program_id(0)
    n = pl.num_programs(0)
    step_rows = ROW_TILES * TM_ROW

    def gather(seg, parity):
        for t in range(ROW_TILES):
            s = parity * ROW_TILES + t

            def make_copy(buf_row, ys_row, k, s=s):
                return pltpu.make_async_copy(ys_hbm.at[_rows(ys_row, k)],
                                             buf.at[s, _rows(buf_row, k)], sem.at[s])
            _segment_copies(seg.at[t], base_ref, make_copy)

    @pl.when(i == 0)
    def _():
        gather(seg_ref, 0)

    @pl.when(i + 1 < n)
    def _():
        gather(segn_ref, 1 - i % 2)

    tn = (((0,), (0,)), ((), ()))
    for t in range(ROW_TILES):
        slot = (i % 2) * ROW_TILES + t
        rows = slice(t * TM_ROW, (t + 1) * TM_ROW)
        pltpu.make_async_copy(ys_hbm.at[_rows(0, TM_ROW)], buf.at[slot], sem.at[slot]).wait()

        perm = _perm_matrix(route_ref.at[t])
        cols = []
        for s in range(SUB):
            hi, lo = _split_bf16(buf[slot, _sublane(s, TM_ROW), :])
            cols.append(lax.dot_general(perm, hi, tn, preferred_element_type=F32)
                        + lax.dot_general(perm, lo, tn, preferred_element_type=F32))
        y = x_ref[rows, :] + jnp.concatenate(cols, axis=1)
        if not final:
            outs[0][rows, :] = y
            _norm_and_lowrank(y, g_ref, wa_ref, outs[1].at[rows, :], outs[2].at[rows, :])
        else:
            y = _rms(y, g_ref[...])

            @pl.when(i < T_PROMPT // step_rows)
            def _():
                outs[0][rows, :] = y

            @pl.when(i >= T_PROMPT // step_rows)
            def _():
                outs[1][rows, :] = y


def _combine(seg, route, grp_base, ys, x, g, wa, final):
    m = x.shape[0]
    step_rows = ROW_TILES * TM_ROW
    nt = m // step_rows
    n_p = T_PROMPT // step_rows
    smem = functools.partial(pl.BlockSpec, (ROW_TILES, 8, LANES), memory_space=pltpu.SMEM)
    tile = (step_rows, D_MODEL)
    if final:
        out_specs = [pl.BlockSpec(tile, lambda i, *_: (jnp.minimum(i, n_p - 1), 0)),
                     pl.BlockSpec(tile, lambda i, *_: (jnp.maximum(i - n_p, 0), 0))]
        out_shape = [jax.ShapeDtypeStruct((T_PROMPT, D_MODEL), F32),
                     jax.ShapeDtypeStruct((T_SAMPLE, D_MODEL), F32)]
    else:
        out_specs = [pl.BlockSpec(tile, lambda i, *_: (i, 0)),
                     pl.BlockSpec(tile, lambda i, *_: (i, 0)),
                     pl.BlockSpec((step_rows, LANES), lambda i, *_: (i, 0))]
        out_shape = [jax.ShapeDtypeStruct((m, D_MODEL), F32),
                     jax.ShapeDtypeStruct((m, D_MODEL), BF16),
                     jax.ShapeDtypeStruct((m, LANES), BF16)]
    grid_spec = pltpu.PrefetchScalarGridSpec(
        num_scalar_prefetch=1,
        grid=(nt,),
        in_specs=[
            smem(lambda i, *_: (i, 0, 0)),
            smem(lambda i, *_: (jnp.minimum(i + 1, nt - 1), 0, 0)),
            pl.BlockSpec((ROW_TILES, ROUTE_ROWS, TM_ROW), lambda i, *_: (i, 0, 0)),
            pl.BlockSpec(memory_space=pl.ANY),
            pl.BlockSpec(tile, lambda i, *_: (i, 0)),
            pl.BlockSpec(g.shape, lambda i, *_: (0, 0)),
            pl.BlockSpec(wa.shape, lambda i, *_: (0, 0)),
        ],
        out_specs=out_specs,
        scratch_shapes=[pltpu.VMEM((2 * ROW_TILES, TM_ROW * SUB, LANES), F32),
                        pltpu.SemaphoreType.DMA((2 * ROW_TILES,))],
    )
    return pl.pallas_call(
        functools.partial(_combine_kernel, final=final),
        grid_spec=grid_spec,
        out_shape=out_shape,
        compiler_params=pltpu.CompilerParams(
            dimension_semantics=("arbitrary",), vmem_limit_bytes=VMEM_LIMIT),
        name="combine",
    )(grp_base, seg, seg, route, ys, x, g, wa)


def _prep_weights(w_in, gla_a2, cm_ws, cm_b, router_group_w, router_group_b,
                  router_expert_w, router_expert_b):
    off = {}
    o = 0
    for name, n in (("h", 512), ("cg", 512), ("bg", 512), ("q", 512), ("k", 512), ("v", 1024),
                    ("r", 1024), ("alr", 16), ("u", 512), ("vv", 512), ("ga", 1024),
                    ("gb", 1024), ("gc", 1024)):
        off[name] = (o, n)
        o += n
    runs = (("ga", "gc"), ("v", "r"), ("q", "k"), ("h", "cg"), ("bg", "bg"), ("u", "vv"))
    w_t = jnp.swapaxes(w_in, 1, 2)
    w_z = jnp.concatenate([w_t[:, off[a][0]:off[b][0] + off[b][1]] for a, b in runs],
                          axis=1).astype(BF16)
    assert w_z.shape[1] == Z_COLS
    a0 = off["alr"][0]
    w_alr = jnp.pad(w_t[:, a0:a0 + GLA_LOWRANK],
                    ((0, 0), (0, LANES - GLA_LOWRANK), (0, 0))).astype(BF16)
    a2 = jnp.pad(gla_a2, ((0, 0), (0, LANES - GLA_LOWRANK), (0, 0))).astype(BF16)
    ws_p = jnp.tril(cm_ws).astype(BF16)
    small = jnp.tril(cm_ws[:, :, :DEC_SEQ, :DEC_SEQ])
    eye = jnp.eye(SEQ_PER_BLK, dtype=F32)
    ws_s = jnp.einsum("ij,lgab->lgiajb", eye, small).reshape(
        DEPTH, CM_GROUPS, ROWS_S, ROWS_S).astype(BF16)
    cmb_p = jnp.broadcast_to(jnp.transpose(cm_b, (0, 2, 1))[:, :, :, None],
                             (DEPTH, CM_CHUNK, CM_GROUPS, CM_GCH)).reshape(DEPTH, CM_CHUNK, CM_CH)
    cmb_s = jnp.tile(cmb_p[:, :DEC_SEQ], (1, SEQ_PER_BLK, 1))
    pad = LANES - MOE_GROUPS - N_EXPERTS
    w_r = jnp.pad(jnp.swapaxes(jnp.concatenate([router_group_w, router_expert_w], axis=-1), 1, 2),
                  ((0, 0), (0, pad), (0, 0)))
    w_r_hi = w_r.astype(BF16)
    w_r_lo = (w_r - w_r_hi.astype(F32)).astype(BF16)
    r_b = jnp.pad(jnp.concatenate([router_group_b, router_expert_b], axis=-1),
                  ((0, 0), (0, pad)))[:, :, None]
    return w_z, w_alr, a2, ws_p, ws_s, cmb_p, cmb_s, w_r_hi, w_r_lo, r_b


def kernel(x_prompt, x_sample, state_conv, state_gla, norm1_g, w_in, conv_w, gla_a2, gla_a_b,
           gla_norm_g, cm_norm_g, cm_ws, cm_b, proj_a, proj_b, proj_c, w_out, norm2_g,
           router_group_w, router_group_b, router_expert_w, router_expert_b,
           exp_w1, exp_w3, exp_w2, final_norm_g):
    (w_z, w_alr, a2, ws_p, ws_s, cmb_p, cmb_s, w_r_hi, w_r_lo, r_b) = _prep_weights(
        w_in, gla_a2, cm_ws, cm_b, router_group_w, router_group_b, router_expert_w,
        router_expert_b)
    pa, pb, pc, wo = (w.astype(BF16) for w in (proj_a, proj_b, proj_c, w_out))
    n1 = norm1_g[:, None, :]
    n2 = norm2_g[:, None, :]
    ab = gla_a_b[:, None, :]
    gng = gla_norm_g[:, None, :]
    cmg = cm_norm_g.reshape(DEPTH, 1, CM_CH)
    fg = final_norm_g[None, :]
    xa = x_prompt.reshape(T_PROMPT, D_MODEL)
    xb = x_sample.reshape(T_SAMPLE, D_MODEL)
    gla_s = None
    conv_p, gla_p, conv_s, cmv_s = [], [], [], []
    xin, alr = _prenorm(xa, xb, n1, w_alr, 0)
    for l in range(DEPTH):
        z, w1, w3, w2 = _inproj(xin, w_z, exp_w1, exp_w3, exp_w2, l)
        acts_p, nconv, ngla = _mix_prompt(z, alr, a2, ab, conv_w, gng, cmg, ws_p, cmb_p, l)
        sc = state_conv[l]
        p2 = jnp.pad(sc, ((0, 0), (0, DEC_SEQ - 2), (0, 0))).reshape(T_SAMPLE, CONV_CH)
        p1 = jnp.pad(sc[:, 1:2], ((0, 0), (0, DEC_SEQ - 1), (0, 0))).reshape(T_SAMPLE, CONV_CH)
        acts_s, cin_s, gla_s, vrows = _mix_sample(z, alr, a2, ab, conv_w, gng, cmg, ws_s, cmb_s,
                                                  p1, p2, state_gla, gla_s, l)
        conv_p.append(nconv)
        gla_p.append(ngla)
        conv_s.append(cin_s.reshape(DEC_BATCH, DEC_SEQ, CONV_CH)[:, DEC_SEQ - (CONV_K - 1):])
        cmv_s.append(vrows.reshape(DEC_BATCH, DEC_SEQ, CM_CH))

        x, xn, route, seg, cnt = _outproj(acts_p, acts_s, z, xa, xb, pa, pb, pc, wo, n2,
                                          w_r_hi, w_r_lo, r_b, l)
        plan = _plan(cnt)
        n_used = plan[PLAN_USED, :1]
        grp_base = plan[PLAN_BASE, :MOE_GROUPS]
        xs = _scatter(seg, route, xn, n_used, plan[PLAN_END, :MOE_GROUPS],
                      plan[PLAN_BLOCKS, :MOE_GROUPS], grp_base)
        ys = _ffn(plan[PLAN_GROUP, :N_BLK], n_used, xs, w1, w3, w2)
        if l == DEPTH - 1:
            out = _combine(seg, route, grp_base, ys, x, fg, w_alr[l], True)
        else:
            xa, xin, alr = _combine(seg, route, grp_base, ys, x, n1[l + 1], w_alr[l + 1], False)
            xb = xa

    y_prompt = out[0].reshape(BATCH, SEQ, D_MODEL)
    y_sample = out[1].reshape(DEC_BATCH, DEC_SEQ, D_MODEL)
    return (y_prompt, y_sample, jnp.stack(conv_p), jnp.stack(gla_p), jnp.stack(conv_s),
            gla_s, jnp.stack(cmv_s))
```

```python
import functools

import jax
import jax.numpy as jnp
from jax import lax
from jax.experimental import pallas as pl
from jax.experimental.pallas import tpu as pltpu

F32 = jnp.float32
BF16 = jnp.bfloat16

D_MODEL = 1024
BATCH = 8
SEQ = 2048
DEPTH = 2
DEC_BATCH = 128
DEC_SEQ = 8
CONV_K = 3
CONV_CH = 512
GLA_HEADS = 4
GLA_DK = 128
GLA_DV = 256
GLA_QK = GLA_HEADS * GLA_DK
GLA_V = GLA_HEADS * GLA_DV
GLA_LOWRANK = 16
GLA_TAU = 16.0
GLA_CHUNK = 64
CM_GROUPS = 4
CM_CHUNK = 128
CM_GCH = 128
CM_CH = 512
MOE_GROUPS = 8
EXPERTS_PER_GROUP = 8
N_EXPERTS = 64
D_EXPERT = 256
EPS = 1e-6

LANES = 128
T_PROMPT = BATCH * SEQ
T_SAMPLE = DEC_BATCH * DEC_SEQ
T_ALL = T_PROMPT + T_SAMPLE

COL_GA, COL_GB, COL_GC = 0, 1024, 2048
COL_V, COL_R, COL_Q, COL_K = 3072, 4096, 5120, 5632
COL_H, COL_CG, COL_BG, COL_U, COL_VV = 6144, 6656, 7168, 7680, 8192
Z_COLS = 8704
ACT_COLS = 2048
ROUTE_ROWS = 2 * EXPERTS_PER_GROUP
ROUTE_LRANK = EXPERTS_PER_GROUP

TM_PRE = 1024
TM_IN = 512
TN_IN = Z_COLS // 2
TC_MIX = 256
SEQ_PER_BLK = 16
ROWS_S = SEQ_PER_BLK * DEC_SEQ
TM_OUT = 256
ROUTE_DELAY = 1
OUT_PARTS = 1
OUT_STAGGER = 1
TM_ROW = TM_OUT
ROW_TILES = 4
MOE_BLK = 256
N_BLK = T_ALL // MOE_BLK + MOE_GROUPS
N_SORTED = N_BLK * MOE_BLK
VMEM_LIMIT = 56 * 1024 * 1024


def _sigmoid(x):
    return 0.5 * jnp.tanh(0.5 * x) + 0.5


def _gelu_tanh(x):
    c = 0.7978845608028654
    half = 0.5 * x
    return half + half * jnp.tanh(x * (c + (c * 0.044715) * (x * x)))


def _log_sigmoid(x):
    log2_e = 1.4426950408889634
    ln_2 = 0.6931471805599453
    return jnp.minimum(x, 0.0) - ln_2 * jnp.log2(1.0 + jnp.exp2(-log2_e * jnp.abs(x)))


def _rms(x, g):
    ms = jnp.mean(x * x, axis=-1, keepdims=True)
    return x * lax.rsqrt(ms + EPS) * g


def _split_bf16(x):
    hi = x.astype(BF16)
    lo = (x - hi.astype(F32)).astype(BF16)
    return hi, lo


def _dot(a, b):
    return jnp.dot(a, b, preferred_element_type=F32)


def _dot_nt(a, b):
    return lax.dot_general(a, b, (((1,), (1,)), ((), ())), preferred_element_type=F32)


def _layer_spec(arr, layer):
    nd = arr.ndim - 1
    return pl.BlockSpec((None,) + arr.shape[1:], lambda *g: (layer,) + (0,) * nd)


def _const_spec(arr):
    nd = arr.ndim
    return pl.BlockSpec(arr.shape, lambda *g: (0,) * nd)


def _two_part_specs(tile, xa, xb):
    n_a = xa.shape[0] // tile
    n_b = xb.shape[0] // tile
    return n_a, [
        pl.BlockSpec((tile, xa.shape[1]), lambda i, *_: (jnp.minimum(i, n_a - 1), 0)),
        pl.BlockSpec((tile, xb.shape[1]), lambda i, *_: (jnp.clip(i - n_a, 0, n_b - 1), 0)),
    ]


def _norm_and_lowrank(x, g_ref, wa_ref, xn_ref, a_ref):
    xn = _rms(x, g_ref[...]).astype(BF16)
    xn_ref[...] = xn
    a_ref[...] = _dot_nt(xn, wa_ref[...]).astype(BF16)


def _prenorm_kernel(xa_ref, xb_ref, g_ref, wa_ref, xn_ref, a_ref, *, n_a):
    x = jnp.where(pl.program_id(0) < n_a, xa_ref[...], xb_ref[...])
    _norm_and_lowrank(x, g_ref, wa_ref, xn_ref, a_ref)


def _prenorm(xa, xb, g, wa, layer):
    m = T_ALL
    n_a, x_specs = _two_part_specs(TM_PRE, xa, xb)
    return pl.pallas_call(
        functools.partial(_prenorm_kernel, n_a=n_a),
        grid=(m // TM_PRE,),
        in_specs=x_specs + [_layer_spec(g, layer), _layer_spec(wa, layer)],
        out_specs=[pl.BlockSpec((TM_PRE, D_MODEL), lambda i: (i, 0)),
                   pl.BlockSpec((TM_PRE, LANES), lambda i: (i, 0))],
        out_shape=[jax.ShapeDtypeStruct((m, D_MODEL), BF16),
                   jax.ShapeDtypeStruct((m, LANES), BF16)],
        compiler_params=pltpu.CompilerParams(
            dimension_semantics=("arbitrary",), vmem_limit_bytes=VMEM_LIMIT),
        name="prenorm",
    )(xa, xb, g, wa)


def _inproj_kernel(xn_ref, w_ref, e1_ref, e3_ref, e2_ref, z_ref, o1_ref, o3_ref, o2_ref):
    o1_ref[...] = e1_ref[...].astype(BF16)
    o3_ref[...] = e3_ref[...].astype(BF16)
    o2_ref[...] = e2_ref[...].astype(BF16)
    z_ref[...] = _dot_nt(xn_ref[...], w_ref[...]).astype(BF16)


def _inproj(xn, w, e1, e3, e2, layer):
    m = xn.shape[0]
    n_i, n_j = m // TM_IN, Z_COLS // TN_IN
    assert n_i * n_j >= N_EXPERTS
    expert = lambda j, i: jnp.minimum(j * n_i + i, N_EXPERTS - 1)
    up, down = (D_MODEL, D_EXPERT), (D_EXPERT, D_MODEL)
    return pl.pallas_call(
        _inproj_kernel,
        grid=(n_j, n_i),
        in_specs=[
            pl.BlockSpec((TM_IN, D_MODEL), lambda j, i: (i, 0)),
            pl.BlockSpec((None, TN_IN, D_MODEL), lambda j, i: (layer, j, 0)),
            pl.BlockSpec((None, None) + up, lambda j, i: (layer, expert(j, i), 0, 0)),
            pl.BlockSpec((None, None) + up, lambda j, i: (layer, expert(j, i), 0, 0)),
            pl.BlockSpec((None, None) + down, lambda j, i: (layer, expert(j, i), 0, 0)),
        ],
        out_specs=[
            pl.BlockSpec((TM_IN, TN_IN), lambda j, i: (i, j)),
            pl.BlockSpec((None,) + up, lambda j, i: (expert(j, i), 0, 0)),
            pl.BlockSpec((None,) + up, lambda j, i: (expert(j, i), 0, 0)),
            pl.BlockSpec((None,) + down, lambda j, i: (expert(j, i), 0, 0)),
        ],
        out_shape=[
            jax.ShapeDtypeStruct((m, Z_COLS), BF16),
            jax.ShapeDtypeStruct((N_EXPERTS,) + up, BF16),
            jax.ShapeDtypeStruct((N_EXPERTS,) + up, BF16),
            jax.ShapeDtypeStruct((N_EXPERTS,) + down, BF16),
        ],
        compiler_params=pltpu.CompilerParams(
            dimension_semantics=("arbitrary", "arbitrary"), vmem_limit_bytes=VMEM_LIMIT),
        name="inproj",
    )(xn, w, e1, e3, e2)


def _gla_log_decay(alr_ref, a2_ref, ab_ref):
    la = _log_sigmoid(_dot(alr_ref[...], a2_ref[...]) + ab_ref[...]) * (1.0 / GLA_TAU)
    return _split_bf16(la)


def _masked_sum(mask, la_hi, la_lo):
    m = jnp.where(mask, 1.0, 0.0).astype(BF16)
    return _dot(m, la_hi) + _dot(m, la_lo)


def _gla_decay_prefix(alr_ref, a2_ref, ab_ref, tril_mask):
    la_hi, la_lo = _gla_log_decay(alr_ref, a2_ref, ab_ref)
    return _masked_sum(tril_mask, la_hi, la_lo)


def _gla_decay_terms(alr_ref, a2_ref, ab_ref, tril_mask, same_mask):
    la_hi, la_lo = _gla_log_decay(alr_ref, a2_ref, ab_ref)
    return (la_hi, la_lo, _masked_sum(tril_mask, la_hi, la_lo),
            _masked_sum(same_mask, la_hi, la_lo))


def _gla_out_gate(o, g_ref, r):
    return _rms(o, g_ref[...]) * (r * _sigmoid(r))


def _chunk_mlp_group(g, bgu_ref, vv_ref, cmg_ref, ws_ref, cmb_ref, n_chunks):
    sl = slice(g * CM_GCH, (g + 1) * CM_GCH)
    ug = _gelu_tanh(bgu_ref[:, CONV_CH + g * CM_GCH:CONV_CH + (g + 1) * CM_GCH])
    vg = _rms(_gelu_tanh(vv_ref[:, sl]).astype(F32), cmg_ref[:, sl])
    vgb = vg.astype(BF16)
    rows = []
    for j in range(n_chunks):
        rs = slice(j * CM_CHUNK, (j + 1) * CM_CHUNK)
        rows.append(_dot(ws_ref[g], vgb[rs]) + cmb_ref[:, sl])
    s = rows[0] if n_chunks == 1 else jnp.concatenate(rows, axis=0)
    return ug * s, vg


def _chunk_mlp(bgu_ref, vv_ref, cmg_ref, ws_ref, cmb_ref, n_chunks):
    parts = [_chunk_mlp_group(g, bgu_ref, vv_ref, cmg_ref, ws_ref, cmb_ref, n_chunks)
             for g in range(CM_GROUPS)]
    return (jnp.concatenate([p[0] for p in parts], axis=1),
            jnp.concatenate([p[1] for p in parts], axis=1))


N_Z_VIEWS = 7
N_MIX_W = 7
SEQ_PER_STEP = 4
SEQ_STAGGER = 3


def _mix_prompt_kernel(*refs):
    n_z = N_Z_VIEWS * SEQ_PER_STEP
    weights = refs[n_z:n_z + N_MIX_W]
    acts_ref, nconv_ref, ngla_ref, st_ref, carry_ref = refs[n_z + N_MIX_W:]

    @pl.when(pl.program_id(1) == 0)
    def _():
        st_ref[...] = jnp.zeros_like(st_ref)
        carry_ref[...] = jnp.zeros_like(carry_ref)

    _interleave(*[
        _delayed(_mix_prompt_seq(*refs[N_Z_VIEWS * s:N_Z_VIEWS * (s + 1)], *weights,
                                acts_ref.at[s], nconv_ref.at[s], st_ref.at[s], carry_ref.at[s]),
                s * SEQ_STAGGER)
        for s in range(SEQ_PER_STEP)])

    @pl.when(pl.program_id(1) == pl.num_programs(1) - 1)
    def _():
        for s in range(SEQ_PER_STEP):
            for hd in range(GLA_HEADS):
                ngla_ref[s, 0, hd] = st_ref[s, hd].T


def _mix_prompt_seq(v_ref, r_ref, qk_ref, hcg_ref, bgu_ref, vv_ref, alr_ref,
                    a2_ref, ab_ref, cw_ref, gng_ref, cmg_ref, ws_ref, cmb_ref,
                    acts_ref, nconv_ref, st_ref, carry_ref):
    tc = TC_MIX

    h = hcg_ref[:, :CONV_CH].astype(F32)
    cg = hcg_ref[:, CONV_CH:].astype(F32)
    bg = bgu_ref[:, :CONV_CH].astype(F32)
    cin = cg * h
    rr = lax.broadcasted_iota(jnp.int32, (tc, tc), 0)
    cc = lax.broadcasted_iota(jnp.int32, (tc, tc), 1)
    cin_b = cin.astype(BF16)
    x1 = _dot(jnp.where(rr - cc == 1, 1.0, 0.0).astype(BF16), cin_b)
    x2 = _dot(jnp.where(rr - cc == 2, 1.0, 0.0).astype(BF16), cin_b)
    conv = x2 * cw_ref[0:1, :] + x1 * cw_ref[1:2, :] + cin * cw_ref[2:3, :]
    c0 = carry_ref[0:1, :]
    c1 = carry_ref[1:2, :]
    row8 = lax.broadcasted_iota(jnp.int32, (8, 1), 0)
    head = jnp.where(row8 == 0, c0 * cw_ref[0:1, :] + c1 * cw_ref[1:2, :],
                     jnp.where(row8 == 1, c1 * cw_ref[0:1, :], 0.0))
    conv = jnp.concatenate([conv[0:8] + head, conv[8:]], axis=0)
    acts_ref[:, 0:CONV_CH] = (bg * conv).astype(BF16)
    carry_ref[0:2, :] = cin[tc - 2:tc, :]
    nconv_ref[0] = cin[tc - 2:tc, :]
    yield

    rr = lax.broadcasted_iota(jnp.int32, (tc, tc), 0)
    cc = lax.broadcasted_iota(jnp.int32, (tc, tc), 1)
    same = (rr >> 6) == (cc >> 6)
    tril = same & (cc <= rr)
    b = _gla_decay_prefix(alr_ref, a2_ref, ab_ref, tril)
    yield
    n_chunks = tc // GLA_CHUNK
    b_last = [b[(c + 1) * GLA_CHUNK - 1:(c + 1) * GLA_CHUNK, :] for c in range(n_chunks)]
    bl = jnp.concatenate([jnp.broadcast_to(r_, (GLA_CHUNK, GLA_QK)) for r_ in b_last], axis=0)
    q = qk_ref[:, :GLA_QK].astype(F32) * (GLA_DK ** -0.5)
    k = qk_ref[:, GLA_QK:].astype(F32)
    q_t = (q * jnp.exp(b)).astype(BF16)
    k_t = (k * jnp.exp(-b)).astype(BF16)
    k_end = (k * jnp.exp(bl - b)).astype(BF16)
    yield
    states = [st_ref[hd] for hd in range(GLA_HEADS)]
    k_cols = [slice(hd * GLA_DK, (hd + 1) * GLA_DK) for hd in range(GLA_HEADS)]
    v_cols = [slice(hd * GLA_DV, (hd + 1) * GLA_DV) for hd in range(GLA_HEADS)]
    o_intra = []
    for hd in range(GLA_HEADS):
        att = jnp.where(tril, _dot_nt(q_t[:, k_cols[hd]], k_t[:, k_cols[hd]]), 0.0).astype(BF16)
        o_intra.append(_dot(att, v_ref[:, v_cols[hd]]))
    yield

    o_rows = [[] for _ in range(GLA_HEADS)]
    assert n_chunks == CM_GROUPS
    for c in range(n_chunks):
        rs = slice(c * GLA_CHUNK, (c + 1) * GLA_CHUNK)
        for hd in range(GLA_HEADS):
            ks = k_cols[hd]
            st = states[hd]
            o_rows[hd].append(o_intra[hd][rs] + _dot_nt(q_t[rs, ks], st.astype(BF16)))
            upd = lax.dot_general(v_ref[rs, v_cols[hd]], k_end[rs, ks], (((0,), (0,)), ((), ())),
                                  preferred_element_type=F32)
            states[hd] = jnp.exp(b_last[c][:, ks]) * st + upd
        us, _ = _chunk_mlp_group(c, bgu_ref, vv_ref, cmg_ref, ws_ref, cmb_ref, tc // CM_CHUNK)
        acts_ref[:, CONV_CH + GLA_V + c * CM_GCH:CONV_CH + GLA_V + (c + 1) * CM_GCH] = (
            us.astype(BF16))
        yield

    for hd in range(GLA_HEADS):
        o = jnp.concatenate(o_rows[hd], axis=0)
        r = r_ref[:, v_cols[hd]]
        acts_ref[:, CONV_CH + hd * GLA_DV:CONV_CH + (hd + 1) * GLA_DV] = (
            _gla_out_gate(o, gng_ref, r).astype(BF16))
        st_ref[hd] = states[hd]
        if hd % 2 == 1:
            yield


def _z_specs(rows, row_map):
    def spec(width, col):
        blk = col // width
        return pl.BlockSpec((rows, width), lambda *g: (row_map(*g), blk))
    return [spec(1024, COL_V), spec(1024, COL_R), spec(1024, COL_Q), spec(1024, COL_H),
            spec(1024, COL_BG), spec(512, COL_VV)]


def _mix_prompt(z, alr, a2, ab, cw, gng, cmg, ws, cmb, layer):
    nt = SEQ // TC_MIX
    nb = BATCH // SEQ_PER_STEP
    small = (a2, ab, cw, gng, cmg, ws, cmb)
    assert len(small) == N_MIX_W
    in_specs, args = [], []
    for s in range(SEQ_PER_STEP):
        row_map = lambda b, c, s=s: (b + s * nb) * nt + c
        in_specs += _z_specs(TC_MIX, row_map) + [
            pl.BlockSpec((TC_MIX, LANES), lambda b, c, row_map=row_map: (row_map(b, c), 0))]
        args += [z] * (N_Z_VIEWS - 1) + [alr]
    in_specs += [_layer_spec(a, layer) for a in small]
    acts, nconv, ngla = pl.pallas_call(
        _mix_prompt_kernel,
        grid=(nb, nt),
        in_specs=in_specs,
        out_specs=[
            pl.BlockSpec((SEQ_PER_STEP, TC_MIX, ACT_COLS), lambda b, c: (0, b * nt + c, 0)),
            pl.BlockSpec((SEQ_PER_STEP, 1, CONV_K - 1, CONV_CH), lambda b, c: (0, b, 0, 0)),
            pl.BlockSpec((SEQ_PER_STEP, 1, GLA_HEADS, GLA_DK, GLA_DV),
                         lambda b, c: (0, b, 0, 0, 0)),
        ],
        out_shape=[
            jax.ShapeDtypeStruct((SEQ_PER_STEP, T_PROMPT // SEQ_PER_STEP, ACT_COLS), BF16),
            jax.ShapeDtypeStruct((SEQ_PER_STEP, nb, CONV_K - 1, CONV_CH), F32),
            jax.ShapeDtypeStruct((SEQ_PER_STEP, nb, GLA_HEADS, GLA_DK, GLA_DV), F32),
        ],
        scratch_shapes=[pltpu.VMEM((SEQ_PER_STEP, GLA_HEADS, GLA_DV, GLA_DK), F32),
                        pltpu.VMEM((SEQ_PER_STEP, 8, CONV_CH), F32)],
        compiler_params=pltpu.CompilerParams(
            dimension_semantics=("arbitrary", "arbitrary"), vmem_limit_bytes=VMEM_LIMIT),
        name="mix_prompt",
    )(*args, *small)
    return (acts.reshape(T_PROMPT, ACT_COLS), nconv.reshape(BATCH, CONV_K - 1, CONV_CH),
            ngla.reshape(BATCH, GLA_HEADS, GLA_DK, GLA_DV))


def _mix_sample_body(v_ref, r_ref, qk_ref, hcg_ref, bgu_ref, vv_ref, alr_ref,
                       a2_ref, ab_ref, cw_ref, gng_ref, cmg_ref, ws_ref, cmb_ref,
                       p1_ref, p2_ref, s0_ref,
                       acts_ref, cin_ref, ns_ref, vrow_ref):
    n = ROWS_S

    pos = lax.broadcasted_iota(jnp.int32, (n, 1), 0) & (DEC_SEQ - 1)
    h = hcg_ref[:, :CONV_CH].astype(F32)
    cg = hcg_ref[:, CONV_CH:].astype(F32)
    bg = bgu_ref[:, :CONV_CH].astype(F32)
    cin = cg * h
    x1 = jnp.where(pos >= 1, pltpu.roll(cin, 1, 0), p1_ref[...])
    x2 = jnp.where(pos >= 2, pltpu.roll(cin, 2, 0), p2_ref[...])
    conv = x2 * cw_ref[0:1, :] + x1 * cw_ref[1:2, :] + cin * cw_ref[2:3, :]
    acts_ref[:, 0:CONV_CH] = (bg * conv).astype(BF16)
    cin_ref[...] = cin

    rr = lax.broadcasted_iota(jnp.int32, (n, n), 0)
    cc = lax.broadcasted_iota(jnp.int32, (n, n), 1)
    same = (rr >> 3) == (cc >> 3)
    tril = same & (cc <= rr)
    la_hi, la_lo, b, bl = _gla_decay_terms(alr_ref, a2_ref, ab_ref, tril, same)
    q = qk_ref[:, :GLA_QK].astype(F32) * (GLA_DK ** -0.5)
    k = qk_ref[:, GLA_QK:].astype(F32)
    q_t = (q * jnp.exp(b)).astype(BF16)
    k_t = (k * jnp.exp(-b)).astype(BF16)
    k_end = k * jnp.exp(bl - b)
    la_hi = la_hi.astype(F32)
    la_lo = la_lo.astype(F32)
    row_seq = lax.broadcasted_iota(jnp.int32, (n, GLA_DK), 0) >> 3
    seq3 = lax.broadcasted_iota(jnp.int32, (SEQ_PER_BLK, GLA_DK, n), 0)
    lane_seq3 = lax.broadcasted_iota(jnp.int32, (SEQ_PER_BLK, GLA_DK, n), 2) >> 3
    mask3 = seq3 == lane_seq3
    ones = jnp.ones((n, GLA_DV), BF16)
    big = SEQ_PER_BLK * GLA_DK

    def per_seq(x_tr):
        x3 = jnp.where(mask3, x_tr[None, :, :], 0.0)
        return x3.reshape(big, n).astype(BF16)

    for hd in range(GLA_HEADS):
        ks = slice(hd * GLA_DK, (hd + 1) * GLA_DK)
        vs = slice(hd * GLA_DV, (hd + 1) * GLA_DV)
        qh = q_t[:, ks]
        vh = v_ref[:, vs]
        att = jnp.where(tril, _dot_nt(qh, k_t[:, ks]), 0.0).astype(BF16)
        o_intra = _dot(att, vh)
        s_old = s0_ref[:, hd].reshape(big, GLA_DV)
        zero = jnp.zeros_like(qh)
        q_big = jnp.concatenate(
            [jnp.where(row_seq == j, qh, zero) for j in range(SEQ_PER_BLK)], axis=1)
        o = o_intra + _dot(q_big, s_old.astype(BF16))
        dlog = _dot(per_seq(la_hi[:, ks].T), ones) + _dot(per_seq(la_lo[:, ks].T), ones)
        upd = _dot(per_seq(k_end[:, ks].T), vh)
        s_new = jnp.exp(dlog) * s_old + upd
        ns_ref[:, hd] = s_new.reshape(SEQ_PER_BLK, GLA_DK, GLA_DV)
        r = r_ref[:, vs]
        acts_ref[:, CONV_CH + hd * GLA_DV:CONV_CH + (hd + 1) * GLA_DV] = (
            _gla_out_gate(o, gng_ref, r).astype(BF16))

    us, vg = _chunk_mlp(bgu_ref, vv_ref, cmg_ref, ws_ref, cmb_ref, 1)
    acts_ref[:, CONV_CH + GLA_V:] = us.astype(BF16)
    vrow_ref[...] = vg


N_MIX_S_IN = 17


def _mix_sample_kernel(*refs, layer):
    if layer == 0:
        @pl.when(pl.program_id(0) == 0)
        def _():
            _mix_sample_body(*refs)

        @pl.when(pl.program_id(0) > 0)
        def _():
            ns_ref = refs[N_MIX_S_IN + 2]
            ns_ref[...] = jnp.zeros_like(ns_ref)
    else:
        _mix_sample_body(*refs[:N_MIX_S_IN], *refs[N_MIX_S_IN + 1:])


def _mix_sample(z, alr, a2, ab, cw, gng, cmg, ws, cmb, p1, p2, s0, ns_all, layer):
    row0 = T_PROMPT // ROWS_S
    n_i = DEC_BATCH // SEQ_PER_BLK
    n_pass = DEPTH if layer == 0 else 1
    blk = lambda p, i: jnp.where(p == 0, i, n_i - 1)
    row_map = lambda p, i: row0 + blk(p, i)
    slot = lambda p, i: (layer + p, i, 0, 0, 0)
    small = (a2, ab, cw, gng, cmg, ws, cmb)
    state_blk = (None, SEQ_PER_BLK, GLA_HEADS, GLA_DK, GLA_DV)
    in_specs = _z_specs(ROWS_S, row_map) + [
        pl.BlockSpec((ROWS_S, LANES), lambda p, i: (row_map(p, i), 0)),
    ] + [_layer_spec(a, layer) for a in small] + [
        pl.BlockSpec((ROWS_S, CONV_CH), lambda p, i: (blk(p, i), 0)),
        pl.BlockSpec((ROWS_S, CONV_CH), lambda p, i: (blk(p, i), 0)),
        pl.BlockSpec(state_blk, lambda p, i: (layer, blk(p, i), 0, 0, 0)),
    ]
    args = (z, z, z, z, z, z, alr, *small, p1, p2, s0)
    assert len(args) == N_MIX_S_IN
    aliases = {}
    if layer > 0:
        in_specs.append(pl.BlockSpec(memory_space=pl.ANY))
        args += (ns_all,)
        aliases = {N_MIX_S_IN: 2}
    return pl.pallas_call(
        functools.partial(_mix_sample_kernel, layer=layer),
        grid=(n_pass, n_i),
        in_specs=in_specs,
        out_specs=[
            pl.BlockSpec((ROWS_S, ACT_COLS), lambda p, i: (blk(p, i), 0)),
            pl.BlockSpec((ROWS_S, CONV_CH), lambda p, i: (blk(p, i), 0)),
            pl.BlockSpec(state_blk, slot),
            pl.BlockSpec((ROWS_S, CM_CH), lambda p, i: (blk(p, i), 0)),
        ],
        out_shape=[
            jax.ShapeDtypeStruct((T_SAMPLE, ACT_COLS), BF16),
            jax.ShapeDtypeStruct((T_SAMPLE, CONV_CH), F32),
            jax.ShapeDtypeStruct((DEPTH, DEC_BATCH, GLA_HEADS, GLA_DK, GLA_DV), F32),
            jax.ShapeDtypeStruct((T_SAMPLE, CM_CH), F32),
        ],
        input_output_aliases=aliases,
        compiler_params=pltpu.CompilerParams(
            dimension_semantics=("arbitrary", "arbitrary"), vmem_limit_bytes=VMEM_LIMIT),
        name="mix_sample",
    )(*args)


def _delayed(gen, n_stages):
    for _ in range(n_stages):
        yield
    yield from gen


def _interleave(*stage_lists):
    pending = list(stage_lists)
    while pending:
        for gen in list(pending):
            if next(gen, StopIteration) is StopIteration:
                pending.remove(gen)


def _outproj_main(i, rows, n_xa, actp_ref, acts_ref, gates_ref, xa_ref, xb_ref, pa_ref, pb_ref,
                  pc_ref, wo_ref, n2_ref, xo_ref, xn_ref, split_ref):
    acts = jnp.where(i < T_PROMPT // TM_OUT, actp_ref[rows, :], acts_ref[rows, :])
    x_in = jnp.where(i < n_xa, xa_ref[rows, :], xb_ref[rows, :])
    yb = _dot(acts[:, CONV_CH:CONV_CH + GLA_V], pb_ref[...])
    ga = _sigmoid(gates_ref[rows, COL_GA:COL_GA + D_MODEL])
    ya = _dot(acts[:, :CONV_CH], pa_ref[...])
    gb = _sigmoid(gates_ref[rows, COL_GB:COL_GB + D_MODEL])
    yield
    yc = _dot(acts[:, CONV_CH + GLA_V:], pc_ref[...])
    gc = _sigmoid(gates_ref[rows, COL_GC:COL_GC + D_MODEL])
    mix = ga * ya + gb * yb
    yield
    mix = mix + gc * yc
    x = x_in + _dot(mix.astype(BF16), wo_ref[...])
    xo_ref[rows, :] = x
    yield
    xn = _rms(x, n2_ref[...])
    xn_ref[rows, :] = xn
    hi, lo = _split_bf16(xn)
    yield
    split_ref[0, rows, :] = hi
    split_ref[1, rows, :] = lo


def _outproj_route(live, split_ref, wrh_ref, wrl_ref, rb_ref, route_ref, seg_ref, cnt_ref):
    tm = TM_OUT
    hi = split_ref[0]
    lo = split_ref[1]
    logits = (_dot_nt(wrh_ref[...], hi) + _dot_nt(wrh_ref[...], lo) + _dot_nt(wrl_ref[...], hi)
              + rb_ref[...])
    yield
    n_sub = EXPERTS_PER_GROUP
    sub = lax.broadcasted_iota(jnp.int32, (n_sub, tm), 0).astype(F32)
    neg = jnp.float32(-jnp.inf)
    lg = logits[0:MOE_GROUPS]
    gmax = jnp.max(lg, axis=0, keepdims=True)
    grp = jnp.min(jnp.where(lg == gmax, sub, 1e9), axis=0, keepdims=True)
    p_grp = 1.0 / jnp.sum(jnp.exp(lg - gmax), axis=0, keepdims=True)
    yield
    le = logits[MOE_GROUPS:MOE_GROUPS + n_sub]
    for g in range(1, MOE_GROUPS):
        le = jnp.where(grp == g, logits[MOE_GROUPS + g * n_sub:MOE_GROUPS + (g + 1) * n_sub], le)
    v1 = jnp.max(le, axis=0, keepdims=True)
    i1 = jnp.min(jnp.where(le == v1, sub, 1e9), axis=0, keepdims=True)
    le2 = jnp.where(sub == i1, neg, le)
    v2 = jnp.max(le2, axis=0, keepdims=True)
    i2 = jnp.min(jnp.where(le2 == v2, sub, 1e9), axis=0, keepdims=True)
    yield
    t = jnp.exp(v2 - v1)
    g1 = p_grp / (1.0 + t)
    g2 = p_grp * t / (1.0 + t)
    gate_t = jnp.where(sub == i1, g1, jnp.where(sub == i2, g2, 0.0))

    onehot_t = jnp.where(sub == grp, 1.0, 0.0)
    csum = jnp.where(live, jnp.sum(onehot_t, axis=1, keepdims=True), 0.0)
    rr = lax.broadcasted_iota(jnp.int32, (tm, tm), 0)
    cc = lax.broadcasted_iota(jnp.int32, (tm, tm), 1)
    earlier = jnp.where(rr < cc, 1.0, 0.0).astype(BF16)
    padded = jnp.concatenate([onehot_t, jnp.zeros_like(onehot_t)], axis=0).astype(BF16)
    same_before = _dot(padded, earlier)[0:n_sub]
    yield
    lower = jnp.sum(jnp.where(sub < grp, csum, 0.0), axis=0, keepdims=True)
    lrank = lower + jnp.sum(onehot_t * same_before, axis=0, keepdims=True)
    route_ref[0:n_sub, :] = gate_t
    route_ref[n_sub:, :] = jnp.broadcast_to(lrank, (n_sub, tm))
    carry = cnt_ref[...]
    lane = lax.broadcasted_iota(jnp.int32, (n_sub, LANES), 1)
    seg_ref[...] = jnp.where(lane == SEG_START, carry,
                             jnp.where(lane == SEG_LEN, csum, 0.0)).astype(jnp.int32)
    cnt_ref[...] = carry + csum


def _outproj_kernel(actp_ref, acts_ref, gates_ref, xa_ref, xb_ref, pa_ref, pb_ref, pc_ref, wo_ref,
                    n2_ref, wrh_ref, wrl_ref, rb_ref, xo_ref, xn_ref, route_ref, seg_ref, cnt_ref,
                    split_ref, *, n_xa):
    i = pl.program_id(0)

    @pl.when(i == 0)
    def _():
        cnt_ref[...] = jnp.zeros_like(cnt_ref)
        split_ref[...] = jnp.zeros_like(split_ref)

    tile = jnp.minimum(i, pl.num_programs(0) - 2)
    part = TM_OUT // OUT_PARTS
    _interleave(
        _delayed(_outproj_route(i >= 1, split_ref, wrh_ref, wrl_ref, rb_ref, route_ref, seg_ref,
                                cnt_ref), ROUTE_DELAY),
        *[_delayed(_outproj_main(tile, pl.ds(p * part, part), n_xa, actp_ref, acts_ref, gates_ref,
                                 xa_ref, xb_ref, pa_ref, pb_ref, pc_ref, wo_ref, n2_ref, xo_ref,
                                 xn_ref, split_ref), p * OUT_STAGGER)
          for p in range(OUT_PARTS)])


def _lane_prefix(v):
    rr = lax.broadcasted_iota(jnp.int32, (LANES, LANES), 0)
    cc = lax.broadcasted_iota(jnp.int32, (LANES, LANES), 1)
    earlier = jnp.where(rr < cc, 1.0, 0.0).astype(BF16)
    return _dot(jnp.broadcast_to(v, (8, LANES)).astype(BF16), earlier)[0:1]


def _outproj(acts_p, acts_s, z, xa, xb, pa, pb, pc, wo, n2, wrh, wrl, rb, layer):
    m = T_ALL
    nt = m // TM_OUT
    row = lambda i: (jnp.minimum(i, nt - 1), 0)
    routed = lambda i: (jnp.maximum(i - 1, 0), 0, 0)
    _, act_specs = _two_part_specs(TM_OUT, acts_p, acts_s)
    n_xa, x_specs = _two_part_specs(TM_OUT, xa, xb)
    weights = (pa, pb, pc, wo, n2, wrh, wrl, rb)
    return pl.pallas_call(
        functools.partial(_outproj_kernel, n_xa=n_xa),
        grid=(nt + 1,),
        in_specs=act_specs + [pl.BlockSpec((TM_OUT, 3 * D_MODEL), row)] + x_specs
        + [_layer_spec(w, layer) for w in weights],
        out_specs=[
            pl.BlockSpec((TM_OUT, D_MODEL), row),
            pl.BlockSpec((TM_OUT, D_MODEL), row),
            pl.BlockSpec((None, ROUTE_ROWS, TM_OUT), routed),
            pl.BlockSpec((None, 8, LANES), routed),
            pl.BlockSpec((8, LANES), lambda i: (0, 0)),
        ],
        scratch_shapes=[pltpu.VMEM((2, TM_OUT, D_MODEL), BF16)],
        out_shape=[
            jax.ShapeDtypeStruct((m, D_MODEL), F32),
            jax.ShapeDtypeStruct((m, D_MODEL), F32),
            jax.ShapeDtypeStruct((m // TM_OUT, ROUTE_ROWS, TM_OUT), F32),
            jax.ShapeDtypeStruct((m // TM_OUT, 8, LANES), jnp.int32),
            jax.ShapeDtypeStruct((8, LANES), F32),
        ],
        compiler_params=pltpu.CompilerParams(
            dimension_semantics=("arbitrary",), vmem_limit_bytes=VMEM_LIMIT),
        name="outproj",
    )(acts_p, acts_s, z, xa, xb, *weights)


PLAN_GROUP, PLAN_USED, PLAN_END, PLAN_BLOCKS, PLAN_BASE = 0, 1, 2, 3, 4


def _plan_kernel(cnt_ref, plan_ref):
    sub_i = lax.broadcasted_iota(jnp.int32, (8, LANES), 0)
    lane8_i = lax.broadcasted_iota(jnp.int32, (8, LANES), 1)
    cnt = jnp.sum(jnp.where(sub_i == lane8_i, cnt_ref[...], 0.0), axis=0, keepdims=True)
    blocks = jnp.floor((cnt + (MOE_BLK - 1)) * (1.0 / MOE_BLK))
    start = _lane_prefix(blocks)
    end = start + blocks
    lane_i = lax.broadcasted_iota(jnp.int32, (1, LANES), 1)
    lane = lane_i.astype(F32)
    grp_of_blk = jnp.zeros((1, LANES), F32)
    for g in range(MOE_GROUPS):
        end_g = jnp.sum(jnp.where(lane_i == g, end, 0.0), axis=-1, keepdims=True)
        grp_of_blk += jnp.where(lane >= end_g, 1.0, 0.0)
    grp_of_blk = jnp.minimum(grp_of_blk, MOE_GROUPS - 1)
    n_used = jnp.sum(blocks, axis=-1, keepdims=True)
    row = lax.broadcasted_iota(jnp.int32, (8, LANES), 0)
    plan_ref[...] = jnp.where(
        row == PLAN_GROUP, grp_of_blk,
        jnp.where(row == PLAN_USED, n_used,
                  jnp.where(row == PLAN_END, end,
                            jnp.where(row == PLAN_BLOCKS, blocks,
                                      jnp.where(row == PLAN_BASE, start * MOE_BLK, 0.0))))
    ).astype(jnp.int32)


def _plan(cnt):
    return pl.pallas_call(
        _plan_kernel,
        out_shape=jax.ShapeDtypeStruct((8, LANES), jnp.int32),
        name="plan",
    )(cnt)


SEG_START, SEG_LEN = 0, 1


def _segment_copies(seg_ref, base_ref, make_copy):
    local = 0
    for g in range(MOE_GROUPS):
        n = seg_ref[g, SEG_LEN]
        first = base_ref[g] + seg_ref[g, SEG_START]
        k = TM_ROW
        while k >= 1:
            done = n & ~(2 * k - 1)
            @pl.when((n & k) != 0)
            def _():
                make_copy(local + done, first + done, k).start()
            k //= 2
        local = local + n


def _perm_matrix(route_ref):
    row = lax.broadcasted_iota(jnp.int32, (TM_ROW, TM_ROW), 0).astype(F32)
    return jnp.where(row == route_ref[ROUTE_LRANK:ROUTE_LRANK + 1, :], 1.0, 0.0).astype(BF16)


SUB = 8
SUB_X = D_MODEL // 2 // LANES
SUB_GATE = SUB_X
HI_MASK = -65536


def _rows(first_row, n_rows):
    return pl.ds(pl.multiple_of(first_row * SUB, SUB), n_rows * SUB)


def _sublane(s, n_rows):
    return pl.ds(s, n_rows, stride=SUB)


def _scatter_kernel(nu_ref, end_ref, nb_ref, base_ref, seg_ref, route_ref, xn_ref, xs_ref,
                    zbuf, sbuf, sems):
    i = pl.program_id(0)
    n = pl.num_programs(0)
    sem = sems.at[0]

    def zero_block(b):
        return pltpu.make_async_copy(zbuf, xs_ref.at[_rows(b * MOE_BLK, MOE_BLK)], sem)

    def each_unfilled_block(fn):
        for g in range(MOE_GROUPS):
            @pl.when(nb_ref[g] > 0)
            def _():
                fn(zero_block(end_ref[g] - 1))
        for b in range(T_ALL // MOE_BLK, N_BLK):
            @pl.when(b >= nu_ref[0])
            def _():
                fn(zero_block(b))

    @pl.when(i == 0)
    def _():
        zbuf[...] = jnp.zeros_like(zbuf)
        sbuf[...] = jnp.zeros_like(sbuf)
        each_unfilled_block(lambda c: c.start())
        each_unfilled_block(lambda c: c.wait())

    def tile_done(s):
        pltpu.make_async_copy(sbuf.at[s], xs_ref.at[_rows(0, TM_ROW)], sems.at[s]).wait()

    @pl.when(i >= 2)
    def _():
        for t in range(ROW_TILES):
            tile_done((i % 2) * ROW_TILES + t)

    def sort_tile(t):
        slot = (i % 2) * ROW_TILES + t
        route = route_ref.at[t]
        perm = _perm_matrix(route)
        xn = xn_ref[t * TM_ROW:(t + 1) * TM_ROW, :]
        xs = lax.bitcast_convert_type(_dot(perm, xn.astype(BF16)), jnp.int32)
        yield
        half = D_MODEL // 2
        for s in range(SUB_X):
            hi = xs[:, s * LANES:(s + 1) * LANES] & HI_MASK
            lo = lax.shift_right_logical(xs[:, half + s * LANES:half + (s + 1) * LANES], 16)
            sbuf[slot, _sublane(s, TM_ROW), :] = hi | lo
        yield
        gate_t = route[0:EXPERTS_PER_GROUP, :]
        gate_t = jnp.concatenate(
            [gate_t, jnp.zeros((LANES - EXPERTS_PER_GROUP, TM_ROW), F32)], axis=0)
        g1 = gate_t.astype(BF16)
        r1 = gate_t - g1.astype(F32)
        g2 = r1.astype(BF16)
        g3 = (r1 - g2.astype(F32)).astype(BF16)
        sbuf[slot, _sublane(SUB_GATE, TM_ROW), :] = lax.bitcast_convert_type(
            _dot_nt(perm, g1) + _dot_nt(perm, g2) + _dot_nt(perm, g3), jnp.int32)

    _interleave(*[_delayed(sort_tile(t), t) for t in range(ROW_TILES)])

    for t in range(ROW_TILES):
        def make_copy(src_row, dst_row, k, slot=(i % 2) * ROW_TILES + t):
            return pltpu.make_async_copy(sbuf.at[slot, _rows(src_row, k)],
                                         xs_ref.at[_rows(dst_row, k)], sems.at[slot])
        _segment_copies(seg_ref.at[t], base_ref, make_copy)

    @pl.when(i == n - 1)
    def _():
        for t in range(ROW_TILES):
            tile_done((i % 2) * ROW_TILES + t)

            @pl.when(n > 1)
            def _():
                tile_done((1 - i % 2) * ROW_TILES + t)


def _scatter(seg, route, xn, n_used, grp_end, grp_blocks, grp_base):
    m = xn.shape[0]
    step_rows = ROW_TILES * TM_ROW
    grid_spec = pltpu.PrefetchScalarGridSpec(
        num_scalar_prefetch=4,
        grid=(m // step_rows,),
        in_specs=[
            pl.BlockSpec((ROW_TILES, 8, LANES), lambda i, *_: (i, 0, 0), memory_space=pltpu.SMEM),
            pl.BlockSpec((ROW_TILES, ROUTE_ROWS, TM_ROW), lambda i, *_: (i, 0, 0)),
            pl.BlockSpec((step_rows, D_MODEL), lambda i, *_: (i, 0)),
        ],
        out_specs=pl.BlockSpec(memory_space=pl.ANY),
        scratch_shapes=[pltpu.VMEM((MOE_BLK * SUB, LANES), jnp.int32),
                        pltpu.VMEM((2 * ROW_TILES, TM_ROW * SUB, LANES), jnp.int32),
                        pltpu.SemaphoreType.DMA((2 * ROW_TILES,))],
    )
    return pl.pallas_call(
        _scatter_kernel,
        grid_spec=grid_spec,
        out_shape=jax.ShapeDtypeStruct((N_SORTED * SUB, LANES), jnp.int32),
        compiler_params=pltpu.CompilerParams(
            dimension_semantics=("arbitrary",), vmem_limit_bytes=VMEM_LIMIT),
        name="scatter",
    )(n_used, grp_end, grp_blocks, grp_base, seg, route, xn)


def _ffn_kernel(bg_ref, nu_ref, xs_ref, w1_ref, w3_ref, w2_ref, y_ref):
    del bg_ref
    b = pl.program_id(0)

    def block_inputs():
        packed = [xs_ref[_sublane(s, MOE_BLK), :] for s in range(SUB_X)]
        x = jnp.concatenate(
            [lax.bitcast_convert_type(u & HI_MASK, F32).astype(BF16) for u in packed]
            + [lax.bitcast_convert_type(lax.shift_left(u, 16), F32).astype(BF16) for u in packed],
            axis=1)
        return x, lax.bitcast_convert_type(xs_ref[_sublane(SUB_GATE, MOE_BLK), :], F32)

    def store(y):
        for s in range(SUB):
            y_ref[_sublane(s, MOE_BLK), :] = y[:, s * LANES:(s + 1) * LANES]

    @pl.when(b < nu_ref[0])
    def _():
        x, gates = block_inputs()
        n_e = EXPERTS_PER_GROUP
        h1 = [_dot(x, w1_ref[e]) for e in range(n_e)]
        h3 = [_dot(x, w3_ref[e]) for e in range(n_e)]
        hs = []
        for e in range(n_e):
            ge = gates[:, e:e + 1]
            hs.append(jnp.where(ge > 0.0, h1[e] * _sigmoid(h1[e]) * h3[e] * ge, 0.0).astype(BF16))
        hcat = jnp.concatenate(hs, axis=1)
        store(_dot(hcat, w2_ref[...].reshape(n_e * D_EXPERT, D_MODEL)))

    @pl.when(b >= nu_ref[0])
    def _():
        y_ref[...] = jnp.zeros_like(y_ref)


def _ffn(blk_group, n_used, xs, w1, w3, w2):
    grouped = (MOE_GROUPS, EXPERTS_PER_GROUP)
    w1, w3, w2 = (w.reshape(grouped + w.shape[1:]) for w in (w1, w3, w2))
    wmap = lambda b, bg, nu: (bg[b], 0, 0, 0)
    grid_spec = pltpu.PrefetchScalarGridSpec(
        num_scalar_prefetch=2,
        grid=(N_BLK,),
        in_specs=[
            pl.BlockSpec((MOE_BLK * SUB, LANES), lambda b, bg, nu: (b, 0)),
            pl.BlockSpec((None, EXPERTS_PER_GROUP, D_MODEL, D_EXPERT), wmap),
            pl.BlockSpec((None, EXPERTS_PER_GROUP, D_MODEL, D_EXPERT), wmap),
            pl.BlockSpec((None, EXPERTS_PER_GROUP, D_EXPERT, D_MODEL), wmap),
        ],
        out_specs=pl.BlockSpec((MOE_BLK * SUB, LANES), lambda b, bg, nu: (b, 0)),
    )
    return pl.pallas_call(
        _ffn_kernel,
        grid_spec=grid_spec,
        out_shape=jax.ShapeDtypeStruct((N_SORTED * SUB, LANES), F32),
        compiler_params=pltpu.CompilerParams(
            dimension_semantics=("arbitrary",), vmem_limit_bytes=VMEM_LIMIT),
        name="ffn",
    )(blk_group, n_used, xs, w1, w3, w2)


def _combine_kernel(base_ref, seg_ref, segn_ref, route_ref, ys_hbm, x_ref, g_ref, wa_ref, *refs,
                    final):
    outs, (buf, sem, stage) = refs[:-3], refs[-3:]
    i = pl.program_id(0)
    n = pl.num_programs(0)
    step_rows = ROW_TILES * TM_ROW

    def gather(seg, parity):
        for t in range(ROW_TILES):
            s = parity * ROW_TILES + t

            def make_copy(buf_row, ys_row, k, s=s):
                return pltpu.make_async_copy(ys_hbm.at[_rows(ys_row, k)],
                                             buf.at[s, _rows(buf_row, k)], sem.at[s])
            _segment_copies(seg.at[t], base_ref, make_copy)

    @pl.when(i == 0)
    def _():
        gather(seg_ref, 0)

    @pl.when(i + 1 < n)
    def _():
        gather(segn_ref, 1 - i % 2)

    for t in range(ROW_TILES):
        slot = (i % 2) * ROW_TILES + t
        pltpu.make_async_copy(ys_hbm.at[_rows(0, TM_ROW)], buf.at[slot], sem.at[slot]).wait()

    def tile_stages(t):
        slot = (i % 2) * ROW_TILES + t
        rows = slice(t * TM_ROW, (t + 1) * TM_ROW)
        tn = (((0,), (0,)), ((), ()))
        perm = _perm_matrix(route_ref.at[t])
        cols = []
        for s in range(SUB):
            hi, lo = _split_bf16(buf[slot, _sublane(s, TM_ROW), :])
            cols.append(lax.dot_general(perm, hi, tn, preferred_element_type=F32)
                        + lax.dot_general(perm, lo, tn, preferred_element_type=F32))
            if s % 4 == 3:
                yield
        y = x_ref[rows, :] + jnp.concatenate(cols, axis=1)
        if not final:
            outs[0][rows, :] = y
            yield
            _norm_and_lowrank(y, g_ref, wa_ref, outs[1].at[rows, :], outs[2].at[rows, :])
        else:
            yield
            stage[rows, :] = _rms(y, g_ref[...])

    _interleave(*[_delayed(tile_stages(t), t) for t in range(ROW_TILES)])

    if final:
        @pl.when(i < T_PROMPT // step_rows)
        def _():
            outs[0][...] = stage[...]

        @pl.when(i >= T_PROMPT // step_rows)
        def _():
            outs[1][...] = stage[...]


def _combine(seg, route, grp_base, ys, x, g, wa, final):
    m = x.shape[0]
    step_rows = ROW_TILES * TM_ROW
    nt = m // step_rows
    n_p = T_PROMPT // step_rows
    smem = functools.partial(pl.BlockSpec, (ROW_TILES, 8, LANES), memory_space=pltpu.SMEM)
    tile = (step_rows, D_MODEL)
    if final:
        out_specs = [pl.BlockSpec(tile, lambda i, *_: (jnp.minimum(i, n_p - 1), 0)),
                     pl.BlockSpec(tile, lambda i, *_: (jnp.maximum(i - n_p, 0), 0))]
        out_shape = [jax.ShapeDtypeStruct((T_PROMPT, D_MODEL), F32),
                     jax.ShapeDtypeStruct((T_SAMPLE, D_MODEL), F32)]
    else:
        out_specs = [pl.BlockSpec(tile, lambda i, *_: (i, 0)),
                     pl.BlockSpec(tile, lambda i, *_: (i, 0)),
                     pl.BlockSpec((step_rows, LANES), lambda i, *_: (i, 0))]
        out_shape = [jax.ShapeDtypeStruct((m, D_MODEL), F32),
                     jax.ShapeDtypeStruct((m, D_MODEL), BF16),
                     jax.ShapeDtypeStruct((m, LANES), BF16)]
    grid_spec = pltpu.PrefetchScalarGridSpec(
        num_scalar_prefetch=1,
        grid=(nt,),
        in_specs=[
            smem(lambda i, *_: (i, 0, 0)),
            smem(lambda i, *_: (jnp.minimum(i + 1, nt - 1), 0, 0)),
            pl.BlockSpec((ROW_TILES, ROUTE_ROWS, TM_ROW), lambda i, *_: (i, 0, 0)),
            pl.BlockSpec(memory_space=pl.ANY),
            pl.BlockSpec(tile, lambda i, *_: (i, 0)),
            pl.BlockSpec(g.shape, lambda i, *_: (0, 0)),
            pl.BlockSpec(wa.shape, lambda i, *_: (0, 0)),
        ],
        out_specs=out_specs,
        scratch_shapes=[pltpu.VMEM((2 * ROW_TILES, TM_ROW * SUB, LANES), F32),
                        pltpu.SemaphoreType.DMA((2 * ROW_TILES,)),
                        pltpu.VMEM(tile if final else (8, LANES), F32)],
    )
    return pl.pallas_call(
        functools.partial(_combine_kernel, final=final),
        grid_spec=grid_spec,
        out_shape=out_shape,
        compiler_params=pltpu.CompilerParams(
            dimension_semantics=("arbitrary",), vmem_limit_bytes=VMEM_LIMIT),
        name="combine",
    )(grp_base, seg, seg, route, ys, x, g, wa)


def _prep_weights(w_in, gla_a2, cm_ws, cm_b, router_group_w, router_group_b,
                  router_expert_w, router_expert_b):
    off = {}
    o = 0
    for name, n in (("h", 512), ("cg", 512), ("bg", 512), ("q", 512), ("k", 512), ("v", 1024),
                    ("r", 1024), ("alr", 16), ("u", 512), ("vv", 512), ("ga", 1024),
                    ("gb", 1024), ("gc", 1024)):
        off[name] = (o, n)
        o += n
    runs = (("ga", "gc"), ("v", "r"), ("q", "k"), ("h", "cg"), ("bg", "bg"), ("u", "vv"))
    w_t = jnp.swapaxes(w_in, 1, 2)
    w_z = jnp.concatenate([w_t[:, off[a][0]:off[b][0] + off[b][1]] for a, b in runs],
                          axis=1).astype(BF16)
    assert w_z.shape[1] == Z_COLS
    a0 = off["alr"][0]
    w_alr = jnp.pad(w_t[:, a0:a0 + GLA_LOWRANK],
                    ((0, 0), (0, LANES - GLA_LOWRANK), (0, 0))).astype(BF16)
    a2 = jnp.pad(gla_a2, ((0, 0), (0, LANES - GLA_LOWRANK), (0, 0))).astype(BF16)
    ws_p = jnp.tril(cm_ws).astype(BF16)
    small = jnp.tril(cm_ws[:, :, :DEC_SEQ, :DEC_SEQ])
    eye = jnp.eye(SEQ_PER_BLK, dtype=F32)
    ws_s = jnp.einsum("ij,lgab->lgiajb", eye, small).reshape(
        DEPTH, CM_GROUPS, ROWS_S, ROWS_S).astype(BF16)
    cmb_p = jnp.broadcast_to(jnp.transpose(cm_b, (0, 2, 1))[:, :, :, None],
                             (DEPTH, CM_CHUNK, CM_GROUPS, CM_GCH)).reshape(DEPTH, CM_CHUNK, CM_CH)
    cmb_s = jnp.tile(cmb_p[:, :DEC_SEQ], (1, SEQ_PER_BLK, 1))
    pad = LANES - MOE_GROUPS - N_EXPERTS
    w_r = jnp.pad(jnp.swapaxes(jnp.concatenate([router_group_w, router_expert_w], axis=-1), 1, 2),
                  ((0, 0), (0, pad), (0, 0)))
    w_r_hi = w_r.astype(BF16)
    w_r_lo = (w_r - w_r_hi.astype(F32)).astype(BF16)
    r_b = jnp.pad(jnp.concatenate([router_group_b, router_expert_b], axis=-1),
                  ((0, 0), (0, pad)))[:, :, None]
    return w_z, w_alr, a2, ws_p, ws_s, cmb_p, cmb_s, w_r_hi, w_r_lo, r_b


def kernel(x_prompt, x_sample, state_conv, state_gla, norm1_g, w_in, conv_w, gla_a2, gla_a_b,
           gla_norm_g, cm_norm_g, cm_ws, cm_b, proj_a, proj_b, proj_c, w_out, norm2_g,
           router_group_w, router_group_b, router_expert_w, router_expert_b,
           exp_w1, exp_w3, exp_w2, final_norm_g):
    (w_z, w_alr, a2, ws_p, ws_s, cmb_p, cmb_s, w_r_hi, w_r_lo, r_b) = _prep_weights(
        w_in, gla_a2, cm_ws, cm_b, router_group_w, router_group_b, router_expert_w,
        router_expert_b)
    pa, pb, pc, wo = (w.astype(BF16) for w in (proj_a, proj_b, proj_c, w_out))
    n1 = norm1_g[:, None, :]
    n2 = norm2_g[:, None, :]
    ab = gla_a_b[:, None, :]
    gng = gla_norm_g[:, None, :]
    cmg = cm_norm_g.reshape(DEPTH, 1, CM_CH)
    fg = final_norm_g[None, :]
    xa = x_prompt.reshape(T_PROMPT, D_MODEL)
    xb = x_sample.reshape(T_SAMPLE, D_MODEL)
    gla_s = None
    conv_p, gla_p, conv_s, cmv_s = [], [], [], []
    xin, alr = _prenorm(xa, xb, n1, w_alr, 0)
    for l in range(DEPTH):
        z, w1, w3, w2 = _inproj(xin, w_z, exp_w1, exp_w3, exp_w2, l)
        acts_p, nconv, ngla = _mix_prompt(z, alr, a2, ab, conv_w, gng, cmg, ws_p, cmb_p, l)
        sc = state_conv[l]
        p2 = jnp.pad(sc, ((0, 0), (0, DEC_SEQ - 2), (0, 0))).reshape(T_SAMPLE, CONV_CH)
        p1 = jnp.pad(sc[:, 1:2], ((0, 0), (0, DEC_SEQ - 1), (0, 0))).reshape(T_SAMPLE, CONV_CH)
        acts_s, cin_s, gla_s, vrows = _mix_sample(z, alr, a2, ab, conv_w, gng, cmg, ws_s, cmb_s,
                                                  p1, p2, state_gla, gla_s, l)
        conv_p.append(nconv)
        gla_p.append(ngla)
        conv_s.append(cin_s.reshape(DEC_BATCH, DEC_SEQ, CONV_CH)[:, DEC_SEQ - (CONV_K - 1):])
        cmv_s.append(vrows.reshape(DEC_BATCH, DEC_SEQ, CM_CH))

        x, xn, route, seg, cnt = _outproj(acts_p, acts_s, z, xa, xb, pa, pb, pc, wo, n2,
                                          w_r_hi, w_r_lo, r_b, l)
        plan = _plan(cnt)
        n_used = plan[PLAN_USED, :1]
        grp_base = plan[PLAN_BASE, :MOE_GROUPS]
        xs = _scatter(seg, route, xn, n_used, plan[PLAN_END, :MOE_GROUPS],
                      plan[PLAN_BLOCKS, :MOE_GROUPS], grp_base)
        ys = _ffn(plan[PLAN_GROUP, :N_BLK], n_used, xs, w1, w3, w2)
        if l == DEPTH - 1:
            out = _combine(seg, route, grp_base, ys, x, fg, w_alr[l], True)
        else:
            xa, xin, alr = _combine(seg, route, grp_base, ys, x, n1[l + 1], w_alr[l + 1], False)
            xb = xa

    y_prompt = out[0].reshape(BATCH, SEQ, D_MODEL)
    y_sample = out[1].reshape(DEC_BATCH, DEC_SEQ, D_MODEL)
    return (y_prompt, y_sample, jnp.stack(conv_p), jnp.stack(gla_p), jnp.stack(conv_s),
            gla_s, jnp.stack(cmv_s))
```

```python
import functools

import jax
import jax.numpy as jnp
from jax import lax
from jax.experimental import pallas as pl
from jax.experimental.pallas import tpu as pltpu

F32 = jnp.float32
BF16 = jnp.bfloat16

D_MODEL = 1024
BATCH = 8
SEQ = 2048
DEPTH = 2
DEC_BATCH = 128
DEC_SEQ = 8
CONV_K = 3
CONV_CH = 512
GLA_HEADS = 4
GLA_DK = 128
GLA_DV = 256
GLA_QK = GLA_HEADS * GLA_DK
GLA_V = GLA_HEADS * GLA_DV
GLA_LOWRANK = 16
GLA_TAU = 16.0
GLA_CHUNK = 64
CM_GROUPS = 4
CM_CHUNK = 128
CM_GCH = 128
CM_CH = 512
MOE_GROUPS = 8
EXPERTS_PER_GROUP = 8
N_EXPERTS = 64
D_EXPERT = 256
EPS = 1e-6

LANES = 128
T_PROMPT = BATCH * SEQ
T_SAMPLE = DEC_BATCH * DEC_SEQ
T_ALL = T_PROMPT + T_SAMPLE

COL_GA, COL_GB, COL_GC = 0, 1024, 2048
COL_V, COL_R, COL_Q, COL_K = 3072, 4096, 5120, 5632
COL_H, COL_CG, COL_BG, COL_U, COL_VV = 6144, 6656, 7168, 7680, 8192
Z_COLS = 8704
ACT_COLS = 2048
ROUTE_ROWS = 2 * EXPERTS_PER_GROUP
ROUTE_LRANK = EXPERTS_PER_GROUP

TM_PRE = 1024
TM_IN = 1024
TN_IN = Z_COLS // 2
TC_MIX = 256
SEQ_PER_BLK = 16
ROWS_S = SEQ_PER_BLK * DEC_SEQ
TM_OUT = 256
ROUTE_DELAY = 1
TM_ROW = TM_OUT
ROW_TILES = 4
MOE_BLK = 256
N_BLK = T_ALL // MOE_BLK + MOE_GROUPS
N_SORTED = N_BLK * MOE_BLK
VMEM_LIMIT = 56 * 1024 * 1024


def _sigmoid(x):
    return 0.5 * jnp.tanh(0.5 * x) + 0.5


def _gelu_tanh(x):
    c = 0.7978845608028654
    half = 0.5 * x
    return half + half * jnp.tanh(x * (c + (c * 0.044715) * (x * x)))


def _log_sigmoid(x):
    log2_e = 1.4426950408889634
    ln_2 = 0.6931471805599453
    return jnp.minimum(x, 0.0) - ln_2 * jnp.log2(1.0 + jnp.exp2(-log2_e * jnp.abs(x)))


def _rms(x, g):
    ms = jnp.mean(x * x, axis=-1, keepdims=True)
    return x * lax.rsqrt(ms + EPS) * g


def _split_bf16(x):
    hi = x.astype(BF16)
    lo = (x - hi.astype(F32)).astype(BF16)
    return hi, lo


def _dot(a, b):
    return jnp.dot(a, b, preferred_element_type=F32)


def _dot_nt(a, b):
    return lax.dot_general(a, b, (((1,), (1,)), ((), ())), preferred_element_type=F32)


def _layer_spec(arr, layer):
    nd = arr.ndim - 1
    return pl.BlockSpec((None,) + arr.shape[1:], lambda *g: (layer,) + (0,) * nd)


def _const_spec(arr):
    nd = arr.ndim
    return pl.BlockSpec(arr.shape, lambda *g: (0,) * nd)


def _two_part_specs(tile, xa, xb):
    n_a = xa.shape[0] // tile
    n_b = xb.shape[0] // tile
    return n_a, [
        pl.BlockSpec((tile, xa.shape[1]), lambda i, *_: (jnp.minimum(i, n_a - 1), 0)),
        pl.BlockSpec((tile, xb.shape[1]), lambda i, *_: (jnp.clip(i - n_a, 0, n_b - 1), 0)),
    ]


def _norm_and_lowrank(x, g_ref, wa_ref, xn_ref, a_ref):
    xn = _rms(x, g_ref[...]).astype(BF16)
    xn_ref[...] = xn
    a_ref[...] = _dot_nt(xn, wa_ref[...]).astype(BF16)


def _prenorm_kernel(xa_ref, xb_ref, g_ref, wa_ref, xn_ref, a_ref, *, n_a):
    x = jnp.where(pl.program_id(0) < n_a, xa_ref[...], xb_ref[...])
    _norm_and_lowrank(x, g_ref, wa_ref, xn_ref, a_ref)


def _prenorm(xa, xb, g, wa, layer):
    m = T_ALL
    n_a, x_specs = _two_part_specs(TM_PRE, xa, xb)
    return pl.pallas_call(
        functools.partial(_prenorm_kernel, n_a=n_a),
        grid=(m // TM_PRE,),
        in_specs=x_specs + [_layer_spec(g, layer), _layer_spec(wa, layer)],
        out_specs=[pl.BlockSpec((TM_PRE, D_MODEL), lambda i: (i, 0)),
                   pl.BlockSpec((TM_PRE, LANES), lambda i: (i, 0))],
        out_shape=[jax.ShapeDtypeStruct((m, D_MODEL), BF16),
                   jax.ShapeDtypeStruct((m, LANES), BF16)],
        compiler_params=pltpu.CompilerParams(
            dimension_semantics=("arbitrary",), vmem_limit_bytes=VMEM_LIMIT),
        name="prenorm",
    )(xa, xb, g, wa)


def _inproj_kernel(xn_ref, w_ref, z_ref):
    z_ref[...] = _dot_nt(xn_ref[...], w_ref[...]).astype(BF16)


def _inproj(xn, w, layer):
    m = xn.shape[0]
    return pl.pallas_call(
        _inproj_kernel,
        grid=(Z_COLS // TN_IN, m // TM_IN),
        in_specs=[
            pl.BlockSpec((TM_IN, D_MODEL), lambda j, i: (i, 0)),
            pl.BlockSpec((None, TN_IN, D_MODEL), lambda j, i: (layer, j, 0)),
        ],
        out_specs=pl.BlockSpec((TM_IN, TN_IN), lambda j, i: (i, j)),
        out_shape=jax.ShapeDtypeStruct((m, Z_COLS), BF16),
        compiler_params=pltpu.CompilerParams(
            dimension_semantics=("arbitrary", "arbitrary"), vmem_limit_bytes=VMEM_LIMIT),
        name="inproj",
    )(xn, w)


EXPERT_SHAPES = ((D_MODEL, D_EXPERT), (D_MODEL, D_EXPERT), (D_EXPERT, D_MODEL))


def _cast_job_specs(layer, n_steps):
    assert n_steps >= N_EXPERTS
    expert = lambda *g: jnp.minimum(g[0], N_EXPERTS - 1)
    return ([pl.BlockSpec((None, None) + s, lambda *g: (layer, expert(*g), 0, 0))
             for s in EXPERT_SHAPES],
            [pl.BlockSpec((None,) + s, lambda *g: (expert(*g), 0, 0)) for s in EXPERT_SHAPES],
            [jax.ShapeDtypeStruct((N_EXPERTS,) + s, BF16) for s in EXPERT_SHAPES])


def _cast_job(f32_refs, bf16_refs):
    for src, dst in zip(f32_refs, bf16_refs):
        dst[...] = src[...].astype(BF16)


def _gla_log_decay(alr_ref, a2_ref, ab_ref):
    la = _log_sigmoid(_dot(alr_ref[...], a2_ref[...]) + ab_ref[...]) * (1.0 / GLA_TAU)
    return _split_bf16(la)


def _masked_sum(mask, la_hi, la_lo):
    m = jnp.where(mask, 1.0, 0.0).astype(BF16)
    return _dot(m, la_hi) + _dot(m, la_lo)


def _gla_decay_terms(alr_ref, a2_ref, ab_ref, tril_mask, same_mask):
    la_hi, la_lo = _gla_log_decay(alr_ref, a2_ref, ab_ref)
    return (la_hi, la_lo, _masked_sum(tril_mask, la_hi, la_lo),
            _masked_sum(same_mask, la_hi, la_lo))


def _gla_out_gate(o, g_ref, r):
    return _rms(o, g_ref[...]) * (r * _sigmoid(r))


def _chunk_mlp_group(g, bgu_ref, vv_ref, cmg_ref, ws_ref, cmb_ref, n_chunks):
    sl = slice(g * CM_GCH, (g + 1) * CM_GCH)
    ug = _gelu_tanh(bgu_ref[:, CONV_CH + g * CM_GCH:CONV_CH + (g + 1) * CM_GCH])
    vg = _rms(_gelu_tanh(vv_ref[:, sl]).astype(F32), cmg_ref[:, sl])
    vgb = vg.astype(BF16)
    rows = []
    for j in range(n_chunks):
        rs = slice(j * CM_CHUNK, (j + 1) * CM_CHUNK)
        rows.append(_dot(ws_ref[g], vgb[rs]) + cmb_ref[:, sl])
    s = rows[0] if n_chunks == 1 else jnp.concatenate(rows, axis=0)
    return ug * s, vg


def _chunk_mlp(bgu_ref, vv_ref, cmg_ref, ws_ref, cmb_ref, n_chunks):
    parts = [_chunk_mlp_group(g, bgu_ref, vv_ref, cmg_ref, ws_ref, cmb_ref, n_chunks)
             for g in range(CM_GROUPS)]
    return (jnp.concatenate([p[0] for p in parts], axis=1),
            jnp.concatenate([p[1] for p in parts], axis=1))


N_Z_VIEWS = 7
N_MIX_W = 7
SEQ_PER_STEP = 4
SEQ_STAGGER = 3


def _mix_prompt_kernel(*refs):
    n_z = N_Z_VIEWS * SEQ_PER_STEP
    weights = refs[n_z:n_z + N_MIX_W]
    acts_ref, nconv_ref, ngla_ref, st_ref, carry_ref = refs[n_z + N_MIX_W:]

    @pl.when(pl.program_id(1) == 0)
    def _():
        st_ref[...] = jnp.zeros_like(st_ref)
        carry_ref[...] = jnp.zeros_like(carry_ref)

    rr = lax.broadcasted_iota(jnp.int32, (TC_MIX, TC_MIX), 0)
    cc = lax.broadcasted_iota(jnp.int32, (TC_MIX, TC_MIX), 1)
    tril = ((rr >> 6) == (cc >> 6)) & (cc <= rr)
    masks = (tril, jnp.where(tril, 1.0, 0.0).astype(BF16),
             jnp.where(rr - cc == 1, 1.0, 0.0).astype(BF16),
             jnp.where(rr - cc == 2, 1.0, 0.0).astype(BF16))

    _interleave(*[
        _delayed(_mix_prompt_seq(*refs[N_Z_VIEWS * s:N_Z_VIEWS * (s + 1)], *weights,
                                acts_ref.at[s], nconv_ref.at[s], st_ref.at[s], carry_ref.at[s],
                                masks),
                s * SEQ_STAGGER)
        for s in range(SEQ_PER_STEP)])

    @pl.when(pl.program_id(1) == pl.num_programs(1) - 1)
    def _():
        for s in range(SEQ_PER_STEP):
            for hd in range(GLA_HEADS):
                ngla_ref[s, 0, hd] = st_ref[s, hd].T


def _mix_prompt_seq(v_ref, r_ref, qk_ref, hcg_ref, bgu_ref, vv_ref, alr_ref,
                    a2_ref, ab_ref, cw_ref, gng_ref, cmg_ref, ws_ref, cmb_ref,
                    acts_ref, nconv_ref, st_ref, carry_ref, masks):
    tc = TC_MIX

    h = hcg_ref[:, :CONV_CH].astype(F32)
    cg = hcg_ref[:, CONV_CH:].astype(F32)
    bg = bgu_ref[:, :CONV_CH].astype(F32)
    cin = cg * h
    tril, tril_01, shift_1, shift_2 = masks
    cin_b = cin.astype(BF16)
    x1 = _dot(shift_1, cin_b)
    x2 = _dot(shift_2, cin_b)
    conv = x2 * cw_ref[0:1, :] + x1 * cw_ref[1:2, :] + cin * cw_ref[2:3, :]
    c0 = carry_ref[0:1, :]
    c1 = carry_ref[1:2, :]
    row8 = lax.broadcasted_iota(jnp.int32, (8, 1), 0)
    head = jnp.where(row8 == 0, c0 * cw_ref[0:1, :] + c1 * cw_ref[1:2, :],
                     jnp.where(row8 == 1, c1 * cw_ref[0:1, :], 0.0))
    conv = jnp.concatenate([conv[0:8] + head, conv[8:]], axis=0)
    acts_ref[:, 0:CONV_CH] = (bg * conv).astype(BF16)
    carry_ref[0:2, :] = cin[tc - 2:tc, :]
    nconv_ref[0] = cin[tc - 2:tc, :]
    yield

    la_hi, la_lo = _gla_log_decay(alr_ref, a2_ref, ab_ref)
    b = _dot(tril_01, la_hi) + _dot(tril_01, la_lo)
    yield
    n_chunks = tc // GLA_CHUNK
    b_last = [b[(c + 1) * GLA_CHUNK - 1:(c + 1) * GLA_CHUNK, :] for c in range(n_chunks)]
    bl = jnp.concatenate([jnp.broadcast_to(r_, (GLA_CHUNK, GLA_QK)) for r_ in b_last], axis=0)
    q = qk_ref[:, :GLA_QK].astype(F32) * (GLA_DK ** -0.5)
    k = qk_ref[:, GLA_QK:].astype(F32)
    q_t = (q * jnp.exp(b)).astype(BF16)
    k_t = (k * jnp.exp(-b)).astype(BF16)
    k_end = (k * jnp.exp(bl - b)).astype(BF16)
    yield
    states = [st_ref[hd] for hd in range(GLA_HEADS)]
    k_cols = [slice(hd * GLA_DK, (hd + 1) * GLA_DK) for hd in range(GLA_HEADS)]
    v_cols = [slice(hd * GLA_DV, (hd + 1) * GLA_DV) for hd in range(GLA_HEADS)]
    o_intra = []
    for hd in range(GLA_HEADS):
        att = jnp.where(tril, _dot_nt(q_t[:, k_cols[hd]], k_t[:, k_cols[hd]]), 0.0).astype(BF16)
        o_intra.append(_dot(att, v_ref[:, v_cols[hd]]))
    yield

    o_rows = [[] for _ in range(GLA_HEADS)]
    assert n_chunks == CM_GROUPS
    for c in range(n_chunks):
        rs = slice(c * GLA_CHUNK, (c + 1) * GLA_CHUNK)
        for hd in range(GLA_HEADS):
            ks = k_cols[hd]
            st = states[hd]
            o_rows[hd].append(o_intra[hd][rs] + _dot_nt(q_t[rs, ks], st.astype(BF16)))
            upd = lax.dot_general(v_ref[rs, v_cols[hd]], k_end[rs, ks], (((0,), (0,)), ((), ())),
                                  preferred_element_type=F32)
            states[hd] = jnp.exp(b_last[c][:, ks]) * st + upd
        us, _ = _chunk_mlp_group(c, bgu_ref, vv_ref, cmg_ref, ws_ref, cmb_ref, tc // CM_CHUNK)
        acts_ref[:, CONV_CH + GLA_V + c * CM_GCH:CONV_CH + GLA_V + (c + 1) * CM_GCH] = (
            us.astype(BF16))
        yield

    for hd in range(GLA_HEADS):
        o = jnp.concatenate(o_rows[hd], axis=0)
        r = r_ref[:, v_cols[hd]]
        acts_ref[:, CONV_CH + hd * GLA_DV:CONV_CH + (hd + 1) * GLA_DV] = (
            _gla_out_gate(o, gng_ref, r).astype(BF16))
        st_ref[hd] = states[hd]
        if hd % 2 == 1:
            yield


def _z_specs(rows, row_map):
    def spec(width, col):
        blk = col // width
        return pl.BlockSpec((rows, width), lambda *g: (row_map(*g), blk))
    return [spec(1024, COL_V), spec(1024, COL_R), spec(1024, COL_Q), spec(1024, COL_H),
            spec(1024, COL_BG), spec(512, COL_VV)]


def _mix_prompt(z, alr, a2, ab, cw, gng, cmg, ws, cmb, layer):
    nt = SEQ // TC_MIX
    nb = BATCH // SEQ_PER_STEP
    small = (a2, ab, cw, gng, cmg, ws, cmb)
    assert len(small) == N_MIX_W
    in_specs, args = [], []
    for s in range(SEQ_PER_STEP):
        row_map = lambda b, c, s=s: (b + s * nb) * nt + c
        in_specs += _z_specs(TC_MIX, row_map) + [
            pl.BlockSpec((TC_MIX, LANES), lambda b, c, row_map=row_map: (row_map(b, c), 0))]
        args += [z] * (N_Z_VIEWS - 1) + [alr]
    in_specs += [_layer_spec(a, layer) for a in small]
    acts, nconv, ngla = pl.pallas_call(
        _mix_prompt_kernel,
        grid=(nb, nt),
        in_specs=in_specs,
        out_specs=[
            pl.BlockSpec((SEQ_PER_STEP, TC_MIX, ACT_COLS), lambda b, c: (0, b * nt + c, 0)),
            pl.BlockSpec((SEQ_PER_STEP, 1, CONV_K - 1, CONV_CH), lambda b, c: (0, b, 0, 0)),
            pl.BlockSpec((SEQ_PER_STEP, 1, GLA_HEADS, GLA_DK, GLA_DV),
                         lambda b, c: (0, b, 0, 0, 0)),
        ],
        out_shape=[
            jax.ShapeDtypeStruct((SEQ_PER_STEP, T_PROMPT // SEQ_PER_STEP, ACT_COLS), BF16),
            jax.ShapeDtypeStruct((SEQ_PER_STEP, nb, CONV_K - 1, CONV_CH), F32),
            jax.ShapeDtypeStruct((SEQ_PER_STEP, nb, GLA_HEADS, GLA_DK, GLA_DV), F32),
        ],
        scratch_shapes=[pltpu.VMEM((SEQ_PER_STEP, GLA_HEADS, GLA_DV, GLA_DK), F32),
                        pltpu.VMEM((SEQ_PER_STEP, 8, CONV_CH), F32)],
        compiler_params=pltpu.CompilerParams(
            dimension_semantics=("arbitrary", "arbitrary"), vmem_limit_bytes=VMEM_LIMIT),
        name="mix_prompt",
    )(*args, *small)
    return (acts.reshape(T_PROMPT, ACT_COLS), nconv.reshape(BATCH, CONV_K - 1, CONV_CH),
            ngla.reshape(BATCH, GLA_HEADS, GLA_DK, GLA_DV))


def _mix_sample_body(v_ref, r_ref, qk_ref, hcg_ref, bgu_ref, vv_ref, alr_ref,
                       a2_ref, ab_ref, cw_ref, gng_ref, cmg_ref, ws_ref, cmb_ref,
                       p1_ref, p2_ref, s0_ref,
                       acts_ref, cin_ref, ns_ref, vrow_ref):
    n = ROWS_S

    pos = lax.broadcasted_iota(jnp.int32, (n, 1), 0) & (DEC_SEQ - 1)
    h = hcg_ref[:, :CONV_CH].astype(F32)
    cg = hcg_ref[:, CONV_CH:].astype(F32)
    bg = bgu_ref[:, :CONV_CH].astype(F32)
    cin = cg * h
    x1 = jnp.where(pos >= 1, pltpu.roll(cin, 1, 0), p1_ref[...])
    x2 = jnp.where(pos >= 2, pltpu.roll(cin, 2, 0), p2_ref[...])
    conv = x2 * cw_ref[0:1, :] + x1 * cw_ref[1:2, :] + cin * cw_ref[2:3, :]
    acts_ref[:, 0:CONV_CH] = (bg * conv).astype(BF16)
    cin_ref[...] = cin

    rr = lax.broadcasted_iota(jnp.int32, (n, n), 0)
    cc = lax.broadcasted_iota(jnp.int32, (n, n), 1)
    same = (rr >> 3) == (cc >> 3)
    tril = same & (cc <= rr)
    la_hi, la_lo, b, bl = _gla_decay_terms(alr_ref, a2_ref, ab_ref, tril, same)
    q = qk_ref[:, :GLA_QK].astype(F32) * (GLA_DK ** -0.5)
    k = qk_ref[:, GLA_QK:].astype(F32)
    q_t = (q * jnp.exp(b)).astype(BF16)
    k_t = (k * jnp.exp(-b)).astype(BF16)
    k_end = k * jnp.exp(bl - b)
    la_hi = la_hi.astype(F32)
    la_lo = la_lo.astype(F32)
    row_seq = lax.broadcasted_iota(jnp.int32, (n, GLA_DK), 0) >> 3
    seq3 = lax.broadcasted_iota(jnp.int32, (SEQ_PER_BLK, GLA_DK, n), 0)
    lane_seq3 = lax.broadcasted_iota(jnp.int32, (SEQ_PER_BLK, GLA_DK, n), 2) >> 3
    mask3 = seq3 == lane_seq3
    ones = jnp.ones((n, GLA_DV), BF16)
    big = SEQ_PER_BLK * GLA_DK

    def per_seq(x_tr):
        x3 = jnp.where(mask3, x_tr[None, :, :], 0.0)
        return x3.reshape(big, n).astype(BF16)

    for hd in range(GLA_HEADS):
        ks = slice(hd * GLA_DK, (hd + 1) * GLA_DK)
        vs = slice(hd * GLA_DV, (hd + 1) * GLA_DV)
        qh = q_t[:, ks]
        vh = v_ref[:, vs]
        att = jnp.where(tril, _dot_nt(qh, k_t[:, ks]), 0.0).astype(BF16)
        o_intra = _dot(att, vh)
        s_old = s0_ref[:, hd].reshape(big, GLA_DV)
        zero = jnp.zeros_like(qh)
        q_big = jnp.concatenate(
            [jnp.where(row_seq == j, qh, zero) for j in range(SEQ_PER_BLK)], axis=1)
        o = o_intra + _dot(q_big, s_old.astype(BF16))
        dlog = _dot(per_seq(la_hi[:, ks].T), ones) + _dot(per_seq(la_lo[:, ks].T), ones)
        upd = _dot(per_seq(k_end[:, ks].T), vh)
        s_new = jnp.exp(dlog) * s_old + upd
        ns_ref[:, hd] = s_new.reshape(SEQ_PER_BLK, GLA_DK, GLA_DV)
        r = r_ref[:, vs]
        acts_ref[:, CONV_CH + hd * GLA_DV:CONV_CH + (hd + 1) * GLA_DV] = (
            _gla_out_gate(o, gng_ref, r).astype(BF16))

    us, vg = _chunk_mlp(bgu_ref, vv_ref, cmg_ref, ws_ref, cmb_ref, 1)
    acts_ref[:, CONV_CH + GLA_V:] = us.astype(BF16)
    vrow_ref[...] = vg


N_MIX_S_IN = 17


def _mix_sample_kernel(*refs, layer):
    if layer == 0:
        @pl.when(pl.program_id(0) == 0)
        def _():
            _mix_sample_body(*refs)

        @pl.when(pl.program_id(0) > 0)
        def _():
            ns_ref = refs[N_MIX_S_IN + 2]
            ns_ref[...] = jnp.zeros_like(ns_ref)
    else:
        _mix_sample_body(*refs[:N_MIX_S_IN], *refs[N_MIX_S_IN + 1:])


def _mix_sample(z, alr, a2, ab, cw, gng, cmg, ws, cmb, p1, p2, s0, ns_all, layer):
    row0 = T_PROMPT // ROWS_S
    n_i = DEC_BATCH // SEQ_PER_BLK
    n_pass = DEPTH if layer == 0 else 1
    blk = lambda p, i: jnp.where(p == 0, i, n_i - 1)
    row_map = lambda p, i: row0 + blk(p, i)
    slot = lambda p, i: (layer + p, i, 0, 0, 0)
    small = (a2, ab, cw, gng, cmg, ws, cmb)
    state_blk = (None, SEQ_PER_BLK, GLA_HEADS, GLA_DK, GLA_DV)
    in_specs = _z_specs(ROWS_S, row_map) + [
        pl.BlockSpec((ROWS_S, LANES), lambda p, i: (row_map(p, i), 0)),
    ] + [_layer_spec(a, layer) for a in small] + [
        pl.BlockSpec((ROWS_S, CONV_CH), lambda p, i: (blk(p, i), 0)),
        pl.BlockSpec((ROWS_S, CONV_CH), lambda p, i: (blk(p, i), 0)),
        pl.BlockSpec(state_blk, lambda p, i: (layer, blk(p, i), 0, 0, 0)),
    ]
    args = (z, z, z, z, z, z, alr, *small, p1, p2, s0)
    assert len(args) == N_MIX_S_IN
    aliases = {}
    if layer > 0:
        in_specs.append(pl.BlockSpec(memory_space=pl.ANY))
        args += (ns_all,)
        aliases = {N_MIX_S_IN: 2}
    return pl.pallas_call(
        functools.partial(_mix_sample_kernel, layer=layer),
        grid=(n_pass, n_i),
        in_specs=in_specs,
        out_specs=[
            pl.BlockSpec((ROWS_S, ACT_COLS), lambda p, i: (blk(p, i), 0)),
            pl.BlockSpec((ROWS_S, CONV_CH), lambda p, i: (blk(p, i), 0)),
            pl.BlockSpec(state_blk, slot),
            pl.BlockSpec((ROWS_S, CM_CH), lambda p, i: (blk(p, i), 0)),
        ],
        out_shape=[
            jax.ShapeDtypeStruct((T_SAMPLE, ACT_COLS), BF16),
            jax.ShapeDtypeStruct((T_SAMPLE, CONV_CH), F32),
            jax.ShapeDtypeStruct((DEPTH, DEC_BATCH, GLA_HEADS, GLA_DK, GLA_DV), F32),
            jax.ShapeDtypeStruct((T_SAMPLE, CM_CH), F32),
        ],
        input_output_aliases=aliases,
        compiler_params=pltpu.CompilerParams(
            dimension_semantics=("arbitrary", "arbitrary"), vmem_limit_bytes=VMEM_LIMIT),
        name="mix_sample",
    )(*args)


def _delayed(gen, n_stages):
    for _ in range(n_stages):
        yield
    yield from gen


def _interleave(*stage_lists):
    pending = list(stage_lists)
    while pending:
        for gen in list(pending):
            if next(gen, StopIteration) is StopIteration:
                pending.remove(gen)


def _outproj_main(i, n_xa, actp_ref, acts_ref, gates_ref, xa_ref, xb_ref, pa_ref, pb_ref, pc_ref,
                  wo_ref, n2_ref, xo_ref, xn_ref, xnb_ref):
    acts = jnp.where(i < T_PROMPT // TM_OUT, actp_ref[...], acts_ref[...])
    x_in = jnp.where(i < n_xa, xa_ref[...], xb_ref[...])
    yb = _dot(acts[:, CONV_CH:CONV_CH + GLA_V], pb_ref[...])
    ga = _sigmoid(gates_ref[:, COL_GA:COL_GA + D_MODEL])
    ya = _dot(acts[:, :CONV_CH], pa_ref[...])
    gb = _sigmoid(gates_ref[:, COL_GB:COL_GB + D_MODEL])
    yield
    yc = _dot(acts[:, CONV_CH + GLA_V:], pc_ref[...])
    gc = _sigmoid(gates_ref[:, COL_GC:COL_GC + D_MODEL])
    mix = ga * ya + gb * yb
    yield
    mix = mix + gc * yc
    x = x_in + _dot(mix.astype(BF16), wo_ref[...])
    xo_ref[...] = x
    yield
    xn = _rms(x, n2_ref[...])
    xn_ref[...] = xn
    xnb = xn.astype(BF16)
    yield
    xnb_ref[...] = xnb


def _outproj_route(live, xnb_ref, wr_ref, rb_ref, route_ref, seg_ref, cnt_ref):
    tm = TM_OUT
    logits = _dot_nt(wr_ref[...], xnb_ref[...]) + rb_ref[...]
    yield
    n_sub = EXPERTS_PER_GROUP
    sub = lax.broadcasted_iota(jnp.int32, (n_sub, tm), 0).astype(F32)
    neg = jnp.float32(-jnp.inf)
    lg = logits[0:MOE_GROUPS]
    gmax = jnp.max(lg, axis=0, keepdims=True)
    grp = jnp.min(jnp.where(lg == gmax, sub, 1e9), axis=0, keepdims=True)
    p_grp = 1.0 / jnp.sum(jnp.exp(lg - gmax), axis=0, keepdims=True)
    yield
    le = logits[MOE_GROUPS:MOE_GROUPS + n_sub]
    for g in range(1, MOE_GROUPS):
        le = jnp.where(grp == g, logits[MOE_GROUPS + g * n_sub:MOE_GROUPS + (g + 1) * n_sub], le)
    v1 = jnp.max(le, axis=0, keepdims=True)
    i1 = jnp.min(jnp.where(le == v1, sub, 1e9), axis=0, keepdims=True)
    le2 = jnp.where(sub == i1, neg, le)
    v2 = jnp.max(le2, axis=0, keepdims=True)
    i2 = jnp.min(jnp.where(le2 == v2, sub, 1e9), axis=0, keepdims=True)
    yield
    t = jnp.exp(v2 - v1)
    g1 = p_grp / (1.0 + t)
    g2 = p_grp * t / (1.0 + t)
    gate_t = jnp.where(sub == i1, g1, jnp.where(sub == i2, g2, 0.0))

    onehot_t = jnp.where(sub == grp, 1.0, 0.0)
    csum = jnp.where(live, jnp.sum(onehot_t, axis=1, keepdims=True), 0.0)
    rr = lax.broadcasted_iota(jnp.int32, (tm, tm), 0)
    cc = lax.broadcasted_iota(jnp.int32, (tm, tm), 1)
    earlier = jnp.where(rr < cc, 1.0, 0.0).astype(BF16)
    padded = jnp.concatenate([onehot_t, jnp.zeros_like(onehot_t)], axis=0).astype(BF16)
    same_before = _dot(padded, earlier)[0:n_sub]
    yield
    lower = jnp.sum(jnp.where(sub < grp, csum, 0.0), axis=0, keepdims=True)
    lrank = lower + jnp.sum(onehot_t * same_before, axis=0, keepdims=True)
    route_ref[0:n_sub, :] = gate_t
    route_ref[n_sub:, :] = jnp.broadcast_to(lrank, (n_sub, tm))
    carry = cnt_ref[...]
    lane = lax.broadcasted_iota(jnp.int32, (n_sub, LANES), 1)
    seg_ref[...] = jnp.where(lane == SEG_START, carry,
                             jnp.where(lane == SEG_LEN, csum, 0.0)).astype(jnp.int32)
    cnt_ref[...] = carry + csum


def _outproj_kernel(actp_ref, acts_ref, gates_ref, xa_ref, xb_ref, pa_ref, pb_ref, pc_ref, wo_ref,
                    n2_ref, wr_ref, rb_ref, *refs, n_xa, cast):
    if cast:
        _cast_job(refs[0:3], refs[8:11])
        refs = refs[3:8] + refs[11:]
    xo_ref, xn_ref, route_ref, seg_ref, cnt_ref, xnb_ref = refs
    i = pl.program_id(0)

    @pl.when(i == 0)
    def _():
        cnt_ref[...] = jnp.zeros_like(cnt_ref)
        xnb_ref[...] = jnp.zeros_like(xnb_ref)

    tile = jnp.minimum(i, pl.num_programs(0) - 2)
    _interleave(
        _delayed(_outproj_route(i >= 1, xnb_ref, wr_ref, rb_ref, route_ref, seg_ref, cnt_ref),
                 ROUTE_DELAY),
        _outproj_main(tile, n_xa, actp_ref, acts_ref, gates_ref, xa_ref, xb_ref, pa_ref, pb_ref,
                      pc_ref, wo_ref, n2_ref, xo_ref, xn_ref, xnb_ref))


def _lane_prefix(v):
    rr = lax.broadcasted_iota(jnp.int32, (LANES, LANES), 0)
    cc = lax.broadcasted_iota(jnp.int32, (LANES, LANES), 1)
    earlier = jnp.where(rr < cc, 1.0, 0.0).astype(BF16)
    return _dot(jnp.broadcast_to(v, (8, LANES)).astype(BF16), earlier)[0:1]


def _outproj(acts_p, acts_s, z, xa, xb, pa, pb, pc, wo, n2, wr, rb, layer, experts=None):
    m = T_ALL
    nt = m // TM_OUT
    row = lambda i: (jnp.minimum(i, nt - 1), 0)
    routed = lambda i: (jnp.maximum(i - 1, 0), 0, 0)
    _, act_specs = _two_part_specs(TM_OUT, acts_p, acts_s)
    n_xa, x_specs = _two_part_specs(TM_OUT, xa, xb)
    weights = (pa, pb, pc, wo, n2, wr, rb)
    cast = experts is not None
    cast_in, cast_out, cast_shape = _cast_job_specs(layer, nt + 1) if cast else ([], [], [])
    return pl.pallas_call(
        functools.partial(_outproj_kernel, n_xa=n_xa, cast=cast),
        grid=(nt + 1,),
        in_specs=act_specs + [pl.BlockSpec((TM_OUT, 3 * D_MODEL), row)] + x_specs
        + [_layer_spec(w, layer) for w in weights] + cast_in,
        out_specs=[
            pl.BlockSpec((TM_OUT, D_MODEL), row),
            pl.BlockSpec((TM_OUT, D_MODEL), row),
            pl.BlockSpec((None, ROUTE_ROWS, TM_OUT), routed),
            pl.BlockSpec((None, 8, LANES), routed),
            pl.BlockSpec((8, LANES), lambda i: (0, 0)),
        ] + cast_out,
        scratch_shapes=[pltpu.VMEM((TM_OUT, D_MODEL), BF16)],
        out_shape=[
            jax.ShapeDtypeStruct((m, D_MODEL), F32),
            jax.ShapeDtypeStruct((m, D_MODEL), F32),
            jax.ShapeDtypeStruct((m // TM_OUT, ROUTE_ROWS, TM_OUT), F32),
            jax.ShapeDtypeStruct((m // TM_OUT, 8, LANES), jnp.int32),
            jax.ShapeDtypeStruct((8, LANES), F32),
        ] + cast_shape,
        compiler_params=pltpu.CompilerParams(
            dimension_semantics=("arbitrary",), vmem_limit_bytes=VMEM_LIMIT),
        name="outproj",
    )(acts_p, acts_s, z, xa, xb, *weights, *(experts or ()))


PLAN_GROUP, PLAN_USED, PLAN_END, PLAN_BLOCKS, PLAN_BASE = 0, 1, 2, 3, 4


def _plan_kernel(cnt_ref, plan_ref):
    sub_i = lax.broadcasted_iota(jnp.int32, (8, LANES), 0)
    lane8_i = lax.broadcasted_iota(jnp.int32, (8, LANES), 1)
    cnt = jnp.sum(jnp.where(sub_i == lane8_i, cnt_ref[...], 0.0), axis=0, keepdims=True)
    blocks = jnp.floor((cnt + (MOE_BLK - 1)) * (1.0 / MOE_BLK))
    start = _lane_prefix(blocks)
    end = start + blocks
    lane_i = lax.broadcasted_iota(jnp.int32, (1, LANES), 1)
    lane = lane_i.astype(F32)
    grp_of_blk = jnp.zeros((1, LANES), F32)
    for g in range(MOE_GROUPS):
        end_g = jnp.sum(jnp.where(lane_i == g, end, 0.0), axis=-1, keepdims=True)
        grp_of_blk += jnp.where(lane >= end_g, 1.0, 0.0)
    grp_of_blk = jnp.minimum(grp_of_blk, MOE_GROUPS - 1)
    n_used = jnp.sum(blocks, axis=-1, keepdims=True)
    row = lax.broadcasted_iota(jnp.int32, (8, LANES), 0)
    plan_ref[...] = jnp.where(
        row == PLAN_GROUP, grp_of_blk,
        jnp.where(row == PLAN_USED, n_used,
                  jnp.where(row == PLAN_END, end,
                            jnp.where(row == PLAN_BLOCKS, blocks,
                                      jnp.where(row == PLAN_BASE, start * MOE_BLK, 0.0))))
    ).astype(jnp.int32)


def _plan(cnt):
    return pl.pallas_call(
        _plan_kernel,
        out_shape=jax.ShapeDtypeStruct((8, LANES), jnp.int32),
        name="plan",
    )(cnt)


SEG_START, SEG_LEN = 0, 1


def _segment_copies(seg_ref, base_ref, make_copy):
    local = 0
    for g in range(MOE_GROUPS):
        n = seg_ref[g, SEG_LEN]
        first = base_ref[g] + seg_ref[g, SEG_START]
        k = TM_ROW
        while k >= 1:
            done = n & ~(2 * k - 1)
            @pl.when((n & k) != 0)
            def _():
                make_copy(local + done, first + done, k).start()
            k //= 2
        local = local + n


def _perm_matrix(route_ref):
    row = lax.broadcasted_iota(jnp.int32, (TM_ROW, TM_ROW), 0).astype(F32)
    return jnp.where(row == route_ref[ROUTE_LRANK:ROUTE_LRANK + 1, :], 1.0, 0.0).astype(BF16)


SUB = 8
SUB_X = D_MODEL // 2 // LANES
SUB_GATE = SUB_X
HI_MASK = -65536


def _rows(first_row, n_rows):
    return pl.ds(pl.multiple_of(first_row * SUB, SUB), n_rows * SUB)


def _sublane(s, n_rows):
    return pl.ds(s, n_rows, stride=SUB)


def _scatter_kernel(nu_ref, end_ref, nb_ref, base_ref, seg_ref, route_ref, xn_ref, xs_ref,
                    zbuf, sbuf, sems):
    i = pl.program_id(0)
    n = pl.num_programs(0)
    sem = sems.at[0]

    def zero_block(b):
        return pltpu.make_async_copy(zbuf, xs_ref.at[_rows(b * MOE_BLK, MOE_BLK)], sem)

    def each_unfilled_block(fn):
        for g in range(MOE_GROUPS):
            @pl.when(nb_ref[g] > 0)
            def _():
                fn(zero_block(end_ref[g] - 1))
        for b in range(T_ALL // MOE_BLK, N_BLK):
            @pl.when(b >= nu_ref[0])
            def _():
                fn(zero_block(b))

    @pl.when(i == 0)
    def _():
        zbuf[...] = jnp.zeros_like(zbuf)
        sbuf[...] = jnp.zeros_like(sbuf)
        each_unfilled_block(lambda c: c.start())
        each_unfilled_block(lambda c: c.wait())

    def tile_done(s):
        pltpu.make_async_copy(sbuf.at[s], xs_ref.at[_rows(0, TM_ROW)], sems.at[s]).wait()

    @pl.when(i >= 2)
    def _():
        for t in range(ROW_TILES):
            tile_done((i % 2) * ROW_TILES + t)

    def sort_tile(t):
        slot = (i % 2) * ROW_TILES + t
        route = route_ref.at[t]
        perm = _perm_matrix(route)
        xn = xn_ref[t * TM_ROW:(t + 1) * TM_ROW, :]
        xs = lax.bitcast_convert_type(_dot(perm, xn.astype(BF16)), jnp.int32)
        yield
        half = D_MODEL // 2
        for s in range(SUB_X):
            hi = xs[:, s * LANES:(s + 1) * LANES] & HI_MASK
            lo = lax.shift_right_logical(xs[:, half + s * LANES:half + (s + 1) * LANES], 16)
            sbuf[slot, _sublane(s, TM_ROW), :] = hi | lo
        yield
        gate_t = route[0:EXPERTS_PER_GROUP, :]
        gate_t = jnp.concatenate(
            [gate_t, jnp.zeros((LANES - EXPERTS_PER_GROUP, TM_ROW), F32)], axis=0)
        g1 = gate_t.astype(BF16)
        r1 = gate_t - g1.astype(F32)
        g2 = r1.astype(BF16)
        g3 = (r1 - g2.astype(F32)).astype(BF16)
        sbuf[slot, _sublane(SUB_GATE, TM_ROW), :] = lax.bitcast_convert_type(
            _dot_nt(perm, g1) + _dot_nt(perm, g2) + _dot_nt(perm, g3), jnp.int32)

    _interleave(*[_delayed(sort_tile(t), t) for t in range(ROW_TILES)])

    for t in range(ROW_TILES):
        def make_copy(src_row, dst_row, k, slot=(i % 2) * ROW_TILES + t):
            return pltpu.make_async_copy(sbuf.at[slot, _rows(src_row, k)],
                                         xs_ref.at[_rows(dst_row, k)], sems.at[slot])
        _segment_copies(seg_ref.at[t], base_ref, make_copy)

    @pl.when(i == n - 1)
    def _():
        for t in range(ROW_TILES):
            tile_done((i % 2) * ROW_TILES + t)

            @pl.when(n > 1)
            def _():
                tile_done((1 - i % 2) * ROW_TILES + t)


def _scatter(seg, route, xn, n_used, grp_end, grp_blocks, grp_base):
    m = xn.shape[0]
    step_rows = ROW_TILES * TM_ROW
    grid_spec = pltpu.PrefetchScalarGridSpec(
        num_scalar_prefetch=4,
        grid=(m // step_rows,),
        in_specs=[
            pl.BlockSpec((ROW_TILES, 8, LANES), lambda i, *_: (i, 0, 0), memory_space=pltpu.SMEM),
            pl.BlockSpec((ROW_TILES, ROUTE_ROWS, TM_ROW), lambda i, *_: (i, 0, 0)),
            pl.BlockSpec((step_rows, D_MODEL), lambda i, *_: (i, 0)),
        ],
        out_specs=pl.BlockSpec(memory_space=pl.ANY),
        scratch_shapes=[pltpu.VMEM((MOE_BLK * SUB, LANES), jnp.int32),
                        pltpu.VMEM((2 * ROW_TILES, TM_ROW * SUB, LANES), jnp.int32),
                        pltpu.SemaphoreType.DMA((2 * ROW_TILES,))],
    )
    return pl.pallas_call(
        _scatter_kernel,
        grid_spec=grid_spec,
        out_shape=jax.ShapeDtypeStruct((N_SORTED * SUB, LANES), jnp.int32),
        compiler_params=pltpu.CompilerParams(
            dimension_semantics=("arbitrary",), vmem_limit_bytes=VMEM_LIMIT),
        name="scatter",
    )(n_used, grp_end, grp_blocks, grp_base, seg, route, xn)


def _ffn_kernel(bg_ref, nu_ref, xs_ref, w1_ref, w3_ref, w2_ref, *refs, cast):
    del bg_ref
    if cast:
        _cast_job(refs[0:3], refs[4:7])
    y_ref = refs[3 if cast else 0]
    b = pl.program_id(0)

    def block_inputs():
        packed = [xs_ref[_sublane(s, MOE_BLK), :] for s in range(SUB_X)]
        x = jnp.concatenate(
            [lax.bitcast_convert_type(u & HI_MASK, F32).astype(BF16) for u in packed]
            + [lax.bitcast_convert_type(lax.shift_left(u, 16), F32).astype(BF16) for u in packed],
            axis=1)
        return x, lax.bitcast_convert_type(xs_ref[_sublane(SUB_GATE, MOE_BLK), :], F32)

    def store(y):
        for s in range(SUB):
            y_ref[_sublane(s, MOE_BLK), :] = y[:, s * LANES:(s + 1) * LANES]

    @pl.when(b < nu_ref[0])
    def _():
        x, gates = block_inputs()
        n_e = EXPERTS_PER_GROUP
        h1 = [_dot(x, w1_ref[e]) for e in range(n_e)]
        h3 = [_dot(x, w3_ref[e]) for e in range(n_e)]
        hs = []
        for e in range(n_e):
            ge = gates[:, e:e + 1]
            hs.append(jnp.where(ge > 0.0, h1[e] * _sigmoid(h1[e]) * h3[e] * ge, 0.0).astype(BF16))
        hcat = jnp.concatenate(hs, axis=1)
        store(_dot(hcat, w2_ref[...].reshape(n_e * D_EXPERT, D_MODEL)))

    @pl.when(b >= nu_ref[0])
    def _():
        y_ref[...] = jnp.zeros_like(y_ref)


def _ffn(blk_group, n_used, xs, w1, w3, w2, cast_layer=None, experts=None):
    grouped = (MOE_GROUPS, EXPERTS_PER_GROUP)
    w1, w3, w2 = (w.reshape(grouped + w.shape[1:]) for w in (w1, w3, w2))
    wmap = lambda b, bg, nu: (bg[b], 0, 0, 0)
    cast = experts is not None
    cast_in, cast_out, cast_shape = _cast_job_specs(cast_layer, N_BLK) if cast else ([], [], [])
    grid_spec = pltpu.PrefetchScalarGridSpec(
        num_scalar_prefetch=2,
        grid=(N_BLK,),
        in_specs=[
            pl.BlockSpec((MOE_BLK * SUB, LANES), lambda b, bg, nu: (b, 0)),
            pl.BlockSpec((None, EXPERTS_PER_GROUP, D_MODEL, D_EXPERT), wmap),
            pl.BlockSpec((None, EXPERTS_PER_GROUP, D_MODEL, D_EXPERT), wmap),
            pl.BlockSpec((None, EXPERTS_PER_GROUP, D_EXPERT, D_MODEL), wmap),
        ] + cast_in,
        out_specs=[pl.BlockSpec((MOE_BLK * SUB, LANES), lambda b, bg, nu: (b, 0))] + cast_out,
    )
    return pl.pallas_call(
        functools.partial(_ffn_kernel, cast=cast),
        grid_spec=grid_spec,
        out_shape=[jax.ShapeDtypeStruct((N_SORTED * SUB, LANES), F32)] + cast_shape,
        compiler_params=pltpu.CompilerParams(
            dimension_semantics=("arbitrary",), vmem_limit_bytes=VMEM_LIMIT),
        name="ffn",
    )(blk_group, n_used, xs, w1, w3, w2, *(experts or ()))


def _combine_kernel(base_ref, seg_ref, segn_ref, route_ref, ys_hbm, x_ref, g_ref, wa_ref, *refs,
                    final):
    outs, (buf, sem, stage) = refs[:-3], refs[-3:]
    i = pl.program_id(0)
    n = pl.num_programs(0)
    step_rows = ROW_TILES * TM_ROW

    def gather(seg, parity):
        for t in range(ROW_TILES):
            s = parity * ROW_TILES + t

            def make_copy(buf_row, ys_row, k, s=s):
                return pltpu.make_async_copy(ys_hbm.at[_rows(ys_row, k)],
                                             buf.at[s, _rows(buf_row, k)], sem.at[s])
            _segment_copies(seg.at[t], base_ref, make_copy)

    @pl.when(i == 0)
    def _():
        gather(seg_ref, 0)

    @pl.when(i + 1 < n)
    def _():
        gather(segn_ref, 1 - i % 2)

    for t in range(ROW_TILES):
        slot = (i % 2) * ROW_TILES + t
        pltpu.make_async_copy(ys_hbm.at[_rows(0, TM_ROW)], buf.at[slot], sem.at[slot]).wait()

    def tile_stages(t):
        slot = (i % 2) * ROW_TILES + t
        rows = slice(t * TM_ROW, (t + 1) * TM_ROW)
        tn = (((0,), (0,)), ((), ()))
        perm = _perm_matrix(route_ref.at[t])
        cols = []
        for s in range(SUB):
            hi, lo = _split_bf16(buf[slot, _sublane(s, TM_ROW), :])
            cols.append(lax.dot_general(perm, hi, tn, preferred_element_type=F32)
                        + lax.dot_general(perm, lo, tn, preferred_element_type=F32))
            if s % 4 == 3:
                yield
        y = x_ref[rows, :] + jnp.concatenate(cols, axis=1)
        if not final:
            outs[0][rows, :] = y
            yield
            _norm_and_lowrank(y, g_ref, wa_ref, outs[1].at[rows, :], outs[2].at[rows, :])
        else:
            yield
            stage[rows, :] = _rms(y, g_ref[...])

    _interleave(*[_delayed(tile_stages(t), t) for t in range(ROW_TILES)])

    if final:
        @pl.when(i < T_PROMPT // step_rows)
        def _():
            outs[0][...] = stage[...]

        @pl.when(i >= T_PROMPT // step_rows)
        def _():
            outs[1][...] = stage[...]


def _combine(seg, route, grp_base, ys, x, g, wa, final):
    m = x.shape[0]
    step_rows = ROW_TILES * TM_ROW
    nt = m // step_rows
    n_p = T_PROMPT // step_rows
    smem = functools.partial(pl.BlockSpec, (ROW_TILES, 8, LANES), memory_space=pltpu.SMEM)
    tile = (step_rows, D_MODEL)
    if final:
        out_specs = [pl.BlockSpec(tile, lambda i, *_: (jnp.minimum(i, n_p - 1), 0)),
                     pl.BlockSpec(tile, lambda i, *_: (jnp.maximum(i - n_p, 0), 0))]
        out_shape = [jax.ShapeDtypeStruct((T_PROMPT, D_MODEL), F32),
                     jax.ShapeDtypeStruct((T_SAMPLE, D_MODEL), F32)]
    else:
        out_specs = [pl.BlockSpec(tile, lambda i, *_: (i, 0)),
                     pl.BlockSpec(tile, lambda i, *_: (i, 0)),
                     pl.BlockSpec((step_rows, LANES), lambda i, *_: (i, 0))]
        out_shape = [jax.ShapeDtypeStruct((m, D_MODEL), F32),
                     jax.ShapeDtypeStruct((m, D_MODEL), BF16),
                     jax.ShapeDtypeStruct((m, LANES), BF16)]
    grid_spec = pltpu.PrefetchScalarGridSpec(
        num_scalar_prefetch=1,
        grid=(nt,),
        in_specs=[
            smem(lambda i, *_: (i, 0, 0)),
            smem(lambda i, *_: (jnp.minimum(i + 1, nt - 1), 0, 0)),
            pl.BlockSpec((ROW_TILES, ROUTE_ROWS, TM_ROW), lambda i, *_: (i, 0, 0)),
            pl.BlockSpec(memory_space=pl.ANY),
            pl.BlockSpec(tile, lambda i, *_: (i, 0)),
            pl.BlockSpec(g.shape, lambda i, *_: (0, 0)),
            pl.BlockSpec(wa.shape, lambda i, *_: (0, 0)),
        ],
        out_specs=out_specs,
        scratch_shapes=[pltpu.VMEM((2 * ROW_TILES, TM_ROW * SUB, LANES), F32),
                        pltpu.SemaphoreType.DMA((2 * ROW_TILES,)),
                        pltpu.VMEM(tile if final else (8, LANES), F32)],
    )
    return pl.pallas_call(
        functools.partial(_combine_kernel, final=final),
        grid_spec=grid_spec,
        out_shape=out_shape,
        compiler_params=pltpu.CompilerParams(
            dimension_semantics=("arbitrary",), vmem_limit_bytes=VMEM_LIMIT),
        name="combine",
    )(grp_base, seg, seg, route, ys, x, g, wa)


def _prep_weights(w_in, gla_a2, cm_ws, cm_b, router_group_w, router_group_b,
                  router_expert_w, router_expert_b):
    off = {}
    o = 0
    for name, n in (("h", 512), ("cg", 512), ("bg", 512), ("q", 512), ("k", 512), ("v", 1024),
                    ("r", 1024), ("alr", 16), ("u", 512), ("vv", 512), ("ga", 1024),
                    ("gb", 1024), ("gc", 1024)):
        off[name] = (o, n)
        o += n
    runs = (("ga", "gc"), ("v", "r"), ("q", "k"), ("h", "cg"), ("bg", "bg"), ("u", "vv"))
    w_t = jnp.swapaxes(w_in, 1, 2)
    w_z = jnp.concatenate([w_t[:, off[a][0]:off[b][0] + off[b][1]] for a, b in runs],
                          axis=1).astype(BF16)
    assert w_z.shape[1] == Z_COLS
    a0 = off["alr"][0]
    w_alr = jnp.pad(w_t[:, a0:a0 + GLA_LOWRANK],
                    ((0, 0), (0, LANES - GLA_LOWRANK), (0, 0))).astype(BF16)
    a2 = jnp.pad(gla_a2, ((0, 0), (0, LANES - GLA_LOWRANK), (0, 0))).astype(BF16)
    ws_p = jnp.tril(cm_ws).astype(BF16)
    small = jnp.tril(cm_ws[:, :, :DEC_SEQ, :DEC_SEQ])
    eye = jnp.eye(SEQ_PER_BLK, dtype=F32)
    ws_s = jnp.einsum("ij,lgab->lgiajb", eye, small).reshape(
        DEPTH, CM_GROUPS, ROWS_S, ROWS_S).astype(BF16)
    cmb_p = jnp.broadcast_to(jnp.transpose(cm_b, (0, 2, 1))[:, :, :, None],
                             (DEPTH, CM_CHUNK, CM_GROUPS, CM_GCH)).reshape(DEPTH, CM_CHUNK, CM_CH)
    cmb_s = jnp.tile(cmb_p[:, :DEC_SEQ], (1, SEQ_PER_BLK, 1))
    pad = LANES - MOE_GROUPS - N_EXPERTS
    w_r = jnp.pad(jnp.swapaxes(jnp.concatenate([router_group_w, router_expert_w], axis=-1), 1, 2),
                  ((0, 0), (0, pad), (0, 0)))
    r_b = jnp.pad(jnp.concatenate([router_group_b, router_expert_b], axis=-1),
                  ((0, 0), (0, pad)))[:, :, None]
    return w_z, w_alr, a2, ws_p, ws_s, cmb_p, cmb_s, w_r.astype(BF16), r_b


def kernel(x_prompt, x_sample, state_conv, state_gla, norm1_g, w_in, conv_w, gla_a2, gla_a_b,
           gla_norm_g, cm_norm_g, cm_ws, cm_b, proj_a, proj_b, proj_c, w_out, norm2_g,
           router_group_w, router_group_b, router_expert_w, router_expert_b,
           exp_w1, exp_w3, exp_w2, final_norm_g):
    (w_z, w_alr, a2, ws_p, ws_s, cmb_p, cmb_s, w_r, r_b) = _prep_weights(
        w_in, gla_a2, cm_ws, cm_b, router_group_w, router_group_b, router_expert_w,
        router_expert_b)
    pa, pb, pc, wo = (w.astype(BF16) for w in (proj_a, proj_b, proj_c, w_out))
    n1 = norm1_g[:, None, :]
    n2 = norm2_g[:, None, :]
    ab = gla_a_b[:, None, :]
    gng = gla_norm_g[:, None, :]
    cmg = cm_norm_g.reshape(DEPTH, 1, CM_CH)
    fg = final_norm_g[None, :]
    xa = x_prompt.reshape(T_PROMPT, D_MODEL)
    xb = x_sample.reshape(T_SAMPLE, D_MODEL)
    gla_s = None
    conv_p, gla_p, conv_s, cmv_s = [], [], [], []
    xin, alr = _prenorm(xa, xb, n1, w_alr, 0)
    experts_f32 = (exp_w1, exp_w3, exp_w2)
    experts = None
    for l in range(DEPTH):
        z = _inproj(xin, w_z, l)
        acts_p, nconv, ngla = _mix_prompt(z, alr, a2, ab, conv_w, gng, cmg, ws_p, cmb_p, l)
        sc = state_conv[l]
        p2 = jnp.pad(sc, ((0, 0), (0, DEC_SEQ - 2), (0, 0))).reshape(T_SAMPLE, CONV_CH)
        p1 = jnp.pad(sc[:, 1:2], ((0, 0), (0, DEC_SEQ - 1), (0, 0))).reshape(T_SAMPLE, CONV_CH)
        acts_s, cin_s, gla_s, vrows = _mix_sample(z, alr, a2, ab, conv_w, gng, cmg, ws_s, cmb_s,
                                                  p1, p2, state_gla, gla_s, l)
        conv_p.append(nconv)
        gla_p.append(ngla)
        conv_s.append(cin_s.reshape(DEC_BATCH, DEC_SEQ, CONV_CH)[:, DEC_SEQ - (CONV_K - 1):])
        cmv_s.append(vrows.reshape(DEC_BATCH, DEC_SEQ, CM_CH))

        outs = _outproj(acts_p, acts_s, z, xa, xb, pa, pb, pc, wo, n2, w_r, r_b, l,
                        experts_f32 if experts is None else None)
        x, xn, route, seg, cnt = outs[:5]
        experts = experts or tuple(outs[5:])
        plan = _plan(cnt)
        n_used = plan[PLAN_USED, :1]
        grp_base = plan[PLAN_BASE, :MOE_GROUPS]
        xs = _scatter(seg, route, xn, n_used, plan[PLAN_END, :MOE_GROUPS],
                      plan[PLAN_BLOCKS, :MOE_GROUPS], grp_base)
        ys, *next_experts = _ffn(plan[PLAN_GROUP, :N_BLK], n_used, xs, *experts,
                                 *((l + 1, experts_f32) if l + 1 < DEPTH else ()))
        experts = tuple(next_experts)
        if l == DEPTH - 1:
            out = _combine(seg, route, grp_base, ys, x, fg, w_alr[l], True)
        else:
            xa, xin, alr = _combine(seg, route, grp_base, ys, x, n1[l + 1], w_alr[l + 1], False)
            xb = xa

    y_prompt = out[0].reshape(BATCH, SEQ, D_MODEL)
    y_sample = out[1].reshape(DEC_BATCH, DEC_SEQ, D_MODEL)
    return (y_prompt, y_sample, jnp.stack(conv_p), jnp.stack(gla_p), jnp.stack(conv_s),
            gla_s, jnp.stack(cmv_s))
```

```python
import functools

import jax
import jax.numpy as jnp
from jax import lax
from jax.experimental import pallas as pl
from jax.experimental.pallas import tpu as pltpu

F32 = jnp.float32
BF16 = jnp.bfloat16

D_MODEL = 1024
BATCH = 8
SEQ = 2048
DEPTH = 2
DEC_BATCH = 128
DEC_SEQ = 8
CONV_K = 3
CONV_CH = 512
GLA_HEADS = 4
GLA_DK = 128
GLA_DV = 256
GLA_QK = GLA_HEADS * GLA_DK
GLA_V = GLA_HEADS * GLA_DV
GLA_LOWRANK = 16
GLA_TAU = 16.0
GLA_CHUNK = 64
CM_GROUPS = 4
CM_CHUNK = 128
CM_GCH = 128
CM_CH = 512
MOE_GROUPS = 8
EXPERTS_PER_GROUP = 8
N_EXPERTS = 64
D_EXPERT = 256
EPS = 1e-6

LANES = 128
T_PROMPT = BATCH * SEQ
T_SAMPLE = DEC_BATCH * DEC_SEQ
T_ALL = T_PROMPT + T_SAMPLE

COL_GA, COL_GB, COL_GC = 0, 1024, 2048
COL_V, COL_R, COL_Q, COL_K = 3072, 4096, 5120, 5632
COL_H, COL_CG, COL_BG, COL_U, COL_VV = 6144, 6656, 7168, 7680, 8192
Z_COLS = 8704
ACT_COLS = 2048
ROUTE_ROWS = 2 * EXPERTS_PER_GROUP
ROUTE_LRANK = EXPERTS_PER_GROUP

TM_PRE = 1024
TM_IN = 512
TN_IN = Z_COLS // 2
TC_MIX = 256
SEQ_PER_BLK = 16
ROWS_S = SEQ_PER_BLK * DEC_SEQ
TM_OUT = 256
ROUTE_DELAY = 1
TM_ROW = TM_OUT
ROW_TILES = 4
MOE_BLK = 256
N_BLK = T_ALL // MOE_BLK + MOE_GROUPS
N_SORTED = N_BLK * MOE_BLK
VMEM_LIMIT = 56 * 1024 * 1024


def _sigmoid(x):
    return 0.5 * jnp.tanh(0.5 * x) + 0.5


def _gelu_tanh(x):
    c = 0.7978845608028654
    half = 0.5 * x
    return half + half * jnp.tanh(x * (c + (c * 0.044715) * (x * x)))


def _log_sigmoid(x):
    log2_e = 1.4426950408889634
    ln_2 = 0.6931471805599453
    return jnp.minimum(x, 0.0) - ln_2 * jnp.log2(1.0 + jnp.exp2(-log2_e * jnp.abs(x)))


def _rms(x, g):
    ms = jnp.mean(x * x, axis=-1, keepdims=True)
    return x * lax.rsqrt(ms + EPS) * g


def _split_bf16(x):
    hi = x.astype(BF16)
    lo = (x - hi.astype(F32)).astype(BF16)
    return hi, lo


def _dot(a, b):
    return jnp.dot(a, b, preferred_element_type=F32)


def _dot_nt(a, b):
    return lax.dot_general(a, b, (((1,), (1,)), ((), ())), preferred_element_type=F32)


def _layer_spec(arr, layer):
    nd = arr.ndim - 1
    return pl.BlockSpec((None,) + arr.shape[1:], lambda *g: (layer,) + (0,) * nd)


def _const_spec(arr):
    nd = arr.ndim
    return pl.BlockSpec(arr.shape, lambda *g: (0,) * nd)


def _two_part_specs(tile, xa, xb):
    n_a = xa.shape[0] // tile
    n_b = xb.shape[0] // tile
    return n_a, [
        pl.BlockSpec((tile, xa.shape[1]), lambda i, *_: (jnp.minimum(i, n_a - 1), 0)),
        pl.BlockSpec((tile, xb.shape[1]), lambda i, *_: (jnp.clip(i - n_a, 0, n_b - 1), 0)),
    ]


def _norm_and_lowrank(x, g_ref, wa_ref, xn_ref, a_ref):
    xn = _rms(x, g_ref[...]).astype(BF16)
    xn_ref[...] = xn
    a_ref[...] = _dot_nt(xn, wa_ref[...]).astype(BF16)


def _prenorm_kernel(xa_ref, xb_ref, g_ref, wa_ref, xn_ref, a_ref, *, n_a):
    x = jnp.where(pl.program_id(0) < n_a, xa_ref[...], xb_ref[...])
    _norm_and_lowrank(x, g_ref, wa_ref, xn_ref, a_ref)


def _prenorm(xa, xb, g, wa, layer):
    m = T_ALL
    n_a, x_specs = _two_part_specs(TM_PRE, xa, xb)
    return pl.pallas_call(
        functools.partial(_prenorm_kernel, n_a=n_a),
        grid=(m // TM_PRE,),
        in_specs=x_specs + [_layer_spec(g, layer), _layer_spec(wa, layer)],
        out_specs=[pl.BlockSpec((TM_PRE, D_MODEL), lambda i: (i, 0)),
                   pl.BlockSpec((TM_PRE, LANES), lambda i: (i, 0))],
        out_shape=[jax.ShapeDtypeStruct((m, D_MODEL), BF16),
                   jax.ShapeDtypeStruct((m, LANES), BF16)],
        compiler_params=pltpu.CompilerParams(
            dimension_semantics=("arbitrary",), vmem_limit_bytes=VMEM_LIMIT),
        name="prenorm",
    )(xa, xb, g, wa)


EXPERT_SHAPES = ((D_MODEL, D_EXPERT), (D_MODEL, D_EXPERT), (D_EXPERT, D_MODEL))


def _inproj_kernel(xn_ref, w_ref, *refs):
    n = len(EXPERT_SHAPES)
    z_ref = refs[n]
    for src, dst in zip(refs[:n], refs[n + 1:]):
        dst[...] = src[...].astype(BF16)
    z_ref[...] = _dot_nt(xn_ref[...], w_ref[...]).astype(BF16)


def _inproj(xn, w, experts, layer):
    m = xn.shape[0]
    n_j, n_i = Z_COLS // TN_IN, m // TM_IN
    assert n_j * n_i >= N_EXPERTS
    expert = lambda j, i: jnp.minimum(j * n_i + i, N_EXPERTS - 1)
    return pl.pallas_call(
        _inproj_kernel,
        grid=(n_j, n_i),
        in_specs=[
            pl.BlockSpec((TM_IN, D_MODEL), lambda j, i: (i, 0)),
            pl.BlockSpec((None, TN_IN, D_MODEL), lambda j, i: (layer, j, 0)),
        ] + [pl.BlockSpec((None, None) + s, lambda j, i: (layer, expert(j, i), 0, 0))
             for s in EXPERT_SHAPES],
        out_specs=[pl.BlockSpec((TM_IN, TN_IN), lambda j, i: (i, j))]
        + [pl.BlockSpec((None,) + s, lambda j, i: (expert(j, i), 0, 0)) for s in EXPERT_SHAPES],
        out_shape=[jax.ShapeDtypeStruct((m, Z_COLS), BF16)]
        + [jax.ShapeDtypeStruct((N_EXPERTS,) + s, BF16) for s in EXPERT_SHAPES],
        compiler_params=pltpu.CompilerParams(
            dimension_semantics=("arbitrary", "arbitrary"), vmem_limit_bytes=VMEM_LIMIT),
        name="inproj",
    )(xn, w, *experts)


def _gla_log_decay(alr_ref, a2_ref, ab_ref):
    la = _log_sigmoid(_dot(alr_ref[...], a2_ref[...]) + ab_ref[...]) * (1.0 / GLA_TAU)
    return _split_bf16(la)


def _masked_sum(mask, la_hi, la_lo):
    m = jnp.where(mask, 1.0, 0.0).astype(BF16)
    return _dot(m, la_hi) + _dot(m, la_lo)


def _gla_decay_terms(alr_ref, a2_ref, ab_ref, tril_mask, same_mask):
    la_hi, la_lo = _gla_log_decay(alr_ref, a2_ref, ab_ref)
    return (la_hi, la_lo, _masked_sum(tril_mask, la_hi, la_lo),
            _masked_sum(same_mask, la_hi, la_lo))


def _gla_out_gate(o, g_ref, r):
    return _rms(o, g_ref[...]) * (r * _sigmoid(r))


def _chunk_mlp_group(g, bgu_ref, vv_ref, cmg_ref, ws_ref, cmb_ref, n_chunks):
    sl = slice(g * CM_GCH, (g + 1) * CM_GCH)
    ug = _gelu_tanh(bgu_ref[:, CONV_CH + g * CM_GCH:CONV_CH + (g + 1) * CM_GCH])
    vg = _rms(_gelu_tanh(vv_ref[:, sl]).astype(F32), cmg_ref[:, sl])
    vgb = vg.astype(BF16)
    rows = []
    for j in range(n_chunks):
        rs = slice(j * CM_CHUNK, (j + 1) * CM_CHUNK)
        rows.append(_dot(ws_ref[g], vgb[rs]) + cmb_ref[:, sl])
    s = rows[0] if n_chunks == 1 else jnp.concatenate(rows, axis=0)
    return ug * s, vg


def _chunk_mlp(bgu_ref, vv_ref, cmg_ref, ws_ref, cmb_ref, n_chunks):
    parts = [_chunk_mlp_group(g, bgu_ref, vv_ref, cmg_ref, ws_ref, cmb_ref, n_chunks)
             for g in range(CM_GROUPS)]
    return (jnp.concatenate([p[0] for p in parts], axis=1),
            jnp.concatenate([p[1] for p in parts], axis=1))


N_Z_VIEWS = 7
N_MIX_W = 7
SEQ_PER_STEP = 4
SEQ_STAGGER = 3


def _mix_prompt_kernel(*refs):
    n_z = N_Z_VIEWS * SEQ_PER_STEP
    weights = refs[n_z:n_z + N_MIX_W]
    acts_ref, nconv_ref, ngla_ref, st_ref, carry_ref = refs[n_z + N_MIX_W:]

    @pl.when(pl.program_id(1) == 0)
    def _():
        st_ref[...] = jnp.zeros_like(st_ref)
        carry_ref[...] = jnp.zeros_like(carry_ref)

    rr = lax.broadcasted_iota(jnp.int32, (TC_MIX, TC_MIX), 0)
    cc = lax.broadcasted_iota(jnp.int32, (TC_MIX, TC_MIX), 1)
    tril = ((rr >> 6) == (cc >> 6)) & (cc <= rr)
    masks = (tril, jnp.where(tril, 1.0, 0.0).astype(BF16),
             jnp.where(rr - cc == 1, 1.0, 0.0).astype(BF16),
             jnp.where(rr - cc == 2, 1.0, 0.0).astype(BF16))

    _interleave(*[
        _delayed(_mix_prompt_seq(*refs[N_Z_VIEWS * s:N_Z_VIEWS * (s + 1)], *weights,
                                acts_ref.at[s], nconv_ref.at[s], st_ref.at[s], carry_ref.at[s],
                                masks),
                s * SEQ_STAGGER)
        for s in range(SEQ_PER_STEP)])

    @pl.when(pl.program_id(1) == pl.num_programs(1) - 1)
    def _():
        for s in range(SEQ_PER_STEP):
            for hd in range(GLA_HEADS):
                ngla_ref[s, 0, hd] = st_ref[s, hd].T


def _mix_prompt_seq(v_ref, r_ref, qk_ref, hcg_ref, bgu_ref, vv_ref, alr_ref,
                    a2_ref, ab_ref, cw_ref, gng_ref, cmg_ref, ws_ref, cmb_ref,
                    acts_ref, nconv_ref, st_ref, carry_ref, masks):
    tc = TC_MIX

    h = hcg_ref[:, :CONV_CH].astype(F32)
    cg = hcg_ref[:, CONV_CH:].astype(F32)
    bg = bgu_ref[:, :CONV_CH].astype(F32)
    cin = cg * h
    tril, tril_01, shift_1, shift_2 = masks
    cin_b = cin.astype(BF16)
    x1 = _dot(shift_1, cin_b)
    x2 = _dot(shift_2, cin_b)
    conv = x2 * cw_ref[0:1, :] + x1 * cw_ref[1:2, :] + cin * cw_ref[2:3, :]
    c0 = carry_ref[0:1, :]
    c1 = carry_ref[1:2, :]
    row8 = lax.broadcasted_iota(jnp.int32, (8, 1), 0)
    head = jnp.where(row8 == 0, c0 * cw_ref[0:1, :] + c1 * cw_ref[1:2, :],
                     jnp.where(row8 == 1, c1 * cw_ref[0:1, :], 0.0))
    conv = jnp.concatenate([conv[0:8] + head, conv[8:]], axis=0)
    acts_ref[:, 0:CONV_CH] = (bg * conv).astype(BF16)
    carry_ref[0:2, :] = cin[tc - 2:tc, :]
    nconv_ref[0] = cin[tc - 2:tc, :]
    yield

    la_hi, la_lo = _gla_log_decay(alr_ref, a2_ref, ab_ref)
    b = _dot(tril_01, la_hi) + _dot(tril_01, la_lo)
    yield
    n_chunks = tc // GLA_CHUNK
    b_last = [b[(c + 1) * GLA_CHUNK - 1:(c + 1) * GLA_CHUNK, :] for c in range(n_chunks)]
    bl = jnp.concatenate([jnp.broadcast_to(r_, (GLA_CHUNK, GLA_QK)) for r_ in b_last], axis=0)
    q = qk_ref[:, :GLA_QK].astype(F32) * (GLA_DK ** -0.5)
    k = qk_ref[:, GLA_QK:].astype(F32)
    q_t = (q * jnp.exp(b)).astype(BF16)
    k_t = (k * jnp.exp(-b)).astype(BF16)
    k_end = (k * jnp.exp(bl - b)).astype(BF16)
    yield
    states = [st_ref[hd] for hd in range(GLA_HEADS)]
    k_cols = [slice(hd * GLA_DK, (hd + 1) * GLA_DK) for hd in range(GLA_HEADS)]
    v_cols = [slice(hd * GLA_DV, (hd + 1) * GLA_DV) for hd in range(GLA_HEADS)]
    o_intra = []
    for hd in range(GLA_HEADS):
        att = jnp.where(tril, _dot_nt(q_t[:, k_cols[hd]], k_t[:, k_cols[hd]]), 0.0).astype(BF16)
        o_intra.append(_dot(att, v_ref[:, v_cols[hd]]))
    yield

    o_rows = [[] for _ in range(GLA_HEADS)]
    assert n_chunks == CM_GROUPS
    for c in range(n_chunks):
        rs = slice(c * GLA_CHUNK, (c + 1) * GLA_CHUNK)
        for hd in range(GLA_HEADS):
            ks = k_cols[hd]
            st = states[hd]
            o_rows[hd].append(o_intra[hd][rs] + _dot_nt(q_t[rs, ks], st.astype(BF16)))
            upd = lax.dot_general(v_ref[rs, v_cols[hd]], k_end[rs, ks], (((0,), (0,)), ((), ())),
                                  preferred_element_type=F32)
            states[hd] = jnp.exp(b_last[c][:, ks]) * st + upd
        us, _ = _chunk_mlp_group(c, bgu_ref, vv_ref, cmg_ref, ws_ref, cmb_ref, tc // CM_CHUNK)
        acts_ref[:, CONV_CH + GLA_V + c * CM_GCH:CONV_CH + GLA_V + (c + 1) * CM_GCH] = (
            us.astype(BF16))
        yield

    for hd in range(GLA_HEADS):
        o = jnp.concatenate(o_rows[hd], axis=0)
        r = r_ref[:, v_cols[hd]]
        acts_ref[:, CONV_CH + hd * GLA_DV:CONV_CH + (hd + 1) * GLA_DV] = (
            _gla_out_gate(o, gng_ref, r).astype(BF16))
        st_ref[hd] = states[hd]
        if hd % 2 == 1:
            yield


def _z_specs(rows, row_map):
    def spec(width, col):
        blk = col // width
        return pl.BlockSpec((rows, width), lambda *g: (row_map(*g), blk))
    return [spec(1024, COL_V), spec(1024, COL_R), spec(1024, COL_Q), spec(1024, COL_H),
            spec(1024, COL_BG), spec(512, COL_VV)]


def _mix_prompt(z, alr, a2, ab, cw, gng, cmg, ws, cmb, layer):
    nt = SEQ // TC_MIX
    nb = BATCH // SEQ_PER_STEP
    small = (a2, ab, cw, gng, cmg, ws, cmb)
    assert len(small) == N_MIX_W
    in_specs, args = [], []
    for s in range(SEQ_PER_STEP):
        row_map = lambda b, c, s=s: (b + s * nb) * nt + c
        in_specs += _z_specs(TC_MIX, row_map) + [
            pl.BlockSpec((TC_MIX, LANES), lambda b, c, row_map=row_map: (row_map(b, c), 0))]
        args += [z] * (N_Z_VIEWS - 1) + [alr]
    in_specs += [_layer_spec(a, layer) for a in small]
    acts, nconv, ngla = pl.pallas_call(
        _mix_prompt_kernel,
        grid=(nb, nt),
        in_specs=in_specs,
        out_specs=[
            pl.BlockSpec((SEQ_PER_STEP, TC_MIX, ACT_COLS), lambda b, c: (0, b * nt + c, 0)),
            pl.BlockSpec((SEQ_PER_STEP, 1, CONV_K - 1, CONV_CH), lambda b, c: (0, b, 0, 0)),
            pl.BlockSpec((SEQ_PER_STEP, 1, GLA_HEADS, GLA_DK, GLA_DV),
                         lambda b, c: (0, b, 0, 0, 0)),
        ],
        out_shape=[
            jax.ShapeDtypeStruct((SEQ_PER_STEP, T_PROMPT // SEQ_PER_STEP, ACT_COLS), BF16),
            jax.ShapeDtypeStruct((SEQ_PER_STEP, nb, CONV_K - 1, CONV_CH), F32),
            jax.ShapeDtypeStruct((SEQ_PER_STEP, nb, GLA_HEADS, GLA_DK, GLA_DV), F32),
        ],
        scratch_shapes=[pltpu.VMEM((SEQ_PER_STEP, GLA_HEADS, GLA_DV, GLA_DK), F32),
                        pltpu.VMEM((SEQ_PER_STEP, 8, CONV_CH), F32)],
        compiler_params=pltpu.CompilerParams(
            dimension_semantics=("arbitrary", "arbitrary"), vmem_limit_bytes=VMEM_LIMIT),
        name="mix_prompt",
    )(*args, *small)
    return (acts.reshape(T_PROMPT, ACT_COLS), nconv.reshape(BATCH, CONV_K - 1, CONV_CH),
            ngla.reshape(BATCH, GLA_HEADS, GLA_DK, GLA_DV))


def _mix_sample_body(v_ref, r_ref, qk_ref, hcg_ref, bgu_ref, vv_ref, alr_ref,
                       a2_ref, ab_ref, cw_ref, gng_ref, cmg_ref, ws_ref, cmb_ref,
                       p1_ref, p2_ref, s0_ref,
                       acts_ref, cin_ref, ns_ref, vrow_ref):
    n = ROWS_S

    pos = lax.broadcasted_iota(jnp.int32, (n, 1), 0) & (DEC_SEQ - 1)
    h = hcg_ref[:, :CONV_CH].astype(F32)
    cg = hcg_ref[:, CONV_CH:].astype(F32)
    bg = bgu_ref[:, :CONV_CH].astype(F32)
    cin = cg * h
    x1 = jnp.where(pos >= 1, pltpu.roll(cin, 1, 0), p1_ref[...])
    x2 = jnp.where(pos >= 2, pltpu.roll(cin, 2, 0), p2_ref[...])
    conv = x2 * cw_ref[0:1, :] + x1 * cw_ref[1:2, :] + cin * cw_ref[2:3, :]
    acts_ref[:, 0:CONV_CH] = (bg * conv).astype(BF16)
    cin_ref[...] = cin

    rr = lax.broadcasted_iota(jnp.int32, (n, n), 0)
    cc = lax.broadcasted_iota(jnp.int32, (n, n), 1)
    same = (rr >> 3) == (cc >> 3)
    tril = same & (cc <= rr)
    la_hi, la_lo, b, bl = _gla_decay_terms(alr_ref, a2_ref, ab_ref, tril, same)
    q = qk_ref[:, :GLA_QK].astype(F32) * (GLA_DK ** -0.5)
    k = qk_ref[:, GLA_QK:].astype(F32)
    q_t = (q * jnp.exp(b)).astype(BF16)
    k_t = (k * jnp.exp(-b)).astype(BF16)
    k_end = k * jnp.exp(bl - b)
    la_hi = la_hi.astype(F32)
    la_lo = la_lo.astype(F32)
    row_seq = lax.broadcasted_iota(jnp.int32, (n, GLA_DK), 0) >> 3
    seq3 = lax.broadcasted_iota(jnp.int32, (SEQ_PER_BLK, GLA_DK, n), 0)
    lane_seq3 = lax.broadcasted_iota(jnp.int32, (SEQ_PER_BLK, GLA_DK, n), 2) >> 3
    mask3 = seq3 == lane_seq3
    ones = jnp.ones((n, GLA_DV), BF16)
    big = SEQ_PER_BLK * GLA_DK

    def per_seq(x_tr):
        x3 = jnp.where(mask3, x_tr[None, :, :], 0.0)
        return x3.reshape(big, n).astype(BF16)

    for hd in range(GLA_HEADS):
        ks = slice(hd * GLA_DK, (hd + 1) * GLA_DK)
        vs = slice(hd * GLA_DV, (hd + 1) * GLA_DV)
        qh = q_t[:, ks]
        vh = v_ref[:, vs]
        att = jnp.where(tril, _dot_nt(qh, k_t[:, ks]), 0.0).astype(BF16)
        o_intra = _dot(att, vh)
        s_old = s0_ref[:, hd].reshape(big, GLA_DV)
        zero = jnp.zeros_like(qh)
        q_big = jnp.concatenate(
            [jnp.where(row_seq == j, qh, zero) for j in range(SEQ_PER_BLK)], axis=1)
        o = o_intra + _dot(q_big, s_old.astype(BF16))
        dlog = _dot(per_seq(la_hi[:, ks].T), ones) + _dot(per_seq(la_lo[:, ks].T), ones)
        upd = _dot(per_seq(k_end[:, ks].T), vh)
        s_new = jnp.exp(dlog) * s_old + upd
        ns_ref[:, hd] = s_new.reshape(SEQ_PER_BLK, GLA_DK, GLA_DV)
        r = r_ref[:, vs]
        acts_ref[:, CONV_CH + hd * GLA_DV:CONV_CH + (hd + 1) * GLA_DV] = (
            _gla_out_gate(o, gng_ref, r).astype(BF16))

    us, vg = _chunk_mlp(bgu_ref, vv_ref, cmg_ref, ws_ref, cmb_ref, 1)
    acts_ref[:, CONV_CH + GLA_V:] = us.astype(BF16)
    vrow_ref[...] = vg


N_MIX_S_IN = 17


def _mix_sample_kernel(*refs, layer):
    if layer == 0:
        @pl.when(pl.program_id(0) == 0)
        def _():
            _mix_sample_body(*refs)

        @pl.when(pl.program_id(0) > 0)
        def _():
            ns_ref = refs[N_MIX_S_IN + 2]
            ns_ref[...] = jnp.zeros_like(ns_ref)
    else:
        _mix_sample_body(*refs[:N_MIX_S_IN], *refs[N_MIX_S_IN + 1:])


def _mix_sample(z, alr, a2, ab, cw, gng, cmg, ws, cmb, p1, p2, s0, ns_all, layer):
    row0 = T_PROMPT // ROWS_S
    n_i = DEC_BATCH // SEQ_PER_BLK
    n_pass = DEPTH if layer == 0 else 1
    blk = lambda p, i: jnp.where(p == 0, i, n_i - 1)
    row_map = lambda p, i: row0 + blk(p, i)
    slot = lambda p, i: (layer + p, i, 0, 0, 0)
    small = (a2, ab, cw, gng, cmg, ws, cmb)
    state_blk = (None, SEQ_PER_BLK, GLA_HEADS, GLA_DK, GLA_DV)
    in_specs = _z_specs(ROWS_S, row_map) + [
        pl.BlockSpec((ROWS_S, LANES), lambda p, i: (row_map(p, i), 0)),
    ] + [_layer_spec(a, layer) for a in small] + [
        pl.BlockSpec((ROWS_S, CONV_CH), lambda p, i: (blk(p, i), 0)),
        pl.BlockSpec((ROWS_S, CONV_CH), lambda p, i: (blk(p, i), 0)),
        pl.BlockSpec(state_blk, lambda p, i: (layer, blk(p, i), 0, 0, 0)),
    ]
    args = (z, z, z, z, z, z, alr, *small, p1, p2, s0)
    assert len(args) == N_MIX_S_IN
    aliases = {}
    if layer > 0:
        in_specs.append(pl.BlockSpec(memory_space=pl.ANY))
        args += (ns_all,)
        aliases = {N_MIX_S_IN: 2}
    return pl.pallas_call(
        functools.partial(_mix_sample_kernel, layer=layer),
        grid=(n_pass, n_i),
        in_specs=in_specs,
        out_specs=[
            pl.BlockSpec((ROWS_S, ACT_COLS), lambda p, i: (blk(p, i), 0)),
            pl.BlockSpec((ROWS_S, CONV_CH), lambda p, i: (blk(p, i), 0)),
            pl.BlockSpec(state_blk, slot),
            pl.BlockSpec((ROWS_S, CM_CH), lambda p, i: (blk(p, i), 0)),
        ],
        out_shape=[
            jax.ShapeDtypeStruct((T_SAMPLE, ACT_COLS), BF16),
            jax.ShapeDtypeStruct((T_SAMPLE, CONV_CH), F32),
            jax.ShapeDtypeStruct((DEPTH, DEC_BATCH, GLA_HEADS, GLA_DK, GLA_DV), F32),
            jax.ShapeDtypeStruct((T_SAMPLE, CM_CH), F32),
        ],
        input_output_aliases=aliases,
        compiler_params=pltpu.CompilerParams(
            dimension_semantics=("arbitrary", "arbitrary"), vmem_limit_bytes=VMEM_LIMIT),
        name="mix_sample",
    )(*args)


def _delayed(gen, n_stages):
    for _ in range(n_stages):
        yield
    yield from gen


def _interleave(*stage_lists):
    pending = list(stage_lists)
    while pending:
        for gen in list(pending):
            if next(gen, StopIteration) is StopIteration:
                pending.remove(gen)


def _outproj_main(i, n_xa, actp_ref, acts_ref, gates_ref, xa_ref, xb_ref, pa_ref, pb_ref, pc_ref,
                  wo_ref, n2_ref, xo_ref, xn_ref, xnb_ref):
    acts = jnp.where(i < T_PROMPT // TM_OUT, actp_ref[...], acts_ref[...])
    x_in = jnp.where(i < n_xa, xa_ref[...], xb_ref[...])
    yb = _dot(acts[:, CONV_CH:CONV_CH + GLA_V], pb_ref[...])
    ga = _sigmoid(gates_ref[:, COL_GA:COL_GA + D_MODEL])
    ya = _dot(acts[:, :CONV_CH], pa_ref[...])
    gb = _sigmoid(gates_ref[:, COL_GB:COL_GB + D_MODEL])
    yield
    yc = _dot(acts[:, CONV_CH + GLA_V:], pc_ref[...])
    gc = _sigmoid(gates_ref[:, COL_GC:COL_GC + D_MODEL])
    mix = ga * ya + gb * yb
    yield
    mix = mix + gc * yc
    x = x_in + _dot(mix.astype(BF16), wo_ref[...])
    xo_ref[...] = x
    yield
    xn = _rms(x, n2_ref[...])
    xn_ref[...] = xn
    xnb = xn.astype(BF16)
    yield
    xnb_ref[...] = xnb


def _outproj_route(live, xnb_ref, wr_ref, rb_ref, route_ref, seg_ref, cnt_ref):
    tm = TM_OUT
    logits = _dot_nt(wr_ref[...], xnb_ref[...]) + rb_ref[...]
    yield
    n_sub = EXPERTS_PER_GROUP
    sub = lax.broadcasted_iota(jnp.int32, (n_sub, tm), 0).astype(F32)
    neg = jnp.float32(-jnp.inf)
    lg = logits[0:MOE_GROUPS]
    gmax = jnp.max(lg, axis=0, keepdims=True)
    grp = jnp.min(jnp.where(lg == gmax, sub, 1e9), axis=0, keepdims=True)
    p_grp = 1.0 / jnp.sum(jnp.exp(lg - gmax), axis=0, keepdims=True)
    yield
    le = logits[MOE_GROUPS:MOE_GROUPS + n_sub]
    for g in range(1, MOE_GROUPS):
        le = jnp.where(grp == g, logits[MOE_GROUPS + g * n_sub:MOE_GROUPS + (g + 1) * n_sub], le)
    v1 = jnp.max(le, axis=0, keepdims=True)
    i1 = jnp.min(jnp.where(le == v1, sub, 1e9), axis=0, keepdims=True)
    le2 = jnp.where(sub == i1, neg, le)
    v2 = jnp.max(le2, axis=0, keepdims=True)
    i2 = jnp.min(jnp.where(le2 == v2, sub, 1e9), axis=0, keepdims=True)
    yield
    t = jnp.exp(v2 - v1)
    g1 = p_grp / (1.0 + t)
    g2 = p_grp * t / (1.0 + t)
    gate_t = jnp.where(sub == i1, g1, jnp.where(sub == i2, g2, 0.0))

    onehot_t = jnp.where(sub == grp, 1.0, 0.0)
    csum = jnp.where(live, jnp.sum(onehot_t, axis=1, keepdims=True), 0.0)
    rr = lax.broadcasted_iota(jnp.int32, (tm, tm), 0)
    cc = lax.broadcasted_iota(jnp.int32, (tm, tm), 1)
    earlier = jnp.where(rr < cc, 1.0, 0.0).astype(BF16)
    padded = jnp.concatenate([onehot_t, jnp.zeros_like(onehot_t)], axis=0).astype(BF16)
    same_before = _dot(padded, earlier)[0:n_sub]
    yield
    lower = jnp.sum(jnp.where(sub < grp, csum, 0.0), axis=0, keepdims=True)
    lrank = lower + jnp.sum(onehot_t * same_before, axis=0, keepdims=True)
    route_ref[0:n_sub, :] = gate_t
    route_ref[n_sub:, :] = jnp.broadcast_to(lrank, (n_sub, tm))
    carry = cnt_ref[...]
    lane = lax.broadcasted_iota(jnp.int32, (n_sub, LANES), 1)
    seg_ref[...] = jnp.where(lane == SEG_START, carry,
                             jnp.where(lane == SEG_LEN, csum, 0.0)).astype(jnp.int32)
    cnt_ref[...] = carry + csum


def _outproj_kernel(actp_ref, acts_ref, gates_ref, xa_ref, xb_ref, pa_ref, pb_ref, pc_ref, wo_ref,
                    n2_ref, wr_ref, rb_ref, xo_ref, xn_ref, route_ref, seg_ref, cnt_ref, xnb_ref,
                    *, n_xa):
    i = pl.program_id(0)

    @pl.when(i == 0)
    def _():
        cnt_ref[...] = jnp.zeros_like(cnt_ref)
        xnb_ref[...] = jnp.zeros_like(xnb_ref)

    tile = jnp.minimum(i, pl.num_programs(0) - 2)
    _interleave(
        _delayed(_outproj_route(i >= 1, xnb_ref, wr_ref, rb_ref, route_ref, seg_ref, cnt_ref),
                 ROUTE_DELAY),
        _outproj_main(tile, n_xa, actp_ref, acts_ref, gates_ref, xa_ref, xb_ref, pa_ref, pb_ref,
                      pc_ref, wo_ref, n2_ref, xo_ref, xn_ref, xnb_ref))


def _lane_prefix(v):
    rr = lax.broadcasted_iota(jnp.int32, (LANES, LANES), 0)
    cc = lax.broadcasted_iota(jnp.int32, (LANES, LANES), 1)
    earlier = jnp.where(rr < cc, 1.0, 0.0).astype(BF16)
    return _dot(jnp.broadcast_to(v, (8, LANES)).astype(BF16), earlier)[0:1]


def _outproj(acts_p, acts_s, z, xa, xb, pa, pb, pc, wo, n2, wr, rb, layer):
    m = T_ALL
    nt = m // TM_OUT
    row = lambda i: (jnp.minimum(i, nt - 1), 0)
    routed = lambda i: (jnp.maximum(i - 1, 0), 0, 0)
    _, act_specs = _two_part_specs(TM_OUT, acts_p, acts_s)
    n_xa, x_specs = _two_part_specs(TM_OUT, xa, xb)
    weights = (pa, pb, pc, wo, n2, wr, rb)
    return pl.pallas_call(
        functools.partial(_outproj_kernel, n_xa=n_xa),
        grid=(nt + 1,),
        in_specs=act_specs + [pl.BlockSpec((TM_OUT, 3 * D_MODEL), row)] + x_specs
        + [_layer_spec(w, layer) for w in weights],
        out_specs=[
            pl.BlockSpec((TM_OUT, D_MODEL), row),
            pl.BlockSpec((TM_OUT, D_MODEL), row),
            pl.BlockSpec((None, ROUTE_ROWS, TM_OUT), routed),
            pl.BlockSpec((None, 8, LANES), routed),
            pl.BlockSpec((8, LANES), lambda i: (0, 0)),
        ],
        scratch_shapes=[pltpu.VMEM((TM_OUT, D_MODEL), BF16)],
        out_shape=[
            jax.ShapeDtypeStruct((m, D_MODEL), F32),
            jax.ShapeDtypeStruct((m, D_MODEL), F32),
            jax.ShapeDtypeStruct((m // TM_OUT, ROUTE_ROWS, TM_OUT), F32),
            jax.ShapeDtypeStruct((m // TM_OUT, 8, LANES), jnp.int32),
            jax.ShapeDtypeStruct((8, LANES), F32),
        ],
        compiler_params=pltpu.CompilerParams(
            dimension_semantics=("arbitrary",), vmem_limit_bytes=VMEM_LIMIT),
        name="outproj",
    )(acts_p, acts_s, z, xa, xb, *weights)


PLAN_GROUP, PLAN_USED, PLAN_END, PLAN_BLOCKS, PLAN_BASE = 0, 1, 2, 3, 4


def _plan_kernel(cnt_ref, plan_ref):
    sub_i = lax.broadcasted_iota(jnp.int32, (8, LANES), 0)
    lane8_i = lax.broadcasted_iota(jnp.int32, (8, LANES), 1)
    cnt = jnp.sum(jnp.where(sub_i == lane8_i, cnt_ref[...], 0.0), axis=0, keepdims=True)
    blocks = jnp.floor((cnt + (MOE_BLK - 1)) * (1.0 / MOE_BLK))
    start = _lane_prefix(blocks)
    end = start + blocks
    lane_i = lax.broadcasted_iota(jnp.int32, (1, LANES), 1)
    lane = lane_i.astype(F32)
    grp_of_blk = jnp.zeros((1, LANES), F32)
    for g in range(MOE_GROUPS):
        end_g = jnp.sum(jnp.where(lane_i == g, end, 0.0), axis=-1, keepdims=True)
        grp_of_blk += jnp.where(lane >= end_g, 1.0, 0.0)
    grp_of_blk = jnp.minimum(grp_of_blk, MOE_GROUPS - 1)
    n_used = jnp.sum(blocks, axis=-1, keepdims=True)
    row = lax.broadcasted_iota(jnp.int32, (8, LANES), 0)
    plan_ref[...] = jnp.where(
        row == PLAN_GROUP, grp_of_blk,
        jnp.where(row == PLAN_USED, n_used,
                  jnp.where(row == PLAN_END, end,
                            jnp.where(row == PLAN_BLOCKS, blocks,
                                      jnp.where(row == PLAN_BASE, start * MOE_BLK, 0.0))))
    ).astype(jnp.int32)


def _plan(cnt):
    return pl.pallas_call(
        _plan_kernel,
        out_shape=jax.ShapeDtypeStruct((8, LANES), jnp.int32),
        name="plan",
    )(cnt)


SEG_START, SEG_LEN = 0, 1


def _segment_copies(seg_ref, base_ref, make_copy):
    local = 0
    for g in range(MOE_GROUPS):
        n = seg_ref[g, SEG_LEN]
        first = base_ref[g] + seg_ref[g, SEG_START]
        k = TM_ROW
        while k >= 1:
            done = n & ~(2 * k - 1)
            @pl.when((n & k) != 0)
            def _():
                make_copy(local + done, first + done, k).start()
            k //= 2
        local = local + n


def _perm_matrix(route_ref):
    row = lax.broadcasted_iota(jnp.int32, (TM_ROW, TM_ROW), 0).astype(F32)
    return jnp.where(row == route_ref[ROUTE_LRANK:ROUTE_LRANK + 1, :], 1.0, 0.0).astype(BF16)


SUB = 8
SUB_X = D_MODEL // 2 // LANES
SUB_GATE = SUB_X
HI_MASK = -65536


def _rows(first_row, n_rows):
    return pl.ds(pl.multiple_of(first_row * SUB, SUB), n_rows * SUB)


def _sublane(s, n_rows):
    return pl.ds(s, n_rows, stride=SUB)


def _scatter_kernel(nu_ref, end_ref, nb_ref, base_ref, seg_ref, route_ref, xn_ref, xs_ref,
                    zbuf, sbuf, sems):
    i = pl.program_id(0)
    n = pl.num_programs(0)
    sem = sems.at[0]

    def zero_block(b):
        return pltpu.make_async_copy(zbuf, xs_ref.at[_rows(b * MOE_BLK, MOE_BLK)], sem)

    def each_unfilled_block(fn):
        for g in range(MOE_GROUPS):
            @pl.when(nb_ref[g] > 0)
            def _():
                fn(zero_block(end_ref[g] - 1))
        for b in range(T_ALL // MOE_BLK, N_BLK):
            @pl.when(b >= nu_ref[0])
            def _():
                fn(zero_block(b))

    @pl.when(i == 0)
    def _():
        zbuf[...] = jnp.zeros_like(zbuf)
        sbuf[...] = jnp.zeros_like(sbuf)
        each_unfilled_block(lambda c: c.start())
        each_unfilled_block(lambda c: c.wait())

    def tile_done(s):
        pltpu.make_async_copy(sbuf.at[s], xs_ref.at[_rows(0, TM_ROW)], sems.at[s]).wait()

    @pl.when(i >= 2)
    def _():
        for t in range(ROW_TILES):
            tile_done((i % 2) * ROW_TILES + t)

    def sort_tile(t):
        slot = (i % 2) * ROW_TILES + t
        route = route_ref.at[t]
        perm = _perm_matrix(route)
        xn = xn_ref[t * TM_ROW:(t + 1) * TM_ROW, :]
        xs = lax.bitcast_convert_type(_dot(perm, xn.astype(BF16)), jnp.int32)
        yield
        half = D_MODEL // 2
        for s in range(SUB_X):
            hi = xs[:, s * LANES:(s + 1) * LANES] & HI_MASK
            lo = lax.shift_right_logical(xs[:, half + s * LANES:half + (s + 1) * LANES], 16)
            sbuf[slot, _sublane(s, TM_ROW), :] = hi | lo
        yield
        gate_t = route[0:EXPERTS_PER_GROUP, :]
        gate_t = jnp.concatenate(
            [gate_t, jnp.zeros((LANES - EXPERTS_PER_GROUP, TM_ROW), F32)], axis=0)
        g1 = gate_t.astype(BF16)
        r1 = gate_t - g1.astype(F32)
        g2 = r1.astype(BF16)
        g3 = (r1 - g2.astype(F32)).astype(BF16)
        sbuf[slot, _sublane(SUB_GATE, TM_ROW), :] = lax.bitcast_convert_type(
            _dot_nt(perm, g1) + _dot_nt(perm, g2) + _dot_nt(perm, g3), jnp.int32)

    _interleave(*[_delayed(sort_tile(t), t) for t in range(ROW_TILES)])

    for t in range(ROW_TILES):
        def make_copy(src_row, dst_row, k, slot=(i % 2) * ROW_TILES + t):
            return pltpu.make_async_copy(sbuf.at[slot, _rows(src_row, k)],
                                         xs_ref.at[_rows(dst_row, k)], sems.at[slot])
        _segment_copies(seg_ref.at[t], base_ref, make_copy)

    @pl.when(i == n - 1)
    def _():
        for t in range(ROW_TILES):
            tile_done((i % 2) * ROW_TILES + t)

            @pl.when(n > 1)
            def _():
                tile_done((1 - i % 2) * ROW_TILES + t)


def _scatter(seg, route, xn, n_used, grp_end, grp_blocks, grp_base):
    m = xn.shape[0]
    step_rows = ROW_TILES * TM_ROW
    grid_spec = pltpu.PrefetchScalarGridSpec(
        num_scalar_prefetch=4,
        grid=(m // step_rows,),
        in_specs=[
            pl.BlockSpec((ROW_TILES, 8, LANES), lambda i, *_: (i, 0, 0), memory_space=pltpu.SMEM),
            pl.BlockSpec((ROW_TILES, ROUTE_ROWS, TM_ROW), lambda i, *_: (i, 0, 0)),
            pl.BlockSpec((step_rows, D_MODEL), lambda i, *_: (i, 0)),
        ],
        out_specs=pl.BlockSpec(memory_space=pl.ANY),
        scratch_shapes=[pltpu.VMEM((MOE_BLK * SUB, LANES), jnp.int32),
                        pltpu.VMEM((2 * ROW_TILES, TM_ROW * SUB, LANES), jnp.int32),
                        pltpu.SemaphoreType.DMA((2 * ROW_TILES,))],
    )
    return pl.pallas_call(
        _scatter_kernel,
        grid_spec=grid_spec,
        out_shape=jax.ShapeDtypeStruct((N_SORTED * SUB, LANES), jnp.int32),
        compiler_params=pltpu.CompilerParams(
            dimension_semantics=("arbitrary",), vmem_limit_bytes=VMEM_LIMIT),
        name="scatter",
    )(n_used, grp_end, grp_blocks, grp_base, seg, route, xn)


def _ffn_kernel(bg_ref, nu_ref, xs_ref, w1_ref, w3_ref, w2_ref, y_ref):
    del bg_ref
    b = pl.program_id(0)

    def block_inputs():
        packed = [xs_ref[_sublane(s, MOE_BLK), :] for s in range(SUB_X)]
        x = jnp.concatenate(
            [lax.bitcast_convert_type(u & HI_MASK, F32).astype(BF16) for u in packed]
            + [lax.bitcast_convert_type(lax.shift_left(u, 16), F32).astype(BF16) for u in packed],
            axis=1)
        return x, lax.bitcast_convert_type(xs_ref[_sublane(SUB_GATE, MOE_BLK), :], F32)

    def store(y):
        for s in range(SUB):
            y_ref[_sublane(s, MOE_BLK), :] = y[:, s * LANES:(s + 1) * LANES]

    @pl.when(b < nu_ref[0])
    def _():
        x, gates = block_inputs()
        n_e = EXPERTS_PER_GROUP
        h1 = [_dot(x, w1_ref[e]) for e in range(n_e)]
        h3 = [_dot(x, w3_ref[e]) for e in range(n_e)]
        hs = []
        for e in range(n_e):
            ge = gates[:, e:e + 1]
            hs.append(jnp.where(ge > 0.0, h1[e] * _sigmoid(h1[e]) * h3[e] * ge, 0.0).astype(BF16))
        hcat = jnp.concatenate(hs, axis=1)
        store(_dot(hcat, w2_ref[...].reshape(n_e * D_EXPERT, D_MODEL)))

    @pl.when(b >= nu_ref[0])
    def _():
        y_ref[...] = jnp.zeros_like(y_ref)


def _ffn(blk_group, n_used, xs, w1, w3, w2):
    grouped = (MOE_GROUPS, EXPERTS_PER_GROUP)
    w1, w3, w2 = (w.reshape(grouped + w.shape[1:]) for w in (w1, w3, w2))
    wmap = lambda b, bg, nu: (bg[b], 0, 0, 0)
    grid_spec = pltpu.PrefetchScalarGridSpec(
        num_scalar_prefetch=2,
        grid=(N_BLK,),
        in_specs=[
            pl.BlockSpec((MOE_BLK * SUB, LANES), lambda b, bg, nu: (b, 0)),
            pl.BlockSpec((None, EXPERTS_PER_GROUP, D_MODEL, D_EXPERT), wmap),
            pl.BlockSpec((None, EXPERTS_PER_GROUP, D_MODEL, D_EXPERT), wmap),
            pl.BlockSpec((None, EXPERTS_PER_GROUP, D_EXPERT, D_MODEL), wmap),
        ],
        out_specs=pl.BlockSpec((MOE_BLK * SUB, LANES), lambda b, bg, nu: (b, 0)),
    )
    return pl.pallas_call(
        _ffn_kernel,
        grid_spec=grid_spec,
        out_shape=jax.ShapeDtypeStruct((N_SORTED * SUB, LANES), F32),
        compiler_params=pltpu.CompilerParams(
            dimension_semantics=("arbitrary",), vmem_limit_bytes=VMEM_LIMIT),
        name="ffn",
    )(blk_group, n_used, xs, w1, w3, w2)


def _combine_kernel(base_ref, seg_ref, segn_ref, route_ref, ys_hbm, x_ref, g_ref, wa_ref, *refs,
                    final):
    outs, (buf, sem, stage) = refs[:-3], refs[-3:]
    i = pl.program_id(0)
    n = pl.num_programs(0)
    step_rows = ROW_TILES * TM_ROW

    def gather(seg, parity):
        for t in range(ROW_TILES):
            s = parity * ROW_TILES + t

            def make_copy(buf_row, ys_row, k, s=s):
                return pltpu.make_async_copy(ys_hbm.at[_rows(ys_row, k)],
                                             buf.at[s, _rows(buf_row, k)], sem.at[s])
            _segment_copies(seg.at[t], base_ref, make_copy)

    @pl.when(i == 0)
    def _():
        gather(seg_ref, 0)

    @pl.when(i + 1 < n)
    def _():
        gather(segn_ref, 1 - i % 2)

    for t in range(ROW_TILES):
        slot = (i % 2) * ROW_TILES + t
        pltpu.make_async_copy(ys_hbm.at[_rows(0, TM_ROW)], buf.at[slot], sem.at[slot]).wait()

    def tile_stages(t):
        slot = (i % 2) * ROW_TILES + t
        rows = slice(t * TM_ROW, (t + 1) * TM_ROW)
        tn = (((0,), (0,)), ((), ()))
        perm = _perm_matrix(route_ref.at[t])
        cols = []
        for s in range(SUB):
            hi, lo = _split_bf16(buf[slot, _sublane(s, TM_ROW), :])
            cols.append(lax.dot_general(perm, hi, tn, preferred_element_type=F32)
                        + lax.dot_general(perm, lo, tn, preferred_element_type=F32))
            if s % 4 == 3:
                yield
        y = x_ref[rows, :] + jnp.concatenate(cols, axis=1)
        if not final:
            outs[0][rows, :] = y
            yield
            _norm_and_lowrank(y, g_ref, wa_ref, outs[1].at[rows, :], outs[2].at[rows, :])
        else:
            yield
            stage[rows, :] = _rms(y, g_ref[...])

    _interleave(*[_delayed(tile_stages(t), t) for t in range(ROW_TILES)])

    if final:
        @pl.when(i < T_PROMPT // step_rows)
        def _():
            outs[0][...] = stage[...]

        @pl.when(i >= T_PROMPT // step_rows)
        def _():
            outs[1][...] = stage[...]


def _combine(seg, route, grp_base, ys, x, g, wa, final):
    m = x.shape[0]
    step_rows = ROW_TILES * TM_ROW
    nt = m // step_rows
    n_p = T_PROMPT // step_rows
    smem = functools.partial(pl.BlockSpec, (ROW_TILES, 8, LANES), memory_space=pltpu.SMEM)
    tile = (step_rows, D_MODEL)
    if final:
        out_specs = [pl.BlockSpec(tile, lambda i, *_: (jnp.minimum(i, n_p - 1), 0)),
                     pl.BlockSpec(tile, lambda i, *_: (jnp.maximum(i - n_p, 0), 0))]
        out_shape = [jax.ShapeDtypeStruct((T_PROMPT, D_MODEL), F32),
                     jax.ShapeDtypeStruct((T_SAMPLE, D_MODEL), F32)]
    else:
        out_specs = [pl.BlockSpec(tile, lambda i, *_: (i, 0)),
                     pl.BlockSpec(tile, lambda i, *_: (i, 0)),
                     pl.BlockSpec((step_rows, LANES), lambda i, *_: (i, 0))]
        out_shape = [jax.ShapeDtypeStruct((m, D_MODEL), F32),
                     jax.ShapeDtypeStruct((m, D_MODEL), BF16),
                     jax.ShapeDtypeStruct((m, LANES), BF16)]
    grid_spec = pltpu.PrefetchScalarGridSpec(
        num_scalar_prefetch=1,
        grid=(nt,),
        in_specs=[
            smem(lambda i, *_: (i, 0, 0)),
            smem(lambda i, *_: (jnp.minimum(i + 1, nt - 1), 0, 0)),
            pl.BlockSpec((ROW_TILES, ROUTE_ROWS, TM_ROW), lambda i, *_: (i, 0, 0)),
            pl.BlockSpec(memory_space=pl.ANY),
            pl.BlockSpec(tile, lambda i, *_: (i, 0)),
            pl.BlockSpec(g.shape, lambda i, *_: (0, 0)),
            pl.BlockSpec(wa.shape, lambda i, *_: (0, 0)),
        ],
        out_specs=out_specs,
        scratch_shapes=[pltpu.VMEM((2 * ROW_TILES, TM_ROW * SUB, LANES), F32),
                        pltpu.SemaphoreType.DMA((2 * ROW_TILES,)),
                        pltpu.VMEM(tile if final else (8, LANES), F32)],
    )
    return pl.pallas_call(
        functools.partial(_combine_kernel, final=final),
        grid_spec=grid_spec,
        out_shape=out_shape,
        compiler_params=pltpu.CompilerParams(
            dimension_semantics=("arbitrary",), vmem_limit_bytes=VMEM_LIMIT),
        name="combine",
    )(grp_base, seg, seg, route, ys, x, g, wa)


def _prep_weights(w_in, gla_a2, cm_ws, cm_b, router_group_w, router_group_b,
                  router_expert_w, router_expert_b):
    off = {}
    o = 0
    for name, n in (("h", 512), ("cg", 512), ("bg", 512), ("q", 512), ("k", 512), ("v", 1024),
                    ("r", 1024), ("alr", 16), ("u", 512), ("vv", 512), ("ga", 1024),
                    ("gb", 1024), ("gc", 1024)):
        off[name] = (o, n)
        o += n
    runs = (("ga", "gc"), ("v", "r"), ("q", "k"), ("h", "cg"), ("bg", "bg"), ("u", "vv"))
    w_t = jnp.swapaxes(w_in, 1, 2)
    w_z = jnp.concatenate([w_t[:, off[a][0]:off[b][0] + off[b][1]] for a, b in runs],
                          axis=1).astype(BF16)
    assert w_z.shape[1] == Z_COLS
    a0 = off["alr"][0]
    w_alr = jnp.pad(w_t[:, a0:a0 + GLA_LOWRANK],
                    ((0, 0), (0, LANES - GLA_LOWRANK), (0, 0))).astype(BF16)
    a2 = jnp.pad(gla_a2, ((0, 0), (0, LANES - GLA_LOWRANK), (0, 0))).astype(BF16)
    ws_p = jnp.tril(cm_ws).astype(BF16)
    small = jnp.tril(cm_ws[:, :, :DEC_SEQ, :DEC_SEQ])
    eye = jnp.eye(SEQ_PER_BLK, dtype=F32)
    ws_s = jnp.einsum("ij,lgab->lgiajb", eye, small).reshape(
        DEPTH, CM_GROUPS, ROWS_S, ROWS_S).astype(BF16)
    cmb_p = jnp.broadcast_to(jnp.transpose(cm_b, (0, 2, 1))[:, :, :, None],
                             (DEPTH, CM_CHUNK, CM_GROUPS, CM_GCH)).reshape(DEPTH, CM_CHUNK, CM_CH)
    cmb_s = jnp.tile(cmb_p[:, :DEC_SEQ], (1, SEQ_PER_BLK, 1))
    pad = LANES - MOE_GROUPS - N_EXPERTS
    w_r = jnp.pad(jnp.swapaxes(jnp.concatenate([router_group_w, router_expert_w], axis=-1), 1, 2),
                  ((0, 0), (0, pad), (0, 0)))
    r_b = jnp.pad(jnp.concatenate([router_group_b, router_expert_b], axis=-1),
                  ((0, 0), (0, pad)))[:, :, None]
    return w_z, w_alr, a2, ws_p, ws_s, cmb_p, cmb_s, w_r.astype(BF16), r_b


def kernel(x_prompt, x_sample, state_conv, state_gla, norm1_g, w_in, conv_w, gla_a2, gla_a_b,
           gla_norm_g, cm_norm_g, cm_ws, cm_b, proj_a, proj_b, proj_c, w_out, norm2_g,
           router_group_w, router_group_b, router_expert_w, router_expert_b,
           exp_w1, exp_w3, exp_w2, final_norm_g):
    (w_z, w_alr, a2, ws_p, ws_s, cmb_p, cmb_s, w_r, r_b) = _prep_weights(
        w_in, gla_a2, cm_ws, cm_b, router_group_w, router_group_b, router_expert_w,
        router_expert_b)
    pa, pb, pc, wo = (w.astype(BF16) for w in (proj_a, proj_b, proj_c, w_out))
    n1 = norm1_g[:, None, :]
    n2 = norm2_g[:, None, :]
    ab = gla_a_b[:, None, :]
    gng = gla_norm_g[:, None, :]
    cmg = cm_norm_g.reshape(DEPTH, 1, CM_CH)
    fg = final_norm_g[None, :]
    xa = x_prompt.reshape(T_PROMPT, D_MODEL)
    xb = x_sample.reshape(T_SAMPLE, D_MODEL)
    gla_s = None
    conv_p, gla_p, conv_s, cmv_s = [], [], [], []
    xin, alr = _prenorm(xa, xb, n1, w_alr, 0)
    for l in range(DEPTH):
        z, *experts = _inproj(xin, w_z, (exp_w1, exp_w3, exp_w2), l)
        acts_p, nconv, ngla = _mix_prompt(z, alr, a2, ab, conv_w, gng, cmg, ws_p, cmb_p, l)
        sc = state_conv[l]
        p2 = jnp.pad(sc, ((0, 0), (0, DEC_SEQ - 2), (0, 0))).reshape(T_SAMPLE, CONV_CH)
        p1 = jnp.pad(sc[:, 1:2], ((0, 0), (0, DEC_SEQ - 1), (0, 0))).reshape(T_SAMPLE, CONV_CH)
        acts_s, cin_s, gla_s, vrows = _mix_sample(z, alr, a2, ab, conv_w, gng, cmg, ws_s, cmb_s,
                                                  p1, p2, state_gla, gla_s, l)
        conv_p.append(nconv)
        gla_p.append(ngla)
        conv_s.append(cin_s.reshape(DEC_BATCH, DEC_SEQ, CONV_CH)[:, DEC_SEQ - (CONV_K - 1):])
        cmv_s.append(vrows.reshape(DEC_BATCH, DEC_SEQ, CM_CH))

        x, xn, route, seg, cnt = _outproj(acts_p, acts_s, z, xa, xb, pa, pb, pc, wo, n2, w_r, r_b,
                                          l)
        plan = _plan(cnt)
        n_used = plan[PLAN_USED, :1]
        grp_base = plan[PLAN_BASE, :MOE_GROUPS]
        xs = _scatter(seg, route, xn, n_used, plan[PLAN_END, :MOE_GROUPS],
                      plan[PLAN_BLOCKS, :MOE_GROUPS], grp_base)
        ys = _ffn(plan[PLAN_GROUP, :N_BLK], n_used, xs, *experts)
        if l == DEPTH - 1:
            out = _combine(seg, route, grp_base, ys, x, fg, w_alr[l], True)
        else:
            xa, xin, alr = _combine(seg, route, grp_base, ys, x, n1[l + 1], w_alr[l + 1], False)
            xb = xa

    y_prompt = out[0].reshape(BATCH, SEQ, D_MODEL)
    y_sample = out[1].reshape(DEC_BATCH, DEC_SEQ, D_MODEL)
    return (y_prompt, y_sample, jnp.stack(conv_p), jnp.stack(gla_p), jnp.stack(conv_s),
            gla_s, jnp.stack(cmv_s))
```

```python
import functools

import jax
import jax.numpy as jnp
from jax import lax
from jax.experimental import pallas as pl
from jax.experimental.pallas import tpu as pltpu

F32 = jnp.float32
BF16 = jnp.bfloat16

D_MODEL = 1024
BATCH = 8
SEQ = 2048
DEPTH = 2
DEC_BATCH = 128
DEC_SEQ = 8
CONV_K = 3
CONV_CH = 512
GLA_HEADS = 4
GLA_DK = 128
GLA_DV = 256
GLA_QK = GLA_HEADS * GLA_DK
GLA_V = GLA_HEADS * GLA_DV
GLA_LOWRANK = 16
GLA_TAU = 16.0
GLA_CHUNK = 64
CM_GROUPS = 4
CM_CHUNK = 128
CM_GCH = 128
CM_CH = 512
MOE_GROUPS = 8
EXPERTS_PER_GROUP = 8
N_EXPERTS = 64
D_EXPERT = 256
EPS = 1e-6

LANES = 128
T_PROMPT = BATCH * SEQ
T_SAMPLE = DEC_BATCH * DEC_SEQ
T_ALL = T_PROMPT + T_SAMPLE

COL_GA, COL_GB, COL_GC = 0, 1024, 2048
COL_V, COL_R, COL_Q, COL_K = 3072, 4096, 5120, 5632
COL_H, COL_CG, COL_BG, COL_U, COL_VV = 6144, 6656, 7168, 7680, 8192
Z_COLS = 8704
ACT_COLS = 2048
ROUTE_ROWS = 2 * EXPERTS_PER_GROUP
ROUTE_LRANK = EXPERTS_PER_GROUP

TM_PRE = 1024
TM_IN = 512
TN_IN = Z_COLS // 2

W_IN_WIDTHS = (("h", 512), ("cg", 512), ("bg", 512), ("q", 512), ("k", 512), ("v", 1024),
               ("r", 1024), ("alr", GLA_LOWRANK), ("u", 512), ("vv", 512), ("ga", 1024),
               ("gb", 1024), ("gc", 1024))
Z_ORDER = ("ga", "gb", "gc", "v", "r", "q", "k", "h", "cg", "bg", "u", "vv")


def _w_in_offsets():
    off, o = {}, 0
    for name, n in W_IN_WIDTHS:
        off[name] = (o, n)
        o += n
    return off


def _w_in_copies():
    off = _w_in_offsets()
    copies = [[] for _ in range(Z_COLS // TN_IN)]
    dst = 0
    for name in Z_ORDER:
        src, n = off[name]
        while n:
            tile = dst // TN_IN
            at = dst - tile * TN_IN
            take = min(n, TN_IN - at)
            runs = copies[tile]
            if runs and runs[-1][0] + runs[-1][2] == src and runs[-1][1] + runs[-1][2] == at:
                runs[-1] = (runs[-1][0], runs[-1][1], runs[-1][2] + take)
            else:
                runs.append((src, at, take))
            src, dst, n = src + take, dst + take, n - take
    assert dst == Z_COLS
    assert all(v % 16 == 0 for runs in copies for run in runs for v in run)
    return copies
TC_MIX = 256
SEQ_PER_BLK = 16
ROWS_S = SEQ_PER_BLK * DEC_SEQ
TM_OUT = 256
ROUTE_DELAY = 1
TM_ROW = TM_OUT
ROW_TILES = 4
MOE_BLK = 256
N_BLK = T_ALL // MOE_BLK + MOE_GROUPS
N_SORTED = N_BLK * MOE_BLK
VMEM_LIMIT = 56 * 1024 * 1024


def _sigmoid(x):
    return 0.5 * jnp.tanh(0.5 * x) + 0.5


def _gelu_tanh(x):
    c = 0.7978845608028654
    half = 0.5 * x
    return half + half * jnp.tanh(x * (c + (c * 0.044715) * (x * x)))


def _log_sigmoid(x):
    log2_e = 1.4426950408889634
    ln_2 = 0.6931471805599453
    return jnp.minimum(x, 0.0) - ln_2 * jnp.log2(1.0 + jnp.exp2(-log2_e * jnp.abs(x)))


def _rms(x, g):
    ms = jnp.mean(x * x, axis=-1, keepdims=True)
    return x * lax.rsqrt(ms + EPS) * g


def _split_bf16(x):
    hi = x.astype(BF16)
    lo = (x - hi.astype(F32)).astype(BF16)
    return hi, lo


def _dot(a, b):
    return jnp.dot(a, b, preferred_element_type=F32)


def _dot_nt(a, b):
    return lax.dot_general(a, b, (((1,), (1,)), ((), ())), preferred_element_type=F32)


def _layer_spec(arr, layer):
    nd = arr.ndim - 1
    return pl.BlockSpec((None,) + arr.shape[1:], lambda *g: (layer,) + (0,) * nd)


def _two_part_specs(tile, xa, xb):
    n_a = xa.shape[0] // tile
    n_b = xb.shape[0] // tile
    return n_a, [
        pl.BlockSpec((tile, xa.shape[1]), lambda i, *_: (jnp.minimum(i, n_a - 1), 0)),
        pl.BlockSpec((tile, xb.shape[1]), lambda i, *_: (jnp.clip(i - n_a, 0, n_b - 1), 0)),
    ]


def _norm_and_lowrank(x, g_ref, wa_ref, xn_ref, a_ref):
    xn = _rms(x, g_ref[...]).astype(BF16)
    xn_ref[...] = xn
    a_ref[...] = _dot_nt(xn, wa_ref[...]).astype(BF16)


def _prenorm_kernel(xa_ref, xb_ref, g_ref, wa_ref, xn_ref, a_ref, *, n_a):
    x = jnp.where(pl.program_id(0) < n_a, xa_ref[...], xb_ref[...])
    _norm_and_lowrank(x, g_ref, wa_ref, xn_ref, a_ref)


def _prenorm(xa, xb, g, wa, layer):
    m = T_ALL
    n_a, x_specs = _two_part_specs(TM_PRE, xa, xb)
    return pl.pallas_call(
        functools.partial(_prenorm_kernel, n_a=n_a),
        grid=(m // TM_PRE,),
        in_specs=x_specs + [_layer_spec(g, layer), _layer_spec(wa, layer)],
        out_specs=[pl.BlockSpec((TM_PRE, D_MODEL), lambda i: (i, 0)),
                   pl.BlockSpec((TM_PRE, LANES), lambda i: (i, 0))],
        out_shape=[jax.ShapeDtypeStruct((m, D_MODEL), BF16),
                   jax.ShapeDtypeStruct((m, LANES), BF16)],
        compiler_params=pltpu.CompilerParams(
            dimension_semantics=("arbitrary",), vmem_limit_bytes=VMEM_LIMIT),
        name="prenorm",
    )(xa, xb, g, wa)


EXPERT_SHAPES = ((D_MODEL, D_EXPERT), (D_MODEL, D_EXPERT), (D_EXPERT, D_MODEL))


def _inproj_kernel(xn_ref, w_hbm, *refs, layer):
    n = len(EXPERT_SHAPES)
    z_ref, w_buf, sem = refs[n], refs[-2], refs[-1]
    j, i = pl.program_id(0), pl.program_id(1)

    def tile_copies(tile):
        return [pltpu.make_async_copy(w_hbm.at[layer, pl.ds(src, rows)],
                                      w_buf.at[tile, pl.ds(at, rows)], sem.at[tile])
                for src, at, rows in _w_in_copies()[tile]]

    @pl.when((j == 0) & (i == 0))
    def _():
        for tile in range(w_buf.shape[0]):
            for cp in tile_copies(tile):
                cp.start()

    for tile in range(w_buf.shape[0]):
        @pl.when((j == tile) & (i == 0))
        def _():
            for cp in tile_copies(tile):
                cp.wait()

    for src, dst in zip(refs[:n], refs[n + 1:]):
        dst[...] = src[...].astype(BF16)
    z_ref[...] = _dot_nt(xn_ref[...], w_buf[j]).astype(BF16)


def _inproj(xn, w, experts, layer):
    m = xn.shape[0]
    n_j, n_i = Z_COLS // TN_IN, m // TM_IN
    assert n_j * n_i >= N_EXPERTS
    expert = lambda j, i: jnp.minimum(j * n_i + i, N_EXPERTS - 1)
    return pl.pallas_call(
        functools.partial(_inproj_kernel, layer=layer),
        grid=(n_j, n_i),
        in_specs=[
            pl.BlockSpec((TM_IN, D_MODEL), lambda j, i: (i, 0)),
            pl.BlockSpec(memory_space=pl.ANY),
        ] + [pl.BlockSpec((None, None) + s, lambda j, i: (layer, expert(j, i), 0, 0))
             for s in EXPERT_SHAPES],
        out_specs=[pl.BlockSpec((TM_IN, TN_IN), lambda j, i: (i, j))]
        + [pl.BlockSpec((None,) + s, lambda j, i: (expert(j, i), 0, 0)) for s in EXPERT_SHAPES],
        out_shape=[jax.ShapeDtypeStruct((m, Z_COLS), BF16)]
        + [jax.ShapeDtypeStruct((N_EXPERTS,) + s, BF16) for s in EXPERT_SHAPES],
        scratch_shapes=[pltpu.VMEM((n_j, TN_IN, D_MODEL), BF16), pltpu.SemaphoreType.DMA((n_j,))],
        compiler_params=pltpu.CompilerParams(
            dimension_semantics=("arbitrary", "arbitrary"), vmem_limit_bytes=VMEM_LIMIT),
        name="inproj",
    )(xn, w, *experts)


def _gla_log_decay(alr_ref, a2_ref, ab_ref):
    la = _log_sigmoid(_dot(alr_ref[...], a2_ref[...]) + ab_ref[...]) * (1.0 / GLA_TAU)
    return _split_bf16(la)


def _masked_sum(mask, la_hi, la_lo):
    m = jnp.where(mask, 1.0, 0.0).astype(BF16)
    return _dot(m, la_hi) + _dot(m, la_lo)


def _gla_decay_terms(alr_ref, a2_ref, ab_ref, tril_mask, same_mask):
    la_hi, la_lo = _gla_log_decay(alr_ref, a2_ref, ab_ref)
    return (la_hi, la_lo, _masked_sum(tril_mask, la_hi, la_lo),
            _masked_sum(same_mask, la_hi, la_lo))


def _gla_out_gate(o, g_ref, r):
    return _rms(o, g_ref[...]) * (r * _sigmoid(r))


def _chunk_mlp_group(g, bgu_ref, vv_ref, cmg_ref, ws_ref, cmb_ref, n_chunks):
    sl = slice(g * CM_GCH, (g + 1) * CM_GCH)
    ug = _gelu_tanh(bgu_ref[:, CONV_CH + g * CM_GCH:CONV_CH + (g + 1) * CM_GCH])
    vg = _rms(_gelu_tanh(vv_ref[:, sl]).astype(F32), cmg_ref[:, sl])
    vgb = vg.astype(BF16)
    rows = []
    for j in range(n_chunks):
        rs = slice(j * CM_CHUNK, (j + 1) * CM_CHUNK)
        rows.append(_dot(ws_ref[g], vgb[rs]) + cmb_ref[:, sl])
    s = rows[0] if n_chunks == 1 else jnp.concatenate(rows, axis=0)
    return ug * s, vg


def _chunk_mlp(bgu_ref, vv_ref, cmg_ref, ws_ref, cmb_ref, n_chunks):
    parts = [_chunk_mlp_group(g, bgu_ref, vv_ref, cmg_ref, ws_ref, cmb_ref, n_chunks)
             for g in range(CM_GROUPS)]
    return (jnp.concatenate([p[0] for p in parts], axis=1),
            jnp.concatenate([p[1] for p in parts], axis=1))


N_Z_VIEWS = 7
N_MIX_W = 7
SEQ_PER_STEP = 4
SEQ_STAGGER = 3


def _mix_prompt_kernel(*refs):
    n_z = N_Z_VIEWS * SEQ_PER_STEP
    weights = refs[n_z:n_z + N_MIX_W]
    acts_ref, nconv_ref, ngla_ref, st_ref, carry_ref = refs[n_z + N_MIX_W:]

    @pl.when(pl.program_id(1) == 0)
    def _():
        st_ref[...] = jnp.zeros_like(st_ref)
        carry_ref[...] = jnp.zeros_like(carry_ref)

    rr = lax.broadcasted_iota(jnp.int32, (TC_MIX, TC_MIX), 0)
    cc = lax.broadcasted_iota(jnp.int32, (TC_MIX, TC_MIX), 1)
    tril = ((rr >> 6) == (cc >> 6)) & (cc <= rr)
    masks = (tril, jnp.where(tril, 1.0, 0.0).astype(BF16),
             jnp.where(rr - cc == 1, 1.0, 0.0).astype(BF16),
             jnp.where(rr - cc == 2, 1.0, 0.0).astype(BF16))

    _interleave(*[
        _delayed(_mix_prompt_seq(*refs[N_Z_VIEWS * s:N_Z_VIEWS * (s + 1)], *weights,
                                acts_ref.at[s], nconv_ref.at[s], st_ref.at[s], carry_ref.at[s],
                                masks),
                s * SEQ_STAGGER)
        for s in range(SEQ_PER_STEP)])

    @pl.when(pl.program_id(1) == pl.num_programs(1) - 1)
    def _():
        for s in range(SEQ_PER_STEP):
            for hd in range(GLA_HEADS):
                ngla_ref[s, 0, hd] = st_ref[s, hd].T


def _mix_prompt_seq(v_ref, r_ref, qk_ref, hcg_ref, bgu_ref, vv_ref, alr_ref,
                    a2_ref, ab_ref, cw_ref, gng_ref, cmg_ref, ws_ref, cmb_ref,
                    acts_ref, nconv_ref, st_ref, carry_ref, masks):
    tc = TC_MIX

    h = hcg_ref[:, :CONV_CH].astype(F32)
    cg = hcg_ref[:, CONV_CH:].astype(F32)
    bg = bgu_ref[:, :CONV_CH].astype(F32)
    cin = cg * h
    tril, tril_01, shift_1, shift_2 = masks
    cin_b = cin.astype(BF16)
    x1 = _dot(shift_1, cin_b)
    x2 = _dot(shift_2, cin_b)
    conv = x2 * cw_ref[0:1, :] + x1 * cw_ref[1:2, :] + cin * cw_ref[2:3, :]
    c0 = carry_ref[0:1, :]
    c1 = carry_ref[1:2, :]
    row8 = lax.broadcasted_iota(jnp.int32, (8, 1), 0)
    head = jnp.where(row8 == 0, c0 * cw_ref[0:1, :] + c1 * cw_ref[1:2, :],
                     jnp.where(row8 == 1, c1 * cw_ref[0:1, :], 0.0))
    conv = jnp.concatenate([conv[0:8] + head, conv[8:]], axis=0)
    acts_ref[:, 0:CONV_CH] = (bg * conv).astype(BF16)
    carry_ref[0:2, :] = cin[tc - 2:tc, :]
    nconv_ref[0] = cin[tc - 2:tc, :]
    yield

    la_hi, la_lo = _gla_log_decay(alr_ref, a2_ref, ab_ref)
    b = _dot(tril_01, la_hi) + _dot(tril_01, la_lo)
    yield
    n_chunks = tc // GLA_CHUNK
    b_last = [b[(c + 1) * GLA_CHUNK - 1:(c + 1) * GLA_CHUNK, :] for c in range(n_chunks)]
    bl = jnp.concatenate([jnp.broadcast_to(r_, (GLA_CHUNK, GLA_QK)) for r_ in b_last], axis=0)
    q = qk_ref[:, :GLA_QK].astype(F32) * (GLA_DK ** -0.5)
    k = qk_ref[:, GLA_QK:].astype(F32)
    q_t = (q * jnp.exp(b)).astype(BF16)
    k_t = (k * jnp.exp(-b)).astype(BF16)
    k_end = (k * jnp.exp(bl - b)).astype(BF16)
    yield
    states = [st_ref[hd] for hd in range(GLA_HEADS)]
    k_cols = [slice(hd * GLA_DK, (hd + 1) * GLA_DK) for hd in range(GLA_HEADS)]
    v_cols = [slice(hd * GLA_DV, (hd + 1) * GLA_DV) for hd in range(GLA_HEADS)]
    o_intra = []
    for hd in range(GLA_HEADS):
        att = jnp.where(tril, _dot_nt(q_t[:, k_cols[hd]], k_t[:, k_cols[hd]]), 0.0).astype(BF16)
        o_intra.append(_dot(att, v_ref[:, v_cols[hd]]))
    yield

    o_rows = [[] for _ in range(GLA_HEADS)]
    assert n_chunks == CM_GROUPS
    for c in range(n_chunks):
        rs = slice(c * GLA_CHUNK, (c + 1) * GLA_CHUNK)
        for hd in range(GLA_HEADS):
            ks = k_cols[hd]
            st = states[hd]
            o_rows[hd].append(o_intra[hd][rs] + _dot_nt(q_t[rs, ks], st.astype(BF16)))
            upd = lax.dot_general(v_ref[rs, v_cols[hd]], k_end[rs, ks], (((0,), (0,)), ((), ())),
                                  preferred_element_type=F32)
            states[hd] = jnp.exp(b_last[c][:, ks]) * st + upd
        us, _ = _chunk_mlp_group(c, bgu_ref, vv_ref, cmg_ref, ws_ref, cmb_ref, tc // CM_CHUNK)
        acts_ref[:, CONV_CH + GLA_V + c * CM_GCH:CONV_CH + GLA_V + (c + 1) * CM_GCH] = (
            us.astype(BF16))
        yield

    for hd in range(GLA_HEADS):
        o = jnp.concatenate(o_rows[hd], axis=0)
        r = r_ref[:, v_cols[hd]]
        acts_ref[:, CONV_CH + hd * GLA_DV:CONV_CH + (hd + 1) * GLA_DV] = (
            _gla_out_gate(o, gng_ref, r).astype(BF16))
        st_ref[hd] = states[hd]
        if hd % 2 == 1:
            yield


def _z_specs(rows, row_map):
    def spec(width, col):
        blk = col // width
        return pl.BlockSpec((rows, width), lambda *g: (row_map(*g), blk))
    return [spec(1024, COL_V), spec(1024, COL_R), spec(1024, COL_Q), spec(1024, COL_H),
            spec(1024, COL_BG), spec(512, COL_VV)]


def _mix_prompt(z, alr, a2, ab, cw, gng, cmg, ws, cmb, layer):
    nt = SEQ // TC_MIX
    nb = BATCH // SEQ_PER_STEP
    small = (a2, ab, cw, gng, cmg, ws, cmb)
    assert len(small) == N_MIX_W
    in_specs, args = [], []
    for s in range(SEQ_PER_STEP):
        row_map = lambda b, c, s=s: (b + s * nb) * nt + c
        in_specs += _z_specs(TC_MIX, row_map) + [
            pl.BlockSpec((TC_MIX, LANES), lambda b, c, row_map=row_map: (row_map(b, c), 0))]
        args += [z] * (N_Z_VIEWS - 1) + [alr]
    in_specs += [_layer_spec(a, layer) for a in small]
    acts, nconv, ngla = pl.pallas_call(
        _mix_prompt_kernel,
        grid=(nb, nt),
        in_specs=in_specs,
        out_specs=[
            pl.BlockSpec((SEQ_PER_STEP, TC_MIX, ACT_COLS), lambda b, c: (0, b * nt + c, 0)),
            pl.BlockSpec((SEQ_PER_STEP, 1, CONV_K - 1, CONV_CH), lambda b, c: (0, b, 0, 0)),
            pl.BlockSpec((SEQ_PER_STEP, 1, GLA_HEADS, GLA_DK, GLA_DV),
                         lambda b, c: (0, b, 0, 0, 0)),
        ],
        out_shape=[
            jax.ShapeDtypeStruct((SEQ_PER_STEP, T_PROMPT // SEQ_PER_STEP, ACT_COLS), BF16),
            jax.ShapeDtypeStruct((SEQ_PER_STEP, nb, CONV_K - 1, CONV_CH), F32),
            jax.ShapeDtypeStruct((SEQ_PER_STEP, nb, GLA_HEADS, GLA_DK, GLA_DV), F32),
        ],
        scratch_shapes=[pltpu.VMEM((SEQ_PER_STEP, GLA_HEADS, GLA_DV, GLA_DK), F32),
                        pltpu.VMEM((SEQ_PER_STEP, 8, CONV_CH), F32)],
        compiler_params=pltpu.CompilerParams(
            dimension_semantics=("arbitrary", "arbitrary"), vmem_limit_bytes=VMEM_LIMIT),
        name="mix_prompt",
    )(*args, *small)
    return (acts.reshape(T_PROMPT, ACT_COLS), nconv.reshape(BATCH, CONV_K - 1, CONV_CH),
            ngla.reshape(BATCH, GLA_HEADS, GLA_DK, GLA_DV))


def _mix_sample_body(v_ref, r_ref, qk_ref, hcg_ref, bgu_ref, vv_ref, alr_ref,
                       a2_ref, ab_ref, cw_ref, gng_ref, cmg_ref, ws_ref, cmb_ref,
                       p1_ref, p2_ref, s0_ref,
                       acts_ref, cin_ref, ns_ref, vrow_ref):
    n = ROWS_S

    pos = lax.broadcasted_iota(jnp.int32, (n, 1), 0) & (DEC_SEQ - 1)
    h = hcg_ref[:, :CONV_CH].astype(F32)
    cg = hcg_ref[:, CONV_CH:].astype(F32)
    bg = bgu_ref[:, :CONV_CH].astype(F32)
    cin = cg * h
    x1 = jnp.where(pos >= 1, pltpu.roll(cin, 1, 0), p1_ref[...])
    x2 = jnp.where(pos >= 2, pltpu.roll(cin, 2, 0), p2_ref[...])
    conv = x2 * cw_ref[0:1, :] + x1 * cw_ref[1:2, :] + cin * cw_ref[2:3, :]
    acts_ref[:, 0:CONV_CH] = (bg * conv).astype(BF16)
    cin_ref[...] = cin

    rr = lax.broadcasted_iota(jnp.int32, (n, n), 0)
    cc = lax.broadcasted_iota(jnp.int32, (n, n), 1)
    same = (rr >> 3) == (cc >> 3)
    tril = same & (cc <= rr)
    la_hi, la_lo, b, bl = _gla_decay_terms(alr_ref, a2_ref, ab_ref, tril, same)
    q = qk_ref[:, :GLA_QK].astype(F32) * (GLA_DK ** -0.5)
    k = qk_ref[:, GLA_QK:].astype(F32)
    q_t = (q * jnp.exp(b)).astype(BF16)
    k_t = (k * jnp.exp(-b)).astype(BF16)
    k_end = k * jnp.exp(bl - b)
    la_hi = la_hi.astype(F32)
    la_lo = la_lo.astype(F32)
    row_seq = lax.broadcasted_iota(jnp.int32, (n, GLA_DK), 0) >> 3
    seq3 = lax.broadcasted_iota(jnp.int32, (SEQ_PER_BLK, GLA_DK, n), 0)
    lane_seq3 = lax.broadcasted_iota(jnp.int32, (SEQ_PER_BLK, GLA_DK, n), 2) >> 3
    mask3 = seq3 == lane_seq3
    ones = jnp.ones((n, GLA_DV), BF16)
    big = SEQ_PER_BLK * GLA_DK

    def per_seq(x_tr):
        x3 = jnp.where(mask3, x_tr[None, :, :], 0.0)
        return x3.reshape(big, n).astype(BF16)

    for hd in range(GLA_HEADS):
        ks = slice(hd * GLA_DK, (hd + 1) * GLA_DK)
        vs = slice(hd * GLA_DV, (hd + 1) * GLA_DV)
        qh = q_t[:, ks]
        vh = v_ref[:, vs]
        att = jnp.where(tril, _dot_nt(qh, k_t[:, ks]), 0.0).astype(BF16)
        o_intra = _dot(att, vh)
        s_old = s0_ref[:, hd].reshape(big, GLA_DV)
        zero = jnp.zeros_like(qh)
        q_big = jnp.concatenate(
            [jnp.where(row_seq == j, qh, zero) for j in range(SEQ_PER_BLK)], axis=1)
        o = o_intra + _dot(q_big, s_old.astype(BF16))
        dlog = _dot(per_seq(la_hi[:, ks].T), ones) + _dot(per_seq(la_lo[:, ks].T), ones)
        upd = _dot(per_seq(k_end[:, ks].T), vh)
        s_new = jnp.exp(dlog) * s_old + upd
        ns_ref[:, hd] = s_new.reshape(SEQ_PER_BLK, GLA_DK, GLA_DV)
        r = r_ref[:, vs]
        acts_ref[:, CONV_CH + hd * GLA_DV:CONV_CH + (hd + 1) * GLA_DV] = (
            _gla_out_gate(o, gng_ref, r).astype(BF16))

    us, vg = _chunk_mlp(bgu_ref, vv_ref, cmg_ref, ws_ref, cmb_ref, 1)
    acts_ref[:, CONV_CH + GLA_V:] = us.astype(BF16)
    vrow_ref[...] = vg


N_MIX_S_IN = 17


def _mix_sample_kernel(*refs, layer):
    if layer == 0:
        @pl.when(pl.program_id(0) == 0)
        def _():
            _mix_sample_body(*refs)

        @pl.when(pl.program_id(0) > 0)
        def _():
            ns_ref = refs[N_MIX_S_IN + 2]
            ns_ref[...] = jnp.zeros_like(ns_ref)
    else:
        _mix_sample_body(*refs[:N_MIX_S_IN], *refs[N_MIX_S_IN + 1:])


def _mix_sample(z, alr, a2, ab, cw, gng, cmg, ws, cmb, p1, p2, s0, ns_all, layer):
    row0 = T_PROMPT // ROWS_S
    n_i = DEC_BATCH // SEQ_PER_BLK
    n_pass = DEPTH if layer == 0 else 1
    blk = lambda p, i: jnp.where(p == 0, i, n_i - 1)
    row_map = lambda p, i: row0 + blk(p, i)
    slot = lambda p, i: (layer + p, i, 0, 0, 0)
    small = (a2, ab, cw, gng, cmg, ws, cmb)
    state_blk = (None, SEQ_PER_BLK, GLA_HEADS, GLA_DK, GLA_DV)
    in_specs = _z_specs(ROWS_S, row_map) + [
        pl.BlockSpec((ROWS_S, LANES), lambda p, i: (row_map(p, i), 0)),
    ] + [_layer_spec(a, layer) for a in small] + [
        pl.BlockSpec((ROWS_S, CONV_CH), lambda p, i: (blk(p, i), 0)),
        pl.BlockSpec((ROWS_S, CONV_CH), lambda p, i: (blk(p, i), 0)),
        pl.BlockSpec(state_blk, lambda p, i: (layer, blk(p, i), 0, 0, 0)),
    ]
    args = (z, z, z, z, z, z, alr, *small, p1, p2, s0)
    assert len(args) == N_MIX_S_IN
    aliases = {}
    if layer > 0:
        in_specs.append(pl.BlockSpec(memory_space=pl.ANY))
        args += (ns_all,)
        aliases = {N_MIX_S_IN: 2}
    return pl.pallas_call(
        functools.partial(_mix_sample_kernel, layer=layer),
        grid=(n_pass, n_i),
        in_specs=in_specs,
        out_specs=[
            pl.BlockSpec((ROWS_S, ACT_COLS), lambda p, i: (blk(p, i), 0)),
            pl.BlockSpec((ROWS_S, CONV_CH), lambda p, i: (blk(p, i), 0)),
            pl.BlockSpec(state_blk, slot),
            pl.BlockSpec((ROWS_S, CM_CH), lambda p, i: (blk(p, i), 0)),
        ],
        out_shape=[
            jax.ShapeDtypeStruct((T_SAMPLE, ACT_COLS), BF16),
            jax.ShapeDtypeStruct((T_SAMPLE, CONV_CH), F32),
            jax.ShapeDtypeStruct((DEPTH, DEC_BATCH, GLA_HEADS, GLA_DK, GLA_DV), F32),
            jax.ShapeDtypeStruct((T_SAMPLE, CM_CH), F32),
        ],
        input_output_aliases=aliases,
        compiler_params=pltpu.CompilerParams(
            dimension_semantics=("arbitrary", "arbitrary"), vmem_limit_bytes=VMEM_LIMIT),
        name="mix_sample",
    )(*args)


def _delayed(gen, n_stages):
    for _ in range(n_stages):
        yield
    yield from gen


def _interleave(*stage_lists):
    pending = list(stage_lists)
    while pending:
        for gen in list(pending):
            if next(gen, StopIteration) is StopIteration:
                pending.remove(gen)


def _outproj_main(i, n_xa, actp_ref, acts_ref, gates_ref, xa_ref, xb_ref, pa_ref, pb_ref, pc_ref,
                  wo_ref, n2_ref, xo_ref, xn_ref, xnb_ref):
    acts = jnp.where(i < T_PROMPT // TM_OUT, actp_ref[...], acts_ref[...])
    x_in = jnp.where(i < n_xa, xa_ref[...], xb_ref[...])
    yb = _dot(acts[:, CONV_CH:CONV_CH + GLA_V], pb_ref[...])
    ga = _sigmoid(gates_ref[:, COL_GA:COL_GA + D_MODEL])
    ya = _dot(acts[:, :CONV_CH], pa_ref[...])
    gb = _sigmoid(gates_ref[:, COL_GB:COL_GB + D_MODEL])
    yield
    yc = _dot(acts[:, CONV_CH + GLA_V:], pc_ref[...])
    gc = _sigmoid(gates_ref[:, COL_GC:COL_GC + D_MODEL])
    mix = ga * ya + gb * yb
    yield
    mix = mix + gc * yc
    x = x_in + _dot(mix.astype(BF16), wo_ref[...])
    xo_ref[...] = x
    yield
    xn = _rms(x, n2_ref[...])
    xn_ref[...] = xn
    xnb = xn.astype(BF16)
    yield
    xnb_ref[...] = xnb


def _outproj_route(live, xnb_ref, wr_ref, rb_ref, route_ref, seg_ref, cnt_ref):
    tm = TM_OUT
    logits = _dot_nt(wr_ref[...], xnb_ref[...]) + rb_ref[...]
    yield
    n_sub = EXPERTS_PER_GROUP
    sub = lax.broadcasted_iota(jnp.int32, (n_sub, tm), 0).astype(F32)
    neg = jnp.float32(-jnp.inf)
    lg = logits[0:MOE_GROUPS]
    gmax = jnp.max(lg, axis=0, keepdims=True)
    grp = jnp.min(jnp.where(lg == gmax, sub, 1e9), axis=0, keepdims=True)
    p_grp = 1.0 / jnp.sum(jnp.exp(lg - gmax), axis=0, keepdims=True)
    yield
    le = logits[MOE_GROUPS:MOE_GROUPS + n_sub]
    for g in range(1, MOE_GROUPS):
        le = jnp.where(grp == g, logits[MOE_GROUPS + g * n_sub:MOE_GROUPS + (g + 1) * n_sub], le)
    v1 = jnp.max(le, axis=0, keepdims=True)
    i1 = jnp.min(jnp.where(le == v1, sub, 1e9), axis=0, keepdims=True)
    le2 = jnp.where(sub == i1, neg, le)
    v2 = jnp.max(le2, axis=0, keepdims=True)
    i2 = jnp.min(jnp.where(le2 == v2, sub, 1e9), axis=0, keepdims=True)
    yield
    t = jnp.exp(v2 - v1)
    g1 = p_grp / (1.0 + t)
    g2 = p_grp * t / (1.0 + t)
    gate_t = jnp.where(sub == i1, g1, jnp.where(sub == i2, g2, 0.0))

    onehot_t = jnp.where(sub == grp, 1.0, 0.0)
    csum = jnp.where(live, jnp.sum(onehot_t, axis=1, keepdims=True), 0.0)
    rr = lax.broadcasted_iota(jnp.int32, (tm, tm), 0)
    cc = lax.broadcasted_iota(jnp.int32, (tm, tm), 1)
    earlier = jnp.where(rr < cc, 1.0, 0.0).astype(BF16)
    padded = jnp.concatenate([onehot_t, jnp.zeros_like(onehot_t)], axis=0).astype(BF16)
    same_before = _dot(padded, earlier)[0:n_sub]
    yield
    lower = jnp.sum(jnp.where(sub < grp, csum, 0.0), axis=0, keepdims=True)
    lrank = lower + jnp.sum(onehot_t * same_before, axis=0, keepdims=True)
    route_ref[0:n_sub, :] = gate_t
    route_ref[n_sub:, :] = jnp.broadcast_to(lrank, (n_sub, tm))
    carry = cnt_ref[...]
    lane = lax.broadcasted_iota(jnp.int32, (n_sub, LANES), 1)
    seg_ref[...] = jnp.where(lane == SEG_START, carry,
                             jnp.where(lane == SEG_LEN, csum, 0.0)).astype(jnp.int32)
    cnt_ref[...] = carry + csum


def _outproj_kernel(actp_ref, acts_ref, gates_ref, xa_ref, xb_ref, pa_ref, pb_ref, pc_ref, wo_ref,
                    n2_ref, wr_ref, rb_ref, xo_ref, xn_ref, route_ref, seg_ref, cnt_ref, xnb_ref,
                    *, n_xa):
    i = pl.program_id(0)

    @pl.when(i == 0)
    def _():
        cnt_ref[...] = jnp.zeros_like(cnt_ref)
        xnb_ref[...] = jnp.zeros_like(xnb_ref)

    tile = jnp.minimum(i, pl.num_programs(0) - 2)
    _interleave(
        _delayed(_outproj_route(i >= 1, xnb_ref, wr_ref, rb_ref, route_ref, seg_ref, cnt_ref),
                 ROUTE_DELAY),
        _outproj_main(tile, n_xa, actp_ref, acts_ref, gates_ref, xa_ref, xb_ref, pa_ref, pb_ref,
                      pc_ref, wo_ref, n2_ref, xo_ref, xn_ref, xnb_ref))


def _lane_prefix(v):
    rr = lax.broadcasted_iota(jnp.int32, (LANES, LANES), 0)
    cc = lax.broadcasted_iota(jnp.int32, (LANES, LANES), 1)
    earlier = jnp.where(rr < cc, 1.0, 0.0).astype(BF16)
    return _dot(jnp.broadcast_to(v, (8, LANES)).astype(BF16), earlier)[0:1]


def _outproj(acts_p, acts_s, z, xa, xb, pa, pb, pc, wo, n2, wr, rb, layer):
    m = T_ALL
    nt = m // TM_OUT
    row = lambda i: (jnp.minimum(i, nt - 1), 0)
    routed = lambda i: (jnp.maximum(i - 1, 0), 0, 0)
    _, act_specs = _two_part_specs(TM_OUT, acts_p, acts_s)
    n_xa, x_specs = _two_part_specs(TM_OUT, xa, xb)
    weights = (pa, pb, pc, wo, n2, wr, rb)
    return pl.pallas_call(
        functools.partial(_outproj_kernel, n_xa=n_xa),
        grid=(nt + 1,),
        in_specs=act_specs + [pl.BlockSpec((TM_OUT, 3 * D_MODEL), row)] + x_specs
        + [_layer_spec(w, layer) for w in weights],
        out_specs=[
            pl.BlockSpec((TM_OUT, D_MODEL), row),
            pl.BlockSpec((TM_OUT, D_MODEL), row),
            pl.BlockSpec((None, ROUTE_ROWS, TM_OUT), routed),
            pl.BlockSpec((None, 8, LANES), routed),
            pl.BlockSpec((8, LANES), lambda i: (0, 0)),
        ],
        scratch_shapes=[pltpu.VMEM((TM_OUT, D_MODEL), BF16)],
        out_shape=[
            jax.ShapeDtypeStruct((m, D_MODEL), F32),
            jax.ShapeDtypeStruct((m, D_MODEL), F32),
            jax.ShapeDtypeStruct((m // TM_OUT, ROUTE_ROWS, TM_OUT), F32),
            jax.ShapeDtypeStruct((m // TM_OUT, 8, LANES), jnp.int32),
            jax.ShapeDtypeStruct((8, LANES), F32),
        ],
        compiler_params=pltpu.CompilerParams(
            dimension_semantics=("arbitrary",), vmem_limit_bytes=VMEM_LIMIT),
        name="outproj",
    )(acts_p, acts_s, z, xa, xb, *weights)


PLAN_GROUP, PLAN_USED, PLAN_END, PLAN_BLOCKS, PLAN_BASE = 0, 1, 2, 3, 4


def _plan_kernel(cnt_ref, plan_ref):
    sub_i = lax.broadcasted_iota(jnp.int32, (8, LANES), 0)
    lane8_i = lax.broadcasted_iota(jnp.int32, (8, LANES), 1)
    cnt = jnp.sum(jnp.where(sub_i == lane8_i, cnt_ref[...], 0.0), axis=0, keepdims=True)
    blocks = jnp.floor((cnt + (MOE_BLK - 1)) * (1.0 / MOE_BLK))
    start = _lane_prefix(blocks)
    end = start + blocks
    lane_i = lax.broadcasted_iota(jnp.int32, (1, LANES), 1)
    lane = lane_i.astype(F32)
    grp_of_blk = jnp.zeros((1, LANES), F32)
    for g in range(MOE_GROUPS):
        end_g = jnp.sum(jnp.where(lane_i == g, end, 0.0), axis=-1, keepdims=True)
        grp_of_blk += jnp.where(lane >= end_g, 1.0, 0.0)
    grp_of_blk = jnp.minimum(grp_of_blk, MOE_GROUPS - 1)
    n_used = jnp.sum(blocks, axis=-1, keepdims=True)
    row = lax.broadcasted_iota(jnp.int32, (8, LANES), 0)
    plan_ref[...] = jnp.where(
        row == PLAN_GROUP, grp_of_blk,
        jnp.where(row == PLAN_USED, n_used,
                  jnp.where(row == PLAN_END, end,
                            jnp.where(row == PLAN_BLOCKS, blocks,
                                      jnp.where(row == PLAN_BASE, start * MOE_BLK, 0.0))))
    ).astype(jnp.int32)


def _plan(cnt):
    return pl.pallas_call(
        _plan_kernel,
        out_shape=jax.ShapeDtypeStruct((8, LANES), jnp.int32),
        name="plan",
    )(cnt)


SEG_START, SEG_LEN = 0, 1


def _segment_copies(seg_ref, base_ref, make_copy):
    local = 0
    for g in range(MOE_GROUPS):
        n = seg_ref[g, SEG_LEN]
        first = base_ref[g] + seg_ref[g, SEG_START]
        k = TM_ROW
        while k >= 1:
            done = n & ~(2 * k - 1)
            @pl.when((n & k) != 0)
            def _():
                make_copy(local + done, first + done, k).start()
            k //= 2
        local = local + n


def _perm_matrix(route_ref):
    row = lax.broadcasted_iota(jnp.int32, (TM_ROW, TM_ROW), 0).astype(F32)
    return jnp.where(row == route_ref[ROUTE_LRANK:ROUTE_LRANK + 1, :], 1.0, 0.0).astype(BF16)


SUB = 8
SUB_X = D_MODEL // 2 // LANES
SUB_GATE = SUB_X
HI_MASK = -65536


def _rows(first_row, n_rows):
    return pl.ds(pl.multiple_of(first_row * SUB, SUB), n_rows * SUB)


def _sublane(s, n_rows):
    return pl.ds(s, n_rows, stride=SUB)


def _scatter_kernel(nu_ref, end_ref, nb_ref, base_ref, seg_ref, route_ref, xn_ref, xs_ref,
                    zbuf, sbuf, sems):
    i = pl.program_id(0)
    n = pl.num_programs(0)
    sem = sems.at[0]

    def zero_block(b):
        return pltpu.make_async_copy(zbuf, xs_ref.at[_rows(b * MOE_BLK, MOE_BLK)], sem)

    def each_unfilled_block(fn):
        for g in range(MOE_GROUPS):
            @pl.when(nb_ref[g] > 0)
            def _():
                fn(zero_block(end_ref[g] - 1))
        for b in range(T_ALL // MOE_BLK, N_BLK):
            @pl.when(b >= nu_ref[0])
            def _():
                fn(zero_block(b))

    @pl.when(i == 0)
    def _():
        zbuf[...] = jnp.zeros_like(zbuf)
        sbuf[...] = jnp.zeros_like(sbuf)
        each_unfilled_block(lambda c: c.start())
        each_unfilled_block(lambda c: c.wait())

    def tile_done(s):
        pltpu.make_async_copy(sbuf.at[s], xs_ref.at[_rows(0, TM_ROW)], sems.at[s]).wait()

    @pl.when(i >= 2)
    def _():
        for t in range(ROW_TILES):
            tile_done((i % 2) * ROW_TILES + t)

    def sort_tile(t):
        slot = (i % 2) * ROW_TILES + t
        route = route_ref.at[t]
        perm = _perm_matrix(route)
        xn = xn_ref[t * TM_ROW:(t + 1) * TM_ROW, :]
        xs = lax.bitcast_convert_type(_dot(perm, xn.astype(BF16)), jnp.int32)
        yield
        half = D_MODEL // 2
        for s in range(SUB_X):
            hi = xs[:, s * LANES:(s + 1) * LANES] & HI_MASK
            lo = lax.shift_right_logical(xs[:, half + s * LANES:half + (s + 1) * LANES], 16)
            sbuf[slot, _sublane(s, TM_ROW), :] = hi | lo
        yield
        gate_t = route[0:EXPERTS_PER_GROUP, :]
        gate_t = jnp.concatenate(
            [gate_t, jnp.zeros((LANES - EXPERTS_PER_GROUP, TM_ROW), F32)], axis=0)
        g1 = gate_t.astype(BF16)
        r1 = gate_t - g1.astype(F32)
        g2 = r1.astype(BF16)
        g3 = (r1 - g2.astype(F32)).astype(BF16)
        sbuf[slot, _sublane(SUB_GATE, TM_ROW), :] = lax.bitcast_convert_type(
            _dot_nt(perm, g1) + _dot_nt(perm, g2) + _dot_nt(perm, g3), jnp.int32)

    _interleave(*[_delayed(sort_tile(t), t) for t in range(ROW_TILES)])

    for t in range(ROW_TILES):
        def make_copy(src_row, dst_row, k, slot=(i % 2) * ROW_TILES + t):
            return pltpu.make_async_copy(sbuf.at[slot, _rows(src_row, k)],
                                         xs_ref.at[_rows(dst_row, k)], sems.at[slot])
        _segment_copies(seg_ref.at[t], base_ref, make_copy)

    @pl.when(i == n - 1)
    def _():
        for t in range(ROW_TILES):
            tile_done((i % 2) * ROW_TILES + t)

            @pl.when(n > 1)
            def _():
                tile_done((1 - i % 2) * ROW_TILES + t)


def _scatter(seg, route, xn, n_used, grp_end, grp_blocks, grp_base):
    m = xn.shape[0]
    step_rows = ROW_TILES * TM_ROW
    grid_spec = pltpu.PrefetchScalarGridSpec(
        num_scalar_prefetch=4,
        grid=(m // step_rows,),
        in_specs=[
            pl.BlockSpec((ROW_TILES, 8, LANES), lambda i, *_: (i, 0, 0), memory_space=pltpu.SMEM),
            pl.BlockSpec((ROW_TILES, ROUTE_ROWS, TM_ROW), lambda i, *_: (i, 0, 0)),
            pl.BlockSpec((step_rows, D_MODEL), lambda i, *_: (i, 0)),
        ],
        out_specs=pl.BlockSpec(memory_space=pl.ANY),
        scratch_shapes=[pltpu.VMEM((MOE_BLK * SUB, LANES), jnp.int32),
                        pltpu.VMEM((2 * ROW_TILES, TM_ROW * SUB, LANES), jnp.int32),
                        pltpu.SemaphoreType.DMA((2 * ROW_TILES,))],
    )
    return pl.pallas_call(
        _scatter_kernel,
        grid_spec=grid_spec,
        out_shape=jax.ShapeDtypeStruct((N_SORTED * SUB, LANES), jnp.int32),
        compiler_params=pltpu.CompilerParams(
            dimension_semantics=("arbitrary",), vmem_limit_bytes=VMEM_LIMIT),
        name="scatter",
    )(n_used, grp_end, grp_blocks, grp_base, seg, route, xn)


def _ffn_kernel(bg_ref, nu_ref, xs_ref, w1_ref, w3_ref, w2_ref, y_ref):
    del bg_ref
    b = pl.program_id(0)

    def block_inputs():
        packed = [xs_ref[_sublane(s, MOE_BLK), :] for s in range(SUB_X)]
        x = jnp.concatenate(
            [lax.bitcast_convert_type(u & HI_MASK, F32).astype(BF16) for u in packed]
            + [lax.bitcast_convert_type(lax.shift_left(u, 16), F32).astype(BF16) for u in packed],
            axis=1)
        return x, lax.bitcast_convert_type(xs_ref[_sublane(SUB_GATE, MOE_BLK), :], F32)

    def store(y):
        for s in range(SUB):
            y_ref[_sublane(s, MOE_BLK), :] = y[:, s * LANES:(s + 1) * LANES]

    @pl.when(b < nu_ref[0])
    def _():
        x, gates = block_inputs()
        n_e = EXPERTS_PER_GROUP
        h1 = [_dot(x, w1_ref[e]) for e in range(n_e)]
        h3 = [_dot(x, w3_ref[e]) for e in range(n_e)]
        hs = []
        for e in range(n_e):
            ge = gates[:, e:e + 1]
            hs.append(jnp.where(ge > 0.0, h1[e] * _sigmoid(h1[e]) * h3[e] * ge, 0.0).astype(BF16))
        hcat = jnp.concatenate(hs, axis=1)
        store(_dot(hcat, w2_ref[...].reshape(n_e * D_EXPERT, D_MODEL)))

    @pl.when(b >= nu_ref[0])
    def _():
        y_ref[...] = jnp.zeros_like(y_ref)


def _ffn(blk_group, n_used, xs, w1, w3, w2):
    grouped = (MOE_GROUPS, EXPERTS_PER_GROUP)
    w1, w3, w2 = (w.reshape(grouped + w.shape[1:]) for w in (w1, w3, w2))
    wmap = lambda b, bg, nu: (bg[b], 0, 0, 0)
    grid_spec = pltpu.PrefetchScalarGridSpec(
        num_scalar_prefetch=2,
        grid=(N_BLK,),
        in_specs=[
            pl.BlockSpec((MOE_BLK * SUB, LANES), lambda b, bg, nu: (b, 0)),
            pl.BlockSpec((None, EXPERTS_PER_GROUP, D_MODEL, D_EXPERT), wmap),
            pl.BlockSpec((None, EXPERTS_PER_GROUP, D_MODEL, D_EXPERT), wmap),
            pl.BlockSpec((None, EXPERTS_PER_GROUP, D_EXPERT, D_MODEL), wmap),
        ],
        out_specs=pl.BlockSpec((MOE_BLK * SUB, LANES), lambda b, bg, nu: (b, 0)),
    )
    return pl.pallas_call(
        _ffn_kernel,
        grid_spec=grid_spec,
        out_shape=jax.ShapeDtypeStruct((N_SORTED * SUB, LANES), F32),
        compiler_params=pltpu.CompilerParams(
            dimension_semantics=("arbitrary",), vmem_limit_bytes=VMEM_LIMIT),
        name="ffn",
    )(blk_group, n_used, xs, w1, w3, w2)


def _combine_kernel(base_ref, seg_ref, segn_ref, route_ref, ys_hbm, x_ref, g_ref, wa_ref, *refs,
                    final):
    outs, (buf, sem, stage) = refs[:-3], refs[-3:]
    i = pl.program_id(0)
    n = pl.num_programs(0)
    step_rows = ROW_TILES * TM_ROW

    def gather(seg, parity):
        for t in range(ROW_TILES):
            s = parity * ROW_TILES + t

            def make_copy(buf_row, ys_row, k, s=s):
                return pltpu.make_async_copy(ys_hbm.at[_rows(ys_row, k)],
                                             buf.at[s, _rows(buf_row, k)], sem.at[s])
            _segment_copies(seg.at[t], base_ref, make_copy)

    @pl.when(i == 0)
    def _():
        gather(seg_ref, 0)

    @pl.when(i + 1 < n)
    def _():
        gather(segn_ref, 1 - i % 2)

    for t in range(ROW_TILES):
        slot = (i % 2) * ROW_TILES + t
        pltpu.make_async_copy(ys_hbm.at[_rows(0, TM_ROW)], buf.at[slot], sem.at[slot]).wait()

    def tile_stages(t):
        slot = (i % 2) * ROW_TILES + t
        rows = slice(t * TM_ROW, (t + 1) * TM_ROW)
        tn = (((0,), (0,)), ((), ()))
        perm = _perm_matrix(route_ref.at[t])
        cols = []
        for s in range(SUB):
            hi, lo = _split_bf16(buf[slot, _sublane(s, TM_ROW), :])
            cols.append(lax.dot_general(perm, hi, tn, preferred_element_type=F32)
                        + lax.dot_general(perm, lo, tn, preferred_element_type=F32))
            if s % 4 == 3:
                yield
        y = x_ref[rows, :] + jnp.concatenate(cols, axis=1)
        if not final:
            outs[0][rows, :] = y
            yield
            _norm_and_lowrank(y, g_ref, wa_ref, outs[1].at[rows, :], outs[2].at[rows, :])
        else:
            yield
            stage[rows, :] = _rms(y, g_ref[...])

    _interleave(*[_delayed(tile_stages(t), t) for t in range(ROW_TILES)])

    if final:
        @pl.when(i < T_PROMPT // step_rows)
        def _():
            outs[0][...] = stage[...]

        @pl.when(i >= T_PROMPT // step_rows)
        def _():
            outs[1][...] = stage[...]


def _combine(seg, route, grp_base, ys, x, g, wa, final):
    m = x.shape[0]
    step_rows = ROW_TILES * TM_ROW
    nt = m // step_rows
    n_p = T_PROMPT // step_rows
    smem = functools.partial(pl.BlockSpec, (ROW_TILES, 8, LANES), memory_space=pltpu.SMEM)
    tile = (step_rows, D_MODEL)
    if final:
        out_specs = [pl.BlockSpec(tile, lambda i, *_: (jnp.minimum(i, n_p - 1), 0)),
                     pl.BlockSpec(tile, lambda i, *_: (jnp.maximum(i - n_p, 0), 0))]
        out_shape = [jax.ShapeDtypeStruct((T_PROMPT, D_MODEL), F32),
                     jax.ShapeDtypeStruct((T_SAMPLE, D_MODEL), F32)]
    else:
        out_specs = [pl.BlockSpec(tile, lambda i, *_: (i, 0)),
                     pl.BlockSpec(tile, lambda i, *_: (i, 0)),
                     pl.BlockSpec((step_rows, LANES), lambda i, *_: (i, 0))]
        out_shape = [jax.ShapeDtypeStruct((m, D_MODEL), F32),
                     jax.ShapeDtypeStruct((m, D_MODEL), BF16),
                     jax.ShapeDtypeStruct((m, LANES), BF16)]
    grid_spec = pltpu.PrefetchScalarGridSpec(
        num_scalar_prefetch=1,
        grid=(nt,),
        in_specs=[
            smem(lambda i, *_: (i, 0, 0)),
            smem(lambda i, *_: (jnp.minimum(i + 1, nt - 1), 0, 0)),
            pl.BlockSpec((ROW_TILES, ROUTE_ROWS, TM_ROW), lambda i, *_: (i, 0, 0)),
            pl.BlockSpec(memory_space=pl.ANY),
            pl.BlockSpec(tile, lambda i, *_: (i, 0)),
            pl.BlockSpec(g.shape, lambda i, *_: (0, 0)),
            pl.BlockSpec(wa.shape, lambda i, *_: (0, 0)),
        ],
        out_specs=out_specs,
        scratch_shapes=[pltpu.VMEM((2 * ROW_TILES, TM_ROW * SUB, LANES), F32),
                        pltpu.SemaphoreType.DMA((2 * ROW_TILES,)),
                        pltpu.VMEM(tile if final else (8, LANES), F32)],
    )
    return pl.pallas_call(
        functools.partial(_combine_kernel, final=final),
        grid_spec=grid_spec,
        out_shape=out_shape,
        compiler_params=pltpu.CompilerParams(
            dimension_semantics=("arbitrary",), vmem_limit_bytes=VMEM_LIMIT),
        name="combine",
    )(grp_base, seg, seg, route, ys, x, g, wa)


def _prep_weights(w_in, gla_a2, cm_ws, cm_b, router_group_w, router_group_b,
                  router_expert_w, router_expert_b):
    w_z = jnp.swapaxes(w_in, 1, 2).astype(BF16)
    a0 = _w_in_offsets()["alr"][0]
    w_alr = jnp.pad(w_z[:, a0:a0 + GLA_LOWRANK], ((0, 0), (0, LANES - GLA_LOWRANK), (0, 0)))
    a2 = jnp.pad(gla_a2, ((0, 0), (0, LANES - GLA_LOWRANK), (0, 0))).astype(BF16)
    ws_p = jnp.tril(cm_ws).astype(BF16)
    small = jnp.tril(cm_ws[:, :, :DEC_SEQ, :DEC_SEQ])
    eye = jnp.eye(SEQ_PER_BLK, dtype=F32)
    ws_s = jnp.einsum("ij,lgab->lgiajb", eye, small).reshape(
        DEPTH, CM_GROUPS, ROWS_S, ROWS_S).astype(BF16)
    cmb_p = jnp.broadcast_to(jnp.transpose(cm_b, (0, 2, 1))[:, :, :, None],
                             (DEPTH, CM_CHUNK, CM_GROUPS, CM_GCH)).reshape(DEPTH, CM_CHUNK, CM_CH)
    cmb_s = jnp.tile(cmb_p[:, :DEC_SEQ], (1, SEQ_PER_BLK, 1))
    pad = LANES - MOE_GROUPS - N_EXPERTS
    w_r = jnp.pad(jnp.swapaxes(jnp.concatenate([router_group_w, router_expert_w], axis=-1), 1, 2),
                  ((0, 0), (0, pad), (0, 0)))
    r_b = jnp.pad(jnp.concatenate([router_group_b, router_expert_b], axis=-1),
                  ((0, 0), (0, pad)))[:, :, None]
    return w_z, w_alr, a2, ws_p, ws_s, cmb_p, cmb_s, w_r.astype(BF16), r_b


def kernel(x_prompt, x_sample, state_conv, state_gla, norm1_g, w_in, conv_w, gla_a2, gla_a_b,
           gla_norm_g, cm_norm_g, cm_ws, cm_b, proj_a, proj_b, proj_c, w_out, norm2_g,
           router_group_w, router_group_b, router_expert_w, router_expert_b,
           exp_w1, exp_w3, exp_w2, final_norm_g):
    (w_z, w_alr, a2, ws_p, ws_s, cmb_p, cmb_s, w_r, r_b) = _prep_weights(
        w_in, gla_a2, cm_ws, cm_b, router_group_w, router_group_b, router_expert_w,
        router_expert_b)
    pa, pb, pc, wo = (w.astype(BF16) for w in (proj_a, proj_b, proj_c, w_out))
    n1 = norm1_g[:, None, :]
    n2 = norm2_g[:, None, :]
    ab = gla_a_b[:, None, :]
    gng = gla_norm_g[:, None, :]
    cmg = cm_norm_g.reshape(DEPTH, 1, CM_CH)
    fg = final_norm_g[None, :]
    xa = x_prompt.reshape(T_PROMPT, D_MODEL)
    xb = x_sample.reshape(T_SAMPLE, D_MODEL)
    gla_s = None
    conv_p, gla_p, conv_s, cmv_s = [], [], [], []
    xin, alr = _prenorm(xa, xb, n1, w_alr, 0)
    for l in range(DEPTH):
        z, *experts = _inproj(xin, w_z, (exp_w1, exp_w3, exp_w2), l)
        acts_p, nconv, ngla = _mix_prompt(z, alr, a2, ab, conv_w, gng, cmg, ws_p, cmb_p, l)
        sc = state_conv[l]
        p2 = jnp.pad(sc, ((0, 0), (0, DEC_SEQ - 2), (0, 0))).reshape(T_SAMPLE, CONV_CH)
        p1 = jnp.pad(sc[:, 1:2], ((0, 0), (0, DEC_SEQ - 1), (0, 0))).reshape(T_SAMPLE, CONV_CH)
        acts_s, cin_s, gla_s, vrows = _mix_sample(z, alr, a2, ab, conv_w, gng, cmg, ws_s, cmb_s,
                                                  p1, p2, state_gla, gla_s, l)
        conv_p.append(nconv)
        gla_p.append(ngla)
        conv_s.append(cin_s.reshape(DEC_BATCH, DEC_SEQ, CONV_CH)[:, DEC_SEQ - (CONV_K - 1):])
        cmv_s.append(vrows.reshape(DEC_BATCH, DEC_SEQ, CM_CH))

        x, xn, route, seg, cnt = _outproj(acts_p, acts_s, z, xa, xb, pa, pb, pc, wo, n2, w_r, r_b,
                                          l)
        plan = _plan(cnt)
        n_used = plan[PLAN_USED, :1]
        grp_base = plan[PLAN_BASE, :MOE_GROUPS]
        xs = _scatter(seg, route, xn, n_used, plan[PLAN_END, :MOE_GROUPS],
                      plan[PLAN_BLOCKS, :MOE_GROUPS], grp_base)
        ys = _ffn(plan[PLAN_GROUP, :N_BLK], n_used, xs, *experts)
        if l == DEPTH - 1:
            out = _combine(seg, route, grp_base, ys, x, fg, w_alr[l], True)
        else:
            xa, xin, alr = _combine(seg, route, grp_base, ys, x, n1[l + 1], w_alr[l + 1], False)
            xb = xa

    y_prompt = out[0].reshape(BATCH, SEQ, D_MODEL)
    y_sample = out[1].reshape(DEC_BATCH, DEC_SEQ, D_MODEL)
    return (y_prompt, y_sample, jnp.stack(conv_p), jnp.stack(gla_p), jnp.stack(conv_s),
            gla_s, jnp.stack(cmv_s))
```

```python
import functools

import jax
import jax.numpy as jnp
from jax import lax
from jax.experimental import pallas as pl
from jax.experimental.pallas import tpu as pltpu

F32 = jnp.float32
BF16 = jnp.bfloat16

D_MODEL = 1024
BATCH = 8
SEQ = 2048
DEPTH = 2
DEC_BATCH = 128
DEC_SEQ = 8
CONV_K = 3
CONV_CH = 512
GLA_HEADS = 4
GLA_DK = 128
GLA_DV = 256
GLA_QK = GLA_HEADS * GLA_DK
GLA_V = GLA_HEADS * GLA_DV
GLA_LOWRANK = 16
GLA_TAU = 16.0
GLA_CHUNK = 64
CM_GROUPS = 4
CM_CHUNK = 128
CM_GCH = 128
CM_CH = 512
MOE_GROUPS = 8
EXPERTS_PER_GROUP = 8
N_EXPERTS = 64
D_EXPERT = 256
EPS = 1e-6

LANES = 128
T_PROMPT = BATCH * SEQ
T_SAMPLE = DEC_BATCH * DEC_SEQ
T_ALL = T_PROMPT + T_SAMPLE

COL_GA, COL_GB, COL_GC = 0, 1024, 2048
COL_V, COL_R, COL_Q, COL_K = 3072, 4096, 5120, 5632
COL_H, COL_CG, COL_BG, COL_U, COL_VV = 6144, 6656, 7168, 7680, 8192
Z_COLS = 8704
ACT_COLS = 2048
ROUTE_ROWS = 2 * EXPERTS_PER_GROUP
ROUTE_LRANK = EXPERTS_PER_GROUP

TM_PRE = 1024
TM_IN = 512
TN_IN = Z_COLS // 2

W_IN_WIDTHS = (("h", 512), ("cg", 512), ("bg", 512), ("q", 512), ("k", 512), ("v", 1024),
               ("r", 1024), ("alr", GLA_LOWRANK), ("u", 512), ("vv", 512), ("ga", 1024),
               ("gb", 1024), ("gc", 1024))
Z_ORDER = ("ga", "gb", "gc", "v", "r", "q", "k", "h", "cg", "bg", "u", "vv")


def _w_in_offsets():
    off, o = {}, 0
    for name, n in W_IN_WIDTHS:
        off[name] = (o, n)
        o += n
    return off


def _w_in_copies():
    off = _w_in_offsets()
    copies = [[] for _ in range(Z_COLS // TN_IN)]
    dst = 0
    for name in Z_ORDER:
        src, n = off[name]
        while n:
            tile = dst // TN_IN
            at = dst - tile * TN_IN
            take = min(n, TN_IN - at)
            runs = copies[tile]
            if runs and runs[-1][0] + runs[-1][2] == src and runs[-1][1] + runs[-1][2] == at:
                runs[-1] = (runs[-1][0], runs[-1][1], runs[-1][2] + take)
            else:
                runs.append((src, at, take))
            src, dst, n = src + take, dst + take, n - take
    assert dst == Z_COLS
    assert all(v % 16 == 0 for runs in copies for run in runs for v in run)
    return copies
TC_MIX = 256
SEQ_PER_BLK = 16
ROWS_S = SEQ_PER_BLK * DEC_SEQ
TM_OUT = 256
ROUTE_DELAY = 1
TM_ROW = TM_OUT
ROW_TILES = 4
COPY_STAGE_GROUPS = 4
MOE_BLK = 256
N_BLK = T_ALL // MOE_BLK + MOE_GROUPS
N_SORTED = N_BLK * MOE_BLK
VMEM_LIMIT = 56 * 1024 * 1024


def _sigmoid(x):
    return 0.5 * jnp.tanh(0.5 * x) + 0.5


def _gelu_tanh(x):
    c = 0.7978845608028654
    half = 0.5 * x
    return half + half * jnp.tanh(x * (c + (c * 0.044715) * (x * x)))


def _log_sigmoid(x):
    log2_e = 1.4426950408889634
    ln_2 = 0.6931471805599453
    return jnp.minimum(x, 0.0) - ln_2 * jnp.log2(1.0 + jnp.exp2(-log2_e * jnp.abs(x)))


def _rms(x, g):
    ms = jnp.mean(x * x, axis=-1, keepdims=True)
    return x * lax.rsqrt(ms + EPS) * g


def _split_bf16(x):
    hi = x.astype(BF16)
    lo = (x - hi.astype(F32)).astype(BF16)
    return hi, lo


def _dot(a, b):
    return jnp.dot(a, b, preferred_element_type=F32)


def _dot_nt(a, b):
    return lax.dot_general(a, b, (((1,), (1,)), ((), ())), preferred_element_type=F32)


def _layer_spec(arr, layer):
    nd = arr.ndim - 1
    return pl.BlockSpec((None,) + arr.shape[1:], lambda *g: (layer,) + (0,) * nd)


def _two_part_specs(tile, xa, xb):
    n_a = xa.shape[0] // tile
    n_b = xb.shape[0] // tile
    return n_a, [
        pl.BlockSpec((tile, xa.shape[1]), lambda i, *_: (jnp.minimum(i, n_a - 1), 0)),
        pl.BlockSpec((tile, xb.shape[1]), lambda i, *_: (jnp.clip(i - n_a, 0, n_b - 1), 0)),
    ]


def _norm_and_lowrank(x, g_ref, wa_ref, xn_ref, a_ref):
    xn = _rms(x, g_ref[...]).astype(BF16)
    xn_ref[...] = xn
    a_ref[...] = _dot_nt(xn, wa_ref[...]).astype(BF16)


def _prenorm_kernel(xa_ref, xb_ref, g_ref, wa_ref, xn_ref, a_ref, *, n_a):
    x = jnp.where(pl.program_id(0) < n_a, xa_ref[...], xb_ref[...])
    _norm_and_lowrank(x, g_ref, wa_ref, xn_ref, a_ref)


def _prenorm(xa, xb, g, wa, layer):
    m = T_ALL
    n_a, x_specs = _two_part_specs(TM_PRE, xa, xb)
    return pl.pallas_call(
        functools.partial(_prenorm_kernel, n_a=n_a),
        grid=(m // TM_PRE,),
        in_specs=x_specs + [_layer_spec(g, layer), _layer_spec(wa, layer)],
        out_specs=[pl.BlockSpec((TM_PRE, D_MODEL), lambda i: (i, 0)),
                   pl.BlockSpec((TM_PRE, LANES), lambda i: (i, 0))],
        out_shape=[jax.ShapeDtypeStruct((m, D_MODEL), BF16),
                   jax.ShapeDtypeStruct((m, LANES), BF16)],
        compiler_params=pltpu.CompilerParams(
            dimension_semantics=("arbitrary",), vmem_limit_bytes=VMEM_LIMIT),
        name="prenorm",
    )(xa, xb, g, wa)


EXPERT_SHAPES = ((D_MODEL, D_EXPERT), (D_MODEL, D_EXPERT), (D_EXPERT, D_MODEL))


def _inproj_kernel(xn_ref, w_hbm, *refs, layer):
    n = len(EXPERT_SHAPES)
    z_ref, w_buf, sem = refs[n], refs[-2], refs[-1]
    j, i = pl.program_id(0), pl.program_id(1)

    def tile_copies(tile):
        return [pltpu.make_async_copy(w_hbm.at[layer, pl.ds(src, rows)],
                                      w_buf.at[tile, pl.ds(at, rows)], sem.at[tile])
                for src, at, rows in _w_in_copies()[tile]]

    @pl.when((j == 0) & (i == 0))
    def _():
        for tile in range(w_buf.shape[0]):
            for cp in tile_copies(tile):
                cp.start()

    for tile in range(w_buf.shape[0]):
        @pl.when((j == tile) & (i == 0))
        def _():
            for cp in tile_copies(tile):
                cp.wait()

    for src, dst in zip(refs[:n], refs[n + 1:]):
        dst[...] = src[...].astype(BF16)
    z_ref[...] = _dot_nt(xn_ref[...], w_buf[j]).astype(BF16)


def _inproj(xn, w, experts, layer):
    m = xn.shape[0]
    n_j, n_i = Z_COLS // TN_IN, m // TM_IN
    assert n_j * n_i >= N_EXPERTS
    expert = lambda j, i: jnp.minimum(j * n_i + i, N_EXPERTS - 1)
    return pl.pallas_call(
        functools.partial(_inproj_kernel, layer=layer),
        grid=(n_j, n_i),
        in_specs=[
            pl.BlockSpec((TM_IN, D_MODEL), lambda j, i: (i, 0)),
            pl.BlockSpec(memory_space=pl.ANY),
        ] + [pl.BlockSpec((None, None) + s, lambda j, i: (layer, expert(j, i), 0, 0))
             for s in EXPERT_SHAPES],
        out_specs=[pl.BlockSpec((TM_IN, TN_IN), lambda j, i: (i, j))]
        + [pl.BlockSpec((None,) + s, lambda j, i: (expert(j, i), 0, 0)) for s in EXPERT_SHAPES],
        out_shape=[jax.ShapeDtypeStruct((m, Z_COLS), BF16)]
        + [jax.ShapeDtypeStruct((N_EXPERTS,) + s, BF16) for s in EXPERT_SHAPES],
        scratch_shapes=[pltpu.VMEM((n_j, TN_IN, D_MODEL), BF16), pltpu.SemaphoreType.DMA((n_j,))],
        compiler_params=pltpu.CompilerParams(
            dimension_semantics=("arbitrary", "arbitrary"), vmem_limit_bytes=VMEM_LIMIT),
        name="inproj",
    )(xn, w, *experts)


def _gla_log_decay(alr_ref, a2_ref, ab_ref):
    la = _log_sigmoid(_dot(alr_ref[...], a2_ref[...]) + ab_ref[...]) * (1.0 / GLA_TAU)
    return _split_bf16(la)


def _masked_sum(mask, la_hi, la_lo):
    m = jnp.where(mask, 1.0, 0.0).astype(BF16)
    return _dot(m, la_hi) + _dot(m, la_lo)


def _gla_decay_terms(alr_ref, a2_ref, ab_ref, tril_mask, same_mask):
    la_hi, la_lo = _gla_log_decay(alr_ref, a2_ref, ab_ref)
    return (la_hi, la_lo, _masked_sum(tril_mask, la_hi, la_lo),
            _masked_sum(same_mask, la_hi, la_lo))


def _gla_out_gate(o, g_ref, r):
    return _rms(o, g_ref[...]) * (r * _sigmoid(r))


def _chunk_mlp_group(g, bgu_ref, vv_ref, cmg_ref, ws_ref, cmb_ref, n_chunks):
    sl = slice(g * CM_GCH, (g + 1) * CM_GCH)
    ug = _gelu_tanh(bgu_ref[:, CONV_CH + g * CM_GCH:CONV_CH + (g + 1) * CM_GCH])
    vg = _rms(_gelu_tanh(vv_ref[:, sl]).astype(F32), cmg_ref[:, sl])
    vgb = vg.astype(BF16)
    rows = []
    for j in range(n_chunks):
        rs = slice(j * CM_CHUNK, (j + 1) * CM_CHUNK)
        rows.append(_dot(ws_ref[g], vgb[rs]) + cmb_ref[:, sl])
    s = rows[0] if n_chunks == 1 else jnp.concatenate(rows, axis=0)
    return ug * s, vg


def _chunk_mlp(bgu_ref, vv_ref, cmg_ref, ws_ref, cmb_ref, n_chunks):
    parts = [_chunk_mlp_group(g, bgu_ref, vv_ref, cmg_ref, ws_ref, cmb_ref, n_chunks)
             for g in range(CM_GROUPS)]
    return (jnp.concatenate([p[0] for p in parts], axis=1),
            jnp.concatenate([p[1] for p in parts], axis=1))


N_Z_VIEWS = 7
N_MIX_W = 7
SEQ_PER_STEP = 4
SEQ_STAGGER = 3


def _mix_prompt_kernel(*refs):
    n_z = N_Z_VIEWS * SEQ_PER_STEP
    weights = refs[n_z:n_z + N_MIX_W]
    acts_ref, nconv_ref, ngla_ref, st_ref, carry_ref = refs[n_z + N_MIX_W:]

    @pl.when(pl.program_id(1) == 0)
    def _():
        st_ref[...] = jnp.zeros_like(st_ref)
        carry_ref[...] = jnp.zeros_like(carry_ref)

    rr = lax.broadcasted_iota(jnp.int32, (TC_MIX, TC_MIX), 0)
    cc = lax.broadcasted_iota(jnp.int32, (TC_MIX, TC_MIX), 1)
    tril = ((rr >> 6) == (cc >> 6)) & (cc <= rr)
    masks = (tril, jnp.where(tril, 1.0, 0.0).astype(BF16),
             jnp.where(rr - cc == 1, 1.0, 0.0).astype(BF16),
             jnp.where(rr - cc == 2, 1.0, 0.0).astype(BF16))

    _interleave(*[
        _delayed(_mix_prompt_seq(*refs[N_Z_VIEWS * s:N_Z_VIEWS * (s + 1)], *weights,
                                acts_ref.at[s], nconv_ref.at[s], st_ref.at[s], carry_ref.at[s],
                                masks),
                s * SEQ_STAGGER)
        for s in range(SEQ_PER_STEP)])

    @pl.when(pl.program_id(1) == pl.num_programs(1) - 1)
    def _():
        for s in range(SEQ_PER_STEP):
            for hd in range(GLA_HEADS):
                ngla_ref[s, 0, hd] = st_ref[s, hd].T


def _mix_prompt_seq(v_ref, r_ref, qk_ref, hcg_ref, bgu_ref, vv_ref, alr_ref,
                    a2_ref, ab_ref, cw_ref, gng_ref, cmg_ref, ws_ref, cmb_ref,
                    acts_ref, nconv_ref, st_ref, carry_ref, masks):
    tc = TC_MIX

    h = hcg_ref[:, :CONV_CH].astype(F32)
    cg = hcg_ref[:, CONV_CH:].astype(F32)
    bg = bgu_ref[:, :CONV_CH].astype(F32)
    cin = cg * h
    tril, tril_01, shift_1, shift_2 = masks
    cin_b = cin.astype(BF16)
    x1 = _dot(shift_1, cin_b)
    x2 = _dot(shift_2, cin_b)
    conv = x2 * cw_ref[0:1, :] + x1 * cw_ref[1:2, :] + cin * cw_ref[2:3, :]
    c0 = carry_ref[0:1, :]
    c1 = carry_ref[1:2, :]
    row8 = lax.broadcasted_iota(jnp.int32, (8, 1), 0)
    head = jnp.where(row8 == 0, c0 * cw_ref[0:1, :] + c1 * cw_ref[1:2, :],
                     jnp.where(row8 == 1, c1 * cw_ref[0:1, :], 0.0))
    conv = jnp.concatenate([conv[0:8] + head, conv[8:]], axis=0)
    acts_ref[:, 0:CONV_CH] = (bg * conv).astype(BF16)
    carry_ref[0:2, :] = cin[tc - 2:tc, :]
    nconv_ref[0] = cin[tc - 2:tc, :]
    yield

    la_hi, la_lo = _gla_log_decay(alr_ref, a2_ref, ab_ref)
    b = _dot(tril_01, la_hi) + _dot(tril_01, la_lo)
    yield
    n_chunks = tc // GLA_CHUNK
    b_last = [b[(c + 1) * GLA_CHUNK - 1:(c + 1) * GLA_CHUNK, :] for c in range(n_chunks)]
    bl = jnp.concatenate([jnp.broadcast_to(r_, (GLA_CHUNK, GLA_QK)) for r_ in b_last], axis=0)
    q = qk_ref[:, :GLA_QK].astype(F32) * (GLA_DK ** -0.5)
    k = qk_ref[:, GLA_QK:].astype(F32)
    q_t = (q * jnp.exp(b)).astype(BF16)
    k_t = (k * jnp.exp(-b)).astype(BF16)
    k_end = (k * jnp.exp(bl - b)).astype(BF16)
    yield
    states = [st_ref[hd] for hd in range(GLA_HEADS)]
    k_cols = [slice(hd * GLA_DK, (hd + 1) * GLA_DK) for hd in range(GLA_HEADS)]
    v_cols = [slice(hd * GLA_DV, (hd + 1) * GLA_DV) for hd in range(GLA_HEADS)]
    o_intra = []
    for hd in range(GLA_HEADS):
        att = jnp.where(tril, _dot_nt(q_t[:, k_cols[hd]], k_t[:, k_cols[hd]]), 0.0).astype(BF16)
        o_intra.append(_dot(att, v_ref[:, v_cols[hd]]))
    yield

    o_rows = [[] for _ in range(GLA_HEADS)]
    assert n_chunks == CM_GROUPS
    for c in range(n_chunks):
        rs = slice(c * GLA_CHUNK, (c + 1) * GLA_CHUNK)
        for hd in range(GLA_HEADS):
            ks = k_cols[hd]
            st = states[hd]
            o_rows[hd].append(o_intra[hd][rs] + _dot_nt(q_t[rs, ks], st.astype(BF16)))
            upd = lax.dot_general(v_ref[rs, v_cols[hd]], k_end[rs, ks], (((0,), (0,)), ((), ())),
                                  preferred_element_type=F32)
            states[hd] = jnp.exp(b_last[c][:, ks]) * st + upd
        us, _ = _chunk_mlp_group(c, bgu_ref, vv_ref, cmg_ref, ws_ref, cmb_ref, tc // CM_CHUNK)
        acts_ref[:, CONV_CH + GLA_V + c * CM_GCH:CONV_CH + GLA_V + (c + 1) * CM_GCH] = (
            us.astype(BF16))
        yield

    for hd in range(GLA_HEADS):
        o = jnp.concatenate(o_rows[hd], axis=0)
        r = r_ref[:, v_cols[hd]]
        acts_ref[:, CONV_CH + hd * GLA_DV:CONV_CH + (hd + 1) * GLA_DV] = (
            _gla_out_gate(o, gng_ref, r).astype(BF16))
        st_ref[hd] = states[hd]
        if hd % 2 == 1:
            yield


def _z_specs(rows, row_map):
    def spec(width, col):
        blk = col // width
        return pl.BlockSpec((rows, width), lambda *g: (row_map(*g), blk))
    return [spec(1024, COL_V), spec(1024, COL_R), spec(1024, COL_Q), spec(1024, COL_H),
            spec(1024, COL_BG), spec(512, COL_VV)]


def _mix_prompt(z, alr, a2, ab, cw, gng, cmg, ws, cmb, layer):
    nt = SEQ // TC_MIX
    nb = BATCH // SEQ_PER_STEP
    small = (a2, ab, cw, gng, cmg, ws, cmb)
    assert len(small) == N_MIX_W
    in_specs, args = [], []
    for s in range(SEQ_PER_STEP):
        row_map = lambda b, c, s=s: (b + s * nb) * nt + c
        in_specs += _z_specs(TC_MIX, row_map) + [
            pl.BlockSpec((TC_MIX, LANES), lambda b, c, row_map=row_map: (row_map(b, c), 0))]
        args += [z] * (N_Z_VIEWS - 1) + [alr]
    in_specs += [_layer_spec(a, layer) for a in small]
    acts, nconv, ngla = pl.pallas_call(
        _mix_prompt_kernel,
        grid=(nb, nt),
        in_specs=in_specs,
        out_specs=[
            pl.BlockSpec((SEQ_PER_STEP, TC_MIX, ACT_COLS), lambda b, c: (0, b * nt + c, 0)),
            pl.BlockSpec((SEQ_PER_STEP, 1, CONV_K - 1, CONV_CH), lambda b, c: (0, b, 0, 0)),
            pl.BlockSpec((SEQ_PER_STEP, 1, GLA_HEADS, GLA_DK, GLA_DV),
                         lambda b, c: (0, b, 0, 0, 0)),
        ],
        out_shape=[
            jax.ShapeDtypeStruct((SEQ_PER_STEP, T_PROMPT // SEQ_PER_STEP, ACT_COLS), BF16),
            jax.ShapeDtypeStruct((SEQ_PER_STEP, nb, CONV_K - 1, CONV_CH), F32),
            jax.ShapeDtypeStruct((SEQ_PER_STEP, nb, GLA_HEADS, GLA_DK, GLA_DV), F32),
        ],
        scratch_shapes=[pltpu.VMEM((SEQ_PER_STEP, GLA_HEADS, GLA_DV, GLA_DK), F32),
                        pltpu.VMEM((SEQ_PER_STEP, 8, CONV_CH), F32)],
        compiler_params=pltpu.CompilerParams(
            dimension_semantics=("arbitrary", "arbitrary"), vmem_limit_bytes=VMEM_LIMIT),
        name="mix_prompt",
    )(*args, *small)
    return (acts.reshape(T_PROMPT, ACT_COLS), nconv.reshape(BATCH, CONV_K - 1, CONV_CH),
            ngla.reshape(BATCH, GLA_HEADS, GLA_DK, GLA_DV))


def _mix_sample_body(v_ref, r_ref, qk_ref, hcg_ref, bgu_ref, vv_ref, alr_ref,
                       a2_ref, ab_ref, cw_ref, gng_ref, cmg_ref, ws_ref, cmb_ref,
                       p1_ref, p2_ref, s0_ref,
                       acts_ref, cin_ref, ns_ref, vrow_ref):
    n = ROWS_S

    pos = lax.broadcasted_iota(jnp.int32, (n, 1), 0) & (DEC_SEQ - 1)
    h = hcg_ref[:, :CONV_CH].astype(F32)
    cg = hcg_ref[:, CONV_CH:].astype(F32)
    bg = bgu_ref[:, :CONV_CH].astype(F32)
    cin = cg * h
    x1 = jnp.where(pos >= 1, pltpu.roll(cin, 1, 0), p1_ref[...])
    x2 = jnp.where(pos >= 2, pltpu.roll(cin, 2, 0), p2_ref[...])
    conv = x2 * cw_ref[0:1, :] + x1 * cw_ref[1:2, :] + cin * cw_ref[2:3, :]
    acts_ref[:, 0:CONV_CH] = (bg * conv).astype(BF16)
    cin_ref[...] = cin

    rr = lax.broadcasted_iota(jnp.int32, (n, n), 0)
    cc = lax.broadcasted_iota(jnp.int32, (n, n), 1)
    same = (rr >> 3) == (cc >> 3)
    tril = same & (cc <= rr)
    la_hi, la_lo, b, bl = _gla_decay_terms(alr_ref, a2_ref, ab_ref, tril, same)
    q = qk_ref[:, :GLA_QK].astype(F32) * (GLA_DK ** -0.5)
    k = qk_ref[:, GLA_QK:].astype(F32)
    q_t = (q * jnp.exp(b)).astype(BF16)
    k_t = (k * jnp.exp(-b)).astype(BF16)
    k_end = k * jnp.exp(bl - b)
    la_hi = la_hi.astype(F32)
    la_lo = la_lo.astype(F32)
    row_seq = lax.broadcasted_iota(jnp.int32, (n, GLA_DK), 0) >> 3
    seq3 = lax.broadcasted_iota(jnp.int32, (SEQ_PER_BLK, GLA_DK, n), 0)
    lane_seq3 = lax.broadcasted_iota(jnp.int32, (SEQ_PER_BLK, GLA_DK, n), 2) >> 3
    mask3 = seq3 == lane_seq3
    ones = jnp.ones((n, GLA_DV), BF16)
    big = SEQ_PER_BLK * GLA_DK

    def per_seq(x_tr):
        x3 = jnp.where(mask3, x_tr[None, :, :], 0.0)
        return x3.reshape(big, n).astype(BF16)

    for hd in range(GLA_HEADS):
        ks = slice(hd * GLA_DK, (hd + 1) * GLA_DK)
        vs = slice(hd * GLA_DV, (hd + 1) * GLA_DV)
        qh = q_t[:, ks]
        vh = v_ref[:, vs]
        att = jnp.where(tril, _dot_nt(qh, k_t[:, ks]), 0.0).astype(BF16)
        o_intra = _dot(att, vh)
        s_old = s0_ref[:, hd].reshape(big, GLA_DV)
        zero = jnp.zeros_like(qh)
        q_big = jnp.concatenate(
            [jnp.where(row_seq == j, qh, zero) for j in range(SEQ_PER_BLK)], axis=1)
        o = o_intra + _dot(q_big, s_old.astype(BF16))
        dlog = _dot(per_seq(la_hi[:, ks].T), ones) + _dot(per_seq(la_lo[:, ks].T), ones)
        upd = _dot(per_seq(k_end[:, ks].T), vh)
        s_new = jnp.exp(dlog) * s_old + upd
        ns_ref[:, hd] = s_new.reshape(SEQ_PER_BLK, GLA_DK, GLA_DV)
        r = r_ref[:, vs]
        acts_ref[:, CONV_CH + hd * GLA_DV:CONV_CH + (hd + 1) * GLA_DV] = (
            _gla_out_gate(o, gng_ref, r).astype(BF16))

    us, vg = _chunk_mlp(bgu_ref, vv_ref, cmg_ref, ws_ref, cmb_ref, 1)
    acts_ref[:, CONV_CH + GLA_V:] = us.astype(BF16)
    vrow_ref[...] = vg


N_MIX_S_IN = 17


def _mix_sample_kernel(*refs, layer):
    if layer == 0:
        @pl.when(pl.program_id(0) == 0)
        def _():
            _mix_sample_body(*refs)

        @pl.when(pl.program_id(0) > 0)
        def _():
            ns_ref = refs[N_MIX_S_IN + 2]
            ns_ref[...] = jnp.zeros_like(ns_ref)
    else:
        _mix_sample_body(*refs[:N_MIX_S_IN], *refs[N_MIX_S_IN + 1:])


def _mix_sample(z, alr, a2, ab, cw, gng, cmg, ws, cmb, p1, p2, s0, ns_all, layer):
    row0 = T_PROMPT // ROWS_S
    n_i = DEC_BATCH // SEQ_PER_BLK
    n_pass = DEPTH if layer == 0 else 1
    blk = lambda p, i: jnp.where(p == 0, i, n_i - 1)
    row_map = lambda p, i: row0 + blk(p, i)
    slot = lambda p, i: (layer + p, i, 0, 0, 0)
    small = (a2, ab, cw, gng, cmg, ws, cmb)
    state_blk = (None, SEQ_PER_BLK, GLA_HEADS, GLA_DK, GLA_DV)
    in_specs = _z_specs(ROWS_S, row_map) + [
        pl.BlockSpec((ROWS_S, LANES), lambda p, i: (row_map(p, i), 0)),
    ] + [_layer_spec(a, layer) for a in small] + [
        pl.BlockSpec((ROWS_S, CONV_CH), lambda p, i: (blk(p, i), 0)),
        pl.BlockSpec((ROWS_S, CONV_CH), lambda p, i: (blk(p, i), 0)),
        pl.BlockSpec(state_blk, lambda p, i: (layer, blk(p, i), 0, 0, 0)),
    ]
    args = (z, z, z, z, z, z, alr, *small, p1, p2, s0)
    assert len(args) == N_MIX_S_IN
    aliases = {}
    if layer > 0:
        in_specs.append(pl.BlockSpec(memory_space=pl.ANY))
        args += (ns_all,)
        aliases = {N_MIX_S_IN: 2}
    return pl.pallas_call(
        functools.partial(_mix_sample_kernel, layer=layer),
        grid=(n_pass, n_i),
        in_specs=in_specs,
        out_specs=[
            pl.BlockSpec((ROWS_S, ACT_COLS), lambda p, i: (blk(p, i), 0)),
            pl.BlockSpec((ROWS_S, CONV_CH), lambda p, i: (blk(p, i), 0)),
            pl.BlockSpec(state_blk, slot),
            pl.BlockSpec((ROWS_S, CM_CH), lambda p, i: (blk(p, i), 0)),
        ],
        out_shape=[
            jax.ShapeDtypeStruct((T_SAMPLE, ACT_COLS), BF16),
            jax.ShapeDtypeStruct((T_SAMPLE, CONV_CH), F32),
            jax.ShapeDtypeStruct((DEPTH, DEC_BATCH, GLA_HEADS, GLA_DK, GLA_DV), F32),
            jax.ShapeDtypeStruct((T_SAMPLE, CM_CH), F32),
        ],
        input_output_aliases=aliases,
        compiler_params=pltpu.CompilerParams(
            dimension_semantics=("arbitrary", "arbitrary"), vmem_limit_bytes=VMEM_LIMIT),
        name="mix_sample",
    )(*args)


def _delayed(gen, n_stages):
    for _ in range(n_stages):
        yield
    yield from gen


def _interleave(*stage_lists):
    pending = list(stage_lists)
    while pending:
        for gen in list(pending):
            if next(gen, StopIteration) is StopIteration:
                pending.remove(gen)


def _outproj_main(i, n_xa, actp_ref, acts_ref, gates_ref, xa_ref, xb_ref, pa_ref, pb_ref, pc_ref,
                  wo_ref, n2_ref, xo_ref, xn_ref, xnb_ref):
    acts = jnp.where(i < T_PROMPT // TM_OUT, actp_ref[...], acts_ref[...])
    x_in = jnp.where(i < n_xa, xa_ref[...], xb_ref[...])
    yb = _dot(acts[:, CONV_CH:CONV_CH + GLA_V], pb_ref[...])
    ga = _sigmoid(gates_ref[:, COL_GA:COL_GA + D_MODEL])
    ya = _dot(acts[:, :CONV_CH], pa_ref[...])
    gb = _sigmoid(gates_ref[:, COL_GB:COL_GB + D_MODEL])
    yield
    yc = _dot(acts[:, CONV_CH + GLA_V:], pc_ref[...])
    gc = _sigmoid(gates_ref[:, COL_GC:COL_GC + D_MODEL])
    mix = ga * ya + gb * yb
    yield
    mix = mix + gc * yc
    x = x_in + _dot(mix.astype(BF16), wo_ref[...])
    xo_ref[...] = x
    yield
    xn = _rms(x, n2_ref[...])
    xn_ref[...] = xn
    xnb = xn.astype(BF16)
    yield
    xnb_ref[...] = xnb


def _outproj_route(live, xnb_ref, wr_ref, rb_ref, route_ref, seg_ref, cnt_ref):
    tm = TM_OUT
    logits = _dot_nt(wr_ref[...], xnb_ref[...]) + rb_ref[...]
    yield
    n_sub = EXPERTS_PER_GROUP
    sub = lax.broadcasted_iota(jnp.int32, (n_sub, tm), 0).astype(F32)
    neg = jnp.float32(-jnp.inf)
    lg = logits[0:MOE_GROUPS]
    gmax = jnp.max(lg, axis=0, keepdims=True)
    grp = jnp.min(jnp.where(lg == gmax, sub, 1e9), axis=0, keepdims=True)
    p_grp = 1.0 / jnp.sum(jnp.exp(lg - gmax), axis=0, keepdims=True)
    yield
    le = logits[MOE_GROUPS:MOE_GROUPS + n_sub]
    for g in range(1, MOE_GROUPS):
        le = jnp.where(grp == g, logits[MOE_GROUPS + g * n_sub:MOE_GROUPS + (g + 1) * n_sub], le)
    v1 = jnp.max(le, axis=0, keepdims=True)
    i1 = jnp.min(jnp.where(le == v1, sub, 1e9), axis=0, keepdims=True)
    le2 = jnp.where(sub == i1, neg, le)
    v2 = jnp.max(le2, axis=0, keepdims=True)
    i2 = jnp.min(jnp.where(le2 == v2, sub, 1e9), axis=0, keepdims=True)
    yield
    t = jnp.exp(v2 - v1)
    g1 = p_grp / (1.0 + t)
    g2 = p_grp * t / (1.0 + t)
    gate_t = jnp.where(sub == i1, g1, jnp.where(sub == i2, g2, 0.0))

    onehot_t = jnp.where(sub == grp, 1.0, 0.0)
    csum = jnp.where(live, jnp.sum(onehot_t, axis=1, keepdims=True), 0.0)
    rr = lax.broadcasted_iota(jnp.int32, (tm, tm), 0)
    cc = lax.broadcasted_iota(jnp.int32, (tm, tm), 1)
    earlier = jnp.where(rr < cc, 1.0, 0.0).astype(BF16)
    padded = jnp.concatenate([onehot_t, jnp.zeros_like(onehot_t)], axis=0).astype(BF16)
    same_before = _dot(padded, earlier)[0:n_sub]
    yield
    lower = jnp.sum(jnp.where(sub < grp, csum, 0.0), axis=0, keepdims=True)
    lrank = lower + jnp.sum(onehot_t * same_before, axis=0, keepdims=True)
    route_ref[0:n_sub, :] = gate_t
    route_ref[n_sub:, :] = jnp.broadcast_to(lrank, (n_sub, tm))
    carry = cnt_ref[...]
    lane = lax.broadcasted_iota(jnp.int32, (n_sub, LANES), 1)
    seg_ref[...] = jnp.where(lane == SEG_START, carry,
                             jnp.where(lane == SEG_LEN, csum, 0.0)).astype(jnp.int32)
    cnt_ref[...] = carry + csum


def _outproj_kernel(actp_ref, acts_ref, gates_ref, xa_ref, xb_ref, pa_ref, pb_ref, pc_ref, wo_ref,
                    n2_ref, wr_ref, rb_ref, xo_ref, xn_ref, route_ref, seg_ref, cnt_ref, xnb_ref,
                    *, n_xa):
    i = pl.program_id(0)

    @pl.when(i == 0)
    def _():
        cnt_ref[...] = jnp.zeros_like(cnt_ref)
        xnb_ref[...] = jnp.zeros_like(xnb_ref)

    tile = jnp.minimum(i, pl.num_programs(0) - 2)
    _interleave(
        _delayed(_outproj_route(i >= 1, xnb_ref, wr_ref, rb_ref, route_ref, seg_ref, cnt_ref),
                 ROUTE_DELAY),
        _outproj_main(tile, n_xa, actp_ref, acts_ref, gates_ref, xa_ref, xb_ref, pa_ref, pb_ref,
                      pc_ref, wo_ref, n2_ref, xo_ref, xn_ref, xnb_ref))


def _lane_prefix(v):
    rr = lax.broadcasted_iota(jnp.int32, (LANES, LANES), 0)
    cc = lax.broadcasted_iota(jnp.int32, (LANES, LANES), 1)
    earlier = jnp.where(rr < cc, 1.0, 0.0).astype(BF16)
    return _dot(jnp.broadcast_to(v, (8, LANES)).astype(BF16), earlier)[0:1]


def _outproj(acts_p, acts_s, z, xa, xb, pa, pb, pc, wo, n2, wr, rb, layer):
    m = T_ALL
    nt = m // TM_OUT
    row = lambda i: (jnp.minimum(i, nt - 1), 0)
    routed = lambda i: (jnp.maximum(i - 1, 0), 0, 0)
    _, act_specs = _two_part_specs(TM_OUT, acts_p, acts_s)
    n_xa, x_specs = _two_part_specs(TM_OUT, xa, xb)
    weights = (pa, pb, pc, wo, n2, wr, rb)
    return pl.pallas_call(
        functools.partial(_outproj_kernel, n_xa=n_xa),
        grid=(nt + 1,),
        in_specs=act_specs + [pl.BlockSpec((TM_OUT, 3 * D_MODEL), row)] + x_specs
        + [_layer_spec(w, layer) for w in weights],
        out_specs=[
            pl.BlockSpec((TM_OUT, D_MODEL), row),
            pl.BlockSpec((TM_OUT, D_MODEL), row),
            pl.BlockSpec((None, ROUTE_ROWS, TM_OUT), routed),
            pl.BlockSpec((None, 8, LANES), routed),
            pl.BlockSpec((8, LANES), lambda i: (0, 0)),
        ],
        scratch_shapes=[pltpu.VMEM((TM_OUT, D_MODEL), BF16)],
        out_shape=[
            jax.ShapeDtypeStruct((m, D_MODEL), F32),
            jax.ShapeDtypeStruct((m, D_MODEL), F32),
            jax.ShapeDtypeStruct((m // TM_OUT, ROUTE_ROWS, TM_OUT), F32),
            jax.ShapeDtypeStruct((m // TM_OUT, 8, LANES), jnp.int32),
            jax.ShapeDtypeStruct((8, LANES), F32),
        ],
        compiler_params=pltpu.CompilerParams(
            dimension_semantics=("arbitrary",), vmem_limit_bytes=VMEM_LIMIT),
        name="outproj",
    )(acts_p, acts_s, z, xa, xb, *weights)


PLAN_GROUP, PLAN_USED, PLAN_END, PLAN_BLOCKS, PLAN_BASE = 0, 1, 2, 3, 4


def _plan_kernel(cnt_ref, plan_ref):
    sub_i = lax.broadcasted_iota(jnp.int32, (8, LANES), 0)
    lane8_i = lax.broadcasted_iota(jnp.int32, (8, LANES), 1)
    cnt = jnp.sum(jnp.where(sub_i == lane8_i, cnt_ref[...], 0.0), axis=0, keepdims=True)
    blocks = jnp.floor((cnt + (MOE_BLK - 1)) * (1.0 / MOE_BLK))
    start = _lane_prefix(blocks)
    end = start + blocks
    lane_i = lax.broadcasted_iota(jnp.int32, (1, LANES), 1)
    lane = lane_i.astype(F32)
    grp_of_blk = jnp.zeros((1, LANES), F32)
    for g in range(MOE_GROUPS):
        end_g = jnp.sum(jnp.where(lane_i == g, end, 0.0), axis=-1, keepdims=True)
        grp_of_blk += jnp.where(lane >= end_g, 1.0, 0.0)
    grp_of_blk = jnp.minimum(grp_of_blk, MOE_GROUPS - 1)
    n_used = jnp.sum(blocks, axis=-1, keepdims=True)
    row = lax.broadcasted_iota(jnp.int32, (8, LANES), 0)
    plan_ref[...] = jnp.where(
        row == PLAN_GROUP, grp_of_blk,
        jnp.where(row == PLAN_USED, n_used,
                  jnp.where(row == PLAN_END, end,
                            jnp.where(row == PLAN_BLOCKS, blocks,
                                      jnp.where(row == PLAN_BASE, start * MOE_BLK, 0.0))))
    ).astype(jnp.int32)


def _plan(cnt):
    return pl.pallas_call(
        _plan_kernel,
        out_shape=jax.ShapeDtypeStruct((8, LANES), jnp.int32),
        name="plan",
    )(cnt)


SEG_START, SEG_LEN = 0, 1


def _segment_copies(seg_ref, base_ref, make_copy, enable=None):
    local = 0
    for g in range(MOE_GROUPS):
        n = seg_ref[g, SEG_LEN]
        first = base_ref[g] + seg_ref[g, SEG_START]
        k = TM_ROW
        while k >= 1:
            done = n & ~(2 * k - 1)
            wanted = (n & k) != 0
            @pl.when(wanted if enable is None else wanted & enable)
            def _():
                make_copy(local + done, first + done, k).start()
            k //= 2
        local = local + n
        if g % COPY_STAGE_GROUPS == COPY_STAGE_GROUPS - 1:
            yield


def _perm_matrix(route_ref):
    row = lax.broadcasted_iota(jnp.int32, (TM_ROW, TM_ROW), 0).astype(F32)
    return jnp.where(row == route_ref[ROUTE_LRANK:ROUTE_LRANK + 1, :], 1.0, 0.0).astype(BF16)


SUB = 8
SUB_X = D_MODEL // 2 // LANES
SUB_GATE = SUB_X
HI_MASK = -65536


def _rows(first_row, n_rows):
    return pl.ds(pl.multiple_of(first_row * SUB, SUB), n_rows * SUB)


def _sublane(s, n_rows):
    return pl.ds(s, n_rows, stride=SUB)


def _scatter_kernel(nu_ref, end_ref, nb_ref, base_ref, seg_ref, route_ref, xn_ref, xs_ref,
                    zbuf, sbuf, sems):
    i = pl.program_id(0)
    n = pl.num_programs(0)
    sem = sems.at[0]

    def zero_block(b):
        return pltpu.make_async_copy(zbuf, xs_ref.at[_rows(b * MOE_BLK, MOE_BLK)], sem)

    def each_unfilled_block(fn):
        for g in range(MOE_GROUPS):
            @pl.when(nb_ref[g] > 0)
            def _():
                fn(zero_block(end_ref[g] - 1))
        for b in range(T_ALL // MOE_BLK, N_BLK):
            @pl.when(b >= nu_ref[0])
            def _():
                fn(zero_block(b))

    @pl.when(i == 0)
    def _():
        zbuf[...] = jnp.zeros_like(zbuf)
        sbuf[...] = jnp.zeros_like(sbuf)
        each_unfilled_block(lambda c: c.start())
        each_unfilled_block(lambda c: c.wait())

    def tile_done(s):
        pltpu.make_async_copy(sbuf.at[s], xs_ref.at[_rows(0, TM_ROW)], sems.at[s]).wait()

    @pl.when(i >= 2)
    def _():
        for t in range(ROW_TILES):
            tile_done((i % 2) * ROW_TILES + t)

    def sort_tile(t):
        slot = (i % 2) * ROW_TILES + t
        route = route_ref.at[t]
        perm = _perm_matrix(route)
        xn = xn_ref[t * TM_ROW:(t + 1) * TM_ROW, :]
        xs = lax.bitcast_convert_type(_dot(perm, xn.astype(BF16)), jnp.int32)
        yield
        half = D_MODEL // 2
        for s in range(SUB_X):
            hi = xs[:, s * LANES:(s + 1) * LANES] & HI_MASK
            lo = lax.shift_right_logical(xs[:, half + s * LANES:half + (s + 1) * LANES], 16)
            sbuf[slot, _sublane(s, TM_ROW), :] = hi | lo
        yield
        gate_t = route[0:EXPERTS_PER_GROUP, :]
        gate_t = jnp.concatenate(
            [gate_t, jnp.zeros((LANES - EXPERTS_PER_GROUP, TM_ROW), F32)], axis=0)
        g1 = gate_t.astype(BF16)
        r1 = gate_t - g1.astype(F32)
        g2 = r1.astype(BF16)
        g3 = (r1 - g2.astype(F32)).astype(BF16)
        sbuf[slot, _sublane(SUB_GATE, TM_ROW), :] = lax.bitcast_convert_type(
            _dot_nt(perm, g1) + _dot_nt(perm, g2) + _dot_nt(perm, g3), jnp.int32)

    _interleave(*[_delayed(sort_tile(t), t) for t in range(ROW_TILES)])

    for t in range(ROW_TILES):
        def make_copy(src_row, dst_row, k, slot=(i % 2) * ROW_TILES + t):
            return pltpu.make_async_copy(sbuf.at[slot, _rows(src_row, k)],
                                         xs_ref.at[_rows(dst_row, k)], sems.at[slot])
        for _ in _segment_copies(seg_ref.at[t], base_ref, make_copy):
            pass

    @pl.when(i == n - 1)
    def _():
        for t in range(ROW_TILES):
            tile_done((i % 2) * ROW_TILES + t)

            @pl.when(n > 1)
            def _():
                tile_done((1 - i % 2) * ROW_TILES + t)


def _scatter(seg, route, xn, n_used, grp_end, grp_blocks, grp_base):
    m = xn.shape[0]
    step_rows = ROW_TILES * TM_ROW
    grid_spec = pltpu.PrefetchScalarGridSpec(
        num_scalar_prefetch=4,
        grid=(m // step_rows,),
        in_specs=[
            pl.BlockSpec((ROW_TILES, 8, LANES), lambda i, *_: (i, 0, 0), memory_space=pltpu.SMEM),
            pl.BlockSpec((ROW_TILES, ROUTE_ROWS, TM_ROW), lambda i, *_: (i, 0, 0)),
            pl.BlockSpec((step_rows, D_MODEL), lambda i, *_: (i, 0)),
        ],
        out_specs=pl.BlockSpec(memory_space=pl.ANY),
        scratch_shapes=[pltpu.VMEM((MOE_BLK * SUB, LANES), jnp.int32),
                        pltpu.VMEM((2 * ROW_TILES, TM_ROW * SUB, LANES), jnp.int32),
                        pltpu.SemaphoreType.DMA((2 * ROW_TILES,))],
    )
    return pl.pallas_call(
        _scatter_kernel,
        grid_spec=grid_spec,
        out_shape=jax.ShapeDtypeStruct((N_SORTED * SUB, LANES), jnp.int32),
        compiler_params=pltpu.CompilerParams(
            dimension_semantics=("arbitrary",), vmem_limit_bytes=VMEM_LIMIT),
        name="scatter",
    )(n_used, grp_end, grp_blocks, grp_base, seg, route, xn)


def _ffn_kernel(bg_ref, nu_ref, xs_ref, w1_ref, w3_ref, w2_ref, y_ref):
    del bg_ref
    b = pl.program_id(0)

    def block_inputs():
        packed = [xs_ref[_sublane(s, MOE_BLK), :] for s in range(SUB_X)]
        x = jnp.concatenate(
            [lax.bitcast_convert_type(u & HI_MASK, F32).astype(BF16) for u in packed]
            + [lax.bitcast_convert_type(lax.shift_left(u, 16), F32).astype(BF16) for u in packed],
            axis=1)
        return x, lax.bitcast_convert_type(xs_ref[_sublane(SUB_GATE, MOE_BLK), :], F32)

    def store(y):
        for s in range(SUB):
            y_ref[_sublane(s, MOE_BLK), :] = y[:, s * LANES:(s + 1) * LANES]

    @pl.when(b < nu_ref[0])
    def _():
        x, gates = block_inputs()
        n_e = EXPERTS_PER_GROUP
        h1 = [_dot(x, w1_ref[e]) for e in range(n_e)]
        h3 = [_dot(x, w3_ref[e]) for e in range(n_e)]
        hs = []
        for e in range(n_e):
            ge = gates[:, e:e + 1]
            hs.append(jnp.where(ge > 0.0, h1[e] * _sigmoid(h1[e]) * h3[e] * ge, 0.0).astype(BF16))
        hcat = jnp.concatenate(hs, axis=1)
        store(_dot(hcat, w2_ref[...].reshape(n_e * D_EXPERT, D_MODEL)))

    @pl.when(b >= nu_ref[0])
    def _():
        y_ref[...] = jnp.zeros_like(y_ref)


def _ffn(blk_group, n_used, xs, w1, w3, w2):
    grouped = (MOE_GROUPS, EXPERTS_PER_GROUP)
    w1, w3, w2 = (w.reshape(grouped + w.shape[1:]) for w in (w1, w3, w2))
    wmap = lambda b, bg, nu: (bg[b], 0, 0, 0)
    grid_spec = pltpu.PrefetchScalarGridSpec(
        num_scalar_prefetch=2,
        grid=(N_BLK,),
        in_specs=[
            pl.BlockSpec((MOE_BLK * SUB, LANES), lambda b, bg, nu: (b, 0)),
            pl.BlockSpec((None, EXPERTS_PER_GROUP, D_MODEL, D_EXPERT), wmap),
            pl.BlockSpec((None, EXPERTS_PER_GROUP, D_MODEL, D_EXPERT), wmap),
            pl.BlockSpec((None, EXPERTS_PER_GROUP, D_EXPERT, D_MODEL), wmap),
        ],
        out_specs=pl.BlockSpec((MOE_BLK * SUB, LANES), lambda b, bg, nu: (b, 0)),
    )
    return pl.pallas_call(
        _ffn_kernel,
        grid_spec=grid_spec,
        out_shape=jax.ShapeDtypeStruct((N_SORTED * SUB, LANES), F32),
        compiler_params=pltpu.CompilerParams(
            dimension_semantics=("arbitrary",), vmem_limit_bytes=VMEM_LIMIT),
        name="ffn",
    )(blk_group, n_used, xs, w1, w3, w2)


def _combine_kernel(base_ref, seg_ref, segn_ref, route_ref, ys_hbm, x_ref, g_ref, wa_ref, *refs,
                    final):
    outs, (buf, sem, stage) = refs[:-3], refs[-3:]
    i = pl.program_id(0)
    n = pl.num_programs(0)
    step_rows = ROW_TILES * TM_ROW

    def gather(seg, parity, t, enable=None):
        s = parity * ROW_TILES + t

        def make_copy(buf_row, ys_row, k):
            return pltpu.make_async_copy(ys_hbm.at[_rows(ys_row, k)],
                                         buf.at[s, _rows(buf_row, k)], sem.at[s])
        return _segment_copies(seg.at[t], base_ref, make_copy, enable)

    @pl.when(i == 0)
    def _():
        for t in range(ROW_TILES):
            for _ in gather(seg_ref, 0, t):
                pass

    for t in range(ROW_TILES):
        slot = (i % 2) * ROW_TILES + t
        pltpu.make_async_copy(ys_hbm.at[_rows(0, TM_ROW)], buf.at[slot], sem.at[slot]).wait()

    def tile_stages(t):
        slot = (i % 2) * ROW_TILES + t
        rows = slice(t * TM_ROW, (t + 1) * TM_ROW)
        tn = (((0,), (0,)), ((), ()))
        perm = _perm_matrix(route_ref.at[t])
        cols = []
        for s in range(SUB):
            hi, lo = _split_bf16(buf[slot, _sublane(s, TM_ROW), :])
            cols.append(lax.dot_general(perm, hi, tn, preferred_element_type=F32)
                        + lax.dot_general(perm, lo, tn, preferred_element_type=F32))
            if s % 4 == 3:
                yield
        y = x_ref[rows, :] + jnp.concatenate(cols, axis=1)
        if not final:
            outs[0][rows, :] = y
            yield
            _norm_and_lowrank(y, g_ref, wa_ref, outs[1].at[rows, :], outs[2].at[rows, :])
        else:
            yield
            stage[rows, :] = _rms(y, g_ref[...])

    _interleave(*[_delayed(tile_stages(t), t) for t in range(ROW_TILES)],
                *[_delayed(gather(segn_ref, 1 - i % 2, t, i + 1 < n), t)
                  for t in range(ROW_TILES)])

    if final:
        @pl.when(i < T_PROMPT // step_rows)
        def _():
            outs[0][...] = stage[...]

        @pl.when(i >= T_PROMPT // step_rows)
        def _():
            outs[1][...] = stage[...]


def _combine(seg, route, grp_base, ys, x, g, wa, final):
    m = x.shape[0]
    step_rows = ROW_TILES * TM_ROW
    nt = m // step_rows
    n_p = T_PROMPT // step_rows
    smem = functools.partial(pl.BlockSpec, (ROW_TILES, 8, LANES), memory_space=pltpu.SMEM)
    tile = (step_rows, D_MODEL)
    if final:
        out_specs = [pl.BlockSpec(tile, lambda i, *_: (jnp.minimum(i, n_p - 1), 0)),
                     pl.BlockSpec(tile, lambda i, *_: (jnp.maximum(i - n_p, 0), 0))]
        out_shape = [jax.ShapeDtypeStruct((T_PROMPT, D_MODEL), F32),
                     jax.ShapeDtypeStruct((T_SAMPLE, D_MODEL), F32)]
    else:
        out_specs = [pl.BlockSpec(tile, lambda i, *_: (i, 0)),
                     pl.BlockSpec(tile, lambda i, *_: (i, 0)),
                     pl.BlockSpec((step_rows, LANES), lambda i, *_: (i, 0))]
        out_shape = [jax.ShapeDtypeStruct((m, D_MODEL), F32),
                     jax.ShapeDtypeStruct((m, D_MODEL), BF16),
                     jax.ShapeDtypeStruct((m, LANES), BF16)]
    grid_spec = pltpu.PrefetchScalarGridSpec(
        num_scalar_prefetch=1,
        grid=(nt,),
        in_specs=[
            smem(lambda i, *_: (i, 0, 0)),
            smem(lambda i, *_: (jnp.minimum(i + 1, nt - 1), 0, 0)),
            pl.BlockSpec((ROW_TILES, ROUTE_ROWS, TM_ROW), lambda i, *_: (i, 0, 0)),
            pl.BlockSpec(memory_space=pl.ANY),
            pl.BlockSpec(tile, lambda i, *_: (i, 0)),
            pl.BlockSpec(g.shape, lambda i, *_: (0, 0)),
            pl.BlockSpec(wa.shape, lambda i, *_: (0, 0)),
        ],
        out_specs=out_specs,
        scratch_shapes=[pltpu.VMEM((2 * ROW_TILES, TM_ROW * SUB, LANES), F32),
                        pltpu.SemaphoreType.DMA((2 * ROW_TILES,)),
                        pltpu.VMEM(tile if final else (8, LANES), F32)],
    )
    return pl.pallas_call(
        functools.partial(_combine_kernel, final=final),
        grid_spec=grid_spec,
        out_shape=out_shape,
        compiler_params=pltpu.CompilerParams(
            dimension_semantics=("arbitrary",), vmem_limit_bytes=VMEM_LIMIT),
        name="combine",
    )(grp_base, seg, seg, route, ys, x, g, wa)


def _prep_weights(w_in, gla_a2, cm_ws, cm_b, router_group_w, router_group_b,
                  router_expert_w, router_expert_b):
    w_z = jnp.swapaxes(w_in, 1, 2).astype(BF16)
    a0 = _w_in_offsets()["alr"][0]
    w_alr = jnp.pad(w_z[:, a0:a0 + GLA_LOWRANK], ((0, 0), (0, LANES - GLA_LOWRANK), (0, 0)))
    a2 = jnp.pad(gla_a2, ((0, 0), (0, LANES - GLA_LOWRANK), (0, 0))).astype(BF16)
    ws_p = jnp.tril(cm_ws).astype(BF16)
    small = jnp.tril(cm_ws[:, :, :DEC_SEQ, :DEC_SEQ])
    eye = jnp.eye(SEQ_PER_BLK, dtype=F32)
    ws_s = jnp.einsum("ij,lgab->lgiajb", eye, small).reshape(
        DEPTH, CM_GROUPS, ROWS_S, ROWS_S).astype(BF16)
    cmb_p = jnp.broadcast_to(jnp.transpose(cm_b, (0, 2, 1))[:, :, :, None],
                             (DEPTH, CM_CHUNK, CM_GROUPS, CM_GCH)).reshape(DEPTH, CM_CHUNK, CM_CH)
    cmb_s = jnp.tile(cmb_p[:, :DEC_SEQ], (1, SEQ_PER_BLK, 1))
    pad = LANES - MOE_GROUPS - N_EXPERTS
    w_r = jnp.pad(jnp.swapaxes(jnp.concatenate([router_group_w, router_expert_w], axis=-1), 1, 2),
                  ((0, 0), (0, pad), (0, 0)))
    r_b = jnp.pad(jnp.concatenate([router_group_b, router_expert_b], axis=-1),
                  ((0, 0), (0, pad)))[:, :, None]
    return w_z, w_alr, a2, ws_p, ws_s, cmb_p, cmb_s, w_r.astype(BF16), r_b


def kernel(x_prompt, x_sample, state_conv, state_gla, norm1_g, w_in, conv_w, gla_a2, gla_a_b,
           gla_norm_g, cm_norm_g, cm_ws, cm_b, proj_a, proj_b, proj_c, w_out, norm2_g,
           router_group_w, router_group_b, router_expert_w, router_expert_b,
           exp_w1, exp_w3, exp_w2, final_norm_g):
    (w_z, w_alr, a2, ws_p, ws_s, cmb_p, cmb_s, w_r, r_b) = _prep_weights(
        w_in, gla_a2, cm_ws, cm_b, router_group_w, router_group_b, router_expert_w,
        router_expert_b)
    pa, pb, pc, wo = (w.astype(BF16) for w in (proj_a, proj_b, proj_c, w_out))
    n1 = norm1_g[:, None, :]
    n2 = norm2_g[:, None, :]
    ab = gla_a_b[:, None, :]
    gng = gla_norm_g[:, None, :]
    cmg = cm_norm_g.reshape(DEPTH, 1, CM_CH)
    fg = final_norm_g[None, :]
    xa = x_prompt.reshape(T_PROMPT, D_MODEL)
    xb = x_sample.reshape(T_SAMPLE, D_MODEL)
    gla_s = None
    conv_p, gla_p, conv_s, cmv_s = [], [], [], []
    xin, alr = _prenorm(xa, xb, n1, w_alr, 0)
    for l in range(DEPTH):
        z, *experts = _inproj(xin, w_z, (exp_w1, exp_w3, exp_w2), l)
        acts_p, nconv, ngla = _mix_prompt(z, alr, a2, ab, conv_w, gng, cmg, ws_p, cmb_p, l)
        sc = state_conv[l]
        p2 = jnp.pad(sc, ((0, 0), (0, DEC_SEQ - 2), (0, 0))).reshape(T_SAMPLE, CONV_CH)
        p1 = jnp.pad(sc[:, 1:2], ((0, 0), (0, DEC_SEQ - 1), (0, 0))).reshape(T_SAMPLE, CONV_CH)
        acts_s, cin_s, gla_s, vrows = _mix_sample(z, alr, a2, ab, conv_w, gng, cmg, ws_s, cmb_s,
                                                  p1, p2, state_gla, gla_s, l)
        conv_p.append(nconv)
        gla_p.append(ngla)
        conv_s.append(cin_s.reshape(DEC_BATCH, DEC_SEQ, CONV_CH)[:, DEC_SEQ - (CONV_K - 1):])
        cmv_s.append(vrows.reshape(DEC_BATCH, DEC_SEQ, CM_CH))

        x, xn, route, seg, cnt = _outproj(acts_p, acts_s, z, xa, xb, pa, pb, pc, wo, n2, w_r, r_b,
                                          l)
        plan = _plan(cnt)
        n_used = plan[PLAN_USED, :1]
        grp_base = plan[PLAN_BASE, :MOE_GROUPS]
        xs = _scatter(seg, route, xn, n_used, plan[PLAN_END, :MOE_GROUPS],
                      plan[PLAN_BLOCKS, :MOE_GROUPS], grp_base)
        ys = _ffn(plan[PLAN_GROUP, :N_BLK], n_used, xs, *experts)
        if l == DEPTH - 1:
            out = _combine(seg, route, grp_base, ys, x, fg, w_alr[l], True)
        else:
            xa, xin, alr = _combine(seg, route, grp_base, ys, x, n1[l + 1], w_alr[l + 1], False)
            xb = xa

    y_prompt = out[0].reshape(BATCH, SEQ, D_MODEL)
    y_sample = out[1].reshape(DEC_BATCH, DEC_SEQ, D_MODEL)
    return (y_prompt, y_sample, jnp.stack(conv_p), jnp.stack(gla_p), jnp.stack(conv_s),
            gla_s, jnp.stack(cmv_s))
```

```python
import functools

import jax
import jax.numpy as jnp
from jax import lax
from jax.experimental import pallas as pl
from jax.experimental.pallas import tpu as pltpu

F32 = jnp.float32
BF16 = jnp.bfloat16

D_MODEL = 1024
BATCH = 8
SEQ = 2048
DEPTH = 2
DEC_BATCH = 128
DEC_SEQ = 8
CONV_K = 3
CONV_CH = 512
GLA_HEADS = 4
GLA_DK = 128
GLA_DV = 256
GLA_QK = GLA_HEADS * GLA_DK
GLA_V = GLA_HEADS * GLA_DV
GLA_LOWRANK = 16
GLA_TAU = 16.0
GLA_CHUNK = 64
CM_GROUPS = 4
CM_CHUNK = 128
CM_GCH = 128
CM_CH = 512
MOE_GROUPS = 8
EXPERTS_PER_GROUP = 8
N_EXPERTS = 64
D_EXPERT = 256
EPS = 1e-6

LANES = 128
T_PROMPT = BATCH * SEQ
T_SAMPLE = DEC_BATCH * DEC_SEQ
T_ALL = T_PROMPT + T_SAMPLE

COL_GA, COL_GB, COL_GC = 0, 1024, 2048
COL_V, COL_R, COL_Q, COL_K = 3072, 4096, 5120, 5632
COL_H, COL_CG, COL_BG, COL_U, COL_VV = 6144, 6656, 7168, 7680, 8192
Z_COLS = 8704
ACT_COLS = 2048
ROUTE_ROWS = 2 * EXPERTS_PER_GROUP
ROUTE_LRANK = EXPERTS_PER_GROUP

TM_PRE = 1024
TM_IN = 512
TN_IN = Z_COLS // 2

W_IN_WIDTHS = (("h", 512), ("cg", 512), ("bg", 512), ("q", 512), ("k", 512), ("v", 1024),
               ("r", 1024), ("alr", GLA_LOWRANK), ("u", 512), ("vv", 512), ("ga", 1024),
               ("gb", 1024), ("gc", 1024))
Z_ORDER = ("ga", "gb", "gc", "v", "r", "q", "k", "h", "cg", "bg", "u", "vv")


def _w_in_offsets():
    off, o = {}, 0
    for name, n in W_IN_WIDTHS:
        off[name] = (o, n)
        o += n
    return off


def _w_in_copies():
    off = _w_in_offsets()
    copies = [[] for _ in range(Z_COLS // TN_IN)]
    dst = 0
    for name in Z_ORDER:
        src, n = off[name]
        while n:
            tile = dst // TN_IN
            at = dst - tile * TN_IN
            take = min(n, TN_IN - at)
            runs = copies[tile]
            if runs and runs[-1][0] + runs[-1][2] == src and runs[-1][1] + runs[-1][2] == at:
                runs[-1] = (runs[-1][0], runs[-1][1], runs[-1][2] + take)
            else:
                runs.append((src, at, take))
            src, dst, n = src + take, dst + take, n - take
    assert dst == Z_COLS
    assert all(v % 16 == 0 for runs in copies for run in runs for v in run)
    return copies
TC_MIX = 256
SEQ_PER_BLK = 16
ROWS_S = SEQ_PER_BLK * DEC_SEQ
TM_OUT = 256
ROUTE_DELAY = 1
TM_ROW = TM_OUT
ROW_TILES = 4
GATHER_AHEAD = 2
COPY_STAGE_GROUPS = 4
MOE_BLK = 256
N_BLK = T_ALL // MOE_BLK + MOE_GROUPS
N_SORTED = N_BLK * MOE_BLK
VMEM_LIMIT = 56 * 1024 * 1024


def _sigmoid(x):
    return 0.5 * jnp.tanh(0.5 * x) + 0.5


def _gelu_tanh(x):
    c = 0.7978845608028654
    half = 0.5 * x
    return half + half * jnp.tanh(x * (c + (c * 0.044715) * (x * x)))


def _log_sigmoid(x):
    log2_e = 1.4426950408889634
    ln_2 = 0.6931471805599453
    return jnp.minimum(x, 0.0) - ln_2 * jnp.log2(1.0 + jnp.exp2(-log2_e * jnp.abs(x)))


def _rms(x, g):
    ms = jnp.mean(x * x, axis=-1, keepdims=True)
    return x * lax.rsqrt(ms + EPS) * g


def _split_bf16(x):
    hi = x.astype(BF16)
    lo = (x - hi.astype(F32)).astype(BF16)
    return hi, lo


def _dot(a, b):
    return jnp.dot(a, b, preferred_element_type=F32)


def _dot_nt(a, b):
    return lax.dot_general(a, b, (((1,), (1,)), ((), ())), preferred_element_type=F32)


def _layer_spec(arr, layer):
    nd = arr.ndim - 1
    return pl.BlockSpec((None,) + arr.shape[1:], lambda *g: (layer,) + (0,) * nd)


def _two_part_specs(tile, xa, xb):
    n_a = xa.shape[0] // tile
    n_b = xb.shape[0] // tile
    return n_a, [
        pl.BlockSpec((tile, xa.shape[1]), lambda i, *_: (jnp.minimum(i, n_a - 1), 0)),
        pl.BlockSpec((tile, xb.shape[1]), lambda i, *_: (jnp.clip(i - n_a, 0, n_b - 1), 0)),
    ]


def _norm_and_lowrank(x, g_ref, wa_ref, xn_ref, a_ref):
    xn = _rms(x, g_ref[...]).astype(BF16)
    xn_ref[...] = xn
    a_ref[...] = _dot_nt(xn, wa_ref[...]).astype(BF16)


def _prenorm_kernel(xa_ref, xb_ref, g_ref, wa_ref, xn_ref, a_ref, *, n_a):
    x = jnp.where(pl.program_id(0) < n_a, xa_ref[...], xb_ref[...])
    _norm_and_lowrank(x, g_ref, wa_ref, xn_ref, a_ref)


def _prenorm(xa, xb, g, wa, layer):
    m = T_ALL
    n_a, x_specs = _two_part_specs(TM_PRE, xa, xb)
    return pl.pallas_call(
        functools.partial(_prenorm_kernel, n_a=n_a),
        grid=(m // TM_PRE,),
        in_specs=x_specs + [_layer_spec(g, layer), _layer_spec(wa, layer)],
        out_specs=[pl.BlockSpec((TM_PRE, D_MODEL), lambda i: (i, 0)),
                   pl.BlockSpec((TM_PRE, LANES), lambda i: (i, 0))],
        out_shape=[jax.ShapeDtypeStruct((m, D_MODEL), BF16),
                   jax.ShapeDtypeStruct((m, LANES), BF16)],
        compiler_params=pltpu.CompilerParams(
            dimension_semantics=("arbitrary",), vmem_limit_bytes=VMEM_LIMIT),
        name="prenorm",
    )(xa, xb, g, wa)


EXPERT_SHAPES = ((D_MODEL, D_EXPERT), (D_MODEL, D_EXPERT), (D_EXPERT, D_MODEL))


def _inproj_kernel(xn_ref, w_hbm, *refs, layer):
    n = len(EXPERT_SHAPES)
    z_ref, w_buf, sem = refs[n], refs[-2], refs[-1]
    j, i = pl.program_id(0), pl.program_id(1)

    def tile_copies(tile):
        return [pltpu.make_async_copy(w_hbm.at[layer, pl.ds(src, rows)],
                                      w_buf.at[tile, pl.ds(at, rows)], sem.at[tile])
                for src, at, rows in _w_in_copies()[tile]]

    @pl.when((j == 0) & (i == 0))
    def _():
        for tile in range(w_buf.shape[0]):
            for cp in tile_copies(tile):
                cp.start()

    for tile in range(w_buf.shape[0]):
        @pl.when((j == tile) & (i == 0))
        def _():
            for cp in tile_copies(tile):
                cp.wait()

    for src, dst in zip(refs[:n], refs[n + 1:]):
        dst[...] = src[...].astype(BF16)
    z_ref[...] = _dot_nt(xn_ref[...], w_buf[j]).astype(BF16)


def _inproj(xn, w, experts, layer):
    m = xn.shape[0]
    n_j, n_i = Z_COLS // TN_IN, m // TM_IN
    assert n_j * n_i >= N_EXPERTS
    expert = lambda j, i: jnp.minimum(j * n_i + i, N_EXPERTS - 1)
    return pl.pallas_call(
        functools.partial(_inproj_kernel, layer=layer),
        grid=(n_j, n_i),
        in_specs=[
            pl.BlockSpec((TM_IN, D_MODEL), lambda j, i: (i, 0)),
            pl.BlockSpec(memory_space=pl.ANY),
        ] + [pl.BlockSpec((None, None) + s, lambda j, i: (layer, expert(j, i), 0, 0))
             for s in EXPERT_SHAPES],
        out_specs=[pl.BlockSpec((TM_IN, TN_IN), lambda j, i: (i, j))]
        + [pl.BlockSpec((None,) + s, lambda j, i: (expert(j, i), 0, 0)) for s in EXPERT_SHAPES],
        out_shape=[jax.ShapeDtypeStruct((m, Z_COLS), BF16)]
        + [jax.ShapeDtypeStruct((N_EXPERTS,) + s, BF16) for s in EXPERT_SHAPES],
        scratch_shapes=[pltpu.VMEM((n_j, TN_IN, D_MODEL), BF16), pltpu.SemaphoreType.DMA((n_j,))],
        compiler_params=pltpu.CompilerParams(
            dimension_semantics=("arbitrary", "arbitrary"), vmem_limit_bytes=VMEM_LIMIT),
        name="inproj",
    )(xn, w, *experts)


def _gla_log_decay(alr_ref, a2_ref, ab_ref):
    la = _log_sigmoid(_dot(alr_ref[...], a2_ref[...]) + ab_ref[...]) * (1.0 / GLA_TAU)
    return _split_bf16(la)


def _masked_sum(mask, la_hi, la_lo):
    m = jnp.where(mask, 1.0, 0.0).astype(BF16)
    return _dot(m, la_hi) + _dot(m, la_lo)


def _gla_decay_terms(alr_ref, a2_ref, ab_ref, tril_mask, same_mask):
    la_hi, la_lo = _gla_log_decay(alr_ref, a2_ref, ab_ref)
    return (la_hi, la_lo, _masked_sum(tril_mask, la_hi, la_lo),
            _masked_sum(same_mask, la_hi, la_lo))


def _gla_out_gate(o, g_ref, r):
    return _rms(o, g_ref[...]) * (r * _sigmoid(r))


def _chunk_mlp_group(g, bgu_ref, vv_ref, cmg_ref, ws_ref, cmb_ref, n_chunks):
    sl = slice(g * CM_GCH, (g + 1) * CM_GCH)
    ug = _gelu_tanh(bgu_ref[:, CONV_CH + g * CM_GCH:CONV_CH + (g + 1) * CM_GCH])
    vg = _rms(_gelu_tanh(vv_ref[:, sl]).astype(F32), cmg_ref[:, sl])
    vgb = vg.astype(BF16)
    rows = []
    for j in range(n_chunks):
        rs = slice(j * CM_CHUNK, (j + 1) * CM_CHUNK)
        rows.append(_dot(ws_ref[g], vgb[rs]) + cmb_ref[:, sl])
    s = rows[0] if n_chunks == 1 else jnp.concatenate(rows, axis=0)
    return ug * s, vg


def _chunk_mlp(bgu_ref, vv_ref, cmg_ref, ws_ref, cmb_ref, n_chunks):
    parts = [_chunk_mlp_group(g, bgu_ref, vv_ref, cmg_ref, ws_ref, cmb_ref, n_chunks)
             for g in range(CM_GROUPS)]
    return (jnp.concatenate([p[0] for p in parts], axis=1),
            jnp.concatenate([p[1] for p in parts], axis=1))


N_Z_VIEWS = 7
N_MIX_W = 7
SEQ_PER_STEP = 4
SEQ_STAGGER = 3


def _mix_prompt_kernel(*refs):
    n_z = N_Z_VIEWS * SEQ_PER_STEP
    weights = refs[n_z:n_z + N_MIX_W]
    acts_ref, nconv_ref, ngla_ref, st_ref, carry_ref = refs[n_z + N_MIX_W:]

    @pl.when(pl.program_id(1) == 0)
    def _():
        st_ref[...] = jnp.zeros_like(st_ref)
        carry_ref[...] = jnp.zeros_like(carry_ref)

    rr = lax.broadcasted_iota(jnp.int32, (TC_MIX, TC_MIX), 0)
    cc = lax.broadcasted_iota(jnp.int32, (TC_MIX, TC_MIX), 1)
    tril = ((rr >> 6) == (cc >> 6)) & (cc <= rr)
    masks = (tril, jnp.where(tril, 1.0, 0.0).astype(BF16),
             jnp.where(rr - cc == 1, 1.0, 0.0).astype(BF16),
             jnp.where(rr - cc == 2, 1.0, 0.0).astype(BF16))

    _interleave(*[
        _delayed(_mix_prompt_seq(*refs[N_Z_VIEWS * s:N_Z_VIEWS * (s + 1)], *weights,
                                acts_ref.at[s], nconv_ref.at[s], st_ref.at[s], carry_ref.at[s],
                                masks),
                s * SEQ_STAGGER)
        for s in range(SEQ_PER_STEP)])

    @pl.when(pl.program_id(1) == pl.num_programs(1) - 1)
    def _():
        for s in range(SEQ_PER_STEP):
            for hd in range(GLA_HEADS):
                ngla_ref[s, 0, hd] = st_ref[s, hd].T


def _mix_prompt_seq(v_ref, r_ref, qk_ref, hcg_ref, bgu_ref, vv_ref, alr_ref,
                    a2_ref, ab_ref, cw_ref, gng_ref, cmg_ref, ws_ref, cmb_ref,
                    acts_ref, nconv_ref, st_ref, carry_ref, masks):
    tc = TC_MIX

    h = hcg_ref[:, :CONV_CH].astype(F32)
    cg = hcg_ref[:, CONV_CH:].astype(F32)
    bg = bgu_ref[:, :CONV_CH].astype(F32)
    cin = cg * h
    tril, tril_01, shift_1, shift_2 = masks
    cin_b = cin.astype(BF16)
    x1 = _dot(shift_1, cin_b)
    x2 = _dot(shift_2, cin_b)
    conv = x2 * cw_ref[0:1, :] + x1 * cw_ref[1:2, :] + cin * cw_ref[2:3, :]
    c0 = carry_ref[0:1, :]
    c1 = carry_ref[1:2, :]
    row8 = lax.broadcasted_iota(jnp.int32, (8, 1), 0)
    head = jnp.where(row8 == 0, c0 * cw_ref[0:1, :] + c1 * cw_ref[1:2, :],
                     jnp.where(row8 == 1, c1 * cw_ref[0:1, :], 0.0))
    conv = jnp.concatenate([conv[0:8] + head, conv[8:]], axis=0)
    acts_ref[:, 0:CONV_CH] = (bg * conv).astype(BF16)
    carry_ref[0:2, :] = cin[tc - 2:tc, :]
    nconv_ref[0] = cin[tc - 2:tc, :]
    yield

    la_hi, la_lo = _gla_log_decay(alr_ref, a2_ref, ab_ref)
    b = _dot(tril_01, la_hi) + _dot(tril_01, la_lo)
    yield
    n_chunks = tc // GLA_CHUNK
    b_last = [b[(c + 1) * GLA_CHUNK - 1:(c + 1) * GLA_CHUNK, :] for c in range(n_chunks)]
    bl = jnp.concatenate([jnp.broadcast_to(r_, (GLA_CHUNK, GLA_QK)) for r_ in b_last], axis=0)
    q = qk_ref[:, :GLA_QK].astype(F32) * (GLA_DK ** -0.5)
    k = qk_ref[:, GLA_QK:].astype(F32)
    q_t = (q * jnp.exp(b)).astype(BF16)
    k_t = (k * jnp.exp(-b)).astype(BF16)
    k_end = (k * jnp.exp(bl - b)).astype(BF16)
    yield
    states = [st_ref[hd] for hd in range(GLA_HEADS)]
    k_cols = [slice(hd * GLA_DK, (hd + 1) * GLA_DK) for hd in range(GLA_HEADS)]
    v_cols = [slice(hd * GLA_DV, (hd + 1) * GLA_DV) for hd in range(GLA_HEADS)]
    o_intra = []
    for hd in range(GLA_HEADS):
        att = jnp.where(tril, _dot_nt(q_t[:, k_cols[hd]], k_t[:, k_cols[hd]]), 0.0).astype(BF16)
        o_intra.append(_dot(att, v_ref[:, v_cols[hd]]))
    yield

    o_rows = [[] for _ in range(GLA_HEADS)]
    assert n_chunks == CM_GROUPS
    for c in range(n_chunks):
        rs = slice(c * GLA_CHUNK, (c + 1) * GLA_CHUNK)
        for hd in range(GLA_HEADS):
            ks = k_cols[hd]
            st = states[hd]
            o_rows[hd].append(o_intra[hd][rs] + _dot_nt(q_t[rs, ks], st.astype(BF16)))
            upd = lax.dot_general(v_ref[rs, v_cols[hd]], k_end[rs, ks], (((0,), (0,)), ((), ())),
                                  preferred_element_type=F32)
            states[hd] = jnp.exp(b_last[c][:, ks]) * st + upd
        us, _ = _chunk_mlp_group(c, bgu_ref, vv_ref, cmg_ref, ws_ref, cmb_ref, tc // CM_CHUNK)
        acts_ref[:, CONV_CH + GLA_V + c * CM_GCH:CONV_CH + GLA_V + (c + 1) * CM_GCH] = (
            us.astype(BF16))
        yield

    for hd in range(GLA_HEADS):
        o = jnp.concatenate(o_rows[hd], axis=0)
        r = r_ref[:, v_cols[hd]]
        acts_ref[:, CONV_CH + hd * GLA_DV:CONV_CH + (hd + 1) * GLA_DV] = (
            _gla_out_gate(o, gng_ref, r).astype(BF16))
        st_ref[hd] = states[hd]
        if hd % 2 == 1:
            yield


def _z_specs(rows, row_map):
    def spec(width, col):
        blk = col // width
        return pl.BlockSpec((rows, width), lambda *g: (row_map(*g), blk))
    return [spec(1024, COL_V), spec(1024, COL_R), spec(1024, COL_Q), spec(1024, COL_H),
            spec(1024, COL_BG), spec(512, COL_VV)]


def _mix_prompt(z, alr, a2, ab, cw, gng, cmg, ws, cmb, layer):
    nt = SEQ // TC_MIX
    nb = BATCH // SEQ_PER_STEP
    small = (a2, ab, cw, gng, cmg, ws, cmb)
    assert len(small) == N_MIX_W
    in_specs, args = [], []
    for s in range(SEQ_PER_STEP):
        row_map = lambda b, c, s=s: (b + s * nb) * nt + c
        in_specs += _z_specs(TC_MIX, row_map) + [
            pl.BlockSpec((TC_MIX, LANES), lambda b, c, row_map=row_map: (row_map(b, c), 0))]
        args += [z] * (N_Z_VIEWS - 1) + [alr]
    in_specs += [_layer_spec(a, layer) for a in small]
    acts, nconv, ngla = pl.pallas_call(
        _mix_prompt_kernel,
        grid=(nb, nt),
        in_specs=in_specs,
        out_specs=[
            pl.BlockSpec((SEQ_PER_STEP, TC_MIX, ACT_COLS), lambda b, c: (0, b * nt + c, 0)),
            pl.BlockSpec((SEQ_PER_STEP, 1, CONV_K - 1, CONV_CH), lambda b, c: (0, b, 0, 0)),
            pl.BlockSpec((SEQ_PER_STEP, 1, GLA_HEADS, GLA_DK, GLA_DV),
                         lambda b, c: (0, b, 0, 0, 0)),
        ],
        out_shape=[
            jax.ShapeDtypeStruct((SEQ_PER_STEP, T_PROMPT // SEQ_PER_STEP, ACT_COLS), BF16),
            jax.ShapeDtypeStruct((SEQ_PER_STEP, nb, CONV_K - 1, CONV_CH), F32),
            jax.ShapeDtypeStruct((SEQ_PER_STEP, nb, GLA_HEADS, GLA_DK, GLA_DV), F32),
        ],
        scratch_shapes=[pltpu.VMEM((SEQ_PER_STEP, GLA_HEADS, GLA_DV, GLA_DK), F32),
                        pltpu.VMEM((SEQ_PER_STEP, 8, CONV_CH), F32)],
        compiler_params=pltpu.CompilerParams(
            dimension_semantics=("arbitrary", "arbitrary"), vmem_limit_bytes=VMEM_LIMIT),
        name="mix_prompt",
    )(*args, *small)
    return (acts.reshape(T_PROMPT, ACT_COLS), nconv.reshape(BATCH, CONV_K - 1, CONV_CH),
            ngla.reshape(BATCH, GLA_HEADS, GLA_DK, GLA_DV))


def _mix_sample_body(v_ref, r_ref, qk_ref, hcg_ref, bgu_ref, vv_ref, alr_ref,
                       a2_ref, ab_ref, cw_ref, gng_ref, cmg_ref, ws_ref, cmb_ref,
                       p1_ref, p2_ref, s0_ref,
                       acts_ref, cin_ref, ns_ref, vrow_ref):
    n = ROWS_S

    pos = lax.broadcasted_iota(jnp.int32, (n, 1), 0) & (DEC_SEQ - 1)
    h = hcg_ref[:, :CONV_CH].astype(F32)
    cg = hcg_ref[:, CONV_CH:].astype(F32)
    bg = bgu_ref[:, :CONV_CH].astype(F32)
    cin = cg * h
    x1 = jnp.where(pos >= 1, pltpu.roll(cin, 1, 0), p1_ref[...])
    x2 = jnp.where(pos >= 2, pltpu.roll(cin, 2, 0), p2_ref[...])
    conv = x2 * cw_ref[0:1, :] + x1 * cw_ref[1:2, :] + cin * cw_ref[2:3, :]
    acts_ref[:, 0:CONV_CH] = (bg * conv).astype(BF16)
    cin_ref[...] = cin

    rr = lax.broadcasted_iota(jnp.int32, (n, n), 0)
    cc = lax.broadcasted_iota(jnp.int32, (n, n), 1)
    same = (rr >> 3) == (cc >> 3)
    tril = same & (cc <= rr)
    la_hi, la_lo, b, bl = _gla_decay_terms(alr_ref, a2_ref, ab_ref, tril, same)
    q = qk_ref[:, :GLA_QK].astype(F32) * (GLA_DK ** -0.5)
    k = qk_ref[:, GLA_QK:].astype(F32)
    q_t = (q * jnp.exp(b)).astype(BF16)
    k_t = (k * jnp.exp(-b)).astype(BF16)
    k_end = k * jnp.exp(bl - b)
    la_hi = la_hi.astype(F32)
    la_lo = la_lo.astype(F32)
    row_seq = lax.broadcasted_iota(jnp.int32, (n, GLA_DK), 0) >> 3
    seq3 = lax.broadcasted_iota(jnp.int32, (SEQ_PER_BLK, GLA_DK, n), 0)
    lane_seq3 = lax.broadcasted_iota(jnp.int32, (SEQ_PER_BLK, GLA_DK, n), 2) >> 3
    mask3 = seq3 == lane_seq3
    ones = jnp.ones((n, GLA_DV), BF16)
    big = SEQ_PER_BLK * GLA_DK

    def per_seq(x_tr):
        x3 = jnp.where(mask3, x_tr[None, :, :], 0.0)
        return x3.reshape(big, n).astype(BF16)

    for hd in range(GLA_HEADS):
        ks = slice(hd * GLA_DK, (hd + 1) * GLA_DK)
        vs = slice(hd * GLA_DV, (hd + 1) * GLA_DV)
        qh = q_t[:, ks]
        vh = v_ref[:, vs]
        att = jnp.where(tril, _dot_nt(qh, k_t[:, ks]), 0.0).astype(BF16)
        o_intra = _dot(att, vh)
        s_old = s0_ref[:, hd].reshape(big, GLA_DV)
        zero = jnp.zeros_like(qh)
        q_big = jnp.concatenate(
            [jnp.where(row_seq == j, qh, zero) for j in range(SEQ_PER_BLK)], axis=1)
        o = o_intra + _dot(q_big, s_old.astype(BF16))
        dlog = _dot(per_seq(la_hi[:, ks].T), ones) + _dot(per_seq(la_lo[:, ks].T), ones)
        upd = _dot(per_seq(k_end[:, ks].T), vh)
        s_new = jnp.exp(dlog) * s_old + upd
        ns_ref[:, hd] = s_new.reshape(SEQ_PER_BLK, GLA_DK, GLA_DV)
        r = r_ref[:, vs]
        acts_ref[:, CONV_CH + hd * GLA_DV:CONV_CH + (hd + 1) * GLA_DV] = (
            _gla_out_gate(o, gng_ref, r).astype(BF16))

    us, vg = _chunk_mlp(bgu_ref, vv_ref, cmg_ref, ws_ref, cmb_ref, 1)
    acts_ref[:, CONV_CH + GLA_V:] = us.astype(BF16)
    vrow_ref[...] = vg


N_MIX_S_IN = 17


def _mix_sample_kernel(*refs, layer):
    if layer == 0:
        @pl.when(pl.program_id(0) == 0)
        def _():
            _mix_sample_body(*refs)

        @pl.when(pl.program_id(0) > 0)
        def _():
            ns_ref = refs[N_MIX_S_IN + 2]
            ns_ref[...] = jnp.zeros_like(ns_ref)
    else:
        _mix_sample_body(*refs[:N_MIX_S_IN], *refs[N_MIX_S_IN + 1:])


def _mix_sample(z, alr, a2, ab, cw, gng, cmg, ws, cmb, p1, p2, s0, ns_all, layer):
    row0 = T_PROMPT // ROWS_S
    n_i = DEC_BATCH // SEQ_PER_BLK
    n_pass = DEPTH if layer == 0 else 1
    blk = lambda p, i: jnp.where(p == 0, i, n_i - 1)
    row_map = lambda p, i: row0 + blk(p, i)
    slot = lambda p, i: (layer + p, i, 0, 0, 0)
    small = (a2, ab, cw, gng, cmg, ws, cmb)
    state_blk = (None, SEQ_PER_BLK, GLA_HEADS, GLA_DK, GLA_DV)
    in_specs = _z_specs(ROWS_S, row_map) + [
        pl.BlockSpec((ROWS_S, LANES), lambda p, i: (row_map(p, i), 0)),
    ] + [_layer_spec(a, layer) for a in small] + [
        pl.BlockSpec((ROWS_S, CONV_CH), lambda p, i: (blk(p, i), 0)),
        pl.BlockSpec((ROWS_S, CONV_CH), lambda p, i: (blk(p, i), 0)),
        pl.BlockSpec(state_blk, lambda p, i: (layer, blk(p, i), 0, 0, 0)),
    ]
    args = (z, z, z, z, z, z, alr, *small, p1, p2, s0)
    assert len(args) == N_MIX_S_IN
    aliases = {}
    if layer > 0:
        in_specs.append(pl.BlockSpec(memory_space=pl.ANY))
        args += (ns_all,)
        aliases = {N_MIX_S_IN: 2}
    return pl.pallas_call(
        functools.partial(_mix_sample_kernel, layer=layer),
        grid=(n_pass, n_i),
        in_specs=in_specs,
        out_specs=[
            pl.BlockSpec((ROWS_S, ACT_COLS), lambda p, i: (blk(p, i), 0)),
            pl.BlockSpec((ROWS_S, CONV_CH), lambda p, i: (blk(p, i), 0)),
            pl.BlockSpec(state_blk, slot),
            pl.BlockSpec((ROWS_S, CM_CH), lambda p, i: (blk(p, i), 0)),
        ],
        out_shape=[
            jax.ShapeDtypeStruct((T_SAMPLE, ACT_COLS), BF16),
            jax.ShapeDtypeStruct((T_SAMPLE, CONV_CH), F32),
            jax.ShapeDtypeStruct((DEPTH, DEC_BATCH, GLA_HEADS, GLA_DK, GLA_DV), F32),
            jax.ShapeDtypeStruct((T_SAMPLE, CM_CH), F32),
        ],
        input_output_aliases=aliases,
        compiler_params=pltpu.CompilerParams(
            dimension_semantics=("arbitrary", "arbitrary"), vmem_limit_bytes=VMEM_LIMIT),
        name="mix_sample",
    )(*args)


def _delayed(gen, n_stages):
    for _ in range(n_stages):
        yield
    yield from gen


def _interleave(*stage_lists):
    pending = list(stage_lists)
    while pending:
        for gen in list(pending):
            if next(gen, StopIteration) is StopIteration:
                pending.remove(gen)


def _outproj_main(i, n_xa, actp_ref, acts_ref, gates_ref, xa_ref, xb_ref, pa_ref, pb_ref, pc_ref,
                  wo_ref, n2_ref, xo_ref, xn_ref, xnb_ref):
    acts = jnp.where(i < T_PROMPT // TM_OUT, actp_ref[...], acts_ref[...])
    x_in = jnp.where(i < n_xa, xa_ref[...], xb_ref[...])
    yb = _dot(acts[:, CONV_CH:CONV_CH + GLA_V], pb_ref[...])
    ga = _sigmoid(gates_ref[:, COL_GA:COL_GA + D_MODEL])
    ya = _dot(acts[:, :CONV_CH], pa_ref[...])
    gb = _sigmoid(gates_ref[:, COL_GB:COL_GB + D_MODEL])
    yield
    yc = _dot(acts[:, CONV_CH + GLA_V:], pc_ref[...])
    gc = _sigmoid(gates_ref[:, COL_GC:COL_GC + D_MODEL])
    mix = ga * ya + gb * yb
    yield
    mix = mix + gc * yc
    x = x_in + _dot(mix.astype(BF16), wo_ref[...])
    xo_ref[...] = x
    yield
    xn = _rms(x, n2_ref[...])
    xn_ref[...] = xn
    xnb = xn.astype(BF16)
    yield
    xnb_ref[...] = xnb


def _outproj_route(live, xnb_ref, wr_ref, rb_ref, route_ref, seg_ref, cnt_ref):
    tm = TM_OUT
    logits = _dot_nt(wr_ref[...], xnb_ref[...]) + rb_ref[...]
    yield
    n_sub = EXPERTS_PER_GROUP
    sub = lax.broadcasted_iota(jnp.int32, (n_sub, tm), 0).astype(F32)
    neg = jnp.float32(-jnp.inf)
    lg = logits[0:MOE_GROUPS]
    gmax = jnp.max(lg, axis=0, keepdims=True)
    grp = jnp.min(jnp.where(lg == gmax, sub, 1e9), axis=0, keepdims=True)
    p_grp = 1.0 / jnp.sum(jnp.exp(lg - gmax), axis=0, keepdims=True)
    yield
    le = logits[MOE_GROUPS:MOE_GROUPS + n_sub]
    for g in range(1, MOE_GROUPS):
        le = jnp.where(grp == g, logits[MOE_GROUPS + g * n_sub:MOE_GROUPS + (g + 1) * n_sub], le)
    v1 = jnp.max(le, axis=0, keepdims=True)
    i1 = jnp.min(jnp.where(le == v1, sub, 1e9), axis=0, keepdims=True)
    le2 = jnp.where(sub == i1, neg, le)
    v2 = jnp.max(le2, axis=0, keepdims=True)
    i2 = jnp.min(jnp.where(le2 == v2, sub, 1e9), axis=0, keepdims=True)
    yield
    t = jnp.exp(v2 - v1)
    g1 = p_grp / (1.0 + t)
    g2 = p_grp * t / (1.0 + t)
    gate_t = jnp.where(sub == i1, g1, jnp.where(sub == i2, g2, 0.0))

    onehot_t = jnp.where(sub == grp, 1.0, 0.0)
    csum = jnp.where(live, jnp.sum(onehot_t, axis=1, keepdims=True), 0.0)
    rr = lax.broadcasted_iota(jnp.int32, (tm, tm), 0)
    cc = lax.broadcasted_iota(jnp.int32, (tm, tm), 1)
    earlier = jnp.where(rr < cc, 1.0, 0.0).astype(BF16)
    padded = jnp.concatenate([onehot_t, jnp.zeros_like(onehot_t)], axis=0).astype(BF16)
    same_before = _dot(padded, earlier)[0:n_sub]
    yield
    lower = jnp.sum(jnp.where(sub < grp, csum, 0.0), axis=0, keepdims=True)
    lrank = lower + jnp.sum(onehot_t * same_before, axis=0, keepdims=True)
    route_ref[0:n_sub, :] = gate_t
    route_ref[n_sub:, :] = jnp.broadcast_to(lrank, (n_sub, tm))
    carry = cnt_ref[...]
    lane = lax.broadcasted_iota(jnp.int32, (n_sub, LANES), 1)
    seg_ref[...] = jnp.where(lane == SEG_START, carry,
                             jnp.where(lane == SEG_LEN, csum, 0.0)).astype(jnp.int32)
    cnt_ref[...] = carry + csum


def _outproj_kernel(actp_ref, acts_ref, gates_ref, xa_ref, xb_ref, pa_ref, pb_ref, pc_ref, wo_ref,
                    n2_ref, wr_ref, rb_ref, xo_ref, xn_ref, route_ref, seg_ref, cnt_ref, xnb_ref,
                    *, n_xa):
    i = pl.program_id(0)

    @pl.when(i == 0)
    def _():
        cnt_ref[...] = jnp.zeros_like(cnt_ref)
        xnb_ref[...] = jnp.zeros_like(xnb_ref)

    tile = jnp.minimum(i, pl.num_programs(0) - 2)
    _interleave(
        _delayed(_outproj_route(i >= 1, xnb_ref, wr_ref, rb_ref, route_ref, seg_ref, cnt_ref),
                 ROUTE_DELAY),
        _outproj_main(tile, n_xa, actp_ref, acts_ref, gates_ref, xa_ref, xb_ref, pa_ref, pb_ref,
                      pc_ref, wo_ref, n2_ref, xo_ref, xn_ref, xnb_ref))


def _lane_prefix(v):
    rr = lax.broadcasted_iota(jnp.int32, (LANES, LANES), 0)
    cc = lax.broadcasted_iota(jnp.int32, (LANES, LANES), 1)
    earlier = jnp.where(rr < cc, 1.0, 0.0).astype(BF16)
    return _dot(jnp.broadcast_to(v, (8, LANES)).astype(BF16), earlier)[0:1]


def _outproj(acts_p, acts_s, z, xa, xb, pa, pb, pc, wo, n2, wr, rb, layer):
    m = T_ALL
    nt = m // TM_OUT
    row = lambda i: (jnp.minimum(i, nt - 1), 0)
    routed = lambda i: (jnp.maximum(i - 1, 0), 0, 0)
    _, act_specs = _two_part_specs(TM_OUT, acts_p, acts_s)
    n_xa, x_specs = _two_part_specs(TM_OUT, xa, xb)
    weights = (pa, pb, pc, wo, n2, wr, rb)
    return pl.pallas_call(
        functools.partial(_outproj_kernel, n_xa=n_xa),
        grid=(nt + 1,),
        in_specs=act_specs + [pl.BlockSpec((TM_OUT, 3 * D_MODEL), row)] + x_specs
        + [_layer_spec(w, layer) for w in weights],
        out_specs=[
            pl.BlockSpec((TM_OUT, D_MODEL), row),
            pl.BlockSpec((TM_OUT, D_MODEL), row),
            pl.BlockSpec((None, ROUTE_ROWS, TM_OUT), routed),
            pl.BlockSpec((None, 8, LANES), routed),
            pl.BlockSpec((8, LANES), lambda i: (0, 0)),
        ],
        scratch_shapes=[pltpu.VMEM((TM_OUT, D_MODEL), BF16)],
        out_shape=[
            jax.ShapeDtypeStruct((m, D_MODEL), F32),
            jax.ShapeDtypeStruct((m, D_MODEL), F32),
            jax.ShapeDtypeStruct((m // TM_OUT, ROUTE_ROWS, TM_OUT), F32),
            jax.ShapeDtypeStruct((m // TM_OUT, 8, LANES), jnp.int32),
            jax.ShapeDtypeStruct((8, LANES), F32),
        ],
        compiler_params=pltpu.CompilerParams(
            dimension_semantics=("arbitrary",), vmem_limit_bytes=VMEM_LIMIT),
        name="outproj",
    )(acts_p, acts_s, z, xa, xb, *weights)


PLAN_GROUP, PLAN_USED, PLAN_END, PLAN_BLOCKS, PLAN_BASE = 0, 1, 2, 3, 4


def _plan_kernel(cnt_ref, plan_ref):
    sub_i = lax.broadcasted_iota(jnp.int32, (8, LANES), 0)
    lane8_i = lax.broadcasted_iota(jnp.int32, (8, LANES), 1)
    cnt = jnp.sum(jnp.where(sub_i == lane8_i, cnt_ref[...], 0.0), axis=0, keepdims=True)
    blocks = jnp.floor((cnt + (MOE_BLK - 1)) * (1.0 / MOE_BLK))
    start = _lane_prefix(blocks)
    end = start + blocks
    lane_i = lax.broadcasted_iota(jnp.int32, (1, LANES), 1)
    lane = lane_i.astype(F32)
    grp_of_blk = jnp.zeros((1, LANES), F32)
    for g in range(MOE_GROUPS):
        end_g = jnp.sum(jnp.where(lane_i == g, end, 0.0), axis=-1, keepdims=True)
        grp_of_blk += jnp.where(lane >= end_g, 1.0, 0.0)
    grp_of_blk = jnp.minimum(grp_of_blk, MOE_GROUPS - 1)
    n_used = jnp.sum(blocks, axis=-1, keepdims=True)
    row = lax.broadcasted_iota(jnp.int32, (8, LANES), 0)
    plan_ref[...] = jnp.where(
        row == PLAN_GROUP, grp_of_blk,
        jnp.where(row == PLAN_USED, n_used,
                  jnp.where(row == PLAN_END, end,
                            jnp.where(row == PLAN_BLOCKS, blocks,
                                      jnp.where(row == PLAN_BASE, start * MOE_BLK, 0.0))))
    ).astype(jnp.int32)


def _plan(cnt):
    return pl.pallas_call(
        _plan_kernel,
        out_shape=jax.ShapeDtypeStruct((8, LANES), jnp.int32),
        name="plan",
    )(cnt)


SEG_START, SEG_LEN = 0, 1


def _segment_copies(seg_ref, base_ref, make_copy, enable=None):
    local = 0
    for g in range(MOE_GROUPS):
        n = seg_ref[g, SEG_LEN]
        first = base_ref[g] + seg_ref[g, SEG_START]
        k = TM_ROW
        while k >= 1:
            done = n & ~(2 * k - 1)
            wanted = (n & k) != 0
            @pl.when(wanted if enable is None else wanted & enable)
            def _():
                make_copy(local + done, first + done, k).start()
            k //= 2
        local = local + n
        if g % COPY_STAGE_GROUPS == COPY_STAGE_GROUPS - 1:
            yield


def _perm_matrix(route_ref):
    row = lax.broadcasted_iota(jnp.int32, (TM_ROW, TM_ROW), 0).astype(F32)
    return jnp.where(row == route_ref[ROUTE_LRANK:ROUTE_LRANK + 1, :], 1.0, 0.0).astype(BF16)


SUB = 8
SUB_X = D_MODEL // 2 // LANES
SUB_GATE = SUB_X
HI_MASK = -65536


def _rows(first_row, n_rows):
    return pl.ds(pl.multiple_of(first_row * SUB, SUB), n_rows * SUB)


def _sublane(s, n_rows):
    return pl.ds(s, n_rows, stride=SUB)


def _scatter_kernel(nu_ref, end_ref, nb_ref, base_ref, seg_ref, route_ref, xn_ref, xs_ref,
                    zbuf, sbuf, sems):
    i = pl.program_id(0)
    n = pl.num_programs(0)
    sem = sems.at[0]

    def zero_block(b):
        return pltpu.make_async_copy(zbuf, xs_ref.at[_rows(b * MOE_BLK, MOE_BLK)], sem)

    def each_unfilled_block(fn):
        for g in range(MOE_GROUPS):
            @pl.when(nb_ref[g] > 0)
            def _():
                fn(zero_block(end_ref[g] - 1))
        for b in range(T_ALL // MOE_BLK, N_BLK):
            @pl.when(b >= nu_ref[0])
            def _():
                fn(zero_block(b))

    @pl.when(i == 0)
    def _():
        zbuf[...] = jnp.zeros_like(zbuf)
        sbuf[...] = jnp.zeros_like(sbuf)
        each_unfilled_block(lambda c: c.start())
        each_unfilled_block(lambda c: c.wait())

    def tile_done(s):
        pltpu.make_async_copy(sbuf.at[s], xs_ref.at[_rows(0, TM_ROW)], sems.at[s]).wait()

    @pl.when(i >= 2)
    def _():
        for t in range(ROW_TILES):
            tile_done((i % 2) * ROW_TILES + t)

    def sort_tile(t):
        slot = (i % 2) * ROW_TILES + t
        route = route_ref.at[t]
        perm = _perm_matrix(route)
        xn = xn_ref[t * TM_ROW:(t + 1) * TM_ROW, :]
        xs = lax.bitcast_convert_type(_dot(perm, xn.astype(BF16)), jnp.int32)
        yield
        half = D_MODEL // 2
        for s in range(SUB_X):
            hi = xs[:, s * LANES:(s + 1) * LANES] & HI_MASK
            lo = lax.shift_right_logical(xs[:, half + s * LANES:half + (s + 1) * LANES], 16)
            sbuf[slot, _sublane(s, TM_ROW), :] = hi | lo
        yield
        gate_t = route[0:EXPERTS_PER_GROUP, :]
        gate_t = jnp.concatenate(
            [gate_t, jnp.zeros((LANES - EXPERTS_PER_GROUP, TM_ROW), F32)], axis=0)
        g1 = gate_t.astype(BF16)
        r1 = gate_t - g1.astype(F32)
        g2 = r1.astype(BF16)
        g3 = (r1 - g2.astype(F32)).astype(BF16)
        sbuf[slot, _sublane(SUB_GATE, TM_ROW), :] = lax.bitcast_convert_type(
            _dot_nt(perm, g1) + _dot_nt(perm, g2) + _dot_nt(perm, g3), jnp.int32)

    _interleave(*[_delayed(sort_tile(t), t) for t in range(ROW_TILES)])

    for t in range(ROW_TILES):
        def make_copy(src_row, dst_row, k, slot=(i % 2) * ROW_TILES + t):
            return pltpu.make_async_copy(sbuf.at[slot, _rows(src_row, k)],
                                         xs_ref.at[_rows(dst_row, k)], sems.at[slot])
        for _ in _segment_copies(seg_ref.at[t], base_ref, make_copy):
            pass

    @pl.when(i == n - 1)
    def _():
        for t in range(ROW_TILES):
            tile_done((i % 2) * ROW_TILES + t)

            @pl.when(n > 1)
            def _():
                tile_done((1 - i % 2) * ROW_TILES + t)


def _scatter(seg, route, xn, n_used, grp_end, grp_blocks, grp_base):
    m = xn.shape[0]
    step_rows = ROW_TILES * TM_ROW
    grid_spec = pltpu.PrefetchScalarGridSpec(
        num_scalar_prefetch=4,
        grid=(m // step_rows,),
        in_specs=[
            pl.BlockSpec((ROW_TILES, 8, LANES), lambda i, *_: (i, 0, 0), memory_space=pltpu.SMEM),
            pl.BlockSpec((ROW_TILES, ROUTE_ROWS, TM_ROW), lambda i, *_: (i, 0, 0)),
            pl.BlockSpec((step_rows, D_MODEL), lambda i, *_: (i, 0)),
        ],
        out_specs=pl.BlockSpec(memory_space=pl.ANY),
        scratch_shapes=[pltpu.VMEM((MOE_BLK * SUB, LANES), jnp.int32),
                        pltpu.VMEM((2 * ROW_TILES, TM_ROW * SUB, LANES), jnp.int32),
                        pltpu.SemaphoreType.DMA((2 * ROW_TILES,))],
    )
    return pl.pallas_call(
        _scatter_kernel,
        grid_spec=grid_spec,
        out_shape=jax.ShapeDtypeStruct((N_SORTED * SUB, LANES), jnp.int32),
        compiler_params=pltpu.CompilerParams(
            dimension_semantics=("arbitrary",), vmem_limit_bytes=VMEM_LIMIT),
        name="scatter",
    )(n_used, grp_end, grp_blocks, grp_base, seg, route, xn)


def _ffn_kernel(bg_ref, nu_ref, xs_ref, w1_ref, w3_ref, w2_ref, y_ref):
    del bg_ref
    b = pl.program_id(0)

    def block_inputs():
        packed = [xs_ref[_sublane(s, MOE_BLK), :] for s in range(SUB_X)]
        x = jnp.concatenate(
            [lax.bitcast_convert_type(u & HI_MASK, F32).astype(BF16) for u in packed]
            + [lax.bitcast_convert_type(lax.shift_left(u, 16), F32).astype(BF16) for u in packed],
            axis=1)
        return x, lax.bitcast_convert_type(xs_ref[_sublane(SUB_GATE, MOE_BLK), :], F32)

    def store(y):
        for s in range(SUB):
            y_ref[_sublane(s, MOE_BLK), :] = y[:, s * LANES:(s + 1) * LANES]

    @pl.when(b < nu_ref[0])
    def _():
        x, gates = block_inputs()
        n_e = EXPERTS_PER_GROUP
        h1 = [_dot(x, w1_ref[e]) for e in range(n_e)]
        h3 = [_dot(x, w3_ref[e]) for e in range(n_e)]
        hs = []
        for e in range(n_e):
            ge = gates[:, e:e + 1]
            hs.append(jnp.where(ge > 0.0, h1[e] * _sigmoid(h1[e]) * h3[e] * ge, 0.0).astype(BF16))
        hcat = jnp.concatenate(hs, axis=1)
        store(_dot(hcat, w2_ref[...].reshape(n_e * D_EXPERT, D_MODEL)))

    @pl.when(b >= nu_ref[0])
    def _():
        y_ref[...] = jnp.zeros_like(y_ref)


def _ffn(blk_group, n_used, xs, w1, w3, w2):
    grouped = (MOE_GROUPS, EXPERTS_PER_GROUP)
    w1, w3, w2 = (w.reshape(grouped + w.shape[1:]) for w in (w1, w3, w2))
    wmap = lambda b, bg, nu: (bg[b], 0, 0, 0)
    grid_spec = pltpu.PrefetchScalarGridSpec(
        num_scalar_prefetch=2,
        grid=(N_BLK,),
        in_specs=[
            pl.BlockSpec((MOE_BLK * SUB, LANES), lambda b, bg, nu: (b, 0)),
            pl.BlockSpec((None, EXPERTS_PER_GROUP, D_MODEL, D_EXPERT), wmap),
            pl.BlockSpec((None, EXPERTS_PER_GROUP, D_MODEL, D_EXPERT), wmap),
            pl.BlockSpec((None, EXPERTS_PER_GROUP, D_EXPERT, D_MODEL), wmap),
        ],
        out_specs=pl.BlockSpec((MOE_BLK * SUB, LANES), lambda b, bg, nu: (b, 0)),
    )
    return pl.pallas_call(
        _ffn_kernel,
        grid_spec=grid_spec,
        out_shape=jax.ShapeDtypeStruct((N_SORTED * SUB, LANES), F32),
        compiler_params=pltpu.CompilerParams(
            dimension_semantics=("arbitrary",), vmem_limit_bytes=VMEM_LIMIT),
        name="ffn",
    )(blk_group, n_used, xs, w1, w3, w2)


def _combine_kernel(base_ref, *refs, final):
    n_buf = GATHER_AHEAD + 1
    seg_refs, (route_ref, ys_hbm, x_ref, g_ref, wa_ref) = refs[:n_buf], refs[n_buf:n_buf + 5]
    outs, (buf, sem, stage) = refs[n_buf + 5:-3], refs[-3:]
    i = pl.program_id(0)
    n = pl.num_programs(0)
    step_rows = ROW_TILES * TM_ROW

    def gather(ahead, t, enable=None):
        s = ((i + ahead) % n_buf) * ROW_TILES + t

        def make_copy(buf_row, ys_row, k):
            return pltpu.make_async_copy(ys_hbm.at[_rows(ys_row, k)],
                                         buf.at[s, _rows(buf_row, k)], sem.at[s])
        return _segment_copies(seg_refs[ahead].at[t], base_ref, make_copy, enable)

    @pl.when(i == 0)
    def _():
        for ahead in range(GATHER_AHEAD):
            for t in range(ROW_TILES):
                for _ in gather(ahead, t, ahead < n):
                    pass

    for t in range(ROW_TILES):
        slot = (i % n_buf) * ROW_TILES + t
        pltpu.make_async_copy(ys_hbm.at[_rows(0, TM_ROW)], buf.at[slot], sem.at[slot]).wait()

    def tile_stages(t):
        slot = (i % n_buf) * ROW_TILES + t
        rows = slice(t * TM_ROW, (t + 1) * TM_ROW)
        tn = (((0,), (0,)), ((), ()))
        perm = _perm_matrix(route_ref.at[t])
        cols = []
        for s in range(SUB):
            hi, lo = _split_bf16(buf[slot, _sublane(s, TM_ROW), :])
            cols.append(lax.dot_general(perm, hi, tn, preferred_element_type=F32)
                        + lax.dot_general(perm, lo, tn, preferred_element_type=F32))
            if s % 4 == 3:
                yield
        y = x_ref[rows, :] + jnp.concatenate(cols, axis=1)
        if not final:
            outs[0][rows, :] = y
            yield
            _norm_and_lowrank(y, g_ref, wa_ref, outs[1].at[rows, :], outs[2].at[rows, :])
        else:
            yield
            stage[rows, :] = _rms(y, g_ref[...])

    _interleave(*[_delayed(tile_stages(t), t) for t in range(ROW_TILES)],
                *[_delayed(gather(GATHER_AHEAD, t, i + GATHER_AHEAD < n), t)
                  for t in range(ROW_TILES)])

    if final:
        @pl.when(i < T_PROMPT // step_rows)
        def _():
            outs[0][...] = stage[...]

        @pl.when(i >= T_PROMPT // step_rows)
        def _():
            outs[1][...] = stage[...]


def _combine(seg, route, grp_base, ys, x, g, wa, final):
    m = x.shape[0]
    step_rows = ROW_TILES * TM_ROW
    nt = m // step_rows
    n_p = T_PROMPT // step_rows
    smem = functools.partial(pl.BlockSpec, (ROW_TILES, 8, LANES), memory_space=pltpu.SMEM)
    tile = (step_rows, D_MODEL)
    if final:
        out_specs = [pl.BlockSpec(tile, lambda i, *_: (jnp.minimum(i, n_p - 1), 0)),
                     pl.BlockSpec(tile, lambda i, *_: (jnp.maximum(i - n_p, 0), 0))]
        out_shape = [jax.ShapeDtypeStruct((T_PROMPT, D_MODEL), F32),
                     jax.ShapeDtypeStruct((T_SAMPLE, D_MODEL), F32)]
    else:
        out_specs = [pl.BlockSpec(tile, lambda i, *_: (i, 0)),
                     pl.BlockSpec(tile, lambda i, *_: (i, 0)),
                     pl.BlockSpec((step_rows, LANES), lambda i, *_: (i, 0))]
        out_shape = [jax.ShapeDtypeStruct((m, D_MODEL), F32),
                     jax.ShapeDtypeStruct((m, D_MODEL), BF16),
                     jax.ShapeDtypeStruct((m, LANES), BF16)]
    grid_spec = pltpu.PrefetchScalarGridSpec(
        num_scalar_prefetch=1,
        grid=(nt,),
        in_specs=[smem(lambda i, *_, ahead=ahead: (jnp.minimum(i + ahead, nt - 1), 0, 0))
                  for ahead in range(GATHER_AHEAD + 1)] + [
            pl.BlockSpec((ROW_TILES, ROUTE_ROWS, TM_ROW), lambda i, *_: (i, 0, 0)),
            pl.BlockSpec(memory_space=pl.ANY),
            pl.BlockSpec(tile, lambda i, *_: (i, 0)),
            pl.BlockSpec(g.shape, lambda i, *_: (0, 0)),
            pl.BlockSpec(wa.shape, lambda i, *_: (0, 0)),
        ],
        out_specs=out_specs,
        scratch_shapes=[pltpu.VMEM(((GATHER_AHEAD + 1) * ROW_TILES, TM_ROW * SUB, LANES), F32),
                        pltpu.SemaphoreType.DMA(((GATHER_AHEAD + 1) * ROW_TILES,)),
                        pltpu.VMEM(tile if final else (8, LANES), F32)],
    )
    return pl.pallas_call(
        functools.partial(_combine_kernel, final=final),
        grid_spec=grid_spec,
        out_shape=out_shape,
        compiler_params=pltpu.CompilerParams(
            dimension_semantics=("arbitrary",), vmem_limit_bytes=VMEM_LIMIT),
        name="combine",
    )(grp_base, *[seg] * (GATHER_AHEAD + 1), route, ys, x, g, wa)


def _prep_weights(w_in, gla_a2, cm_ws, cm_b, router_group_w, router_group_b,
                  router_expert_w, router_expert_b):
    w_z = jnp.swapaxes(w_in, 1, 2).astype(BF16)
    a0 = _w_in_offsets()["alr"][0]
    w_alr = jnp.pad(w_z[:, a0:a0 + GLA_LOWRANK], ((0, 0), (0, LANES - GLA_LOWRANK), (0, 0)))
    a2 = jnp.pad(gla_a2, ((0, 0), (0, LANES - GLA_LOWRANK), (0, 0))).astype(BF16)
    ws_p = jnp.tril(cm_ws).astype(BF16)
    small = jnp.tril(cm_ws[:, :, :DEC_SEQ, :DEC_SEQ])
    eye = jnp.eye(SEQ_PER_BLK, dtype=F32)
    ws_s = jnp.einsum("ij,lgab->lgiajb", eye, small).reshape(
        DEPTH, CM_GROUPS, ROWS_S, ROWS_S).astype(BF16)
    cmb_p = jnp.broadcast_to(jnp.transpose(cm_b, (0, 2, 1))[:, :, :, None],
                             (DEPTH, CM_CHUNK, CM_GROUPS, CM_GCH)).reshape(DEPTH, CM_CHUNK, CM_CH)
    cmb_s = jnp.tile(cmb_p[:, :DEC_SEQ], (1, SEQ_PER_BLK, 1))
    pad = LANES - MOE_GROUPS - N_EXPERTS
    w_r = jnp.pad(jnp.swapaxes(jnp.concatenate([router_group_w, router_expert_w], axis=-1), 1, 2),
                  ((0, 0), (0, pad), (0, 0)))
    r_b = jnp.pad(jnp.concatenate([router_group_b, router_expert_b], axis=-1),
                  ((0, 0), (0, pad)))[:, :, None]
    return w_z, w_alr, a2, ws_p, ws_s, cmb_p, cmb_s, w_r.astype(BF16), r_b


def kernel(x_prompt, x_sample, state_conv, state_gla, norm1_g, w_in, conv_w, gla_a2, gla_a_b,
           gla_norm_g, cm_norm_g, cm_ws, cm_b, proj_a, proj_b, proj_c, w_out, norm2_g,
           router_group_w, router_group_b, router_expert_w, router_expert_b,
           exp_w1, exp_w3, exp_w2, final_norm_g):
    (w_z, w_alr, a2, ws_p, ws_s, cmb_p, cmb_s, w_r, r_b) = _prep_weights(
        w_in, gla_a2, cm_ws, cm_b, router_group_w, router_group_b, router_expert_w,
        router_expert_b)
    pa, pb, pc, wo = (w.astype(BF16) for w in (proj_a, proj_b, proj_c, w_out))
    n1 = norm1_g[:, None, :]
    n2 = norm2_g[:, None, :]
    ab = gla_a_b[:, None, :]
    gng = gla_norm_g[:, None, :]
    cmg = cm_norm_g.reshape(DEPTH, 1, CM_CH)
    fg = final_norm_g[None, :]
    xa = x_prompt.reshape(T_PROMPT, D_MODEL)
    xb = x_sample.reshape(T_SAMPLE, D_MODEL)
    gla_s = None
    conv_p, gla_p, conv_s, cmv_s = [], [], [], []
    xin, alr = _prenorm(xa, xb, n1, w_alr, 0)
    for l in range(DEPTH):
        z, *experts = _inproj(xin, w_z, (exp_w1, exp_w3, exp_w2), l)
        acts_p, nconv, ngla = _mix_prompt(z, alr, a2, ab, conv_w, gng, cmg, ws_p, cmb_p, l)
        sc = state_conv[l]
        p2 = jnp.pad(sc, ((0, 0), (0, DEC_SEQ - 2), (0, 0))).reshape(T_SAMPLE, CONV_CH)
        p1 = jnp.pad(sc[:, 1:2], ((0, 0), (0, DEC_SEQ - 1), (0, 0))).reshape(T_SAMPLE, CONV_CH)
        acts_s, cin_s, gla_s, vrows = _mix_sample(z, alr, a2, ab, conv_w, gng, cmg, ws_s, cmb_s,
                                                  p1, p2, state_gla, gla_s, l)
        conv_p.append(nconv)
        gla_p.append(ngla)
        conv_s.append(cin_s.reshape(DEC_BATCH, DEC_SEQ, CONV_CH)[:, DEC_SEQ - (CONV_K - 1):])
        cmv_s.append(vrows.reshape(DEC_BATCH, DEC_SEQ, CM_CH))

        x, xn, route, seg, cnt = _outproj(acts_p, acts_s, z, xa, xb, pa, pb, pc, wo, n2, w_r, r_b,
                                          l)
        plan = _plan(cnt)
        n_used = plan[PLAN_USED, :1]
        grp_base = plan[PLAN_BASE, :MOE_GROUPS]
        xs = _scatter(seg, route, xn, n_used, plan[PLAN_END, :MOE_GROUPS],
                      plan[PLAN_BLOCKS, :MOE_GROUPS], grp_base)
        ys = _ffn(plan[PLAN_GROUP, :N_BLK], n_used, xs, *experts)
        if l == DEPTH - 1:
            out = _combine(seg, route, grp_base, ys, x, fg, w_alr[l], True)
        else:
            xa, xin, alr = _combine(seg, route, grp_base, ys, x, n1[l + 1], w_alr[l + 1], False)
            xb = xa

    y_prompt = out[0].reshape(BATCH, SEQ, D_MODEL)
    y_sample = out[1].reshape(DEC_BATCH, DEC_SEQ, D_MODEL)
    return (y_prompt, y_sample, jnp.stack(conv_p), jnp.stack(gla_p), jnp.stack(conv_s),
            gla_s, jnp.stack(cmv_s))
```

```python
import functools

import jax
import jax.numpy as jnp
from jax import lax
from jax.experimental import pallas as pl
from jax.experimental.pallas import tpu as pltpu

F32 = jnp.float32
BF16 = jnp.bfloat16

D_MODEL = 1024
BATCH = 8
SEQ = 2048
DEPTH = 2
DEC_BATCH = 128
DEC_SEQ = 8
CONV_K = 3
CONV_CH = 512
GLA_HEADS = 4
GLA_DK = 128
GLA_DV = 256
GLA_QK = GLA_HEADS * GLA_DK
GLA_V = GLA_HEADS * GLA_DV
GLA_LOWRANK = 16
GLA_TAU = 16.0
GLA_CHUNK = 64
CM_GROUPS = 4
CM_CHUNK = 128
CM_GCH = 128
CM_CH = 512
MOE_GROUPS = 8
EXPERTS_PER_GROUP = 8
N_EXPERTS = 64
D_EXPERT = 256
EPS = 1e-6

LANES = 128
T_PROMPT = BATCH * SEQ
T_SAMPLE = DEC_BATCH * DEC_SEQ
T_ALL = T_PROMPT + T_SAMPLE

COL_GA, COL_GB, COL_GC = 0, 1024, 2048
COL_V, COL_R, COL_Q, COL_K = 3072, 4096, 5120, 5632
COL_H, COL_CG, COL_BG, COL_U, COL_VV = 6144, 6656, 7168, 7680, 8192
Z_COLS = 8704
ACT_COLS = 2048
ROUTE_ROWS = 2 * EXPERTS_PER_GROUP
ROUTE_LRANK = EXPERTS_PER_GROUP

TM_PRE = 1024
TM_IN = 1024
TN_IN = Z_COLS // 2

W_IN_WIDTHS = (("h", 512), ("cg", 512), ("bg", 512), ("q", 512), ("k", 512), ("v", 1024),
               ("r", 1024), ("alr", GLA_LOWRANK), ("u", 512), ("vv", 512), ("ga", 1024),
               ("gb", 1024), ("gc", 1024))
Z_ORDER = ("ga", "gb", "gc", "v", "r", "q", "k", "h", "cg", "bg", "u", "vv")


def _w_in_offsets():
    off, o = {}, 0
    for name, n in W_IN_WIDTHS:
        off[name] = (o, n)
        o += n
    return off


def _w_in_copies():
    off = _w_in_offsets()
    copies = [[] for _ in range(Z_COLS // TN_IN)]
    dst = 0
    for name in Z_ORDER:
        src, n = off[name]
        while n:
            tile = dst // TN_IN
            at = dst - tile * TN_IN
            take = min(n, TN_IN - at)
            runs = copies[tile]
            if runs and runs[-1][0] + runs[-1][2] == src and runs[-1][1] + runs[-1][2] == at:
                runs[-1] = (runs[-1][0], runs[-1][1], runs[-1][2] + take)
            else:
                runs.append((src, at, take))
            src, dst, n = src + take, dst + take, n - take
    assert dst == Z_COLS
    assert all(v % 16 == 0 for runs in copies for run in runs for v in run)
    return copies
TC_MIX = 256
SEQ_PER_BLK = 16
ROWS_S = SEQ_PER_BLK * DEC_SEQ
TM_OUT = 256
ROUTE_DELAY = 1
TM_ROW = TM_OUT
ROW_TILES = 4
GATHER_AHEAD = 2
COPY_STAGE_GROUPS = 4
MOE_BLK = 256
N_BLK = T_ALL // MOE_BLK + MOE_GROUPS
N_SORTED = N_BLK * MOE_BLK
VMEM_CAPACITY = 64 * 1024 * 1024
VMEM_LIMIT = 56 * 1024 * 1024
VMEM_SLACK = 4 * 1024 * 1024


def _sigmoid(x):
    return 0.5 * jnp.tanh(0.5 * x) + 0.5


def _gelu_tanh(x):
    c = 0.7978845608028654
    half = 0.5 * x
    return half + half * jnp.tanh(x * (c + (c * 0.044715) * (x * x)))


def _log_sigmoid(x):
    log2_e = 1.4426950408889634
    ln_2 = 0.6931471805599453
    return jnp.minimum(x, 0.0) - ln_2 * jnp.log2(1.0 + jnp.exp2(-log2_e * jnp.abs(x)))


def _rms(x, g):
    ms = jnp.mean(x * x, axis=-1, keepdims=True)
    return x * lax.rsqrt(ms + EPS) * g


def _split_bf16(x):
    hi = x.astype(BF16)
    lo = (x - hi.astype(F32)).astype(BF16)
    return hi, lo


def _dot(a, b):
    return jnp.dot(a, b, preferred_element_type=F32)


def _dot_nt(a, b):
    return lax.dot_general(a, b, (((1,), (1,)), ((), ())), preferred_element_type=F32)


def _layer_spec(arr, layer):
    nd = arr.ndim - 1
    return pl.BlockSpec((None,) + arr.shape[1:], lambda *g: (layer,) + (0,) * nd)


def _two_part_specs(tile, xa, xb):
    n_a = xa.shape[0] // tile
    n_b = xb.shape[0] // tile
    return n_a, [
        pl.BlockSpec((tile, xa.shape[1]), lambda i, *_: (jnp.minimum(i, n_a - 1), 0)),
        pl.BlockSpec((tile, xb.shape[1]), lambda i, *_: (jnp.clip(i - n_a, 0, n_b - 1), 0)),
    ]


def _norm_and_lowrank(x, g_ref, wa_ref, xn_ref, a_ref):
    xn = _rms(x, g_ref[...]).astype(BF16)
    xn_ref[...] = xn
    a_ref[...] = _dot_nt(xn, wa_ref[...]).astype(BF16)


def _prenorm_kernel(xa_ref, xb_ref, g_ref, wa_ref, xn_ref, a_ref, *, n_a):
    x = jnp.where(pl.program_id(0) < n_a, xa_ref[...], xb_ref[...])
    _norm_and_lowrank(x, g_ref, wa_ref, xn_ref, a_ref)


def _prenorm(xa, xb, g, wa, layer):
    m = T_ALL
    n_a, x_specs = _two_part_specs(TM_PRE, xa, xb)
    return pl.pallas_call(
        functools.partial(_prenorm_kernel, n_a=n_a),
        grid=(m // TM_PRE,),
        in_specs=x_specs + [_layer_spec(g, layer), _layer_spec(wa, layer)],
        out_specs=[pl.BlockSpec((TM_PRE, D_MODEL), lambda i: (i, 0)),
                   pl.BlockSpec((TM_PRE, LANES), lambda i: (i, 0))],
        out_shape=[jax.ShapeDtypeStruct((m, D_MODEL), BF16),
                   jax.ShapeDtypeStruct((m, LANES), BF16)],
        compiler_params=pltpu.CompilerParams(
            dimension_semantics=("arbitrary",), vmem_limit_bytes=VMEM_LIMIT),
        name="prenorm",
    )(xa, xb, g, wa)


EXPERT_SHAPES = ((D_MODEL, D_EXPERT), (D_MODEL, D_EXPERT), (D_EXPERT, D_MODEL))


def _inproj_kernel(xn_ref, w_hbm, *refs, layer):
    n = len(EXPERT_SHAPES)
    z_ref, w_buf, sem = refs[n], refs[-2], refs[-1]
    j, i = pl.program_id(0), pl.program_id(1)

    def tile_copies(tile):
        return [pltpu.make_async_copy(w_hbm.at[layer, pl.ds(src, rows)],
                                      w_buf.at[tile, pl.ds(at, rows)], sem.at[tile])
                for src, at, rows in _w_in_copies()[tile]]

    @pl.when((j == 0) & (i == 0))
    def _():
        for tile in range(w_buf.shape[0]):
            for cp in tile_copies(tile):
                cp.start()

    for tile in range(w_buf.shape[0]):
        @pl.when((j == tile) & (i == 0))
        def _():
            for cp in tile_copies(tile):
                cp.wait()

    for src, dst in zip(refs[:n], refs[n + 1:]):
        dst[...] = src[...].astype(BF16)
    z_ref[...] = _dot_nt(xn_ref[...], w_buf[j]).astype(BF16)


def _inproj(xn, w, experts, layer):
    m = xn.shape[0]
    n_j, n_i = Z_COLS // TN_IN, m // TM_IN
    per_step = -(-N_EXPERTS // (n_j * n_i))
    assert N_EXPERTS % per_step == 0
    expert = lambda j, i: jnp.minimum(j * n_i + i, N_EXPERTS // per_step - 1)
    vmem_bytes = (2 * Z_COLS * D_MODEL + 2 * 2 * TM_IN * (TN_IN + D_MODEL)
                  + 2 * per_step * (4 + 2) * sum(a * b for a, b in EXPERT_SHAPES))
    assert vmem_bytes + VMEM_SLACK <= VMEM_CAPACITY
    return pl.pallas_call(
        functools.partial(_inproj_kernel, layer=layer),
        grid=(n_j, n_i),
        in_specs=[
            pl.BlockSpec((TM_IN, D_MODEL), lambda j, i: (i, 0)),
            pl.BlockSpec(memory_space=pl.ANY),
        ] + [pl.BlockSpec((None, per_step) + s, lambda j, i: (layer, expert(j, i), 0, 0))
             for s in EXPERT_SHAPES],
        out_specs=[pl.BlockSpec((TM_IN, TN_IN), lambda j, i: (i, j))]
        + [pl.BlockSpec((per_step,) + s, lambda j, i: (expert(j, i), 0, 0))
           for s in EXPERT_SHAPES],
        out_shape=[jax.ShapeDtypeStruct((m, Z_COLS), BF16)]
        + [jax.ShapeDtypeStruct((N_EXPERTS,) + s, BF16) for s in EXPERT_SHAPES],
        scratch_shapes=[pltpu.VMEM((n_j, TN_IN, D_MODEL), BF16), pltpu.SemaphoreType.DMA((n_j,))],
        compiler_params=pltpu.CompilerParams(
            dimension_semantics=("arbitrary", "arbitrary"),
            vmem_limit_bytes=vmem_bytes + VMEM_SLACK),
        name="inproj",
    )(xn, w, *experts)


def _gla_log_decay(alr_ref, a2_ref, ab_ref):
    la = _log_sigmoid(_dot(alr_ref[...], a2_ref[...]) + ab_ref[...]) * (1.0 / GLA_TAU)
    return _split_bf16(la)


def _masked_sum(mask, la_hi, la_lo):
    m = jnp.where(mask, 1.0, 0.0).astype(BF16)
    return _dot(m, la_hi) + _dot(m, la_lo)


def _gla_decay_terms(alr_ref, a2_ref, ab_ref, tril_mask, same_mask):
    la_hi, la_lo = _gla_log_decay(alr_ref, a2_ref, ab_ref)
    return (la_hi, la_lo, _masked_sum(tril_mask, la_hi, la_lo),
            _masked_sum(same_mask, la_hi, la_lo))


def _gla_out_gate(o, g_ref, r):
    return _rms(o, g_ref[...]) * (r * _sigmoid(r))


def _chunk_mlp_group(g, bgu_ref, vv_ref, cmg_ref, ws_ref, cmb_ref, n_chunks):
    sl = slice(g * CM_GCH, (g + 1) * CM_GCH)
    ug = _gelu_tanh(bgu_ref[:, CONV_CH + g * CM_GCH:CONV_CH + (g + 1) * CM_GCH])
    vg = _rms(_gelu_tanh(vv_ref[:, sl]).astype(F32), cmg_ref[:, sl])
    vgb = vg.astype(BF16)
    rows = []
    for j in range(n_chunks):
        rs = slice(j * CM_CHUNK, (j + 1) * CM_CHUNK)
        rows.append(_dot(ws_ref[g], vgb[rs]) + cmb_ref[:, sl])
    s = rows[0] if n_chunks == 1 else jnp.concatenate(rows, axis=0)
    return ug * s, vg


def _chunk_mlp(bgu_ref, vv_ref, cmg_ref, ws_ref, cmb_ref, n_chunks):
    parts = [_chunk_mlp_group(g, bgu_ref, vv_ref, cmg_ref, ws_ref, cmb_ref, n_chunks)
             for g in range(CM_GROUPS)]
    return (jnp.concatenate([p[0] for p in parts], axis=1),
            jnp.concatenate([p[1] for p in parts], axis=1))


N_Z_VIEWS = 7
N_MIX_W = 7
SEQ_PER_STEP = 4
SEQ_STAGGER = 3


def _mix_prompt_kernel(*refs):
    n_z = N_Z_VIEWS * SEQ_PER_STEP
    weights = refs[n_z:n_z + N_MIX_W]
    acts_ref, nconv_ref, ngla_ref, st_ref, carry_ref = refs[n_z + N_MIX_W:]

    @pl.when(pl.program_id(1) == 0)
    def _():
        st_ref[...] = jnp.zeros_like(st_ref)
        carry_ref[...] = jnp.zeros_like(carry_ref)

    rr = lax.broadcasted_iota(jnp.int32, (TC_MIX, TC_MIX), 0)
    cc = lax.broadcasted_iota(jnp.int32, (TC_MIX, TC_MIX), 1)
    tril = ((rr >> 6) == (cc >> 6)) & (cc <= rr)
    masks = (tril, jnp.where(tril, 1.0, 0.0).astype(BF16),
             jnp.where(rr - cc == 1, 1.0, 0.0).astype(BF16),
             jnp.where(rr - cc == 2, 1.0, 0.0).astype(BF16))

    _interleave(*[
        _delayed(_mix_prompt_seq(*refs[N_Z_VIEWS * s:N_Z_VIEWS * (s + 1)], *weights,
                                acts_ref.at[s], nconv_ref.at[s], st_ref.at[s], carry_ref.at[s],
                                masks),
                s * SEQ_STAGGER)
        for s in range(SEQ_PER_STEP)])

    @pl.when(pl.program_id(1) == pl.num_programs(1) - 1)
    def _():
        for s in range(SEQ_PER_STEP):
            for hd in range(GLA_HEADS):
                ngla_ref[s, 0, hd] = st_ref[s, hd].T


def _mix_prompt_seq(v_ref, r_ref, qk_ref, hcg_ref, bgu_ref, vv_ref, alr_ref,
                    a2_ref, ab_ref, cw_ref, gng_ref, cmg_ref, ws_ref, cmb_ref,
                    acts_ref, nconv_ref, st_ref, carry_ref, masks):
    tc = TC_MIX

    h = hcg_ref[:, :CONV_CH].astype(F32)
    cg = hcg_ref[:, CONV_CH:].astype(F32)
    bg = bgu_ref[:, :CONV_CH].astype(F32)
    cin = cg * h
    tril, tril_01, shift_1, shift_2 = masks
    cin_b = cin.astype(BF16)
    x1 = _dot(shift_1, cin_b)
    x2 = _dot(shift_2, cin_b)
    conv = x2 * cw_ref[0:1, :] + x1 * cw_ref[1:2, :] + cin * cw_ref[2:3, :]
    c0 = carry_ref[0:1, :]
    c1 = carry_ref[1:2, :]
    row8 = lax.broadcasted_iota(jnp.int32, (8, 1), 0)
    head = jnp.where(row8 == 0, c0 * cw_ref[0:1, :] + c1 * cw_ref[1:2, :],
                     jnp.where(row8 == 1, c1 * cw_ref[0:1, :], 0.0))
    conv = jnp.concatenate([conv[0:8] + head, conv[8:]], axis=0)
    acts_ref[:, 0:CONV_CH] = (bg * conv).astype(BF16)
    carry_ref[0:2, :] = cin[tc - 2:tc, :]
    nconv_ref[0] = cin[tc - 2:tc, :]
    yield

    la_hi, la_lo = _gla_log_decay(alr_ref, a2_ref, ab_ref)
    b = _dot(tril_01, la_hi) + _dot(tril_01, la_lo)
    yield
    n_chunks = tc // GLA_CHUNK
    b_last = [b[(c + 1) * GLA_CHUNK - 1:(c + 1) * GLA_CHUNK, :] for c in range(n_chunks)]
    bl = jnp.concatenate([jnp.broadcast_to(r_, (GLA_CHUNK, GLA_QK)) for r_ in b_last], axis=0)
    q = qk_ref[:, :GLA_QK].astype(F32) * (GLA_DK ** -0.5)
    k = qk_ref[:, GLA_QK:].astype(F32)
    q_t = (q * jnp.exp(b)).astype(BF16)
    k_t = (k * jnp.exp(-b)).astype(BF16)
    k_end = (k * jnp.exp(bl - b)).astype(BF16)
    yield
    states = [st_ref[hd] for hd in range(GLA_HEADS)]
    k_cols = [slice(hd * GLA_DK, (hd + 1) * GLA_DK) for hd in range(GLA_HEADS)]
    v_cols = [slice(hd * GLA_DV, (hd + 1) * GLA_DV) for hd in range(GLA_HEADS)]
    o_intra = []
    for hd in range(GLA_HEADS):
        att = jnp.where(tril, _dot_nt(q_t[:, k_cols[hd]], k_t[:, k_cols[hd]]), 0.0).astype(BF16)
        o_intra.append(_dot(att, v_ref[:, v_cols[hd]]))
    yield

    o_rows = [[] for _ in range(GLA_HEADS)]
    assert n_chunks == CM_GROUPS
    for c in range(n_chunks):
        rs = slice(c * GLA_CHUNK, (c + 1) * GLA_CHUNK)
        for hd in range(GLA_HEADS):
            ks = k_cols[hd]
            st = states[hd]
            o_rows[hd].append(o_intra[hd][rs] + _dot_nt(q_t[rs, ks], st.astype(BF16)))
            upd = lax.dot_general(v_ref[rs, v_cols[hd]], k_end[rs, ks], (((0,), (0,)), ((), ())),
                                  preferred_element_type=F32)
            states[hd] = jnp.exp(b_last[c][:, ks]) * st + upd
        us, _ = _chunk_mlp_group(c, bgu_ref, vv_ref, cmg_ref, ws_ref, cmb_ref, tc // CM_CHUNK)
        acts_ref[:, CONV_CH + GLA_V + c * CM_GCH:CONV_CH + GLA_V + (c + 1) * CM_GCH] = (
            us.astype(BF16))
        yield

    for hd in range(GLA_HEADS):
        o = jnp.concatenate(o_rows[hd], axis=0)
        r = r_ref[:, v_cols[hd]]
        acts_ref[:, CONV_CH + hd * GLA_DV:CONV_CH + (hd + 1) * GLA_DV] = (
            _gla_out_gate(o, gng_ref, r).astype(BF16))
        st_ref[hd] = states[hd]
        if hd % 2 == 1:
            yield


def _z_specs(rows, row_map):
    def spec(width, col):
        blk = col // width
        return pl.BlockSpec((rows, width), lambda *g: (row_map(*g), blk))
    return [spec(1024, COL_V), spec(1024, COL_R), spec(1024, COL_Q), spec(1024, COL_H),
            spec(1024, COL_BG), spec(512, COL_VV)]


def _mix_prompt(z, alr, a2, ab, cw, gng, cmg, ws, cmb, layer):
    nt = SEQ // TC_MIX
    nb = BATCH // SEQ_PER_STEP
    small = (a2, ab, cw, gng, cmg, ws, cmb)
    assert len(small) == N_MIX_W
    in_specs, args = [], []
    for s in range(SEQ_PER_STEP):
        row_map = lambda b, c, s=s: (b + s * nb) * nt + c
        in_specs += _z_specs(TC_MIX, row_map) + [
            pl.BlockSpec((TC_MIX, LANES), lambda b, c, row_map=row_map: (row_map(b, c), 0))]
        args += [z] * (N_Z_VIEWS - 1) + [alr]
    in_specs += [_layer_spec(a, layer) for a in small]
    acts, nconv, ngla = pl.pallas_call(
        _mix_prompt_kernel,
        grid=(nb, nt),
        in_specs=in_specs,
        out_specs=[
            pl.BlockSpec((SEQ_PER_STEP, TC_MIX, ACT_COLS), lambda b, c: (0, b * nt + c, 0)),
            pl.BlockSpec((SEQ_PER_STEP, 1, CONV_K - 1, CONV_CH), lambda b, c: (0, b, 0, 0)),
            pl.BlockSpec((SEQ_PER_STEP, 1, GLA_HEADS, GLA_DK, GLA_DV),
                         lambda b, c: (0, b, 0, 0, 0)),
        ],
        out_shape=[
            jax.ShapeDtypeStruct((SEQ_PER_STEP, T_PROMPT // SEQ_PER_STEP, ACT_COLS), BF16),
            jax.ShapeDtypeStruct((SEQ_PER_STEP, nb, CONV_K - 1, CONV_CH), F32),
            jax.ShapeDtypeStruct((SEQ_PER_STEP, nb, GLA_HEADS, GLA_DK, GLA_DV), F32),
        ],
        scratch_shapes=[pltpu.VMEM((SEQ_PER_STEP, GLA_HEADS, GLA_DV, GLA_DK), F32),
                        pltpu.VMEM((SEQ_PER_STEP, 8, CONV_CH), F32)],
        compiler_params=pltpu.CompilerParams(
            dimension_semantics=("arbitrary", "arbitrary"), vmem_limit_bytes=VMEM_LIMIT),
        name="mix_prompt",
    )(*args, *small)
    return (acts.reshape(T_PROMPT, ACT_COLS), nconv.reshape(BATCH, CONV_K - 1, CONV_CH),
            ngla.reshape(BATCH, GLA_HEADS, GLA_DK, GLA_DV))


def _mix_sample_body(v_ref, r_ref, qk_ref, hcg_ref, bgu_ref, vv_ref, alr_ref,
                       a2_ref, ab_ref, cw_ref, gng_ref, cmg_ref, ws_ref, cmb_ref,
                       p1_ref, p2_ref, s0_ref,
                       acts_ref, cin_ref, ns_ref, vrow_ref):
    n = ROWS_S

    pos = lax.broadcasted_iota(jnp.int32, (n, 1), 0) & (DEC_SEQ - 1)
    h = hcg_ref[:, :CONV_CH].astype(F32)
    cg = hcg_ref[:, CONV_CH:].astype(F32)
    bg = bgu_ref[:, :CONV_CH].astype(F32)
    cin = cg * h
    x1 = jnp.where(pos >= 1, pltpu.roll(cin, 1, 0), p1_ref[...])
    x2 = jnp.where(pos >= 2, pltpu.roll(cin, 2, 0), p2_ref[...])
    conv = x2 * cw_ref[0:1, :] + x1 * cw_ref[1:2, :] + cin * cw_ref[2:3, :]
    acts_ref[:, 0:CONV_CH] = (bg * conv).astype(BF16)
    cin_ref[...] = cin

    rr = lax.broadcasted_iota(jnp.int32, (n, n), 0)
    cc = lax.broadcasted_iota(jnp.int32, (n, n), 1)
    same = (rr >> 3) == (cc >> 3)
    tril = same & (cc <= rr)
    la_hi, la_lo, b, bl = _gla_decay_terms(alr_ref, a2_ref, ab_ref, tril, same)
    q = qk_ref[:, :GLA_QK].astype(F32) * (GLA_DK ** -0.5)
    k = qk_ref[:, GLA_QK:].astype(F32)
    q_t = (q * jnp.exp(b)).astype(BF16)
    k_t = (k * jnp.exp(-b)).astype(BF16)
    k_end = k * jnp.exp(bl - b)
    la_hi = la_hi.astype(F32)
    la_lo = la_lo.astype(F32)
    row_seq = lax.broadcasted_iota(jnp.int32, (n, GLA_DK), 0) >> 3
    seq3 = lax.broadcasted_iota(jnp.int32, (SEQ_PER_BLK, GLA_DK, n), 0)
    lane_seq3 = lax.broadcasted_iota(jnp.int32, (SEQ_PER_BLK, GLA_DK, n), 2) >> 3
    mask3 = seq3 == lane_seq3
    ones = jnp.ones((n, GLA_DV), BF16)
    big = SEQ_PER_BLK * GLA_DK

    def per_seq(x_tr):
        x3 = jnp.where(mask3, x_tr[None, :, :], 0.0)
        return x3.reshape(big, n).astype(BF16)

    for hd in range(GLA_HEADS):
        ks = slice(hd * GLA_DK, (hd + 1) * GLA_DK)
        vs = slice(hd * GLA_DV, (hd + 1) * GLA_DV)
        qh = q_t[:, ks]
        vh = v_ref[:, vs]
        att = jnp.where(tril, _dot_nt(qh, k_t[:, ks]), 0.0).astype(BF16)
        o_intra = _dot(att, vh)
        s_old = s0_ref[:, hd].reshape(big, GLA_DV)
        zero = jnp.zeros_like(qh)
        q_big = jnp.concatenate(
            [jnp.where(row_seq == j, qh, zero) for j in range(SEQ_PER_BLK)], axis=1)
        o = o_intra + _dot(q_big, s_old.astype(BF16))
        dlog = _dot(per_seq(la_hi[:, ks].T), ones) + _dot(per_seq(la_lo[:, ks].T), ones)
        upd = _dot(per_seq(k_end[:, ks].T), vh)
        s_new = jnp.exp(dlog) * s_old + upd
        ns_ref[:, hd] = s_new.reshape(SEQ_PER_BLK, GLA_DK, GLA_DV)
        r = r_ref[:, vs]
        acts_ref[:, CONV_CH + hd * GLA_DV:CONV_CH + (hd + 1) * GLA_DV] = (
            _gla_out_gate(o, gng_ref, r).astype(BF16))

    us, vg = _chunk_mlp(bgu_ref, vv_ref, cmg_ref, ws_ref, cmb_ref, 1)
    acts_ref[:, CONV_CH + GLA_V:] = us.astype(BF16)
    vrow_ref[...] = vg


N_MIX_S_IN = 17


def _mix_sample_kernel(*refs, layer):
    if layer == 0:
        @pl.when(pl.program_id(0) == 0)
        def _():
            _mix_sample_body(*refs)

        @pl.when(pl.program_id(0) > 0)
        def _():
            ns_ref = refs[N_MIX_S_IN + 2]
            ns_ref[...] = jnp.zeros_like(ns_ref)
    else:
        _mix_sample_body(*refs[:N_MIX_S_IN], *refs[N_MIX_S_IN + 1:])


def _mix_sample(z, alr, a2, ab, cw, gng, cmg, ws, cmb, p1, p2, s0, ns_all, layer):
    row0 = T_PROMPT // ROWS_S
    n_i = DEC_BATCH // SEQ_PER_BLK
    n_pass = DEPTH if layer == 0 else 1
    blk = lambda p, i: jnp.where(p == 0, i, n_i - 1)
    row_map = lambda p, i: row0 + blk(p, i)
    slot = lambda p, i: (layer + p, i, 0, 0, 0)
    small = (a2, ab, cw, gng, cmg, ws, cmb)
    state_blk = (None, SEQ_PER_BLK, GLA_HEADS, GLA_DK, GLA_DV)
    in_specs = _z_specs(ROWS_S, row_map) + [
        pl.BlockSpec((ROWS_S, LANES), lambda p, i: (row_map(p, i), 0)),
    ] + [_layer_spec(a, layer) for a in small] + [
        pl.BlockSpec((ROWS_S, CONV_CH), lambda p, i: (blk(p, i), 0)),
        pl.BlockSpec((ROWS_S, CONV_CH), lambda p, i: (blk(p, i), 0)),
        pl.BlockSpec(state_blk, lambda p, i: (layer, blk(p, i), 0, 0, 0)),
    ]
    args = (z, z, z, z, z, z, alr, *small, p1, p2, s0)
    assert len(args) == N_MIX_S_IN
    aliases = {}
    if layer > 0:
        in_specs.append(pl.BlockSpec(memory_space=pl.ANY))
        args += (ns_all,)
        aliases = {N_MIX_S_IN: 2}
    return pl.pallas_call(
        functools.partial(_mix_sample_kernel, layer=layer),
        grid=(n_pass, n_i),
        in_specs=in_specs,
        out_specs=[
            pl.BlockSpec((ROWS_S, ACT_COLS), lambda p, i: (blk(p, i), 0)),
            pl.BlockSpec((ROWS_S, CONV_CH), lambda p, i: (blk(p, i), 0)),
            pl.BlockSpec(state_blk, slot),
            pl.BlockSpec((ROWS_S, CM_CH), lambda p, i: (blk(p, i), 0)),
        ],
        out_shape=[
            jax.ShapeDtypeStruct((T_SAMPLE, ACT_COLS), BF16),
            jax.ShapeDtypeStruct((T_SAMPLE, CONV_CH), F32),
            jax.ShapeDtypeStruct((DEPTH, DEC_BATCH, GLA_HEADS, GLA_DK, GLA_DV), F32),
            jax.ShapeDtypeStruct((T_SAMPLE, CM_CH), F32),
        ],
        input_output_aliases=aliases,
        compiler_params=pltpu.CompilerParams(
            dimension_semantics=("arbitrary", "arbitrary"), vmem_limit_bytes=VMEM_LIMIT),
        name="mix_sample",
    )(*args)


def _delayed(gen, n_stages):
    for _ in range(n_stages):
        yield
    yield from gen


def _interleave(*stage_lists):
    pending = list(stage_lists)
    while pending:
        for gen in list(pending):
            if next(gen, StopIteration) is StopIteration:
                pending.remove(gen)


def _outproj_main(i, n_xa, actp_ref, acts_ref, gates_ref, xa_ref, xb_ref, pa_ref, pb_ref, pc_ref,
                  wo_ref, n2_ref, xo_ref, xn_ref, xnb_ref):
    acts = jnp.where(i < T_PROMPT // TM_OUT, actp_ref[...], acts_ref[...])
    x_in = jnp.where(i < n_xa, xa_ref[...], xb_ref[...])
    yb = _dot(acts[:, CONV_CH:CONV_CH + GLA_V], pb_ref[...])
    ga = _sigmoid(gates_ref[:, COL_GA:COL_GA + D_MODEL])
    ya = _dot(acts[:, :CONV_CH], pa_ref[...])
    gb = _sigmoid(gates_ref[:, COL_GB:COL_GB + D_MODEL])
    yield
    yc = _dot(acts[:, CONV_CH + GLA_V:], pc_ref[...])
    gc = _sigmoid(gates_ref[:, COL_GC:COL_GC + D_MODEL])
    mix = ga * ya + gb * yb
    yield
    mix = mix + gc * yc
    x = x_in + _dot(mix.astype(BF16), wo_ref[...])
    xo_ref[...] = x
    yield
    xn = _rms(x, n2_ref[...])
    xn_ref[...] = xn
    xnb = xn.astype(BF16)
    yield
    xnb_ref[...] = xnb


def _outproj_route(live, xnb_ref, wr_ref, rb_ref, route_ref, seg_ref, cnt_ref):
    tm = TM_OUT
    logits = _dot_nt(wr_ref[...], xnb_ref[...]) + rb_ref[...]
    yield
    n_sub = EXPERTS_PER_GROUP
    sub = lax.broadcasted_iota(jnp.int32, (n_sub, tm), 0).astype(F32)
    neg = jnp.float32(-jnp.inf)
    lg = logits[0:MOE_GROUPS]
    gmax = jnp.max(lg, axis=0, keepdims=True)
    grp = jnp.min(jnp.where(lg == gmax, sub, 1e9), axis=0, keepdims=True)
    p_grp = 1.0 / jnp.sum(jnp.exp(lg - gmax), axis=0, keepdims=True)
    yield
    le = logits[MOE_GROUPS:MOE_GROUPS + n_sub]
    for g in range(1, MOE_GROUPS):
        le = jnp.where(grp == g, logits[MOE_GROUPS + g * n_sub:MOE_GROUPS + (g + 1) * n_sub], le)
    v1 = jnp.max(le, axis=0, keepdims=True)
    i1 = jnp.min(jnp.where(le == v1, sub, 1e9), axis=0, keepdims=True)
    le2 = jnp.where(sub == i1, neg, le)
    v2 = jnp.max(le2, axis=0, keepdims=True)
    i2 = jnp.min(jnp.where(le2 == v2, sub, 1e9), axis=0, keepdims=True)
    yield
    t = jnp.exp(v2 - v1)
    g1 = p_grp / (1.0 + t)
    g2 = p_grp * t / (1.0 + t)
    gate_t = jnp.where(sub == i1, g1, jnp.where(sub == i2, g2, 0.0))

    onehot_t = jnp.where(sub == grp, 1.0, 0.0)
    csum = jnp.where(live, jnp.sum(onehot_t, axis=1, keepdims=True), 0.0)
    rr = lax.broadcasted_iota(jnp.int32, (tm, tm), 0)
    cc = lax.broadcasted_iota(jnp.int32, (tm, tm), 1)
    earlier = jnp.where(rr < cc, 1.0, 0.0).astype(BF16)
    padded = jnp.concatenate([onehot_t, jnp.zeros_like(onehot_t)], axis=0).astype(BF16)
    same_before = _dot(padded, earlier)[0:n_sub]
    yield
    lower = jnp.sum(jnp.where(sub < grp, csum, 0.0), axis=0, keepdims=True)
    lrank = lower + jnp.sum(onehot_t * same_before, axis=0, keepdims=True)
    route_ref[0:n_sub, :] = gate_t
    route_ref[n_sub:, :] = jnp.broadcast_to(lrank, (n_sub, tm))
    carry = cnt_ref[...]
    lane = lax.broadcasted_iota(jnp.int32, (n_sub, LANES), 1)
    seg_ref[...] = jnp.where(lane == SEG_START, carry,
                             jnp.where(lane == SEG_LEN, csum, 0.0)).astype(jnp.int32)
    cnt_ref[...] = carry + csum


def _outproj_kernel(actp_ref, acts_ref, gates_ref, xa_ref, xb_ref, pa_ref, pb_ref, pc_ref, wo_ref,
                    n2_ref, wr_ref, rb_ref, xo_ref, xn_ref, route_ref, seg_ref, cnt_ref, xnb_ref,
                    *, n_xa):
    i = pl.program_id(0)

    @pl.when(i == 0)
    def _():
        cnt_ref[...] = jnp.zeros_like(cnt_ref)
        xnb_ref[...] = jnp.zeros_like(xnb_ref)

    tile = jnp.minimum(i, pl.num_programs(0) - 2)
    _interleave(
        _delayed(_outproj_route(i >= 1, xnb_ref, wr_ref, rb_ref, route_ref, seg_ref, cnt_ref),
                 ROUTE_DELAY),
        _outproj_main(tile, n_xa, actp_ref, acts_ref, gates_ref, xa_ref, xb_ref, pa_ref, pb_ref,
                      pc_ref, wo_ref, n2_ref, xo_ref, xn_ref, xnb_ref))


def _lane_prefix(v):
    rr = lax.broadcasted_iota(jnp.int32, (LANES, LANES), 0)
    cc = lax.broadcasted_iota(jnp.int32, (LANES, LANES), 1)
    earlier = jnp.where(rr < cc, 1.0, 0.0).astype(BF16)
    return _dot(jnp.broadcast_to(v, (8, LANES)).astype(BF16), earlier)[0:1]


def _outproj(acts_p, acts_s, z, xa, xb, pa, pb, pc, wo, n2, wr, rb, layer):
    m = T_ALL
    nt = m // TM_OUT
    row = lambda i: (jnp.minimum(i, nt - 1), 0)
    routed = lambda i: (jnp.maximum(i - 1, 0), 0, 0)
    _, act_specs = _two_part_specs(TM_OUT, acts_p, acts_s)
    n_xa, x_specs = _two_part_specs(TM_OUT, xa, xb)
    weights = (pa, pb, pc, wo, n2, wr, rb)
    return pl.pallas_call(
        functools.partial(_outproj_kernel, n_xa=n_xa),
        grid=(nt + 1,),
        in_specs=act_specs + [pl.BlockSpec((TM_OUT, 3 * D_MODEL), row)] + x_specs
        + [_layer_spec(w, layer) for w in weights],
        out_specs=[
            pl.BlockSpec((TM_OUT, D_MODEL), row),
            pl.BlockSpec((TM_OUT, D_MODEL), row),
            pl.BlockSpec((None, ROUTE_ROWS, TM_OUT), routed),
            pl.BlockSpec((None, 8, LANES), routed),
            pl.BlockSpec((8, LANES), lambda i: (0, 0)),
        ],
        scratch_shapes=[pltpu.VMEM((TM_OUT, D_MODEL), BF16)],
        out_shape=[
            jax.ShapeDtypeStruct((m, D_MODEL), F32),
            jax.ShapeDtypeStruct((m, D_MODEL), F32),
            jax.ShapeDtypeStruct((m // TM_OUT, ROUTE_ROWS, TM_OUT), F32),
            jax.ShapeDtypeStruct((m // TM_OUT, 8, LANES), jnp.int32),
            jax.ShapeDtypeStruct((8, LANES), F32),
        ],
        compiler_params=pltpu.CompilerParams(
            dimension_semantics=("arbitrary",), vmem_limit_bytes=VMEM_LIMIT),
        name="outproj",
    )(acts_p, acts_s, z, xa, xb, *weights)


PLAN_GROUP, PLAN_USED, PLAN_END, PLAN_BLOCKS, PLAN_BASE = 0, 1, 2, 3, 4


def _plan_kernel(cnt_ref, plan_ref):
    sub_i = lax.broadcasted_iota(jnp.int32, (8, LANES), 0)
    lane8_i = lax.broadcasted_iota(jnp.int32, (8, LANES), 1)
    cnt = jnp.sum(jnp.where(sub_i == lane8_i, cnt_ref[...], 0.0), axis=0, keepdims=True)
    blocks = jnp.floor((cnt + (MOE_BLK - 1)) * (1.0 / MOE_BLK))
    start = _lane_prefix(blocks)
    end = start + blocks
    lane_i = lax.broadcasted_iota(jnp.int32, (1, LANES), 1)
    lane = lane_i.astype(F32)
    grp_of_blk = jnp.zeros((1, LANES), F32)
    for g in range(MOE_GROUPS):
        end_g = jnp.sum(jnp.where(lane_i == g, end, 0.0), axis=-1, keepdims=True)
        grp_of_blk += jnp.where(lane >= end_g, 1.0, 0.0)
    grp_of_blk = jnp.minimum(grp_of_blk, MOE_GROUPS - 1)
    n_used = jnp.sum(blocks, axis=-1, keepdims=True)
    row = lax.broadcasted_iota(jnp.int32, (8, LANES), 0)
    plan_ref[...] = jnp.where(
        row == PLAN_GROUP, grp_of_blk,
        jnp.where(row == PLAN_USED, n_used,
                  jnp.where(row == PLAN_END, end,
                            jnp.where(row == PLAN_BLOCKS, blocks,
                                      jnp.where(row == PLAN_BASE, start * MOE_BLK, 0.0))))
    ).astype(jnp.int32)


def _plan(cnt):
    return pl.pallas_call(
        _plan_kernel,
        out_shape=jax.ShapeDtypeStruct((8, LANES), jnp.int32),
        name="plan",
    )(cnt)


SEG_START, SEG_LEN = 0, 1


def _segment_copies(seg_ref, base_ref, make_copy, enable=None):
    local = 0
    for g in range(MOE_GROUPS):
        n = seg_ref[g, SEG_LEN]
        first = base_ref[g] + seg_ref[g, SEG_START]
        k = TM_ROW
        while k >= 1:
            done = n & ~(2 * k - 1)
            wanted = (n & k) != 0
            @pl.when(wanted if enable is None else wanted & enable)
            def _():
                make_copy(local + done, first + done, k).start()
            k //= 2
        local = local + n
        if g % COPY_STAGE_GROUPS == COPY_STAGE_GROUPS - 1:
            yield


def _perm_matrix(route_ref):
    row = lax.broadcasted_iota(jnp.int32, (TM_ROW, TM_ROW), 0).astype(F32)
    return jnp.where(row == route_ref[ROUTE_LRANK:ROUTE_LRANK + 1, :], 1.0, 0.0).astype(BF16)


SUB = 8
SUB_X = D_MODEL // 2 // LANES
SUB_GATE = SUB_X
HI_MASK = -65536


def _rows(first_row, n_rows):
    return pl.ds(pl.multiple_of(first_row * SUB, SUB), n_rows * SUB)


def _sublane(s, n_rows):
    return pl.ds(s, n_rows, stride=SUB)


def _scatter_kernel(nu_ref, end_ref, nb_ref, base_ref, seg_ref, route_ref, xn_ref, xs_ref,
                    zbuf, sbuf, sems):
    i = pl.program_id(0)
    n = pl.num_programs(0)
    sem = sems.at[0]

    def zero_block(b):
        return pltpu.make_async_copy(zbuf, xs_ref.at[_rows(b * MOE_BLK, MOE_BLK)], sem)

    def each_unfilled_block(fn):
        for g in range(MOE_GROUPS):
            @pl.when(nb_ref[g] > 0)
            def _():
                fn(zero_block(end_ref[g] - 1))
        for b in range(T_ALL // MOE_BLK, N_BLK):
            @pl.when(b >= nu_ref[0])
            def _():
                fn(zero_block(b))

    @pl.when(i == 0)
    def _():
        zbuf[...] = jnp.zeros_like(zbuf)
        sbuf[...] = jnp.zeros_like(sbuf)
        each_unfilled_block(lambda c: c.start())
        each_unfilled_block(lambda c: c.wait())

    def tile_done(s):
        pltpu.make_async_copy(sbuf.at[s], xs_ref.at[_rows(0, TM_ROW)], sems.at[s]).wait()

    @pl.when(i >= 2)
    def _():
        for t in range(ROW_TILES):
            tile_done((i % 2) * ROW_TILES + t)

    def sort_tile(t):
        slot = (i % 2) * ROW_TILES + t
        route = route_ref.at[t]
        perm = _perm_matrix(route)
        xn = xn_ref[t * TM_ROW:(t + 1) * TM_ROW, :]
        xs = lax.bitcast_convert_type(_dot(perm, xn.astype(BF16)), jnp.int32)
        yield
        half = D_MODEL // 2
        for s in range(SUB_X):
            hi = xs[:, s * LANES:(s + 1) * LANES] & HI_MASK
            lo = lax.shift_right_logical(xs[:, half + s * LANES:half + (s + 1) * LANES], 16)
            sbuf[slot, _sublane(s, TM_ROW), :] = hi | lo
        yield
        gate_t = route[0:EXPERTS_PER_GROUP, :]
        gate_t = jnp.concatenate(
            [gate_t, jnp.zeros((LANES - EXPERTS_PER_GROUP, TM_ROW), F32)], axis=0)
        g1 = gate_t.astype(BF16)
        r1 = gate_t - g1.astype(F32)
        g2 = r1.astype(BF16)
        g3 = (r1 - g2.astype(F32)).astype(BF16)
        sbuf[slot, _sublane(SUB_GATE, TM_ROW), :] = lax.bitcast_convert_type(
            _dot_nt(perm, g1) + _dot_nt(perm, g2) + _dot_nt(perm, g3), jnp.int32)

    _interleave(*[_delayed(sort_tile(t), t) for t in range(ROW_TILES)])

    for t in range(ROW_TILES):
        def make_copy(src_row, dst_row, k, slot=(i % 2) * ROW_TILES + t):
            return pltpu.make_async_copy(sbuf.at[slot, _rows(src_row, k)],
                                         xs_ref.at[_rows(dst_row, k)], sems.at[slot])
        for _ in _segment_copies(seg_ref.at[t], base_ref, make_copy):
            pass

    @pl.when(i == n - 1)
    def _():
        for t in range(ROW_TILES):
            tile_done((i % 2) * ROW_TILES + t)

            @pl.when(n > 1)
            def _():
                tile_done((1 - i % 2) * ROW_TILES + t)


def _scatter(seg, route, xn, n_used, grp_end, grp_blocks, grp_base):
    m = xn.shape[0]
    step_rows = ROW_TILES * TM_ROW
    grid_spec = pltpu.PrefetchScalarGridSpec(
        num_scalar_prefetch=4,
        grid=(m // step_rows,),
        in_specs=[
            pl.BlockSpec((ROW_TILES, 8, LANES), lambda i, *_: (i, 0, 0), memory_space=pltpu.SMEM),
            pl.BlockSpec((ROW_TILES, ROUTE_ROWS, TM_ROW), lambda i, *_: (i, 0, 0)),
            pl.BlockSpec((step_rows, D_MODEL), lambda i, *_: (i, 0)),
        ],
        out_specs=pl.BlockSpec(memory_space=pl.ANY),
        scratch_shapes=[pltpu.VMEM((MOE_BLK * SUB, LANES), jnp.int32),
                        pltpu.VMEM((2 * ROW_TILES, TM_ROW * SUB, LANES), jnp.int32),
                        pltpu.SemaphoreType.DMA((2 * ROW_TILES,))],
    )
    return pl.pallas_call(
        _scatter_kernel,
        grid_spec=grid_spec,
        out_shape=jax.ShapeDtypeStruct((N_SORTED * SUB, LANES), jnp.int32),
        compiler_params=pltpu.CompilerParams(
            dimension_semantics=("arbitrary",), vmem_limit_bytes=VMEM_LIMIT),
        name="scatter",
    )(n_used, grp_end, grp_blocks, grp_base, seg, route, xn)


def _ffn_kernel(bg_ref, nu_ref, xs_ref, w1_ref, w3_ref, w2_ref, y_ref):
    del bg_ref
    b = pl.program_id(0)

    def block_inputs():
        packed = [xs_ref[_sublane(s, MOE_BLK), :] for s in range(SUB_X)]
        x = jnp.concatenate(
            [lax.bitcast_convert_type(u & HI_MASK, F32).astype(BF16) for u in packed]
            + [lax.bitcast_convert_type(lax.shift_left(u, 16), F32).astype(BF16) for u in packed],
            axis=1)
        return x, lax.bitcast_convert_type(xs_ref[_sublane(SUB_GATE, MOE_BLK), :], F32)

    def store(y):
        for s in range(SUB):
            y_ref[_sublane(s, MOE_BLK), :] = y[:, s * LANES:(s + 1) * LANES]

    @pl.when(b < nu_ref[0])
    def _():
        x, gates = block_inputs()
        n_e = EXPERTS_PER_GROUP
        h1 = [_dot(x, w1_ref[e]) for e in range(n_e)]
        h3 = [_dot(x, w3_ref[e]) for e in range(n_e)]
        hs = []
        for e in range(n_e):
            ge = gates[:, e:e + 1]
            hs.append(jnp.where(ge > 0.0, h1[e] * _sigmoid(h1[e]) * h3[e] * ge, 0.0).astype(BF16))
        hcat = jnp.concatenate(hs, axis=1)
        store(_dot(hcat, w2_ref[...].reshape(n_e * D_EXPERT, D_MODEL)))

    @pl.when(b >= nu_ref[0])
    def _():
        y_ref[...] = jnp.zeros_like(y_ref)


def _ffn(blk_group, n_used, xs, w1, w3, w2):
    grouped = (MOE_GROUPS, EXPERTS_PER_GROUP)
    w1, w3, w2 = (w.reshape(grouped + w.shape[1:]) for w in (w1, w3, w2))
    wmap = lambda b, bg, nu: (bg[b], 0, 0, 0)
    grid_spec = pltpu.PrefetchScalarGridSpec(
        num_scalar_prefetch=2,
        grid=(N_BLK,),
        in_specs=[
            pl.BlockSpec((MOE_BLK * SUB, LANES), lambda b, bg, nu: (b, 0)),
            pl.BlockSpec((None, EXPERTS_PER_GROUP, D_MODEL, D_EXPERT), wmap),
            pl.BlockSpec((None, EXPERTS_PER_GROUP, D_MODEL, D_EXPERT), wmap),
            pl.BlockSpec((None, EXPERTS_PER_GROUP, D_EXPERT, D_MODEL), wmap),
        ],
        out_specs=pl.BlockSpec((MOE_BLK * SUB, LANES), lambda b, bg, nu: (b, 0)),
    )
    return pl.pallas_call(
        _ffn_kernel,
        grid_spec=grid_spec,
        out_shape=jax.ShapeDtypeStruct((N_SORTED * SUB, LANES), F32),
        compiler_params=pltpu.CompilerParams(
            dimension_semantics=("arbitrary",), vmem_limit_bytes=VMEM_LIMIT),
        name="ffn",
    )(blk_group, n_used, xs, w1, w3, w2)


def _combine_kernel(base_ref, *refs, final):
    n_buf = GATHER_AHEAD + 1
    seg_refs, (route_ref, ys_hbm, x_ref, g_ref, wa_ref) = refs[:n_buf], refs[n_buf:n_buf + 5]
    outs, (buf, sem, stage) = refs[n_buf + 5:-3], refs[-3:]
    i = pl.program_id(0)
    n = pl.num_programs(0)
    step_rows = ROW_TILES * TM_ROW

    def gather(ahead, t, enable=None):
        s = ((i + ahead) % n_buf) * ROW_TILES + t

        def make_copy(buf_row, ys_row, k):
            return pltpu.make_async_copy(ys_hbm.at[_rows(ys_row, k)],
                                         buf.at[s, _rows(buf_row, k)], sem.at[s])
        return _segment_copies(seg_refs[ahead].at[t], base_ref, make_copy, enable)

    @pl.when(i == 0)
    def _():
        for ahead in range(GATHER_AHEAD):
            for t in range(ROW_TILES):
                for _ in gather(ahead, t, ahead < n):
                    pass

    for t in range(ROW_TILES):
        slot = (i % n_buf) * ROW_TILES + t
        pltpu.make_async_copy(ys_hbm.at[_rows(0, TM_ROW)], buf.at[slot], sem.at[slot]).wait()

    def tile_stages(t):
        slot = (i % n_buf) * ROW_TILES + t
        rows = slice(t * TM_ROW, (t + 1) * TM_ROW)
        tn = (((0,), (0,)), ((), ()))
        perm = _perm_matrix(route_ref.at[t])
        cols = []
        for s in range(SUB):
            hi, lo = _split_bf16(buf[slot, _sublane(s, TM_ROW), :])
            cols.append(lax.dot_general(perm, hi, tn, preferred_element_type=F32)
                        + lax.dot_general(perm, lo, tn, preferred_element_type=F32))
            if s % 4 == 3:
                yield
        y = x_ref[rows, :] + jnp.concatenate(cols, axis=1)
        if not final:
            outs[0][rows, :] = y
            yield
            _norm_and_lowrank(y, g_ref, wa_ref, outs[1].at[rows, :], outs[2].at[rows, :])
        else:
            yield
            stage[rows, :] = _rms(y, g_ref[...])

    _interleave(*[_delayed(tile_stages(t), t) for t in range(ROW_TILES)],
                *[_delayed(gather(GATHER_AHEAD, t, i + GATHER_AHEAD < n), t)
                  for t in range(ROW_TILES)])

    if final:
        @pl.when(i < T_PROMPT // step_rows)
        def _():
            outs[0][...] = stage[...]

        @pl.when(i >= T_PROMPT // step_rows)
        def _():
            outs[1][...] = stage[...]


def _combine(seg, route, grp_base, ys, x, g, wa, final):
    m = x.shape[0]
    step_rows = ROW_TILES * TM_ROW
    nt = m // step_rows
    n_p = T_PROMPT // step_rows
    smem = functools.partial(pl.BlockSpec, (ROW_TILES, 8, LANES), memory_space=pltpu.SMEM)
    tile = (step_rows, D_MODEL)
    if final:
        out_specs = [pl.BlockSpec(tile, lambda i, *_: (jnp.minimum(i, n_p - 1), 0)),
                     pl.BlockSpec(tile, lambda i, *_: (jnp.maximum(i - n_p, 0), 0))]
        out_shape = [jax.ShapeDtypeStruct((T_PROMPT, D_MODEL), F32),
                     jax.ShapeDtypeStruct((T_SAMPLE, D_MODEL), F32)]
    else:
        out_specs = [pl.BlockSpec(tile, lambda i, *_: (i, 0)),
                     pl.BlockSpec(tile, lambda i, *_: (i, 0)),
                     pl.BlockSpec((step_rows, LANES), lambda i, *_: (i, 0))]
        out_shape = [jax.ShapeDtypeStruct((m, D_MODEL), F32),
                     jax.ShapeDtypeStruct((m, D_MODEL), BF16),
                     jax.ShapeDtypeStruct((m, LANES), BF16)]
    grid_spec = pltpu.PrefetchScalarGridSpec(
        num_scalar_prefetch=1,
        grid=(nt,),
        in_specs=[smem(lambda i, *_, ahead=ahead: (jnp.minimum(i + ahead, nt - 1), 0, 0))
                  for ahead in range(GATHER_AHEAD + 1)] + [
            pl.BlockSpec((ROW_TILES, ROUTE_ROWS, TM_ROW), lambda i, *_: (i, 0, 0)),
            pl.BlockSpec(memory_space=pl.ANY),
            pl.BlockSpec(tile, lambda i, *_: (i, 0)),
            pl.BlockSpec(g.shape, lambda i, *_: (0, 0)),
            pl.BlockSpec(wa.shape, lambda i, *_: (0, 0)),
        ],
        out_specs=out_specs,
        scratch_shapes=[pltpu.VMEM(((GATHER_AHEAD + 1) * ROW_TILES, TM_ROW * SUB, LANES), F32),
                        pltpu.SemaphoreType.DMA(((GATHER_AHEAD + 1) * ROW_TILES,)),
                        pltpu.VMEM(tile if final else (8, LANES), F32)],
    )
    return pl.pallas_call(
        functools.partial(_combine_kernel, final=final),
        grid_spec=grid_spec,
        out_shape=out_shape,
        compiler_params=pltpu.CompilerParams(
            dimension_semantics=("arbitrary",), vmem_limit_bytes=VMEM_LIMIT),
        name="combine",
    )(grp_base, *[seg] * (GATHER_AHEAD + 1), route, ys, x, g, wa)


def _prep_weights(w_in, gla_a2, cm_ws, cm_b, router_group_w, router_group_b,
                  router_expert_w, router_expert_b):
    w_z = jnp.swapaxes(w_in, 1, 2).astype(BF16)
    a0 = _w_in_offsets()["alr"][0]
    w_alr = jnp.pad(w_z[:, a0:a0 + GLA_LOWRANK], ((0, 0), (0, LANES - GLA_LOWRANK), (0, 0)))
    a2 = jnp.pad(gla_a2, ((0, 0), (0, LANES - GLA_LOWRANK), (0, 0))).astype(BF16)
    ws_p = jnp.tril(cm_ws).astype(BF16)
    small = jnp.tril(cm_ws[:, :, :DEC_SEQ, :DEC_SEQ])
    eye = jnp.eye(SEQ_PER_BLK, dtype=F32)
    ws_s = jnp.einsum("ij,lgab->lgiajb", eye, small).reshape(
        DEPTH, CM_GROUPS, ROWS_S, ROWS_S).astype(BF16)
    cmb_p = jnp.broadcast_to(jnp.transpose(cm_b, (0, 2, 1))[:, :, :, None],
                             (DEPTH, CM_CHUNK, CM_GROUPS, CM_GCH)).reshape(DEPTH, CM_CHUNK, CM_CH)
    cmb_s = jnp.tile(cmb_p[:, :DEC_SEQ], (1, SEQ_PER_BLK, 1))
    pad = LANES - MOE_GROUPS - N_EXPERTS
    w_r = jnp.pad(jnp.swapaxes(jnp.concatenate([router_group_w, router_expert_w], axis=-1), 1, 2),
                  ((0, 0), (0, pad), (0, 0)))
    r_b = jnp.pad(jnp.concatenate([router_group_b, router_expert_b], axis=-1),
                  ((0, 0), (0, pad)))[:, :, None]
    return w_z, w_alr, a2, ws_p, ws_s, cmb_p, cmb_s, w_r.astype(BF16), r_b


def kernel(x_prompt, x_sample, state_conv, state_gla, norm1_g, w_in, conv_w, gla_a2, gla_a_b,
           gla_norm_g, cm_norm_g, cm_ws, cm_b, proj_a, proj_b, proj_c, w_out, norm2_g,
           router_group_w, router_group_b, router_expert_w, router_expert_b,
           exp_w1, exp_w3, exp_w2, final_norm_g):
    (w_z, w_alr, a2, ws_p, ws_s, cmb_p, cmb_s, w_r, r_b) = _prep_weights(
        w_in, gla_a2, cm_ws, cm_b, router_group_w, router_group_b, router_expert_w,
        router_expert_b)
    pa, pb, pc, wo = (w.astype(BF16) for w in (proj_a, proj_b, proj_c, w_out))
    n1 = norm1_g[:, None, :]
    n2 = norm2_g[:, None, :]
    ab = gla_a_b[:, None, :]
    gng = gla_norm_g[:, None, :]
    cmg = cm_norm_g.reshape(DEPTH, 1, CM_CH)
    fg = final_norm_g[None, :]
    xa = x_prompt.reshape(T_PROMPT, D_MODEL)
    xb = x_sample.reshape(T_SAMPLE, D_MODEL)
    gla_s = None
    conv_p, gla_p, conv_s, cmv_s = [], [], [], []
    xin, alr = _prenorm(xa, xb, n1, w_alr, 0)
    for l in range(DEPTH):
        z, *experts = _inproj(xin, w_z, (exp_w1, exp_w3, exp_w2), l)
        acts_p, nconv, ngla = _mix_prompt(z, alr, a2, ab, conv_w, gng, cmg, ws_p, cmb_p, l)
        sc = state_conv[l]
        p2 = jnp.pad(sc, ((0, 0), (0, DEC_SEQ - 2), (0, 0))).reshape(T_SAMPLE, CONV_CH)
        p1 = jnp.pad(sc[:, 1:2], ((0, 0), (0, DEC_SEQ - 1), (0, 0))).reshape(T_SAMPLE, CONV_CH)
        acts_s, cin_s, gla_s, vrows = _mix_sample(z, alr, a2, ab, conv_w, gng, cmg, ws_s, cmb_s,
                                                  p1, p2, state_gla, gla_s, l)
        conv_p.append(nconv)
        gla_p.append(ngla)
        conv_s.append(cin_s.reshape(DEC_BATCH, DEC_SEQ, CONV_CH)[:, DEC_SEQ - (CONV_K - 1):])
        cmv_s.append(vrows.reshape(DEC_BATCH, DEC_SEQ, CM_CH))

        x, xn, route, seg, cnt = _outproj(acts_p, acts_s, z, xa, xb, pa, pb, pc, wo, n2, w_r, r_b,
                                          l)
        plan = _plan(cnt)
        n_used = plan[PLAN_USED, :1]
        grp_base = plan[PLAN_BASE, :MOE_GROUPS]
        xs = _scatter(seg, route, xn, n_used, plan[PLAN_END, :MOE_GROUPS],
                      plan[PLAN_BLOCKS, :MOE_GROUPS], grp_base)
        ys = _ffn(plan[PLAN_GROUP, :N_BLK], n_used, xs, *experts)
        if l == DEPTH - 1:
            out = _combine(seg, route, grp_base, ys, x, fg, w_alr[l], True)
        else:
            xa, xin, alr = _combine(seg, route, grp_base, ys, x, n1[l + 1], w_alr[l + 1], False)
            xb = xa

    y_prompt = out[0].reshape(BATCH, SEQ, D_MODEL)
    y_sample = out[1].reshape(DEC_BATCH, DEC_SEQ, D_MODEL)
    return (y_prompt, y_sample, jnp.stack(conv_p), jnp.stack(gla_p), jnp.stack(conv_s),
            gla_s, jnp.stack(cmv_s))
```

```python
import functools

import jax
import jax.numpy as jnp
from jax import lax
from jax.experimental import pallas as pl
from jax.experimental.pallas import tpu as pltpu

F32 = jnp.float32
BF16 = jnp.bfloat16

D_MODEL = 1024
BATCH = 8
SEQ = 2048
DEPTH = 2
DEC_BATCH = 128
DEC_SEQ = 8
CONV_K = 3
CONV_CH = 512
GLA_HEADS = 4
GLA_DK = 128
GLA_DV = 256
GLA_QK = GLA_HEADS * GLA_DK
GLA_V = GLA_HEADS * GLA_DV
GLA_LOWRANK = 16
GLA_TAU = 16.0
GLA_CHUNK = 64
CM_GROUPS = 4
CM_CHUNK = 128
CM_GCH = 128
CM_CH = 512
MOE_GROUPS = 8
EXPERTS_PER_GROUP = 8
N_EXPERTS = 64
D_EXPERT = 256
EPS = 1e-6

LANES = 128
T_PROMPT = BATCH * SEQ
T_SAMPLE = DEC_BATCH * DEC_SEQ
T_ALL = T_PROMPT + T_SAMPLE

COL_GA, COL_GB, COL_GC = 0, 1024, 2048
COL_V, COL_R, COL_Q, COL_K = 3072, 4096, 5120, 5632
COL_H, COL_CG, COL_BG, COL_U, COL_VV = 6144, 6656, 7168, 7680, 8192
Z_COLS = 8704
ACT_COLS = 2048
ROUTE_ROWS = 2 * EXPERTS_PER_GROUP
ROUTE_LRANK = EXPERTS_PER_GROUP

TM_PRE = 1024
TM_IN = 1024
TN_IN = Z_COLS // 2

W_IN_WIDTHS = (("h", 512), ("cg", 512), ("bg", 512), ("q", 512), ("k", 512), ("v", 1024),
               ("r", 1024), ("alr", GLA_LOWRANK), ("u", 512), ("vv", 512), ("ga", 1024),
               ("gb", 1024), ("gc", 1024))
Z_ORDER = ("ga", "gb", "gc", "v", "r", "q", "k", "h", "cg", "bg", "u", "vv")


def _w_in_offsets():
    off, o = {}, 0
    for name, n in W_IN_WIDTHS:
        off[name] = (o, n)
        o += n
    return off


def _w_in_copies():
    off = _w_in_offsets()
    copies = [[] for _ in range(Z_COLS // TN_IN)]
    dst = 0
    for name in Z_ORDER:
        src, n = off[name]
        while n:
            tile = dst // TN_IN
            at = dst - tile * TN_IN
            take = min(n, TN_IN - at)
            runs = copies[tile]
            if runs and runs[-1][0] + runs[-1][2] == src and runs[-1][1] + runs[-1][2] == at:
                runs[-1] = (runs[-1][0], runs[-1][1], runs[-1][2] + take)
            else:
                runs.append((src, at, take))
            src, dst, n = src + take, dst + take, n - take
    assert dst == Z_COLS
    assert all(v % 16 == 0 for runs in copies for run in runs for v in run)
    return copies
TC_MIX = 256
SEQ_PER_BLK = 16
ROWS_S = SEQ_PER_BLK * DEC_SEQ
TM_OUT = 256
ROUTE_DELAY = 1
TM_ROW = TM_OUT
ROW_TILES = 4
GATHER_AHEAD = 2
COPY_STAGE_GROUPS = 4
MOE_BLK = 512
N_BLK = T_ALL // MOE_BLK + MOE_GROUPS
N_SORTED = N_BLK * MOE_BLK
VMEM_CAPACITY = 64 * 1024 * 1024
VMEM_LIMIT = 56 * 1024 * 1024
VMEM_SLACK = 4 * 1024 * 1024


def _sigmoid(x):
    return 0.5 * jnp.tanh(0.5 * x) + 0.5


def _gelu_tanh(x):
    c = 0.7978845608028654
    half = 0.5 * x
    return half + half * jnp.tanh(x * (c + (c * 0.044715) * (x * x)))


def _log_sigmoid(x):
    log2_e = 1.4426950408889634
    ln_2 = 0.6931471805599453
    return jnp.minimum(x, 0.0) - ln_2 * jnp.log2(1.0 + jnp.exp2(-log2_e * jnp.abs(x)))


def _rms(x, g):
    ms = jnp.mean(x * x, axis=-1, keepdims=True)
    return x * lax.rsqrt(ms + EPS) * g


def _split_bf16(x):
    hi = x.astype(BF16)
    lo = (x - hi.astype(F32)).astype(BF16)
    return hi, lo


def _dot(a, b):
    return jnp.dot(a, b, preferred_element_type=F32)


def _dot_nt(a, b):
    return lax.dot_general(a, b, (((1,), (1,)), ((), ())), preferred_element_type=F32)


def _layer_spec(arr, layer):
    nd = arr.ndim - 1
    return pl.BlockSpec((None,) + arr.shape[1:], lambda *g: (layer,) + (0,) * nd)


def _two_part_specs(tile, xa, xb):
    n_a = xa.shape[0] // tile
    n_b = xb.shape[0] // tile
    return n_a, [
        pl.BlockSpec((tile, xa.shape[1]), lambda i, *_: (jnp.minimum(i, n_a - 1), 0)),
        pl.BlockSpec((tile, xb.shape[1]), lambda i, *_: (jnp.clip(i - n_a, 0, n_b - 1), 0)),
    ]


def _norm_and_lowrank(x, g_ref, wa_ref, xn_ref, a_ref):
    xn = _rms(x, g_ref[...]).astype(BF16)
    xn_ref[...] = xn
    a_ref[...] = _dot_nt(xn, wa_ref[...]).astype(BF16)


def _prenorm_kernel(xa_ref, xb_ref, g_ref, wa_ref, xn_ref, a_ref, *, n_a):
    x = jnp.where(pl.program_id(0) < n_a, xa_ref[...], xb_ref[...])
    _norm_and_lowrank(x, g_ref, wa_ref, xn_ref, a_ref)


def _prenorm(xa, xb, g, wa, layer):
    m = T_ALL
    n_a, x_specs = _two_part_specs(TM_PRE, xa, xb)
    return pl.pallas_call(
        functools.partial(_prenorm_kernel, n_a=n_a),
        grid=(m // TM_PRE,),
        in_specs=x_specs + [_layer_spec(g, layer), _layer_spec(wa, layer)],
        out_specs=[pl.BlockSpec((TM_PRE, D_MODEL), lambda i: (i, 0)),
                   pl.BlockSpec((TM_PRE, LANES), lambda i: (i, 0))],
        out_shape=[jax.ShapeDtypeStruct((m, D_MODEL), BF16),
                   jax.ShapeDtypeStruct((m, LANES), BF16)],
        compiler_params=pltpu.CompilerParams(
            dimension_semantics=("arbitrary",), vmem_limit_bytes=VMEM_LIMIT),
        name="prenorm",
    )(xa, xb, g, wa)


EXPERT_SHAPES = ((D_MODEL, D_EXPERT), (D_MODEL, D_EXPERT), (D_EXPERT, D_MODEL))


def _inproj_kernel(xn_ref, w_hbm, *refs, layer):
    n = len(EXPERT_SHAPES)
    z_ref, w_buf, sem = refs[n], refs[-2], refs[-1]
    j, i = pl.program_id(0), pl.program_id(1)

    def tile_copies(tile):
        return [pltpu.make_async_copy(w_hbm.at[layer, pl.ds(src, rows)],
                                      w_buf.at[tile, pl.ds(at, rows)], sem.at[tile])
                for src, at, rows in _w_in_copies()[tile]]

    @pl.when((j == 0) & (i == 0))
    def _():
        for tile in range(w_buf.shape[0]):
            for cp in tile_copies(tile):
                cp.start()

    for tile in range(w_buf.shape[0]):
        @pl.when((j == tile) & (i == 0))
        def _():
            for cp in tile_copies(tile):
                cp.wait()

    for src, dst in zip(refs[:n], refs[n + 1:]):
        dst[...] = src[...].astype(BF16)
    z_ref[...] = _dot_nt(xn_ref[...], w_buf[j]).astype(BF16)


def _inproj(xn, w, experts, layer):
    m = xn.shape[0]
    n_j, n_i = Z_COLS // TN_IN, m // TM_IN
    per_step = -(-N_EXPERTS // (n_j * n_i))
    assert N_EXPERTS % per_step == 0
    expert = lambda j, i: jnp.minimum(j * n_i + i, N_EXPERTS // per_step - 1)
    vmem_bytes = (2 * Z_COLS * D_MODEL + 2 * 2 * TM_IN * (TN_IN + D_MODEL)
                  + 2 * per_step * (4 + 2) * sum(a * b for a, b in EXPERT_SHAPES))
    assert vmem_bytes + VMEM_SLACK <= VMEM_CAPACITY
    return pl.pallas_call(
        functools.partial(_inproj_kernel, layer=layer),
        grid=(n_j, n_i),
        in_specs=[
            pl.BlockSpec((TM_IN, D_MODEL), lambda j, i: (i, 0)),
            pl.BlockSpec(memory_space=pl.ANY),
        ] + [pl.BlockSpec((None, per_step) + s, lambda j, i: (layer, expert(j, i), 0, 0))
             for s in EXPERT_SHAPES],
        out_specs=[pl.BlockSpec((TM_IN, TN_IN), lambda j, i: (i, j))]
        + [pl.BlockSpec((per_step,) + s, lambda j, i: (expert(j, i), 0, 0))
           for s in EXPERT_SHAPES],
        out_shape=[jax.ShapeDtypeStruct((m, Z_COLS), BF16)]
        + [jax.ShapeDtypeStruct((N_EXPERTS,) + s, BF16) for s in EXPERT_SHAPES],
        scratch_shapes=[pltpu.VMEM((n_j, TN_IN, D_MODEL), BF16), pltpu.SemaphoreType.DMA((n_j,))],
        compiler_params=pltpu.CompilerParams(
            dimension_semantics=("arbitrary", "arbitrary"),
            vmem_limit_bytes=vmem_bytes + VMEM_SLACK),
        name="inproj",
    )(xn, w, *experts)


def _gla_log_decay(alr_ref, a2_ref, ab_ref):
    la = _log_sigmoid(_dot(alr_ref[...], a2_ref[...]) + ab_ref[...]) * (1.0 / GLA_TAU)
    return _split_bf16(la)


def _masked_sum(mask, la_hi, la_lo):
    m = jnp.where(mask, 1.0, 0.0).astype(BF16)
    return _dot(m, la_hi) + _dot(m, la_lo)


def _gla_decay_terms(alr_ref, a2_ref, ab_ref, tril_mask, same_mask):
    la_hi, la_lo = _gla_log_decay(alr_ref, a2_ref, ab_ref)
    return (la_hi, la_lo, _masked_sum(tril_mask, la_hi, la_lo),
            _masked_sum(same_mask, la_hi, la_lo))


def _gla_out_gate(o, g_ref, r):
    return _rms(o, g_ref[...]) * (r * _sigmoid(r))


def _chunk_mlp_group(g, bgu_ref, vv_ref, cmg_ref, ws_ref, cmb_ref, n_chunks):
    sl = slice(g * CM_GCH, (g + 1) * CM_GCH)
    ug = _gelu_tanh(bgu_ref[:, CONV_CH + g * CM_GCH:CONV_CH + (g + 1) * CM_GCH])
    vg = _rms(_gelu_tanh(vv_ref[:, sl]).astype(F32), cmg_ref[:, sl])
    vgb = vg.astype(BF16)
    rows = []
    for j in range(n_chunks):
        rs = slice(j * CM_CHUNK, (j + 1) * CM_CHUNK)
        rows.append(_dot(ws_ref[g], vgb[rs]) + cmb_ref[:, sl])
    s = rows[0] if n_chunks == 1 else jnp.concatenate(rows, axis=0)
    return ug * s, vg


def _chunk_mlp(bgu_ref, vv_ref, cmg_ref, ws_ref, cmb_ref, n_chunks):
    parts = [_chunk_mlp_group(g, bgu_ref, vv_ref, cmg_ref, ws_ref, cmb_ref, n_chunks)
             for g in range(CM_GROUPS)]
    return (jnp.concatenate([p[0] for p in parts], axis=1),
            jnp.concatenate([p[1] for p in parts], axis=1))


N_Z_VIEWS = 7
N_MIX_W = 7
SEQ_PER_STEP = 4
SEQ_STAGGER = 3


def _mix_prompt_kernel(*refs):
    n_z = N_Z_VIEWS * SEQ_PER_STEP
    weights = refs[n_z:n_z + N_MIX_W]
    acts_ref, nconv_ref, ngla_ref, st_ref, carry_ref = refs[n_z + N_MIX_W:]

    @pl.when(pl.program_id(1) == 0)
    def _():
        st_ref[...] = jnp.zeros_like(st_ref)
        carry_ref[...] = jnp.zeros_like(carry_ref)

    rr = lax.broadcasted_iota(jnp.int32, (TC_MIX, TC_MIX), 0)
    cc = lax.broadcasted_iota(jnp.int32, (TC_MIX, TC_MIX), 1)
    tril = ((rr >> 6) == (cc >> 6)) & (cc <= rr)
    masks = (tril, jnp.where(tril, 1.0, 0.0).astype(BF16),
             jnp.where(rr - cc == 1, 1.0, 0.0).astype(BF16),
             jnp.where(rr - cc == 2, 1.0, 0.0).astype(BF16))

    _interleave(*[
        _delayed(_mix_prompt_seq(*refs[N_Z_VIEWS * s:N_Z_VIEWS * (s + 1)], *weights,
                                acts_ref.at[s], nconv_ref.at[s], st_ref.at[s], carry_ref.at[s],
                                masks),
                s * SEQ_STAGGER)
        for s in range(SEQ_PER_STEP)])

    @pl.when(pl.program_id(1) == pl.num_programs(1) - 1)
    def _():
        for s in range(SEQ_PER_STEP):
            for hd in range(GLA_HEADS):
                ngla_ref[s, 0, hd] = st_ref[s, hd].T


def _mix_prompt_seq(v_ref, r_ref, qk_ref, hcg_ref, bgu_ref, vv_ref, alr_ref,
                    a2_ref, ab_ref, cw_ref, gng_ref, cmg_ref, ws_ref, cmb_ref,
                    acts_ref, nconv_ref, st_ref, carry_ref, masks):
    tc = TC_MIX

    h = hcg_ref[:, :CONV_CH].astype(F32)
    cg = hcg_ref[:, CONV_CH:].astype(F32)
    bg = bgu_ref[:, :CONV_CH].astype(F32)
    cin = cg * h
    tril, tril_01, shift_1, shift_2 = masks
    cin_b = cin.astype(BF16)
    x1 = _dot(shift_1, cin_b)
    x2 = _dot(shift_2, cin_b)
    conv = x2 * cw_ref[0:1, :] + x1 * cw_ref[1:2, :] + cin * cw_ref[2:3, :]
    c0 = carry_ref[0:1, :]
    c1 = carry_ref[1:2, :]
    row8 = lax.broadcasted_iota(jnp.int32, (8, 1), 0)
    head = jnp.where(row8 == 0, c0 * cw_ref[0:1, :] + c1 * cw_ref[1:2, :],
                     jnp.where(row8 == 1, c1 * cw_ref[0:1, :], 0.0))
    conv = jnp.concatenate([conv[0:8] + head, conv[8:]], axis=0)
    acts_ref[:, 0:CONV_CH] = (bg * conv).astype(BF16)
    carry_ref[0:2, :] = cin[tc - 2:tc, :]
    nconv_ref[0] = cin[tc - 2:tc, :]
    yield

    la_hi, la_lo = _gla_log_decay(alr_ref, a2_ref, ab_ref)
    b = _dot(tril_01, la_hi) + _dot(tril_01, la_lo)
    yield
    n_chunks = tc // GLA_CHUNK
    b_last = [b[(c + 1) * GLA_CHUNK - 1:(c + 1) * GLA_CHUNK, :] for c in range(n_chunks)]
    bl = jnp.concatenate([jnp.broadcast_to(r_, (GLA_CHUNK, GLA_QK)) for r_ in b_last], axis=0)
    q = qk_ref[:, :GLA_QK].astype(F32) * (GLA_DK ** -0.5)
    k = qk_ref[:, GLA_QK:].astype(F32)
    q_t = (q * jnp.exp(b)).astype(BF16)
    k_t = (k * jnp.exp(-b)).astype(BF16)
    k_end = (k * jnp.exp(bl - b)).astype(BF16)
    yield
    states = [st_ref[hd] for hd in range(GLA_HEADS)]
    k_cols = [slice(hd * GLA_DK, (hd + 1) * GLA_DK) for hd in range(GLA_HEADS)]
    v_cols = [slice(hd * GLA_DV, (hd + 1) * GLA_DV) for hd in range(GLA_HEADS)]
    o_intra = []
    for hd in range(GLA_HEADS):
        att = jnp.where(tril, _dot_nt(q_t[:, k_cols[hd]], k_t[:, k_cols[hd]]), 0.0).astype(BF16)
        o_intra.append(_dot(att, v_ref[:, v_cols[hd]]))
    yield

    o_rows = [[] for _ in range(GLA_HEADS)]
    assert n_chunks == CM_GROUPS
    for c in range(n_chunks):
        rs = slice(c * GLA_CHUNK, (c + 1) * GLA_CHUNK)
        for hd in range(GLA_HEADS):
            ks = k_cols[hd]
            st = states[hd]
            o_rows[hd].append(o_intra[hd][rs] + _dot_nt(q_t[rs, ks], st.astype(BF16)))
            upd = lax.dot_general(v_ref[rs, v_cols[hd]], k_end[rs, ks], (((0,), (0,)), ((), ())),
                                  preferred_element_type=F32)
            states[hd] = jnp.exp(b_last[c][:, ks]) * st + upd
        us, _ = _chunk_mlp_group(c, bgu_ref, vv_ref, cmg_ref, ws_ref, cmb_ref, tc // CM_CHUNK)
        acts_ref[:, CONV_CH + GLA_V + c * CM_GCH:CONV_CH + GLA_V + (c + 1) * CM_GCH] = (
            us.astype(BF16))
        yield

    for hd in range(GLA_HEADS):
        o = jnp.concatenate(o_rows[hd], axis=0)
        r = r_ref[:, v_cols[hd]]
        acts_ref[:, CONV_CH + hd * GLA_DV:CONV_CH + (hd + 1) * GLA_DV] = (
            _gla_out_gate(o, gng_ref, r).astype(BF16))
        st_ref[hd] = states[hd]
        if hd % 2 == 1:
            yield


def _z_specs(rows, row_map):
    def spec(width, col):
        blk = col // width
        return pl.BlockSpec((rows, width), lambda *g: (row_map(*g), blk))
    return [spec(1024, COL_V), spec(1024, COL_R), spec(1024, COL_Q), spec(1024, COL_H),
            spec(1024, COL_BG), spec(512, COL_VV)]


def _mix_prompt(z, alr, a2, ab, cw, gng, cmg, ws, cmb, layer):
    nt = SEQ // TC_MIX
    nb = BATCH // SEQ_PER_STEP
    small = (a2, ab, cw, gng, cmg, ws, cmb)
    assert len(small) == N_MIX_W
    in_specs, args = [], []
    for s in range(SEQ_PER_STEP):
        row_map = lambda b, c, s=s: (b + s * nb) * nt + c
        in_specs += _z_specs(TC_MIX, row_map) + [
            pl.BlockSpec((TC_MIX, LANES), lambda b, c, row_map=row_map: (row_map(b, c), 0))]
        args += [z] * (N_Z_VIEWS - 1) + [alr]
    in_specs += [_layer_spec(a, layer) for a in small]
    acts, nconv, ngla = pl.pallas_call(
        _mix_prompt_kernel,
        grid=(nb, nt),
        in_specs=in_specs,
        out_specs=[
            pl.BlockSpec((SEQ_PER_STEP, TC_MIX, ACT_COLS), lambda b, c: (0, b * nt + c, 0)),
            pl.BlockSpec((SEQ_PER_STEP, 1, CONV_K - 1, CONV_CH), lambda b, c: (0, b, 0, 0)),
            pl.BlockSpec((SEQ_PER_STEP, 1, GLA_HEADS, GLA_DK, GLA_DV),
                         lambda b, c: (0, b, 0, 0, 0)),
        ],
        out_shape=[
            jax.ShapeDtypeStruct((SEQ_PER_STEP, T_PROMPT // SEQ_PER_STEP, ACT_COLS), BF16),
            jax.ShapeDtypeStruct((SEQ_PER_STEP, nb, CONV_K - 1, CONV_CH), F32),
            jax.ShapeDtypeStruct((SEQ_PER_STEP, nb, GLA_HEADS, GLA_DK, GLA_DV), F32),
        ],
        scratch_shapes=[pltpu.VMEM((SEQ_PER_STEP, GLA_HEADS, GLA_DV, GLA_DK), F32),
                        pltpu.VMEM((SEQ_PER_STEP, 8, CONV_CH), F32)],
        compiler_params=pltpu.CompilerParams(
            dimension_semantics=("arbitrary", "arbitrary"), vmem_limit_bytes=VMEM_LIMIT),
        name="mix_prompt",
    )(*args, *small)
    return (acts.reshape(T_PROMPT, ACT_COLS), nconv.reshape(BATCH, CONV_K - 1, CONV_CH),
            ngla.reshape(BATCH, GLA_HEADS, GLA_DK, GLA_DV))


def _mix_sample_body(v_ref, r_ref, qk_ref, hcg_ref, bgu_ref, vv_ref, alr_ref,
                       a2_ref, ab_ref, cw_ref, gng_ref, cmg_ref, ws_ref, cmb_ref,
                       p1_ref, p2_ref, s0_ref,
                       acts_ref, cin_ref, ns_ref, vrow_ref):
    n = ROWS_S

    pos = lax.broadcasted_iota(jnp.int32, (n, 1), 0) & (DEC_SEQ - 1)
    h = hcg_ref[:, :CONV_CH].astype(F32)
    cg = hcg_ref[:, CONV_CH:].astype(F32)
    bg = bgu_ref[:, :CONV_CH].astype(F32)
    cin = cg * h
    x1 = jnp.where(pos >= 1, pltpu.roll(cin, 1, 0), p1_ref[...])
    x2 = jnp.where(pos >= 2, pltpu.roll(cin, 2, 0), p2_ref[...])
    conv = x2 * cw_ref[0:1, :] + x1 * cw_ref[1:2, :] + cin * cw_ref[2:3, :]
    acts_ref[:, 0:CONV_CH] = (bg * conv).astype(BF16)
    cin_ref[...] = cin

    rr = lax.broadcasted_iota(jnp.int32, (n, n), 0)
    cc = lax.broadcasted_iota(jnp.int32, (n, n), 1)
    same = (rr >> 3) == (cc >> 3)
    tril = same & (cc <= rr)
    la_hi, la_lo, b, bl = _gla_decay_terms(alr_ref, a2_ref, ab_ref, tril, same)
    q = qk_ref[:, :GLA_QK].astype(F32) * (GLA_DK ** -0.5)
    k = qk_ref[:, GLA_QK:].astype(F32)
    q_t = (q * jnp.exp(b)).astype(BF16)
    k_t = (k * jnp.exp(-b)).astype(BF16)
    k_end = k * jnp.exp(bl - b)
    la_hi = la_hi.astype(F32)
    la_lo = la_lo.astype(F32)
    row_seq = lax.broadcasted_iota(jnp.int32, (n, GLA_DK), 0) >> 3
    seq3 = lax.broadcasted_iota(jnp.int32, (SEQ_PER_BLK, GLA_DK, n), 0)
    lane_seq3 = lax.broadcasted_iota(jnp.int32, (SEQ_PER_BLK, GLA_DK, n), 2) >> 3
    mask3 = seq3 == lane_seq3
    ones = jnp.ones((n, GLA_DV), BF16)
    big = SEQ_PER_BLK * GLA_DK

    def per_seq(x_tr):
        x3 = jnp.where(mask3, x_tr[None, :, :], 0.0)
        return x3.reshape(big, n).astype(BF16)

    for hd in range(GLA_HEADS):
        ks = slice(hd * GLA_DK, (hd + 1) * GLA_DK)
        vs = slice(hd * GLA_DV, (hd + 1) * GLA_DV)
        qh = q_t[:, ks]
        vh = v_ref[:, vs]
        att = jnp.where(tril, _dot_nt(qh, k_t[:, ks]), 0.0).astype(BF16)
        o_intra = _dot(att, vh)
        s_old = s0_ref[:, hd].reshape(big, GLA_DV)
        zero = jnp.zeros_like(qh)
        q_big = jnp.concatenate(
            [jnp.where(row_seq == j, qh, zero) for j in range(SEQ_PER_BLK)], axis=1)
        o = o_intra + _dot(q_big, s_old.astype(BF16))
        dlog = _dot(per_seq(la_hi[:, ks].T), ones) + _dot(per_seq(la_lo[:, ks].T), ones)
        upd = _dot(per_seq(k_end[:, ks].T), vh)
        s_new = jnp.exp(dlog) * s_old + upd
        ns_ref[:, hd] = s_new.reshape(SEQ_PER_BLK, GLA_DK, GLA_DV)
        r = r_ref[:, vs]
        acts_ref[:, CONV_CH + hd * GLA_DV:CONV_CH + (hd + 1) * GLA_DV] = (
            _gla_out_gate(o, gng_ref, r).astype(BF16))

    us, vg = _chunk_mlp(bgu_ref, vv_ref, cmg_ref, ws_ref, cmb_ref, 1)
    acts_ref[:, CONV_CH + GLA_V:] = us.astype(BF16)
    vrow_ref[...] = vg


N_MIX_S_IN = 17


def _mix_sample_kernel(*refs, layer):
    if layer == 0:
        @pl.when(pl.program_id(0) == 0)
        def _():
            _mix_sample_body(*refs)

        @pl.when(pl.program_id(0) > 0)
        def _():
            ns_ref = refs[N_MIX_S_IN + 2]
            ns_ref[...] = jnp.zeros_like(ns_ref)
    else:
        _mix_sample_body(*refs[:N_MIX_S_IN], *refs[N_MIX_S_IN + 1:])


def _mix_sample(z, alr, a2, ab, cw, gng, cmg, ws, cmb, p1, p2, s0, ns_all, layer):
    row0 = T_PROMPT // ROWS_S
    n_i = DEC_BATCH // SEQ_PER_BLK
    n_pass = DEPTH if layer == 0 else 1
    blk = lambda p, i: jnp.where(p == 0, i, n_i - 1)
    row_map = lambda p, i: row0 + blk(p, i)
    slot = lambda p, i: (layer + p, i, 0, 0, 0)
    small = (a2, ab, cw, gng, cmg, ws, cmb)
    state_blk = (None, SEQ_PER_BLK, GLA_HEADS, GLA_DK, GLA_DV)
    in_specs = _z_specs(ROWS_S, row_map) + [
        pl.BlockSpec((ROWS_S, LANES), lambda p, i: (row_map(p, i), 0)),
    ] + [_layer_spec(a, layer) for a in small] + [
        pl.BlockSpec((ROWS_S, CONV_CH), lambda p, i: (blk(p, i), 0)),
        pl.BlockSpec((ROWS_S, CONV_CH), lambda p, i: (blk(p, i), 0)),
        pl.BlockSpec(state_blk, lambda p, i: (layer, blk(p, i), 0, 0, 0)),
    ]
    args = (z, z, z, z, z, z, alr, *small, p1, p2, s0)
    assert len(args) == N_MIX_S_IN
    aliases = {}
    if layer > 0:
        in_specs.append(pl.BlockSpec(memory_space=pl.ANY))
        args += (ns_all,)
        aliases = {N_MIX_S_IN: 2}
    return pl.pallas_call(
        functools.partial(_mix_sample_kernel, layer=layer),
        grid=(n_pass, n_i),
        in_specs=in_specs,
        out_specs=[
            pl.BlockSpec((ROWS_S, ACT_COLS), lambda p, i: (blk(p, i), 0)),
            pl.BlockSpec((ROWS_S, CONV_CH), lambda p, i: (blk(p, i), 0)),
            pl.BlockSpec(state_blk, slot),
            pl.BlockSpec((ROWS_S, CM_CH), lambda p, i: (blk(p, i), 0)),
        ],
        out_shape=[
            jax.ShapeDtypeStruct((T_SAMPLE, ACT_COLS), BF16),
            jax.ShapeDtypeStruct((T_SAMPLE, CONV_CH), F32),
            jax.ShapeDtypeStruct((DEPTH, DEC_BATCH, GLA_HEADS, GLA_DK, GLA_DV), F32),
            jax.ShapeDtypeStruct((T_SAMPLE, CM_CH), F32),
        ],
        input_output_aliases=aliases,
        compiler_params=pltpu.CompilerParams(
            dimension_semantics=("arbitrary", "arbitrary"), vmem_limit_bytes=VMEM_LIMIT),
        name="mix_sample",
    )(*args)


def _delayed(gen, n_stages):
    for _ in range(n_stages):
        yield
    yield from gen


def _interleave(*stage_lists):
    pending = list(stage_lists)
    while pending:
        for gen in list(pending):
            if next(gen, StopIteration) is StopIteration:
                pending.remove(gen)


def _outproj_main(i, n_xa, actp_ref, acts_ref, gates_ref, xa_ref, xb_ref, pa_ref, pb_ref, pc_ref,
                  wo_ref, n2_ref, xo_ref, xn_ref, xnb_ref):
    acts = jnp.where(i < T_PROMPT // TM_OUT, actp_ref[...], acts_ref[...])
    x_in = jnp.where(i < n_xa, xa_ref[...], xb_ref[...])
    yb = _dot(acts[:, CONV_CH:CONV_CH + GLA_V], pb_ref[...])
    ga = _sigmoid(gates_ref[:, COL_GA:COL_GA + D_MODEL])
    ya = _dot(acts[:, :CONV_CH], pa_ref[...])
    gb = _sigmoid(gates_ref[:, COL_GB:COL_GB + D_MODEL])
    yield
    yc = _dot(acts[:, CONV_CH + GLA_V:], pc_ref[...])
    gc = _sigmoid(gates_ref[:, COL_GC:COL_GC + D_MODEL])
    mix = ga * ya + gb * yb
    yield
    mix = mix + gc * yc
    x = x_in + _dot(mix.astype(BF16), wo_ref[...])
    xo_ref[...] = x
    yield
    xn = _rms(x, n2_ref[...])
    xn_ref[...] = xn
    xnb = xn.astype(BF16)
    yield
    xnb_ref[...] = xnb


def _outproj_route(live, xnb_ref, wr_ref, rb_ref, route_ref, seg_ref, cnt_ref):
    tm = TM_OUT
    logits = _dot_nt(wr_ref[...], xnb_ref[...]) + rb_ref[...]
    yield
    n_sub = EXPERTS_PER_GROUP
    sub = lax.broadcasted_iota(jnp.int32, (n_sub, tm), 0).astype(F32)
    neg = jnp.float32(-jnp.inf)
    lg = logits[0:MOE_GROUPS]
    gmax = jnp.max(lg, axis=0, keepdims=True)
    grp = jnp.min(jnp.where(lg == gmax, sub, 1e9), axis=0, keepdims=True)
    p_grp = 1.0 / jnp.sum(jnp.exp(lg - gmax), axis=0, keepdims=True)
    yield
    le = logits[MOE_GROUPS:MOE_GROUPS + n_sub]
    for g in range(1, MOE_GROUPS):
        le = jnp.where(grp == g, logits[MOE_GROUPS + g * n_sub:MOE_GROUPS + (g + 1) * n_sub], le)
    v1 = jnp.max(le, axis=0, keepdims=True)
    i1 = jnp.min(jnp.where(le == v1, sub, 1e9), axis=0, keepdims=True)
    le2 = jnp.where(sub == i1, neg, le)
    v2 = jnp.max(le2, axis=0, keepdims=True)
    i2 = jnp.min(jnp.where(le2 == v2, sub, 1e9), axis=0, keepdims=True)
    yield
    t = jnp.exp(v2 - v1)
    g1 = p_grp / (1.0 + t)
    g2 = p_grp * t / (1.0 + t)
    gate_t = jnp.where(sub == i1, g1, jnp.where(sub == i2, g2, 0.0))

    onehot_t = jnp.where(sub == grp, 1.0, 0.0)
    csum = jnp.where(live, jnp.sum(onehot_t, axis=1, keepdims=True), 0.0)
    rr = lax.broadcasted_iota(jnp.int32, (tm, tm), 0)
    cc = lax.broadcasted_iota(jnp.int32, (tm, tm), 1)
    earlier = jnp.where(rr < cc, 1.0, 0.0).astype(BF16)
    padded = jnp.concatenate([onehot_t, jnp.zeros_like(onehot_t)], axis=0).astype(BF16)
    same_before = _dot(padded, earlier)[0:n_sub]
    yield
    lower = jnp.sum(jnp.where(sub < grp, csum, 0.0), axis=0, keepdims=True)
    lrank = lower + jnp.sum(onehot_t * same_before, axis=0, keepdims=True)
    route_ref[0:n_sub, :] = gate_t
    route_ref[n_sub:, :] = jnp.broadcast_to(lrank, (n_sub, tm))
    carry = cnt_ref[...]
    lane = lax.broadcasted_iota(jnp.int32, (n_sub, LANES), 1)
    seg_ref[...] = jnp.where(lane == SEG_START, carry,
                             jnp.where(lane == SEG_LEN, csum, 0.0)).astype(jnp.int32)
    cnt_ref[...] = carry + csum


def _outproj_kernel(actp_ref, acts_ref, gates_ref, xa_ref, xb_ref, pa_ref, pb_ref, pc_ref, wo_ref,
                    n2_ref, wr_ref, rb_ref, xo_ref, xn_ref, route_ref, seg_ref, cnt_ref, xnb_ref,
                    *, n_xa):
    i = pl.program_id(0)

    @pl.when(i == 0)
    def _():
        cnt_ref[...] = jnp.zeros_like(cnt_ref)
        xnb_ref[...] = jnp.zeros_like(xnb_ref)

    tile = jnp.minimum(i, pl.num_programs(0) - 2)
    _interleave(
        _delayed(_outproj_route(i >= 1, xnb_ref, wr_ref, rb_ref, route_ref, seg_ref, cnt_ref),
                 ROUTE_DELAY),
        _outproj_main(tile, n_xa, actp_ref, acts_ref, gates_ref, xa_ref, xb_ref, pa_ref, pb_ref,
                      pc_ref, wo_ref, n2_ref, xo_ref, xn_ref, xnb_ref))


def _lane_prefix(v):
    rr = lax.broadcasted_iota(jnp.int32, (LANES, LANES), 0)
    cc = lax.broadcasted_iota(jnp.int32, (LANES, LANES), 1)
    earlier = jnp.where(rr < cc, 1.0, 0.0).astype(BF16)
    return _dot(jnp.broadcast_to(v, (8, LANES)).astype(BF16), earlier)[0:1]


def _outproj(acts_p, acts_s, z, xa, xb, pa, pb, pc, wo, n2, wr, rb, layer):
    m = T_ALL
    nt = m // TM_OUT
    row = lambda i: (jnp.minimum(i, nt - 1), 0)
    routed = lambda i: (jnp.maximum(i - 1, 0), 0, 0)
    _, act_specs = _two_part_specs(TM_OUT, acts_p, acts_s)
    n_xa, x_specs = _two_part_specs(TM_OUT, xa, xb)
    weights = (pa, pb, pc, wo, n2, wr, rb)
    return pl.pallas_call(
        functools.partial(_outproj_kernel, n_xa=n_xa),
        grid=(nt + 1,),
        in_specs=act_specs + [pl.BlockSpec((TM_OUT, 3 * D_MODEL), row)] + x_specs
        + [_layer_spec(w, layer) for w in weights],
        out_specs=[
            pl.BlockSpec((TM_OUT, D_MODEL), row),
            pl.BlockSpec((TM_OUT, D_MODEL), row),
            pl.BlockSpec((None, ROUTE_ROWS, TM_OUT), routed),
            pl.BlockSpec((None, 8, LANES), routed),
            pl.BlockSpec((8, LANES), lambda i: (0, 0)),
        ],
        scratch_shapes=[pltpu.VMEM((TM_OUT, D_MODEL), BF16)],
        out_shape=[
            jax.ShapeDtypeStruct((m, D_MODEL), F32),
            jax.ShapeDtypeStruct((m, D_MODEL), F32),
            jax.ShapeDtypeStruct((m // TM_OUT, ROUTE_ROWS, TM_OUT), F32),
            jax.ShapeDtypeStruct((m // TM_OUT, 8, LANES), jnp.int32),
            jax.ShapeDtypeStruct((8, LANES), F32),
        ],
        compiler_params=pltpu.CompilerParams(
            dimension_semantics=("arbitrary",), vmem_limit_bytes=VMEM_LIMIT),
        name="outproj",
    )(acts_p, acts_s, z, xa, xb, *weights)


PLAN_GROUP, PLAN_USED, PLAN_END, PLAN_BLOCKS, PLAN_BASE = 0, 1, 2, 3, 4


def _plan_kernel(cnt_ref, plan_ref):
    sub_i = lax.broadcasted_iota(jnp.int32, (8, LANES), 0)
    lane8_i = lax.broadcasted_iota(jnp.int32, (8, LANES), 1)
    cnt = jnp.sum(jnp.where(sub_i == lane8_i, cnt_ref[...], 0.0), axis=0, keepdims=True)
    blocks = jnp.floor((cnt + (MOE_BLK - 1)) * (1.0 / MOE_BLK))
    start = _lane_prefix(blocks)
    end = start + blocks
    lane_i = lax.broadcasted_iota(jnp.int32, (1, LANES), 1)
    lane = lane_i.astype(F32)
    grp_of_blk = jnp.zeros((1, LANES), F32)
    for g in range(MOE_GROUPS):
        end_g = jnp.sum(jnp.where(lane_i == g, end, 0.0), axis=-1, keepdims=True)
        grp_of_blk += jnp.where(lane >= end_g, 1.0, 0.0)
    grp_of_blk = jnp.minimum(grp_of_blk, MOE_GROUPS - 1)
    n_used = jnp.sum(blocks, axis=-1, keepdims=True)
    row = lax.broadcasted_iota(jnp.int32, (8, LANES), 0)
    plan_ref[...] = jnp.where(
        row == PLAN_GROUP, grp_of_blk,
        jnp.where(row == PLAN_USED, n_used,
                  jnp.where(row == PLAN_END, end,
                            jnp.where(row == PLAN_BLOCKS, blocks,
                                      jnp.where(row == PLAN_BASE, start * MOE_BLK, 0.0))))
    ).astype(jnp.int32)


def _plan(cnt):
    return pl.pallas_call(
        _plan_kernel,
        out_shape=jax.ShapeDtypeStruct((8, LANES), jnp.int32),
        name="plan",
    )(cnt)


SEG_START, SEG_LEN = 0, 1


def _segment_copies(seg_ref, base_ref, make_copy, enable=None):
    local = 0
    for g in range(MOE_GROUPS):
        n = seg_ref[g, SEG_LEN]
        first = base_ref[g] + seg_ref[g, SEG_START]
        k = TM_ROW
        while k >= 1:
            done = n & ~(2 * k - 1)
            wanted = (n & k) != 0
            @pl.when(wanted if enable is None else wanted & enable)
            def _():
                make_copy(local + done, first + done, k).start()
            k //= 2
        local = local + n
        if g % COPY_STAGE_GROUPS == COPY_STAGE_GROUPS - 1:
            yield


def _perm_matrix(route_ref):
    row = lax.broadcasted_iota(jnp.int32, (TM_ROW, TM_ROW), 0).astype(F32)
    return jnp.where(row == route_ref[ROUTE_LRANK:ROUTE_LRANK + 1, :], 1.0, 0.0).astype(BF16)


SUB = 8
SUB_X = D_MODEL // 2 // LANES
SUB_GATE = SUB_X
HI_MASK = -65536


def _rows(first_row, n_rows):
    return pl.ds(pl.multiple_of(first_row * SUB, SUB), n_rows * SUB)


def _sublane(s, n_rows):
    return pl.ds(s, n_rows, stride=SUB)


def _scatter_kernel(nu_ref, end_ref, nb_ref, base_ref, seg_ref, route_ref, xn_ref, xs_ref,
                    zbuf, sbuf, sems):
    i = pl.program_id(0)
    n = pl.num_programs(0)
    sem = sems.at[0]

    def zero_block(b):
        return pltpu.make_async_copy(zbuf, xs_ref.at[_rows(b * MOE_BLK, MOE_BLK)], sem)

    def each_unfilled_block(fn):
        for g in range(MOE_GROUPS):
            @pl.when(nb_ref[g] > 0)
            def _():
                fn(zero_block(end_ref[g] - 1))
        for b in range(T_ALL // MOE_BLK, N_BLK):
            @pl.when(b >= nu_ref[0])
            def _():
                fn(zero_block(b))

    @pl.when(i == 0)
    def _():
        zbuf[...] = jnp.zeros_like(zbuf)
        sbuf[...] = jnp.zeros_like(sbuf)
        each_unfilled_block(lambda c: c.start())
        each_unfilled_block(lambda c: c.wait())

    def tile_done(s):
        pltpu.make_async_copy(sbuf.at[s], xs_ref.at[_rows(0, TM_ROW)], sems.at[s]).wait()

    @pl.when(i >= 2)
    def _():
        for t in range(ROW_TILES):
            tile_done((i % 2) * ROW_TILES + t)

    def sort_tile(t):
        slot = (i % 2) * ROW_TILES + t
        route = route_ref.at[t]
        perm = _perm_matrix(route)
        xn = xn_ref[t * TM_ROW:(t + 1) * TM_ROW, :]
        xs = lax.bitcast_convert_type(_dot(perm, xn.astype(BF16)), jnp.int32)
        yield
        half = D_MODEL // 2
        for s in range(SUB_X):
            hi = xs[:, s * LANES:(s + 1) * LANES] & HI_MASK
            lo = lax.shift_right_logical(xs[:, half + s * LANES:half + (s + 1) * LANES], 16)
            sbuf[slot, _sublane(s, TM_ROW), :] = hi | lo
        yield
        gate_t = route[0:EXPERTS_PER_GROUP, :]
        gate_t = jnp.concatenate(
            [gate_t, jnp.zeros((LANES - EXPERTS_PER_GROUP, TM_ROW), F32)], axis=0)
        g1 = gate_t.astype(BF16)
        r1 = gate_t - g1.astype(F32)
        g2 = r1.astype(BF16)
        g3 = (r1 - g2.astype(F32)).astype(BF16)
        sbuf[slot, _sublane(SUB_GATE, TM_ROW), :] = lax.bitcast_convert_type(
            _dot_nt(perm, g1) + _dot_nt(perm, g2) + _dot_nt(perm, g3), jnp.int32)

    _interleave(*[_delayed(sort_tile(t), t) for t in range(ROW_TILES)])

    for t in range(ROW_TILES):
        def make_copy(src_row, dst_row, k, slot=(i % 2) * ROW_TILES + t):
            return pltpu.make_async_copy(sbuf.at[slot, _rows(src_row, k)],
                                         xs_ref.at[_rows(dst_row, k)], sems.at[slot])
        for _ in _segment_copies(seg_ref.at[t], base_ref, make_copy):
            pass

    @pl.when(i == n - 1)
    def _():
        for t in range(ROW_TILES):
            tile_done((i % 2) * ROW_TILES + t)

            @pl.when(n > 1)
            def _():
                tile_done((1 - i % 2) * ROW_TILES + t)


def _scatter(seg, route, xn, n_used, grp_end, grp_blocks, grp_base):
    m = xn.shape[0]
    step_rows = ROW_TILES * TM_ROW
    grid_spec = pltpu.PrefetchScalarGridSpec(
        num_scalar_prefetch=4,
        grid=(m // step_rows,),
        in_specs=[
            pl.BlockSpec((ROW_TILES, 8, LANES), lambda i, *_: (i, 0, 0), memory_space=pltpu.SMEM),
            pl.BlockSpec((ROW_TILES, ROUTE_ROWS, TM_ROW), lambda i, *_: (i, 0, 0)),
            pl.BlockSpec((step_rows, D_MODEL), lambda i, *_: (i, 0)),
        ],
        out_specs=pl.BlockSpec(memory_space=pl.ANY),
        scratch_shapes=[pltpu.VMEM((MOE_BLK * SUB, LANES), jnp.int32),
                        pltpu.VMEM((2 * ROW_TILES, TM_ROW * SUB, LANES), jnp.int32),
                        pltpu.SemaphoreType.DMA((2 * ROW_TILES,))],
    )
    return pl.pallas_call(
        _scatter_kernel,
        grid_spec=grid_spec,
        out_shape=jax.ShapeDtypeStruct((N_SORTED * SUB, LANES), jnp.int32),
        compiler_params=pltpu.CompilerParams(
            dimension_semantics=("arbitrary",), vmem_limit_bytes=VMEM_LIMIT),
        name="scatter",
    )(n_used, grp_end, grp_blocks, grp_base, seg, route, xn)


def _ffn_kernel(bg_ref, nu_ref, xs_ref, w1_ref, w3_ref, w2_ref, y_ref):
    del bg_ref
    b = pl.program_id(0)

    def block_inputs():
        packed = [xs_ref[_sublane(s, MOE_BLK), :] for s in range(SUB_X)]
        x = jnp.concatenate(
            [lax.bitcast_convert_type(u & HI_MASK, F32).astype(BF16) for u in packed]
            + [lax.bitcast_convert_type(lax.shift_left(u, 16), F32).astype(BF16) for u in packed],
            axis=1)
        return x, lax.bitcast_convert_type(xs_ref[_sublane(SUB_GATE, MOE_BLK), :], F32)

    def store(y):
        for s in range(SUB):
            y_ref[_sublane(s, MOE_BLK), :] = y[:, s * LANES:(s + 1) * LANES]

    @pl.when(b < nu_ref[0])
    def _():
        x, gates = block_inputs()
        n_e = EXPERTS_PER_GROUP
        h1 = [_dot(x, w1_ref[e]) for e in range(n_e)]
        h3 = [_dot(x, w3_ref[e]) for e in range(n_e)]
        hs = []
        for e in range(n_e):
            ge = gates[:, e:e + 1]
            hs.append(jnp.where(ge > 0.0, h1[e] * _sigmoid(h1[e]) * h3[e] * ge, 0.0).astype(BF16))
        hcat = jnp.concatenate(hs, axis=1)
        store(_dot(hcat, w2_ref[...].reshape(n_e * D_EXPERT, D_MODEL)))

    @pl.when(b >= nu_ref[0])
    def _():
        y_ref[...] = jnp.zeros_like(y_ref)


def _ffn(blk_group, n_used, xs, w1, w3, w2):
    grouped = (MOE_GROUPS, EXPERTS_PER_GROUP)
    w1, w3, w2 = (w.reshape(grouped + w.shape[1:]) for w in (w1, w3, w2))
    wmap = lambda b, bg, nu: (bg[b], 0, 0, 0)
    grid_spec = pltpu.PrefetchScalarGridSpec(
        num_scalar_prefetch=2,
        grid=(N_BLK,),
        in_specs=[
            pl.BlockSpec((MOE_BLK * SUB, LANES), lambda b, bg, nu: (b, 0)),
            pl.BlockSpec((None, EXPERTS_PER_GROUP, D_MODEL, D_EXPERT), wmap),
            pl.BlockSpec((None, EXPERTS_PER_GROUP, D_MODEL, D_EXPERT), wmap),
            pl.BlockSpec((None, EXPERTS_PER_GROUP, D_EXPERT, D_MODEL), wmap),
        ],
        out_specs=pl.BlockSpec((MOE_BLK * SUB, LANES), lambda b, bg, nu: (b, 0)),
    )
    return pl.pallas_call(
        _ffn_kernel,
        grid_spec=grid_spec,
        out_shape=jax.ShapeDtypeStruct((N_SORTED * SUB, LANES), F32),
        compiler_params=pltpu.CompilerParams(
            dimension_semantics=("arbitrary",), vmem_limit_bytes=VMEM_LIMIT),
        name="ffn",
    )(blk_group, n_used, xs, w1, w3, w2)


def _combine_kernel(base_ref, *refs, final):
    n_buf = GATHER_AHEAD + 1
    seg_refs, (route_ref, ys_hbm, x_ref, g_ref, wa_ref) = refs[:n_buf], refs[n_buf:n_buf + 5]
    outs, (buf, sem, stage) = refs[n_buf + 5:-3], refs[-3:]
    i = pl.program_id(0)
    n = pl.num_programs(0)
    step_rows = ROW_TILES * TM_ROW

    def gather(ahead, t, enable=None):
        s = ((i + ahead) % n_buf) * ROW_TILES + t

        def make_copy(buf_row, ys_row, k):
            return pltpu.make_async_copy(ys_hbm.at[_rows(ys_row, k)],
                                         buf.at[s, _rows(buf_row, k)], sem.at[s])
        return _segment_copies(seg_refs[ahead].at[t], base_ref, make_copy, enable)

    @pl.when(i == 0)
    def _():
        for ahead in range(GATHER_AHEAD):
            for t in range(ROW_TILES):
                for _ in gather(ahead, t, ahead < n):
                    pass

    for t in range(ROW_TILES):
        slot = (i % n_buf) * ROW_TILES + t
        pltpu.make_async_copy(ys_hbm.at[_rows(0, TM_ROW)], buf.at[slot], sem.at[slot]).wait()

    def tile_stages(t):
        slot = (i % n_buf) * ROW_TILES + t
        rows = slice(t * TM_ROW, (t + 1) * TM_ROW)
        tn = (((0,), (0,)), ((), ()))
        perm = _perm_matrix(route_ref.at[t])
        cols = []
        for s in range(SUB):
            hi, lo = _split_bf16(buf[slot, _sublane(s, TM_ROW), :])
            cols.append(lax.dot_general(perm, hi, tn, preferred_element_type=F32)
                        + lax.dot_general(perm, lo, tn, preferred_element_type=F32))
            if s % 4 == 3:
                yield
        y = x_ref[rows, :] + jnp.concatenate(cols, axis=1)
        if not final:
            outs[0][rows, :] = y
            yield
            _norm_and_lowrank(y, g_ref, wa_ref, outs[1].at[rows, :], outs[2].at[rows, :])
        else:
            yield
            stage[rows, :] = _rms(y, g_ref[...])

    _interleave(*[_delayed(tile_stages(t), t) for t in range(ROW_TILES)],
                *[_delayed(gather(GATHER_AHEAD, t, i + GATHER_AHEAD < n), t)
                  for t in range(ROW_TILES)])

    if final:
        @pl.when(i < T_PROMPT // step_rows)
        def _():
            outs[0][...] = stage[...]

        @pl.when(i >= T_PROMPT // step_rows)
        def _():
            outs[1][...] = stage[...]


def _combine(seg, route, grp_base, ys, x, g, wa, final):
    m = x.shape[0]
    step_rows = ROW_TILES * TM_ROW
    nt = m // step_rows
    n_p = T_PROMPT // step_rows
    smem = functools.partial(pl.BlockSpec, (ROW_TILES, 8, LANES), memory_space=pltpu.SMEM)
    tile = (step_rows, D_MODEL)
    if final:
        out_specs = [pl.BlockSpec(tile, lambda i, *_: (jnp.minimum(i, n_p - 1), 0)),
                     pl.BlockSpec(tile, lambda i, *_: (jnp.maximum(i - n_p, 0), 0))]
        out_shape = [jax.ShapeDtypeStruct((T_PROMPT, D_MODEL), F32),
                     jax.ShapeDtypeStruct((T_SAMPLE, D_MODEL), F32)]
    else:
        out_specs = [pl.BlockSpec(tile, lambda i, *_: (i, 0)),
                     pl.BlockSpec(tile, lambda i, *_: (i, 0)),
                     pl.BlockSpec((step_rows, LANES), lambda i, *_: (i, 0))]
        out_shape = [jax.ShapeDtypeStruct((m, D_MODEL), F32),
                     jax.ShapeDtypeStruct((m, D_MODEL), BF16),
                     jax.ShapeDtypeStruct((m, LANES), BF16)]
    grid_spec = pltpu.PrefetchScalarGridSpec(
        num_scalar_prefetch=1,
        grid=(nt,),
        in_specs=[smem(lambda i, *_, ahead=ahead: (jnp.minimum(i + ahead, nt - 1), 0, 0))
                  for ahead in range(GATHER_AHEAD + 1)] + [
            pl.BlockSpec((ROW_TILES, ROUTE_ROWS, TM_ROW), lambda i, *_: (i, 0, 0)),
            pl.BlockSpec(memory_space=pl.ANY),
            pl.BlockSpec(tile, lambda i, *_: (i, 0)),
            pl.BlockSpec(g.shape, lambda i, *_: (0, 0)),
            pl.BlockSpec(wa.shape, lambda i, *_: (0, 0)),
        ],
        out_specs=out_specs,
        scratch_shapes=[pltpu.VMEM(((GATHER_AHEAD + 1) * ROW_TILES, TM_ROW * SUB, LANES), F32),
                        pltpu.SemaphoreType.DMA(((GATHER_AHEAD + 1) * ROW_TILES,)),
                        pltpu.VMEM(tile if final else (8, LANES), F32)],
    )
    return pl.pallas_call(
        functools.partial(_combine_kernel, final=final),
        grid_spec=grid_spec,
        out_shape=out_shape,
        compiler_params=pltpu.CompilerParams(
            dimension_semantics=("arbitrary",), vmem_limit_bytes=VMEM_LIMIT),
        name="combine",
    )(grp_base, *[seg] * (GATHER_AHEAD + 1), route, ys, x, g, wa)


def _prep_weights(w_in, gla_a2, cm_ws, cm_b, router_group_w, router_group_b,
                  router_expert_w, router_expert_b):
    w_z = jnp.swapaxes(w_in, 1, 2).astype(BF16)
    a0 = _w_in_offsets()["alr"][0]
    w_alr = jnp.pad(w_z[:, a0:a0 + GLA_LOWRANK], ((0, 0), (0, LANES - GLA_LOWRANK), (0, 0)))
    a2 = jnp.pad(gla_a2, ((0, 0), (0, LANES - GLA_LOWRANK), (0, 0))).astype(BF16)
    ws_p = jnp.tril(cm_ws).astype(BF16)
    small = jnp.tril(cm_ws[:, :, :DEC_SEQ, :DEC_SEQ])
    eye = jnp.eye(SEQ_PER_BLK, dtype=F32)
    ws_s = jnp.einsum("ij,lgab->lgiajb", eye, small).reshape(
        DEPTH, CM_GROUPS, ROWS_S, ROWS_S).astype(BF16)
    cmb_p = jnp.broadcast_to(jnp.transpose(cm_b, (0, 2, 1))[:, :, :, None],
                             (DEPTH, CM_CHUNK, CM_GROUPS, CM_GCH)).reshape(DEPTH, CM_CHUNK, CM_CH)
    cmb_s = jnp.tile(cmb_p[:, :DEC_SEQ], (1, SEQ_PER_BLK, 1))
    pad = LANES - MOE_GROUPS - N_EXPERTS
    w_r = jnp.pad(jnp.swapaxes(jnp.concatenate([router_group_w, router_expert_w], axis=-1), 1, 2),
                  ((0, 0), (0, pad), (0, 0)))
    r_b = jnp.pad(jnp.concatenate([router_group_b, router_expert_b], axis=-1),
                  ((0, 0), (0, pad)))[:, :, None]
    return w_z, w_alr, a2, ws_p, ws_s, cmb_p, cmb_s, w_r.astype(BF16), r_b


def kernel(x_prompt, x_sample, state_conv, state_gla, norm1_g, w_in, conv_w, gla_a2, gla_a_b,
           gla_norm_g, cm_norm_g, cm_ws, cm_b, proj_a, proj_b, proj_c, w_out, norm2_g,
           router_group_w, router_group_b, router_expert_w, router_expert_b,
           exp_w1, exp_w3, exp_w2, final_norm_g):
    (w_z, w_alr, a2, ws_p, ws_s, cmb_p, cmb_s, w_r, r_b) = _prep_weights(
        w_in, gla_a2, cm_ws, cm_b, router_group_w, router_group_b, router_expert_w,
        router_expert_b)
    pa, pb, pc, wo = (w.astype(BF16) for w in (proj_a, proj_b, proj_c, w_out))
    n1 = norm1_g[:, None, :]
    n2 = norm2_g[:, None, :]
    ab = gla_a_b[:, None, :]
    gng = gla_norm_g[:, None, :]
    cmg = cm_norm_g.reshape(DEPTH, 1, CM_CH)
    fg = final_norm_g[None, :]
    xa = x_prompt.reshape(T_PROMPT, D_MODEL)
    xb = x_sample.reshape(T_SAMPLE, D_MODEL)
    gla_s = None
    conv_p, gla_p, conv_s, cmv_s = [], [], [], []
    xin, alr = _prenorm(xa, xb, n1, w_alr, 0)
    for l in range(DEPTH):
        z, *experts = _inproj(xin, w_z, (exp_w1, exp_w3, exp_w2), l)
        acts_p, nconv, ngla = _mix_prompt(z, alr, a2, ab, conv_w, gng, cmg, ws_p, cmb_p, l)
        sc = state_conv[l]
        p2 = jnp.pad(sc, ((0, 0), (0, DEC_SEQ - 2), (0, 0))).reshape(T_SAMPLE, CONV_CH)
        p1 = jnp.pad(sc[:, 1:2], ((0, 0), (0, DEC_SEQ - 1), (0, 0))).reshape(T_SAMPLE, CONV_CH)
        acts_s, cin_s, gla_s, vrows = _mix_sample(z, alr, a2, ab, conv_w, gng, cmg, ws_s, cmb_s,
                                                  p1, p2, state_gla, gla_s, l)
        conv_p.append(nconv)
        gla_p.append(ngla)
        conv_s.append(cin_s.reshape(DEC_BATCH, DEC_SEQ, CONV_CH)[:, DEC_SEQ - (CONV_K - 1):])
        cmv_s.append(vrows.reshape(DEC_BATCH, DEC_SEQ, CM_CH))

        x, xn, route, seg, cnt = _outproj(acts_p, acts_s, z, xa, xb, pa, pb, pc, wo, n2, w_r, r_b,
                                          l)
        plan = _plan(cnt)
        n_used = plan[PLAN_USED, :1]
        grp_base = plan[PLAN_BASE, :MOE_GROUPS]
        xs = _scatter(seg, route, xn, n_used, plan[PLAN_END, :MOE_GROUPS],
                      plan[PLAN_BLOCKS, :MOE_GROUPS], grp_base)
        ys = _ffn(plan[PLAN_GROUP, :N_BLK], n_used, xs, *experts)
        if l == DEPTH - 1:
            out = _combine(seg, route, grp_base, ys, x, fg, w_alr[l], True)
        else:
            xa, xin, alr = _combine(seg, route, grp_base, ys, x, n1[l + 1], w_alr[l + 1], False)
            xb = xa

    y_prompt = out[0].reshape(BATCH, SEQ, D_MODEL)
    y_sample = out[1].reshape(DEC_BATCH, DEC_SEQ, D_MODEL)
    return (y_prompt, y_sample, jnp.stack(conv_p), jnp.stack(gla_p), jnp.stack(conv_s),
            gla_s, jnp.stack(cmv_s))
```

```python
import functools

import jax
import jax.numpy as jnp
from jax import lax
from jax.experimental import pallas as pl
from jax.experimental.pallas import tpu as pltpu

F32 = jnp.float32
BF16 = jnp.bfloat16

D_MODEL = 1024
BATCH = 8
SEQ = 2048
DEPTH = 2
DEC_BATCH = 128
DEC_SEQ = 8
CONV_K = 3
CONV_CH = 512
GLA_HEADS = 4
GLA_DK = 128
GLA_DV = 256
GLA_QK = GLA_HEADS * GLA_DK
GLA_V = GLA_HEADS * GLA_DV
GLA_LOWRANK = 16
GLA_TAU = 16.0
GLA_CHUNK = 64
CM_GROUPS = 4
CM_CHUNK = 128
CM_GCH = 128
CM_CH = 512
MOE_GROUPS = 8
EXPERTS_PER_GROUP = 8
N_EXPERTS = 64
D_EXPERT = 256
EPS = 1e-6

LANES = 128
T_PROMPT = BATCH * SEQ
T_SAMPLE = DEC_BATCH * DEC_SEQ
T_ALL = T_PROMPT + T_SAMPLE

COL_GA, COL_GB, COL_GC = 0, 1024, 2048
COL_V, COL_R, COL_Q, COL_K = 3072, 4096, 5120, 5632
COL_H, COL_CG, COL_BG, COL_U, COL_VV = 6144, 6656, 7168, 7680, 8192
Z_COLS = 8704
ACT_COLS = 2048
ROUTE_ROWS = 2 * EXPERTS_PER_GROUP
ROUTE_LRANK = EXPERTS_PER_GROUP

TM_PRE = 1024
TM_IN = 1024
TN_IN = Z_COLS // 2

W_IN_WIDTHS = (("h", 512), ("cg", 512), ("bg", 512), ("q", 512), ("k", 512), ("v", 1024),
               ("r", 1024), ("alr", GLA_LOWRANK), ("u", 512), ("vv", 512), ("ga", 1024),
               ("gb", 1024), ("gc", 1024))
Z_ORDER = ("ga", "gb", "gc", "v", "r", "q", "k", "h", "cg", "bg", "u", "vv")


def _w_in_offsets():
    off, o = {}, 0
    for name, n in W_IN_WIDTHS:
        off[name] = (o, n)
        o += n
    return off


def _w_in_copies():
    off = _w_in_offsets()
    copies = [[] for _ in range(Z_COLS // TN_IN)]
    dst = 0
    for name in Z_ORDER:
        src, n = off[name]
        while n:
            tile = dst // TN_IN
            at = dst - tile * TN_IN
            take = min(n, TN_IN - at)
            runs = copies[tile]
            if runs and runs[-1][0] + runs[-1][2] == src and runs[-1][1] + runs[-1][2] == at:
                runs[-1] = (runs[-1][0], runs[-1][1], runs[-1][2] + take)
            else:
                runs.append((src, at, take))
            src, dst, n = src + take, dst + take, n - take
    assert dst == Z_COLS
    assert all(v % 16 == 0 for runs in copies for run in runs for v in run)
    return copies
TC_MIX = 256
SEQ_PER_BLK = 16
ROWS_S = SEQ_PER_BLK * DEC_SEQ
TM_OUT = 256
ROUTE_DELAY = 3
TM_ROW = TM_OUT
ROW_TILES = 4
GATHER_AHEAD = 2
COPY_STAGE_GROUPS = 4
MOE_BLK = 256
N_BLK = T_ALL // MOE_BLK + MOE_GROUPS
N_SORTED = N_BLK * MOE_BLK
VMEM_CAPACITY = 64 * 1024 * 1024
VMEM_LIMIT = 56 * 1024 * 1024
VMEM_SLACK = 4 * 1024 * 1024


def _sigmoid(x):
    return 0.5 * jnp.tanh(0.5 * x) + 0.5


def _gelu_tanh(x):
    c = 0.7978845608028654
    half = 0.5 * x
    return half + half * jnp.tanh(x * (c + (c * 0.044715) * (x * x)))


def _log_sigmoid(x):
    log2_e = 1.4426950408889634
    ln_2 = 0.6931471805599453
    return jnp.minimum(x, 0.0) - ln_2 * jnp.log2(1.0 + jnp.exp2(-log2_e * jnp.abs(x)))


def _rms(x, g):
    ms = jnp.mean(x * x, axis=-1, keepdims=True)
    return x * lax.rsqrt(ms + EPS) * g


def _split_bf16(x):
    hi = x.astype(BF16)
    lo = (x - hi.astype(F32)).astype(BF16)
    return hi, lo


def _dot(a, b):
    return jnp.dot(a, b, preferred_element_type=F32)


def _dot_nt(a, b):
    return lax.dot_general(a, b, (((1,), (1,)), ((), ())), preferred_element_type=F32)


def _layer_spec(arr, layer):
    nd = arr.ndim - 1
    return pl.BlockSpec((None,) + arr.shape[1:], lambda *g: (layer,) + (0,) * nd)


def _two_part_specs(tile, xa, xb):
    n_a = xa.shape[0] // tile
    n_b = xb.shape[0] // tile
    return n_a, [
        pl.BlockSpec((tile, xa.shape[1]), lambda i, *_: (jnp.minimum(i, n_a - 1), 0)),
        pl.BlockSpec((tile, xb.shape[1]), lambda i, *_: (jnp.clip(i - n_a, 0, n_b - 1), 0)),
    ]


def _norm_and_lowrank(x, g_ref, wa_ref, xn_ref, a_ref):
    xn = _rms(x, g_ref[...]).astype(BF16)
    xn_ref[...] = xn
    a_ref[...] = _dot_nt(xn, wa_ref[...]).astype(BF16)


def _prenorm_kernel(xa_ref, xb_ref, g_ref, wa_ref, xn_ref, a_ref, *, n_a):
    x = jnp.where(pl.program_id(0) < n_a, xa_ref[...], xb_ref[...])
    _norm_and_lowrank(x, g_ref, wa_ref, xn_ref, a_ref)


def _prenorm(xa, xb, g, wa, layer):
    m = T_ALL
    n_a, x_specs = _two_part_specs(TM_PRE, xa, xb)
    return pl.pallas_call(
        functools.partial(_prenorm_kernel, n_a=n_a),
        grid=(m // TM_PRE,),
        in_specs=x_specs + [_layer_spec(g, layer), _layer_spec(wa, layer)],
        out_specs=[pl.BlockSpec((TM_PRE, D_MODEL), lambda i: (i, 0)),
                   pl.BlockSpec((TM_PRE, LANES), lambda i: (i, 0))],
        out_shape=[jax.ShapeDtypeStruct((m, D_MODEL), BF16),
                   jax.ShapeDtypeStruct((m, LANES), BF16)],
        compiler_params=pltpu.CompilerParams(
            dimension_semantics=("arbitrary",), vmem_limit_bytes=VMEM_LIMIT),
        name="prenorm",
    )(xa, xb, g, wa)


EXPERT_SHAPES = ((D_MODEL, D_EXPERT), (D_MODEL, D_EXPERT), (D_EXPERT, D_MODEL))


def _inproj_kernel(xn_ref, w_hbm, *refs, layer):
    n = len(EXPERT_SHAPES)
    z_ref, w_buf, sem = refs[n], refs[-2], refs[-1]
    j, i = pl.program_id(0), pl.program_id(1)

    def tile_copies(tile):
        return [pltpu.make_async_copy(w_hbm.at[layer, pl.ds(src, rows)],
                                      w_buf.at[tile, pl.ds(at, rows)], sem.at[tile])
                for src, at, rows in _w_in_copies()[tile]]

    @pl.when((j == 0) & (i == 0))
    def _():
        for tile in range(w_buf.shape[0]):
            for cp in tile_copies(tile):
                cp.start()

    for tile in range(w_buf.shape[0]):
        @pl.when((j == tile) & (i == 0))
        def _():
            for cp in tile_copies(tile):
                cp.wait()

    for src, dst in zip(refs[:n], refs[n + 1:]):
        dst[...] = src[...].astype(BF16)
    z_ref[...] = _dot_nt(xn_ref[...], w_buf[j]).astype(BF16)


def _inproj(xn, w, experts, layer):
    m = xn.shape[0]
    n_j, n_i = Z_COLS // TN_IN, m // TM_IN
    per_step = -(-N_EXPERTS // (n_j * n_i))
    assert N_EXPERTS % per_step == 0
    expert = lambda j, i: jnp.minimum(j * n_i + i, N_EXPERTS // per_step - 1)
    vmem_bytes = (2 * Z_COLS * D_MODEL + 2 * 2 * TM_IN * (TN_IN + D_MODEL)
                  + 2 * per_step * (4 + 2) * sum(a * b for a, b in EXPERT_SHAPES))
    assert vmem_bytes + VMEM_SLACK <= VMEM_CAPACITY
    return pl.pallas_call(
        functools.partial(_inproj_kernel, layer=layer),
        grid=(n_j, n_i),
        in_specs=[
            pl.BlockSpec((TM_IN, D_MODEL), lambda j, i: (i, 0)),
            pl.BlockSpec(memory_space=pl.ANY),
        ] + [pl.BlockSpec((None, per_step) + s, lambda j, i: (layer, expert(j, i), 0, 0))
             for s in EXPERT_SHAPES],
        out_specs=[pl.BlockSpec((TM_IN, TN_IN), lambda j, i: (i, j))]
        + [pl.BlockSpec((per_step,) + s, lambda j, i: (expert(j, i), 0, 0))
           for s in EXPERT_SHAPES],
        out_shape=[jax.ShapeDtypeStruct((m, Z_COLS), BF16)]
        + [jax.ShapeDtypeStruct((N_EXPERTS,) + s, BF16) for s in EXPERT_SHAPES],
        scratch_shapes=[pltpu.VMEM((n_j, TN_IN, D_MODEL), BF16), pltpu.SemaphoreType.DMA((n_j,))],
        compiler_params=pltpu.CompilerParams(
            dimension_semantics=("arbitrary", "arbitrary"),
            vmem_limit_bytes=vmem_bytes + VMEM_SLACK),
        name="inproj",
    )(xn, w, *experts)


def _gla_log_decay(alr_ref, a2_ref, ab_ref):
    la = _log_sigmoid(_dot(alr_ref[...], a2_ref[...]) + ab_ref[...]) * (1.0 / GLA_TAU)
    return _split_bf16(la)


def _masked_sum(mask, la_hi, la_lo):
    m = jnp.where(mask, 1.0, 0.0).astype(BF16)
    return _dot(m, la_hi) + _dot(m, la_lo)


def _gla_decay_terms(alr_ref, a2_ref, ab_ref, tril_mask, same_mask):
    la_hi, la_lo = _gla_log_decay(alr_ref, a2_ref, ab_ref)
    return (la_hi, la_lo, _masked_sum(tril_mask, la_hi, la_lo),
            _masked_sum(same_mask, la_hi, la_lo))


def _gla_out_gate(o, g_ref, r):
    return _rms(o, g_ref[...]) * (r * _sigmoid(r))


def _chunk_mlp_group(g, bgu_ref, vv_ref, cmg_ref, ws_ref, cmb_ref, n_chunks):
    sl = slice(g * CM_GCH, (g + 1) * CM_GCH)
    ug = _gelu_tanh(bgu_ref[:, CONV_CH + g * CM_GCH:CONV_CH + (g + 1) * CM_GCH])
    vg = _rms(_gelu_tanh(vv_ref[:, sl]).astype(F32), cmg_ref[:, sl])
    vgb = vg.astype(BF16)
    rows = []
    for j in range(n_chunks):
        rs = slice(j * CM_CHUNK, (j + 1) * CM_CHUNK)
        rows.append(_dot(ws_ref[g], vgb[rs]) + cmb_ref[:, sl])
    s = rows[0] if n_chunks == 1 else jnp.concatenate(rows, axis=0)
    return ug * s, vg


def _chunk_mlp(bgu_ref, vv_ref, cmg_ref, ws_ref, cmb_ref, n_chunks):
    parts = [_chunk_mlp_group(g, bgu_ref, vv_ref, cmg_ref, ws_ref, cmb_ref, n_chunks)
             for g in range(CM_GROUPS)]
    return (jnp.concatenate([p[0] for p in parts], axis=1),
            jnp.concatenate([p[1] for p in parts], axis=1))


N_Z_VIEWS = 7
N_MIX_W = 7
SEQ_PER_STEP = 4
SEQ_STAGGER = 3


def _mix_prompt_kernel(*refs):
    n_z = N_Z_VIEWS * SEQ_PER_STEP
    weights = refs[n_z:n_z + N_MIX_W]
    acts_ref, nconv_ref, ngla_ref, st_ref, carry_ref = refs[n_z + N_MIX_W:]

    @pl.when(pl.program_id(1) == 0)
    def _():
        st_ref[...] = jnp.zeros_like(st_ref)
        carry_ref[...] = jnp.zeros_like(carry_ref)

    rr = lax.broadcasted_iota(jnp.int32, (TC_MIX, TC_MIX), 0)
    cc = lax.broadcasted_iota(jnp.int32, (TC_MIX, TC_MIX), 1)
    tril = ((rr >> 6) == (cc >> 6)) & (cc <= rr)
    masks = (tril, jnp.where(tril, 1.0, 0.0).astype(BF16),
             jnp.where(rr - cc == 1, 1.0, 0.0).astype(BF16),
             jnp.where(rr - cc == 2, 1.0, 0.0).astype(BF16))

    _interleave(*[
        _delayed(_mix_prompt_seq(*refs[N_Z_VIEWS * s:N_Z_VIEWS * (s + 1)], *weights,
                                acts_ref.at[s], nconv_ref.at[s], st_ref.at[s], carry_ref.at[s],
                                masks),
                s * SEQ_STAGGER)
        for s in range(SEQ_PER_STEP)])

    @pl.when(pl.program_id(1) == pl.num_programs(1) - 1)
    def _():
        for s in range(SEQ_PER_STEP):
            for hd in range(GLA_HEADS):
                ngla_ref[s, 0, hd] = st_ref[s, hd].T


def _mix_prompt_seq(v_ref, r_ref, qk_ref, hcg_ref, bgu_ref, vv_ref, alr_ref,
                    a2_ref, ab_ref, cw_ref, gng_ref, cmg_ref, ws_ref, cmb_ref,
                    acts_ref, nconv_ref, st_ref, carry_ref, masks):
    tc = TC_MIX

    h = hcg_ref[:, :CONV_CH].astype(F32)
    cg = hcg_ref[:, CONV_CH:].astype(F32)
    bg = bgu_ref[:, :CONV_CH].astype(F32)
    cin = cg * h
    tril, tril_01, shift_1, shift_2 = masks
    cin_b = cin.astype(BF16)
    x1 = _dot(shift_1, cin_b)
    x2 = _dot(shift_2, cin_b)
    conv = x2 * cw_ref[0:1, :] + x1 * cw_ref[1:2, :] + cin * cw_ref[2:3, :]
    c0 = carry_ref[0:1, :]
    c1 = carry_ref[1:2, :]
    row8 = lax.broadcasted_iota(jnp.int32, (8, 1), 0)
    head = jnp.where(row8 == 0, c0 * cw_ref[0:1, :] + c1 * cw_ref[1:2, :],
                     jnp.where(row8 == 1, c1 * cw_ref[0:1, :], 0.0))
    conv = jnp.concatenate([conv[0:8] + head, conv[8:]], axis=0)
    acts_ref[:, 0:CONV_CH] = (bg * conv).astype(BF16)
    carry_ref[0:2, :] = cin[tc - 2:tc, :]
    nconv_ref[0] = cin[tc - 2:tc, :]
    yield

    la_hi, la_lo = _gla_log_decay(alr_ref, a2_ref, ab_ref)
    b = _dot(tril_01, la_hi) + _dot(tril_01, la_lo)
    yield
    n_chunks = tc // GLA_CHUNK
    b_last = [b[(c + 1) * GLA_CHUNK - 1:(c + 1) * GLA_CHUNK, :] for c in range(n_chunks)]
    bl = jnp.concatenate([jnp.broadcast_to(r_, (GLA_CHUNK, GLA_QK)) for r_ in b_last], axis=0)
    q = qk_ref[:, :GLA_QK].astype(F32) * (GLA_DK ** -0.5)
    k = qk_ref[:, GLA_QK:].astype(F32)
    q_t = (q * jnp.exp(b)).astype(BF16)
    k_t = (k * jnp.exp(-b)).astype(BF16)
    k_end = (k * jnp.exp(bl - b)).astype(BF16)
    yield
    states = [st_ref[hd] for hd in range(GLA_HEADS)]
    k_cols = [slice(hd * GLA_DK, (hd + 1) * GLA_DK) for hd in range(GLA_HEADS)]
    v_cols = [slice(hd * GLA_DV, (hd + 1) * GLA_DV) for hd in range(GLA_HEADS)]
    o_intra = []
    for hd in range(GLA_HEADS):
        att = jnp.where(tril, _dot_nt(q_t[:, k_cols[hd]], k_t[:, k_cols[hd]]), 0.0).astype(BF16)
        o_intra.append(_dot(att, v_ref[:, v_cols[hd]]))
    yield

    o_rows = [[] for _ in range(GLA_HEADS)]
    assert n_chunks == CM_GROUPS
    for c in range(n_chunks):
        rs = slice(c * GLA_CHUNK, (c + 1) * GLA_CHUNK)
        for hd in range(GLA_HEADS):
            ks = k_cols[hd]
            st = states[hd]
            o_rows[hd].append(o_intra[hd][rs] + _dot_nt(q_t[rs, ks], st.astype(BF16)))
            upd = lax.dot_general(v_ref[rs, v_cols[hd]], k_end[rs, ks], (((0,), (0,)), ((), ())),
                                  preferred_element_type=F32)
            states[hd] = jnp.exp(b_last[c][:, ks]) * st + upd
        us, _ = _chunk_mlp_group(c, bgu_ref, vv_ref, cmg_ref, ws_ref, cmb_ref, tc // CM_CHUNK)
        acts_ref[:, CONV_CH + GLA_V + c * CM_GCH:CONV_CH + GLA_V + (c + 1) * CM_GCH] = (
            us.astype(BF16))
        yield

    for hd in range(GLA_HEADS):
        o = jnp.concatenate(o_rows[hd], axis=0)
        r = r_ref[:, v_cols[hd]]
        acts_ref[:, CONV_CH + hd * GLA_DV:CONV_CH + (hd + 1) * GLA_DV] = (
            _gla_out_gate(o, gng_ref, r).astype(BF16))
        st_ref[hd] = states[hd]
        if hd % 2 == 1:
            yield


def _z_specs(rows, row_map):
    def spec(width, col):
        blk = col // width
        return pl.BlockSpec((rows, width), lambda *g: (row_map(*g), blk))
    return [spec(1024, COL_V), spec(1024, COL_R), spec(1024, COL_Q), spec(1024, COL_H),
            spec(1024, COL_BG), spec(512, COL_VV)]


def _mix_prompt(z, alr, a2, ab, cw, gng, cmg, ws, cmb, layer):
    nt = SEQ // TC_MIX
    nb = BATCH // SEQ_PER_STEP
    small = (a2, ab, cw, gng, cmg, ws, cmb)
    assert len(small) == N_MIX_W
    in_specs, args = [], []
    for s in range(SEQ_PER_STEP):
        row_map = lambda b, c, s=s: (b + s * nb) * nt + c
        in_specs += _z_specs(TC_MIX, row_map) + [
            pl.BlockSpec((TC_MIX, LANES), lambda b, c, row_map=row_map: (row_map(b, c), 0))]
        args += [z] * (N_Z_VIEWS - 1) + [alr]
    in_specs += [_layer_spec(a, layer) for a in small]
    acts, nconv, ngla = pl.pallas_call(
        _mix_prompt_kernel,
        grid=(nb, nt),
        in_specs=in_specs,
        out_specs=[
            pl.BlockSpec((SEQ_PER_STEP, TC_MIX, ACT_COLS), lambda b, c: (0, b * nt + c, 0)),
            pl.BlockSpec((SEQ_PER_STEP, 1, CONV_K - 1, CONV_CH), lambda b, c: (0, b, 0, 0)),
            pl.BlockSpec((SEQ_PER_STEP, 1, GLA_HEADS, GLA_DK, GLA_DV),
                         lambda b, c: (0, b, 0, 0, 0)),
        ],
        out_shape=[
            jax.ShapeDtypeStruct((SEQ_PER_STEP, T_PROMPT // SEQ_PER_STEP, ACT_COLS), BF16),
            jax.ShapeDtypeStruct((SEQ_PER_STEP, nb, CONV_K - 1, CONV_CH), F32),
            jax.ShapeDtypeStruct((SEQ_PER_STEP, nb, GLA_HEADS, GLA_DK, GLA_DV), F32),
        ],
        scratch_shapes=[pltpu.VMEM((SEQ_PER_STEP, GLA_HEADS, GLA_DV, GLA_DK), F32),
                        pltpu.VMEM((SEQ_PER_STEP, 8, CONV_CH), F32)],
        compiler_params=pltpu.CompilerParams(
            dimension_semantics=("arbitrary", "arbitrary"), vmem_limit_bytes=VMEM_LIMIT),
        name="mix_prompt",
    )(*args, *small)
    return (acts.reshape(T_PROMPT, ACT_COLS), nconv.reshape(BATCH, CONV_K - 1, CONV_CH),
            ngla.reshape(BATCH, GLA_HEADS, GLA_DK, GLA_DV))


def _mix_sample_body(v_ref, r_ref, qk_ref, hcg_ref, bgu_ref, vv_ref, alr_ref,
                       a2_ref, ab_ref, cw_ref, gng_ref, cmg_ref, ws_ref, cmb_ref,
                       p1_ref, p2_ref, s0_ref,
                       acts_ref, cin_ref, ns_ref, vrow_ref):
    n = ROWS_S

    pos = lax.broadcasted_iota(jnp.int32, (n, 1), 0) & (DEC_SEQ - 1)
    h = hcg_ref[:, :CONV_CH].astype(F32)
    cg = hcg_ref[:, CONV_CH:].astype(F32)
    bg = bgu_ref[:, :CONV_CH].astype(F32)
    cin = cg * h
    x1 = jnp.where(pos >= 1, pltpu.roll(cin, 1, 0), p1_ref[...])
    x2 = jnp.where(pos >= 2, pltpu.roll(cin, 2, 0), p2_ref[...])
    conv = x2 * cw_ref[0:1, :] + x1 * cw_ref[1:2, :] + cin * cw_ref[2:3, :]
    acts_ref[:, 0:CONV_CH] = (bg * conv).astype(BF16)
    cin_ref[...] = cin

    rr = lax.broadcasted_iota(jnp.int32, (n, n), 0)
    cc = lax.broadcasted_iota(jnp.int32, (n, n), 1)
    same = (rr >> 3) == (cc >> 3)
    tril = same & (cc <= rr)
    la_hi, la_lo, b, bl = _gla_decay_terms(alr_ref, a2_ref, ab_ref, tril, same)
    q = qk_ref[:, :GLA_QK].astype(F32) * (GLA_DK ** -0.5)
    k = qk_ref[:, GLA_QK:].astype(F32)
    q_t = (q * jnp.exp(b)).astype(BF16)
    k_t = (k * jnp.exp(-b)).astype(BF16)
    k_end = k * jnp.exp(bl - b)
    la_hi = la_hi.astype(F32)
    la_lo = la_lo.astype(F32)
    row_seq = lax.broadcasted_iota(jnp.int32, (n, GLA_DK), 0) >> 3
    seq3 = lax.broadcasted_iota(jnp.int32, (SEQ_PER_BLK, GLA_DK, n), 0)
    lane_seq3 = lax.broadcasted_iota(jnp.int32, (SEQ_PER_BLK, GLA_DK, n), 2) >> 3
    mask3 = seq3 == lane_seq3
    ones = jnp.ones((n, GLA_DV), BF16)
    big = SEQ_PER_BLK * GLA_DK

    def per_seq(x_tr):
        x3 = jnp.where(mask3, x_tr[None, :, :], 0.0)
        return x3.reshape(big, n).astype(BF16)

    for hd in range(GLA_HEADS):
        ks = slice(hd * GLA_DK, (hd + 1) * GLA_DK)
        vs = slice(hd * GLA_DV, (hd + 1) * GLA_DV)
        qh = q_t[:, ks]
        vh = v_ref[:, vs]
        att = jnp.where(tril, _dot_nt(qh, k_t[:, ks]), 0.0).astype(BF16)
        o_intra = _dot(att, vh)
        s_old = s0_ref[:, hd].reshape(big, GLA_DV)
        zero = jnp.zeros_like(qh)
        q_big = jnp.concatenate(
            [jnp.where(row_seq == j, qh, zero) for j in range(SEQ_PER_BLK)], axis=1)
        o = o_intra + _dot(q_big, s_old.astype(BF16))
        dlog = _dot(per_seq(la_hi[:, ks].T), ones) + _dot(per_seq(la_lo[:, ks].T), ones)
        upd = _dot(per_seq(k_end[:, ks].T), vh)
        s_new = jnp.exp(dlog) * s_old + upd
        ns_ref[:, hd] = s_new.reshape(SEQ_PER_BLK, GLA_DK, GLA_DV)
        r = r_ref[:, vs]
        acts_ref[:, CONV_CH + hd * GLA_DV:CONV_CH + (hd + 1) * GLA_DV] = (
            _gla_out_gate(o, gng_ref, r).astype(BF16))

    us, vg = _chunk_mlp(bgu_ref, vv_ref, cmg_ref, ws_ref, cmb_ref, 1)
    acts_ref[:, CONV_CH + GLA_V:] = us.astype(BF16)
    vrow_ref[...] = vg


N_MIX_S_IN = 17


def _mix_sample_kernel(*refs, layer):
    if layer == 0:
        @pl.when(pl.program_id(0) == 0)
        def _():
            _mix_sample_body(*refs)

        @pl.when(pl.program_id(0) > 0)
        def _():
            ns_ref = refs[N_MIX_S_IN + 2]
            ns_ref[...] = jnp.zeros_like(ns_ref)
    else:
        _mix_sample_body(*refs[:N_MIX_S_IN], *refs[N_MIX_S_IN + 1:])


def _mix_sample(z, alr, a2, ab, cw, gng, cmg, ws, cmb, p1, p2, s0, ns_all, layer):
    row0 = T_PROMPT // ROWS_S
    n_i = DEC_BATCH // SEQ_PER_BLK
    n_pass = DEPTH if layer == 0 else 1
    blk = lambda p, i: jnp.where(p == 0, i, n_i - 1)
    row_map = lambda p, i: row0 + blk(p, i)
    slot = lambda p, i: (layer + p, i, 0, 0, 0)
    small = (a2, ab, cw, gng, cmg, ws, cmb)
    state_blk = (None, SEQ_PER_BLK, GLA_HEADS, GLA_DK, GLA_DV)
    in_specs = _z_specs(ROWS_S, row_map) + [
        pl.BlockSpec((ROWS_S, LANES), lambda p, i: (row_map(p, i), 0)),
    ] + [_layer_spec(a, layer) for a in small] + [
        pl.BlockSpec((ROWS_S, CONV_CH), lambda p, i: (blk(p, i), 0)),
        pl.BlockSpec((ROWS_S, CONV_CH), lambda p, i: (blk(p, i), 0)),
        pl.BlockSpec(state_blk, lambda p, i: (layer, blk(p, i), 0, 0, 0)),
    ]
    args = (z, z, z, z, z, z, alr, *small, p1, p2, s0)
    assert len(args) == N_MIX_S_IN
    aliases = {}
    if layer > 0:
        in_specs.append(pl.BlockSpec(memory_space=pl.ANY))
        args += (ns_all,)
        aliases = {N_MIX_S_IN: 2}
    return pl.pallas_call(
        functools.partial(_mix_sample_kernel, layer=layer),
        grid=(n_pass, n_i),
        in_specs=in_specs,
        out_specs=[
            pl.BlockSpec((ROWS_S, ACT_COLS), lambda p, i: (blk(p, i), 0)),
            pl.BlockSpec((ROWS_S, CONV_CH), lambda p, i: (blk(p, i), 0)),
            pl.BlockSpec(state_blk, slot),
            pl.BlockSpec((ROWS_S, CM_CH), lambda p, i: (blk(p, i), 0)),
        ],
        out_shape=[
            jax.ShapeDtypeStruct((T_SAMPLE, ACT_COLS), BF16),
            jax.ShapeDtypeStruct((T_SAMPLE, CONV_CH), F32),
            jax.ShapeDtypeStruct((DEPTH, DEC_BATCH, GLA_HEADS, GLA_DK, GLA_DV), F32),
            jax.ShapeDtypeStruct((T_SAMPLE, CM_CH), F32),
        ],
        input_output_aliases=aliases,
        compiler_params=pltpu.CompilerParams(
            dimension_semantics=("arbitrary", "arbitrary"), vmem_limit_bytes=VMEM_LIMIT),
        name="mix_sample",
    )(*args)


def _delayed(gen, n_stages):
    for _ in range(n_stages):
        yield
    yield from gen


def _interleave(*stage_lists):
    pending = list(stage_lists)
    while pending:
        for gen in list(pending):
            if next(gen, StopIteration) is StopIteration:
                pending.remove(gen)


def _outproj_main(i, n_xa, actp_ref, acts_ref, gates_ref, xa_ref, xb_ref, pa_ref, pb_ref, pc_ref,
                  wo_ref, n2_ref, xo_ref, xn_ref, xnb_ref):
    acts = jnp.where(i < T_PROMPT // TM_OUT, actp_ref[...], acts_ref[...])
    x_in = jnp.where(i < n_xa, xa_ref[...], xb_ref[...])
    yb = _dot(acts[:, CONV_CH:CONV_CH + GLA_V], pb_ref[...])
    ga = _sigmoid(gates_ref[:, COL_GA:COL_GA + D_MODEL])
    ya = _dot(acts[:, :CONV_CH], pa_ref[...])
    gb = _sigmoid(gates_ref[:, COL_GB:COL_GB + D_MODEL])
    yield
    yc = _dot(acts[:, CONV_CH + GLA_V:], pc_ref[...])
    gc = _sigmoid(gates_ref[:, COL_GC:COL_GC + D_MODEL])
    mix = ga * ya + gb * yb
    yield
    mix = mix + gc * yc
    x = x_in + _dot(mix.astype(BF16), wo_ref[...])
    xo_ref[...] = x
    yield
    xn = _rms(x, n2_ref[...])
    xn_ref[...] = xn
    xnb = xn.astype(BF16)
    yield
    xnb_ref[...] = xnb


def _outproj_route(live, xnb_ref, wr_ref, rb_ref, route_ref, seg_ref, cnt_ref):
    tm = TM_OUT
    logits = _dot_nt(wr_ref[...], xnb_ref[...]) + rb_ref[...]
    yield
    n_sub = EXPERTS_PER_GROUP
    sub = lax.broadcasted_iota(jnp.int32, (n_sub, tm), 0).astype(F32)
    neg = jnp.float32(-jnp.inf)
    lg = logits[0:MOE_GROUPS]
    gmax = jnp.max(lg, axis=0, keepdims=True)
    grp = jnp.min(jnp.where(lg == gmax, sub, 1e9), axis=0, keepdims=True)
    p_grp = 1.0 / jnp.sum(jnp.exp(lg - gmax), axis=0, keepdims=True)
    yield
    le = logits[MOE_GROUPS:MOE_GROUPS + n_sub]
    for g in range(1, MOE_GROUPS):
        le = jnp.where(grp == g, logits[MOE_GROUPS + g * n_sub:MOE_GROUPS + (g + 1) * n_sub], le)
    v1 = jnp.max(le, axis=0, keepdims=True)
    i1 = jnp.min(jnp.where(le == v1, sub, 1e9), axis=0, keepdims=True)
    le2 = jnp.where(sub == i1, neg, le)
    v2 = jnp.max(le2, axis=0, keepdims=True)
    i2 = jnp.min(jnp.where(le2 == v2, sub, 1e9), axis=0, keepdims=True)
    yield
    t = jnp.exp(v2 - v1)
    g1 = p_grp / (1.0 + t)
    g2 = p_grp * t / (1.0 + t)
    gate_t = jnp.where(sub == i1, g1, jnp.where(sub == i2, g2, 0.0))

    onehot_t = jnp.where(sub == grp, 1.0, 0.0)
    csum = jnp.where(live, jnp.sum(onehot_t, axis=1, keepdims=True), 0.0)
    rr = lax.broadcasted_iota(jnp.int32, (tm, tm), 0)
    cc = lax.broadcasted_iota(jnp.int32, (tm, tm), 1)
    earlier = jnp.where(rr < cc, 1.0, 0.0).astype(BF16)
    padded = jnp.concatenate([onehot_t, jnp.zeros_like(onehot_t)], axis=0).astype(BF16)
    same_before = _dot(padded, earlier)[0:n_sub]
    yield
    lower = jnp.sum(jnp.where(sub < grp, csum, 0.0), axis=0, keepdims=True)
    lrank = lower + jnp.sum(onehot_t * same_before, axis=0, keepdims=True)
    route_ref[0:n_sub, :] = gate_t
    route_ref[n_sub:, :] = jnp.broadcast_to(lrank, (n_sub, tm))
    carry = cnt_ref[...]
    lane = lax.broadcasted_iota(jnp.int32, (n_sub, LANES), 1)
    seg_ref[...] = jnp.where(lane == SEG_START, carry,
                             jnp.where(lane == SEG_LEN, csum, 0.0)).astype(jnp.int32)
    cnt_ref[...] = carry + csum


def _outproj_kernel(actp_ref, acts_ref, gates_ref, xa_ref, xb_ref, pa_ref, pb_ref, pc_ref, wo_ref,
                    n2_ref, wr_ref, rb_ref, xo_ref, xn_ref, route_ref, seg_ref, cnt_ref, xnb_ref,
                    *, n_xa):
    i = pl.program_id(0)

    @pl.when(i == 0)
    def _():
        cnt_ref[...] = jnp.zeros_like(cnt_ref)
        xnb_ref[...] = jnp.zeros_like(xnb_ref)

    tile = jnp.minimum(i, pl.num_programs(0) - 2)
    _interleave(
        _delayed(_outproj_route(i >= 1, xnb_ref, wr_ref, rb_ref, route_ref, seg_ref, cnt_ref),
                 ROUTE_DELAY),
        _outproj_main(tile, n_xa, actp_ref, acts_ref, gates_ref, xa_ref, xb_ref, pa_ref, pb_ref,
                      pc_ref, wo_ref, n2_ref, xo_ref, xn_ref, xnb_ref))


def _lane_prefix(v):
    rr = lax.broadcasted_iota(jnp.int32, (LANES, LANES), 0)
    cc = lax.broadcasted_iota(jnp.int32, (LANES, LANES), 1)
    earlier = jnp.where(rr < cc, 1.0, 0.0).astype(BF16)
    return _dot(jnp.broadcast_to(v, (8, LANES)).astype(BF16), earlier)[0:1]


def _outproj(acts_p, acts_s, z, xa, xb, pa, pb, pc, wo, n2, wr, rb, layer):
    m = T_ALL
    nt = m // TM_OUT
    row = lambda i: (jnp.minimum(i, nt - 1), 0)
    routed = lambda i: (jnp.maximum(i - 1, 0), 0, 0)
    _, act_specs = _two_part_specs(TM_OUT, acts_p, acts_s)
    n_xa, x_specs = _two_part_specs(TM_OUT, xa, xb)
    weights = (pa, pb, pc, wo, n2, wr, rb)
    return pl.pallas_call(
        functools.partial(_outproj_kernel, n_xa=n_xa),
        grid=(nt + 1,),
        in_specs=act_specs + [pl.BlockSpec((TM_OUT, 3 * D_MODEL), row)] + x_specs
        + [_layer_spec(w, layer) for w in weights],
        out_specs=[
            pl.BlockSpec((TM_OUT, D_MODEL), row),
            pl.BlockSpec((TM_OUT, D_MODEL), row),
            pl.BlockSpec((None, ROUTE_ROWS, TM_OUT), routed),
            pl.BlockSpec((None, 8, LANES), routed),
            pl.BlockSpec((8, LANES), lambda i: (0, 0)),
        ],
        scratch_shapes=[pltpu.VMEM((TM_OUT, D_MODEL), BF16)],
        out_shape=[
            jax.ShapeDtypeStruct((m, D_MODEL), F32),
            jax.ShapeDtypeStruct((m, D_MODEL), F32),
            jax.ShapeDtypeStruct((m // TM_OUT, ROUTE_ROWS, TM_OUT), F32),
            jax.ShapeDtypeStruct((m // TM_OUT, 8, LANES), jnp.int32),
            jax.ShapeDtypeStruct((8, LANES), F32),
        ],
        compiler_params=pltpu.CompilerParams(
            dimension_semantics=("arbitrary",), vmem_limit_bytes=VMEM_LIMIT),
        name="outproj",
    )(acts_p, acts_s, z, xa, xb, *weights)


PLAN_GROUP, PLAN_USED, PLAN_END, PLAN_BLOCKS, PLAN_BASE = 0, 1, 2, 3, 4


def _plan_kernel(cnt_ref, plan_ref):
    sub_i = lax.broadcasted_iota(jnp.int32, (8, LANES), 0)
    lane8_i = lax.broadcasted_iota(jnp.int32, (8, LANES), 1)
    cnt = jnp.sum(jnp.where(sub_i == lane8_i, cnt_ref[...], 0.0), axis=0, keepdims=True)
    blocks = jnp.floor((cnt + (MOE_BLK - 1)) * (1.0 / MOE_BLK))
    start = _lane_prefix(blocks)
    end = start + blocks
    lane_i = lax.broadcasted_iota(jnp.int32, (1, LANES), 1)
    lane = lane_i.astype(F32)
    grp_of_blk = jnp.zeros((1, LANES), F32)
    for g in range(MOE_GROUPS):
        end_g = jnp.sum(jnp.where(lane_i == g, end, 0.0), axis=-1, keepdims=True)
        grp_of_blk += jnp.where(lane >= end_g, 1.0, 0.0)
    grp_of_blk = jnp.minimum(grp_of_blk, MOE_GROUPS - 1)
    n_used = jnp.sum(blocks, axis=-1, keepdims=True)
    row = lax.broadcasted_iota(jnp.int32, (8, LANES), 0)
    plan_ref[...] = jnp.where(
        row == PLAN_GROUP, grp_of_blk,
        jnp.where(row == PLAN_USED, n_used,
                  jnp.where(row == PLAN_END, end,
                            jnp.where(row == PLAN_BLOCKS, blocks,
                                      jnp.where(row == PLAN_BASE, start * MOE_BLK, 0.0))))
    ).astype(jnp.int32)


def _plan(cnt):
    return pl.pallas_call(
        _plan_kernel,
        out_shape=jax.ShapeDtypeStruct((8, LANES), jnp.int32),
        name="plan",
    )(cnt)


SEG_START, SEG_LEN = 0, 1


def _segment_copies(seg_ref, base_ref, make_copy, enable=None):
    local = 0
    for g in range(MOE_GROUPS):
        n = seg_ref[g, SEG_LEN]
        first = base_ref[g] + seg_ref[g, SEG_START]
        k = TM_ROW
        while k >= 1:
            done = n & ~(2 * k - 1)
            wanted = (n & k) != 0
            @pl.when(wanted if enable is None else wanted & enable)
            def _():
                make_copy(local + done, first + done, k).start()
            k //= 2
        local = local + n
        if g % COPY_STAGE_GROUPS == COPY_STAGE_GROUPS - 1:
            yield


def _perm_matrix(route_ref):
    row = lax.broadcasted_iota(jnp.int32, (TM_ROW, TM_ROW), 0).astype(F32)
    return jnp.where(row == route_ref[ROUTE_LRANK:ROUTE_LRANK + 1, :], 1.0, 0.0).astype(BF16)


SUB = 8
SUB_X = D_MODEL // 2 // LANES
SUB_GATE = SUB_X
HI_MASK = -65536


def _rows(first_row, n_rows):
    return pl.ds(pl.multiple_of(first_row * SUB, SUB), n_rows * SUB)


def _sublane(s, n_rows):
    return pl.ds(s, n_rows, stride=SUB)


def _scatter_kernel(nu_ref, end_ref, nb_ref, base_ref, seg_ref, route_ref, xn_ref, xs_ref,
                    zbuf, sbuf, sems):
    i = pl.program_id(0)
    n = pl.num_programs(0)
    sem = sems.at[0]

    def zero_block(b):
        return pltpu.make_async_copy(zbuf, xs_ref.at[_rows(b * MOE_BLK, MOE_BLK)], sem)

    def each_unfilled_block(fn):
        for g in range(MOE_GROUPS):
            @pl.when(nb_ref[g] > 0)
            def _():
                fn(zero_block(end_ref[g] - 1))
        for b in range(T_ALL // MOE_BLK, N_BLK):
            @pl.when(b >= nu_ref[0])
            def _():
                fn(zero_block(b))

    @pl.when(i == 0)
    def _():
        zbuf[...] = jnp.zeros_like(zbuf)
        sbuf[...] = jnp.zeros_like(sbuf)
        each_unfilled_block(lambda c: c.start())
        each_unfilled_block(lambda c: c.wait())

    def tile_done(s):
        pltpu.make_async_copy(sbuf.at[s], xs_ref.at[_rows(0, TM_ROW)], sems.at[s]).wait()

    @pl.when(i >= 2)
    def _():
        for t in range(ROW_TILES):
            tile_done((i % 2) * ROW_TILES + t)

    def sort_tile(t):
        slot = (i % 2) * ROW_TILES + t
        route = route_ref.at[t]
        perm = _perm_matrix(route)
        xn = xn_ref[t * TM_ROW:(t + 1) * TM_ROW, :]
        xs = lax.bitcast_convert_type(_dot(perm, xn.astype(BF16)), jnp.int32)
        yield
        half = D_MODEL // 2
        for s in range(SUB_X):
            hi = xs[:, s * LANES:(s + 1) * LANES] & HI_MASK
            lo = lax.shift_right_logical(xs[:, half + s * LANES:half + (s + 1) * LANES], 16)
            sbuf[slot, _sublane(s, TM_ROW), :] = hi | lo
        yield
        gate_t = route[0:EXPERTS_PER_GROUP, :]
        gate_t = jnp.concatenate(
            [gate_t, jnp.zeros((LANES - EXPERTS_PER_GROUP, TM_ROW), F32)], axis=0)
        g1 = gate_t.astype(BF16)
        r1 = gate_t - g1.astype(F32)
        g2 = r1.astype(BF16)
        g3 = (r1 - g2.astype(F32)).astype(BF16)
        sbuf[slot, _sublane(SUB_GATE, TM_ROW), :] = lax.bitcast_convert_type(
            _dot_nt(perm, g1) + _dot_nt(perm, g2) + _dot_nt(perm, g3), jnp.int32)

    _interleave(*[_delayed(sort_tile(t), t) for t in range(ROW_TILES)])

    for t in range(ROW_TILES):
        def make_copy(src_row, dst_row, k, slot=(i % 2) * ROW_TILES + t):
            return pltpu.make_async_copy(sbuf.at[slot, _rows(src_row, k)],
                                         xs_ref.at[_rows(dst_row, k)], sems.at[slot])
        for _ in _segment_copies(seg_ref.at[t], base_ref, make_copy):
            pass

    @pl.when(i == n - 1)
    def _():
        for t in range(ROW_TILES):
            tile_done((i % 2) * ROW_TILES + t)

            @pl.when(n > 1)
            def _():
                tile_done((1 - i % 2) * ROW_TILES + t)


def _scatter(seg, route, xn, n_used, grp_end, grp_blocks, grp_base):
    m = xn.shape[0]
    step_rows = ROW_TILES * TM_ROW
    grid_spec = pltpu.PrefetchScalarGridSpec(
        num_scalar_prefetch=4,
        grid=(m // step_rows,),
        in_specs=[
            pl.BlockSpec((ROW_TILES, 8, LANES), lambda i, *_: (i, 0, 0), memory_space=pltpu.SMEM),
            pl.BlockSpec((ROW_TILES, ROUTE_ROWS, TM_ROW), lambda i, *_: (i, 0, 0)),
            pl.BlockSpec((step_rows, D_MODEL), lambda i, *_: (i, 0)),
        ],
        out_specs=pl.BlockSpec(memory_space=pl.ANY),
        scratch_shapes=[pltpu.VMEM((MOE_BLK * SUB, LANES), jnp.int32),
                        pltpu.VMEM((2 * ROW_TILES, TM_ROW * SUB, LANES), jnp.int32),
                        pltpu.SemaphoreType.DMA((2 * ROW_TILES,))],
    )
    return pl.pallas_call(
        _scatter_kernel,
        grid_spec=grid_spec,
        out_shape=jax.ShapeDtypeStruct((N_SORTED * SUB, LANES), jnp.int32),
        compiler_params=pltpu.CompilerParams(
            dimension_semantics=("arbitrary",), vmem_limit_bytes=VMEM_LIMIT),
        name="scatter",
    )(n_used, grp_end, grp_blocks, grp_base, seg, route, xn)


def _ffn_kernel(bg_ref, nu_ref, xs_ref, w1_ref, w3_ref, w2_ref, y_ref):
    del bg_ref
    b = pl.program_id(0)

    def block_inputs():
        packed = [xs_ref[_sublane(s, MOE_BLK), :] for s in range(SUB_X)]
        x = jnp.concatenate(
            [lax.bitcast_convert_type(u & HI_MASK, F32).astype(BF16) for u in packed]
            + [lax.bitcast_convert_type(lax.shift_left(u, 16), F32).astype(BF16) for u in packed],
            axis=1)
        return x, lax.bitcast_convert_type(xs_ref[_sublane(SUB_GATE, MOE_BLK), :], F32)

    def store(y):
        for s in range(SUB):
            y_ref[_sublane(s, MOE_BLK), :] = y[:, s * LANES:(s + 1) * LANES]

    @pl.when(b < nu_ref[0])
    def _():
        x, gates = block_inputs()
        n_e = EXPERTS_PER_GROUP
        h1 = [_dot(x, w1_ref[e]) for e in range(n_e)]
        h3 = [_dot(x, w3_ref[e]) for e in range(n_e)]
        hs = []
        for e in range(n_e):
            ge = gates[:, e:e + 1]
            hs.append(jnp.where(ge > 0.0, h1[e] * _sigmoid(h1[e]) * h3[e] * ge, 0.0).astype(BF16))
        hcat = jnp.concatenate(hs, axis=1)
        store(_dot(hcat, w2_ref[...].reshape(n_e * D_EXPERT, D_MODEL)))

    @pl.when(b >= nu_ref[0])
    def _():
        y_ref[...] = jnp.zeros_like(y_ref)


def _ffn(blk_group, n_used, xs, w1, w3, w2):
    grouped = (MOE_GROUPS, EXPERTS_PER_GROUP)
    w1, w3, w2 = (w.reshape(grouped + w.shape[1:]) for w in (w1, w3, w2))
    wmap = lambda b, bg, nu: (bg[b], 0, 0, 0)
    grid_spec = pltpu.PrefetchScalarGridSpec(
        num_scalar_prefetch=2,
        grid=(N_BLK,),
        in_specs=[
            pl.BlockSpec((MOE_BLK * SUB, LANES), lambda b, bg, nu: (b, 0)),
            pl.BlockSpec((None, EXPERTS_PER_GROUP, D_MODEL, D_EXPERT), wmap),
            pl.BlockSpec((None, EXPERTS_PER_GROUP, D_MODEL, D_EXPERT), wmap),
            pl.BlockSpec((None, EXPERTS_PER_GROUP, D_EXPERT, D_MODEL), wmap),
        ],
        out_specs=pl.BlockSpec((MOE_BLK * SUB, LANES), lambda b, bg, nu: (b, 0)),
    )
    return pl.pallas_call(
        _ffn_kernel,
        grid_spec=grid_spec,
        out_shape=jax.ShapeDtypeStruct((N_SORTED * SUB, LANES), F32),
        compiler_params=pltpu.CompilerParams(
            dimension_semantics=("arbitrary",), vmem_limit_bytes=VMEM_LIMIT),
        name="ffn",
    )(blk_group, n_used, xs, w1, w3, w2)


def _combine_kernel(base_ref, *refs, final):
    n_buf = GATHER_AHEAD + 1
    seg_refs, (route_ref, ys_hbm, x_ref, g_ref, wa_ref) = refs[:n_buf], refs[n_buf:n_buf + 5]
    outs, (buf, sem, stage) = refs[n_buf + 5:-3], refs[-3:]
    i = pl.program_id(0)
    n = pl.num_programs(0)
    step_rows = ROW_TILES * TM_ROW

    def gather(ahead, t, enable=None):
        s = ((i + ahead) % n_buf) * ROW_TILES + t

        def make_copy(buf_row, ys_row, k):
            return pltpu.make_async_copy(ys_hbm.at[_rows(ys_row, k)],
                                         buf.at[s, _rows(buf_row, k)], sem.at[s])
        return _segment_copies(seg_refs[ahead].at[t], base_ref, make_copy, enable)

    @pl.when(i == 0)
    def _():
        for ahead in range(GATHER_AHEAD):
            for t in range(ROW_TILES):
                for _ in gather(ahead, t, ahead < n):
                    pass

    for t in range(ROW_TILES):
        slot = (i % n_buf) * ROW_TILES + t
        pltpu.make_async_copy(ys_hbm.at[_rows(0, TM_ROW)], buf.at[slot], sem.at[slot]).wait()

    def tile_stages(t):
        slot = (i % n_buf) * ROW_TILES + t
        rows = slice(t * TM_ROW, (t + 1) * TM_ROW)
        tn = (((0,), (0,)), ((), ()))
        perm = _perm_matrix(route_ref.at[t])
        cols = []
        for s in range(SUB):
            hi, lo = _split_bf16(buf[slot, _sublane(s, TM_ROW), :])
            cols.append(lax.dot_general(perm, hi, tn, preferred_element_type=F32)
                        + lax.dot_general(perm, lo, tn, preferred_element_type=F32))
            if s % 4 == 3:
                yield
        y = x_ref[rows, :] + jnp.concatenate(cols, axis=1)
        if not final:
            outs[0][rows, :] = y
            yield
            _norm_and_lowrank(y, g_ref, wa_ref, outs[1].at[rows, :], outs[2].at[rows, :])
        else:
            yield
            stage[rows, :] = _rms(y, g_ref[...])

    _interleave(*[_delayed(tile_stages(t), t) for t in range(ROW_TILES)],
                *[_delayed(gather(GATHER_AHEAD, t, i + GATHER_AHEAD < n), t)
                  for t in range(ROW_TILES)])

    if final:
        @pl.when(i < T_PROMPT // step_rows)
        def _():
            outs[0][...] = stage[...]

        @pl.when(i >= T_PROMPT // step_rows)
        def _():
            outs[1][...] = stage[...]


def _combine(seg, route, grp_base, ys, x, g, wa, final):
    m = x.shape[0]
    step_rows = ROW_TILES * TM_ROW
    nt = m // step_rows
    n_p = T_PROMPT // step_rows
    smem = functools.partial(pl.BlockSpec, (ROW_TILES, 8, LANES), memory_space=pltpu.SMEM)
    tile = (step_rows, D_MODEL)
    if final:
        out_specs = [pl.BlockSpec(tile, lambda i, *_: (jnp.minimum(i, n_p - 1), 0)),
                     pl.BlockSpec(tile, lambda i, *_: (jnp.maximum(i - n_p, 0), 0))]
        out_shape = [jax.ShapeDtypeStruct((T_PROMPT, D_MODEL), F32),
                     jax.ShapeDtypeStruct((T_SAMPLE, D_MODEL), F32)]
    else:
        out_specs = [pl.BlockSpec(tile, lambda i, *_: (i, 0)),
                     pl.BlockSpec(tile, lambda i, *_: (i, 0)),
                     pl.BlockSpec((step_rows, LANES), lambda i, *_: (i, 0))]
        out_shape = [jax.ShapeDtypeStruct((m, D_MODEL), F32),
                     jax.ShapeDtypeStruct((m, D_MODEL), BF16),
                     jax.ShapeDtypeStruct((m, LANES), BF16)]
    grid_spec = pltpu.PrefetchScalarGridSpec(
        num_scalar_prefetch=1,
        grid=(nt,),
        in_specs=[smem(lambda i, *_, ahead=ahead: (jnp.minimum(i + ahead, nt - 1), 0, 0))
                  for ahead in range(GATHER_AHEAD + 1)] + [
            pl.BlockSpec((ROW_TILES, ROUTE_ROWS, TM_ROW), lambda i, *_: (i, 0, 0)),
            pl.BlockSpec(memory_space=pl.ANY),
            pl.BlockSpec(tile, lambda i, *_: (i, 0)),
            pl.BlockSpec(g.shape, lambda i, *_: (0, 0)),
            pl.BlockSpec(wa.shape, lambda i, *_: (0, 0)),
        ],
        out_specs=out_specs,
        scratch_shapes=[pltpu.VMEM(((GATHER_AHEAD + 1) * ROW_TILES, TM_ROW * SUB, LANES), F32),
                        pltpu.SemaphoreType.DMA(((GATHER_AHEAD + 1) * ROW_TILES,)),
                        pltpu.VMEM(tile if final else (8, LANES), F32)],
    )
    return pl.pallas_call(
        functools.partial(_combine_kernel, final=final),
        grid_spec=grid_spec,
        out_shape=out_shape,
        compiler_params=pltpu.CompilerParams(
            dimension_semantics=("arbitrary",), vmem_limit_bytes=VMEM_LIMIT),
        name="combine",
    )(grp_base, *[seg] * (GATHER_AHEAD + 1), route, ys, x, g, wa)


def _prep_weights(w_in, gla_a2, cm_ws, cm_b, router_group_w, router_group_b,
                  router_expert_w, router_expert_b):
    w_z = jnp.swapaxes(w_in, 1, 2).astype(BF16)
    a0 = _w_in_offsets()["alr"][0]
    w_alr = jnp.pad(w_z[:, a0:a0 + GLA_LOWRANK], ((0, 0), (0, LANES - GLA_LOWRANK), (0, 0)))
    a2 = jnp.pad(gla_a2, ((0, 0), (0, LANES - GLA_LOWRANK), (0, 0))).astype(BF16)
    ws_p = jnp.tril(cm_ws).astype(BF16)
    small = jnp.tril(cm_ws[:, :, :DEC_SEQ, :DEC_SEQ])
    eye = jnp.eye(SEQ_PER_BLK, dtype=F32)
    ws_s = jnp.einsum("ij,lgab->lgiajb", eye, small).reshape(
        DEPTH, CM_GROUPS, ROWS_S, ROWS_S).astype(BF16)
    cmb_p = jnp.broadcast_to(jnp.transpose(cm_b, (0, 2, 1))[:, :, :, None],
                             (DEPTH, CM_CHUNK, CM_GROUPS, CM_GCH)).reshape(DEPTH, CM_CHUNK, CM_CH)
    cmb_s = jnp.tile(cmb_p[:, :DEC_SEQ], (1, SEQ_PER_BLK, 1))
    pad = LANES - MOE_GROUPS - N_EXPERTS
    w_r = jnp.pad(jnp.swapaxes(jnp.concatenate([router_group_w, router_expert_w], axis=-1), 1, 2),
                  ((0, 0), (0, pad), (0, 0)))
    r_b = jnp.pad(jnp.concatenate([router_group_b, router_expert_b], axis=-1),
                  ((0, 0), (0, pad)))[:, :, None]
    return w_z, w_alr, a2, ws_p, ws_s, cmb_p, cmb_s, w_r.astype(BF16), r_b


def kernel(x_prompt, x_sample, state_conv, state_gla, norm1_g, w_in, conv_w, gla_a2, gla_a_b,
           gla_norm_g, cm_norm_g, cm_ws, cm_b, proj_a, proj_b, proj_c, w_out, norm2_g,
           router_group_w, router_group_b, router_expert_w, router_expert_b,
           exp_w1, exp_w3, exp_w2, final_norm_g):
    (w_z, w_alr, a2, ws_p, ws_s, cmb_p, cmb_s, w_r, r_b) = _prep_weights(
        w_in, gla_a2, cm_ws, cm_b, router_group_w, router_group_b, router_expert_w,
        router_expert_b)
    pa, pb, pc, wo = (w.astype(BF16) for w in (proj_a, proj_b, proj_c, w_out))
    n1 = norm1_g[:, None, :]
    n2 = norm2_g[:, None, :]
    ab = gla_a_b[:, None, :]
    gng = gla_norm_g[:, None, :]
    cmg = cm_norm_g.reshape(DEPTH, 1, CM_CH)
    fg = final_norm_g[None, :]
    xa = x_prompt.reshape(T_PROMPT, D_MODEL)
    xb = x_sample.reshape(T_SAMPLE, D_MODEL)
    gla_s = None
    conv_p, gla_p, conv_s, cmv_s = [], [], [], []
    xin, alr = _prenorm(xa, xb, n1, w_alr, 0)
    for l in range(DEPTH):
        z, *experts = _inproj(xin, w_z, (exp_w1, exp_w3, exp_w2), l)
        acts_p, nconv, ngla = _mix_prompt(z, alr, a2, ab, conv_w, gng, cmg, ws_p, cmb_p, l)
        sc = state_conv[l]
        p2 = jnp.pad(sc, ((0, 0), (0, DEC_SEQ - 2), (0, 0))).reshape(T_SAMPLE, CONV_CH)
        p1 = jnp.pad(sc[:, 1:2], ((0, 0), (0, DEC_SEQ - 1), (0, 0))).reshape(T_SAMPLE, CONV_CH)
        acts_s, cin_s, gla_s, vrows = _mix_sample(z, alr, a2, ab, conv_w, gng, cmg, ws_s, cmb_s,
                                                  p1, p2, state_gla, gla_s, l)
        conv_p.append(nconv)
        gla_p.append(ngla)
        conv_s.append(cin_s.reshape(DEC_BATCH, DEC_SEQ, CONV_CH)[:, DEC_SEQ - (CONV_K - 1):])
        cmv_s.append(vrows.reshape(DEC_BATCH, DEC_SEQ, CM_CH))

        x, xn, route, seg, cnt = _outproj(acts_p, acts_s, z, xa, xb, pa, pb, pc, wo, n2, w_r, r_b,
                                          l)
        plan = _plan(cnt)
        n_used = plan[PLAN_USED, :1]
        grp_base = plan[PLAN_BASE, :MOE_GROUPS]
        xs = _scatter(seg, route, xn, n_used, plan[PLAN_END, :MOE_GROUPS],
                      plan[PLAN_BLOCKS, :MOE_GROUPS], grp_base)
        ys = _ffn(plan[PLAN_GROUP, :N_BLK], n_used, xs, *experts)
        if l == DEPTH - 1:
            out = _combine(seg, route, grp_base, ys, x, fg, w_alr[l], True)
        else:
            xa, xin, alr = _combine(seg, route, grp_base, ys, x, n1[l + 1], w_alr[l + 1], False)
            xb = xa

    y_prompt = out[0].reshape(BATCH, SEQ, D_MODEL)
    y_sample = out[1].reshape(DEC_BATCH, DEC_SEQ, D_MODEL)
    return (y_prompt, y_sample, jnp.stack(conv_p), jnp.stack(gla_p), jnp.stack(conv_s),
            gla_s, jnp.stack(cmv_s))
```

```python
import functools

import jax
import jax.numpy as jnp
from jax import lax
from jax.experimental import pallas as pl
from jax.experimental.pallas import tpu as pltpu

F32 = jnp.float32
BF16 = jnp.bfloat16

D_MODEL = 1024
BATCH = 8
SEQ = 2048
DEPTH = 2
DEC_BATCH = 128
DEC_SEQ = 8
CONV_K = 3
CONV_CH = 512
GLA_HEADS = 4
GLA_DK = 128
GLA_DV = 256
GLA_QK = GLA_HEADS * GLA_DK
GLA_V = GLA_HEADS * GLA_DV
GLA_LOWRANK = 16
GLA_TAU = 16.0
GLA_CHUNK = 64
CM_GROUPS = 4
CM_CHUNK = 128
CM_GCH = 128
CM_CH = 512
MOE_GROUPS = 8
EXPERTS_PER_GROUP = 8
N_EXPERTS = 64
D_EXPERT = 256
EPS = 1e-6

LANES = 128
T_PROMPT = BATCH * SEQ
T_SAMPLE = DEC_BATCH * DEC_SEQ
T_ALL = T_PROMPT + T_SAMPLE

COL_GA, COL_GB, COL_GC = 0, 1024, 2048
COL_V, COL_R, COL_Q, COL_K = 3072, 4096, 5120, 5632
COL_H, COL_CG, COL_BG, COL_U, COL_VV = 6144, 6656, 7168, 7680, 8192
Z_COLS = 8704
ACT_COLS = 2048
ROUTE_ROWS = 2 * EXPERTS_PER_GROUP
ROUTE_LRANK = EXPERTS_PER_GROUP

TM_PRE = 1024
TM_IN = 1024
TN_IN = Z_COLS // 2

W_IN_WIDTHS = (("h", 512), ("cg", 512), ("bg", 512), ("q", 512), ("k", 512), ("v", 1024),
               ("r", 1024), ("alr", GLA_LOWRANK), ("u", 512), ("vv", 512), ("ga", 1024),
               ("gb", 1024), ("gc", 1024))
Z_ORDER = ("ga", "gb", "gc", "v", "r", "q", "k", "h", "cg", "bg", "u", "vv")


def _w_in_offsets():
    off, o = {}, 0
    for name, n in W_IN_WIDTHS:
        off[name] = (o, n)
        o += n
    return off


def _w_in_copies():
    off = _w_in_offsets()
    copies = [[] for _ in range(Z_COLS // TN_IN)]
    dst = 0
    for name in Z_ORDER:
        src, n = off[name]
        while n:
            tile = dst // TN_IN
            at = dst - tile * TN_IN
            take = min(n, TN_IN - at)
            runs = copies[tile]
            if runs and runs[-1][0] + runs[-1][2] == src and runs[-1][1] + runs[-1][2] == at:
                runs[-1] = (runs[-1][0], runs[-1][1], runs[-1][2] + take)
            else:
                runs.append((src, at, take))
            src, dst, n = src + take, dst + take, n - take
    assert dst == Z_COLS
    assert all(v % 16 == 0 for runs in copies for run in runs for v in run)
    return copies
TC_MIX = 256
SEQ_PER_BLK = 16
ROWS_S = SEQ_PER_BLK * DEC_SEQ
TM_OUT = 256
ROUTE_DELAY = 3
TM_ROW = TM_OUT
ROW_TILES = 4
GATHER_AHEAD = 2
COPY_STAGE_GROUPS = 4
MOE_BLK = 256
N_BLK = T_ALL // MOE_BLK + MOE_GROUPS
N_SORTED = N_BLK * MOE_BLK
VMEM_CAPACITY = 64 * 1024 * 1024
VMEM_LIMIT = 56 * 1024 * 1024
VMEM_SLACK = 4 * 1024 * 1024


def _sigmoid(x):
    return 0.5 * jnp.tanh(0.5 * x) + 0.5


def _gelu_tanh(x):
    c = 0.7978845608028654
    half = 0.5 * x
    return half + half * jnp.tanh(x * (c + (c * 0.044715) * (x * x)))


def _log_sigmoid(x):
    log2_e = 1.4426950408889634
    ln_2 = 0.6931471805599453
    return jnp.minimum(x, 0.0) - ln_2 * jnp.log2(1.0 + jnp.exp2(-log2_e * jnp.abs(x)))


def _rms(x, g):
    ms = jnp.mean(x * x, axis=-1, keepdims=True)
    return x * lax.rsqrt(ms + EPS) * g


def _split_bf16(x):
    hi = x.astype(BF16)
    lo = (x - hi.astype(F32)).astype(BF16)
    return hi, lo


def _dot(a, b):
    return jnp.dot(a, b, preferred_element_type=F32)


def _dot_nt(a, b):
    return lax.dot_general(a, b, (((1,), (1,)), ((), ())), preferred_element_type=F32)


def _layer_spec(arr, layer):
    nd = arr.ndim - 1
    return pl.BlockSpec((None,) + arr.shape[1:], lambda *g: (layer,) + (0,) * nd)


def _two_part_specs(tile, xa, xb):
    n_a = xa.shape[0] // tile
    n_b = xb.shape[0] // tile
    return n_a, [
        pl.BlockSpec((tile, xa.shape[1]), lambda i, *_: (jnp.minimum(i, n_a - 1), 0)),
        pl.BlockSpec((tile, xb.shape[1]), lambda i, *_: (jnp.clip(i - n_a, 0, n_b - 1), 0)),
    ]


def _norm_and_lowrank(x, g_ref, wa_ref, xn_ref, a_ref):
    xn = _rms(x, g_ref[...]).astype(BF16)
    xn_ref[...] = xn
    a_ref[...] = _dot_nt(xn, wa_ref[...]).astype(BF16)


def _prenorm_kernel(xa_ref, xb_ref, g_ref, wa_ref, xn_ref, a_ref, *, n_a):
    x = jnp.where(pl.program_id(0) < n_a, xa_ref[...], xb_ref[...])
    _norm_and_lowrank(x, g_ref, wa_ref, xn_ref, a_ref)


def _prenorm(xa, xb, g, wa, layer):
    m = T_ALL
    n_a, x_specs = _two_part_specs(TM_PRE, xa, xb)
    return pl.pallas_call(
        functools.partial(_prenorm_kernel, n_a=n_a),
        grid=(m // TM_PRE,),
        in_specs=x_specs + [_layer_spec(g, layer), _layer_spec(wa, layer)],
        out_specs=[pl.BlockSpec((TM_PRE, D_MODEL), lambda i: (i, 0)),
                   pl.BlockSpec((TM_PRE, LANES), lambda i: (i, 0))],
        out_shape=[jax.ShapeDtypeStruct((m, D_MODEL), BF16),
                   jax.ShapeDtypeStruct((m, LANES), BF16)],
        compiler_params=pltpu.CompilerParams(
            dimension_semantics=("arbitrary",), vmem_limit_bytes=VMEM_LIMIT),
        name="prenorm",
    )(xa, xb, g, wa)


EXPERT_SHAPES = ((D_MODEL, D_EXPERT), (D_MODEL, D_EXPERT), (D_EXPERT, D_MODEL))


def _inproj_kernel(xn_ref, w_hbm, *refs, layer):
    n = len(EXPERT_SHAPES)
    z_ref, w_buf, sem = refs[n], refs[-2], refs[-1]
    j, i = pl.program_id(0), pl.program_id(1)

    def tile_copies(tile):
        return [pltpu.make_async_copy(w_hbm.at[layer, pl.ds(src, rows)],
                                      w_buf.at[tile, pl.ds(at, rows)], sem.at[tile])
                for src, at, rows in _w_in_copies()[tile]]

    @pl.when((j == 0) & (i == 0))
    def _():
        for tile in range(w_buf.shape[0]):
            for cp in tile_copies(tile):
                cp.start()

    for tile in range(w_buf.shape[0]):
        @pl.when((j == tile) & (i == 0))
        def _():
            for cp in tile_copies(tile):
                cp.wait()

    for src, dst in zip(refs[:n], refs[n + 1:]):
        dst[...] = src[...].astype(BF16)
    z_ref[...] = _dot_nt(xn_ref[...], w_buf[j]).astype(BF16)


def _inproj(xn, w, experts, layer):
    m = xn.shape[0]
    n_j, n_i = Z_COLS // TN_IN, m // TM_IN
    per_step = -(-N_EXPERTS // (n_j * n_i))
    assert N_EXPERTS % per_step == 0
    expert = lambda j, i: jnp.minimum(j * n_i + i, N_EXPERTS // per_step - 1)
    vmem_bytes = (2 * Z_COLS * D_MODEL + 2 * 2 * TM_IN * (TN_IN + D_MODEL)
                  + 2 * per_step * (4 + 2) * sum(a * b for a, b in EXPERT_SHAPES))
    assert vmem_bytes + VMEM_SLACK <= VMEM_CAPACITY
    return pl.pallas_call(
        functools.partial(_inproj_kernel, layer=layer),
        grid=(n_j, n_i),
        in_specs=[
            pl.BlockSpec((TM_IN, D_MODEL), lambda j, i: (i, 0)),
            pl.BlockSpec(memory_space=pl.ANY),
        ] + [pl.BlockSpec((None, per_step) + s, lambda j, i: (layer, expert(j, i), 0, 0))
             for s in EXPERT_SHAPES],
        out_specs=[pl.BlockSpec((TM_IN, TN_IN), lambda j, i: (i, j))]
        + [pl.BlockSpec((per_step,) + s, lambda j, i: (expert(j, i), 0, 0))
           for s in EXPERT_SHAPES],
        out_shape=[jax.ShapeDtypeStruct((m, Z_COLS), BF16)]
        + [jax.ShapeDtypeStruct((N_EXPERTS,) + s, BF16) for s in EXPERT_SHAPES],
        scratch_shapes=[pltpu.VMEM((n_j, TN_IN, D_MODEL), BF16), pltpu.SemaphoreType.DMA((n_j,))],
        compiler_params=pltpu.CompilerParams(
            dimension_semantics=("arbitrary", "arbitrary"),
            vmem_limit_bytes=vmem_bytes + VMEM_SLACK),
        name="inproj",
    )(xn, w, *experts)


def _gla_log_decay(alr_ref, a2_ref, ab_ref):
    la = _log_sigmoid(_dot(alr_ref[...], a2_ref[...]) + ab_ref[...]) * (1.0 / GLA_TAU)
    return _split_bf16(la)


def _masked_sum(mask, la_hi, la_lo):
    m = jnp.where(mask, 1.0, 0.0).astype(BF16)
    return _dot(m, la_hi) + _dot(m, la_lo)


def _gla_decay_terms(alr_ref, a2_ref, ab_ref, tril_mask, same_mask):
    la_hi, la_lo = _gla_log_decay(alr_ref, a2_ref, ab_ref)
    return (la_hi, la_lo, _masked_sum(tril_mask, la_hi, la_lo),
            _masked_sum(same_mask, la_hi, la_lo))


def _gla_out_gate(o, g_ref, r):
    return _rms(o, g_ref[...]) * (r * _sigmoid(r))


def _chunk_mlp_group(g, bgu_ref, vv_ref, cmg_ref, ws_ref, cmb_ref, n_chunks):
    sl = slice(g * CM_GCH, (g + 1) * CM_GCH)
    ug = _gelu_tanh(bgu_ref[:, CONV_CH + g * CM_GCH:CONV_CH + (g + 1) * CM_GCH])
    vg = _rms(_gelu_tanh(vv_ref[:, sl]).astype(F32), cmg_ref[:, sl])
    vgb = vg.astype(BF16)
    rows = []
    for j in range(n_chunks):
        rs = slice(j * CM_CHUNK, (j + 1) * CM_CHUNK)
        rows.append(_dot(ws_ref[g], vgb[rs]) + cmb_ref[:, sl])
    s = rows[0] if n_chunks == 1 else jnp.concatenate(rows, axis=0)
    return ug * s, vg


def _chunk_mlp(bgu_ref, vv_ref, cmg_ref, ws_ref, cmb_ref, n_chunks):
    parts = [_chunk_mlp_group(g, bgu_ref, vv_ref, cmg_ref, ws_ref, cmb_ref, n_chunks)
             for g in range(CM_GROUPS)]
    return (jnp.concatenate([p[0] for p in parts], axis=1),
            jnp.concatenate([p[1] for p in parts], axis=1))


N_Z_VIEWS = 7
N_MIX_W = 7
SEQ_PER_STEP = 4
SEQ_STAGGER = 3


def _mix_prompt_kernel(*refs):
    n_z = N_Z_VIEWS * SEQ_PER_STEP
    weights = refs[n_z:n_z + N_MIX_W]
    acts_ref, nconv_ref, ngla_ref, st_ref, carry_ref = refs[n_z + N_MIX_W:]

    @pl.when(pl.program_id(1) == 0)
    def _():
        st_ref[...] = jnp.zeros_like(st_ref)
        carry_ref[...] = jnp.zeros_like(carry_ref)

    rr = lax.broadcasted_iota(jnp.int32, (TC_MIX, TC_MIX), 0)
    cc = lax.broadcasted_iota(jnp.int32, (TC_MIX, TC_MIX), 1)
    tril = ((rr >> 6) == (cc >> 6)) & (cc <= rr)
    masks = (tril, jnp.where(tril, 1.0, 0.0).astype(BF16),
             jnp.where(rr - cc == 1, 1.0, 0.0).astype(BF16),
             jnp.where(rr - cc == 2, 1.0, 0.0).astype(BF16))

    _interleave(*[
        _delayed(_mix_prompt_seq(*refs[N_Z_VIEWS * s:N_Z_VIEWS * (s + 1)], *weights,
                                acts_ref.at[s], nconv_ref.at[s], st_ref.at[s], carry_ref.at[s],
                                masks),
                s * SEQ_STAGGER)
        for s in range(SEQ_PER_STEP)])

    @pl.when(pl.program_id(1) == pl.num_programs(1) - 1)
    def _():
        for s in range(SEQ_PER_STEP):
            for hd in range(GLA_HEADS):
                ngla_ref[s, 0, hd] = st_ref[s, hd].T


def _mix_prompt_seq(v_ref, r_ref, qk_ref, hcg_ref, bgu_ref, vv_ref, alr_ref,
                    a2_ref, ab_ref, cw_ref, gng_ref, cmg_ref, ws_ref, cmb_ref,
                    acts_ref, nconv_ref, st_ref, carry_ref, masks):
    tc = TC_MIX

    h = hcg_ref[:, :CONV_CH].astype(F32)
    cg = hcg_ref[:, CONV_CH:].astype(F32)
    bg = bgu_ref[:, :CONV_CH].astype(F32)
    cin = cg * h
    tril, tril_01, shift_1, shift_2 = masks
    cin_b = cin.astype(BF16)
    x1 = _dot(shift_1, cin_b)
    x2 = _dot(shift_2, cin_b)
    conv = x2 * cw_ref[0:1, :] + x1 * cw_ref[1:2, :] + cin * cw_ref[2:3, :]
    c0 = carry_ref[0:1, :]
    c1 = carry_ref[1:2, :]
    row8 = lax.broadcasted_iota(jnp.int32, (8, 1), 0)
    head = jnp.where(row8 == 0, c0 * cw_ref[0:1, :] + c1 * cw_ref[1:2, :],
                     jnp.where(row8 == 1, c1 * cw_ref[0:1, :], 0.0))
    conv = jnp.concatenate([conv[0:8] + head, conv[8:]], axis=0)
    acts_ref[:, 0:CONV_CH] = (bg * conv).astype(BF16)
    carry_ref[0:2, :] = cin[tc - 2:tc, :]
    nconv_ref[0] = cin[tc - 2:tc, :]
    yield

    la_hi, la_lo = _gla_log_decay(alr_ref, a2_ref, ab_ref)
    b = _dot(tril_01, la_hi) + _dot(tril_01, la_lo)
    yield
    n_chunks = tc // GLA_CHUNK
    b_last = [b[(c + 1) * GLA_CHUNK - 1:(c + 1) * GLA_CHUNK, :] for c in range(n_chunks)]
    bl = jnp.concatenate([jnp.broadcast_to(r_, (GLA_CHUNK, GLA_QK)) for r_ in b_last], axis=0)
    q = qk_ref[:, :GLA_QK].astype(F32) * (GLA_DK ** -0.5)
    k = qk_ref[:, GLA_QK:].astype(F32)
    q_t = (q * jnp.exp(b)).astype(BF16)
    k_t = (k * jnp.exp(-b)).astype(BF16)
    k_end = (k * jnp.exp(bl - b)).astype(BF16)
    yield
    states = [st_ref[hd] for hd in range(GLA_HEADS)]
    k_cols = [slice(hd * GLA_DK, (hd + 1) * GLA_DK) for hd in range(GLA_HEADS)]
    v_cols = [slice(hd * GLA_DV, (hd + 1) * GLA_DV) for hd in range(GLA_HEADS)]
    o_intra = []
    for hd in range(GLA_HEADS):
        att = jnp.where(tril, _dot_nt(q_t[:, k_cols[hd]], k_t[:, k_cols[hd]]), 0.0).astype(BF16)
        o_intra.append(_dot(att, v_ref[:, v_cols[hd]]))
    yield

    o_rows = [[] for _ in range(GLA_HEADS)]
    assert n_chunks == CM_GROUPS
    for c in range(n_chunks):
        rs = slice(c * GLA_CHUNK, (c + 1) * GLA_CHUNK)
        for hd in range(GLA_HEADS):
            ks = k_cols[hd]
            st = states[hd]
            o_rows[hd].append(o_intra[hd][rs] + _dot_nt(q_t[rs, ks], st.astype(BF16)))
            upd = lax.dot_general(v_ref[rs, v_cols[hd]], k_end[rs, ks], (((0,), (0,)), ((), ())),
                                  preferred_element_type=F32)
            states[hd] = jnp.exp(b_last[c][:, ks]) * st + upd
        us, _ = _chunk_mlp_group(c, bgu_ref, vv_ref, cmg_ref, ws_ref, cmb_ref, tc // CM_CHUNK)
        acts_ref[:, CONV_CH + GLA_V + c * CM_GCH:CONV_CH + GLA_V + (c + 1) * CM_GCH] = (
            us.astype(BF16))
        yield

    for hd in range(GLA_HEADS):
        o = jnp.concatenate(o_rows[hd], axis=0)
        r = r_ref[:, v_cols[hd]]
        acts_ref[:, CONV_CH + hd * GLA_DV:CONV_CH + (hd + 1) * GLA_DV] = (
            _gla_out_gate(o, gng_ref, r).astype(BF16))
        st_ref[hd] = states[hd]
        if hd % 2 == 1:
            yield


def _z_specs(rows, row_map):
    def spec(width, col):
        blk = col // width
        return pl.BlockSpec((rows, width), lambda *g: (row_map(*g), blk))
    return [spec(1024, COL_V), spec(1024, COL_R), spec(1024, COL_Q), spec(1024, COL_H),
            spec(1024, COL_BG), spec(512, COL_VV)]


def _mix_prompt(z, alr, a2, ab, cw, gng, cmg, ws, cmb, layer):
    nt = SEQ // TC_MIX
    nb = BATCH // SEQ_PER_STEP
    small = (a2, ab, cw, gng, cmg, ws, cmb)
    assert len(small) == N_MIX_W
    in_specs, args = [], []
    for s in range(SEQ_PER_STEP):
        row_map = lambda b, c, s=s: (b + s * nb) * nt + c
        in_specs += _z_specs(TC_MIX, row_map) + [
            pl.BlockSpec((TC_MIX, LANES), lambda b, c, row_map=row_map: (row_map(b, c), 0))]
        args += [z] * (N_Z_VIEWS - 1) + [alr]
    in_specs += [_layer_spec(a, layer) for a in small]
    acts, nconv, ngla = pl.pallas_call(
        _mix_prompt_kernel,
        grid=(nb, nt),
        in_specs=in_specs,
        out_specs=[
            pl.BlockSpec((SEQ_PER_STEP, TC_MIX, ACT_COLS), lambda b, c: (0, b * nt + c, 0)),
            pl.BlockSpec((SEQ_PER_STEP, 1, CONV_K - 1, CONV_CH), lambda b, c: (0, b, 0, 0)),
            pl.BlockSpec((SEQ_PER_STEP, 1, GLA_HEADS, GLA_DK, GLA_DV),
                         lambda b, c: (0, b, 0, 0, 0)),
        ],
        out_shape=[
            jax.ShapeDtypeStruct((SEQ_PER_STEP, T_PROMPT // SEQ_PER_STEP, ACT_COLS), BF16),
            jax.ShapeDtypeStruct((SEQ_PER_STEP, nb, CONV_K - 1, CONV_CH), F32),
            jax.ShapeDtypeStruct((SEQ_PER_STEP, nb, GLA_HEADS, GLA_DK, GLA_DV), F32),
        ],
        scratch_shapes=[pltpu.VMEM((SEQ_PER_STEP, GLA_HEADS, GLA_DV, GLA_DK), F32),
                        pltpu.VMEM((SEQ_PER_STEP, 8, CONV_CH), F32)],
        compiler_params=pltpu.CompilerParams(
            dimension_semantics=("arbitrary", "arbitrary"), vmem_limit_bytes=VMEM_LIMIT),
        name="mix_prompt",
    )(*args, *small)
    return (acts.reshape(T_PROMPT, ACT_COLS), nconv.reshape(BATCH, CONV_K - 1, CONV_CH),
            ngla.reshape(BATCH, GLA_HEADS, GLA_DK, GLA_DV))


def _mix_sample_body(v_ref, r_ref, qk_ref, hcg_ref, bgu_ref, vv_ref, alr_ref,
                       a2_ref, ab_ref, cw_ref, gng_ref, cmg_ref, ws_ref, cmb_ref,
                       p1_ref, p2_ref, s0_ref,
                       acts_ref, cin_ref, ns_ref, vrow_ref):
    n = ROWS_S

    pos = lax.broadcasted_iota(jnp.int32, (n, 1), 0) & (DEC_SEQ - 1)
    h = hcg_ref[:, :CONV_CH].astype(F32)
    cg = hcg_ref[:, CONV_CH:].astype(F32)
    bg = bgu_ref[:, :CONV_CH].astype(F32)
    cin = cg * h
    x1 = jnp.where(pos >= 1, pltpu.roll(cin, 1, 0), p1_ref[...])
    x2 = jnp.where(pos >= 2, pltpu.roll(cin, 2, 0), p2_ref[...])
    conv = x2 * cw_ref[0:1, :] + x1 * cw_ref[1:2, :] + cin * cw_ref[2:3, :]
    acts_ref[:, 0:CONV_CH] = (bg * conv).astype(BF16)
    cin_ref[...] = cin

    rr = lax.broadcasted_iota(jnp.int32, (n, n), 0)
    cc = lax.broadcasted_iota(jnp.int32, (n, n), 1)
    same = (rr >> 3) == (cc >> 3)
    tril = same & (cc <= rr)
    la_hi, la_lo, b, bl = _gla_decay_terms(alr_ref, a2_ref, ab_ref, tril, same)
    q = qk_ref[:, :GLA_QK].astype(F32) * (GLA_DK ** -0.5)
    k = qk_ref[:, GLA_QK:].astype(F32)
    q_t = (q * jnp.exp(b)).astype(BF16)
    k_t = (k * jnp.exp(-b)).astype(BF16)
    k_end = k * jnp.exp(bl - b)
    la_hi = la_hi.astype(F32)
    la_lo = la_lo.astype(F32)
    row_seq = lax.broadcasted_iota(jnp.int32, (n, GLA_DK), 0) >> 3
    seq3 = lax.broadcasted_iota(jnp.int32, (SEQ_PER_BLK, GLA_DK, n), 0)
    lane_seq3 = lax.broadcasted_iota(jnp.int32, (SEQ_PER_BLK, GLA_DK, n), 2) >> 3
    mask3 = seq3 == lane_seq3
    ones = jnp.ones((n, GLA_DV), BF16)
    big = SEQ_PER_BLK * GLA_DK

    def per_seq(x_tr):
        x3 = jnp.where(mask3, x_tr[None, :, :], 0.0)
        return x3.reshape(big, n).astype(BF16)

    for hd in range(GLA_HEADS):
        ks = slice(hd * GLA_DK, (hd + 1) * GLA_DK)
        vs = slice(hd * GLA_DV, (hd + 1) * GLA_DV)
        qh = q_t[:, ks]
        vh = v_ref[:, vs]
        att = jnp.where(tril, _dot_nt(qh, k_t[:, ks]), 0.0).astype(BF16)
        o_intra = _dot(att, vh)
        s_old = s0_ref[:, hd].reshape(big, GLA_DV)
        zero = jnp.zeros_like(qh)
        q_big = jnp.concatenate(
            [jnp.where(row_seq == j, qh, zero) for j in range(SEQ_PER_BLK)], axis=1)
        o = o_intra + _dot(q_big, s_old.astype(BF16))
        dlog = _dot(per_seq(la_hi[:, ks].T), ones) + _dot(per_seq(la_lo[:, ks].T), ones)
        upd = _dot(per_seq(k_end[:, ks].T), vh)
        s_new = jnp.exp(dlog) * s_old + upd
        ns_ref[:, hd] = s_new.reshape(SEQ_PER_BLK, GLA_DK, GLA_DV)
        r = r_ref[:, vs]
        acts_ref[:, CONV_CH + hd * GLA_DV:CONV_CH + (hd + 1) * GLA_DV] = (
            _gla_out_gate(o, gng_ref, r).astype(BF16))

    us, vg = _chunk_mlp(bgu_ref, vv_ref, cmg_ref, ws_ref, cmb_ref, 1)
    acts_ref[:, CONV_CH + GLA_V:] = us.astype(BF16)
    vrow_ref[...] = vg


N_MIX_S_IN = 17


def _mix_sample_kernel(*refs, layer):
    if layer == 0:
        @pl.when(pl.program_id(0) == 0)
        def _():
            _mix_sample_body(*refs)

        @pl.when(pl.program_id(0) > 0)
        def _():
            ns_ref = refs[N_MIX_S_IN + 2]
            ns_ref[...] = jnp.zeros_like(ns_ref)
    else:
        _mix_sample_body(*refs[:N_MIX_S_IN], *refs[N_MIX_S_IN + 1:])


def _mix_sample(z, alr, a2, ab, cw, gng, cmg, ws, cmb, p1, p2, s0, ns_all, layer):
    row0 = T_PROMPT // ROWS_S
    n_i = DEC_BATCH // SEQ_PER_BLK
    n_pass = DEPTH if layer == 0 else 1
    blk = lambda p, i: jnp.where(p == 0, i, n_i - 1)
    row_map = lambda p, i: row0 + blk(p, i)
    slot = lambda p, i: (layer + p, i, 0, 0, 0)
    small = (a2, ab, cw, gng, cmg, ws, cmb)
    state_blk = (None, SEQ_PER_BLK, GLA_HEADS, GLA_DK, GLA_DV)
    in_specs = _z_specs(ROWS_S, row_map) + [
        pl.BlockSpec((ROWS_S, LANES), lambda p, i: (row_map(p, i), 0)),
    ] + [_layer_spec(a, layer) for a in small] + [
        pl.BlockSpec((ROWS_S, CONV_CH), lambda p, i: (blk(p, i), 0)),
        pl.BlockSpec((ROWS_S, CONV_CH), lambda p, i: (blk(p, i), 0)),
        pl.BlockSpec(state_blk, lambda p, i: (layer, blk(p, i), 0, 0, 0)),
    ]
    args = (z, z, z, z, z, z, alr, *small, p1, p2, s0)
    assert len(args) == N_MIX_S_IN
    aliases = {}
    if layer > 0:
        in_specs.append(pl.BlockSpec(memory_space=pl.ANY))
        args += (ns_all,)
        aliases = {N_MIX_S_IN: 2}
    return pl.pallas_call(
        functools.partial(_mix_sample_kernel, layer=layer),
        grid=(n_pass, n_i),
        in_specs=in_specs,
        out_specs=[
            pl.BlockSpec((ROWS_S, ACT_COLS), lambda p, i: (blk(p, i), 0)),
            pl.BlockSpec((ROWS_S, CONV_CH), lambda p, i: (blk(p, i), 0)),
            pl.BlockSpec(state_blk, slot),
            pl.BlockSpec((ROWS_S, CM_CH), lambda p, i: (blk(p, i), 0)),
        ],
        out_shape=[
            jax.ShapeDtypeStruct((T_SAMPLE, ACT_COLS), BF16),
            jax.ShapeDtypeStruct((T_SAMPLE, CONV_CH), F32),
            jax.ShapeDtypeStruct((DEPTH, DEC_BATCH, GLA_HEADS, GLA_DK, GLA_DV), F32),
            jax.ShapeDtypeStruct((T_SAMPLE, CM_CH), F32),
        ],
        input_output_aliases=aliases,
        compiler_params=pltpu.CompilerParams(
            dimension_semantics=("arbitrary", "arbitrary"), vmem_limit_bytes=VMEM_LIMIT),
        name="mix_sample",
    )(*args)


def _delayed(gen, n_stages):
    for _ in range(n_stages):
        yield
    yield from gen


def _interleave(*stage_lists):
    pending = list(stage_lists)
    while pending:
        for gen in list(pending):
            if next(gen, StopIteration) is StopIteration:
                pending.remove(gen)


def _outproj_main(i, n_xa, actp_ref, acts_ref, gates_ref, xa_ref, xb_ref, pa_ref, pb_ref, pc_ref,
                  wo_ref, n2_ref, xo_ref, xn_ref, xnb_ref):
    acts = jnp.where(i < T_PROMPT // TM_OUT, actp_ref[...], acts_ref[...])
    x_in = jnp.where(i < n_xa, xa_ref[...], xb_ref[...])
    yb = _dot(acts[:, CONV_CH:CONV_CH + GLA_V], pb_ref[...])
    ga = _sigmoid(gates_ref[:, COL_GA:COL_GA + D_MODEL])
    ya = _dot(acts[:, :CONV_CH], pa_ref[...])
    gb = _sigmoid(gates_ref[:, COL_GB:COL_GB + D_MODEL])
    yield
    yc = _dot(acts[:, CONV_CH + GLA_V:], pc_ref[...])
    gc = _sigmoid(gates_ref[:, COL_GC:COL_GC + D_MODEL])
    mix = ga * ya + gb * yb
    yield
    mix = mix + gc * yc
    x = x_in + _dot(mix.astype(BF16), wo_ref[...])
    xo_ref[...] = x
    yield
    xn = _rms(x, n2_ref[...])
    xn_ref[...] = xn
    xnb = xn.astype(BF16)
    yield
    xnb_ref[...] = xnb


def _outproj_route(live, xnb_ref, wr_ref, rb_ref, route_ref, seg_ref, cnt_ref):
    tm = TM_OUT
    logits = _dot_nt(wr_ref[...], xnb_ref[...]) + rb_ref[...]
    yield
    n_sub = EXPERTS_PER_GROUP
    sub = lax.broadcasted_iota(jnp.int32, (n_sub, tm), 0).astype(F32)
    neg = jnp.float32(-jnp.inf)
    lg = logits[0:MOE_GROUPS]
    gmax = jnp.max(lg, axis=0, keepdims=True)
    grp = jnp.min(jnp.where(lg == gmax, sub, 1e9), axis=0, keepdims=True)
    p_grp = 1.0 / jnp.sum(jnp.exp(lg - gmax), axis=0, keepdims=True)
    yield
    le = logits[MOE_GROUPS:MOE_GROUPS + n_sub]
    for g in range(1, MOE_GROUPS):
        le = jnp.where(grp == g, logits[MOE_GROUPS + g * n_sub:MOE_GROUPS + (g + 1) * n_sub], le)
    v1 = jnp.max(le, axis=0, keepdims=True)
    i1 = jnp.min(jnp.where(le == v1, sub, 1e9), axis=0, keepdims=True)
    le2 = jnp.where(sub == i1, neg, le)
    v2 = jnp.max(le2, axis=0, keepdims=True)
    i2 = jnp.min(jnp.where(le2 == v2, sub, 1e9), axis=0, keepdims=True)
    yield
    t = jnp.exp(v2 - v1)
    g1 = p_grp / (1.0 + t)
    g2 = p_grp * t / (1.0 + t)
    gate_t = jnp.where(sub == i1, g1, jnp.where(sub == i2, g2, 0.0))

    onehot_t = jnp.where(sub == grp, 1.0, 0.0)
    csum = jnp.where(live, jnp.sum(onehot_t, axis=1, keepdims=True), 0.0)
    rr = lax.broadcasted_iota(jnp.int32, (tm, tm), 0)
    cc = lax.broadcasted_iota(jnp.int32, (tm, tm), 1)
    earlier = jnp.where(rr < cc, 1.0, 0.0).astype(BF16)
    padded = jnp.concatenate([onehot_t, jnp.zeros_like(onehot_t)], axis=0).astype(BF16)
    same_before = _dot(padded, earlier)[0:n_sub]
    yield
    lower = jnp.sum(jnp.where(sub < grp, csum, 0.0), axis=0, keepdims=True)
    lrank = lower + jnp.sum(onehot_t * same_before, axis=0, keepdims=True)
    route_ref[0:n_sub, :] = gate_t
    route_ref[n_sub:, :] = jnp.broadcast_to(lrank, (n_sub, tm))
    carry = cnt_ref[...]
    lane = lax.broadcasted_iota(jnp.int32, (n_sub, LANES), 1)
    seg_ref[...] = jnp.where(lane == SEG_START, carry,
                             jnp.where(lane == SEG_LEN, csum, 0.0)).astype(jnp.int32)
    cnt_ref[...] = carry + csum


def _outproj_kernel(actp_ref, acts_ref, gates_ref, xa_ref, xb_ref, pa_ref, pb_ref, pc_ref, wo_ref,
                    n2_ref, wr_ref, rb_ref, xo_ref, xn_ref, route_ref, seg_ref, cnt_ref, xnb_ref,
                    *, n_xa):
    i = pl.program_id(0)

    @pl.when(i == 0)
    def _():
        cnt_ref[...] = jnp.zeros_like(cnt_ref)
        xnb_ref[...] = jnp.zeros_like(xnb_ref)

    tile = jnp.minimum(i, pl.num_programs(0) - 2)
    _interleave(
        _delayed(_outproj_route(i >= 1, xnb_ref, wr_ref, rb_ref, route_ref, seg_ref, cnt_ref),
                 ROUTE_DELAY),
        _outproj_main(tile, n_xa, actp_ref, acts_ref, gates_ref, xa_ref, xb_ref, pa_ref, pb_ref,
                      pc_ref, wo_ref, n2_ref, xo_ref, xn_ref, xnb_ref))


def _lane_prefix(v):
    rr = lax.broadcasted_iota(jnp.int32, (LANES, LANES), 0)
    cc = lax.broadcasted_iota(jnp.int32, (LANES, LANES), 1)
    earlier = jnp.where(rr < cc, 1.0, 0.0).astype(BF16)
    return _dot(jnp.broadcast_to(v, (8, LANES)).astype(BF16), earlier)[0:1]


def _outproj(acts_p, acts_s, z, xa, xb, pa, pb, pc, wo, n2, wr, rb, layer):
    m = T_ALL
    nt = m // TM_OUT
    row = lambda i: (jnp.minimum(i, nt - 1), 0)
    routed = lambda i: (jnp.maximum(i - 1, 0), 0, 0)
    _, act_specs = _two_part_specs(TM_OUT, acts_p, acts_s)
    n_xa, x_specs = _two_part_specs(TM_OUT, xa, xb)
    weights = (pa, pb, pc, wo, n2, wr, rb)
    return pl.pallas_call(
        functools.partial(_outproj_kernel, n_xa=n_xa),
        grid=(nt + 1,),
        in_specs=act_specs + [pl.BlockSpec((TM_OUT, 3 * D_MODEL), row)] + x_specs
        + [_layer_spec(w, layer) for w in weights],
        out_specs=[
            pl.BlockSpec((TM_OUT, D_MODEL), row),
            pl.BlockSpec((TM_OUT, D_MODEL), row),
            pl.BlockSpec((None, ROUTE_ROWS, TM_OUT), routed),
            pl.BlockSpec((None, 8, LANES), routed),
            pl.BlockSpec((8, LANES), lambda i: (0, 0)),
        ],
        scratch_shapes=[pltpu.VMEM((TM_OUT, D_MODEL), BF16)],
        out_shape=[
            jax.ShapeDtypeStruct((m, D_MODEL), F32),
            jax.ShapeDtypeStruct((m, D_MODEL), F32),
            jax.ShapeDtypeStruct((m // TM_OUT, ROUTE_ROWS, TM_OUT), F32),
            jax.ShapeDtypeStruct((m // TM_OUT, 8, LANES), jnp.int32),
            jax.ShapeDtypeStruct((8, LANES), F32),
        ],
        compiler_params=pltpu.CompilerParams(
            dimension_semantics=("arbitrary",), vmem_limit_bytes=VMEM_LIMIT),
        name="outproj",
    )(acts_p, acts_s, z, xa, xb, *weights)


PLAN_GROUP, PLAN_USED, PLAN_END, PLAN_BLOCKS, PLAN_BASE = 0, 1, 2, 3, 4


def _plan_kernel(cnt_ref, plan_ref):
    sub_i = lax.broadcasted_iota(jnp.int32, (8, LANES), 0)
    lane8_i = lax.broadcasted_iota(jnp.int32, (8, LANES), 1)
    cnt = jnp.sum(jnp.where(sub_i == lane8_i, cnt_ref[...], 0.0), axis=0, keepdims=True)
    blocks = jnp.floor((cnt + (MOE_BLK - 1)) * (1.0 / MOE_BLK))
    start = _lane_prefix(blocks)
    end = start + blocks
    lane_i = lax.broadcasted_iota(jnp.int32, (1, LANES), 1)
    lane = lane_i.astype(F32)
    grp_of_blk = jnp.zeros((1, LANES), F32)
    for g in range(MOE_GROUPS):
        end_g = jnp.sum(jnp.where(lane_i == g, end, 0.0), axis=-1, keepdims=True)
        grp_of_blk += jnp.where(lane >= end_g, 1.0, 0.0)
    grp_of_blk = jnp.minimum(grp_of_blk, MOE_GROUPS - 1)
    n_used = jnp.sum(blocks, axis=-1, keepdims=True)
    row = lax.broadcasted_iota(jnp.int32, (8, LANES), 0)
    plan_ref[...] = jnp.where(
        row == PLAN_GROUP, grp_of_blk,
        jnp.where(row == PLAN_USED, n_used,
                  jnp.where(row == PLAN_END, end,
                            jnp.where(row == PLAN_BLOCKS, blocks,
                                      jnp.where(row == PLAN_BASE, start * MOE_BLK, 0.0))))
    ).astype(jnp.int32)


def _plan(cnt):
    return pl.pallas_call(
        _plan_kernel,
        out_shape=jax.ShapeDtypeStruct((8, LANES), jnp.int32),
        name="plan",
    )(cnt)


SEG_START, SEG_LEN = 0, 1


def _segment_copies(seg_ref, base_ref, make_copy, enable=None, both_queues=False):
    local = 0
    for g in range(MOE_GROUPS):
        n = seg_ref[g, SEG_LEN]
        first = base_ref[g] + seg_ref[g, SEG_START]
        k = TM_ROW
        piece = g
        while k >= 1:
            done = n & ~(2 * k - 1)
            wanted = (n & k) != 0
            @pl.when(wanted if enable is None else wanted & enable)
            def _():
                make_copy(local + done, first + done, k).start(
                    priority=piece % 2 if both_queues else 0)
            k //= 2
            piece += 1
        local = local + n
        if g % COPY_STAGE_GROUPS == COPY_STAGE_GROUPS - 1:
            yield


def _perm_matrix(route_ref):
    row = lax.broadcasted_iota(jnp.int32, (TM_ROW, TM_ROW), 0).astype(F32)
    return jnp.where(row == route_ref[ROUTE_LRANK:ROUTE_LRANK + 1, :], 1.0, 0.0).astype(BF16)


SUB = 8
SUB_X = D_MODEL // 2 // LANES
SUB_GATE = SUB_X
HI_MASK = -65536


def _rows(first_row, n_rows):
    return pl.ds(pl.multiple_of(first_row * SUB, SUB), n_rows * SUB)


def _sublane(s, n_rows):
    return pl.ds(s, n_rows, stride=SUB)


def _scatter_kernel(nu_ref, end_ref, nb_ref, base_ref, seg_ref, route_ref, xn_ref, xs_ref,
                    zbuf, sbuf, sems):
    i = pl.program_id(0)
    n = pl.num_programs(0)
    sem = sems.at[0]

    def zero_block(b):
        return pltpu.make_async_copy(zbuf, xs_ref.at[_rows(b * MOE_BLK, MOE_BLK)], sem)

    def each_unfilled_block(fn):
        for g in range(MOE_GROUPS):
            @pl.when(nb_ref[g] > 0)
            def _():
                fn(zero_block(end_ref[g] - 1))
        for b in range(T_ALL // MOE_BLK, N_BLK):
            @pl.when(b >= nu_ref[0])
            def _():
                fn(zero_block(b))

    @pl.when(i == 0)
    def _():
        zbuf[...] = jnp.zeros_like(zbuf)
        sbuf[...] = jnp.zeros_like(sbuf)
        each_unfilled_block(lambda c: c.start())
        each_unfilled_block(lambda c: c.wait())

    def tile_done(s):
        pltpu.make_async_copy(sbuf.at[s], xs_ref.at[_rows(0, TM_ROW)], sems.at[s]).wait()

    @pl.when(i >= 2)
    def _():
        for t in range(ROW_TILES):
            tile_done((i % 2) * ROW_TILES + t)

    def sort_tile(t):
        slot = (i % 2) * ROW_TILES + t
        route = route_ref.at[t]
        perm = _perm_matrix(route)
        xn = xn_ref[t * TM_ROW:(t + 1) * TM_ROW, :]
        xs = lax.bitcast_convert_type(_dot(perm, xn.astype(BF16)), jnp.int32)
        yield
        half = D_MODEL // 2
        for s in range(SUB_X):
            hi = xs[:, s * LANES:(s + 1) * LANES] & HI_MASK
            lo = lax.shift_right_logical(xs[:, half + s * LANES:half + (s + 1) * LANES], 16)
            sbuf[slot, _sublane(s, TM_ROW), :] = hi | lo
        yield
        gate_t = route[0:EXPERTS_PER_GROUP, :]
        gate_t = jnp.concatenate(
            [gate_t, jnp.zeros((LANES - EXPERTS_PER_GROUP, TM_ROW), F32)], axis=0)
        g1 = gate_t.astype(BF16)
        r1 = gate_t - g1.astype(F32)
        g2 = r1.astype(BF16)
        g3 = (r1 - g2.astype(F32)).astype(BF16)
        sbuf[slot, _sublane(SUB_GATE, TM_ROW), :] = lax.bitcast_convert_type(
            _dot_nt(perm, g1) + _dot_nt(perm, g2) + _dot_nt(perm, g3), jnp.int32)

    _interleave(*[_delayed(sort_tile(t), t) for t in range(ROW_TILES)])

    for t in range(ROW_TILES):
        def make_copy(src_row, dst_row, k, slot=(i % 2) * ROW_TILES + t):
            return pltpu.make_async_copy(sbuf.at[slot, _rows(src_row, k)],
                                         xs_ref.at[_rows(dst_row, k)], sems.at[slot])
        for _ in _segment_copies(seg_ref.at[t], base_ref, make_copy, both_queues=True):
            pass

    @pl.when(i == n - 1)
    def _():
        for t in range(ROW_TILES):
            tile_done((i % 2) * ROW_TILES + t)

            @pl.when(n > 1)
            def _():
                tile_done((1 - i % 2) * ROW_TILES + t)


def _scatter(seg, route, xn, n_used, grp_end, grp_blocks, grp_base):
    m = xn.shape[0]
    step_rows = ROW_TILES * TM_ROW
    grid_spec = pltpu.PrefetchScalarGridSpec(
        num_scalar_prefetch=4,
        grid=(m // step_rows,),
        in_specs=[
            pl.BlockSpec((ROW_TILES, 8, LANES), lambda i, *_: (i, 0, 0), memory_space=pltpu.SMEM),
            pl.BlockSpec((ROW_TILES, ROUTE_ROWS, TM_ROW), lambda i, *_: (i, 0, 0)),
            pl.BlockSpec((step_rows, D_MODEL), lambda i, *_: (i, 0)),
        ],
        out_specs=pl.BlockSpec(memory_space=pl.ANY),
        scratch_shapes=[pltpu.VMEM((MOE_BLK * SUB, LANES), jnp.int32),
                        pltpu.VMEM((2 * ROW_TILES, TM_ROW * SUB, LANES), jnp.int32),
                        pltpu.SemaphoreType.DMA((2 * ROW_TILES,))],
    )
    return pl.pallas_call(
        _scatter_kernel,
        grid_spec=grid_spec,
        out_shape=jax.ShapeDtypeStruct((N_SORTED * SUB, LANES), jnp.int32),
        compiler_params=pltpu.CompilerParams(
            dimension_semantics=("arbitrary",), vmem_limit_bytes=VMEM_LIMIT),
        name="scatter",
    )(n_used, grp_end, grp_blocks, grp_base, seg, route, xn)


def _ffn_kernel(bg_ref, nu_ref, xs_ref, w1_ref, w3_ref, w2_ref, y_ref):
    del bg_ref
    b = pl.program_id(0)

    def block_inputs():
        packed = [xs_ref[_sublane(s, MOE_BLK), :] for s in range(SUB_X)]
        x = jnp.concatenate(
            [lax.bitcast_convert_type(u & HI_MASK, F32).astype(BF16) for u in packed]
            + [lax.bitcast_convert_type(lax.shift_left(u, 16), F32).astype(BF16) for u in packed],
            axis=1)
        return x, lax.bitcast_convert_type(xs_ref[_sublane(SUB_GATE, MOE_BLK), :], F32)

    def store(y):
        for s in range(SUB):
            y_ref[_sublane(s, MOE_BLK), :] = y[:, s * LANES:(s + 1) * LANES]

    @pl.when(b < nu_ref[0])
    def _():
        x, gates = block_inputs()
        n_e = EXPERTS_PER_GROUP
        h1 = [_dot(x, w1_ref[e]) for e in range(n_e)]
        h3 = [_dot(x, w3_ref[e]) for e in range(n_e)]
        hs = []
        for e in range(n_e):
            ge = gates[:, e:e + 1]
            hs.append(jnp.where(ge > 0.0, h1[e] * _sigmoid(h1[e]) * h3[e] * ge, 0.0).astype(BF16))
        hcat = jnp.concatenate(hs, axis=1)
        store(_dot(hcat, w2_ref[...].reshape(n_e * D_EXPERT, D_MODEL)))

    @pl.when(b >= nu_ref[0])
    def _():
        y_ref[...] = jnp.zeros_like(y_ref)


def _ffn(blk_group, n_used, xs, w1, w3, w2):
    grouped = (MOE_GROUPS, EXPERTS_PER_GROUP)
    w1, w3, w2 = (w.reshape(grouped + w.shape[1:]) for w in (w1, w3, w2))
    wmap = lambda b, bg, nu: (bg[b], 0, 0, 0)
    grid_spec = pltpu.PrefetchScalarGridSpec(
        num_scalar_prefetch=2,
        grid=(N_BLK,),
        in_specs=[
            pl.BlockSpec((MOE_BLK * SUB, LANES), lambda b, bg, nu: (b, 0)),
            pl.BlockSpec((None, EXPERTS_PER_GROUP, D_MODEL, D_EXPERT), wmap),
            pl.BlockSpec((None, EXPERTS_PER_GROUP, D_MODEL, D_EXPERT), wmap),
            pl.BlockSpec((None, EXPERTS_PER_GROUP, D_EXPERT, D_MODEL), wmap),
        ],
        out_specs=pl.BlockSpec((MOE_BLK * SUB, LANES), lambda b, bg, nu: (b, 0)),
    )
    return pl.pallas_call(
        _ffn_kernel,
        grid_spec=grid_spec,
        out_shape=jax.ShapeDtypeStruct((N_SORTED * SUB, LANES), F32),
        compiler_params=pltpu.CompilerParams(
            dimension_semantics=("arbitrary",), vmem_limit_bytes=VMEM_LIMIT),
        name="ffn",
    )(blk_group, n_used, xs, w1, w3, w2)


def _combine_kernel(base_ref, *refs, final):
    n_buf = GATHER_AHEAD + 1
    seg_refs, (route_ref, ys_hbm, x_ref, g_ref, wa_ref) = refs[:n_buf], refs[n_buf:n_buf + 5]
    outs, (buf, sem, stage) = refs[n_buf + 5:-3], refs[-3:]
    i = pl.program_id(0)
    n = pl.num_programs(0)
    step_rows = ROW_TILES * TM_ROW

    def gather(ahead, t, enable=None):
        s = ((i + ahead) % n_buf) * ROW_TILES + t

        def make_copy(buf_row, ys_row, k):
            return pltpu.make_async_copy(ys_hbm.at[_rows(ys_row, k)],
                                         buf.at[s, _rows(buf_row, k)], sem.at[s])
        return _segment_copies(seg_refs[ahead].at[t], base_ref, make_copy, enable)

    @pl.when(i == 0)
    def _():
        for ahead in range(GATHER_AHEAD):
            for t in range(ROW_TILES):
                for _ in gather(ahead, t, ahead < n):
                    pass

    for t in range(ROW_TILES):
        slot = (i % n_buf) * ROW_TILES + t
        pltpu.make_async_copy(ys_hbm.at[_rows(0, TM_ROW)], buf.at[slot], sem.at[slot]).wait()

    def tile_stages(t):
        slot = (i % n_buf) * ROW_TILES + t
        rows = slice(t * TM_ROW, (t + 1) * TM_ROW)
        tn = (((0,), (0,)), ((), ()))
        perm = _perm_matrix(route_ref.at[t])
        cols = []
        for s in range(SUB):
            hi, lo = _split_bf16(buf[slot, _sublane(s, TM_ROW), :])
            cols.append(lax.dot_general(perm, hi, tn, preferred_element_type=F32)
                        + lax.dot_general(perm, lo, tn, preferred_element_type=F32))
            if s % 4 == 3:
                yield
        y = x_ref[rows, :] + jnp.concatenate(cols, axis=1)
        if not final:
            outs[0][rows, :] = y
            yield
            _norm_and_lowrank(y, g_ref, wa_ref, outs[1].at[rows, :], outs[2].at[rows, :])
        else:
            yield
            stage[rows, :] = _rms(y, g_ref[...])

    _interleave(*[_delayed(tile_stages(t), t) for t in range(ROW_TILES)],
                *[_delayed(gather(GATHER_AHEAD, t, i + GATHER_AHEAD < n), t)
                  for t in range(ROW_TILES)])

    if final:
        @pl.when(i < T_PROMPT // step_rows)
        def _():
            outs[0][...] = stage[...]

        @pl.when(i >= T_PROMPT // step_rows)
        def _():
            outs[1][...] = stage[...]


def _combine(seg, route, grp_base, ys, x, g, wa, final):
    m = x.shape[0]
    step_rows = ROW_TILES * TM_ROW
    nt = m // step_rows
    n_p = T_PROMPT // step_rows
    smem = functools.partial(pl.BlockSpec, (ROW_TILES, 8, LANES), memory_space=pltpu.SMEM)
    tile = (step_rows, D_MODEL)
    if final:
        out_specs = [pl.BlockSpec(tile, lambda i, *_: (jnp.minimum(i, n_p - 1), 0)),
                     pl.BlockSpec(tile, lambda i, *_: (jnp.maximum(i - n_p, 0), 0))]
        out_shape = [jax.ShapeDtypeStruct((T_PROMPT, D_MODEL), F32),
                     jax.ShapeDtypeStruct((T_SAMPLE, D_MODEL), F32)]
    else:
        out_specs = [pl.BlockSpec(tile, lambda i, *_: (i, 0)),
                     pl.BlockSpec(tile, lambda i, *_: (i, 0)),
                     pl.BlockSpec((step_rows, LANES), lambda i, *_: (i, 0))]
        out_shape = [jax.ShapeDtypeStruct((m, D_MODEL), F32),
                     jax.ShapeDtypeStruct((m, D_MODEL), BF16),
                     jax.ShapeDtypeStruct((m, LANES), BF16)]
    grid_spec = pltpu.PrefetchScalarGridSpec(
        num_scalar_prefetch=1,
        grid=(nt,),
        in_specs=[smem(lambda i, *_, ahead=ahead: (jnp.minimum(i + ahead, nt - 1), 0, 0))
                  for ahead in range(GATHER_AHEAD + 1)] + [
            pl.BlockSpec((ROW_TILES, ROUTE_ROWS, TM_ROW), lambda i, *_: (i, 0, 0)),
            pl.BlockSpec(memory_space=pl.ANY),
            pl.BlockSpec(tile, lambda i, *_: (i, 0)),
            pl.BlockSpec(g.shape, lambda i, *_: (0, 0)),
            pl.BlockSpec(wa.shape, lambda i, *_: (0, 0)),
        ],
        out_specs=out_specs,
        scratch_shapes=[pltpu.VMEM(((GATHER_AHEAD + 1) * ROW_TILES, TM_ROW * SUB, LANES), F32),
                        pltpu.SemaphoreType.DMA(((GATHER_AHEAD + 1) * ROW_TILES,)),
                        pltpu.VMEM(tile if final else (8, LANES), F32)],
    )
    return pl.pallas_call(
        functools.partial(_combine_kernel, final=final),
        grid_spec=grid_spec,
        out_shape=out_shape,
        compiler_params=pltpu.CompilerParams(
            dimension_semantics=("arbitrary",), vmem_limit_bytes=VMEM_LIMIT),
        name="combine",
    )(grp_base, *[seg] * (GATHER_AHEAD + 1), route, ys, x, g, wa)


def _prep_weights(w_in, gla_a2, cm_ws, cm_b, router_group_w, router_group_b,
                  router_expert_w, router_expert_b):
    w_z = jnp.swapaxes(w_in, 1, 2).astype(BF16)
    a0 = _w_in_offsets()["alr"][0]
    w_alr = jnp.pad(w_z[:, a0:a0 + GLA_LOWRANK], ((0, 0), (0, LANES - GLA_LOWRANK), (0, 0)))
    a2 = jnp.pad(gla_a2, ((0, 0), (0, LANES - GLA_LOWRANK), (0, 0))).astype(BF16)
    ws_p = jnp.tril(cm_ws).astype(BF16)
    small = jnp.tril(cm_ws[:, :, :DEC_SEQ, :DEC_SEQ])
    eye = jnp.eye(SEQ_PER_BLK, dtype=F32)
    ws_s = jnp.einsum("ij,lgab->lgiajb", eye, small).reshape(
        DEPTH, CM_GROUPS, ROWS_S, ROWS_S).astype(BF16)
    cmb_p = jnp.broadcast_to(jnp.transpose(cm_b, (0, 2, 1))[:, :, :, None],
                             (DEPTH, CM_CHUNK, CM_GROUPS, CM_GCH)).reshape(DEPTH, CM_CHUNK, CM_CH)
    cmb_s = jnp.tile(cmb_p[:, :DEC_SEQ], (1, SEQ_PER_BLK, 1))
    pad = LANES - MOE_GROUPS - N_EXPERTS
    w_r = jnp.pad(jnp.swapaxes(jnp.concatenate([router_group_w, router_expert_w], axis=-1), 1, 2),
                  ((0, 0), (0, pad), (0, 0)))
    r_b = jnp.pad(jnp.concatenate([router_group_b, router_expert_b], axis=-1),
                  ((0, 0), (0, pad)))[:, :, None]
    return w_z, w_alr, a2, ws_p, ws_s, cmb_p, cmb_s, w_r.astype(BF16), r_b


def kernel(x_prompt, x_sample, state_conv, state_gla, norm1_g, w_in, conv_w, gla_a2, gla_a_b,
           gla_norm_g, cm_norm_g, cm_ws, cm_b, proj_a, proj_b, proj_c, w_out, norm2_g,
           router_group_w, router_group_b, router_expert_w, router_expert_b,
           exp_w1, exp_w3, exp_w2, final_norm_g):
    (w_z, w_alr, a2, ws_p, ws_s, cmb_p, cmb_s, w_r, r_b) = _prep_weights(
        w_in, gla_a2, cm_ws, cm_b, router_group_w, router_group_b, router_expert_w,
        router_expert_b)
    pa, pb, pc, wo = (w.astype(BF16) for w in (proj_a, proj_b, proj_c, w_out))
    n1 = norm1_g[:, None, :]
    n2 = norm2_g[:, None, :]
    ab = gla_a_b[:, None, :]
    gng = gla_norm_g[:, None, :]
    cmg = cm_norm_g.reshape(DEPTH, 1, CM_CH)
    fg = final_norm_g[None, :]
    xa = x_prompt.reshape(T_PROMPT, D_MODEL)
    xb = x_sample.reshape(T_SAMPLE, D_MODEL)
    gla_s = None
    conv_p, gla_p, conv_s, cmv_s = [], [], [], []
    xin, alr = _prenorm(xa, xb, n1, w_alr, 0)
    for l in range(DEPTH):
        z, *experts = _inproj(xin, w_z, (exp_w1, exp_w3, exp_w2), l)
        acts_p, nconv, ngla = _mix_prompt(z, alr, a2, ab, conv_w, gng, cmg, ws_p, cmb_p, l)
        sc = state_conv[l]
        p2 = jnp.pad(sc, ((0, 0), (0, DEC_SEQ - 2), (0, 0))).reshape(T_SAMPLE, CONV_CH)
        p1 = jnp.pad(sc[:, 1:2], ((0, 0), (0, DEC_SEQ - 1), (0, 0))).reshape(T_SAMPLE, CONV_CH)
        acts_s, cin_s, gla_s, vrows = _mix_sample(z, alr, a2, ab, conv_w, gng, cmg, ws_s, cmb_s,
                                                  p1, p2, state_gla, gla_s, l)
        conv_p.append(nconv)
        gla_p.append(ngla)
        conv_s.append(cin_s.reshape(DEC_BATCH, DEC_SEQ, CONV_CH)[:, DEC_SEQ - (CONV_K - 1):])
        cmv_s.append(vrows.reshape(DEC_BATCH, DEC_SEQ, CM_CH))

        x, xn, route, seg, cnt = _outproj(acts_p, acts_s, z, xa, xb, pa, pb, pc, wo, n2, w_r, r_b,
                                          l)
        plan = _plan(cnt)
        n_used = plan[PLAN_USED, :1]
        grp_base = plan[PLAN_BASE, :MOE_GROUPS]
        xs = _scatter(seg, route, xn, n_used, plan[PLAN_END, :MOE_GROUPS],
                      plan[PLAN_BLOCKS, :MOE_GROUPS], grp_base)
        ys = _ffn(plan[PLAN_GROUP, :N_BLK], n_used, xs, *experts)
        if l == DEPTH - 1:
            out = _combine(seg, route, grp_base, ys, x, fg, w_alr[l], True)
        else:
            xa, xin, alr = _combine(seg, route, grp_base, ys, x, n1[l + 1], w_alr[l + 1], False)
            xb = xa

    y_prompt = out[0].reshape(BATCH, SEQ, D_MODEL)
    y_sample = out[1].reshape(DEC_BATCH, DEC_SEQ, D_MODEL)
    return (y_prompt, y_sample, jnp.stack(conv_p), jnp.stack(gla_p), jnp.stack(conv_s),
            gla_s, jnp.stack(cmv_s))
```
